```python
import jax, jax.numpy as jnp
from jax import lax
import numpy as np

D_MODEL = 1024
BATCH = 16
SEQ = 2048
DEPTH = 1

N_META = 16
N_HEADS = 16
HEAD_DIM = 64
D_ATTN = N_HEADS * HEAD_DIM
D_CONV = D_MODEL
CONV_WIDTH = 31
D_FF = 4 * D_MODEL
Q_BLOCK = 128
RMS_EPS = 1e-6
LN_EPS = 1e-5

PART_SIZES = [D_ATTN, D_ATTN, D_ATTN, N_HEADS, 2 * D_CONV, D_MODEL, D_MODEL]
PART_OFFSETS = [int(o) for o in np.cumsum([0] + PART_SIZES)]
N_IN = PART_OFFSETS[-1]

kernel_name = "fox_conformer_parallel_gated_hybrid"


def rms_norm(x, g):
    xf = x.astype(jnp.float32)
    y = xf * lax.rsqrt(jnp.mean(xf * xf, axis=-1, keepdims=True) + RMS_EPS)
    return (y * g.astype(jnp.float32)).astype(x.dtype)


def layer_norm(x, g, b):
    xf = x.astype(jnp.float32)
    mu = jnp.mean(xf, axis=-1, keepdims=True)
    xc = xf - mu
    var = jnp.mean(xc * xc, axis=-1, keepdims=True)
    y = xc * lax.rsqrt(var + LN_EPS) * g.astype(jnp.float32) + b.astype(jnp.float32)
    return y.astype(x.dtype)


def forgetting_attention(q, k, v, fg_logit):
    T = q.shape[1]
    scale = HEAD_DIM ** -0.5
    log_f = jax.nn.log_sigmoid(fg_logit.astype(jnp.float32))
    cum = jnp.transpose(jnp.cumsum(log_f, axis=1), (0, 2, 1))
    starts = [0] + list(range(N_META, T, Q_BLOCK))
    ends = starts[1:] + [T]
    outs = []
    for q0, q1 in zip(starts, ends):
        s = jnp.einsum('bqhd,bkhd->bhqk', q[:, q0:q1], k[:, :q1]).astype(jnp.float32) * scale
        s = s + cum[:, :, q0:q1, None] - cum[:, :, None, :q1]
        causal = jnp.arange(q0, q1)[:, None] >= jnp.arange(q1)[None, :]
        s = jnp.where(causal, s, -jnp.inf)
        p = jax.nn.softmax(s, axis=-1).astype(v.dtype)
        outs.append(jnp.einsum('bhqk,bkhd->bqhd', p, v[:, :q1]))
    return jnp.concatenate(outs, axis=1)


def causal_depthwise_conv(u, w, b):
    C = u.shape[-1]
    y = lax.conv_general_dilated(
        u, w[:, None, :].astype(u.dtype), window_strides=(1,),
        padding=((CONV_WIDTH - 1, 0),), dimension_numbers=('NWC', 'WIO', 'NWC'),
        feature_group_count=C)
    return y + b.astype(u.dtype)


def _fwd_setup_inputs(seed: int = 0) -> dict:
    key = jax.random.key(seed)
    ks = jax.random.split(key, 20)
    nrm = lambda k, shape, fan_in: jax.random.normal(k, shape, jnp.float32) * (fan_in ** -0.5)
    gain = lambda k, shape: 1.0 + 0.02 * jax.random.normal(k, shape, jnp.float32)
    small = lambda k, shape: 0.02 * jax.random.normal(k, shape, jnp.float32)
    return {
        "x": jax.random.normal(ks[0], (BATCH, SEQ, D_MODEL), jnp.float32),
        "meta_tokens": jax.random.normal(ks[1], (N_META, D_MODEL), jnp.float32),
        "norm_mix_gain": gain(ks[2], (DEPTH, D_MODEL)),
        "w_in": nrm(ks[3], (DEPTH, D_MODEL, N_IN), D_MODEL),
        "b_forget": jax.random.uniform(ks[4], (DEPTH, N_HEADS), jnp.float32, 1.0, 6.0),
        "w_attn_out": nrm(ks[5], (DEPTH, D_ATTN, D_MODEL), D_ATTN),
        "b_glu": small(ks[6], (DEPTH, 2 * D_CONV)),
        "conv_dw_w": nrm(ks[7], (DEPTH, CONV_WIDTH, D_CONV), CONV_WIDTH),
        "conv_dw_b": small(ks[8], (DEPTH, D_CONV)),
        "conv_ln_gain": gain(ks[9], (DEPTH, D_CONV)),
        "conv_ln_bias": small(ks[10], (DEPTH, D_CONV)),
        "w_conv_out": nrm(ks[11], (DEPTH, D_CONV, D_MODEL), D_CONV),
        "b_conv_out": small(ks[12], (DEPTH, D_MODEL)),
        "w_out": nrm(ks[13], (DEPTH, D_MODEL, D_MODEL), D_MODEL),
        "norm_mlp_gain": gain(ks[14], (DEPTH, D_MODEL)),
        "w_mlp_up": nrm(ks[15], (DEPTH, D_MODEL, D_FF), D_MODEL),
        "w_mlp_down": nrm(ks[16], (DEPTH, D_FF, D_MODEL), D_FF),
        "final_norm_gain": gain(ks[17], (D_MODEL,)),
    }


def _fwd_reference(x, meta_tokens, norm_mix_gain, w_in, b_forget, w_attn_out, b_glu,
              conv_dw_w, conv_dw_b, conv_ln_gain, conv_ln_bias, w_conv_out, b_conv_out,
              w_out, norm_mlp_gain, w_mlp_up, w_mlp_down, final_norm_gain):
    B = x.shape[0]
    meta = jnp.broadcast_to(meta_tokens[None].astype(x.dtype), (B, N_META, D_MODEL))
    h_res = jnp.concatenate([meta, x], axis=1)
    T = h_res.shape[1]
    o = PART_OFFSETS
    for l in range(DEPTH):
        hn = rms_norm(h_res, norm_mix_gain[l])
        w_l = w_in[l]
        part = lambda i: hn @ w_l[:, o[i]:o[i + 1]]
        q = part(0).reshape(B, T, N_HEADS, HEAD_DIM)
        k = part(1).reshape(B, T, N_HEADS, HEAD_DIM)
        v = part(2).reshape(B, T, N_HEADS, HEAD_DIM)
        fg = part(3) + b_forget[l]
        glu_in = part(4) + b_glu[l]
        gate_attn = jax.nn.sigmoid(part(5))
        gate_conv = jax.nn.sigmoid(part(6))

        a = forgetting_attention(q, k, v, fg).reshape(B, T, D_ATTN) @ w_attn_out[l]

        u = glu_in[..., :D_CONV] * jax.nn.sigmoid(glu_in[..., D_CONV:])
        c = causal_depthwise_conv(u, conv_dw_w[l], conv_dw_b[l])
        c = jax.nn.silu(layer_norm(c, conv_ln_gain[l], conv_ln_bias[l]))
        c = c @ w_conv_out[l] + b_conv_out[l]

        h_res = h_res + (gate_attn * a + gate_conv * c) @ w_out[l]

        hn = rms_norm(h_res, norm_mlp_gain[l])
        h_res = h_res + jnp.square(jax.nn.relu(hn @ w_mlp_up[l])) @ w_mlp_down[l]
    y = rms_norm(h_res, final_norm_gain)
    return y[:, N_META:]


import jax as _jax
import jax.numpy as _jnp

TWIN_FORMAT = 'train_step'
FWD_PARAMS = ['x', 'meta_tokens', 'norm_mix_gain', 'w_in', 'b_forget', 'w_attn_out', 'b_glu', 'conv_dw_w', 'conv_dw_b', 'conv_ln_gain', 'conv_ln_bias', 'w_conv_out', 'b_conv_out', 'w_out', 'norm_mlp_gain', 'w_mlp_up', 'w_mlp_down', 'final_norm_gain']
TWIN_WEIGHTS = ['meta_tokens', 'norm_mix_gain', 'w_in', 'b_forget', 'w_attn_out', 'b_glu', 'conv_dw_w', 'conv_dw_b', 'conv_ln_gain', 'conv_ln_bias', 'w_conv_out', 'b_conv_out', 'w_out', 'norm_mlp_gain', 'w_mlp_up', 'w_mlp_down', 'final_norm_gain']
TWIN_DIFF_INPUT = 'x'
TWIN_INPUTS = ['x', 'meta_tokens', 'norm_mix_gain', 'w_in', 'b_forget', 'w_attn_out', 'b_glu', 'conv_dw_w', 'conv_dw_b', 'conv_ln_gain', 'conv_ln_bias', 'w_conv_out', 'b_conv_out', 'w_out', 'norm_mlp_gain', 'w_mlp_up', 'w_mlp_down', 'final_norm_gain', 'loss_target', 'm_meta_tokens', 'm_norm_mix_gain', 'm_w_in', 'm_b_forget', 'm_w_attn_out', 'm_b_glu', 'm_conv_dw_w', 'm_conv_dw_b', 'm_conv_ln_gain', 'm_conv_ln_bias', 'm_w_conv_out', 'm_b_conv_out', 'm_w_out', 'm_norm_mlp_gain', 'm_w_mlp_up', 'm_w_mlp_down', 'm_final_norm_gain', 'v_meta_tokens', 'v_norm_mix_gain', 'v_w_in', 'v_b_forget', 'v_w_attn_out', 'v_b_glu', 'v_conv_dw_w', 'v_conv_dw_b', 'v_conv_ln_gain', 'v_conv_ln_bias', 'v_w_conv_out', 'v_b_conv_out', 'v_w_out', 'v_norm_mlp_gain', 'v_w_mlp_up', 'v_w_mlp_down', 'v_final_norm_gain']
TWIN_OUTPUTS = ['loss', 'grad_x', 'grad_meta_tokens', 'grad_norm_mix_gain', 'grad_w_in', 'grad_b_forget', 'grad_w_attn_out', 'grad_b_glu', 'grad_conv_dw_w', 'grad_conv_dw_b', 'grad_conv_ln_gain', 'grad_conv_ln_bias', 'grad_w_conv_out', 'grad_b_conv_out', 'grad_w_out', 'grad_norm_mlp_gain', 'grad_w_mlp_up', 'grad_w_mlp_down', 'grad_final_norm_gain', 'delta_meta_tokens', 'delta_norm_mix_gain', 'delta_w_in', 'delta_b_forget', 'delta_w_attn_out', 'delta_b_glu', 'delta_conv_dw_w', 'delta_conv_dw_b', 'delta_conv_ln_gain', 'delta_conv_ln_bias', 'delta_w_conv_out', 'delta_b_conv_out', 'delta_w_out', 'delta_norm_mlp_gain', 'delta_w_mlp_up', 'delta_w_mlp_down', 'delta_final_norm_gain', 'new_m_meta_tokens', 'new_m_norm_mix_gain', 'new_m_w_in', 'new_m_b_forget', 'new_m_w_attn_out', 'new_m_b_glu', 'new_m_conv_dw_w', 'new_m_conv_dw_b', 'new_m_conv_ln_gain', 'new_m_conv_ln_bias', 'new_m_w_conv_out', 'new_m_b_conv_out', 'new_m_w_out', 'new_m_norm_mlp_gain', 'new_m_w_mlp_up', 'new_m_w_mlp_down', 'new_m_final_norm_gain', 'new_v_meta_tokens', 'new_v_norm_mix_gain', 'new_v_w_in', 'new_v_b_forget', 'new_v_w_attn_out', 'new_v_b_glu', 'new_v_conv_dw_w', 'new_v_conv_dw_b', 'new_v_conv_ln_gain', 'new_v_conv_ln_bias', 'new_v_w_conv_out', 'new_v_b_conv_out', 'new_v_w_out', 'new_v_norm_mlp_gain', 'new_v_w_mlp_up', 'new_v_w_mlp_down', 'new_v_final_norm_gain']
TWIN_LEAF_KINDS = {'loss': 'loss', 'grad_x': 'grad_x', 'grad_meta_tokens': 'grad_w', 'grad_norm_mix_gain': 'grad_w', 'grad_w_in': 'grad_w', 'grad_b_forget': 'grad_w', 'grad_w_attn_out': 'grad_w', 'grad_b_glu': 'grad_w', 'grad_conv_dw_w': 'grad_w', 'grad_conv_dw_b': 'grad_w', 'grad_conv_ln_gain': 'grad_w', 'grad_conv_ln_bias': 'grad_w', 'grad_w_conv_out': 'grad_w', 'grad_b_conv_out': 'grad_w', 'grad_w_out': 'grad_w', 'grad_norm_mlp_gain': 'grad_w', 'grad_w_mlp_up': 'grad_w', 'grad_w_mlp_down': 'grad_w', 'grad_final_norm_gain': 'grad_w', 'delta_meta_tokens': 'delta_w', 'delta_norm_mix_gain': 'delta_w', 'delta_w_in': 'delta_w', 'delta_b_forget': 'delta_w', 'delta_w_attn_out': 'delta_w', 'delta_b_glu': 'delta_w', 'delta_conv_dw_w': 'delta_w', 'delta_conv_dw_b': 'delta_w', 'delta_conv_ln_gain': 'delta_w', 'delta_conv_ln_bias': 'delta_w', 'delta_w_conv_out': 'delta_w', 'delta_b_conv_out': 'delta_w', 'delta_w_out': 'delta_w', 'delta_norm_mlp_gain': 'delta_w', 'delta_w_mlp_up': 'delta_w', 'delta_w_mlp_down': 'delta_w', 'delta_final_norm_gain': 'delta_w', 'new_m_meta_tokens': 'new_m', 'new_m_norm_mix_gain': 'new_m', 'new_m_w_in': 'new_m', 'new_m_b_forget': 'new_m', 'new_m_w_attn_out': 'new_m', 'new_m_b_glu': 'new_m', 'new_m_conv_dw_w': 'new_m', 'new_m_conv_dw_b': 'new_m', 'new_m_conv_ln_gain': 'new_m', 'new_m_conv_ln_bias': 'new_m', 'new_m_w_conv_out': 'new_m', 'new_m_b_conv_out': 'new_m', 'new_m_w_out': 'new_m', 'new_m_norm_mlp_gain': 'new_m', 'new_m_w_mlp_up': 'new_m', 'new_m_w_mlp_down': 'new_m', 'new_m_final_norm_gain': 'new_m', 'new_v_meta_tokens': 'new_v', 'new_v_norm_mix_gain': 'new_v', 'new_v_w_in': 'new_v', 'new_v_b_forget': 'new_v', 'new_v_w_attn_out': 'new_v', 'new_v_b_glu': 'new_v', 'new_v_conv_dw_w': 'new_v', 'new_v_conv_dw_b': 'new_v', 'new_v_conv_ln_gain': 'new_v', 'new_v_conv_ln_bias': 'new_v', 'new_v_w_conv_out': 'new_v', 'new_v_b_conv_out': 'new_v', 'new_v_w_out': 'new_v', 'new_v_norm_mlp_gain': 'new_v', 'new_v_w_mlp_up': 'new_v', 'new_v_w_mlp_down': 'new_v', 'new_v_final_norm_gain': 'new_v'}


def _forward(args):
    return _fwd_reference(*[args[k] for k in FWD_PARAMS])


def _output_shape():
    out = _jax.eval_shape(lambda: _forward(_fwd_setup_inputs(0)))
    return out.shape, out.dtype

N_MICROBATCH = 1
ADAM_LR = 0.001
ADAM_B1 = 0.9
ADAM_B2 = 0.999
ADAM_EPS = 1e-08
ADAM_WD = 0.01
ADAM_STEP = 10
PER_EXAMPLE_BATCH_AXIS = {'x': 0, 'loss_target': 0}
SHARED_INPUTS = []
_WEIGHT_DTYPES = {'meta_tokens': _jnp.float32, 'norm_mix_gain': _jnp.float32, 'w_in': _jnp.float32, 'b_forget': _jnp.float32, 'w_attn_out': _jnp.float32, 'b_glu': _jnp.float32, 'conv_dw_w': _jnp.float32, 'conv_dw_b': _jnp.float32, 'conv_ln_gain': _jnp.float32, 'conv_ln_bias': _jnp.float32, 'w_conv_out': _jnp.float32, 'b_conv_out': _jnp.float32, 'w_out': _jnp.float32, 'norm_mlp_gain': _jnp.float32, 'w_mlp_up': _jnp.float32, 'w_mlp_down': _jnp.float32, 'final_norm_gain': _jnp.float32}
MOMENT_SCALE = {'meta_tokens': 4.296079e-03, 'norm_mix_gain': 8.789614e-02, 'w_in': 3.242644e-02, 'b_forget': 1.188949e-01, 'w_attn_out': 3.284041e-02, 'b_glu': 5.450885e-02, 'conv_dw_w': 6.128682e-02, 'conv_dw_b': 1.236642e-01, 'conv_ln_gain': 7.326721e-02, 'conv_ln_bias': 6.750073e-02, 'w_conv_out': 6.169538e-02, 'b_conv_out': 1.327531e-01, 'w_out': 6.922199e-02, 'norm_mlp_gain': 1.573013e-01, 'w_mlp_up': 7.737941e-02, 'w_mlp_down': 1.498978e-01, 'final_norm_gain': 3.228600e+01}


def _to_microbatches(a, axis):
    t = _jnp.moveaxis(a, axis, 0)
    t = t.reshape((N_MICROBATCH, t.shape[0] // N_MICROBATCH) + t.shape[1:])
    return _jnp.moveaxis(t, 1, axis + 1)


def setup_inputs(seed: int = 0) -> dict:
    inp = _fwd_setup_inputs(seed)
    key = _jax.random.fold_in(_jax.random.key(seed), 7919)
    shape, _ = _output_shape()
    out = dict(inp)
    out["loss_target"] = _jax.random.normal(_jax.random.fold_in(key, 0), shape, _jnp.float32)
    for i, name in enumerate(TWIN_WEIGHTS):
        w = inp[name].astype(_jnp.float32)
        if MOMENT_SCALE is None:
            s = _jnp.sqrt(_jnp.mean(_jnp.square(w)) + 1e-30)
        else:
            s = MOMENT_SCALE[name]
        km, kv = _jax.random.split(_jax.random.fold_in(key, i + 1))
        out[name] = w
        out["m_" + name] = s * _jax.random.normal(km, w.shape, _jnp.float32)
        out["v_" + name] = (s * s) * _jax.random.uniform(kv, w.shape, _jnp.float32, 0.5, 1.5)
    if N_MICROBATCH > 1:
        for name, axis in PER_EXAMPLE_BATCH_AXIS.items():
            out[name] = _to_microbatches(out[name], axis)
    return {'x': out['x'], 'meta_tokens': out['meta_tokens'], 'norm_mix_gain': out['norm_mix_gain'], 'w_in': out['w_in'], 'b_forget': out['b_forget'], 'w_attn_out': out['w_attn_out'], 'b_glu': out['b_glu'], 'conv_dw_w': out['conv_dw_w'], 'conv_dw_b': out['conv_dw_b'], 'conv_ln_gain': out['conv_ln_gain'], 'conv_ln_bias': out['conv_ln_bias'], 'w_conv_out': out['w_conv_out'], 'b_conv_out': out['b_conv_out'], 'w_out': out['w_out'], 'norm_mlp_gain': out['norm_mlp_gain'], 'w_mlp_up': out['w_mlp_up'], 'w_mlp_down': out['w_mlp_down'], 'final_norm_gain': out['final_norm_gain'], 'loss_target': out['loss_target'], 'm_meta_tokens': out['m_meta_tokens'], 'm_norm_mix_gain': out['m_norm_mix_gain'], 'm_w_in': out['m_w_in'], 'm_b_forget': out['m_b_forget'], 'm_w_attn_out': out['m_w_attn_out'], 'm_b_glu': out['m_b_glu'], 'm_conv_dw_w': out['m_conv_dw_w'], 'm_conv_dw_b': out['m_conv_dw_b'], 'm_conv_ln_gain': out['m_conv_ln_gain'], 'm_conv_ln_bias': out['m_conv_ln_bias'], 'm_w_conv_out': out['m_w_conv_out'], 'm_b_conv_out': out['m_b_conv_out'], 'm_w_out': out['m_w_out'], 'm_norm_mlp_gain': out['m_norm_mlp_gain'], 'm_w_mlp_up': out['m_w_mlp_up'], 'm_w_mlp_down': out['m_w_mlp_down'], 'm_final_norm_gain': out['m_final_norm_gain'], 'v_meta_tokens': out['v_meta_tokens'], 'v_norm_mix_gain': out['v_norm_mix_gain'], 'v_w_in': out['v_w_in'], 'v_b_forget': out['v_b_forget'], 'v_w_attn_out': out['v_w_attn_out'], 'v_b_glu': out['v_b_glu'], 'v_conv_dw_w': out['v_conv_dw_w'], 'v_conv_dw_b': out['v_conv_dw_b'], 'v_conv_ln_gain': out['v_conv_ln_gain'], 'v_conv_ln_bias': out['v_conv_ln_bias'], 'v_w_conv_out': out['v_w_conv_out'], 'v_b_conv_out': out['v_b_conv_out'], 'v_w_out': out['v_w_out'], 'v_norm_mlp_gain': out['v_norm_mlp_gain'], 'v_w_mlp_up': out['v_w_mlp_up'], 'v_w_mlp_down': out['v_w_mlp_down'], 'v_final_norm_gain': out['v_final_norm_gain']}


def _loss(weights, diff, rest, loss_target):
    with _jax.named_scope("forward"):
        args = {**rest, TWIN_DIFF_INPUT: diff, **{k: w.astype(_WEIGHT_DTYPES[k]) for k, w in weights.items()}}
        y = _forward(args)
    with _jax.named_scope("loss_head"):
        err = _jnp.square(y.astype(_jnp.float32) - loss_target)
        return 0.5 * _jnp.sum(_jnp.mean(err, axis=-1)) if err.ndim else 0.5 * err


def _adamw(w, g, m, v):
    m = ADAM_B1 * m + (1.0 - ADAM_B1) * g
    v = ADAM_B2 * v + (1.0 - ADAM_B2) * _jnp.square(g)
    m_hat = m / (1.0 - ADAM_B1 ** ADAM_STEP)
    v_hat = v / (1.0 - ADAM_B2 ** ADAM_STEP)
    delta = -ADAM_LR * (m_hat / (_jnp.sqrt(v_hat) + ADAM_EPS) + ADAM_WD * w)
    return delta, m, v


def reference(x, meta_tokens, norm_mix_gain, w_in, b_forget, w_attn_out, b_glu, conv_dw_w, conv_dw_b, conv_ln_gain, conv_ln_bias, w_conv_out, b_conv_out, w_out, norm_mlp_gain, w_mlp_up, w_mlp_down, final_norm_gain, loss_target, m_meta_tokens, m_norm_mix_gain, m_w_in, m_b_forget, m_w_attn_out, m_b_glu, m_conv_dw_w, m_conv_dw_b, m_conv_ln_gain, m_conv_ln_bias, m_w_conv_out, m_b_conv_out, m_w_out, m_norm_mlp_gain, m_w_mlp_up, m_w_mlp_down, m_final_norm_gain, v_meta_tokens, v_norm_mix_gain, v_w_in, v_b_forget, v_w_attn_out, v_b_glu, v_conv_dw_w, v_conv_dw_b, v_conv_ln_gain, v_conv_ln_bias, v_w_conv_out, v_b_conv_out, v_w_out, v_norm_mlp_gain, v_w_mlp_up, v_w_mlp_down, v_final_norm_gain):
    given = dict(x=x, meta_tokens=meta_tokens, norm_mix_gain=norm_mix_gain, w_in=w_in, b_forget=b_forget, w_attn_out=w_attn_out, b_glu=b_glu, conv_dw_w=conv_dw_w, conv_dw_b=conv_dw_b, conv_ln_gain=conv_ln_gain, conv_ln_bias=conv_ln_bias, w_conv_out=w_conv_out, b_conv_out=b_conv_out, w_out=w_out, norm_mlp_gain=norm_mlp_gain, w_mlp_up=w_mlp_up, w_mlp_down=w_mlp_down, final_norm_gain=final_norm_gain, loss_target=loss_target, m_meta_tokens=m_meta_tokens, m_norm_mix_gain=m_norm_mix_gain, m_w_in=m_w_in, m_b_forget=m_b_forget, m_w_attn_out=m_w_attn_out, m_b_glu=m_b_glu, m_conv_dw_w=m_conv_dw_w, m_conv_dw_b=m_conv_dw_b, m_conv_ln_gain=m_conv_ln_gain, m_conv_ln_bias=m_conv_ln_bias, m_w_conv_out=m_w_conv_out, m_b_conv_out=m_b_conv_out, m_w_out=m_w_out, m_norm_mlp_gain=m_norm_mlp_gain, m_w_mlp_up=m_w_mlp_up, m_w_mlp_down=m_w_mlp_down, m_final_norm_gain=m_final_norm_gain, v_meta_tokens=v_meta_tokens, v_norm_mix_gain=v_norm_mix_gain, v_w_in=v_w_in, v_b_forget=v_b_forget, v_w_attn_out=v_w_attn_out, v_b_glu=v_b_glu, v_conv_dw_w=v_conv_dw_w, v_conv_dw_b=v_conv_dw_b, v_conv_ln_gain=v_conv_ln_gain, v_conv_ln_bias=v_conv_ln_bias, v_w_conv_out=v_w_conv_out, v_b_conv_out=v_b_conv_out, v_w_out=v_w_out, v_norm_mlp_gain=v_norm_mlp_gain, v_w_mlp_up=v_w_mlp_up, v_w_mlp_down=v_w_mlp_down, v_final_norm_gain=v_final_norm_gain)
    weights = {n: given[n] for n in TWIN_WEIGHTS}
    shared = {n: given[n] for n in SHARED_INPUTS}
    per_example = {n: given[n] for n in ['x']}
    grad_fn = _jax.value_and_grad(_loss, argnums=(0, 1))

    def one_microbatch(ex, loss_target):
        ex = dict(ex)
        diff = ex.pop(TWIN_DIFF_INPUT)
        return grad_fn(weights, diff, {**shared, **ex}, loss_target)

    if N_MICROBATCH == 1:
        loss, (grad_w, grad_x) = one_microbatch(per_example, given["loss_target"])
    else:
        def body(carry, xs):
            loss_sum, grad_sum = carry
            l_k, (gw_k, gx_k) = one_microbatch(xs[0], xs[1])
            with _jax.named_scope("update"):
                return (loss_sum + l_k, _jax.tree.map(_jnp.add, grad_sum, gw_k)), gx_k

        init = (_jnp.zeros((), _jnp.float32), _jax.tree.map(_jnp.zeros_like, weights))
        (loss, grad_w), grad_x = _jax.lax.scan(body, init, (per_example, given["loss_target"]))
    with _jax.named_scope("update"):
        delta_w, new_m, new_v = {}, {}, {}
        for n in TWIN_WEIGHTS:
            delta_w[n], new_m[n], new_v[n] = _adamw(weights[n], grad_w[n], given["m_" + n], given["v_" + n])
    return (loss, grad_x, *[grad_w[n] for n in TWIN_WEIGHTS], *[delta_w[n] for n in TWIN_WEIGHTS],
            *[new_m[n] for n in TWIN_WEIGHTS], *[new_v[n] for n in TWIN_WEIGHTS])
```

```python
import functools

import jax
import jax.numpy as jnp
from jax import lax
from jax.experimental import pallas as pl
from jax.experimental.pallas import tpu as pltpu

F32, BF16 = jnp.float32, jnp.bfloat16
N_DEV = 8
N_META = 16
HEAD_DIM = 64
LANES = 128
CONV_W = 31
RMS_EPS = 1e-6
LN_EPS = 1e-5
ROW_TILE = 688
SEG = 1024
FG_PAD = 128
VMEM_LIMIT = 56 * 1024 * 1024
ADAM_LR, ADAM_B1, ADAM_B2, ADAM_EPS, ADAM_WD, ADAM_STEP = 0.001, 0.9, 0.999, 1e-08, 0.01, 10
NEG = -1e30

SMALL = ("norm_mix_gain", "b_forget", "b_glu", "conv_dw_b", "conv_ln_gain", "conv_ln_bias", "b_conv_out",
         "norm_mlp_gain", "final_norm_gain")
SMALL_W = {"norm_mix_gain": 1024, "b_forget": 128, "b_glu": 2048, "conv_dw_b": 1024, "conv_ln_gain": 1024,
           "conv_ln_bias": 1024, "b_conv_out": 1024, "norm_mlp_gain": 1024, "final_norm_gain": 1024}
SMALL_N = {"norm_mix_gain": 1024, "b_forget": 16, "b_glu": 2048, "conv_dw_b": 1024, "conv_ln_gain": 1024,
           "conv_ln_bias": 1024, "b_conv_out": 1024, "norm_mlp_gain": 1024, "final_norm_gain": 1024}


def _params(sem=None):
    return pltpu.CompilerParams(dimension_semantics=sem, vmem_limit_bytes=VMEM_LIMIT)


def _sigmoid(x):
    return 1.0 / (1.0 + jnp.exp(-x))


def _dot_nt(a, b):
    return lax.dot_general(a, b, (((1,), (1,)), ((), ())), preferred_element_type=F32)


def _dot_tn(a, b):
    return lax.dot_general(a, b, (((0,), (0,)), ((), ())), preferred_element_type=F32)


def _exchange(name, items):
    n = len(items)

    def body(*refs):
        srcs, dsts = refs[:n], refs[n:2 * n]
        send_sems, recv_sems, local_sems = refs[2 * n:]
        x, y, c = lax.axis_index("x"), lax.axis_index("y"), lax.axis_index("c")
        me = 4 * x + 2 * y + c
        started = []
        for a in range(n):
            own = srcs[a].at[me] if items[a][1] else srcs[a]
            cp = pltpu.make_async_copy(own, dsts[a].at[me], local_sems.at[a])
            cp.start()
            started.append(cp)
        sends, recvs = [], []
        for k in range(1, N_DEV):
            px = 1 - x if k & 4 else x
            py = 1 - y if k & 2 else y
            pc = 1 - c if k & 1 else c
            peer = 4 * px + 2 * py + pc
            for a in range(n):
                src = srcs[a].at[peer] if items[a][1] else srcs[a]
                send = pltpu.make_async_remote_copy(
                    src_ref=src, dst_ref=dsts[a].at[me], send_sem=send_sems.at[a, k - 1],
                    recv_sem=recv_sems.at[a, k - 1], device_id=(px, py, pc), device_id_type=pl.DeviceIdType.MESH)
                send.start()
                sends.append(send)
                recvs.append(pltpu.make_async_remote_copy(
                    src_ref=src, dst_ref=dsts[a].at[peer], send_sem=send_sems.at[a, k - 1],
                    recv_sem=recv_sems.at[a, k - 1], device_id=(px, py, pc), device_id_type=pl.DeviceIdType.MESH))
        for r in recvs:
            r.wait_recv()
        for s in sends:
            s.wait_send()
        for cp in started:
            cp.wait()

    out_shape = []
    for arr, per_dest in items:
        shp = arr.shape if per_dest else (N_DEV,) + arr.shape
        out_shape.append(jax.ShapeDtypeStruct(shp, arr.dtype))
    any_spec = pl.BlockSpec(memory_space=pl.ANY)
    return pl.pallas_call(
        body, name=name, out_shape=out_shape,
        in_specs=[any_spec] * n, out_specs=[any_spec] * n,
        scratch_shapes=[pltpu.SemaphoreType.DMA((n, N_DEV - 1)), pltpu.SemaphoreType.DMA((n, N_DEV - 1)),
                        pltpu.SemaphoreType.DMA((n,))],
    )(*[it[0] for it in items])


def _mm_nn(name, x, w, w_spec, n_out, tn, out_dtype, relu2=False):
    m, k = x.shape
    tm = ROW_TILE

    def body(x_ref, w_ref, *outs):
        acc = jnp.dot(x_ref[...], w_ref[...], preferred_element_type=F32)
        outs[0][...] = acc.astype(outs[0].dtype)
        if relu2:
            r = jnp.maximum(acc, 0.0)
            outs[1][...] = (r * r).astype(outs[1].dtype)

    o_spec = pl.BlockSpec((tm, tn), lambda i, j: (i, j))
    shapes = [jax.ShapeDtypeStruct((m, n_out), out_dtype)]
    if relu2:
        shapes.append(jax.ShapeDtypeStruct((m, n_out), BF16))
    res = pl.pallas_call(
        body, name=name, out_shape=shapes, grid=(m // tm, n_out // tn),
        in_specs=[pl.BlockSpec((tm, k), lambda i, j: (i, 0)), w_spec],
        out_specs=[o_spec] * len(shapes), compiler_params=_params(("parallel", "parallel")),
    )(x, w)
    return res if relu2 else res[0]


def _mm_nt(name, pairs, m, n_out, tn, out_dtype, relu_bwd_of=None):
    tm = ROW_TILE
    np_ = len(pairs)

    def body(*refs):
        acc = None
        for p in range(np_):
            d = _dot_nt(refs[2 * p][...], refs[2 * p + 1][...])
            acc = d if acc is None else acc + d
        if relu_bwd_of is not None:
            acc = acc * (2.0 * jnp.maximum(refs[2 * np_][...].astype(F32), 0.0))
        refs[-1][...] = acc.astype(refs[-1].dtype)

    o_spec = pl.BlockSpec((tm, tn), lambda i, j: (i, j))
    operands, specs = [], []
    for a, a_spec, w, w_spec in pairs:
        operands += [a, w]
        specs += [a_spec, w_spec]
    if relu_bwd_of is not None:
        operands.append(relu_bwd_of)
        specs.append(o_spec)
    return pl.pallas_call(
        body, name=name, out_shape=jax.ShapeDtypeStruct((m, n_out), out_dtype), grid=(m // tm, n_out // tn),
        in_specs=specs, out_specs=o_spec, compiler_params=_params(("parallel", "parallel")),
    )(*operands)


def _mm_tn(name, x, x_col, ta, dy, dy_col, tb, out_shape, out_spec, grid_ab):
    m = x.shape[0]
    tm = ROW_TILE
    nk = m // tm

    def body(x_ref, dy_ref, o_ref):
        @pl.when(pl.program_id(2) == 0)
        def _():
            o_ref[...] = jnp.zeros(o_ref.shape, F32)

        o_ref[...] += _dot_tn(x_ref[...], dy_ref[...])

    return pl.pallas_call(
        body, name=name, out_shape=jax.ShapeDtypeStruct(out_shape, F32), grid=grid_ab + (nk,),
        in_specs=[pl.BlockSpec((tm, ta), lambda a, b, k: (k, x_col(a))),
                  pl.BlockSpec((tm, tb), lambda a, b, k: (k, dy_col(b)))],
        out_specs=out_spec, compiler_params=_params(("parallel", "parallel", "arbitrary")),
    )(x, dy)


def _w_cols(k, tn, off_blocks):
    return pl.BlockSpec((k, tn), lambda i, j: (0, off_blocks + j))


def _a_rows(kw, col_block=0):
    return pl.BlockSpec((ROW_TILE, kw), lambda i, j: (i, col_block))


def _w_rows(tn, kw, col_block=0):
    return pl.BlockSpec((tn, kw), lambda i, j: (j, col_block))


def _grad_w(name, x, dy):
    nb = dy.shape[1]
    tb = min(nb, 1024)
    return _mm_tn(name, x, lambda a: a, 1024, dy, lambda b: b, tb, (x.shape[1], nb),
                  pl.BlockSpec((1024, tb), lambda a, b, k: (a, b)), (x.shape[1] // 1024, nb // tb))


def _row_spec(width):
    return pl.BlockSpec((ROW_TILE, width), lambda i: (i, 0))


def _vec_spec(width):
    return pl.BlockSpec((1, width), lambda i: (0, 0))


def _rms_fwd(name, h, g, res=None):
    n, d = h.shape

    def body(*refs):
        if res is None:
            h_ref, g_ref, hn_ref = refs
            hv = h_ref[...]
        else:
            h_ref, r_ref, g_ref, hs_ref, hn_ref = refs
            hv = h_ref[...] + r_ref[...]
            hs_ref[...] = hv
        r = lax.rsqrt(jnp.mean(hv * hv, axis=-1, keepdims=True) + RMS_EPS)
        hn_ref[...] = (hv * r * g_ref[...]).astype(BF16)

    ins = [h, g] if res is None else [h, res, g]
    in_specs = [_row_spec(d), _vec_spec(d)] if res is None else [_row_spec(d), _row_spec(d), _vec_spec(d)]
    hn_shape = jax.ShapeDtypeStruct((n, d), BF16)
    if res is None:
        out_shape, out_specs = hn_shape, _row_spec(d)
    else:
        out_shape, out_specs = [jax.ShapeDtypeStruct((n, d), F32), hn_shape], [_row_spec(d), _row_spec(d)]
    return pl.pallas_call(body, name=name, out_shape=out_shape, grid=(n // ROW_TILE,), in_specs=in_specs,
                          out_specs=out_specs, compiler_params=_params(("parallel",)))(*ins)


def _rms_bwd(name, dhn, h, g, dres, batch, with_bf16=False, with_meta=False):
    n, d = h.shape
    t = n // batch
    nt = t // ROW_TILE

    def body(dhn_ref, h_ref, g_ref, dres_ref, *outs):
        first = (pl.program_id(0) == 0) & (pl.program_id(1) == 0)
        hv = h_ref[...]
        r = lax.rsqrt(jnp.mean(hv * hv, axis=-1, keepdims=True) + RMS_EPS)
        nrm = hv * r
        dn = dhn_ref[...] * g_ref[...]
        dh = dres_ref[...] + r * (dn - nrm * jnp.mean(dn * nrm, axis=-1, keepdims=True))
        outs[0][...] = dh
        dg_ref = outs[1]

        @pl.when(first)
        def _():
            dg_ref[...] = jnp.zeros(dg_ref.shape, F32)

        dg_ref[...] += jnp.sum(dhn_ref[...] * nrm, axis=0, keepdims=True)
        nxt = 2
        if with_bf16:
            outs[nxt][...] = dh.astype(BF16)
            nxt += 1
        if with_meta:
            meta_ref = outs[nxt]

            @pl.when(first)
            def _():
                meta_ref[...] = jnp.zeros(meta_ref.shape, F32)

            @pl.when(pl.program_id(1) == 0)
            def _():
                meta_ref[...] += dh[0:N_META, :]

    row = pl.BlockSpec((ROW_TILE, d), lambda b, j: (b * nt + j, 0))
    vec = pl.BlockSpec((1, d), lambda b, j: (0, 0))
    shapes = [jax.ShapeDtypeStruct((n, d), F32), jax.ShapeDtypeStruct((1, d), F32)]
    specs = [row, vec]
    if with_bf16:
        shapes.append(jax.ShapeDtypeStruct((n, d), BF16))
        specs.append(row)
    if with_meta:
        shapes.append(jax.ShapeDtypeStruct((N_META, d), F32))
        specs.append(pl.BlockSpec((N_META, d), lambda b, j: (0, 0)))
    return pl.pallas_call(body, name=name, out_shape=shapes, grid=(batch, nt), in_specs=[row, row, vec, row],
                          out_specs=specs, compiler_params=_params(("arbitrary", "arbitrary")))(dhn, h, g, dres)


def _final(name, h1, dn, tgt, g, batch):
    n, d = h1.shape
    t = n // batch
    nt = t // ROW_TILE

    def body(h1_ref, dn_ref, tgt_ref, g_ref, dh_ref, dhb_ref, loss_ref, dg_ref):
        first = (pl.program_id(0) == 0) & (pl.program_id(1) == 0)
        hv = h1_ref[...] + dn_ref[...]
        r = lax.rsqrt(jnp.mean(hv * hv, axis=-1, keepdims=True) + RMS_EPS)
        nrm = hv * r
        gv = g_ref[...]
        pos = pl.program_id(1) * ROW_TILE + lax.broadcasted_iota(jnp.int32, (ROW_TILE, 1), 0)
        diff = jnp.where(pos >= N_META, nrm * gv - tgt_ref[...], 0.0)
        dy = diff * (1.0 / d)

        @pl.when(first)
        def _():
            loss_ref[...] = jnp.zeros(loss_ref.shape, F32)
            dg_ref[...] = jnp.zeros(dg_ref.shape, F32)

        loss_ref[...] += jnp.full(loss_ref.shape, 0.5 / d, F32) * jnp.sum(diff * diff)
        dg_ref[...] += jnp.sum(dy * nrm, axis=0, keepdims=True)
        dng = dy * gv
        dh = r * (dng - nrm * jnp.mean(dng * nrm, axis=-1, keepdims=True))
        dh_ref[...] = dh
        dhb_ref[...] = dh.astype(BF16)

    row = pl.BlockSpec((ROW_TILE, d), lambda b, j: (b * nt + j, 0))
    vec = pl.BlockSpec((1, d), lambda b, j: (0, 0))
    return pl.pallas_call(
        body, name=name, grid=(batch, nt), in_specs=[row, row, row, vec],
        out_shape=[jax.ShapeDtypeStruct((n, d), F32), jax.ShapeDtypeStruct((n, d), BF16),
                   jax.ShapeDtypeStruct((8, 128), F32), jax.ShapeDtypeStruct((1, d), F32)],
        out_specs=[row, row, pl.BlockSpec((8, 128), lambda b, j: (0, 0)), vec],
        compiler_params=_params(("arbitrary", "arbitrary")))(h1, dn, tgt, g)


def _ln_silu_fwd(name, c1, g, b):
    n, d = c1.shape

    def body(c_ref, g_ref, b_ref, o_ref):
        xv = c_ref[...]
        xc = xv - jnp.mean(xv, axis=-1, keepdims=True)
        rstd = lax.rsqrt(jnp.mean(xc * xc, axis=-1, keepdims=True) + LN_EPS)
        c2 = xc * rstd * g_ref[...] + b_ref[...]
        o_ref[...] = (c2 * _sigmoid(c2)).astype(BF16)

    return pl.pallas_call(body, name=name, out_shape=jax.ShapeDtypeStruct((n, d), BF16), grid=(n // ROW_TILE,),
                          in_specs=[_row_spec(d), _vec_spec(d), _vec_spec(d)], out_specs=_row_spec(d),
                          compiler_params=_params(("parallel",)))(c1, g, b)


def _ln_silu_bwd(name, dc3, c1, g, b):
    n, d = c1.shape

    def body(d_ref, c_ref, g_ref, b_ref, dc1_ref, dg_ref, db_ref):
        xv = c_ref[...]
        xc = xv - jnp.mean(xv, axis=-1, keepdims=True)
        rstd = lax.rsqrt(jnp.mean(xc * xc, axis=-1, keepdims=True) + LN_EPS)
        xh = xc * rstd
        c2 = xh * g_ref[...] + b_ref[...]
        s = _sigmoid(c2)
        dc2 = d_ref[...] * (s * (1.0 + c2 * (1.0 - s)))

        @pl.when(pl.program_id(0) == 0)
        def _():
            dg_ref[...] = jnp.zeros(dg_ref.shape, F32)
            db_ref[...] = jnp.zeros(db_ref.shape, F32)

        dg_ref[...] += jnp.sum(dc2 * xh, axis=0, keepdims=True)
        db_ref[...] += jnp.sum(dc2, axis=0, keepdims=True)
        dxh = dc2 * g_ref[...]
        dc1_ref[...] = rstd * (dxh - jnp.mean(dxh, axis=-1, keepdims=True)
                               - xh * jnp.mean(dxh * xh, axis=-1, keepdims=True))

    return pl.pallas_call(
        body, name=name, grid=(n // ROW_TILE,),
        out_shape=[jax.ShapeDtypeStruct((n, d), F32), jax.ShapeDtypeStruct((1, d), F32),
                   jax.ShapeDtypeStruct((1, d), F32)],
        in_specs=[_row_spec(d), _row_spec(d), _vec_spec(d), _vec_spec(d)],
        out_specs=[_row_spec(d), _vec_spec(d), _vec_spec(d)],
        compiler_params=_params(("arbitrary",)))(dc3, c1, g, b)


MERGE_TC = 512


def _merge_fwd(name, gates, a, c, b_co):
    n, d = a.shape
    nc = d // MERGE_TC

    def body(ga_ref, gc_ref, a_ref, c_ref, b_ref, m_ref):
        m = _sigmoid(ga_ref[...]) * a_ref[...] + _sigmoid(gc_ref[...]) * (c_ref[...] + b_ref[...])
        m_ref[...] = m.astype(BF16)

    blk = lambda off: pl.BlockSpec((ROW_TILE, MERGE_TC), lambda i, j: (i, off + j))
    return pl.pallas_call(
        body, name=name, out_shape=jax.ShapeDtypeStruct((n, d), BF16), grid=(n // ROW_TILE, nc),
        in_specs=[blk(0), blk(nc), blk(0), blk(0), pl.BlockSpec((1, MERGE_TC), lambda i, j: (0, j))],
        out_specs=blk(0), compiler_params=_params(("parallel", "parallel")))(gates, gates, a, c, b_co)


def _merge_bwd(name, dm, gates, a, c, b_co):
    n, d = a.shape
    nc = d // MERGE_TC

    def body(dm_ref, ga_ref, gc_ref, a_ref, c_ref, b_ref, da_ref, dc_ref, dga_ref, dgc_ref, dbco_ref):
        dmv = dm_ref[...]
        sa, sc = _sigmoid(ga_ref[...]), _sigmoid(gc_ref[...])
        dc = dmv * sc
        da_ref[...] = (dmv * sa).astype(BF16)
        dc_ref[...] = dc.astype(BF16)
        dga_ref[...] = (dmv * a_ref[...] * sa * (1.0 - sa)).astype(BF16)
        dgc_ref[...] = (dmv * (c_ref[...] + b_ref[...]) * sc * (1.0 - sc)).astype(BF16)

        @pl.when(pl.program_id(1) == 0)
        def _():
            dbco_ref[...] = jnp.zeros(dbco_ref.shape, F32)

        dbco_ref[...] += jnp.sum(dc, axis=0, keepdims=True)

    blk = lambda off: pl.BlockSpec((ROW_TILE, MERGE_TC), lambda j, i: (i, off + j))
    vec = pl.BlockSpec((1, MERGE_TC), lambda j, i: (0, j))
    act = jax.ShapeDtypeStruct((n, d), BF16)
    return pl.pallas_call(
        body, name=name, grid=(nc, n // ROW_TILE),
        out_shape=[act, act, act, act, jax.ShapeDtypeStruct((1, d), F32)],
        in_specs=[blk(0), blk(0), blk(nc), blk(0), blk(0), vec],
        out_specs=[blk(0), blk(0), blk(0), blk(0), vec],
        compiler_params=_params(("parallel", "arbitrary")))(dm, gates, gates, a, c, b_co)


CONV_TC = 128
CONV_HALO = 32


def _glu_conv_fwd(name, glu, b_glu, w_dw, b_dw, batch):
    n, c2 = glu.shape
    c = c2 // 2
    t = n // batch
    nc = c // CONV_TC

    def body(a_ref, gt_ref, ba_ref, bg_ref, w_ref, bdw_ref, o_ref, pad_ref):
        u = (a_ref[...] + ba_ref[...]) * _sigmoid(gt_ref[...] + bg_ref[...])
        pad_ref[0:CONV_HALO, :] = jnp.zeros((CONV_HALO, CONV_TC), F32)
        pad_ref[CONV_HALO:CONV_HALO + t, :] = u
        acc = jnp.zeros((t, CONV_TC), F32) + bdw_ref[...]
        for j in range(CONV_W):
            off = CONV_HALO - (CONV_W - 1) + j
            acc = acc + w_ref[j:j + 1, :] * pad_ref[off:off + t, :]
        o_ref[...] = acc

    seq = lambda off: pl.BlockSpec((t, CONV_TC), lambda b, j: (b, off + j))
    vec = lambda off: pl.BlockSpec((1, CONV_TC), lambda b, j: (0, off + j))
    return pl.pallas_call(
        body, name=name, out_shape=jax.ShapeDtypeStruct((n, c), F32), grid=(batch, nc),
        in_specs=[seq(0), seq(nc), vec(0), vec(nc), pl.BlockSpec((CONV_W, CONV_TC), lambda b, j: (0, j)), vec(0)],
        out_specs=seq(0), scratch_shapes=[pltpu.VMEM((t + CONV_HALO, CONV_TC), F32)],
        compiler_params=_params(("parallel", "parallel")))(glu, glu, b_glu, b_glu, w_dw, b_dw)


def _glu_conv_bwd(name, dc1, glu, b_glu, w_dw, batch):
    n, c2 = glu.shape
    c = c2 // 2
    t = n // batch
    nc = c // CONV_TC

    def body(d_ref, a_ref, gt_ref, ba_ref, bg_ref, w_ref, dga_ref, dgg_ref, dw_ref, dbdw_ref, dba_ref, dbg_ref,
             padu_ref, padd_ref):
        av = a_ref[...] + ba_ref[...]
        sg = _sigmoid(gt_ref[...] + bg_ref[...])
        dc = d_ref[...]
        padu_ref[0:CONV_HALO, :] = jnp.zeros((CONV_HALO, CONV_TC), F32)
        padu_ref[CONV_HALO:CONV_HALO + t, :] = av * sg
        padd_ref[0:t, :] = dc
        padd_ref[t:t + CONV_HALO, :] = jnp.zeros((CONV_HALO, CONV_TC), F32)

        @pl.when(pl.program_id(1) == 0)
        def _():
            dw_ref[...] = jnp.zeros(dw_ref.shape, F32)
            dbdw_ref[...] = jnp.zeros(dbdw_ref.shape, F32)
            dba_ref[...] = jnp.zeros(dba_ref.shape, F32)
            dbg_ref[...] = jnp.zeros(dbg_ref.shape, F32)

        du = jnp.zeros((t, CONV_TC), F32)
        for j in range(CONV_W):
            back = CONV_W - 1 - j
            du = du + w_ref[j:j + 1, :] * padd_ref[back:back + t, :]
            off = CONV_HALO - (CONV_W - 1) + j
            dw_ref[j:j + 1, :] += jnp.sum(dc * padu_ref[off:off + t, :], axis=0, keepdims=True)
        dga = du * sg
        dgg = du * av * sg * (1.0 - sg)
        dga_ref[...] = dga.astype(BF16)
        dgg_ref[...] = dgg.astype(BF16)
        dbdw_ref[...] += jnp.sum(dc, axis=0, keepdims=True)
        dba_ref[...] += jnp.sum(dga, axis=0, keepdims=True)
        dbg_ref[...] += jnp.sum(dgg, axis=0, keepdims=True)

    seq = lambda off: pl.BlockSpec((t, CONV_TC), lambda j, b: (b, off + j))
    vec = lambda off: pl.BlockSpec((1, CONV_TC), lambda j, b: (0, off + j))
    wsp = pl.BlockSpec((CONV_W, CONV_TC), lambda j, b: (0, j))
    act = jax.ShapeDtypeStruct((n, c), BF16)
    v = jax.ShapeDtypeStruct((1, c), F32)
    return pl.pallas_call(
        body, name=name, grid=(nc, batch),
        out_shape=[act, act, jax.ShapeDtypeStruct((CONV_W, c), F32), v, v, v],
        in_specs=[seq(0), seq(0), seq(nc), vec(0), vec(nc), wsp],
        out_specs=[seq(0), seq(0), wsp, vec(0), vec(0), vec(0)],
        scratch_shapes=[pltpu.VMEM((t + CONV_HALO, CONV_TC), F32), pltpu.VMEM((t + CONV_HALO, CONV_TC), F32)],
        compiler_params=_params(("parallel", "arbitrary")))(dc1, glu, glu, b_glu, b_glu, w_dw)


def _split3(x):
    hi = x.astype(BF16)
    r = x - hi.astype(F32)
    mid = r.astype(BF16)
    lo = (r - mid.astype(F32)).astype(BF16)
    return hi, mid, lo


def _tri_matmul(tri, x):
    hi, mid, lo = _split3(x)
    dot = lambda v: jnp.dot(tri, v, preferred_element_type=F32)
    return dot(hi) + dot(mid) + dot(lo)


def _fox_prep_fwd(name, fg, b_fg, batch):
    n, w = fg.shape
    t = n // batch
    nq = t // ROW_TILE

    def body(fg_ref, b_ref, cum_ref):
        row = lax.broadcasted_iota(jnp.int32, (ROW_TILE, ROW_TILE), 0)
        col = lax.broadcasted_iota(jnp.int32, (ROW_TILE, ROW_TILE), 1)
        tri = (row >= col).astype(BF16)
        for k in range(nq):
            rows = slice(k * ROW_TILE, (k + 1) * ROW_TILE)
            z = fg_ref[rows, :] + b_ref[...]
            logf = jnp.minimum(z, 0.0) - jnp.log(1.0 + jnp.exp(-jnp.abs(z)))
            cum = _tri_matmul(tri, logf)
            if k > 0:
                cum = cum + cum_ref[k * ROW_TILE - 1:k * ROW_TILE, :]
            cum_ref[rows, :] = cum

    seq = pl.BlockSpec((t, w), lambda b: (b, 0))
    return pl.pallas_call(body, name=name, out_shape=jax.ShapeDtypeStruct((n, w), F32), grid=(batch,),
                          in_specs=[seq, pl.BlockSpec((1, w), lambda b: (0, 0))], out_specs=seq,
                          compiler_params=_params(("parallel",)))(fg, b_fg)


def _fox_prep_bwd(name, dcum_k, dcum_q, fg, b_fg, batch):
    n, w = fg.shape
    t = n // batch
    nq = t // ROW_TILE

    def body(dk_ref, dq_ref, fg_ref, b_ref, dfg_ref, db_ref, rev_ref):
        row = lax.broadcasted_iota(jnp.int32, (ROW_TILE, ROW_TILE), 0)
        col = lax.broadcasted_iota(jnp.int32, (ROW_TILE, ROW_TILE), 1)
        tri = (col >= row).astype(BF16)

        @pl.when(pl.program_id(0) == 0)
        def _():
            db_ref[...] = jnp.zeros(db_ref.shape, F32)

        for k in reversed(range(nq)):
            rows = slice(k * ROW_TILE, (k + 1) * ROW_TILE)
            dlog = _tri_matmul(tri, dk_ref[rows, :] + dq_ref[rows, :])
            if k < nq - 1:
                dlog = dlog + rev_ref[(k + 1) * ROW_TILE:(k + 1) * ROW_TILE + 1, :]
            rev_ref[rows, :] = dlog
            dfg = dlog * _sigmoid(-(fg_ref[rows, :] + b_ref[...]))
            dfg_ref[rows, :] = dfg.astype(BF16)
            db_ref[...] += jnp.sum(dfg, axis=0, keepdims=True)

    seq = pl.BlockSpec((t, w), lambda b: (b, 0))
    vec = pl.BlockSpec((1, w), lambda b: (0, 0))
    return pl.pallas_call(
        body, name=name, grid=(batch,),
        out_shape=[jax.ShapeDtypeStruct((n, w), BF16), jax.ShapeDtypeStruct((1, w), F32)],
        in_specs=[seq, seq, seq, vec], out_specs=[seq, vec], scratch_shapes=[pltpu.VMEM((t, w), F32)],
        compiler_params=_params(("arbitrary",)))(dcum_k, dcum_q, fg, b_fg)


def _head_masks(x):
    lane = lax.broadcasted_iota(jnp.int32, x.shape, 1)
    zero = jnp.zeros(x.shape, x.dtype)
    return jnp.where(lane < HEAD_DIM, x, zero), jnp.where(lane >= HEAD_DIM, x, zero)


def _attn_specs(t, n_pairs, nq):
    qkv = lambda off: pl.BlockSpec((t, LANES), lambda b, h: (b, off + h))
    cumc = pl.BlockSpec((None, None, t, 2), lambda b, h: (b, h, 0, 0))
    cumr = pl.BlockSpec((None, None, nq, 8, ROW_TILE), lambda b, h: (b, h, 0, 0, 0))
    return qkv, cumc, cumr


def _attn_fwd(name, qkv, cumc, cumr, batch):
    n, w3 = qkv.shape
    w = w3 // 3
    t = n // batch
    nq = t // ROW_TILE
    n_pairs = w // LANES
    tq = ROW_TILE

    def body(q_ref, k_ref, v_ref, cc_ref, cr_ref, o_ref, lse_ref):
        row = lax.broadcasted_iota(jnp.int32, (tq, tq), 0)
        col = lax.broadcasted_iota(jnp.int32, (tq, tq), 1)
        causal = row >= col
        lane = lax.broadcasted_iota(jnp.int32, (tq, LANES), 1)
        for i in range(nq):
            rows = slice(i * tq, (i + 1) * tq)
            qs = _head_masks(q_ref[rows, :] * 0.125)
            outs, lses = [], []
            for hh in range(2):
                cq = cc_ref[rows, hh:hh + 1]
                m = jnp.full((tq, 1), NEG, F32)
                l = jnp.zeros((tq, 1), F32)
                acc = jnp.zeros((tq, LANES), F32)
                for j in range(i + 1):
                    cols = slice(j * tq, (j + 1) * tq)
                    s = _dot_nt(qs[hh], k_ref[cols, :]) + cq - cr_ref[j, hh:hh + 1, :]
                    if j == i:
                        s = jnp.where(causal, s, NEG)
                    m_new = jnp.maximum(m, jnp.max(s, axis=1, keepdims=True))
                    alpha = jnp.exp(m - m_new)
                    p = jnp.exp(s - m_new)
                    l = alpha * l + jnp.sum(p, axis=1, keepdims=True)
                    acc = alpha * acc + jnp.dot(p.astype(BF16), v_ref[cols, :], preferred_element_type=F32)
                    m = m_new
                outs.append(acc / l)
                lses.append(m + jnp.log(l))
            o_ref[rows, :] = jnp.where(lane < HEAD_DIM, outs[0], outs[1]).astype(BF16)
            lse_ref[rows, :] = jnp.where(lane < HEAD_DIM, lses[0], lses[1])

    qkv_spec, cumc_spec, cumr_spec = _attn_specs(t, n_pairs, nq)
    return pl.pallas_call(
        body, name=name, grid=(batch, n_pairs),
        out_shape=[jax.ShapeDtypeStruct((n, w), BF16), jax.ShapeDtypeStruct((n, w), F32)],
        in_specs=[qkv_spec(0), qkv_spec(n_pairs), qkv_spec(2 * n_pairs), cumc_spec, cumr_spec],
        out_specs=[qkv_spec(0), qkv_spec(0)],
        compiler_params=_params(("parallel", "parallel")))(qkv, qkv, qkv, cumc, cumr)


def _attn_bwd(name, qkv, o, do, lse, cumc, cumr, batch):
    n, w3 = qkv.shape
    w = w3 // 3
    t = n // batch
    nq = t // ROW_TILE
    n_pairs = w // LANES
    tq = ROW_TILE

    def body(q_ref, k_ref, v_ref, o_ref, do_ref, lse_ref, cc_ref, cr_ref, dq_ref, dk_ref, dv_ref, dcr_ref, dcq_ref,
             dk_acc, dv_acc):
        row = lax.broadcasted_iota(jnp.int32, (tq, tq), 0)
        col = lax.broadcasted_iota(jnp.int32, (tq, tq), 1)
        causal = row >= col
        lane = lax.broadcasted_iota(jnp.int32, (tq, LANES), 1)
        dk_acc[...] = jnp.zeros(dk_acc.shape, F32)
        dv_acc[...] = jnp.zeros(dv_acc.shape, F32)
        dcr_ref[...] = jnp.zeros(dcr_ref.shape, F32)
        for i in range(nq):
            rows = slice(i * tq, (i + 1) * tq)
            qs = _head_masks(q_ref[rows, :] * 0.125)
            dos = _head_masks(do_ref[rows, :])
            dq = jnp.zeros((tq, LANES), F32)
            dcq = []
            for hh in range(2):
                row_sum = jnp.zeros((tq, 1), F32)
                cq = cc_ref[rows, hh:hh + 1]
                lse = lse_ref[rows, hh * HEAD_DIM:hh * HEAD_DIM + 1]
                delta = jnp.sum(dos[hh].astype(F32) * o_ref[rows, :].astype(F32), axis=1, keepdims=True)
                for j in range(i + 1):
                    cols = slice(j * tq, (j + 1) * tq)
                    s = _dot_nt(qs[hh], k_ref[cols, :]) + cq - cr_ref[j, hh:hh + 1, :]
                    p = jnp.exp(s - lse)
                    if j == i:
                        p = jnp.where(causal, p, 0.0)
                    dp = _dot_nt(dos[hh], v_ref[cols, :])
                    ds = p * (dp - delta)
                    pb, dsb = p.astype(BF16), ds.astype(BF16)
                    km = _head_masks(k_ref[cols, :])[hh]
                    dv_acc[cols, :] += _dot_tn(pb, dos[hh])
                    dk_acc[cols, :] += _dot_tn(dsb, qs[hh])
                    dq = dq + jnp.dot(dsb, km, preferred_element_type=F32)
                    dcr_ref[j, hh:hh + 1, :] -= jnp.sum(ds, axis=0, keepdims=True)
                    row_sum = row_sum + jnp.sum(ds, axis=1, keepdims=True)
                dcq.append(row_sum)
            dq_ref[rows, :] = (dq * 0.125).astype(BF16)
            dcq_ref[rows, :] = jnp.where(lane < HEAD_DIM, dcq[0], dcq[1])
        dk_ref[...] = dk_acc[...].astype(BF16)
        dv_ref[...] = dv_acc[...].astype(BF16)

    qkv_spec, cumc_spec, cumr_spec = _attn_specs(t, n_pairs, nq)
    act = jax.ShapeDtypeStruct((n, w), BF16)
    return pl.pallas_call(
        body, name=name, grid=(batch, n_pairs),
        out_shape=[act, act, act, jax.ShapeDtypeStruct(cumr.shape, F32), jax.ShapeDtypeStruct((n, w), F32)],
        in_specs=[qkv_spec(0), qkv_spec(n_pairs), qkv_spec(2 * n_pairs), qkv_spec(0), qkv_spec(0), qkv_spec(0),
                  cumc_spec, cumr_spec],
        out_specs=[qkv_spec(0), qkv_spec(0), qkv_spec(0), cumr_spec, qkv_spec(0)],
        scratch_shapes=[pltpu.VMEM((t, LANES), F32), pltpu.VMEM((t, LANES), F32)],
        compiler_params=_params(("parallel", "parallel")))(qkv, qkv, qkv, o, do, lse, cumc, cumr)


def _adamw(name, parts, w, m, v):
    r, c = w.shape
    tr = 128 if r % 128 == 0 else r
    c1 = 1.0 - ADAM_B1 ** ADAM_STEP
    c2 = 1.0 - ADAM_B2 ** ADAM_STEP

    def body(p_ref, w_ref, m_ref, v_ref, g_ref, d_ref, m2_ref, v2_ref):
        g = p_ref[0]
        for s in range(1, N_DEV):
            g = g + p_ref[s]
        m2 = ADAM_B1 * m_ref[...] + (1.0 - ADAM_B1) * g
        v2 = ADAM_B2 * v_ref[...] + (1.0 - ADAM_B2) * (g * g)
        g_ref[...] = g
        m2_ref[...] = m2
        v2_ref[...] = v2
        d_ref[...] = -ADAM_LR * ((m2 / c1) / (jnp.sqrt(v2 / c2) + ADAM_EPS) + ADAM_WD * w_ref[...])

    blk = pl.BlockSpec((tr, c), lambda i: (i, 0))
    shp = jax.ShapeDtypeStruct((r, c), F32)
    return pl.pallas_call(
        body, name=name, out_shape=[shp] * 4, grid=(r // tr,),
        in_specs=[pl.BlockSpec((N_DEV, tr, c), lambda i: (0, i, 0)), blk, blk, blk], out_specs=[blk] * 4,
        compiler_params=_params(("parallel",)))(parts, w, m, v)


def _cat_small(vals):
    parts = []
    for name in SMALL:
        v = vals[name].reshape(1, -1).astype(F32)
        parts.append(jnp.pad(v, ((0, 0), (0, SMALL_W[name] - v.shape[1]))))
    return jnp.concatenate(parts, axis=1)


def _split_small(row, shapes):
    out, off = {}, 0
    for name in SMALL:
        out[name] = row[0, off:off + SMALL_N[name]].reshape(shapes[name])
        off += SMALL_W[name]
    return out


def _cols_from_shards(g):
    return jnp.transpose(g, (1, 0, 2)).reshape(g.shape[1], N_DEV * g.shape[2])


def _shards_from_cols(a):
    r, c = a.shape
    return jnp.transpose(a.reshape(r, N_DEV, c // N_DEV), (1, 0, 2))


def kernel(x, meta_tokens, norm_mix_gain, w_in, b_forget, w_attn_out, b_glu, conv_dw_w, conv_dw_b, conv_ln_gain, conv_ln_bias, w_conv_out, b_conv_out, w_out, norm_mlp_gain, w_mlp_up, w_mlp_down, final_norm_gain, loss_target, m_meta_tokens, m_norm_mix_gain, m_w_in, m_b_forget, m_w_attn_out, m_b_glu, m_conv_dw_w, m_conv_dw_b, m_conv_ln_gain, m_conv_ln_bias, m_w_conv_out, m_b_conv_out, m_w_out, m_norm_mlp_gain, m_w_mlp_up, m_w_mlp_down, m_final_norm_gain, v_meta_tokens, v_norm_mix_gain, v_w_in, v_b_forget, v_w_attn_out, v_b_glu, v_conv_dw_w, v_conv_dw_b, v_conv_ln_gain, v_conv_ln_bias, v_w_conv_out, v_b_conv_out, v_w_out, v_norm_mlp_gain, v_w_mlp_up, v_w_mlp_down, v_final_norm_gain):
    weights = dict(meta_tokens=meta_tokens, norm_mix_gain=norm_mix_gain, w_in=w_in, b_forget=b_forget, w_attn_out=w_attn_out, b_glu=b_glu, conv_dw_w=conv_dw_w, conv_dw_b=conv_dw_b, conv_ln_gain=conv_ln_gain, conv_ln_bias=conv_ln_bias, w_conv_out=w_conv_out, b_conv_out=b_conv_out, w_out=w_out, norm_mlp_gain=norm_mlp_gain, w_mlp_up=w_mlp_up, w_mlp_down=w_mlp_down, final_norm_gain=final_norm_gain)
    mom_m = dict(meta_tokens=m_meta_tokens, norm_mix_gain=m_norm_mix_gain, w_in=m_w_in, b_forget=m_b_forget, w_attn_out=m_w_attn_out, b_glu=m_b_glu, conv_dw_w=m_conv_dw_w, conv_dw_b=m_conv_dw_b, conv_ln_gain=m_conv_ln_gain, conv_ln_bias=m_conv_ln_bias, w_conv_out=m_w_conv_out, b_conv_out=m_b_conv_out, w_out=m_w_out, norm_mlp_gain=m_norm_mlp_gain, w_mlp_up=m_w_mlp_up, w_mlp_down=m_w_mlp_down, final_norm_gain=m_final_norm_gain)
    mom_v = dict(meta_tokens=v_meta_tokens, norm_mix_gain=v_norm_mix_gain, w_in=v_w_in, b_forget=v_b_forget, w_attn_out=v_w_attn_out, b_glu=v_b_glu, conv_dw_w=v_conv_dw_w, conv_dw_b=v_conv_dw_b, conv_ln_gain=v_conv_ln_gain, conv_ln_bias=v_conv_ln_bias, w_conv_out=v_w_conv_out, b_conv_out=v_b_conv_out, w_out=v_w_out, norm_mlp_gain=v_norm_mlp_gain, w_mlp_up=v_w_mlp_up, w_mlp_down=v_w_mlp_down, final_norm_gain=v_final_norm_gain)
    names = list(weights)
    batch, seq, d = x.shape
    t = seq + N_META
    n = batch * t
    nq = t // ROW_TILE
    n_pairs = d // LANES
    assert t % ROW_TILE == 0 and d == SEG

    gathered = _exchange("gather_weights", [
        (w_in[0].astype(BF16), False), (w_attn_out[0].astype(BF16), False), (w_conv_out[0].astype(BF16), False),
        (w_out[0].astype(BF16), False), (w_mlp_up[0].astype(BF16), False), (w_mlp_down[0].astype(BF16), False),
        (meta_tokens, False), (conv_dw_w[0], False)])
    w_in_f = _cols_from_shards(gathered[0])
    o_fg = 3 * SEG
    n_fg = b_forget.shape[1]
    w_pad = jnp.concatenate([w_in_f[:, :o_fg], w_in_f[:, o_fg + n_fg:], w_in_f[:, o_fg:o_fg + n_fg],
                             jnp.zeros((d, FG_PAD - n_fg), BF16)], axis=1)
    w_ao = gathered[1].reshape(d, d)
    w_co = gathered[2].reshape(d, d)
    w_o = gathered[3].reshape(d, d)
    w_up = gathered[4]
    w_dn = gathered[5].reshape(-1, d)
    d_ff = w_dn.shape[0]
    ff_blk = d_ff // N_DEV
    meta_f = _cols_from_shards(gathered[6])
    w_dw = _cols_from_shards(gathered[7])

    row2 = lambda v: v.reshape(1, -1)
    g1, g2, g3 = row2(norm_mix_gain), row2(norm_mlp_gain), row2(final_norm_gain)
    b_fg = jnp.pad(b_forget, ((0, 0), (0, FG_PAD - n_fg)))
    h0 = jnp.concatenate([jnp.broadcast_to(meta_f[None], (batch, N_META, d)), x], axis=1).reshape(n, d)
    tgt = jnp.concatenate([jnp.zeros((batch, N_META, d), F32), loss_target], axis=1).reshape(n, d)

    hn1 = _rms_fwd("rms1", h0, g1)
    qkv = _mm_nn("proj_qkv", hn1, w_pad, _w_cols(d, 512, 0), 3 * SEG, 512, BF16)
    glu = _mm_nn("proj_glu", hn1, w_pad, _w_cols(d, 512, 3 * SEG // 512), 2 * SEG, 512, F32)
    gates = _mm_nn("proj_gates", hn1, w_pad, _w_cols(d, 512, 5 * SEG // 512), 2 * SEG, 512, F32)
    fg = _mm_nn("proj_fg", hn1, w_pad, _w_cols(d, FG_PAD, 7 * SEG // FG_PAD), FG_PAD, FG_PAD, F32)

    cum = _fox_prep_fwd("fox_cumsum", fg, b_fg, batch)
    cum_h = cum.reshape(batch, t, FG_PAD)[:, :, :2 * n_pairs].reshape(batch, t, n_pairs, 2)
    cumc = jnp.transpose(cum_h, (0, 2, 1, 3))
    cumr = jnp.transpose(cum_h.reshape(batch, nq, ROW_TILE, n_pairs, 2), (0, 3, 1, 4, 2))
    cumr = jnp.pad(cumr, ((0, 0), (0, 0), (0, 0), (0, 6), (0, 0)))
    o, lse = _attn_fwd("attn_fwd", qkv, cumc, cumr, batch)
    a = _mm_nn("attn_out", o, w_ao, _w_cols(d, 512, 0), d, 512, F32)

    c1 = _glu_conv_fwd("glu_conv", glu, b_glu, w_dw, conv_dw_b, batch)
    c3 = _ln_silu_fwd("ln_silu", c1, conv_ln_gain, conv_ln_bias)
    c = _mm_nn("conv_out", c3, w_co, _w_cols(d, 512, 0), d, 512, F32)

    mrg = _merge_fwd("merge", gates, a, c, b_conv_out)
    mo = _mm_nn("mix_out", mrg, w_o, _w_cols(d, 512, 0), d, 512, F32)
    h1, hn2 = _rms_fwd("resid_rms2", h0, g2, res=mo)
    per = ff_blk // 512
    up, act = _mm_nn("mlp_up", hn2, w_up, pl.BlockSpec((None, d, 512), lambda i, j: (j // per, 0, j % per)),
                     d_ff, 512, BF16, relu2=True)
    dn = _mm_nn("mlp_down", act, w_dn, _w_cols(d_ff, 512, 0), d, 512, F32)
    dh2, dh2b, loss_blk, dg3 = _final("final_loss", h1, dn, tgt, g3, batch)

    dup = _mm_nt("d_mlp_down", [(dh2b, _a_rows(d), w_dn, _w_rows(512, d))], n, d_ff, 512, BF16, relu_bwd_of=up)
    dw_dn = _grad_w("gw_mlp_down", act, dh2b)
    dhn2 = _mm_nt("d_mlp_up", [(dup, _a_rows(ff_blk, g), w_up, pl.BlockSpec((None, 256, ff_blk), lambda i, j, g=g: (g, j, 0)))
                               for g in range(N_DEV)], n, d, 256, F32)
    dw_up = _mm_tn("gw_mlp_up", hn2, lambda a_: 0, d, dup, lambda b_: b_, ff_blk, (N_DEV, d, ff_blk),
                   pl.BlockSpec((None, d, ff_blk), lambda a_, b_, k: (b_, 0, 0)), (1, N_DEV))
    dh1, dg2, dh1b = _rms_bwd("rms2_bwd", dhn2, h1, g2, dh2, batch, with_bf16=True)

    dm = _mm_nt("d_mix_out", [(dh1b, _a_rows(d), w_o, _w_rows(512, d))], n, d, 512, F32)
    dw_o = _grad_w("gw_mix_out", mrg, dh1b)
    da, dc, dga, dgc, dbco = _merge_bwd("merge_bwd", dm, gates, a, c, b_conv_out)

    do = _mm_nt("d_attn_out", [(da, _a_rows(d), w_ao, _w_rows(512, d))], n, d, 512, BF16)
    dw_ao = _grad_w("gw_attn_out", o, da)
    dc3 = _mm_nt("d_conv_out", [(dc, _a_rows(d), w_co, _w_rows(512, d))], n, d, 512, F32)
    dw_co = _grad_w("gw_conv_out", c3, dc)

    dc1, dg_ln, db_ln = _ln_silu_bwd("ln_silu_bwd", dc3, c1, conv_ln_gain, conv_ln_bias)
    dglu_a, dglu_g, dw_dw, db_dw, dbg_a, dbg_g = _glu_conv_bwd("glu_conv_bwd", dc1, glu, b_glu, w_dw, batch)

    dq, dk, dv, dcumr, dcumq = _attn_bwd("attn_bwd", qkv, o, do, lse, cumc, cumr, batch)
    pad_fg = lambda g_: jnp.pad(g_, ((0, 0), (0, FG_PAD - 2 * n_pairs)))
    dcum_k = pad_fg(jnp.transpose(dcumr[:, :, :, :2, :], (0, 2, 4, 1, 3)).reshape(n, 2 * n_pairs))
    dcum_q = pad_fg(dcumq[:, ::HEAD_DIM])
    dfg, db_fg = _fox_prep_bwd("fox_cumsum_bwd", dcum_k, dcum_q, fg, b_fg, batch)

    segs = [dq, dk, dv, dglu_a, dglu_g, dga, dgc]
    pairs = [(s_, _a_rows(SEG), w_pad, _w_rows(256, SEG, i)) for i, s_ in enumerate(segs)]
    pairs.append((dfg, _a_rows(FG_PAD), w_pad, _w_rows(256, FG_PAD, 7 * SEG // FG_PAD)))
    dhn1 = _mm_nt("d_proj_in", pairs, n, d, 256, F32)
    gw_seg = [_grad_w("gw_in_%d" % i, hn1, s_) for i, s_ in enumerate(segs)]
    gw_fg = _grad_w("gw_in_fg", hn1, dfg)
    dh0, dg1, dmeta = _rms_bwd("rms1_bwd", dhn1, h0, g1, dh1, batch, with_meta=True)

    grad_x = dh0.reshape(batch, t, d)[:, N_META:, :]
    dw_in = jnp.concatenate(gw_seg[:3] + [gw_fg[:, :n_fg]] + gw_seg[3:], axis=1)

    small_g = dict(norm_mix_gain=dg1, b_forget=db_fg[:, :n_fg], b_glu=jnp.concatenate([dbg_a, dbg_g], axis=1),
                   conv_dw_b=db_dw, conv_ln_gain=dg_ln, conv_ln_bias=db_ln, b_conv_out=dbco, norm_mlp_gain=dg2,
                   final_norm_gain=dg3)
    big = dict(w_in=_shards_from_cols(dw_in), w_attn_out=dw_ao.reshape(N_DEV, d // N_DEV, d),
               w_conv_out=dw_co.reshape(N_DEV, d // N_DEV, d), w_out=dw_o.reshape(N_DEV, d // N_DEV, d),
               w_mlp_up=dw_up, w_mlp_down=dw_dn.reshape(N_DEV, ff_blk, d), meta_tokens=_shards_from_cols(dmeta),
               conv_dw_w=_shards_from_cols(dw_dw))
    big_names = list(big)
    reduced = _exchange("scatter_grads", [(big[k], True) for k in big_names] + [(_cat_small(small_g), False)])

    grads, deltas, new_m, new_v = {}, {}, {}, {}
    for k, parts in zip(big_names, reduced[:-1]):
        shp = weights[k].shape
        w2 = lambda arr: arr.reshape(parts.shape[1:])
        res = _adamw("adamw_" + k, parts, w2(weights[k]), w2(mom_m[k]), w2(mom_v[k]))
        grads[k], deltas[k], new_m[k], new_v[k] = [r.reshape(shp) for r in res]
    res = _adamw("adamw_small", reduced[-1], _cat_small(weights), _cat_small(mom_m), _cat_small(mom_v))
    shapes = {k: weights[k].shape for k in SMALL}
    for dst, r in zip((grads, deltas, new_m, new_v), res):
        dst.update(_split_small(r, shapes))

    loss = lax.psum(loss_blk[0, 0], ("x", "y", "c"))
    return (loss, grad_x, *[grads[k] for k in names], *[deltas[k] for k in names],
            *[new_m[k] for k in names], *[new_v[k] for k in names])
```

```python
import functools

import jax
import jax.numpy as jnp
from jax import lax
from jax.experimental import pallas as pl
from jax.experimental.pallas import tpu as pltpu

F32, BF16 = jnp.float32, jnp.bfloat16
N_DEV = 8
N_META = 16
HEAD_DIM = 64
LANES = 128
CONV_W = 31
RMS_EPS = 1e-6
LN_EPS = 1e-5
ROW_TILE = 688
SEG = 1024
FG_PAD = 128
VMEM_LIMIT = 56 * 1024 * 1024
ADAM_LR, ADAM_B1, ADAM_B2, ADAM_EPS, ADAM_WD, ADAM_STEP = 0.001, 0.9, 0.999, 1e-08, 0.01, 10
NEG = -1e30

SMALL = ("norm_mix_gain", "b_forget", "b_glu", "conv_dw_b", "conv_ln_gain", "conv_ln_bias", "b_conv_out",
         "norm_mlp_gain", "final_norm_gain")
SMALL_W = {"norm_mix_gain": 1024, "b_forget": 128, "b_glu": 2048, "conv_dw_b": 1024, "conv_ln_gain": 1024,
           "conv_ln_bias": 1024, "b_conv_out": 1024, "norm_mlp_gain": 1024, "final_norm_gain": 1024}
SMALL_N = {"norm_mix_gain": 1024, "b_forget": 16, "b_glu": 2048, "conv_dw_b": 1024, "conv_ln_gain": 1024,
           "conv_ln_bias": 1024, "b_conv_out": 1024, "norm_mlp_gain": 1024, "final_norm_gain": 1024}


def _params(sem=None):
    return pltpu.CompilerParams(dimension_semantics=sem, vmem_limit_bytes=VMEM_LIMIT)


def _sigmoid(x):
    return 1.0 / (1.0 + jnp.exp(-x))


def _dot_nt(a, b):
    return lax.dot_general(a, b, (((1,), (1,)), ((), ())), preferred_element_type=F32)


def _dot_tn(a, b):
    return lax.dot_general(a, b, (((0,), (0,)), ((), ())), preferred_element_type=F32)


def _exchange(name, items):
    n = len(items)

    def body(*refs):
        srcs, dsts = refs[:n], refs[n:2 * n]
        send_sems, recv_sems, local_sems = refs[2 * n:]
        x, y, c = lax.axis_index("x"), lax.axis_index("y"), lax.axis_index("c")
        me = 4 * x + 2 * y + c
        started = []
        for a in range(n):
            own = srcs[a].at[me] if items[a][1] else srcs[a]
            cp = pltpu.make_async_copy(own, dsts[a].at[me], local_sems.at[a])
            cp.start()
            started.append(cp)
        sends, recvs = [], []
        for k in range(1, N_DEV):
            px = 1 - x if k & 4 else x
            py = 1 - y if k & 2 else y
            pc = 1 - c if k & 1 else c
            peer = 4 * px + 2 * py + pc
            for a in range(n):
                src = srcs[a].at[peer] if items[a][1] else srcs[a]
                send = pltpu.make_async_remote_copy(
                    src_ref=src, dst_ref=dsts[a].at[me], send_sem=send_sems.at[a, k - 1],
                    recv_sem=recv_sems.at[a, k - 1], device_id=(px, py, pc), device_id_type=pl.DeviceIdType.MESH)
                send.start()
                sends.append(send)
                recvs.append(pltpu.make_async_remote_copy(
                    src_ref=src, dst_ref=dsts[a].at[peer], send_sem=send_sems.at[a, k - 1],
                    recv_sem=recv_sems.at[a, k - 1], device_id=(px, py, pc), device_id_type=pl.DeviceIdType.MESH))
        for r in recvs:
            r.wait_recv()
        for s in sends:
            s.wait_send()
        for cp in started:
            cp.wait()

    out_shape = []
    for arr, per_dest in items:
        shp = arr.shape if per_dest else (N_DEV,) + arr.shape
        out_shape.append(jax.ShapeDtypeStruct(shp, arr.dtype))
    any_spec = pl.BlockSpec(memory_space=pl.ANY)
    return pl.pallas_call(
        body, name=name, out_shape=out_shape,
        in_specs=[any_spec] * n, out_specs=[any_spec] * n,
        scratch_shapes=[pltpu.SemaphoreType.DMA((n, N_DEV - 1)), pltpu.SemaphoreType.DMA((n, N_DEV - 1)),
                        pltpu.SemaphoreType.DMA((n,))],
    )(*[it[0] for it in items])


def _mm_nn(name, x, w, w_spec, n_out, tn, out_dtype, relu2=False):
    m, k = x.shape
    tm = ROW_TILE

    def body(x_ref, w_ref, *outs):
        acc = jnp.dot(x_ref[...], w_ref[...], preferred_element_type=F32)
        outs[0][...] = acc.astype(outs[0].dtype)
        if relu2:
            r = jnp.maximum(acc, 0.0)
            outs[1][...] = (r * r).astype(outs[1].dtype)

    o_spec = pl.BlockSpec((tm, tn), lambda i, j: (i, j))
    shapes = [jax.ShapeDtypeStruct((m, n_out), out_dtype)]
    if relu2:
        shapes.append(jax.ShapeDtypeStruct((m, n_out), BF16))
    res = pl.pallas_call(
        body, name=name, out_shape=shapes, grid=(m // tm, n_out // tn),
        in_specs=[pl.BlockSpec((tm, k), lambda i, j: (i, 0)), w_spec],
        out_specs=[o_spec] * len(shapes), compiler_params=_params(("parallel", "parallel")),
    )(x, w)
    return res if relu2 else res[0]


def _mm_nt(name, pairs, m, n_out, tn, out_dtype, relu_bwd_of=None):
    tm = ROW_TILE
    np_ = len(pairs)

    def body(*refs):
        acc = None
        for p in range(np_):
            d = _dot_nt(refs[2 * p][...], refs[2 * p + 1][...])
            acc = d if acc is None else acc + d
        if relu_bwd_of is not None:
            acc = acc * (2.0 * jnp.maximum(refs[2 * np_][...].astype(F32), 0.0))
        refs[-1][...] = acc.astype(refs[-1].dtype)

    o_spec = pl.BlockSpec((tm, tn), lambda i, j: (i, j))
    operands, specs = [], []
    for a, a_spec, w, w_spec in pairs:
        operands += [a, w]
        specs += [a_spec, w_spec]
    if relu_bwd_of is not None:
        operands.append(relu_bwd_of)
        specs.append(o_spec)
    return pl.pallas_call(
        body, name=name, out_shape=jax.ShapeDtypeStruct((m, n_out), out_dtype), grid=(m // tm, n_out // tn),
        in_specs=specs, out_specs=o_spec, compiler_params=_params(("parallel", "parallel")),
    )(*operands)


def _mm_tn(name, x, x_col, ta, dy, dy_col, tb, out_shape, out_spec, grid_ab):
    m = x.shape[0]
    tm = ROW_TILE
    nk = m // tm

    def body(x_ref, dy_ref, o_ref, acc_ref):
        k = pl.program_id(2)

        @pl.when(k == 0)
        def _():
            acc_ref[...] = jnp.zeros(acc_ref.shape, F32)

        acc_ref[...] += _dot_tn(x_ref[...], dy_ref[...])

        @pl.when(k == nk - 1)
        def _():
            o_ref[...] = acc_ref[...].astype(BF16)

    return pl.pallas_call(
        body, name=name, out_shape=jax.ShapeDtypeStruct(out_shape, BF16), grid=grid_ab + (nk,),
        in_specs=[pl.BlockSpec((tm, ta), lambda a, b, k: (k, x_col(a))),
                  pl.BlockSpec((tm, tb), lambda a, b, k: (k, dy_col(b)))],
        out_specs=out_spec, scratch_shapes=[pltpu.VMEM((ta, tb), F32)],
        compiler_params=_params(("parallel", "parallel", "arbitrary")),
    )(x, dy)


def _w_cols(k, tn, off_blocks):
    return pl.BlockSpec((k, tn), lambda i, j: (0, off_blocks + j))


def _a_rows(kw, col_block=0):
    return pl.BlockSpec((ROW_TILE, kw), lambda i, j: (i, col_block))


def _w_rows(tn, kw, col_block=0):
    return pl.BlockSpec((tn, kw), lambda i, j: (j, col_block))


def _grad_w(name, x, dy):
    nb = dy.shape[1]
    tb = min(nb, 1024)
    return _mm_tn(name, x, lambda a: a, 1024, dy, lambda b: b, tb, (x.shape[1], nb),
                  pl.BlockSpec((1024, tb), lambda a, b, k: (a, b)), (x.shape[1] // 1024, nb // tb))


def _row_spec(width):
    return pl.BlockSpec((ROW_TILE, width), lambda i: (i, 0))


def _vec_spec(width):
    return pl.BlockSpec((1, width), lambda i: (0, 0))


def _rms_fwd(name, h, g, res=None):
    n, d = h.shape

    def body(*refs):
        if res is None:
            h_ref, g_ref, hn_ref = refs
            hv = h_ref[...]
        else:
            h_ref, r_ref, g_ref, hs_ref, hn_ref = refs
            hv = h_ref[...] + r_ref[...]
            hs_ref[...] = hv
        r = lax.rsqrt(jnp.mean(hv * hv, axis=-1, keepdims=True) + RMS_EPS)
        hn_ref[...] = (hv * r * g_ref[...]).astype(BF16)

    ins = [h, g] if res is None else [h, res, g]
    in_specs = [_row_spec(d), _vec_spec(d)] if res is None else [_row_spec(d), _row_spec(d), _vec_spec(d)]
    hn_shape = jax.ShapeDtypeStruct((n, d), BF16)
    if res is None:
        out_shape, out_specs = hn_shape, _row_spec(d)
    else:
        out_shape, out_specs = [jax.ShapeDtypeStruct((n, d), F32), hn_shape], [_row_spec(d), _row_spec(d)]
    return pl.pallas_call(body, name=name, out_shape=out_shape, grid=(n // ROW_TILE,), in_specs=in_specs,
                          out_specs=out_specs, compiler_params=_params(("parallel",)))(*ins)


def _rms_bwd(name, dhn, h, g, dres, batch, with_bf16=False, with_meta=False):
    n, d = h.shape
    t = n // batch
    nt = t // ROW_TILE

    def body(dhn_ref, h_ref, g_ref, dres_ref, *outs):
        first = (pl.program_id(0) == 0) & (pl.program_id(1) == 0)
        hv = h_ref[...]
        r = lax.rsqrt(jnp.mean(hv * hv, axis=-1, keepdims=True) + RMS_EPS)
        nrm = hv * r
        dn = dhn_ref[...] * g_ref[...]
        dh = dres_ref[...] + r * (dn - nrm * jnp.mean(dn * nrm, axis=-1, keepdims=True))
        outs[0][...] = dh
        dg_ref = outs[1]

        @pl.when(first)
        def _():
            dg_ref[...] = jnp.zeros(dg_ref.shape, F32)

        dg_ref[...] += jnp.sum(dhn_ref[...] * nrm, axis=0, keepdims=True)
        nxt = 2
        if with_bf16:
            outs[nxt][...] = dh.astype(BF16)
            nxt += 1
        if with_meta:
            meta_ref = outs[nxt]

            @pl.when(first)
            def _():
                meta_ref[...] = jnp.zeros(meta_ref.shape, F32)

            @pl.when(pl.program_id(1) == 0)
            def _():
                meta_ref[...] += dh[0:N_META, :]

    row = pl.BlockSpec((ROW_TILE, d), lambda b, j: (b * nt + j, 0))
    vec = pl.BlockSpec((1, d), lambda b, j: (0, 0))
    shapes = [jax.ShapeDtypeStruct((n, d), F32), jax.ShapeDtypeStruct((1, d), F32)]
    specs = [row, vec]
    if with_bf16:
        shapes.append(jax.ShapeDtypeStruct((n, d), BF16))
        specs.append(row)
    if with_meta:
        shapes.append(jax.ShapeDtypeStruct((N_META, d), F32))
        specs.append(pl.BlockSpec((N_META, d), lambda b, j: (0, 0)))
    return pl.pallas_call(body, name=name, out_shape=shapes, grid=(batch, nt), in_specs=[row, row, vec, row],
                          out_specs=specs, compiler_params=_params(("arbitrary", "arbitrary")))(dhn, h, g, dres)


def _final(name, h1, dn, tgt, g, batch):
    n, d = h1.shape
    t = n // batch
    nt = t // ROW_TILE

    def body(h1_ref, dn_ref, tgt_ref, g_ref, dh_ref, dhb_ref, loss_ref, dg_ref):
        first = (pl.program_id(0) == 0) & (pl.program_id(1) == 0)
        hv = h1_ref[...] + dn_ref[...]
        r = lax.rsqrt(jnp.mean(hv * hv, axis=-1, keepdims=True) + RMS_EPS)
        nrm = hv * r
        gv = g_ref[...]
        pos = pl.program_id(1) * ROW_TILE + lax.broadcasted_iota(jnp.int32, (ROW_TILE, 1), 0)
        diff = jnp.where(pos >= N_META, nrm * gv - tgt_ref[...], 0.0)
        dy = diff * (1.0 / d)

        @pl.when(first)
        def _():
            loss_ref[...] = jnp.zeros(loss_ref.shape, F32)
            dg_ref[...] = jnp.zeros(dg_ref.shape, F32)

        loss_ref[...] += jnp.full(loss_ref.shape, 0.5 / d, F32) * jnp.sum(diff * diff)
        dg_ref[...] += jnp.sum(dy * nrm, axis=0, keepdims=True)
        dng = dy * gv
        dh = r * (dng - nrm * jnp.mean(dng * nrm, axis=-1, keepdims=True))
        dh_ref[...] = dh
        dhb_ref[...] = dh.astype(BF16)

    row = pl.BlockSpec((ROW_TILE, d), lambda b, j: (b * nt + j, 0))
    vec = pl.BlockSpec((1, d), lambda b, j: (0, 0))
    return pl.pallas_call(
        body, name=name, grid=(batch, nt), in_specs=[row, row, row, vec],
        out_shape=[jax.ShapeDtypeStruct((n, d), F32), jax.ShapeDtypeStruct((n, d), BF16),
                   jax.ShapeDtypeStruct((8, 128), F32), jax.ShapeDtypeStruct((1, d), F32)],
        out_specs=[row, row, pl.BlockSpec((8, 128), lambda b, j: (0, 0)), vec],
        compiler_params=_params(("arbitrary", "arbitrary")))(h1, dn, tgt, g)


def _ln_silu_fwd(name, c1, g, b):
    n, d = c1.shape

    def body(c_ref, g_ref, b_ref, o_ref):
        xv = c_ref[...]
        xc = xv - jnp.mean(xv, axis=-1, keepdims=True)
        rstd = lax.rsqrt(jnp.mean(xc * xc, axis=-1, keepdims=True) + LN_EPS)
        c2 = xc * rstd * g_ref[...] + b_ref[...]
        o_ref[...] = (c2 * _sigmoid(c2)).astype(BF16)

    return pl.pallas_call(body, name=name, out_shape=jax.ShapeDtypeStruct((n, d), BF16), grid=(n // ROW_TILE,),
                          in_specs=[_row_spec(d), _vec_spec(d), _vec_spec(d)], out_specs=_row_spec(d),
                          compiler_params=_params(("parallel",)))(c1, g, b)


def _ln_silu_bwd(name, dc3, c1, g, b):
    n, d = c1.shape

    def body(d_ref, c_ref, g_ref, b_ref, dc1_ref, dg_ref, db_ref):
        xv = c_ref[...]
        xc = xv - jnp.mean(xv, axis=-1, keepdims=True)
        rstd = lax.rsqrt(jnp.mean(xc * xc, axis=-1, keepdims=True) + LN_EPS)
        xh = xc * rstd
        c2 = xh * g_ref[...] + b_ref[...]
        s = _sigmoid(c2)
        dc2 = d_ref[...] * (s * (1.0 + c2 * (1.0 - s)))

        @pl.when(pl.program_id(0) == 0)
        def _():
            dg_ref[...] = jnp.zeros(dg_ref.shape, F32)
            db_ref[...] = jnp.zeros(db_ref.shape, F32)

        dg_ref[...] += jnp.sum(dc2 * xh, axis=0, keepdims=True)
        db_ref[...] += jnp.sum(dc2, axis=0, keepdims=True)
        dxh = dc2 * g_ref[...]
        dc1_ref[...] = rstd * (dxh - jnp.mean(dxh, axis=-1, keepdims=True)
                               - xh * jnp.mean(dxh * xh, axis=-1, keepdims=True))

    return pl.pallas_call(
        body, name=name, grid=(n // ROW_TILE,),
        out_shape=[jax.ShapeDtypeStruct((n, d), F32), jax.ShapeDtypeStruct((1, d), F32),
                   jax.ShapeDtypeStruct((1, d), F32)],
        in_specs=[_row_spec(d), _row_spec(d), _vec_spec(d), _vec_spec(d)],
        out_specs=[_row_spec(d), _vec_spec(d), _vec_spec(d)],
        compiler_params=_params(("arbitrary",)))(dc3, c1, g, b)


MERGE_TC = 512


def _merge_fwd(name, gates, a, c, b_co):
    n, d = a.shape
    nc = d // MERGE_TC

    def body(ga_ref, gc_ref, a_ref, c_ref, b_ref, m_ref):
        m = _sigmoid(ga_ref[...]) * a_ref[...] + _sigmoid(gc_ref[...]) * (c_ref[...] + b_ref[...])
        m_ref[...] = m.astype(BF16)

    blk = lambda off: pl.BlockSpec((ROW_TILE, MERGE_TC), lambda i, j: (i, off + j))
    return pl.pallas_call(
        body, name=name, out_shape=jax.ShapeDtypeStruct((n, d), BF16), grid=(n // ROW_TILE, nc),
        in_specs=[blk(0), blk(nc), blk(0), blk(0), pl.BlockSpec((1, MERGE_TC), lambda i, j: (0, j))],
        out_specs=blk(0), compiler_params=_params(("parallel", "parallel")))(gates, gates, a, c, b_co)


def _merge_bwd(name, dm, gates, a, c, b_co):
    n, d = a.shape
    nc = d // MERGE_TC

    def body(dm_ref, ga_ref, gc_ref, a_ref, c_ref, b_ref, da_ref, dc_ref, dga_ref, dgc_ref, dbco_ref):
        dmv = dm_ref[...]
        sa, sc = _sigmoid(ga_ref[...]), _sigmoid(gc_ref[...])
        dc = dmv * sc
        da_ref[...] = (dmv * sa).astype(BF16)
        dc_ref[...] = dc.astype(BF16)
        dga_ref[...] = (dmv * a_ref[...] * sa * (1.0 - sa)).astype(BF16)
        dgc_ref[...] = (dmv * (c_ref[...] + b_ref[...]) * sc * (1.0 - sc)).astype(BF16)

        @pl.when(pl.program_id(1) == 0)
        def _():
            dbco_ref[...] = jnp.zeros(dbco_ref.shape, F32)

        dbco_ref[...] += jnp.sum(dc, axis=0, keepdims=True)

    blk = lambda off: pl.BlockSpec((ROW_TILE, MERGE_TC), lambda j, i: (i, off + j))
    vec = pl.BlockSpec((1, MERGE_TC), lambda j, i: (0, j))
    act = jax.ShapeDtypeStruct((n, d), BF16)
    return pl.pallas_call(
        body, name=name, grid=(nc, n // ROW_TILE),
        out_shape=[act, act, act, act, jax.ShapeDtypeStruct((1, d), F32)],
        in_specs=[blk(0), blk(0), blk(nc), blk(0), blk(0), vec],
        out_specs=[blk(0), blk(0), blk(0), blk(0), vec],
        compiler_params=_params(("parallel", "arbitrary")))(dm, gates, gates, a, c, b_co)


CONV_TC = 128
CONV_HALO = 32


def _glu_conv_fwd(name, glu, b_glu, w_dw, b_dw, batch):
    n, c2 = glu.shape
    c = c2 // 2
    t = n // batch
    nc = c // CONV_TC

    def body(a_ref, gt_ref, ba_ref, bg_ref, w_ref, bdw_ref, o_ref, pad_ref):
        u = (a_ref[...] + ba_ref[...]) * _sigmoid(gt_ref[...] + bg_ref[...])
        pad_ref[0:CONV_HALO, :] = jnp.zeros((CONV_HALO, CONV_TC), F32)
        pad_ref[CONV_HALO:CONV_HALO + t, :] = u
        acc = jnp.zeros((t, CONV_TC), F32) + bdw_ref[...]
        for j in range(CONV_W):
            off = CONV_HALO - (CONV_W - 1) + j
            acc = acc + w_ref[j:j + 1, :] * pad_ref[off:off + t, :]
        o_ref[...] = acc

    seq = lambda off: pl.BlockSpec((t, CONV_TC), lambda b, j: (b, off + j))
    vec = lambda off: pl.BlockSpec((1, CONV_TC), lambda b, j: (0, off + j))
    return pl.pallas_call(
        body, name=name, out_shape=jax.ShapeDtypeStruct((n, c), F32), grid=(batch, nc),
        in_specs=[seq(0), seq(nc), vec(0), vec(nc), pl.BlockSpec((CONV_W, CONV_TC), lambda b, j: (0, j)), vec(0)],
        out_specs=seq(0), scratch_shapes=[pltpu.VMEM((t + CONV_HALO, CONV_TC), F32)],
        compiler_params=_params(("parallel", "parallel")))(glu, glu, b_glu, b_glu, w_dw, b_dw)


def _glu_conv_bwd(name, dc1, glu, b_glu, w_dw, batch):
    n, c2 = glu.shape
    c = c2 // 2
    t = n // batch
    nc = c // CONV_TC

    def body(d_ref, a_ref, gt_ref, ba_ref, bg_ref, w_ref, dga_ref, dgg_ref, dw_ref, dbdw_ref, dba_ref, dbg_ref,
             padu_ref, padd_ref):
        av = a_ref[...] + ba_ref[...]
        sg = _sigmoid(gt_ref[...] + bg_ref[...])
        dc = d_ref[...]
        padu_ref[0:CONV_HALO, :] = jnp.zeros((CONV_HALO, CONV_TC), F32)
        padu_ref[CONV_HALO:CONV_HALO + t, :] = av * sg
        padd_ref[0:t, :] = dc
        padd_ref[t:t + CONV_HALO, :] = jnp.zeros((CONV_HALO, CONV_TC), F32)

        @pl.when(pl.program_id(1) == 0)
        def _():
            dw_ref[...] = jnp.zeros(dw_ref.shape, F32)
            dbdw_ref[...] = jnp.zeros(dbdw_ref.shape, F32)
            dba_ref[...] = jnp.zeros(dba_ref.shape, F32)
            dbg_ref[...] = jnp.zeros(dbg_ref.shape, F32)

        du = jnp.zeros((t, CONV_TC), F32)
        for j in range(CONV_W):
            back = CONV_W - 1 - j
            du = du + w_ref[j:j + 1, :] * padd_ref[back:back + t, :]
            off = CONV_HALO - (CONV_W - 1) + j
            dw_ref[j:j + 1, :] += jnp.sum(dc * padu_ref[off:off + t, :], axis=0, keepdims=True)
        dga = du * sg
        dgg = du * av * sg * (1.0 - sg)
        dga_ref[...] = dga.astype(BF16)
        dgg_ref[...] = dgg.astype(BF16)
        dbdw_ref[...] += jnp.sum(dc, axis=0, keepdims=True)
        dba_ref[...] += jnp.sum(dga, axis=0, keepdims=True)
        dbg_ref[...] += jnp.sum(dgg, axis=0, keepdims=True)

    seq = lambda off: pl.BlockSpec((t, CONV_TC), lambda j, b: (b, off + j))
    vec = lambda off: pl.BlockSpec((1, CONV_TC), lambda j, b: (0, off + j))
    wsp = pl.BlockSpec((CONV_W, CONV_TC), lambda j, b: (0, j))
    act = jax.ShapeDtypeStruct((n, c), BF16)
    v = jax.ShapeDtypeStruct((1, c), F32)
    return pl.pallas_call(
        body, name=name, grid=(nc, batch),
        out_shape=[act, act, jax.ShapeDtypeStruct((CONV_W, c), F32), v, v, v],
        in_specs=[seq(0), seq(0), seq(nc), vec(0), vec(nc), wsp],
        out_specs=[seq(0), seq(0), wsp, vec(0), vec(0), vec(0)],
        scratch_shapes=[pltpu.VMEM((t + CONV_HALO, CONV_TC), F32), pltpu.VMEM((t + CONV_HALO, CONV_TC), F32)],
        compiler_params=_params(("parallel", "arbitrary")))(dc1, glu, glu, b_glu, b_glu, w_dw)


def _split3(x):
    hi = x.astype(BF16)
    r = x - hi.astype(F32)
    mid = r.astype(BF16)
    lo = (r - mid.astype(F32)).astype(BF16)
    return hi, mid, lo


def _tri_matmul(tri, x):
    hi, mid, lo = _split3(x)
    dot = lambda v: jnp.dot(tri, v, preferred_element_type=F32)
    return dot(hi) + dot(mid) + dot(lo)


def _fox_prep_fwd(name, fg, b_fg, batch):
    n, w = fg.shape
    t = n // batch
    nq = t // ROW_TILE

    def body(fg_ref, b_ref, cum_ref):
        row = lax.broadcasted_iota(jnp.int32, (ROW_TILE, ROW_TILE), 0)
        col = lax.broadcasted_iota(jnp.int32, (ROW_TILE, ROW_TILE), 1)
        tri = (row >= col).astype(BF16)
        for k in range(nq):
            rows = slice(k * ROW_TILE, (k + 1) * ROW_TILE)
            z = fg_ref[rows, :] + b_ref[...]
            logf = jnp.minimum(z, 0.0) - jnp.log(1.0 + jnp.exp(-jnp.abs(z)))
            cum = _tri_matmul(tri, logf)
            if k > 0:
                cum = cum + cum_ref[k * ROW_TILE - 1:k * ROW_TILE, :]
            cum_ref[rows, :] = cum

    seq = pl.BlockSpec((t, w), lambda b: (b, 0))
    return pl.pallas_call(body, name=name, out_shape=jax.ShapeDtypeStruct((n, w), F32), grid=(batch,),
                          in_specs=[seq, pl.BlockSpec((1, w), lambda b: (0, 0))], out_specs=seq,
                          compiler_params=_params(("parallel",)))(fg, b_fg)


def _fox_prep_bwd(name, dcum_k, dcum_q, fg, b_fg, batch):
    n, w = fg.shape
    t = n // batch
    nq = t // ROW_TILE

    def body(dk_ref, dq_ref, fg_ref, b_ref, dfg_ref, db_ref, rev_ref):
        row = lax.broadcasted_iota(jnp.int32, (ROW_TILE, ROW_TILE), 0)
        col = lax.broadcasted_iota(jnp.int32, (ROW_TILE, ROW_TILE), 1)
        tri = (col >= row).astype(BF16)

        @pl.when(pl.program_id(0) == 0)
        def _():
            db_ref[...] = jnp.zeros(db_ref.shape, F32)

        for k in reversed(range(nq)):
            rows = slice(k * ROW_TILE, (k + 1) * ROW_TILE)
            dlog = _tri_matmul(tri, dk_ref[rows, :] + dq_ref[rows, :])
            if k < nq - 1:
                dlog = dlog + rev_ref[(k + 1) * ROW_TILE:(k + 1) * ROW_TILE + 1, :]
            rev_ref[rows, :] = dlog
            dfg = dlog * _sigmoid(-(fg_ref[rows, :] + b_ref[...]))
            dfg_ref[rows, :] = dfg.astype(BF16)
            db_ref[...] += jnp.sum(dfg, axis=0, keepdims=True)

    seq = pl.BlockSpec((t, w), lambda b: (b, 0))
    vec = pl.BlockSpec((1, w), lambda b: (0, 0))
    return pl.pallas_call(
        body, name=name, grid=(batch,),
        out_shape=[jax.ShapeDtypeStruct((n, w), BF16), jax.ShapeDtypeStruct((1, w), F32)],
        in_specs=[seq, seq, seq, vec], out_specs=[seq, vec], scratch_shapes=[pltpu.VMEM((t, w), F32)],
        compiler_params=_params(("arbitrary",)))(dcum_k, dcum_q, fg, b_fg)


def _head_masks(x):
    lane = lax.broadcasted_iota(jnp.int32, x.shape, 1)
    zero = jnp.zeros(x.shape, x.dtype)
    return jnp.where(lane < HEAD_DIM, x, zero), jnp.where(lane >= HEAD_DIM, x, zero)


def _attn_specs(t, nq):
    qkv = lambda off: pl.BlockSpec((t, LANES), lambda b, h: (b, off + h))
    cumr = pl.BlockSpec((None, None, nq, 8, ROW_TILE), lambda b, h: (b, h, 0, 0, 0))
    return qkv, cumr


def _attn_fwd(name, qkv, cumr, batch):
    n, w3 = qkv.shape
    w = w3 // 3
    t = n // batch
    nq = t // ROW_TILE
    n_pairs = w // LANES
    tq = ROW_TILE

    def body(q_ref, k_ref, v_ref, cr_ref, o_ref, lse_ref):
        row = lax.broadcasted_iota(jnp.int32, (tq, tq), 0)
        col = lax.broadcasted_iota(jnp.int32, (tq, tq), 1)
        causal = row >= col
        lane = lax.broadcasted_iota(jnp.int32, (tq, LANES), 1)
        for i in range(nq):
            rows = slice(i * tq, (i + 1) * tq)
            qs = _head_masks(q_ref[rows, :] * 0.125)
            outs, lses = [], []
            for hh in range(2):
                m = jnp.full((tq, 1), NEG, F32)
                l = jnp.zeros((tq, 1), F32)
                acc = jnp.zeros((tq, LANES), F32)
                for j in range(i + 1):
                    cols = slice(j * tq, (j + 1) * tq)
                    s = _dot_nt(qs[hh], k_ref[cols, :]) - cr_ref[j, hh:hh + 1, :]
                    if j == i:
                        s = jnp.where(causal, s, NEG)
                    m_new = jnp.maximum(m, jnp.max(s, axis=1, keepdims=True))
                    alpha = jnp.exp(m - m_new)
                    p = jnp.exp(s - m_new)
                    l = alpha * l + jnp.sum(p, axis=1, keepdims=True)
                    acc = alpha * acc + jnp.dot(p.astype(BF16), v_ref[cols, :], preferred_element_type=F32)
                    m = m_new
                outs.append(acc / l)
                lses.append(m + jnp.log(l))
            o_ref[rows, :] = jnp.where(lane < HEAD_DIM, outs[0], outs[1]).astype(BF16)
            lse_ref[rows, :] = jnp.where(lane < HEAD_DIM, lses[0], lses[1])

    qkv_spec, cumr_spec = _attn_specs(t, nq)
    return pl.pallas_call(
        body, name=name, grid=(batch, n_pairs),
        out_shape=[jax.ShapeDtypeStruct((n, w), BF16), jax.ShapeDtypeStruct((n, w), F32)],
        in_specs=[qkv_spec(0), qkv_spec(n_pairs), qkv_spec(2 * n_pairs), cumr_spec],
        out_specs=[qkv_spec(0), qkv_spec(0)],
        compiler_params=_params(("parallel", "parallel")))(qkv, qkv, qkv, cumr)


def _attn_bwd(name, qkv, o, do, lse, cumr, batch):
    n, w3 = qkv.shape
    w = w3 // 3
    t = n // batch
    nq = t // ROW_TILE
    n_pairs = w // LANES
    tq = ROW_TILE

    def body(q_ref, k_ref, v_ref, o_ref, do_ref, lse_ref, cr_ref, dq_ref, dk_ref, dv_ref, dcr_ref, dcq_ref,
             dk_acc, dv_acc):
        pair = pl.program_id(1)
        row = lax.broadcasted_iota(jnp.int32, (tq, tq), 0)
        col = lax.broadcasted_iota(jnp.int32, (tq, tq), 1)
        causal = row >= col
        lane = lax.broadcasted_iota(jnp.int32, (tq, LANES), 1)
        dk_acc[...] = jnp.zeros(dk_acc.shape, F32)
        dv_acc[...] = jnp.zeros(dv_acc.shape, F32)
        dcr_ref[...] = jnp.zeros(dcr_ref.shape, F32)

        @pl.when(pair == 0)
        def _():
            dcq_ref[...] = jnp.zeros(dcq_ref.shape, F32)

        for i in range(nq):
            rows = slice(i * tq, (i + 1) * tq)
            qs = _head_masks(q_ref[rows, :] * 0.125)
            dos = _head_masks(do_ref[rows, :])
            dq = jnp.zeros((tq, LANES), F32)
            dcq = []
            for hh in range(2):
                row_sum = jnp.zeros((tq, 1), F32)
                lse = lse_ref[rows, hh * HEAD_DIM:hh * HEAD_DIM + 1]
                delta = jnp.sum(dos[hh].astype(F32) * o_ref[rows, :].astype(F32), axis=1, keepdims=True)
                for j in range(i + 1):
                    cols = slice(j * tq, (j + 1) * tq)
                    s = _dot_nt(qs[hh], k_ref[cols, :]) - cr_ref[j, hh:hh + 1, :]
                    p = jnp.exp(s - lse)
                    if j == i:
                        p = jnp.where(causal, p, 0.0)
                    dp = _dot_nt(dos[hh], v_ref[cols, :])
                    ds = p * (dp - delta)
                    pb, dsb = p.astype(BF16), ds.astype(BF16)
                    km = _head_masks(k_ref[cols, :])[hh]
                    dv_acc[cols, :] += _dot_tn(pb, dos[hh])
                    dk_acc[cols, :] += _dot_tn(dsb, qs[hh])
                    dq = dq + jnp.dot(dsb, km, preferred_element_type=F32)
                    dcr_ref[j, hh:hh + 1, :] -= jnp.sum(ds, axis=0, keepdims=True)
                    row_sum = row_sum + jnp.sum(ds, axis=1, keepdims=True)
                dcq.append(row_sum)
            dq_ref[rows, :] = (dq * 0.125).astype(BF16)
            dcq_ref[rows, :] = jnp.where(lane == 2 * pair, dcq[0],
                                         jnp.where(lane == 2 * pair + 1, dcq[1], dcq_ref[rows, :]))
        dk_ref[...] = dk_acc[...].astype(BF16)
        dv_ref[...] = dv_acc[...].astype(BF16)

    qkv_spec, cumr_spec = _attn_specs(t, nq)
    act = jax.ShapeDtypeStruct((n, w), BF16)
    return pl.pallas_call(
        body, name=name, grid=(batch, n_pairs),
        out_shape=[act, act, act, jax.ShapeDtypeStruct(cumr.shape, F32), jax.ShapeDtypeStruct((n, LANES), F32)],
        in_specs=[qkv_spec(0), qkv_spec(n_pairs), qkv_spec(2 * n_pairs), qkv_spec(0), qkv_spec(0), qkv_spec(0),
                  cumr_spec],
        out_specs=[qkv_spec(0), qkv_spec(0), qkv_spec(0), cumr_spec, pl.BlockSpec((t, LANES), lambda b, h: (b, 0))],
        scratch_shapes=[pltpu.VMEM((t, LANES), F32), pltpu.VMEM((t, LANES), F32)],
        compiler_params=_params(("parallel", "arbitrary")))(qkv, qkv, qkv, o, do, lse, cumr)


def _adamw(name, parts, w, m, v):
    r, c = w.shape
    tr = 128 if r % 128 == 0 else r
    c1 = 1.0 - ADAM_B1 ** ADAM_STEP
    c2 = 1.0 - ADAM_B2 ** ADAM_STEP

    def body(p_ref, w_ref, m_ref, v_ref, g_ref, d_ref, m2_ref, v2_ref):
        g = p_ref[0].astype(F32)
        for s in range(1, N_DEV):
            g = g + p_ref[s].astype(F32)
        m2 = ADAM_B1 * m_ref[...] + (1.0 - ADAM_B1) * g
        v2 = ADAM_B2 * v_ref[...] + (1.0 - ADAM_B2) * (g * g)
        g_ref[...] = g
        m2_ref[...] = m2
        v2_ref[...] = v2
        d_ref[...] = -ADAM_LR * ((m2 / c1) / (jnp.sqrt(v2 / c2) + ADAM_EPS) + ADAM_WD * w_ref[...])

    blk = pl.BlockSpec((tr, c), lambda i: (i, 0))
    shp = jax.ShapeDtypeStruct((r, c), F32)
    return pl.pallas_call(
        body, name=name, out_shape=[shp] * 4, grid=(r // tr,),
        in_specs=[pl.BlockSpec((N_DEV, tr, c), lambda i: (0, i, 0)), blk, blk, blk], out_specs=[blk] * 4,
        compiler_params=_params(("parallel",)))(parts, w, m, v)


def _cat_small(vals):
    parts = []
    for name in SMALL:
        v = vals[name].reshape(1, -1).astype(F32)
        parts.append(jnp.pad(v, ((0, 0), (0, SMALL_W[name] - v.shape[1]))))
    return jnp.concatenate(parts, axis=1)


def _split_small(row, shapes):
    out, off = {}, 0
    for name in SMALL:
        out[name] = row[0, off:off + SMALL_N[name]].reshape(shapes[name])
        off += SMALL_W[name]
    return out


def _w_in_pieces(lo, hi, n_fg):
    o_fg = 3 * SEG
    out = []
    c = lo
    while c < hi:
        if c < o_fg:
            src, base, end = c // SEG, (c // SEG) * SEG, (c // SEG + 1) * SEG
        elif c < o_fg + n_fg:
            src, base, end = 7, o_fg, o_fg + n_fg
        else:
            k = (c - o_fg - n_fg) // SEG
            src, base, end = 3 + k, o_fg + n_fg + k * SEG, o_fg + n_fg + (k + 1) * SEG
        stop = min(hi, end)
        out.append((src, c - base, stop - base, c))
        c = stop
    return out


def _cols_from_shards(g):
    return jnp.transpose(g, (1, 0, 2)).reshape(g.shape[1], N_DEV * g.shape[2])


def _shards_from_cols(a):
    r, c = a.shape
    return jnp.transpose(a.reshape(r, N_DEV, c // N_DEV), (1, 0, 2))


def kernel(x, meta_tokens, norm_mix_gain, w_in, b_forget, w_attn_out, b_glu, conv_dw_w, conv_dw_b, conv_ln_gain, conv_ln_bias, w_conv_out, b_conv_out, w_out, norm_mlp_gain, w_mlp_up, w_mlp_down, final_norm_gain, loss_target, m_meta_tokens, m_norm_mix_gain, m_w_in, m_b_forget, m_w_attn_out, m_b_glu, m_conv_dw_w, m_conv_dw_b, m_conv_ln_gain, m_conv_ln_bias, m_w_conv_out, m_b_conv_out, m_w_out, m_norm_mlp_gain, m_w_mlp_up, m_w_mlp_down, m_final_norm_gain, v_meta_tokens, v_norm_mix_gain, v_w_in, v_b_forget, v_w_attn_out, v_b_glu, v_conv_dw_w, v_conv_dw_b, v_conv_ln_gain, v_conv_ln_bias, v_w_conv_out, v_b_conv_out, v_w_out, v_norm_mlp_gain, v_w_mlp_up, v_w_mlp_down, v_final_norm_gain):
    weights = dict(meta_tokens=meta_tokens, norm_mix_gain=norm_mix_gain, w_in=w_in, b_forget=b_forget, w_attn_out=w_attn_out, b_glu=b_glu, conv_dw_w=conv_dw_w, conv_dw_b=conv_dw_b, conv_ln_gain=conv_ln_gain, conv_ln_bias=conv_ln_bias, w_conv_out=w_conv_out, b_conv_out=b_conv_out, w_out=w_out, norm_mlp_gain=norm_mlp_gain, w_mlp_up=w_mlp_up, w_mlp_down=w_mlp_down, final_norm_gain=final_norm_gain)
    mom_m = dict(meta_tokens=m_meta_tokens, norm_mix_gain=m_norm_mix_gain, w_in=m_w_in, b_forget=m_b_forget, w_attn_out=m_w_attn_out, b_glu=m_b_glu, conv_dw_w=m_conv_dw_w, conv_dw_b=m_conv_dw_b, conv_ln_gain=m_conv_ln_gain, conv_ln_bias=m_conv_ln_bias, w_conv_out=m_w_conv_out, b_conv_out=m_b_conv_out, w_out=m_w_out, norm_mlp_gain=m_norm_mlp_gain, w_mlp_up=m_w_mlp_up, w_mlp_down=m_w_mlp_down, final_norm_gain=m_final_norm_gain)
    mom_v = dict(meta_tokens=v_meta_tokens, norm_mix_gain=v_norm_mix_gain, w_in=v_w_in, b_forget=v_b_forget, w_attn_out=v_w_attn_out, b_glu=v_b_glu, conv_dw_w=v_conv_dw_w, conv_dw_b=v_conv_dw_b, conv_ln_gain=v_conv_ln_gain, conv_ln_bias=v_conv_ln_bias, w_conv_out=v_w_conv_out, b_conv_out=v_b_conv_out, w_out=v_w_out, norm_mlp_gain=v_norm_mlp_gain, w_mlp_up=v_w_mlp_up, w_mlp_down=v_w_mlp_down, final_norm_gain=v_final_norm_gain)
    names = list(weights)
    batch, seq, d = x.shape
    t = seq + N_META
    n = batch * t
    nq = t // ROW_TILE
    n_pairs = d // LANES
    assert t % ROW_TILE == 0 and d == SEG

    gathered = _exchange("gather_weights", [
        (w_in[0].astype(BF16), False), (w_attn_out[0].astype(BF16), False), (w_conv_out[0].astype(BF16), False),
        (w_out[0].astype(BF16), False), (w_mlp_up[0].astype(BF16), False), (w_mlp_down[0].astype(BF16), False),
        (meta_tokens, False), (conv_dw_w[0], False)])
    n_fg = b_forget.shape[1]
    shard_w = w_in.shape[2]
    seg_cols = [[] for _ in range(8)]
    for p in range(N_DEV):
        for src, c0, c1, orig in _w_in_pieces(p * shard_w, (p + 1) * shard_w, n_fg):
            seg_cols[src].append(gathered[0][p][:, orig - p * shard_w:orig - p * shard_w + c1 - c0])
    w_pad = jnp.concatenate([c_ for src in range(8) for c_ in seg_cols[src]]
                            + [jnp.zeros((d, FG_PAD - n_fg), BF16)], axis=1)
    w_ao = gathered[1].reshape(d, d)
    w_co = gathered[2].reshape(d, d)
    w_o = gathered[3].reshape(d, d)
    w_up = gathered[4]
    w_dn = gathered[5].reshape(-1, d)
    d_ff = w_dn.shape[0]
    ff_blk = d_ff // N_DEV
    meta_f = _cols_from_shards(gathered[6])
    w_dw = _cols_from_shards(gathered[7])

    row2 = lambda v: v.reshape(1, -1)
    g1, g2, g3 = row2(norm_mix_gain), row2(norm_mlp_gain), row2(final_norm_gain)
    b_fg = jnp.pad(b_forget, ((0, 0), (0, FG_PAD - n_fg)))
    h0 = jnp.concatenate([jnp.broadcast_to(meta_f[None], (batch, N_META, d)), x], axis=1).reshape(n, d)
    tgt = jnp.concatenate([jnp.zeros((batch, N_META, d), F32), loss_target], axis=1).reshape(n, d)

    hn1 = _rms_fwd("rms1", h0, g1)
    qkv = _mm_nn("proj_qkv", hn1, w_pad, _w_cols(d, 512, 0), 3 * SEG, 512, BF16)
    glu = _mm_nn("proj_glu", hn1, w_pad, _w_cols(d, 512, 3 * SEG // 512), 2 * SEG, 512, F32)
    gates = _mm_nn("proj_gates", hn1, w_pad, _w_cols(d, 512, 5 * SEG // 512), 2 * SEG, 512, F32)
    fg = _mm_nn("proj_fg", hn1, w_pad, _w_cols(d, FG_PAD, 7 * SEG // FG_PAD), FG_PAD, FG_PAD, F32)

    cum = _fox_prep_fwd("fox_cumsum", fg, b_fg, batch)
    cum_h = cum.reshape(batch, t, FG_PAD)[:, :, :2 * n_pairs].reshape(batch, t, n_pairs, 2)
    cumr = jnp.transpose(cum_h.reshape(batch, nq, ROW_TILE, n_pairs, 2), (0, 3, 1, 4, 2))
    cumr = jnp.pad(cumr, ((0, 0), (0, 0), (0, 0), (0, 6), (0, 0)))
    o, lse = _attn_fwd("attn_fwd", qkv, cumr, batch)
    a = _mm_nn("attn_out", o, w_ao, _w_cols(d, 512, 0), d, 512, F32)

    c1 = _glu_conv_fwd("glu_conv", glu, b_glu, w_dw, conv_dw_b, batch)
    c3 = _ln_silu_fwd("ln_silu", c1, conv_ln_gain, conv_ln_bias)
    c = _mm_nn("conv_out", c3, w_co, _w_cols(d, 512, 0), d, 512, F32)

    mrg = _merge_fwd("merge", gates, a, c, b_conv_out)
    mo = _mm_nn("mix_out", mrg, w_o, _w_cols(d, 512, 0), d, 512, F32)
    h1, hn2 = _rms_fwd("resid_rms2", h0, g2, res=mo)
    per = ff_blk // 512
    up, act = _mm_nn("mlp_up", hn2, w_up, pl.BlockSpec((None, d, 512), lambda i, j: (j // per, 0, j % per)),
                     d_ff, 512, BF16, relu2=True)
    dn = _mm_nn("mlp_down", act, w_dn, _w_cols(d_ff, 512, 0), d, 512, F32)
    dh2, dh2b, loss_blk, dg3 = _final("final_loss", h1, dn, tgt, g3, batch)

    dup = _mm_nt("d_mlp_down", [(dh2b, _a_rows(d), w_dn, _w_rows(512, d))], n, d_ff, 512, BF16, relu_bwd_of=up)
    dw_dn = _grad_w("gw_mlp_down", act, dh2b)
    dhn2 = _mm_nt("d_mlp_up", [(dup, _a_rows(ff_blk, g), w_up, pl.BlockSpec((None, 256, ff_blk), lambda i, j, g=g: (g, j, 0)))
                               for g in range(N_DEV)], n, d, 256, F32)
    dw_up = _mm_tn("gw_mlp_up", hn2, lambda a_: 0, d, dup, lambda b_: b_, ff_blk, (N_DEV, d, ff_blk),
                   pl.BlockSpec((None, d, ff_blk), lambda a_, b_, k: (b_, 0, 0)), (1, N_DEV))
    dh1, dg2, dh1b = _rms_bwd("rms2_bwd", dhn2, h1, g2, dh2, batch, with_bf16=True)

    dm = _mm_nt("d_mix_out", [(dh1b, _a_rows(d), w_o, _w_rows(512, d))], n, d, 512, F32)
    dw_o = _grad_w("gw_mix_out", mrg, dh1b)
    da, dc, dga, dgc, dbco = _merge_bwd("merge_bwd", dm, gates, a, c, b_conv_out)

    do = _mm_nt("d_attn_out", [(da, _a_rows(d), w_ao, _w_rows(512, d))], n, d, 512, BF16)
    dw_ao = _grad_w("gw_attn_out", o, da)
    dc3 = _mm_nt("d_conv_out", [(dc, _a_rows(d), w_co, _w_rows(512, d))], n, d, 512, F32)
    dw_co = _grad_w("gw_conv_out", c3, dc)

    dc1, dg_ln, db_ln = _ln_silu_bwd("ln_silu_bwd", dc3, c1, conv_ln_gain, conv_ln_bias)
    dglu_a, dglu_g, dw_dw, db_dw, dbg_a, dbg_g = _glu_conv_bwd("glu_conv_bwd", dc1, glu, b_glu, w_dw, batch)

    dq, dk, dv, dcumr, dcum_q = _attn_bwd("attn_bwd", qkv, o, do, lse, cumr, batch)
    dcum_k = jnp.transpose(dcumr[:, :, :, :2, :], (0, 2, 4, 1, 3)).reshape(n, 2 * n_pairs)
    dcum_k = jnp.pad(dcum_k, ((0, 0), (0, FG_PAD - 2 * n_pairs)))
    dfg, db_fg = _fox_prep_bwd("fox_cumsum_bwd", dcum_k, dcum_q, fg, b_fg, batch)

    segs = [dq, dk, dv, dglu_a, dglu_g, dga, dgc]
    pairs = [(s_, _a_rows(SEG), w_pad, _w_rows(256, SEG, i)) for i, s_ in enumerate(segs)]
    pairs.append((dfg, _a_rows(FG_PAD), w_pad, _w_rows(256, FG_PAD, 7 * SEG // FG_PAD)))
    dhn1 = _mm_nt("d_proj_in", pairs, n, d, 256, F32)
    gw_seg = [_grad_w("gw_in_%d" % i, hn1, s_) for i, s_ in enumerate(segs)]
    gw_fg = _grad_w("gw_in_fg", hn1, dfg)
    dh0, dg1, dmeta = _rms_bwd("rms1_bwd", dhn1, h0, g1, dh1, batch, with_meta=True)

    grad_x = dh0.reshape(batch, t, d)[:, N_META:, :]
    gw_src = gw_seg + [gw_fg]
    dw_in = jnp.stack([jnp.concatenate([gw_src[src][:, c0:c1]
                                        for src, c0, c1, _ in _w_in_pieces(p * shard_w, (p + 1) * shard_w, n_fg)], axis=1)
                       for p in range(N_DEV)])

    small_g = dict(norm_mix_gain=dg1, b_forget=db_fg[:, :n_fg], b_glu=jnp.concatenate([dbg_a, dbg_g], axis=1),
                   conv_dw_b=db_dw, conv_ln_gain=dg_ln, conv_ln_bias=db_ln, b_conv_out=dbco, norm_mlp_gain=dg2,
                   final_norm_gain=dg3)
    big = dict(w_in=dw_in, w_attn_out=dw_ao.reshape(N_DEV, d // N_DEV, d),
               w_conv_out=dw_co.reshape(N_DEV, d // N_DEV, d), w_out=dw_o.reshape(N_DEV, d // N_DEV, d),
               w_mlp_up=dw_up, w_mlp_down=dw_dn.reshape(N_DEV, ff_blk, d), meta_tokens=_shards_from_cols(dmeta),
               conv_dw_w=_shards_from_cols(dw_dw))
    big_names = list(big)
    reduced = list(_exchange("scatter_grads", [(big[k], True) for k in big_names]
                             + [(_cat_small(small_g), False), (loss_blk[0:1, :], False)]))
    loss = jnp.sum(reduced.pop()[:, 0, 0])

    grads, deltas, new_m, new_v = {}, {}, {}, {}
    for k, parts in zip(big_names, reduced[:-1]):
        shp = weights[k].shape
        w2 = lambda arr: arr.reshape(parts.shape[1:])
        res = _adamw("adamw_" + k, parts, w2(weights[k]), w2(mom_m[k]), w2(mom_v[k]))
        grads[k], deltas[k], new_m[k], new_v[k] = [r.reshape(shp) for r in res]
    res = _adamw("adamw_small", reduced[-1], _cat_small(weights), _cat_small(mom_m), _cat_small(mom_v))
    shapes = {k: weights[k].shape for k in SMALL}
    for dst, r in zip((grads, deltas, new_m, new_v), res):
        dst.update(_split_small(r, shapes))

    return (loss, grad_x, *[grads[k] for k in names], *[deltas[k] for k in names],
            *[new_m[k] for k in names], *[new_v[k] for k in names])
```

```python
import functools

import jax
import jax.numpy as jnp
from jax import lax
from jax.experimental import pallas as pl
from jax.experimental.pallas import tpu as pltpu

F32, BF16 = jnp.float32, jnp.bfloat16
N_DEV = 8
N_META = 16
HEAD_DIM = 64
LANES = 128
CONV_W = 31
RMS_EPS = 1e-6
LN_EPS = 1e-5
ROW_TILE = 688
SEG = 1024
FG_PAD = 128
VMEM_LIMIT = 56 * 1024 * 1024
ADAM_LR, ADAM_B1, ADAM_B2, ADAM_EPS, ADAM_WD, ADAM_STEP = 0.001, 0.9, 0.999, 1e-08, 0.01, 10
NEG = -1e30

SMALL = ("norm_mix_gain", "b_forget", "b_glu", "conv_dw_b", "conv_ln_gain", "conv_ln_bias", "b_conv_out",
         "norm_mlp_gain", "final_norm_gain")
SMALL_W = {"norm_mix_gain": 1024, "b_forget": 128, "b_glu": 2048, "conv_dw_b": 1024, "conv_ln_gain": 1024,
           "conv_ln_bias": 1024, "b_conv_out": 1024, "norm_mlp_gain": 1024, "final_norm_gain": 1024}
SMALL_N = {"norm_mix_gain": 1024, "b_forget": 16, "b_glu": 2048, "conv_dw_b": 1024, "conv_ln_gain": 1024,
           "conv_ln_bias": 1024, "b_conv_out": 1024, "norm_mlp_gain": 1024, "final_norm_gain": 1024}


def _params(sem=None):
    return pltpu.CompilerParams(dimension_semantics=sem, vmem_limit_bytes=VMEM_LIMIT)


def _sigmoid(x):
    return 1.0 / (1.0 + jnp.exp(-x))


def _dot_nt(a, b):
    return lax.dot_general(a, b, (((1,), (1,)), ((), ())), preferred_element_type=F32)


def _dot_tn(a, b):
    return lax.dot_general(a, b, (((0,), (0,)), ((), ())), preferred_element_type=F32)


HBM_SPEC = pl.BlockSpec(memory_space=pltpu.HBM)
SEM_SPEC = pl.BlockSpec(memory_space=pltpu.SEMAPHORE)
DATAFLOW = pltpu.SideEffectType.DATAFLOW_SIDE_EFFECTING


def _device_index():
    return 4 * lax.axis_index("x") + 2 * lax.axis_index("y") + lax.axis_index("c")


def _peers():
    x, y, c = lax.axis_index("x"), lax.axis_index("y"), lax.axis_index("c")
    out = []
    for k in range(1, N_DEV):
        px = 1 - x if k & 4 else x
        py = 1 - y if k & 2 else y
        pc = 1 - c if k & 1 else c
        out.append((k, (px, py, pc), 4 * px + 2 * py + pc))
    return out


def _peer_copy(per_dest, src_ref, land_ref, send_sems, recv_sems, a, k, dev, peer):
    src = src_ref.at[peer] if per_dest else src_ref
    return pltpu.make_async_remote_copy(
        src_ref=src, dst_ref=land_ref.at[_device_index()], send_sem=send_sems.at[a * (N_DEV - 1) + k - 1],
        recv_sem=recv_sems.at[a * (N_DEV - 1) + k - 1], device_id=dev, device_id_type=pl.DeviceIdType.MESH)


def _exchange_start(name, items):
    n = len(items)
    per_dest = [it[1] for it in items]

    def body(*refs):
        srcs, lands = refs[:n], refs[n:2 * n]
        send_sems, recv_sems, token = refs[2 * n], refs[2 * n + 1], refs[-1]
        for a in range(n):
            for k, dev, peer in _peers():
                _peer_copy(per_dest[a], srcs[a], lands[a], send_sems, recv_sems, a, k, dev, peer).start()
        token[...] = jnp.zeros(token.shape, F32)

    srcs = [pltpu.with_memory_space_constraint(it[0], pltpu.HBM) for it in items]
    lands = []
    for arr, pd in items:
        shp = arr.shape if pd else (N_DEV,) + arr.shape
        lands.append(pltpu.with_memory_space_constraint(lax.empty(shp, arr.dtype), pltpu.HBM))
    sems = pltpu.SemaphoreType.DMA((n * (N_DEV - 1),))
    res = pl.pallas_call(
        body, name=name,
        out_shape=(sems, sems, *[pltpu.HBM(a_.shape, a_.dtype) for a_ in srcs + lands],
                   jax.ShapeDtypeStruct((8, 128), F32)),
        in_specs=[HBM_SPEC] * (2 * n),
        out_specs=(SEM_SPEC, SEM_SPEC, *[HBM_SPEC] * (2 * n), pl.BlockSpec(memory_space=pltpu.VMEM)),
        input_output_aliases={i: 2 + i for i in range(2 * n)},
        compiler_params=pltpu.CompilerParams(has_side_effects=DATAFLOW),
    )(*srcs, *lands)
    return dict(per_dest=per_dest, send=res[0], recv=res[1], srcs=list(res[2:2 + n]),
                lands=list(res[2 + n:2 + 2 * n]), token=res[-1])


def _exchange_wait(name, started, after):
    per_dest = started["per_dest"]
    n = len(per_dest)

    def body(*refs):
        srcs, lands = refs[:n], refs[n:2 * n]
        send_sems, recv_sems = refs[2 * n], refs[2 * n + 1]
        for a in range(n):
            for k, dev, peer in _peers():
                cp = _peer_copy(per_dest[a], srcs[a], lands[a], send_sems, recv_sems, a, k, dev, peer)
                cp.wait_send()
                cp.wait_recv()

    bufs = started["srcs"] + started["lands"]
    res = pl.pallas_call(
        body, name=name, out_shape=tuple(pltpu.HBM(b_.shape, b_.dtype) for b_ in bufs),
        in_specs=[HBM_SPEC] * (2 * n) + [SEM_SPEC, SEM_SPEC, pl.BlockSpec(memory_space=pl.ANY)],
        out_specs=tuple([HBM_SPEC] * (2 * n)), input_output_aliases={i: i for i in range(2 * n)},
        compiler_params=pltpu.CompilerParams(has_side_effects=DATAFLOW),
    )(*bufs, started["send"], started["recv"], after)
    me = _device_index()
    out = []
    for pd, src, land in zip(per_dest, res[:n], res[n:]):
        own = lax.dynamic_index_in_dim(src, me, 0, keepdims=True) if pd else src[None]
        out.append(lax.dynamic_update_slice_in_dim(land, own, me, axis=0))
    return out


def _mm_nn(name, x, w, w_spec, n_out, tn, out_dtype, relu2=False):
    m, k = x.shape
    tm = ROW_TILE

    def body(x_ref, w_ref, *outs):
        acc = jnp.dot(x_ref[...], w_ref[...], preferred_element_type=F32)
        outs[0][...] = acc.astype(outs[0].dtype)
        if relu2:
            r = jnp.maximum(acc, 0.0)
            outs[1][...] = (r * r).astype(outs[1].dtype)

    o_spec = pl.BlockSpec((tm, tn), lambda i, j: (i, j))
    shapes = [jax.ShapeDtypeStruct((m, n_out), out_dtype)]
    if relu2:
        shapes.append(jax.ShapeDtypeStruct((m, n_out), BF16))
    res = pl.pallas_call(
        body, name=name, out_shape=shapes, grid=(m // tm, n_out // tn),
        in_specs=[pl.BlockSpec((tm, k), lambda i, j: (i, 0)), w_spec],
        out_specs=[o_spec] * len(shapes), compiler_params=_params(("parallel", "parallel")),
    )(x, w)
    return res if relu2 else res[0]


def _mm_nt(name, pairs, m, n_out, tn, out_dtype, relu_bwd_of=None):
    tm = ROW_TILE
    np_ = len(pairs)

    def body(*refs):
        acc = None
        for p in range(np_):
            d = _dot_nt(refs[2 * p][...], refs[2 * p + 1][...])
            acc = d if acc is None else acc + d
        if relu_bwd_of is not None:
            acc = acc * (2.0 * jnp.maximum(refs[2 * np_][...].astype(F32), 0.0))
        refs[-1][...] = acc.astype(refs[-1].dtype)

    o_spec = pl.BlockSpec((tm, tn), lambda i, j: (i, j))
    operands, specs = [], []
    for a, a_spec, w, w_spec in pairs:
        operands += [a, w]
        specs += [a_spec, w_spec]
    if relu_bwd_of is not None:
        operands.append(relu_bwd_of)
        specs.append(o_spec)
    return pl.pallas_call(
        body, name=name, out_shape=jax.ShapeDtypeStruct((m, n_out), out_dtype), grid=(m // tm, n_out // tn),
        in_specs=specs, out_specs=o_spec, compiler_params=_params(("parallel", "parallel")),
    )(*operands)


def _mm_tn(name, x, x_col, ta, dy, dy_col, tb, out_shape, out_spec, grid_ab):
    m = x.shape[0]
    tm = ROW_TILE
    nk = m // tm

    def body(x_ref, dy_ref, o_ref, acc_ref):
        k = pl.program_id(2)

        @pl.when(k == 0)
        def _():
            acc_ref[...] = jnp.zeros(acc_ref.shape, F32)

        acc_ref[...] += _dot_tn(x_ref[...], dy_ref[...])

        @pl.when(k == nk - 1)
        def _():
            o_ref[...] = acc_ref[...].astype(BF16)

    return pl.pallas_call(
        body, name=name, out_shape=jax.ShapeDtypeStruct(out_shape, BF16), grid=grid_ab + (nk,),
        in_specs=[pl.BlockSpec((tm, ta), lambda a, b, k: (k, x_col(a))),
                  pl.BlockSpec((tm, tb), lambda a, b, k: (k, dy_col(b)))],
        out_specs=out_spec, scratch_shapes=[pltpu.VMEM((ta, tb), F32)],
        compiler_params=_params(("parallel", "parallel", "arbitrary")),
    )(x, dy)


def _w_cols(k, tn, off_blocks):
    return pl.BlockSpec((k, tn), lambda i, j: (0, off_blocks + j))


def _a_rows(kw, col_block=0):
    return pl.BlockSpec((ROW_TILE, kw), lambda i, j: (i, col_block))


def _w_rows(tn, kw, col_block=0):
    return pl.BlockSpec((tn, kw), lambda i, j: (j, col_block))


def _grad_w(name, x, dy):
    nb = dy.shape[1]
    tb = min(nb, 1024)
    return _mm_tn(name, x, lambda a: a, 1024, dy, lambda b: b, tb, (x.shape[1], nb),
                  pl.BlockSpec((1024, tb), lambda a, b, k: (a, b)), (x.shape[1] // 1024, nb // tb))


def _row_spec(width):
    return pl.BlockSpec((ROW_TILE, width), lambda i: (i, 0))


def _vec_spec(width):
    return pl.BlockSpec((1, width), lambda i: (0, 0))


def _rms_fwd(name, h, g, res=None):
    n, d = h.shape

    def body(*refs):
        if res is None:
            h_ref, g_ref, hn_ref = refs
            hv = h_ref[...]
        else:
            h_ref, r_ref, g_ref, hs_ref, hn_ref = refs
            hv = h_ref[...] + r_ref[...]
            hs_ref[...] = hv
        r = lax.rsqrt(jnp.mean(hv * hv, axis=-1, keepdims=True) + RMS_EPS)
        hn_ref[...] = (hv * r * g_ref[...]).astype(BF16)

    ins = [h, g] if res is None else [h, res, g]
    in_specs = [_row_spec(d), _vec_spec(d)] if res is None else [_row_spec(d), _row_spec(d), _vec_spec(d)]
    hn_shape = jax.ShapeDtypeStruct((n, d), BF16)
    if res is None:
        out_shape, out_specs = hn_shape, _row_spec(d)
    else:
        out_shape, out_specs = [jax.ShapeDtypeStruct((n, d), F32), hn_shape], [_row_spec(d), _row_spec(d)]
    return pl.pallas_call(body, name=name, out_shape=out_shape, grid=(n // ROW_TILE,), in_specs=in_specs,
                          out_specs=out_specs, compiler_params=_params(("parallel",)))(*ins)


def _rms_bwd(name, dhn, h, g, dres, batch, with_bf16=False, with_meta=False):
    n, d = h.shape
    t = n // batch
    nt = t // ROW_TILE

    def body(dhn_ref, h_ref, g_ref, dres_ref, *outs):
        first = (pl.program_id(0) == 0) & (pl.program_id(1) == 0)
        hv = h_ref[...]
        r = lax.rsqrt(jnp.mean(hv * hv, axis=-1, keepdims=True) + RMS_EPS)
        nrm = hv * r
        dn = dhn_ref[...] * g_ref[...]
        dh = dres_ref[...] + r * (dn - nrm * jnp.mean(dn * nrm, axis=-1, keepdims=True))
        outs[0][...] = dh
        dg_ref = outs[1]

        @pl.when(first)
        def _():
            dg_ref[...] = jnp.zeros(dg_ref.shape, F32)

        dg_ref[...] += jnp.sum(dhn_ref[...] * nrm, axis=0, keepdims=True)
        nxt = 2
        if with_bf16:
            outs[nxt][...] = dh.astype(BF16)
            nxt += 1
        if with_meta:
            meta_ref = outs[nxt]

            @pl.when(first)
            def _():
                meta_ref[...] = jnp.zeros(meta_ref.shape, F32)

            @pl.when(pl.program_id(1) == 0)
            def _():
                meta_ref[...] += dh[0:N_META, :]

    row = pl.BlockSpec((ROW_TILE, d), lambda b, j: (b * nt + j, 0))
    vec = pl.BlockSpec((1, d), lambda b, j: (0, 0))
    shapes = [jax.ShapeDtypeStruct((n, d), F32), jax.ShapeDtypeStruct((1, d), F32)]
    specs = [row, vec]
    if with_bf16:
        shapes.append(jax.ShapeDtypeStruct((n, d), BF16))
        specs.append(row)
    if with_meta:
        shapes.append(jax.ShapeDtypeStruct((N_META, d), F32))
        specs.append(pl.BlockSpec((N_META, d), lambda b, j: (0, 0)))
    return pl.pallas_call(body, name=name, out_shape=shapes, grid=(batch, nt), in_specs=[row, row, vec, row],
                          out_specs=specs, compiler_params=_params(("arbitrary", "arbitrary")))(dhn, h, g, dres)


def _final(name, h1, dn, tgt, g, batch):
    n, d = h1.shape
    t = n // batch
    nt = t // ROW_TILE

    def body(h1_ref, dn_ref, tgt_ref, g_ref, dh_ref, dhb_ref, loss_ref, dg_ref):
        first = (pl.program_id(0) == 0) & (pl.program_id(1) == 0)
        hv = h1_ref[...] + dn_ref[...]
        r = lax.rsqrt(jnp.mean(hv * hv, axis=-1, keepdims=True) + RMS_EPS)
        nrm = hv * r
        gv = g_ref[...]
        pos = pl.program_id(1) * ROW_TILE + lax.broadcasted_iota(jnp.int32, (ROW_TILE, 1), 0)
        diff = jnp.where(pos >= N_META, nrm * gv - tgt_ref[...], 0.0)
        dy = diff * (1.0 / d)

        @pl.when(first)
        def _():
            loss_ref[...] = jnp.zeros(loss_ref.shape, F32)
            dg_ref[...] = jnp.zeros(dg_ref.shape, F32)

        loss_ref[...] += jnp.full(loss_ref.shape, 0.5 / d, F32) * jnp.sum(diff * diff)
        dg_ref[...] += jnp.sum(dy * nrm, axis=0, keepdims=True)
        dng = dy * gv
        dh = r * (dng - nrm * jnp.mean(dng * nrm, axis=-1, keepdims=True))
        dh_ref[...] = dh
        dhb_ref[...] = dh.astype(BF16)

    row = pl.BlockSpec((ROW_TILE, d), lambda b, j: (b * nt + j, 0))
    vec = pl.BlockSpec((1, d), lambda b, j: (0, 0))
    return pl.pallas_call(
        body, name=name, grid=(batch, nt), in_specs=[row, row, row, vec],
        out_shape=[jax.ShapeDtypeStruct((n, d), F32), jax.ShapeDtypeStruct((n, d), BF16),
                   jax.ShapeDtypeStruct((8, 128), F32), jax.ShapeDtypeStruct((1, d), F32)],
        out_specs=[row, row, pl.BlockSpec((8, 128), lambda b, j: (0, 0)), vec],
        compiler_params=_params(("arbitrary", "arbitrary")))(h1, dn, tgt, g)


def _ln_silu_fwd(name, c1, g, b):
    n, d = c1.shape

    def body(c_ref, g_ref, b_ref, o_ref):
        xv = c_ref[...]
        xc = xv - jnp.mean(xv, axis=-1, keepdims=True)
        rstd = lax.rsqrt(jnp.mean(xc * xc, axis=-1, keepdims=True) + LN_EPS)
        c2 = xc * rstd * g_ref[...] + b_ref[...]
        o_ref[...] = (c2 * _sigmoid(c2)).astype(BF16)

    return pl.pallas_call(body, name=name, out_shape=jax.ShapeDtypeStruct((n, d), BF16), grid=(n // ROW_TILE,),
                          in_specs=[_row_spec(d), _vec_spec(d), _vec_spec(d)], out_specs=_row_spec(d),
                          compiler_params=_params(("parallel",)))(c1, g, b)


def _ln_silu_bwd(name, dc3, c1, g, b):
    n, d = c1.shape

    def body(d_ref, c_ref, g_ref, b_ref, dc1_ref, dg_ref, db_ref):
        xv = c_ref[...]
        xc = xv - jnp.mean(xv, axis=-1, keepdims=True)
        rstd = lax.rsqrt(jnp.mean(xc * xc, axis=-1, keepdims=True) + LN_EPS)
        xh = xc * rstd
        c2 = xh * g_ref[...] + b_ref[...]
        s = _sigmoid(c2)
        dc2 = d_ref[...] * (s * (1.0 + c2 * (1.0 - s)))

        @pl.when(pl.program_id(0) == 0)
        def _():
            dg_ref[...] = jnp.zeros(dg_ref.shape, F32)
            db_ref[...] = jnp.zeros(db_ref.shape, F32)

        dg_ref[...] += jnp.sum(dc2 * xh, axis=0, keepdims=True)
        db_ref[...] += jnp.sum(dc2, axis=0, keepdims=True)
        dxh = dc2 * g_ref[...]
        dc1_ref[...] = rstd * (dxh - jnp.mean(dxh, axis=-1, keepdims=True)
                               - xh * jnp.mean(dxh * xh, axis=-1, keepdims=True))

    return pl.pallas_call(
        body, name=name, grid=(n // ROW_TILE,),
        out_shape=[jax.ShapeDtypeStruct((n, d), F32), jax.ShapeDtypeStruct((1, d), F32),
                   jax.ShapeDtypeStruct((1, d), F32)],
        in_specs=[_row_spec(d), _row_spec(d), _vec_spec(d), _vec_spec(d)],
        out_specs=[_row_spec(d), _vec_spec(d), _vec_spec(d)],
        compiler_params=_params(("arbitrary",)))(dc3, c1, g, b)


MERGE_TC = 512


def _merge_fwd(name, gates, a, c, b_co):
    n, d = a.shape
    nc = d // MERGE_TC

    def body(ga_ref, gc_ref, a_ref, c_ref, b_ref, m_ref):
        m = _sigmoid(ga_ref[...]) * a_ref[...] + _sigmoid(gc_ref[...]) * (c_ref[...] + b_ref[...])
        m_ref[...] = m.astype(BF16)

    blk = lambda off: pl.BlockSpec((ROW_TILE, MERGE_TC), lambda i, j: (i, off + j))
    return pl.pallas_call(
        body, name=name, out_shape=jax.ShapeDtypeStruct((n, d), BF16), grid=(n // ROW_TILE, nc),
        in_specs=[blk(0), blk(nc), blk(0), blk(0), pl.BlockSpec((1, MERGE_TC), lambda i, j: (0, j))],
        out_specs=blk(0), compiler_params=_params(("parallel", "parallel")))(gates, gates, a, c, b_co)


def _merge_bwd(name, dm, gates, a, c, b_co):
    n, d = a.shape
    nc = d // MERGE_TC

    def body(dm_ref, ga_ref, gc_ref, a_ref, c_ref, b_ref, da_ref, dc_ref, dga_ref, dgc_ref, dbco_ref):
        dmv = dm_ref[...]
        sa, sc = _sigmoid(ga_ref[...]), _sigmoid(gc_ref[...])
        dc = dmv * sc
        da_ref[...] = (dmv * sa).astype(BF16)
        dc_ref[...] = dc.astype(BF16)
        dga_ref[...] = (dmv * a_ref[...] * sa * (1.0 - sa)).astype(BF16)
        dgc_ref[...] = (dmv * (c_ref[...] + b_ref[...]) * sc * (1.0 - sc)).astype(BF16)

        @pl.when(pl.program_id(1) == 0)
        def _():
            dbco_ref[...] = jnp.zeros(dbco_ref.shape, F32)

        dbco_ref[...] += jnp.sum(dc, axis=0, keepdims=True)

    blk = lambda off: pl.BlockSpec((ROW_TILE, MERGE_TC), lambda j, i: (i, off + j))
    vec = pl.BlockSpec((1, MERGE_TC), lambda j, i: (0, j))
    act = jax.ShapeDtypeStruct((n, d), BF16)
    return pl.pallas_call(
        body, name=name, grid=(nc, n // ROW_TILE),
        out_shape=[act, act, act, act, jax.ShapeDtypeStruct((1, d), F32)],
        in_specs=[blk(0), blk(0), blk(nc), blk(0), blk(0), vec],
        out_specs=[blk(0), blk(0), blk(0), blk(0), vec],
        compiler_params=_params(("parallel", "arbitrary")))(dm, gates, gates, a, c, b_co)


CONV_TC = 128
CONV_HALO = 32


def _glu_conv_fwd(name, glu, b_glu, w_dw, b_dw, batch):
    n, c2 = glu.shape
    c = c2 // 2
    t = n // batch
    nc = c // CONV_TC

    def body(a_ref, gt_ref, ba_ref, bg_ref, w_ref, bdw_ref, o_ref, pad_ref):
        u = (a_ref[...] + ba_ref[...]) * _sigmoid(gt_ref[...] + bg_ref[...])
        pad_ref[0:CONV_HALO, :] = jnp.zeros((CONV_HALO, CONV_TC), F32)
        pad_ref[CONV_HALO:CONV_HALO + t, :] = u
        acc = jnp.zeros((t, CONV_TC), F32) + bdw_ref[...]
        for j in range(CONV_W):
            off = CONV_HALO - (CONV_W - 1) + j
            acc = acc + w_ref[j:j + 1, :] * pad_ref[off:off + t, :]
        o_ref[...] = acc

    seq = lambda off: pl.BlockSpec((t, CONV_TC), lambda b, j: (b, off + j))
    vec = lambda off: pl.BlockSpec((1, CONV_TC), lambda b, j: (0, off + j))
    return pl.pallas_call(
        body, name=name, out_shape=jax.ShapeDtypeStruct((n, c), F32), grid=(batch, nc),
        in_specs=[seq(0), seq(nc), vec(0), vec(nc), pl.BlockSpec((CONV_W, CONV_TC), lambda b, j: (0, j)), vec(0)],
        out_specs=seq(0), scratch_shapes=[pltpu.VMEM((t + CONV_HALO, CONV_TC), F32)],
        compiler_params=_params(("parallel", "parallel")))(glu, glu, b_glu, b_glu, w_dw, b_dw)


def _glu_conv_bwd(name, dc1, glu, b_glu, w_dw, batch):
    n, c2 = glu.shape
    c = c2 // 2
    t = n // batch
    nc = c // CONV_TC

    def body(d_ref, a_ref, gt_ref, ba_ref, bg_ref, w_ref, dga_ref, dgg_ref, dw_ref, dbdw_ref, dba_ref, dbg_ref,
             padu_ref, padd_ref):
        av = a_ref[...] + ba_ref[...]
        sg = _sigmoid(gt_ref[...] + bg_ref[...])
        dc = d_ref[...]
        padu_ref[0:CONV_HALO, :] = jnp.zeros((CONV_HALO, CONV_TC), F32)
        padu_ref[CONV_HALO:CONV_HALO + t, :] = av * sg
        padd_ref[0:t, :] = dc
        padd_ref[t:t + CONV_HALO, :] = jnp.zeros((CONV_HALO, CONV_TC), F32)

        @pl.when(pl.program_id(1) == 0)
        def _():
            dw_ref[...] = jnp.zeros(dw_ref.shape, F32)
            dbdw_ref[...] = jnp.zeros(dbdw_ref.shape, F32)
            dba_ref[...] = jnp.zeros(dba_ref.shape, F32)
            dbg_ref[...] = jnp.zeros(dbg_ref.shape, F32)

        du = jnp.zeros((t, CONV_TC), F32)
        for j in range(CONV_W):
            back = CONV_W - 1 - j
            du = du + w_ref[j:j + 1, :] * padd_ref[back:back + t, :]
            off = CONV_HALO - (CONV_W - 1) + j
            dw_ref[j:j + 1, :] += jnp.sum(dc * padu_ref[off:off + t, :], axis=0, keepdims=True)
        dga = du * sg
        dgg = du * av * sg * (1.0 - sg)
        dga_ref[...] = dga.astype(BF16)
        dgg_ref[...] = dgg.astype(BF16)
        dbdw_ref[...] += jnp.sum(dc, axis=0, keepdims=True)
        dba_ref[...] += jnp.sum(dga, axis=0, keepdims=True)
        dbg_ref[...] += jnp.sum(dgg, axis=0, keepdims=True)

    seq = lambda off: pl.BlockSpec((t, CONV_TC), lambda j, b: (b, off + j))
    vec = lambda off: pl.BlockSpec((1, CONV_TC), lambda j, b: (0, off + j))
    wsp = pl.BlockSpec((CONV_W, CONV_TC), lambda j, b: (0, j))
    act = jax.ShapeDtypeStruct((n, c), BF16)
    v = jax.ShapeDtypeStruct((1, c), F32)
    return pl.pallas_call(
        body, name=name, grid=(nc, batch),
        out_shape=[act, act, jax.ShapeDtypeStruct((CONV_W, c), F32), v, v, v],
        in_specs=[seq(0), seq(0), seq(nc), vec(0), vec(nc), wsp],
        out_specs=[seq(0), seq(0), wsp, vec(0), vec(0), vec(0)],
        scratch_shapes=[pltpu.VMEM((t + CONV_HALO, CONV_TC), F32), pltpu.VMEM((t + CONV_HALO, CONV_TC), F32)],
        compiler_params=_params(("parallel", "arbitrary")))(dc1, glu, glu, b_glu, b_glu, w_dw)


def _split3(x):
    hi = x.astype(BF16)
    r = x - hi.astype(F32)
    mid = r.astype(BF16)
    lo = (r - mid.astype(F32)).astype(BF16)
    return hi, mid, lo


def _tri_matmul(tri, x):
    hi, mid, lo = _split3(x)
    dot = lambda v: jnp.dot(tri, v, preferred_element_type=F32)
    return dot(hi) + dot(mid) + dot(lo)


def _fox_prep_fwd(name, fg, b_fg, batch):
    n, w = fg.shape
    t = n // batch
    nq = t // ROW_TILE

    def body(fg_ref, b_ref, cum_ref):
        row = lax.broadcasted_iota(jnp.int32, (ROW_TILE, ROW_TILE), 0)
        col = lax.broadcasted_iota(jnp.int32, (ROW_TILE, ROW_TILE), 1)
        tri = (row >= col).astype(BF16)
        for k in range(nq):
            rows = slice(k * ROW_TILE, (k + 1) * ROW_TILE)
            z = fg_ref[rows, :] + b_ref[...]
            logf = jnp.minimum(z, 0.0) - jnp.log(1.0 + jnp.exp(-jnp.abs(z)))
            cum = _tri_matmul(tri, logf)
            if k > 0:
                cum = cum + cum_ref[k * ROW_TILE - 1:k * ROW_TILE, :]
            cum_ref[rows, :] = cum

    seq = pl.BlockSpec((t, w), lambda b: (b, 0))
    return pl.pallas_call(body, name=name, out_shape=jax.ShapeDtypeStruct((n, w), F32), grid=(batch,),
                          in_specs=[seq, pl.BlockSpec((1, w), lambda b: (0, 0))], out_specs=seq,
                          compiler_params=_params(("parallel",)))(fg, b_fg)


def _fox_prep_bwd(name, dcum_k, dcum_q, fg, b_fg, batch):
    n, w = fg.shape
    t = n // batch
    nq = t // ROW_TILE

    def body(dk_ref, dq_ref, fg_ref, b_ref, dfg_ref, db_ref, rev_ref):
        row = lax.broadcasted_iota(jnp.int32, (ROW_TILE, ROW_TILE), 0)
        col = lax.broadcasted_iota(jnp.int32, (ROW_TILE, ROW_TILE), 1)
        tri = (col >= row).astype(BF16)

        @pl.when(pl.program_id(0) == 0)
        def _():
            db_ref[...] = jnp.zeros(db_ref.shape, F32)

        for k in reversed(range(nq)):
            rows = slice(k * ROW_TILE, (k + 1) * ROW_TILE)
            dlog = _tri_matmul(tri, dk_ref[rows, :] + dq_ref[rows, :])
            if k < nq - 1:
                dlog = dlog + rev_ref[(k + 1) * ROW_TILE:(k + 1) * ROW_TILE + 1, :]
            rev_ref[rows, :] = dlog
            dfg = dlog * _sigmoid(-(fg_ref[rows, :] + b_ref[...]))
            dfg_ref[rows, :] = dfg.astype(BF16)
            db_ref[...] += jnp.sum(dfg, axis=0, keepdims=True)

    seq = pl.BlockSpec((t, w), lambda b: (b, 0))
    vec = pl.BlockSpec((1, w), lambda b: (0, 0))
    return pl.pallas_call(
        body, name=name, grid=(batch,),
        out_shape=[jax.ShapeDtypeStruct((n, w), BF16), jax.ShapeDtypeStruct((1, w), F32)],
        in_specs=[seq, seq, seq, vec], out_specs=[seq, vec], scratch_shapes=[pltpu.VMEM((t, w), F32)],
        compiler_params=_params(("arbitrary",)))(dcum_k, dcum_q, fg, b_fg)


def _head_masks(x):
    lane = lax.broadcasted_iota(jnp.int32, x.shape, 1)
    zero = jnp.zeros(x.shape, x.dtype)
    return jnp.where(lane < HEAD_DIM, x, zero), jnp.where(lane >= HEAD_DIM, x, zero)


def _attn_specs(t, nq):
    qkv = lambda off: pl.BlockSpec((t, LANES), lambda b, h: (b, off + h))
    cumr = pl.BlockSpec((None, None, nq, 8, ROW_TILE), lambda b, h: (b, h, 0, 0, 0))
    return qkv, cumr


def _attn_fwd(name, qkv, cumr, batch):
    n, w3 = qkv.shape
    w = w3 // 3
    t = n // batch
    nq = t // ROW_TILE
    n_pairs = w // LANES
    tq = ROW_TILE

    def body(q_ref, k_ref, v_ref, cr_ref, o_ref, lse_ref):
        row = lax.broadcasted_iota(jnp.int32, (tq, tq), 0)
        col = lax.broadcasted_iota(jnp.int32, (tq, tq), 1)
        causal = row >= col
        lane = lax.broadcasted_iota(jnp.int32, (tq, LANES), 1)
        for i in range(nq):
            rows = slice(i * tq, (i + 1) * tq)
            qs = _head_masks(q_ref[rows, :] * 0.125)
            outs, lses = [], []
            for hh in range(2):
                m = jnp.full((tq, 1), NEG, F32)
                l = jnp.zeros((tq, 1), F32)
                acc = jnp.zeros((tq, LANES), F32)
                for j in range(i + 1):
                    cols = slice(j * tq, (j + 1) * tq)
                    s = _dot_nt(qs[hh], k_ref[cols, :]) - cr_ref[j, hh:hh + 1, :]
                    if j == i:
                        s = jnp.where(causal, s, NEG)
                    m_new = jnp.maximum(m, jnp.max(s, axis=1, keepdims=True))
                    alpha = jnp.exp(m - m_new)
                    p = jnp.exp(s - m_new)
                    l = alpha * l + jnp.sum(p, axis=1, keepdims=True)
                    acc = alpha * acc + jnp.dot(p.astype(BF16), v_ref[cols, :], preferred_element_type=F32)
                    m = m_new
                outs.append(acc / l)
                lses.append(m + jnp.log(l))
            o_ref[rows, :] = jnp.where(lane < HEAD_DIM, outs[0], outs[1]).astype(BF16)
            lse_ref[rows, :] = jnp.where(lane < HEAD_DIM, lses[0], lses[1])

    qkv_spec, cumr_spec = _attn_specs(t, nq)
    return pl.pallas_call(
        body, name=name, grid=(batch, n_pairs),
        out_shape=[jax.ShapeDtypeStruct((n, w), BF16), jax.ShapeDtypeStruct((n, w), F32)],
        in_specs=[qkv_spec(0), qkv_spec(n_pairs), qkv_spec(2 * n_pairs), cumr_spec],
        out_specs=[qkv_spec(0), qkv_spec(0)],
        compiler_params=_params(("parallel", "parallel")))(qkv, qkv, qkv, cumr)


def _attn_bwd(name, qkv, o, do, lse, cumr, batch):
    n, w3 = qkv.shape
    w = w3 // 3
    t = n // batch
    nq = t // ROW_TILE
    n_pairs = w // LANES
    tq = ROW_TILE

    def body(q_ref, k_ref, v_ref, o_ref, do_ref, lse_ref, cr_ref, dq_ref, dk_ref, dv_ref, dcr_ref, dcq_ref,
             dk_acc, dv_acc):
        pair = pl.program_id(1)
        row = lax.broadcasted_iota(jnp.int32, (tq, tq), 0)
        col = lax.broadcasted_iota(jnp.int32, (tq, tq), 1)
        causal = row >= col
        lane = lax.broadcasted_iota(jnp.int32, (tq, LANES), 1)
        dk_acc[...] = jnp.zeros(dk_acc.shape, F32)
        dv_acc[...] = jnp.zeros(dv_acc.shape, F32)
        dcr_ref[...] = jnp.zeros(dcr_ref.shape, F32)

        @pl.when(pair == 0)
        def _():
            dcq_ref[...] = jnp.zeros(dcq_ref.shape, F32)

        for i in range(nq):
            rows = slice(i * tq, (i + 1) * tq)
            qs = _head_masks(q_ref[rows, :] * 0.125)
            dos = _head_masks(do_ref[rows, :])
            dq = jnp.zeros((tq, LANES), F32)
            dcq = []
            for hh in range(2):
                row_sum = jnp.zeros((tq, 1), F32)
                lse = lse_ref[rows, hh * HEAD_DIM:hh * HEAD_DIM + 1]
                delta = jnp.sum(dos[hh].astype(F32) * o_ref[rows, :].astype(F32), axis=1, keepdims=True)
                for j in range(i + 1):
                    cols = slice(j * tq, (j + 1) * tq)
                    s = _dot_nt(qs[hh], k_ref[cols, :]) - cr_ref[j, hh:hh + 1, :]
                    p = jnp.exp(s - lse)
                    if j == i:
                        p = jnp.where(causal, p, 0.0)
                    dp = _dot_nt(dos[hh], v_ref[cols, :])
                    ds = p * (dp - delta)
                    pb, dsb = p.astype(BF16), ds.astype(BF16)
                    km = _head_masks(k_ref[cols, :])[hh]
                    dv_acc[cols, :] += _dot_tn(pb, dos[hh])
                    dk_acc[cols, :] += _dot_tn(dsb, qs[hh])
                    dq = dq + jnp.dot(dsb, km, preferred_element_type=F32)
                    dcr_ref[j, hh:hh + 1, :] -= jnp.sum(ds, axis=0, keepdims=True)
                    row_sum = row_sum + jnp.sum(ds, axis=1, keepdims=True)
                dcq.append(row_sum)
            dq_ref[rows, :] = (dq * 0.125).astype(BF16)
            dcq_ref[rows, :] = jnp.where(lane == 2 * pair, dcq[0],
                                         jnp.where(lane == 2 * pair + 1, dcq[1], dcq_ref[rows, :]))
        dk_ref[...] = dk_acc[...].astype(BF16)
        dv_ref[...] = dv_acc[...].astype(BF16)

    qkv_spec, cumr_spec = _attn_specs(t, nq)
    act = jax.ShapeDtypeStruct((n, w), BF16)
    return pl.pallas_call(
        body, name=name, grid=(batch, n_pairs),
        out_shape=[act, act, act, jax.ShapeDtypeStruct(cumr.shape, F32), jax.ShapeDtypeStruct((n, LANES), F32)],
        in_specs=[qkv_spec(0), qkv_spec(n_pairs), qkv_spec(2 * n_pairs), qkv_spec(0), qkv_spec(0), qkv_spec(0),
                  cumr_spec],
        out_specs=[qkv_spec(0), qkv_spec(0), qkv_spec(0), cumr_spec, pl.BlockSpec((t, LANES), lambda b, h: (b, 0))],
        scratch_shapes=[pltpu.VMEM((t, LANES), F32), pltpu.VMEM((t, LANES), F32)],
        compiler_params=_params(("parallel", "arbitrary")))(qkv, qkv, qkv, o, do, lse, cumr)


def _adamw(name, parts, w, m, v):
    r, c = w.shape
    tr = 128 if r % 128 == 0 else r
    c1 = 1.0 - ADAM_B1 ** ADAM_STEP
    c2 = 1.0 - ADAM_B2 ** ADAM_STEP

    def body(p_ref, w_ref, m_ref, v_ref, g_ref, d_ref, m2_ref, v2_ref):
        g = p_ref[0].astype(F32)
        for s in range(1, N_DEV):
            g = g + p_ref[s].astype(F32)
        m2 = ADAM_B1 * m_ref[...] + (1.0 - ADAM_B1) * g
        v2 = ADAM_B2 * v_ref[...] + (1.0 - ADAM_B2) * (g * g)
        g_ref[...] = g
        m2_ref[...] = m2
        v2_ref[...] = v2
        d_ref[...] = -ADAM_LR * ((m2 / c1) / (jnp.sqrt(v2 / c2) + ADAM_EPS) + ADAM_WD * w_ref[...])

    blk = pl.BlockSpec((tr, c), lambda i: (i, 0))
    shp = jax.ShapeDtypeStruct((r, c), F32)
    return pl.pallas_call(
        body, name=name, out_shape=[shp] * 4, grid=(r // tr,),
        in_specs=[pl.BlockSpec((N_DEV, tr, c), lambda i: (0, i, 0)), blk, blk, blk], out_specs=[blk] * 4,
        compiler_params=_params(("parallel",)))(parts, w, m, v)


def _cat_small(vals):
    parts = []
    for name in SMALL:
        v = vals[name].reshape(1, -1).astype(F32)
        parts.append(jnp.pad(v, ((0, 0), (0, SMALL_W[name] - v.shape[1]))))
    return jnp.concatenate(parts, axis=1)


def _split_small(row, shapes):
    out, off = {}, 0
    for name in SMALL:
        out[name] = row[0, off:off + SMALL_N[name]].reshape(shapes[name])
        off += SMALL_W[name]
    return out


def _w_in_pieces(lo, hi, n_fg):
    o_fg = 3 * SEG
    out = []
    c = lo
    while c < hi:
        if c < o_fg:
            src, base, end = c // SEG, (c // SEG) * SEG, (c // SEG + 1) * SEG
        elif c < o_fg + n_fg:
            src, base, end = 7, o_fg, o_fg + n_fg
        else:
            k = (c - o_fg - n_fg) // SEG
            src, base, end = 3 + k, o_fg + n_fg + k * SEG, o_fg + n_fg + (k + 1) * SEG
        stop = min(hi, end)
        out.append((src, c - base, stop - base, c))
        c = stop
    return out


def _cols_from_shards(g):
    return jnp.transpose(g, (1, 0, 2)).reshape(g.shape[1], N_DEV * g.shape[2])


def _shards_from_cols(a):
    r, c = a.shape
    return jnp.transpose(a.reshape(r, N_DEV, c // N_DEV), (1, 0, 2))


def kernel(x, meta_tokens, norm_mix_gain, w_in, b_forget, w_attn_out, b_glu, conv_dw_w, conv_dw_b, conv_ln_gain, conv_ln_bias, w_conv_out, b_conv_out, w_out, norm_mlp_gain, w_mlp_up, w_mlp_down, final_norm_gain, loss_target, m_meta_tokens, m_norm_mix_gain, m_w_in, m_b_forget, m_w_attn_out, m_b_glu, m_conv_dw_w, m_conv_dw_b, m_conv_ln_gain, m_conv_ln_bias, m_w_conv_out, m_b_conv_out, m_w_out, m_norm_mlp_gain, m_w_mlp_up, m_w_mlp_down, m_final_norm_gain, v_meta_tokens, v_norm_mix_gain, v_w_in, v_b_forget, v_w_attn_out, v_b_glu, v_conv_dw_w, v_conv_dw_b, v_conv_ln_gain, v_conv_ln_bias, v_w_conv_out, v_b_conv_out, v_w_out, v_norm_mlp_gain, v_w_mlp_up, v_w_mlp_down, v_final_norm_gain):
    weights = dict(meta_tokens=meta_tokens, norm_mix_gain=norm_mix_gain, w_in=w_in, b_forget=b_forget, w_attn_out=w_attn_out, b_glu=b_glu, conv_dw_w=conv_dw_w, conv_dw_b=conv_dw_b, conv_ln_gain=conv_ln_gain, conv_ln_bias=conv_ln_bias, w_conv_out=w_conv_out, b_conv_out=b_conv_out, w_out=w_out, norm_mlp_gain=norm_mlp_gain, w_mlp_up=w_mlp_up, w_mlp_down=w_mlp_down, final_norm_gain=final_norm_gain)
    mom_m = dict(meta_tokens=m_meta_tokens, norm_mix_gain=m_norm_mix_gain, w_in=m_w_in, b_forget=m_b_forget, w_attn_out=m_w_attn_out, b_glu=m_b_glu, conv_dw_w=m_conv_dw_w, conv_dw_b=m_conv_dw_b, conv_ln_gain=m_conv_ln_gain, conv_ln_bias=m_conv_ln_bias, w_conv_out=m_w_conv_out, b_conv_out=m_b_conv_out, w_out=m_w_out, norm_mlp_gain=m_norm_mlp_gain, w_mlp_up=m_w_mlp_up, w_mlp_down=m_w_mlp_down, final_norm_gain=m_final_norm_gain)
    mom_v = dict(meta_tokens=v_meta_tokens, norm_mix_gain=v_norm_mix_gain, w_in=v_w_in, b_forget=v_b_forget, w_attn_out=v_w_attn_out, b_glu=v_b_glu, conv_dw_w=v_conv_dw_w, conv_dw_b=v_conv_dw_b, conv_ln_gain=v_conv_ln_gain, conv_ln_bias=v_conv_ln_bias, w_conv_out=v_w_conv_out, b_conv_out=v_b_conv_out, w_out=v_w_out, norm_mlp_gain=v_norm_mlp_gain, w_mlp_up=v_w_mlp_up, w_mlp_down=v_w_mlp_down, final_norm_gain=v_final_norm_gain)
    names = list(weights)
    batch, seq, d = x.shape
    t = seq + N_META
    n = batch * t
    nq = t // ROW_TILE
    n_pairs = d // LANES
    assert t % ROW_TILE == 0 and d == SEG

    gather_a = _exchange_start("gather_in_start", [(w_in[0].astype(BF16), False), (meta_tokens, False),
                                                   (conv_dw_w[0], False)])
    gather_b = _exchange_start("gather_rest_start", [
        (w_attn_out[0].astype(BF16), False), (w_conv_out[0].astype(BF16), False), (w_out[0].astype(BF16), False),
        (w_mlp_up[0].astype(BF16), False), (w_mlp_down[0].astype(BF16), False)])
    w_in_g, meta_g, w_dw_g = _exchange_wait("gather_in_wait", gather_a, gather_b["token"])
    n_fg = b_forget.shape[1]
    shard_w = w_in.shape[2]
    seg_cols = [[] for _ in range(8)]
    for p in range(N_DEV):
        for src, c0, c1, orig in _w_in_pieces(p * shard_w, (p + 1) * shard_w, n_fg):
            seg_cols[src].append(w_in_g[p][:, orig - p * shard_w:orig - p * shard_w + c1 - c0])
    w_pad = jnp.concatenate([c_ for src in range(8) for c_ in seg_cols[src]]
                            + [jnp.zeros((d, FG_PAD - n_fg), BF16)], axis=1)
    d_ff = w_mlp_down.shape[1] * N_DEV
    ff_blk = d_ff // N_DEV
    meta_f = _cols_from_shards(meta_g)
    w_dw = _cols_from_shards(w_dw_g)

    row2 = lambda v: v.reshape(1, -1)
    g1, g2, g3 = row2(norm_mix_gain), row2(norm_mlp_gain), row2(final_norm_gain)
    b_fg = jnp.pad(b_forget, ((0, 0), (0, FG_PAD - n_fg)))
    h0 = jnp.concatenate([jnp.broadcast_to(meta_f[None], (batch, N_META, d)), x], axis=1).reshape(n, d)
    tgt = jnp.concatenate([jnp.zeros((batch, N_META, d), F32), loss_target], axis=1).reshape(n, d)

    hn1 = _rms_fwd("rms1", h0, g1)
    qkv = _mm_nn("proj_qkv", hn1, w_pad, _w_cols(d, 512, 0), 3 * SEG, 512, BF16)
    glu = _mm_nn("proj_glu", hn1, w_pad, _w_cols(d, 512, 3 * SEG // 512), 2 * SEG, 512, F32)
    gates = _mm_nn("proj_gates", hn1, w_pad, _w_cols(d, 512, 5 * SEG // 512), 2 * SEG, 512, F32)
    fg = _mm_nn("proj_fg", hn1, w_pad, _w_cols(d, FG_PAD, 7 * SEG // FG_PAD), FG_PAD, FG_PAD, F32)

    cum = _fox_prep_fwd("fox_cumsum", fg, b_fg, batch)
    cum_h = cum.reshape(batch, t, FG_PAD)[:, :, :2 * n_pairs].reshape(batch, t, n_pairs, 2)
    cumr = jnp.transpose(cum_h.reshape(batch, nq, ROW_TILE, n_pairs, 2), (0, 3, 1, 4, 2))
    cumr = jnp.pad(cumr, ((0, 0), (0, 0), (0, 0), (0, 6), (0, 0)))
    o, lse = _attn_fwd("attn_fwd", qkv, cumr, batch)
    rest = _exchange_wait("gather_rest_wait", gather_b, o)
    w_ao, w_co, w_o = [r_.reshape(d, d) for r_ in rest[:3]]
    w_up = rest[3]
    w_dn = rest[4].reshape(d_ff, d)
    a = _mm_nn("attn_out", o, w_ao, _w_cols(d, 512, 0), d, 512, F32)

    c1 = _glu_conv_fwd("glu_conv", glu, b_glu, w_dw, conv_dw_b, batch)
    c3 = _ln_silu_fwd("ln_silu", c1, conv_ln_gain, conv_ln_bias)
    c = _mm_nn("conv_out", c3, w_co, _w_cols(d, 512, 0), d, 512, F32)

    mrg = _merge_fwd("merge", gates, a, c, b_conv_out)
    mo = _mm_nn("mix_out", mrg, w_o, _w_cols(d, 512, 0), d, 512, F32)
    h1, hn2 = _rms_fwd("resid_rms2", h0, g2, res=mo)
    per = ff_blk // 512
    up, act = _mm_nn("mlp_up", hn2, w_up, pl.BlockSpec((None, d, 512), lambda i, j: (j // per, 0, j % per)),
                     d_ff, 512, BF16, relu2=True)
    dn = _mm_nn("mlp_down", act, w_dn, _w_cols(d_ff, 512, 0), d, 512, F32)
    dh2, dh2b, loss_blk, dg3 = _final("final_loss", h1, dn, tgt, g3, batch)

    dup = _mm_nt("d_mlp_down", [(dh2b, _a_rows(d), w_dn, _w_rows(512, d))], n, d_ff, 512, BF16, relu_bwd_of=up)
    dw_dn = _grad_w("gw_mlp_down", act, dh2b)
    dhn2 = _mm_nt("d_mlp_up", [(dup, _a_rows(ff_blk, g), w_up, pl.BlockSpec((None, 256, ff_blk), lambda i, j, g=g: (g, j, 0)))
                               for g in range(N_DEV)], n, d, 256, F32)
    dw_up = _mm_tn("gw_mlp_up", hn2, lambda a_: 0, d, dup, lambda b_: b_, ff_blk, (N_DEV, d, ff_blk),
                   pl.BlockSpec((None, d, ff_blk), lambda a_, b_, k: (b_, 0, 0)), (1, N_DEV))
    scatter_1 = _exchange_start("scatter_mlp_start", [(dw_dn.reshape(N_DEV, ff_blk, d), True), (dw_up, True)])
    dh1, dg2, dh1b = _rms_bwd("rms2_bwd", dhn2, h1, g2 + scatter_1["token"][0:1, 0:1], dh2, batch, with_bf16=True)

    dm = _mm_nt("d_mix_out", [(dh1b, _a_rows(d), w_o, _w_rows(512, d))], n, d, 512, F32)
    dw_o = _grad_w("gw_mix_out", mrg, dh1b)
    da, dc, dga, dgc, dbco = _merge_bwd("merge_bwd", dm, gates, a, c, b_conv_out)

    do = _mm_nt("d_attn_out", [(da, _a_rows(d), w_ao, _w_rows(512, d))], n, d, 512, BF16)
    dw_ao = _grad_w("gw_attn_out", o, da)
    dc3 = _mm_nt("d_conv_out", [(dc, _a_rows(d), w_co, _w_rows(512, d))], n, d, 512, F32)
    dw_co = _grad_w("gw_conv_out", c3, dc)

    scatter_2 = _exchange_start("scatter_mix_start", [(dw_.reshape(N_DEV, d // N_DEV, d), True)
                                                      for dw_ in (dw_o, dw_ao, dw_co)])
    dc1, dg_ln, db_ln = _ln_silu_bwd("ln_silu_bwd", dc3, c1, conv_ln_gain + scatter_2["token"][0:1, 0:1],
                                     conv_ln_bias)
    dglu_a, dglu_g, dw_dw, db_dw, dbg_a, dbg_g = _glu_conv_bwd("glu_conv_bwd", dc1, glu, b_glu, w_dw, batch)

    dq, dk, dv, dcumr, dcum_q = _attn_bwd("attn_bwd", qkv, o, do, lse, cumr, batch)
    dcum_k = jnp.transpose(dcumr[:, :, :, :2, :], (0, 2, 4, 1, 3)).reshape(n, 2 * n_pairs)
    dcum_k = jnp.pad(dcum_k, ((0, 0), (0, FG_PAD - 2 * n_pairs)))
    dfg, db_fg = _fox_prep_bwd("fox_cumsum_bwd", dcum_k, dcum_q, fg, b_fg, batch)

    segs = [dq, dk, dv, dglu_a, dglu_g, dga, dgc]
    pairs = [(s_, _a_rows(SEG), w_pad, _w_rows(256, SEG, i)) for i, s_ in enumerate(segs)]
    pairs.append((dfg, _a_rows(FG_PAD), w_pad, _w_rows(256, FG_PAD, 7 * SEG // FG_PAD)))
    dhn1 = _mm_nt("d_proj_in", pairs, n, d, 256, F32)
    gw_seg = [_grad_w("gw_in_%d" % i, hn1, s_) for i, s_ in enumerate(segs)]
    gw_fg = _grad_w("gw_in_fg", hn1, dfg)
    dh0, dg1, dmeta = _rms_bwd("rms1_bwd", dhn1, h0, g1, dh1, batch, with_meta=True)

    grad_x = dh0.reshape(batch, t, d)[:, N_META:, :]
    gw_src = gw_seg + [gw_fg]
    dw_in = jnp.stack([jnp.concatenate([gw_src[src][:, c0:c1]
                                        for src, c0, c1, _ in _w_in_pieces(p * shard_w, (p + 1) * shard_w, n_fg)], axis=1)
                       for p in range(N_DEV)])

    small_g = dict(norm_mix_gain=dg1, b_forget=db_fg[:, :n_fg], b_glu=jnp.concatenate([dbg_a, dbg_g], axis=1),
                   conv_dw_b=db_dw, conv_ln_gain=dg_ln, conv_ln_bias=db_ln, b_conv_out=dbco, norm_mlp_gain=dg2,
                   final_norm_gain=dg3)
    scatter_3 = _exchange_start("scatter_in_start", [
        (dw_in, True), (_shards_from_cols(dmeta), True), (_shards_from_cols(dw_dw), True),
        (_cat_small(small_g), False), (loss_blk[0:1, :], False)])

    grads, deltas, new_m, new_v = {}, {}, {}, {}

    def update(k, parts):
        shp = weights[k].shape
        w2 = lambda arr: arr.reshape(parts.shape[1:])
        res_ = _adamw("adamw_" + k, parts, w2(weights[k]), w2(mom_m[k]), w2(mom_v[k]))
        grads[k], deltas[k], new_m[k], new_v[k] = [r.reshape(shp) for r in res_]

    for k, parts in zip(("w_mlp_down", "w_mlp_up"), _exchange_wait("scatter_mlp_wait", scatter_1, scatter_3["token"])):
        update(k, parts)
    for k, parts in zip(("w_out", "w_attn_out", "w_conv_out"),
                        _exchange_wait("scatter_mix_wait", scatter_2, deltas["w_mlp_up"])):
        update(k, parts)
    reduced = _exchange_wait("scatter_in_wait", scatter_3, deltas["w_conv_out"])
    loss = jnp.sum(reduced.pop()[:, 0, 0])
    for k, parts in zip(("w_in", "meta_tokens", "conv_dw_w"), reduced[:-1]):
        update(k, parts)
    res = _adamw("adamw_small", reduced[-1], _cat_small(weights), _cat_small(mom_m), _cat_small(mom_v))
    shapes = {k: weights[k].shape for k in SMALL}
    for dst, r in zip((grads, deltas, new_m, new_v), res):
        dst.update(_split_small(r, shapes))

    return (loss, grad_x, *[grads[k] for k in names], *[deltas[k] for k in names],
            *[new_m[k] for k in names], *[new_v[k] for k in names])
```

```python
import functools

import jax
import jax.numpy as jnp
from jax import lax
from jax.experimental import pallas as pl
from jax.experimental.pallas import tpu as pltpu

F32, BF16 = jnp.float32, jnp.bfloat16
N_DEV = 8
N_META = 16
HEAD_DIM = 64
LANES = 128
CONV_W = 31
RMS_EPS = 1e-6
LN_EPS = 1e-5
ROW_TILE = 688
MM_TM = 2 * ROW_TILE
SEG = 1024
FG_PAD = 128
VMEM_LIMIT = 56 * 1024 * 1024
ADAM_LR, ADAM_B1, ADAM_B2, ADAM_EPS, ADAM_WD, ADAM_STEP = 0.001, 0.9, 0.999, 1e-08, 0.01, 10
NEG = -1e30

SMALL = ("norm_mix_gain", "b_forget", "b_glu", "conv_dw_b", "conv_ln_gain", "conv_ln_bias", "b_conv_out",
         "norm_mlp_gain", "final_norm_gain")
SMALL_W = {"norm_mix_gain": 1024, "b_forget": 128, "b_glu": 2048, "conv_dw_b": 1024, "conv_ln_gain": 1024,
           "conv_ln_bias": 1024, "b_conv_out": 1024, "norm_mlp_gain": 1024, "final_norm_gain": 1024}
SMALL_N = {"norm_mix_gain": 1024, "b_forget": 16, "b_glu": 2048, "conv_dw_b": 1024, "conv_ln_gain": 1024,
           "conv_ln_bias": 1024, "b_conv_out": 1024, "norm_mlp_gain": 1024, "final_norm_gain": 1024}


def _params(sem=None):
    return pltpu.CompilerParams(dimension_semantics=sem, vmem_limit_bytes=VMEM_LIMIT)


def _sigmoid(x):
    return 1.0 / (1.0 + jnp.exp(-x))


def _dot_nt(a, b):
    return lax.dot_general(a, b, (((1,), (1,)), ((), ())), preferred_element_type=F32)


def _dot_tn(a, b):
    return lax.dot_general(a, b, (((0,), (0,)), ((), ())), preferred_element_type=F32)


HBM_SPEC = pl.BlockSpec(memory_space=pltpu.HBM)
SEM_SPEC = pl.BlockSpec(memory_space=pltpu.SEMAPHORE)
DATAFLOW = pltpu.SideEffectType.DATAFLOW_SIDE_EFFECTING


def _device_index():
    return 4 * lax.axis_index("x") + 2 * lax.axis_index("y") + lax.axis_index("c")


def _peers():
    x, y, c = lax.axis_index("x"), lax.axis_index("y"), lax.axis_index("c")
    out = []
    for k in range(1, N_DEV):
        px = 1 - x if k & 4 else x
        py = 1 - y if k & 2 else y
        pc = 1 - c if k & 1 else c
        out.append((k, (px, py, pc), 4 * px + 2 * py + pc))
    return out


def _peer_copy(per_dest, src_ref, land_ref, send_sems, recv_sems, a, k, dev, peer):
    src = src_ref.at[peer] if per_dest else src_ref
    return pltpu.make_async_remote_copy(
        src_ref=src, dst_ref=land_ref.at[_device_index()], send_sem=send_sems.at[a * (N_DEV - 1) + k - 1],
        recv_sem=recv_sems.at[a * (N_DEV - 1) + k - 1], device_id=dev, device_id_type=pl.DeviceIdType.MESH)


def _exchange_start(name, items):
    n = len(items)
    per_dest = [it[1] for it in items]

    def body(*refs):
        srcs, lands = refs[:n], refs[n:2 * n]
        send_sems, recv_sems, token = refs[2 * n], refs[2 * n + 1], refs[-1]
        for a in range(n):
            for k, dev, peer in _peers():
                _peer_copy(per_dest[a], srcs[a], lands[a], send_sems, recv_sems, a, k, dev, peer).start()
        token[...] = jnp.zeros(token.shape, F32)

    srcs = [pltpu.with_memory_space_constraint(it[0], pltpu.HBM) for it in items]
    lands = []
    for arr, pd in items:
        shp = arr.shape if pd else (N_DEV,) + arr.shape
        lands.append(pltpu.with_memory_space_constraint(lax.empty(shp, arr.dtype), pltpu.HBM))
    sems = pltpu.SemaphoreType.DMA((n * (N_DEV - 1),))
    res = pl.pallas_call(
        body, name=name,
        out_shape=(sems, sems, *[pltpu.HBM(a_.shape, a_.dtype) for a_ in srcs + lands],
                   jax.ShapeDtypeStruct((8, 128), F32)),
        in_specs=[HBM_SPEC] * (2 * n),
        out_specs=(SEM_SPEC, SEM_SPEC, *[HBM_SPEC] * (2 * n), pl.BlockSpec(memory_space=pltpu.VMEM)),
        input_output_aliases={i: 2 + i for i in range(2 * n)},
        compiler_params=pltpu.CompilerParams(has_side_effects=DATAFLOW),
    )(*srcs, *lands)
    return dict(per_dest=per_dest, send=res[0], recv=res[1], srcs=list(res[2:2 + n]),
                lands=list(res[2 + n:2 + 2 * n]), token=res[-1])


def _exchange_wait(name, started, after):
    per_dest = started["per_dest"]
    n = len(per_dest)

    def body(*refs):
        srcs, lands = refs[:n], refs[n:2 * n]
        send_sems, recv_sems = refs[2 * n], refs[2 * n + 1]
        for a in range(n):
            for k, dev, peer in _peers():
                cp = _peer_copy(per_dest[a], srcs[a], lands[a], send_sems, recv_sems, a, k, dev, peer)
                cp.wait_send()
                cp.wait_recv()

    bufs = started["srcs"] + started["lands"]
    res = pl.pallas_call(
        body, name=name, out_shape=tuple(pltpu.HBM(b_.shape, b_.dtype) for b_ in bufs),
        in_specs=[HBM_SPEC] * (2 * n) + [SEM_SPEC, SEM_SPEC, pl.BlockSpec(memory_space=pl.ANY)],
        out_specs=tuple([HBM_SPEC] * (2 * n)), input_output_aliases={i: i for i in range(2 * n)},
        compiler_params=pltpu.CompilerParams(has_side_effects=DATAFLOW),
    )(*bufs, started["send"], started["recv"], after)
    me = _device_index()
    out = []
    for pd, src, land in zip(per_dest, res[:n], res[n:]):
        own = lax.dynamic_index_in_dim(src, me, 0, keepdims=True) if pd else src[None]
        out.append(lax.dynamic_update_slice_in_dim(land, own, me, axis=0))
    return out


def _mm_nn(name, x, w, w_spec, n_out, tn, out_dtype, relu2=False, tm=MM_TM):
    m, k = x.shape

    def body(x_ref, w_ref, *outs):
        acc = jnp.dot(x_ref[...], w_ref[...], preferred_element_type=F32)
        outs[0][...] = acc.astype(outs[0].dtype)
        if relu2:
            r = jnp.maximum(acc, 0.0)
            outs[1][...] = (r * r).astype(outs[1].dtype)

    o_spec = pl.BlockSpec((tm, tn), lambda i, j: (i, j))
    shapes = [jax.ShapeDtypeStruct((m, n_out), out_dtype)]
    if relu2:
        shapes.append(jax.ShapeDtypeStruct((m, n_out), BF16))
    res = pl.pallas_call(
        body, name=name, out_shape=shapes, grid=(m // tm, n_out // tn),
        in_specs=[pl.BlockSpec((tm, k), lambda i, j: (i, 0)), w_spec],
        out_specs=[o_spec] * len(shapes), compiler_params=_params(("parallel", "parallel")),
    )(x, w)
    return res if relu2 else res[0]


def _mm_nt(name, pairs, m, n_out, tn, out_dtype, relu_bwd_of=None, tm=MM_TM):
    np_ = len(pairs)

    def body(*refs):
        acc = None
        for p in range(np_):
            d = _dot_nt(refs[2 * p][...], refs[2 * p + 1][...])
            acc = d if acc is None else acc + d
        if relu_bwd_of is not None:
            acc = acc * (2.0 * jnp.maximum(refs[2 * np_][...].astype(F32), 0.0))
        refs[-1][...] = acc.astype(refs[-1].dtype)

    o_spec = pl.BlockSpec((tm, tn), lambda i, j: (i, j))
    operands, specs = [], []
    for a, a_spec, w, w_spec in pairs:
        operands += [a, w]
        specs += [a_spec, w_spec]
    if relu_bwd_of is not None:
        operands.append(relu_bwd_of)
        specs.append(o_spec)
    return pl.pallas_call(
        body, name=name, out_shape=jax.ShapeDtypeStruct((m, n_out), out_dtype), grid=(m // tm, n_out // tn),
        in_specs=specs, out_specs=o_spec, compiler_params=_params(("parallel", "parallel")),
    )(*operands)


def _mm_tn(name, x, x_col, ta, dy, dy_col, tb, out_shape, out_spec, grid_ab):
    m = x.shape[0]
    tm = MM_TM
    nk = m // tm

    def body(x_ref, dy_ref, o_ref, acc_ref):
        k = pl.program_id(2)

        @pl.when(k == 0)
        def _():
            acc_ref[...] = jnp.zeros(acc_ref.shape, F32)

        acc_ref[...] += _dot_tn(x_ref[...], dy_ref[...])

        @pl.when(k == nk - 1)
        def _():
            o_ref[...] = acc_ref[...].astype(BF16)

    return pl.pallas_call(
        body, name=name, out_shape=jax.ShapeDtypeStruct(out_shape, BF16), grid=grid_ab + (nk,),
        in_specs=[pl.BlockSpec((tm, ta), lambda a, b, k: (k, x_col(a))),
                  pl.BlockSpec((tm, tb), lambda a, b, k: (k, dy_col(b)))],
        out_specs=out_spec, scratch_shapes=[pltpu.VMEM((ta, tb), F32)],
        compiler_params=_params(("parallel", "parallel", "arbitrary")),
    )(x, dy)


def _w_cols(k, tn, off_blocks):
    return pl.BlockSpec((k, tn), lambda i, j: (0, off_blocks + j))


def _a_rows(kw, col_block=0, tm=MM_TM):
    return pl.BlockSpec((tm, kw), lambda i, j: (i, col_block))


def _w_rows(tn, kw, col_block=0):
    return pl.BlockSpec((tn, kw), lambda i, j: (j, col_block))


def _grad_w(name, x, dy):
    nb = dy.shape[1]
    tb = min(nb, 1024)
    return _mm_tn(name, x, lambda a: a, 1024, dy, lambda b: b, tb, (x.shape[1], nb),
                  pl.BlockSpec((1024, tb), lambda a, b, k: (a, b)), (x.shape[1] // 1024, nb // tb))


def _row_spec(width):
    return pl.BlockSpec((ROW_TILE, width), lambda i: (i, 0))


def _vec_spec(width):
    return pl.BlockSpec((1, width), lambda i: (0, 0))


def _rms_fwd(name, h, g, res=None):
    n, d = h.shape

    def body(*refs):
        if res is None:
            h_ref, g_ref, hn_ref = refs
            hv = h_ref[...]
        else:
            h_ref, r_ref, g_ref, hs_ref, hn_ref = refs
            hv = h_ref[...] + r_ref[...]
            hs_ref[...] = hv
        r = lax.rsqrt(jnp.mean(hv * hv, axis=-1, keepdims=True) + RMS_EPS)
        hn_ref[...] = (hv * r * g_ref[...]).astype(BF16)

    ins = [h, g] if res is None else [h, res, g]
    in_specs = [_row_spec(d), _vec_spec(d)] if res is None else [_row_spec(d), _row_spec(d), _vec_spec(d)]
    hn_shape = jax.ShapeDtypeStruct((n, d), BF16)
    if res is None:
        out_shape, out_specs = hn_shape, _row_spec(d)
    else:
        out_shape, out_specs = [jax.ShapeDtypeStruct((n, d), F32), hn_shape], [_row_spec(d), _row_spec(d)]
    return pl.pallas_call(body, name=name, out_shape=out_shape, grid=(n // ROW_TILE,), in_specs=in_specs,
                          out_specs=out_specs, compiler_params=_params(("parallel",)))(*ins)


def _rms_bwd(name, dhn, h, g, dres, batch, with_bf16=False, with_meta=False):
    n, d = h.shape
    t = n // batch
    nt = t // ROW_TILE

    def body(dhn_ref, h_ref, g_ref, dres_ref, *outs):
        first = (pl.program_id(0) == 0) & (pl.program_id(1) == 0)
        hv = h_ref[...]
        r = lax.rsqrt(jnp.mean(hv * hv, axis=-1, keepdims=True) + RMS_EPS)
        nrm = hv * r
        dn = dhn_ref[...] * g_ref[...]
        dh = dres_ref[...] + r * (dn - nrm * jnp.mean(dn * nrm, axis=-1, keepdims=True))
        outs[0][...] = dh
        dg_ref = outs[1]

        @pl.when(first)
        def _():
            dg_ref[...] = jnp.zeros(dg_ref.shape, F32)

        dg_ref[...] += jnp.sum(dhn_ref[...] * nrm, axis=0, keepdims=True)
        nxt = 2
        if with_bf16:
            outs[nxt][...] = dh.astype(BF16)
            nxt += 1
        if with_meta:
            meta_ref = outs[nxt]

            @pl.when(first)
            def _():
                meta_ref[...] = jnp.zeros(meta_ref.shape, F32)

            @pl.when(pl.program_id(1) == 0)
            def _():
                meta_ref[...] += dh[0:N_META, :]

    row = pl.BlockSpec((ROW_TILE, d), lambda b, j: (b * nt + j, 0))
    vec = pl.BlockSpec((1, d), lambda b, j: (0, 0))
    shapes = [jax.ShapeDtypeStruct((n, d), F32), jax.ShapeDtypeStruct((1, d), F32)]
    specs = [row, vec]
    if with_bf16:
        shapes.append(jax.ShapeDtypeStruct((n, d), BF16))
        specs.append(row)
    if with_meta:
        shapes.append(jax.ShapeDtypeStruct((N_META, d), F32))
        specs.append(pl.BlockSpec((N_META, d), lambda b, j: (0, 0)))
    return pl.pallas_call(body, name=name, out_shape=shapes, grid=(batch, nt), in_specs=[row, row, vec, row],
                          out_specs=specs, compiler_params=_params(("arbitrary", "arbitrary")))(dhn, h, g, dres)


def _final(name, h1, dn, tgt, g, batch):
    n, d = h1.shape
    t = n // batch
    nt = t // ROW_TILE

    def body(h1_ref, dn_ref, tgt_ref, g_ref, dh_ref, dhb_ref, loss_ref, dg_ref):
        first = (pl.program_id(0) == 0) & (pl.program_id(1) == 0)
        hv = h1_ref[...] + dn_ref[...]
        r = lax.rsqrt(jnp.mean(hv * hv, axis=-1, keepdims=True) + RMS_EPS)
        nrm = hv * r
        gv = g_ref[...]
        pos = pl.program_id(1) * ROW_TILE + lax.broadcasted_iota(jnp.int32, (ROW_TILE, 1), 0)
        diff = jnp.where(pos >= N_META, nrm * gv - tgt_ref[...], 0.0)
        dy = diff * (1.0 / d)

        @pl.when(first)
        def _():
            loss_ref[...] = jnp.zeros(loss_ref.shape, F32)
            dg_ref[...] = jnp.zeros(dg_ref.shape, F32)

        loss_ref[...] += jnp.full(loss_ref.shape, 0.5 / d, F32) * jnp.sum(diff * diff)
        dg_ref[...] += jnp.sum(dy * nrm, axis=0, keepdims=True)
        dng = dy * gv
        dh = r * (dng - nrm * jnp.mean(dng * nrm, axis=-1, keepdims=True))
        dh_ref[...] = dh
        dhb_ref[...] = dh.astype(BF16)

    row = pl.BlockSpec((ROW_TILE, d), lambda b, j: (b * nt + j, 0))
    vec = pl.BlockSpec((1, d), lambda b, j: (0, 0))
    return pl.pallas_call(
        body, name=name, grid=(batch, nt), in_specs=[row, row, row, vec],
        out_shape=[jax.ShapeDtypeStruct((n, d), F32), jax.ShapeDtypeStruct((n, d), BF16),
                   jax.ShapeDtypeStruct((8, 128), F32), jax.ShapeDtypeStruct((1, d), F32)],
        out_specs=[row, row, pl.BlockSpec((8, 128), lambda b, j: (0, 0)), vec],
        compiler_params=_params(("arbitrary", "arbitrary")))(h1, dn, tgt, g)


def _ln_silu_fwd(name, c1, g, b):
    n, d = c1.shape

    def body(c_ref, g_ref, b_ref, o_ref):
        xv = c_ref[...]
        xc = xv - jnp.mean(xv, axis=-1, keepdims=True)
        rstd = lax.rsqrt(jnp.mean(xc * xc, axis=-1, keepdims=True) + LN_EPS)
        c2 = xc * rstd * g_ref[...] + b_ref[...]
        o_ref[...] = (c2 * _sigmoid(c2)).astype(BF16)

    return pl.pallas_call(body, name=name, out_shape=jax.ShapeDtypeStruct((n, d), BF16), grid=(n // ROW_TILE,),
                          in_specs=[_row_spec(d), _vec_spec(d), _vec_spec(d)], out_specs=_row_spec(d),
                          compiler_params=_params(("parallel",)))(c1, g, b)


def _ln_silu_bwd(name, dc3, c1, g, b):
    n, d = c1.shape

    def body(d_ref, c_ref, g_ref, b_ref, dc1_ref, dg_ref, db_ref):
        xv = c_ref[...]
        xc = xv - jnp.mean(xv, axis=-1, keepdims=True)
        rstd = lax.rsqrt(jnp.mean(xc * xc, axis=-1, keepdims=True) + LN_EPS)
        xh = xc * rstd
        c2 = xh * g_ref[...] + b_ref[...]
        s = _sigmoid(c2)
        dc2 = d_ref[...] * (s * (1.0 + c2 * (1.0 - s)))

        @pl.when(pl.program_id(0) == 0)
        def _():
            dg_ref[...] = jnp.zeros(dg_ref.shape, F32)
            db_ref[...] = jnp.zeros(db_ref.shape, F32)

        dg_ref[...] += jnp.sum(dc2 * xh, axis=0, keepdims=True)
        db_ref[...] += jnp.sum(dc2, axis=0, keepdims=True)
        dxh = dc2 * g_ref[...]
        dc1_ref[...] = rstd * (dxh - jnp.mean(dxh, axis=-1, keepdims=True)
                               - xh * jnp.mean(dxh * xh, axis=-1, keepdims=True))

    return pl.pallas_call(
        body, name=name, grid=(n // ROW_TILE,),
        out_shape=[jax.ShapeDtypeStruct((n, d), F32), jax.ShapeDtypeStruct((1, d), F32),
                   jax.ShapeDtypeStruct((1, d), F32)],
        in_specs=[_row_spec(d), _row_spec(d), _vec_spec(d), _vec_spec(d)],
        out_specs=[_row_spec(d), _vec_spec(d), _vec_spec(d)],
        compiler_params=_params(("arbitrary",)))(dc3, c1, g, b)


MERGE_TC = 512


def _merge_fwd(name, gates, a, c, b_co):
    n, d = a.shape
    nc = d // MERGE_TC

    def body(ga_ref, gc_ref, a_ref, c_ref, b_ref, m_ref):
        m = _sigmoid(ga_ref[...]) * a_ref[...] + _sigmoid(gc_ref[...]) * (c_ref[...] + b_ref[...])
        m_ref[...] = m.astype(BF16)

    blk = lambda off: pl.BlockSpec((ROW_TILE, MERGE_TC), lambda i, j: (i, off + j))
    return pl.pallas_call(
        body, name=name, out_shape=jax.ShapeDtypeStruct((n, d), BF16), grid=(n // ROW_TILE, nc),
        in_specs=[blk(0), blk(nc), blk(0), blk(0), pl.BlockSpec((1, MERGE_TC), lambda i, j: (0, j))],
        out_specs=blk(0), compiler_params=_params(("parallel", "parallel")))(gates, gates, a, c, b_co)


def _merge_bwd(name, dm, gates, a, c, b_co):
    n, d = a.shape
    nc = d // MERGE_TC

    def body(dm_ref, ga_ref, gc_ref, a_ref, c_ref, b_ref, da_ref, dc_ref, dga_ref, dgc_ref, dbco_ref):
        dmv = dm_ref[...]
        sa, sc = _sigmoid(ga_ref[...]), _sigmoid(gc_ref[...])
        dc = dmv * sc
        da_ref[...] = (dmv * sa).astype(BF16)
        dc_ref[...] = dc.astype(BF16)
        dga_ref[...] = (dmv * a_ref[...] * sa * (1.0 - sa)).astype(BF16)
        dgc_ref[...] = (dmv * (c_ref[...] + b_ref[...]) * sc * (1.0 - sc)).astype(BF16)

        @pl.when(pl.program_id(1) == 0)
        def _():
            dbco_ref[...] = jnp.zeros(dbco_ref.shape, F32)

        dbco_ref[...] += jnp.sum(dc, axis=0, keepdims=True)

    blk = lambda off: pl.BlockSpec((ROW_TILE, MERGE_TC), lambda j, i: (i, off + j))
    vec = pl.BlockSpec((1, MERGE_TC), lambda j, i: (0, j))
    act = jax.ShapeDtypeStruct((n, d), BF16)
    return pl.pallas_call(
        body, name=name, grid=(nc, n // ROW_TILE),
        out_shape=[act, act, act, act, jax.ShapeDtypeStruct((1, d), F32)],
        in_specs=[blk(0), blk(0), blk(nc), blk(0), blk(0), vec],
        out_specs=[blk(0), blk(0), blk(0), blk(0), vec],
        compiler_params=_params(("parallel", "arbitrary")))(dm, gates, gates, a, c, b_co)


CONV_TC = 128
CONV_HALO = 32


def _glu_conv_fwd(name, glu, b_glu, w_dw, b_dw, batch):
    n, c2 = glu.shape
    c = c2 // 2
    t = n // batch
    nc = c // CONV_TC

    def body(a_ref, gt_ref, ba_ref, bg_ref, w_ref, bdw_ref, o_ref, pad_ref):
        u = (a_ref[...] + ba_ref[...]) * _sigmoid(gt_ref[...] + bg_ref[...])
        pad_ref[0:CONV_HALO, :] = jnp.zeros((CONV_HALO, CONV_TC), F32)
        pad_ref[CONV_HALO:CONV_HALO + t, :] = u
        acc = jnp.zeros((t, CONV_TC), F32) + bdw_ref[...]
        for j in range(CONV_W):
            off = CONV_HALO - (CONV_W - 1) + j
            acc = acc + w_ref[j:j + 1, :] * pad_ref[off:off + t, :]
        o_ref[...] = acc

    seq = lambda off: pl.BlockSpec((t, CONV_TC), lambda b, j: (b, off + j))
    vec = lambda off: pl.BlockSpec((1, CONV_TC), lambda b, j: (0, off + j))
    return pl.pallas_call(
        body, name=name, out_shape=jax.ShapeDtypeStruct((n, c), F32), grid=(batch, nc),
        in_specs=[seq(0), seq(nc), vec(0), vec(nc), pl.BlockSpec((CONV_W, CONV_TC), lambda b, j: (0, j)), vec(0)],
        out_specs=seq(0), scratch_shapes=[pltpu.VMEM((t + CONV_HALO, CONV_TC), F32)],
        compiler_params=_params(("parallel", "parallel")))(glu, glu, b_glu, b_glu, w_dw, b_dw)


def _glu_conv_bwd(name, dc1, glu, b_glu, w_dw, batch):
    n, c2 = glu.shape
    c = c2 // 2
    t = n // batch
    nc = c // CONV_TC

    def body(d_ref, a_ref, gt_ref, ba_ref, bg_ref, w_ref, dga_ref, dgg_ref, dw_ref, dbdw_ref, dba_ref, dbg_ref,
             padu_ref, padd_ref):
        av = a_ref[...] + ba_ref[...]
        sg = _sigmoid(gt_ref[...] + bg_ref[...])
        dc = d_ref[...]
        padu_ref[0:CONV_HALO, :] = jnp.zeros((CONV_HALO, CONV_TC), F32)
        padu_ref[CONV_HALO:CONV_HALO + t, :] = av * sg
        padd_ref[0:t, :] = dc
        padd_ref[t:t + CONV_HALO, :] = jnp.zeros((CONV_HALO, CONV_TC), F32)

        @pl.when(pl.program_id(1) == 0)
        def _():
            dw_ref[...] = jnp.zeros(dw_ref.shape, F32)
            dbdw_ref[...] = jnp.zeros(dbdw_ref.shape, F32)
            dba_ref[...] = jnp.zeros(dba_ref.shape, F32)
            dbg_ref[...] = jnp.zeros(dbg_ref.shape, F32)

        du = jnp.zeros((t, CONV_TC), F32)
        for j in range(CONV_W):
            back = CONV_W - 1 - j
            du = du + w_ref[j:j + 1, :] * padd_ref[back:back + t, :]
            off = CONV_HALO - (CONV_W - 1) + j
            dw_ref[j:j + 1, :] += jnp.sum(dc * padu_ref[off:off + t, :], axis=0, keepdims=True)
        dga = du * sg
        dgg = du * av * sg * (1.0 - sg)
        dga_ref[...] = dga.astype(BF16)
        dgg_ref[...] = dgg.astype(BF16)
        dbdw_ref[...] += jnp.sum(dc, axis=0, keepdims=True)
        dba_ref[...] += jnp.sum(dga, axis=0, keepdims=True)
        dbg_ref[...] += jnp.sum(dgg, axis=0, keepdims=True)

    seq = lambda off: pl.BlockSpec((t, CONV_TC), lambda j, b: (b, off + j))
    vec = lambda off: pl.BlockSpec((1, CONV_TC), lambda j, b: (0, off + j))
    wsp = pl.BlockSpec((CONV_W, CONV_TC), lambda j, b: (0, j))
    act = jax.ShapeDtypeStruct((n, c), BF16)
    v = jax.ShapeDtypeStruct((1, c), F32)
    return pl.pallas_call(
        body, name=name, grid=(nc, batch),
        out_shape=[act, act, jax.ShapeDtypeStruct((CONV_W, c), F32), v, v, v],
        in_specs=[seq(0), seq(0), seq(nc), vec(0), vec(nc), wsp],
        out_specs=[seq(0), seq(0), wsp, vec(0), vec(0), vec(0)],
        scratch_shapes=[pltpu.VMEM((t + CONV_HALO, CONV_TC), F32), pltpu.VMEM((t + CONV_HALO, CONV_TC), F32)],
        compiler_params=_params(("parallel", "arbitrary")))(dc1, glu, glu, b_glu, b_glu, w_dw)


def _split3(x):
    hi = x.astype(BF16)
    r = x - hi.astype(F32)
    mid = r.astype(BF16)
    lo = (r - mid.astype(F32)).astype(BF16)
    return hi, mid, lo


def _tri_matmul(tri, x):
    hi, mid, lo = _split3(x)
    dot = lambda v: jnp.dot(tri, v, preferred_element_type=F32)
    return dot(hi) + dot(mid) + dot(lo)


def _fox_prep_fwd(name, fg, b_fg, batch):
    n, w = fg.shape
    t = n // batch
    nq = t // ROW_TILE

    def body(fg_ref, b_ref, cum_ref):
        row = lax.broadcasted_iota(jnp.int32, (ROW_TILE, ROW_TILE), 0)
        col = lax.broadcasted_iota(jnp.int32, (ROW_TILE, ROW_TILE), 1)
        tri = (row >= col).astype(BF16)
        for k in range(nq):
            rows = slice(k * ROW_TILE, (k + 1) * ROW_TILE)
            z = fg_ref[rows, :] + b_ref[...]
            logf = jnp.minimum(z, 0.0) - jnp.log(1.0 + jnp.exp(-jnp.abs(z)))
            cum = _tri_matmul(tri, logf)
            if k > 0:
                cum = cum + cum_ref[k * ROW_TILE - 1:k * ROW_TILE, :]
            cum_ref[rows, :] = cum

    seq = pl.BlockSpec((t, w), lambda b: (b, 0))
    return pl.pallas_call(body, name=name, out_shape=jax.ShapeDtypeStruct((n, w), F32), grid=(batch,),
                          in_specs=[seq, pl.BlockSpec((1, w), lambda b: (0, 0))], out_specs=seq,
                          compiler_params=_params(("parallel",)))(fg, b_fg)


def _fox_prep_bwd(name, dcum_k, dcum_q, fg, b_fg, batch):
    n, w = fg.shape
    t = n // batch
    nq = t // ROW_TILE

    def body(dk_ref, dq_ref, fg_ref, b_ref, dfg_ref, db_ref, rev_ref):
        row = lax.broadcasted_iota(jnp.int32, (ROW_TILE, ROW_TILE), 0)
        col = lax.broadcasted_iota(jnp.int32, (ROW_TILE, ROW_TILE), 1)
        tri = (col >= row).astype(BF16)

        @pl.when(pl.program_id(0) == 0)
        def _():
            db_ref[...] = jnp.zeros(db_ref.shape, F32)

        for k in reversed(range(nq)):
            rows = slice(k * ROW_TILE, (k + 1) * ROW_TILE)
            dlog = _tri_matmul(tri, dk_ref[rows, :] + dq_ref[rows, :])
            if k < nq - 1:
                dlog = dlog + rev_ref[(k + 1) * ROW_TILE:(k + 1) * ROW_TILE + 1, :]
            rev_ref[rows, :] = dlog
            dfg = dlog * _sigmoid(-(fg_ref[rows, :] + b_ref[...]))
            dfg_ref[rows, :] = dfg.astype(BF16)
            db_ref[...] += jnp.sum(dfg, axis=0, keepdims=True)

    seq = pl.BlockSpec((t, w), lambda b: (b, 0))
    vec = pl.BlockSpec((1, w), lambda b: (0, 0))
    return pl.pallas_call(
        body, name=name, grid=(batch,),
        out_shape=[jax.ShapeDtypeStruct((n, w), BF16), jax.ShapeDtypeStruct((1, w), F32)],
        in_specs=[seq, seq, seq, vec], out_specs=[seq, vec], scratch_shapes=[pltpu.VMEM((t, w), F32)],
        compiler_params=_params(("arbitrary",)))(dcum_k, dcum_q, fg, b_fg)


def _head_masks(x):
    lane = lax.broadcasted_iota(jnp.int32, x.shape, 1)
    zero = jnp.zeros(x.shape, x.dtype)
    return jnp.where(lane < HEAD_DIM, x, zero), jnp.where(lane >= HEAD_DIM, x, zero)


def _attn_specs(t, nq):
    qkv = lambda off: pl.BlockSpec((t, LANES), lambda b, h: (b, off + h))
    cumr = pl.BlockSpec((None, None, nq, 8, ROW_TILE), lambda b, h: (b, h, 0, 0, 0))
    return qkv, cumr


def _attn_fwd(name, qkv, cumr, batch):
    n, w3 = qkv.shape
    w = w3 // 3
    t = n // batch
    nq = t // ROW_TILE
    n_pairs = w // LANES
    tq = ROW_TILE

    def body(q_ref, k_ref, v_ref, cr_ref, o_ref, lse_ref):
        row = lax.broadcasted_iota(jnp.int32, (tq, tq), 0)
        col = lax.broadcasted_iota(jnp.int32, (tq, tq), 1)
        causal = row >= col
        lane = lax.broadcasted_iota(jnp.int32, (tq, LANES), 1)
        for i in range(nq):
            rows = slice(i * tq, (i + 1) * tq)
            qs = _head_masks(q_ref[rows, :] * 0.125)
            outs, lses = [], []
            for hh in range(2):
                m = jnp.full((tq, 1), NEG, F32)
                l = jnp.zeros((tq, 1), F32)
                acc = jnp.zeros((tq, LANES), F32)
                for j in range(i + 1):
                    cols = slice(j * tq, (j + 1) * tq)
                    s = _dot_nt(qs[hh], k_ref[cols, :]) - cr_ref[j, hh:hh + 1, :]
                    if j == i:
                        s = jnp.where(causal, s, NEG)
                    m_new = jnp.maximum(m, jnp.max(s, axis=1, keepdims=True))
                    alpha = jnp.exp(m - m_new)
                    p = jnp.exp(s - m_new)
                    l = alpha * l + jnp.sum(p, axis=1, keepdims=True)
                    acc = alpha * acc + jnp.dot(p.astype(BF16), v_ref[cols, :], preferred_element_type=F32)
                    m = m_new
                outs.append(acc / l)
                lses.append(m + jnp.log(l))
            o_ref[rows, :] = jnp.where(lane < HEAD_DIM, outs[0], outs[1]).astype(BF16)
            lse_ref[rows, :] = jnp.where(lane < HEAD_DIM, lses[0], lses[1])

    qkv_spec, cumr_spec = _attn_specs(t, nq)
    return pl.pallas_call(
        body, name=name, grid=(batch, n_pairs),
        out_shape=[jax.ShapeDtypeStruct((n, w), BF16), jax.ShapeDtypeStruct((n, w), F32)],
        in_specs=[qkv_spec(0), qkv_spec(n_pairs), qkv_spec(2 * n_pairs), cumr_spec],
        out_specs=[qkv_spec(0), qkv_spec(0)],
        compiler_params=_params(("parallel", "parallel")))(qkv, qkv, qkv, cumr)


def _attn_bwd(name, qkv, o, do, lse, cumr, batch):
    n, w3 = qkv.shape
    w = w3 // 3
    t = n // batch
    nq = t // ROW_TILE
    n_pairs = w // LANES
    tq = ROW_TILE

    def body(q_ref, k_ref, v_ref, o_ref, do_ref, lse_ref, cr_ref, dq_ref, dk_ref, dv_ref, dcr_ref, dcq_ref,
             dk_acc, dv_acc):
        pair = pl.program_id(1)
        row = lax.broadcasted_iota(jnp.int32, (tq, tq), 0)
        col = lax.broadcasted_iota(jnp.int32, (tq, tq), 1)
        causal = row >= col
        lane = lax.broadcasted_iota(jnp.int32, (tq, LANES), 1)
        dk_acc[...] = jnp.zeros(dk_acc.shape, F32)
        dv_acc[...] = jnp.zeros(dv_acc.shape, F32)
        dcr_ref[...] = jnp.zeros(dcr_ref.shape, F32)

        @pl.when(pair == 0)
        def _():
            dcq_ref[...] = jnp.zeros(dcq_ref.shape, F32)

        for i in range(nq):
            rows = slice(i * tq, (i + 1) * tq)
            qs = _head_masks(q_ref[rows, :] * 0.125)
            dos = _head_masks(do_ref[rows, :])
            dq = jnp.zeros((tq, LANES), F32)
            dcq = []
            for hh in range(2):
                row_sum = jnp.zeros((tq, 1), F32)
                lse = lse_ref[rows, hh * HEAD_DIM:hh * HEAD_DIM + 1]
                delta = jnp.sum(dos[hh].astype(F32) * o_ref[rows, :].astype(F32), axis=1, keepdims=True)
                for j in range(i + 1):
                    cols = slice(j * tq, (j + 1) * tq)
                    s = _dot_nt(qs[hh], k_ref[cols, :]) - cr_ref[j, hh:hh + 1, :]
                    p = jnp.exp(s - lse)
                    if j == i:
                        p = jnp.where(causal, p, 0.0)
                    dp = _dot_nt(dos[hh], v_ref[cols, :])
                    ds = p * (dp - delta)
                    pb, dsb = p.astype(BF16), ds.astype(BF16)
                    km = _head_masks(k_ref[cols, :])[hh]
                    dv_acc[cols, :] += _dot_tn(pb, dos[hh])
                    dk_acc[cols, :] += _dot_tn(dsb, qs[hh])
                    dq = dq + jnp.dot(dsb, km, preferred_element_type=F32)
                    dcr_ref[j, hh:hh + 1, :] -= jnp.sum(ds, axis=0, keepdims=True)
                    row_sum = row_sum + jnp.sum(ds, axis=1, keepdims=True)
                dcq.append(row_sum)
            dq_ref[rows, :] = (dq * 0.125).astype(BF16)
            dcq_ref[rows, :] = jnp.where(lane == 2 * pair, dcq[0],
                                         jnp.where(lane == 2 * pair + 1, dcq[1], dcq_ref[rows, :]))
        dk_ref[...] = dk_acc[...].astype(BF16)
        dv_ref[...] = dv_acc[...].astype(BF16)

    qkv_spec, cumr_spec = _attn_specs(t, nq)
    act = jax.ShapeDtypeStruct((n, w), BF16)
    return pl.pallas_call(
        body, name=name, grid=(batch, n_pairs),
        out_shape=[act, act, act, jax.ShapeDtypeStruct(cumr.shape, F32), jax.ShapeDtypeStruct((n, LANES), F32)],
        in_specs=[qkv_spec(0), qkv_spec(n_pairs), qkv_spec(2 * n_pairs), qkv_spec(0), qkv_spec(0), qkv_spec(0),
                  cumr_spec],
        out_specs=[qkv_spec(0), qkv_spec(0), qkv_spec(0), cumr_spec, pl.BlockSpec((t, LANES), lambda b, h: (b, 0))],
        scratch_shapes=[pltpu.VMEM((t, LANES), F32), pltpu.VMEM((t, LANES), F32)],
        compiler_params=_params(("parallel", "arbitrary")))(qkv, qkv, qkv, o, do, lse, cumr)


def _adamw(name, parts, w, m, v):
    r, c = w.shape
    tr = 128 if r % 128 == 0 else r
    c1 = 1.0 - ADAM_B1 ** ADAM_STEP
    c2 = 1.0 - ADAM_B2 ** ADAM_STEP

    def body(p_ref, w_ref, m_ref, v_ref, g_ref, d_ref, m2_ref, v2_ref):
        g = p_ref[0].astype(F32)
        for s in range(1, N_DEV):
            g = g + p_ref[s].astype(F32)
        m2 = ADAM_B1 * m_ref[...] + (1.0 - ADAM_B1) * g
        v2 = ADAM_B2 * v_ref[...] + (1.0 - ADAM_B2) * (g * g)
        g_ref[...] = g
        m2_ref[...] = m2
        v2_ref[...] = v2
        d_ref[...] = -ADAM_LR * ((m2 / c1) / (jnp.sqrt(v2 / c2) + ADAM_EPS) + ADAM_WD * w_ref[...])

    blk = pl.BlockSpec((tr, c), lambda i: (i, 0))
    shp = jax.ShapeDtypeStruct((r, c), F32)
    return pl.pallas_call(
        body, name=name, out_shape=[shp] * 4, grid=(r // tr,),
        in_specs=[pl.BlockSpec((N_DEV, tr, c), lambda i: (0, i, 0)), blk, blk, blk], out_specs=[blk] * 4,
        compiler_params=_params(("parallel",)))(parts, w, m, v)


def _cat_small(vals):
    parts = []
    for name in SMALL:
        v = vals[name].reshape(1, -1).astype(F32)
        parts.append(jnp.pad(v, ((0, 0), (0, SMALL_W[name] - v.shape[1]))))
    return jnp.concatenate(parts, axis=1)


def _split_small(row, shapes):
    out, off = {}, 0
    for name in SMALL:
        out[name] = row[0, off:off + SMALL_N[name]].reshape(shapes[name])
        off += SMALL_W[name]
    return out


def _w_in_pieces(lo, hi, n_fg):
    o_fg = 3 * SEG
    out = []
    c = lo
    while c < hi:
        if c < o_fg:
            src, base, end = c // SEG, (c // SEG) * SEG, (c // SEG + 1) * SEG
        elif c < o_fg + n_fg:
            src, base, end = 7, o_fg, o_fg + n_fg
        else:
            k = (c - o_fg - n_fg) // SEG
            src, base, end = 3 + k, o_fg + n_fg + k * SEG, o_fg + n_fg + (k + 1) * SEG
        stop = min(hi, end)
        out.append((src, c - base, stop - base, c))
        c = stop
    return out


def _cols_from_shards(g):
    return jnp.transpose(g, (1, 0, 2)).reshape(g.shape[1], N_DEV * g.shape[2])


def _shards_from_cols(a):
    r, c = a.shape
    return jnp.transpose(a.reshape(r, N_DEV, c // N_DEV), (1, 0, 2))


def kernel(x, meta_tokens, norm_mix_gain, w_in, b_forget, w_attn_out, b_glu, conv_dw_w, conv_dw_b, conv_ln_gain, conv_ln_bias, w_conv_out, b_conv_out, w_out, norm_mlp_gain, w_mlp_up, w_mlp_down, final_norm_gain, loss_target, m_meta_tokens, m_norm_mix_gain, m_w_in, m_b_forget, m_w_attn_out, m_b_glu, m_conv_dw_w, m_conv_dw_b, m_conv_ln_gain, m_conv_ln_bias, m_w_conv_out, m_b_conv_out, m_w_out, m_norm_mlp_gain, m_w_mlp_up, m_w_mlp_down, m_final_norm_gain, v_meta_tokens, v_norm_mix_gain, v_w_in, v_b_forget, v_w_attn_out, v_b_glu, v_conv_dw_w, v_conv_dw_b, v_conv_ln_gain, v_conv_ln_bias, v_w_conv_out, v_b_conv_out, v_w_out, v_norm_mlp_gain, v_w_mlp_up, v_w_mlp_down, v_final_norm_gain):
    weights = dict(meta_tokens=meta_tokens, norm_mix_gain=norm_mix_gain, w_in=w_in, b_forget=b_forget, w_attn_out=w_attn_out, b_glu=b_glu, conv_dw_w=conv_dw_w, conv_dw_b=conv_dw_b, conv_ln_gain=conv_ln_gain, conv_ln_bias=conv_ln_bias, w_conv_out=w_conv_out, b_conv_out=b_conv_out, w_out=w_out, norm_mlp_gain=norm_mlp_gain, w_mlp_up=w_mlp_up, w_mlp_down=w_mlp_down, final_norm_gain=final_norm_gain)
    mom_m = dict(meta_tokens=m_meta_tokens, norm_mix_gain=m_norm_mix_gain, w_in=m_w_in, b_forget=m_b_forget, w_attn_out=m_w_attn_out, b_glu=m_b_glu, conv_dw_w=m_conv_dw_w, conv_dw_b=m_conv_dw_b, conv_ln_gain=m_conv_ln_gain, conv_ln_bias=m_conv_ln_bias, w_conv_out=m_w_conv_out, b_conv_out=m_b_conv_out, w_out=m_w_out, norm_mlp_gain=m_norm_mlp_gain, w_mlp_up=m_w_mlp_up, w_mlp_down=m_w_mlp_down, final_norm_gain=m_final_norm_gain)
    mom_v = dict(meta_tokens=v_meta_tokens, norm_mix_gain=v_norm_mix_gain, w_in=v_w_in, b_forget=v_b_forget, w_attn_out=v_w_attn_out, b_glu=v_b_glu, conv_dw_w=v_conv_dw_w, conv_dw_b=v_conv_dw_b, conv_ln_gain=v_conv_ln_gain, conv_ln_bias=v_conv_ln_bias, w_conv_out=v_w_conv_out, b_conv_out=v_b_conv_out, w_out=v_w_out, norm_mlp_gain=v_norm_mlp_gain, w_mlp_up=v_w_mlp_up, w_mlp_down=v_w_mlp_down, final_norm_gain=v_final_norm_gain)
    names = list(weights)
    batch, seq, d = x.shape
    t = seq + N_META
    n = batch * t
    nq = t // ROW_TILE
    n_pairs = d // LANES
    assert t % ROW_TILE == 0 and d == SEG

    gather_a = _exchange_start("gather_in_start", [(w_in[0].astype(BF16), False), (meta_tokens, False),
                                                   (conv_dw_w[0], False)])
    w_in_g, meta_g, w_dw_g = _exchange_wait("gather_in_wait", gather_a, gather_a["token"])
    gather_b = _exchange_start("gather_rest_start", [
        (w_attn_out[0].astype(BF16), False), (w_conv_out[0].astype(BF16), False), (w_out[0].astype(BF16), False),
        (w_mlp_up[0].astype(BF16), False), (w_mlp_down[0].astype(BF16), False)])
    n_fg = b_forget.shape[1]
    shard_w = w_in.shape[2]
    seg_cols = [[] for _ in range(8)]
    for p in range(N_DEV):
        for src, c0, c1, orig in _w_in_pieces(p * shard_w, (p + 1) * shard_w, n_fg):
            seg_cols[src].append(w_in_g[p][:, orig - p * shard_w:orig - p * shard_w + c1 - c0])
    w_pad = jnp.concatenate([c_ for src in range(8) for c_ in seg_cols[src]]
                            + [jnp.zeros((d, FG_PAD - n_fg), BF16)], axis=1)
    d_ff = w_mlp_down.shape[1] * N_DEV
    ff_blk = d_ff // N_DEV
    meta_f = _cols_from_shards(meta_g)
    w_dw = _cols_from_shards(w_dw_g)

    row2 = lambda v: v.reshape(1, -1)
    g1, g2, g3 = row2(norm_mix_gain) + gather_b["token"][0:1, 0:1], row2(norm_mlp_gain), row2(final_norm_gain)
    b_fg = jnp.pad(b_forget, ((0, 0), (0, FG_PAD - n_fg)))
    h0 = jnp.concatenate([jnp.broadcast_to(meta_f[None], (batch, N_META, d)), x], axis=1).reshape(n, d)
    tgt = jnp.concatenate([jnp.zeros((batch, N_META, d), F32), loss_target], axis=1).reshape(n, d)

    hn1 = _rms_fwd("rms1", h0, g1)
    qkv = _mm_nn("proj_qkv", hn1, w_pad, _w_cols(d, SEG, 0), 3 * SEG, SEG, BF16)
    glu = _mm_nn("proj_glu", hn1, w_pad, _w_cols(d, SEG, 3), 2 * SEG, SEG, F32)
    gates = _mm_nn("proj_gates", hn1, w_pad, _w_cols(d, SEG, 5), 2 * SEG, SEG, F32)
    fg = _mm_nn("proj_fg", hn1, w_pad, _w_cols(d, FG_PAD, 7 * SEG // FG_PAD), FG_PAD, FG_PAD, F32)

    cum = _fox_prep_fwd("fox_cumsum", fg, b_fg, batch)
    cum_h = cum.reshape(batch, t, FG_PAD)[:, :, :2 * n_pairs].reshape(batch, t, n_pairs, 2)
    cumr = jnp.transpose(cum_h.reshape(batch, nq, ROW_TILE, n_pairs, 2), (0, 3, 1, 4, 2))
    cumr = jnp.pad(cumr, ((0, 0), (0, 0), (0, 0), (0, 6), (0, 0)))
    o, lse = _attn_fwd("attn_fwd", qkv, cumr, batch)
    rest = _exchange_wait("gather_rest_wait", gather_b, o)
    w_ao, w_co, w_o = [r_.reshape(d, d) for r_ in rest[:3]]
    w_up = rest[3]
    w_dn = rest[4].reshape(d_ff, d)
    a = _mm_nn("attn_out", o, w_ao, _w_cols(d, d, 0), d, d, F32)

    c1 = _glu_conv_fwd("glu_conv", glu, b_glu, w_dw, conv_dw_b, batch)
    c3 = _ln_silu_fwd("ln_silu", c1, conv_ln_gain, conv_ln_bias)
    c = _mm_nn("conv_out", c3, w_co, _w_cols(d, d, 0), d, d, F32)

    mrg = _merge_fwd("merge", gates, a, c, b_conv_out)
    mo = _mm_nn("mix_out", mrg, w_o, _w_cols(d, d, 0), d, d, F32)
    h1, hn2 = _rms_fwd("resid_rms2", h0, g2, res=mo)
    per = ff_blk // 512
    up, act = _mm_nn("mlp_up", hn2, w_up, pl.BlockSpec((None, d, 512), lambda i, j: (j // per, 0, j % per)),
                     d_ff, 512, BF16, relu2=True)
    dn = _mm_nn("mlp_down", act, w_dn, _w_cols(d_ff, d, 0), d, d, F32, tm=ROW_TILE)
    dh2, dh2b, loss_blk, dg3 = _final("final_loss", h1, dn, tgt, g3, batch)

    dup = _mm_nt("d_mlp_down", [(dh2b, _a_rows(d), w_dn, _w_rows(d, d))], n, d_ff, d, BF16, relu_bwd_of=up)
    dw_dn = _grad_w("gw_mlp_down", act, dh2b)
    dhn2 = _mm_nt("d_mlp_up", [(dup, _a_rows(ff_blk, g), w_up, pl.BlockSpec((None, 512, ff_blk), lambda i, j, g=g: (g, j, 0)))
                               for g in range(N_DEV)], n, d, 512, F32)
    dw_up = _mm_tn("gw_mlp_up", hn2, lambda a_: 0, d, dup, lambda b_: b_, ff_blk, (N_DEV, d, ff_blk),
                   pl.BlockSpec((None, d, ff_blk), lambda a_, b_, k: (b_, 0, 0)), (1, N_DEV))
    scatter_1 = _exchange_start("scatter_mlp_start", [(dw_dn.reshape(N_DEV, ff_blk, d), True), (dw_up, True)])
    dh1, dg2, dh1b = _rms_bwd("rms2_bwd", dhn2, h1, g2 + scatter_1["token"][0:1, 0:1], dh2, batch, with_bf16=True)

    dm = _mm_nt("d_mix_out", [(dh1b, _a_rows(d), w_o, _w_rows(d, d))], n, d, d, F32)
    dw_o = _grad_w("gw_mix_out", mrg, dh1b)
    da, dc, dga, dgc, dbco = _merge_bwd("merge_bwd", dm, gates, a, c, b_conv_out)

    do = _mm_nt("d_attn_out", [(da, _a_rows(d), w_ao, _w_rows(d, d))], n, d, d, BF16)
    dw_ao = _grad_w("gw_attn_out", o, da)
    dc3 = _mm_nt("d_conv_out", [(dc, _a_rows(d), w_co, _w_rows(d, d))], n, d, d, F32)
    dw_co = _grad_w("gw_conv_out", c3, dc)

    scatter_2 = _exchange_start("scatter_mix_start", [(dw_.reshape(N_DEV, d // N_DEV, d), True)
                                                      for dw_ in (dw_o, dw_ao, dw_co)])
    dc1, dg_ln, db_ln = _ln_silu_bwd("ln_silu_bwd", dc3, c1, conv_ln_gain + scatter_2["token"][0:1, 0:1],
                                     conv_ln_bias)
    dglu_a, dglu_g, dw_dw, db_dw, dbg_a, dbg_g = _glu_conv_bwd("glu_conv_bwd", dc1, glu, b_glu, w_dw, batch)

    dq, dk, dv, dcumr, dcum_q = _attn_bwd("attn_bwd", qkv, o, do, lse, cumr, batch)
    dcum_k = jnp.transpose(dcumr[:, :, :, :2, :], (0, 2, 4, 1, 3)).reshape(n, 2 * n_pairs)
    dcum_k = jnp.pad(dcum_k, ((0, 0), (0, FG_PAD - 2 * n_pairs)))
    dfg, db_fg = _fox_prep_bwd("fox_cumsum_bwd", dcum_k, dcum_q, fg, b_fg, batch)

    segs = [dq, dk, dv, dglu_a, dglu_g, dga, dgc]
    pairs = [(s_, _a_rows(SEG, 0, ROW_TILE), w_pad, _w_rows(512, SEG, i)) for i, s_ in enumerate(segs)]
    pairs.append((dfg, _a_rows(FG_PAD, 0, ROW_TILE), w_pad, _w_rows(512, FG_PAD, 7 * SEG // FG_PAD)))
    dhn1 = _mm_nt("d_proj_in", pairs, n, d, 512, F32, tm=ROW_TILE)
    gw_seg = [_grad_w("gw_in_%d" % i, hn1, s_) for i, s_ in enumerate(segs)]
    gw_fg = _grad_w("gw_in_fg", hn1, dfg)
    dh0, dg1, dmeta = _rms_bwd("rms1_bwd", dhn1, h0, g1, dh1, batch, with_meta=True)

    grad_x = dh0.reshape(batch, t, d)[:, N_META:, :]
    gw_src = gw_seg + [gw_fg]
    dw_in = jnp.stack([jnp.concatenate([gw_src[src][:, c0:c1]
                                        for src, c0, c1, _ in _w_in_pieces(p * shard_w, (p + 1) * shard_w, n_fg)], axis=1)
                       for p in range(N_DEV)])

    small_g = dict(norm_mix_gain=dg1, b_forget=db_fg[:, :n_fg], b_glu=jnp.concatenate([dbg_a, dbg_g], axis=1),
                   conv_dw_b=db_dw, conv_ln_gain=dg_ln, conv_ln_bias=db_ln, b_conv_out=dbco, norm_mlp_gain=dg2,
                   final_norm_gain=dg3)
    scatter_3 = _exchange_start("scatter_in_start", [
        (dw_in, True), (_shards_from_cols(dmeta), True), (_shards_from_cols(dw_dw), True),
        (_cat_small(small_g), False), (loss_blk[0:1, :], False)])

    grads, deltas, new_m, new_v = {}, {}, {}, {}

    def update(k, parts):
        shp = weights[k].shape
        w2 = lambda arr: arr.reshape(parts.shape[1:])
        res_ = _adamw("adamw_" + k, parts, w2(weights[k]), w2(mom_m[k]), w2(mom_v[k]))
        grads[k], deltas[k], new_m[k], new_v[k] = [r.reshape(shp) for r in res_]

    for k, parts in zip(("w_mlp_down", "w_mlp_up"), _exchange_wait("scatter_mlp_wait", scatter_1, scatter_3["token"])):
        update(k, parts)
    for k, parts in zip(("w_out", "w_attn_out", "w_conv_out"),
                        _exchange_wait("scatter_mix_wait", scatter_2, deltas["w_mlp_up"])):
        update(k, parts)
    reduced = _exchange_wait("scatter_in_wait", scatter_3, deltas["w_conv_out"])
    loss = jnp.sum(reduced.pop()[:, 0, 0])
    for k, parts in zip(("w_in", "meta_tokens", "conv_dw_w"), reduced[:-1]):
        update(k, parts)
    res = _adamw("adamw_small", reduced[-1], _cat_small(weights), _cat_small(mom_m), _cat_small(mom_v))
    shapes = {k: weights[k].shape for k in SMALL}
    for dst, r in zip((grads, deltas, new_m, new_v), res):
        dst.update(_split_small(r, shapes))

    return (loss, grad_x, *[grads[k] for k in names], *[deltas[k] for k in names],
            *[new_m[k] for k in names], *[new_v[k] for k in names])
```

```python
import functools

import jax
import jax.numpy as jnp
from jax import lax
from jax.experimental import pallas as pl
from jax.experimental.pallas import tpu as pltpu

F32, BF16 = jnp.float32, jnp.bfloat16
N_DEV = 8
N_META = 16
HEAD_DIM = 64
LANES = 128
CONV_W = 31
RMS_EPS = 1e-6
LN_EPS = 1e-5
ROW_TILE = 688
MM_TM = 2 * ROW_TILE
SEG = 1024
FG_PAD = 128
VMEM_LIMIT = 56 * 1024 * 1024
ADAM_LR, ADAM_B1, ADAM_B2, ADAM_EPS, ADAM_WD, ADAM_STEP = 0.001, 0.9, 0.999, 1e-08, 0.01, 10
NEG = -1e30

SMALL = ("norm_mix_gain", "b_forget", "b_glu", "conv_dw_b", "conv_ln_gain", "conv_ln_bias", "b_conv_out",
         "norm_mlp_gain", "final_norm_gain")
SMALL_W = {"norm_mix_gain": 1024, "b_forget": 128, "b_glu": 2048, "conv_dw_b": 1024, "conv_ln_gain": 1024,
           "conv_ln_bias": 1024, "b_conv_out": 1024, "norm_mlp_gain": 1024, "final_norm_gain": 1024}
SMALL_N = {"norm_mix_gain": 1024, "b_forget": 16, "b_glu": 2048, "conv_dw_b": 1024, "conv_ln_gain": 1024,
           "conv_ln_bias": 1024, "b_conv_out": 1024, "norm_mlp_gain": 1024, "final_norm_gain": 1024}


def _params(sem=None):
    return pltpu.CompilerParams(dimension_semantics=sem, vmem_limit_bytes=VMEM_LIMIT)


def _sigmoid(x):
    return 1.0 / (1.0 + jnp.exp(-x))


def _dot_nt(a, b):
    return lax.dot_general(a, b, (((1,), (1,)), ((), ())), preferred_element_type=F32)


def _dot_tn(a, b):
    return lax.dot_general(a, b, (((0,), (0,)), ((), ())), preferred_element_type=F32)


HBM_SPEC = pl.BlockSpec(memory_space=pltpu.HBM)
SEM_SPEC = pl.BlockSpec(memory_space=pltpu.SEMAPHORE)
DATAFLOW = pltpu.SideEffectType.DATAFLOW_SIDE_EFFECTING


def _device_index():
    return 4 * lax.axis_index("x") + 2 * lax.axis_index("y") + lax.axis_index("c")


def _peers():
    x, y, c = lax.axis_index("x"), lax.axis_index("y"), lax.axis_index("c")
    out = []
    for k in range(1, N_DEV):
        px = 1 - x if k & 4 else x
        py = 1 - y if k & 2 else y
        pc = 1 - c if k & 1 else c
        out.append((k, (px, py, pc), 4 * px + 2 * py + pc))
    return out


def _peer_copy(per_dest, src_ref, land_ref, send_sems, recv_sems, a, k, dev, peer):
    src = src_ref.at[peer] if per_dest else src_ref
    return pltpu.make_async_remote_copy(
        src_ref=src, dst_ref=land_ref.at[_device_index()], send_sem=send_sems.at[a * (N_DEV - 1) + k - 1],
        recv_sem=recv_sems.at[a * (N_DEV - 1) + k - 1], device_id=dev, device_id_type=pl.DeviceIdType.MESH)


def _exchange_start(name, items):
    n = len(items)
    per_dest = [it[1] for it in items]

    def body(*refs):
        srcs, lands = refs[:n], refs[n:2 * n]
        send_sems, recv_sems, token = refs[2 * n], refs[2 * n + 1], refs[-1]
        for a in range(n):
            for k, dev, peer in _peers():
                _peer_copy(per_dest[a], srcs[a], lands[a], send_sems, recv_sems, a, k, dev, peer).start()
        token[...] = jnp.zeros(token.shape, F32)

    srcs = [pltpu.with_memory_space_constraint(it[0], pltpu.HBM) for it in items]
    lands = []
    for arr, pd in items:
        shp = arr.shape if pd else (N_DEV,) + arr.shape
        lands.append(pltpu.with_memory_space_constraint(lax.empty(shp, arr.dtype), pltpu.HBM))
    sems = pltpu.SemaphoreType.DMA((n * (N_DEV - 1),))
    res = pl.pallas_call(
        body, name=name,
        out_shape=(sems, sems, *[pltpu.HBM(a_.shape, a_.dtype) for a_ in srcs + lands],
                   jax.ShapeDtypeStruct((8, 128), F32)),
        in_specs=[HBM_SPEC] * (2 * n),
        out_specs=(SEM_SPEC, SEM_SPEC, *[HBM_SPEC] * (2 * n), pl.BlockSpec(memory_space=pltpu.VMEM)),
        input_output_aliases={i: 2 + i for i in range(2 * n)},
        compiler_params=pltpu.CompilerParams(has_side_effects=DATAFLOW),
    )(*srcs, *lands)
    return dict(per_dest=per_dest, send=res[0], recv=res[1], srcs=list(res[2:2 + n]),
                lands=list(res[2 + n:2 + 2 * n]), token=res[-1])


def _exchange_wait(name, started, after):
    per_dest = started["per_dest"]
    n = len(per_dest)

    def body(*refs):
        srcs, lands = refs[:n], refs[n:2 * n]
        send_sems, recv_sems = refs[2 * n], refs[2 * n + 1]
        for a in range(n):
            for k, dev, peer in _peers():
                cp = _peer_copy(per_dest[a], srcs[a], lands[a], send_sems, recv_sems, a, k, dev, peer)
                cp.wait_send()
                cp.wait_recv()

    bufs = started["srcs"] + started["lands"]
    res = pl.pallas_call(
        body, name=name, out_shape=tuple(pltpu.HBM(b_.shape, b_.dtype) for b_ in bufs),
        in_specs=[HBM_SPEC] * (2 * n) + [SEM_SPEC, SEM_SPEC, pl.BlockSpec(memory_space=pl.ANY)],
        out_specs=tuple([HBM_SPEC] * (2 * n)), input_output_aliases={i: i for i in range(2 * n)},
        compiler_params=pltpu.CompilerParams(has_side_effects=DATAFLOW),
    )(*bufs, started["send"], started["recv"], after)
    me = _device_index()
    out = []
    for pd, src, land in zip(per_dest, res[:n], res[n:]):
        own = lax.dynamic_index_in_dim(src, me, 0, keepdims=True) if pd else src[None]
        out.append(lax.dynamic_update_slice_in_dim(land, own, me, axis=0))
    return out


def _mm_nn(name, x, w, w_spec, n_out, tn, out_dtype, relu2=False, tm=MM_TM):
    m, k = x.shape

    def body(x_ref, w_ref, *outs):
        acc = jnp.dot(x_ref[...], w_ref[...], preferred_element_type=F32)
        outs[0][...] = acc.astype(outs[0].dtype)
        if relu2:
            r = jnp.maximum(acc, 0.0)
            outs[1][...] = (r * r).astype(outs[1].dtype)

    o_spec = pl.BlockSpec((tm, tn), lambda i, j: (i, j))
    shapes = [jax.ShapeDtypeStruct((m, n_out), out_dtype)]
    if relu2:
        shapes.append(jax.ShapeDtypeStruct((m, n_out), BF16))
    res = pl.pallas_call(
        body, name=name, out_shape=shapes, grid=(m // tm, n_out // tn),
        in_specs=[pl.BlockSpec((tm, k), lambda i, j: (i, 0)), w_spec],
        out_specs=[o_spec] * len(shapes), compiler_params=_params(("parallel", "parallel")),
    )(x, w)
    return res if relu2 else res[0]


def _mm_nt(name, pairs, m, n_out, tn, out_dtype, relu_bwd_of=None, tm=MM_TM, after=None):
    np_ = len(pairs)

    def body(*refs):
        acc = None
        for p in range(np_):
            d = _dot_nt(refs[2 * p][...], refs[2 * p + 1][...])
            acc = d if acc is None else acc + d
        if relu_bwd_of is not None:
            acc = acc * (2.0 * jnp.maximum(refs[2 * np_][...].astype(F32), 0.0))
        refs[-1][...] = acc.astype(refs[-1].dtype)

    o_spec = pl.BlockSpec((tm, tn), lambda i, j: (i, j))
    operands, specs = [], []
    for a, a_spec, w, w_spec in pairs:
        operands += [a, w]
        specs += [a_spec, w_spec]
    if relu_bwd_of is not None:
        operands.append(relu_bwd_of)
        specs.append(o_spec)
    if after is not None:
        operands.append(after)
        specs.append(pl.BlockSpec((8, 128), lambda i, j: (0, 0)))
    return pl.pallas_call(
        body, name=name, out_shape=jax.ShapeDtypeStruct((m, n_out), out_dtype), grid=(m // tm, n_out // tn),
        in_specs=specs, out_specs=o_spec, compiler_params=_params(("parallel", "parallel")),
    )(*operands)


def _mm_tn(name, x, x_col, ta, dy, dy_col, tb, out_shape, out_spec, grid_ab):
    m = x.shape[0]
    tm = MM_TM
    nk = m // tm

    def body(x_ref, dy_ref, o_ref, acc_ref):
        k = pl.program_id(2)

        @pl.when(k == 0)
        def _():
            acc_ref[...] = jnp.zeros(acc_ref.shape, F32)

        acc_ref[...] += _dot_tn(x_ref[...], dy_ref[...])

        @pl.when(k == nk - 1)
        def _():
            o_ref[...] = acc_ref[...].astype(BF16)

    return pl.pallas_call(
        body, name=name, out_shape=jax.ShapeDtypeStruct(out_shape, BF16), grid=grid_ab + (nk,),
        in_specs=[pl.BlockSpec((tm, ta), lambda a, b, k: (k, x_col(a))),
                  pl.BlockSpec((tm, tb), lambda a, b, k: (k, dy_col(b)))],
        out_specs=out_spec, scratch_shapes=[pltpu.VMEM((ta, tb), F32)],
        compiler_params=_params(("parallel", "parallel", "arbitrary")),
    )(x, dy)


def _w_cols(k, tn, off_blocks):
    return pl.BlockSpec((k, tn), lambda i, j: (0, off_blocks + j))


def _a_rows(kw, col_block=0, tm=MM_TM):
    return pl.BlockSpec((tm, kw), lambda i, j: (i, col_block))


def _w_rows(tn, kw, col_block=0):
    return pl.BlockSpec((tn, kw), lambda i, j: (j, col_block))


def _grad_w(name, x, dy):
    nb = dy.shape[1]
    tb = min(nb, 1024)
    return _mm_tn(name, x, lambda a: a, 1024, dy, lambda b: b, tb, (x.shape[1], nb),
                  pl.BlockSpec((1024, tb), lambda a, b, k: (a, b)), (x.shape[1] // 1024, nb // tb))


def _row_spec(width):
    return pl.BlockSpec((ROW_TILE, width), lambda i: (i, 0))


def _vec_spec(width):
    return pl.BlockSpec((1, width), lambda i: (0, 0))


def _rms_fwd(name, h, g, res=None):
    n, d = h.shape

    def body(*refs):
        if res is None:
            h_ref, g_ref, hn_ref = refs
            hv = h_ref[...]
        else:
            h_ref, r_ref, g_ref, hs_ref, hn_ref = refs
            hv = h_ref[...] + r_ref[...]
            hs_ref[...] = hv
        r = lax.rsqrt(jnp.mean(hv * hv, axis=-1, keepdims=True) + RMS_EPS)
        hn_ref[...] = (hv * r * g_ref[...]).astype(BF16)

    ins = [h, g] if res is None else [h, res, g]
    in_specs = [_row_spec(d), _vec_spec(d)] if res is None else [_row_spec(d), _row_spec(d), _vec_spec(d)]
    hn_shape = jax.ShapeDtypeStruct((n, d), BF16)
    if res is None:
        out_shape, out_specs = hn_shape, _row_spec(d)
    else:
        out_shape, out_specs = [jax.ShapeDtypeStruct((n, d), F32), hn_shape], [_row_spec(d), _row_spec(d)]
    return pl.pallas_call(body, name=name, out_shape=out_shape, grid=(n // ROW_TILE,), in_specs=in_specs,
                          out_specs=out_specs, compiler_params=_params(("parallel",)))(*ins)


def _rms_bwd(name, dhn, h, g, dres, batch, with_bf16=False, with_meta=False):
    n, d = h.shape
    t = n // batch
    nt = t // ROW_TILE

    def body(dhn_ref, h_ref, g_ref, dres_ref, *outs):
        first = (pl.program_id(0) == 0) & (pl.program_id(1) == 0)
        hv = h_ref[...]
        r = lax.rsqrt(jnp.mean(hv * hv, axis=-1, keepdims=True) + RMS_EPS)
        nrm = hv * r
        dn = dhn_ref[...] * g_ref[...]
        dh = dres_ref[...] + r * (dn - nrm * jnp.mean(dn * nrm, axis=-1, keepdims=True))
        outs[0][...] = dh
        dg_ref = outs[1]

        @pl.when(first)
        def _():
            dg_ref[...] = jnp.zeros(dg_ref.shape, F32)

        dg_ref[...] += jnp.sum(dhn_ref[...] * nrm, axis=0, keepdims=True)
        nxt = 2
        if with_bf16:
            outs[nxt][...] = dh.astype(BF16)
            nxt += 1
        if with_meta:
            meta_ref = outs[nxt]

            @pl.when(first)
            def _():
                meta_ref[...] = jnp.zeros(meta_ref.shape, F32)

            @pl.when(pl.program_id(1) == 0)
            def _():
                meta_ref[...] += dh[0:N_META, :]

    row = pl.BlockSpec((ROW_TILE, d), lambda b, j: (b * nt + j, 0))
    vec = pl.BlockSpec((1, d), lambda b, j: (0, 0))
    shapes = [jax.ShapeDtypeStruct((n, d), F32), jax.ShapeDtypeStruct((1, d), F32)]
    specs = [row, vec]
    if with_bf16:
        shapes.append(jax.ShapeDtypeStruct((n, d), BF16))
        specs.append(row)
    if with_meta:
        shapes.append(jax.ShapeDtypeStruct((N_META, d), F32))
        specs.append(pl.BlockSpec((N_META, d), lambda b, j: (0, 0)))
    return pl.pallas_call(body, name=name, out_shape=shapes, grid=(batch, nt), in_specs=[row, row, vec, row],
                          out_specs=specs, compiler_params=_params(("arbitrary", "arbitrary")))(dhn, h, g, dres)


def _final(name, h1, dn, tgt, g, batch):
    n, d = h1.shape
    t = n // batch
    nt = t // ROW_TILE

    def body(h1_ref, dn_ref, tgt_ref, g_ref, dh_ref, dhb_ref, loss_ref, dg_ref):
        first = (pl.program_id(0) == 0) & (pl.program_id(1) == 0)
        hv = h1_ref[...] + dn_ref[...]
        r = lax.rsqrt(jnp.mean(hv * hv, axis=-1, keepdims=True) + RMS_EPS)
        nrm = hv * r
        gv = g_ref[...]
        pos = pl.program_id(1) * ROW_TILE + lax.broadcasted_iota(jnp.int32, (ROW_TILE, 1), 0)
        diff = jnp.where(pos >= N_META, nrm * gv - tgt_ref[...], 0.0)
        dy = diff * (1.0 / d)

        @pl.when(first)
        def _():
            loss_ref[...] = jnp.zeros(loss_ref.shape, F32)
            dg_ref[...] = jnp.zeros(dg_ref.shape, F32)

        loss_ref[...] += jnp.full(loss_ref.shape, 0.5 / d, F32) * jnp.sum(diff * diff)
        dg_ref[...] += jnp.sum(dy * nrm, axis=0, keepdims=True)
        dng = dy * gv
        dh = r * (dng - nrm * jnp.mean(dng * nrm, axis=-1, keepdims=True))
        dh_ref[...] = dh
        dhb_ref[...] = dh.astype(BF16)

    row = pl.BlockSpec((ROW_TILE, d), lambda b, j: (b * nt + j, 0))
    vec = pl.BlockSpec((1, d), lambda b, j: (0, 0))
    return pl.pallas_call(
        body, name=name, grid=(batch, nt), in_specs=[row, row, row, vec],
        out_shape=[jax.ShapeDtypeStruct((n, d), F32), jax.ShapeDtypeStruct((n, d), BF16),
                   jax.ShapeDtypeStruct((8, 128), F32), jax.ShapeDtypeStruct((1, d), F32)],
        out_specs=[row, row, pl.BlockSpec((8, 128), lambda b, j: (0, 0)), vec],
        compiler_params=_params(("arbitrary", "arbitrary")))(h1, dn, tgt, g)


def _ln_silu_fwd(name, c1, g, b):
    n, d = c1.shape

    def body(c_ref, g_ref, b_ref, o_ref):
        xv = c_ref[...]
        xc = xv - jnp.mean(xv, axis=-1, keepdims=True)
        rstd = lax.rsqrt(jnp.mean(xc * xc, axis=-1, keepdims=True) + LN_EPS)
        c2 = xc * rstd * g_ref[...] + b_ref[...]
        o_ref[...] = (c2 * _sigmoid(c2)).astype(BF16)

    return pl.pallas_call(body, name=name, out_shape=jax.ShapeDtypeStruct((n, d), BF16), grid=(n // ROW_TILE,),
                          in_specs=[_row_spec(d), _vec_spec(d), _vec_spec(d)], out_specs=_row_spec(d),
                          compiler_params=_params(("parallel",)))(c1, g, b)


def _ln_silu_bwd(name, dc3, c1, g, b):
    n, d = c1.shape

    def body(d_ref, c_ref, g_ref, b_ref, dc1_ref, dg_ref, db_ref):
        xv = c_ref[...]
        xc = xv - jnp.mean(xv, axis=-1, keepdims=True)
        rstd = lax.rsqrt(jnp.mean(xc * xc, axis=-1, keepdims=True) + LN_EPS)
        xh = xc * rstd
        c2 = xh * g_ref[...] + b_ref[...]
        s = _sigmoid(c2)
        dc2 = d_ref[...] * (s * (1.0 + c2 * (1.0 - s)))

        @pl.when(pl.program_id(0) == 0)
        def _():
            dg_ref[...] = jnp.zeros(dg_ref.shape, F32)
            db_ref[...] = jnp.zeros(db_ref.shape, F32)

        dg_ref[...] += jnp.sum(dc2 * xh, axis=0, keepdims=True)
        db_ref[...] += jnp.sum(dc2, axis=0, keepdims=True)
        dxh = dc2 * g_ref[...]
        dc1_ref[...] = rstd * (dxh - jnp.mean(dxh, axis=-1, keepdims=True)
                               - xh * jnp.mean(dxh * xh, axis=-1, keepdims=True))

    return pl.pallas_call(
        body, name=name, grid=(n // ROW_TILE,),
        out_shape=[jax.ShapeDtypeStruct((n, d), F32), jax.ShapeDtypeStruct((1, d), F32),
                   jax.ShapeDtypeStruct((1, d), F32)],
        in_specs=[_row_spec(d), _row_spec(d), _vec_spec(d), _vec_spec(d)],
        out_specs=[_row_spec(d), _vec_spec(d), _vec_spec(d)],
        compiler_params=_params(("arbitrary",)))(dc3, c1, g, b)


MERGE_TC = 512


def _merge_fwd(name, gates, a, c, b_co):
    n, d = a.shape
    nc = d // MERGE_TC

    def body(ga_ref, gc_ref, a_ref, c_ref, b_ref, m_ref):
        m = _sigmoid(ga_ref[...]) * a_ref[...] + _sigmoid(gc_ref[...]) * (c_ref[...] + b_ref[...])
        m_ref[...] = m.astype(BF16)

    blk = lambda off: pl.BlockSpec((ROW_TILE, MERGE_TC), lambda i, j: (i, off + j))
    return pl.pallas_call(
        body, name=name, out_shape=jax.ShapeDtypeStruct((n, d), BF16), grid=(n // ROW_TILE, nc),
        in_specs=[blk(0), blk(nc), blk(0), blk(0), pl.BlockSpec((1, MERGE_TC), lambda i, j: (0, j))],
        out_specs=blk(0), compiler_params=_params(("parallel", "parallel")))(gates, gates, a, c, b_co)


def _merge_bwd(name, dm, gates, a, c, b_co):
    n, d = a.shape
    nc = d // MERGE_TC

    def body(dm_ref, ga_ref, gc_ref, a_ref, c_ref, b_ref, da_ref, dc_ref, dga_ref, dgc_ref, dbco_ref):
        dmv = dm_ref[...]
        sa, sc = _sigmoid(ga_ref[...]), _sigmoid(gc_ref[...])
        dc = dmv * sc
        da_ref[...] = (dmv * sa).astype(BF16)
        dc_ref[...] = dc.astype(BF16)
        dga_ref[...] = (dmv * a_ref[...] * sa * (1.0 - sa)).astype(BF16)
        dgc_ref[...] = (dmv * (c_ref[...] + b_ref[...]) * sc * (1.0 - sc)).astype(BF16)

        @pl.when(pl.program_id(1) == 0)
        def _():
            dbco_ref[...] = jnp.zeros(dbco_ref.shape, F32)

        dbco_ref[...] += jnp.sum(dc, axis=0, keepdims=True)

    blk = lambda off: pl.BlockSpec((ROW_TILE, MERGE_TC), lambda j, i: (i, off + j))
    vec = pl.BlockSpec((1, MERGE_TC), lambda j, i: (0, j))
    act = jax.ShapeDtypeStruct((n, d), BF16)
    return pl.pallas_call(
        body, name=name, grid=(nc, n // ROW_TILE),
        out_shape=[act, act, act, act, jax.ShapeDtypeStruct((1, d), F32)],
        in_specs=[blk(0), blk(0), blk(nc), blk(0), blk(0), vec],
        out_specs=[blk(0), blk(0), blk(0), blk(0), vec],
        compiler_params=_params(("parallel", "arbitrary")))(dm, gates, gates, a, c, b_co)


CONV_TC = 128
CONV_HALO = 32


def _conv_chunk(t):
    return 48 if t % 48 == 0 else 32 if t % 32 == 0 else 16


def _fold8(x):
    out = x[0:8]
    for k in range(1, x.shape[0] // 8):
        out = out + x[8 * k:8 * k + 8]
    return out


def _glu_conv_fwd(name, glu, b_glu, w_dw, b_dw, batch):
    n, c2 = glu.shape
    c = c2 // 2
    t = n // batch
    nc = c // CONV_TC

    def body(a_ref, gt_ref, ba_ref, bg_ref, w_ref, bdw_ref, o_ref, pad_ref):
        u = (a_ref[...] + ba_ref[...]) * _sigmoid(gt_ref[...] + bg_ref[...])
        pad_ref[0:CONV_HALO, :] = jnp.zeros((CONV_HALO, CONV_TC), F32)
        pad_ref[CONV_HALO:CONV_HALO + t, :] = u
        ch = _conv_chunk(t)
        for r0 in range(0, t, ch):
            acc = jnp.zeros((ch, CONV_TC), F32) + bdw_ref[...]
            for j in range(CONV_W):
                off = r0 + CONV_HALO - (CONV_W - 1) + j
                acc = acc + w_ref[j:j + 1, :] * pad_ref[off:off + ch, :]
            o_ref[r0:r0 + ch, :] = acc

    seq = lambda off: pl.BlockSpec((t, CONV_TC), lambda b, j: (b, off + j))
    vec = lambda off: pl.BlockSpec((1, CONV_TC), lambda b, j: (0, off + j))
    return pl.pallas_call(
        body, name=name, out_shape=jax.ShapeDtypeStruct((n, c), F32), grid=(batch, nc),
        in_specs=[seq(0), seq(nc), vec(0), vec(nc), pl.BlockSpec((CONV_W, CONV_TC), lambda b, j: (0, j)), vec(0)],
        out_specs=seq(0), scratch_shapes=[pltpu.VMEM((t + CONV_HALO, CONV_TC), F32)],
        compiler_params=_params(("parallel", "parallel")))(glu, glu, b_glu, b_glu, w_dw, b_dw)


def _glu_conv_bwd(name, dc1, glu, b_glu, w_dw, batch):
    n, c2 = glu.shape
    c = c2 // 2
    t = n // batch
    nc = c // CONV_TC

    def body(d_ref, a_ref, gt_ref, ba_ref, bg_ref, w_ref, dga_ref, dgg_ref, dw_ref, dbdw_ref, dba_ref, dbg_ref,
             padu_ref, padd_ref):
        av = a_ref[...] + ba_ref[...]
        sg = _sigmoid(gt_ref[...] + bg_ref[...])
        dc = d_ref[...]
        padu_ref[0:CONV_HALO, :] = jnp.zeros((CONV_HALO, CONV_TC), F32)
        padu_ref[CONV_HALO:CONV_HALO + t, :] = av * sg
        padd_ref[0:t, :] = dc
        padd_ref[t:t + CONV_HALO, :] = jnp.zeros((CONV_HALO, CONV_TC), F32)

        @pl.when(pl.program_id(1) == 0)
        def _():
            dw_ref[...] = jnp.zeros(dw_ref.shape, F32)
            dbdw_ref[...] = jnp.zeros(dbdw_ref.shape, F32)
            dba_ref[...] = jnp.zeros(dba_ref.shape, F32)
            dbg_ref[...] = jnp.zeros(dbg_ref.shape, F32)

        ch = _conv_chunk(t)
        zero8 = jnp.zeros((8, CONV_TC), F32)
        dw_acc = [zero8] * CONV_W
        sum_dc, sum_a, sum_g = zero8, zero8, zero8
        for r0 in range(0, t, ch):
            dcc = d_ref[r0:r0 + ch, :]
            du = jnp.zeros((ch, CONV_TC), F32)
            for j in range(CONV_W):
                back = r0 + CONV_W - 1 - j
                du = du + w_ref[j:j + 1, :] * padd_ref[back:back + ch, :]
                off = r0 + CONV_HALO - (CONV_W - 1) + j
                dw_acc[j] = dw_acc[j] + _fold8(dcc * padu_ref[off:off + ch, :])
            sgc = _sigmoid(gt_ref[r0:r0 + ch, :] + bg_ref[...])
            dga = du * sgc
            dgg = du * padu_ref[CONV_HALO + r0:CONV_HALO + r0 + ch, :] * (1.0 - sgc)
            dga_ref[r0:r0 + ch, :] = dga.astype(BF16)
            dgg_ref[r0:r0 + ch, :] = dgg.astype(BF16)
            sum_dc, sum_a, sum_g = sum_dc + _fold8(dcc), sum_a + _fold8(dga), sum_g + _fold8(dgg)
        for j in range(CONV_W):
            dw_ref[j:j + 1, :] += jnp.sum(dw_acc[j], axis=0, keepdims=True)
        dbdw_ref[...] += jnp.sum(sum_dc, axis=0, keepdims=True)
        dba_ref[...] += jnp.sum(sum_a, axis=0, keepdims=True)
        dbg_ref[...] += jnp.sum(sum_g, axis=0, keepdims=True)

    seq = lambda off: pl.BlockSpec((t, CONV_TC), lambda j, b: (b, off + j))
    vec = lambda off: pl.BlockSpec((1, CONV_TC), lambda j, b: (0, off + j))
    wsp = pl.BlockSpec((CONV_W, CONV_TC), lambda j, b: (0, j))
    act = jax.ShapeDtypeStruct((n, c), BF16)
    v = jax.ShapeDtypeStruct((1, c), F32)
    return pl.pallas_call(
        body, name=name, grid=(nc, batch),
        out_shape=[act, act, jax.ShapeDtypeStruct((CONV_W, c), F32), v, v, v],
        in_specs=[seq(0), seq(0), seq(nc), vec(0), vec(nc), wsp],
        out_specs=[seq(0), seq(0), wsp, vec(0), vec(0), vec(0)],
        scratch_shapes=[pltpu.VMEM((t + CONV_HALO, CONV_TC), F32), pltpu.VMEM((t + CONV_HALO, CONV_TC), F32)],
        compiler_params=_params(("parallel", "arbitrary")))(dc1, glu, glu, b_glu, b_glu, w_dw)


def _split3(x):
    hi = x.astype(BF16)
    r = x - hi.astype(F32)
    mid = r.astype(BF16)
    lo = (r - mid.astype(F32)).astype(BF16)
    return hi, mid, lo


def _tri_matmul(tri, x):
    hi, mid, lo = _split3(x)
    dot = lambda v: jnp.dot(tri, v, preferred_element_type=F32)
    return dot(hi) + dot(mid) + dot(lo)


def _fox_prep_fwd(name, fg, b_fg, batch):
    n, w = fg.shape
    t = n // batch
    nq = t // ROW_TILE

    def body(fg_ref, b_ref, cum_ref):
        row = lax.broadcasted_iota(jnp.int32, (ROW_TILE, ROW_TILE), 0)
        col = lax.broadcasted_iota(jnp.int32, (ROW_TILE, ROW_TILE), 1)
        tri = (row >= col).astype(BF16)
        for k in range(nq):
            rows = slice(k * ROW_TILE, (k + 1) * ROW_TILE)
            z = fg_ref[rows, :] + b_ref[...]
            logf = jnp.minimum(z, 0.0) - jnp.log(1.0 + jnp.exp(-jnp.abs(z)))
            cum = _tri_matmul(tri, logf)
            if k > 0:
                cum = cum + cum_ref[k * ROW_TILE - 1:k * ROW_TILE, :]
            cum_ref[rows, :] = cum

    seq = pl.BlockSpec((t, w), lambda b: (b, 0))
    return pl.pallas_call(body, name=name, out_shape=jax.ShapeDtypeStruct((n, w), F32), grid=(batch,),
                          in_specs=[seq, pl.BlockSpec((1, w), lambda b: (0, 0))], out_specs=seq,
                          compiler_params=_params(("parallel",)))(fg, b_fg)


def _fox_prep_bwd(name, dcum_k, dcum_q, fg, b_fg, batch):
    n, w = fg.shape
    t = n // batch
    nq = t // ROW_TILE

    def body(dk_ref, dq_ref, fg_ref, b_ref, dfg_ref, db_ref, rev_ref):
        row = lax.broadcasted_iota(jnp.int32, (ROW_TILE, ROW_TILE), 0)
        col = lax.broadcasted_iota(jnp.int32, (ROW_TILE, ROW_TILE), 1)
        tri = (col >= row).astype(BF16)

        @pl.when(pl.program_id(0) == 0)
        def _():
            db_ref[...] = jnp.zeros(db_ref.shape, F32)

        for k in reversed(range(nq)):
            rows = slice(k * ROW_TILE, (k + 1) * ROW_TILE)
            dlog = _tri_matmul(tri, dk_ref[rows, :] + dq_ref[rows, :])
            if k < nq - 1:
                dlog = dlog + rev_ref[(k + 1) * ROW_TILE:(k + 1) * ROW_TILE + 1, :]
            rev_ref[rows, :] = dlog
            dfg = dlog * _sigmoid(-(fg_ref[rows, :] + b_ref[...]))
            dfg_ref[rows, :] = dfg.astype(BF16)
            db_ref[...] += jnp.sum(dfg, axis=0, keepdims=True)

    seq = pl.BlockSpec((t, w), lambda b: (b, 0))
    vec = pl.BlockSpec((1, w), lambda b: (0, 0))
    return pl.pallas_call(
        body, name=name, grid=(batch,),
        out_shape=[jax.ShapeDtypeStruct((n, w), BF16), jax.ShapeDtypeStruct((1, w), F32)],
        in_specs=[seq, seq, seq, vec], out_specs=[seq, vec], scratch_shapes=[pltpu.VMEM((t, w), F32)],
        compiler_params=_params(("arbitrary",)))(dcum_k, dcum_q, fg, b_fg)


def _head_masks(x):
    lane = lax.broadcasted_iota(jnp.int32, x.shape, 1)
    zero = jnp.zeros(x.shape, x.dtype)
    return jnp.where(lane < HEAD_DIM, x, zero), jnp.where(lane >= HEAD_DIM, x, zero)


def _attn_specs(t, nq):
    qkv = lambda off: pl.BlockSpec((t, LANES), lambda b, h: (b, off + h))
    cumr = pl.BlockSpec((None, None, nq, 8, ROW_TILE), lambda b, h: (b, h, 0, 0, 0))
    return qkv, cumr


DIAG_ENDS = (176, 352, 528, 688)


def _causal_piece(r0, r1):
    row = lax.broadcasted_iota(jnp.int32, (r1 - r0, r1), 0) + r0
    col = lax.broadcasted_iota(jnp.int32, (r1 - r0, r1), 1)
    return row >= col


def _attn_fwd(name, qkv, cumr, batch):
    n, w3 = qkv.shape
    w = w3 // 3
    t = n // batch
    nq = t // ROW_TILE
    n_pairs = w // LANES
    tq = ROW_TILE
    assert DIAG_ENDS[-1] == tq

    def body(q_ref, k_ref, v_ref, cr_ref, o_ref, lse_ref):
        def update(state, s, v):
            m, l, acc = state
            m_new = jnp.maximum(m, jnp.max(s, axis=1, keepdims=True))
            alpha = jnp.exp(m - m_new)
            p = jnp.exp(s - m_new)
            l = alpha * l + jnp.sum(p, axis=1, keepdims=True)
            acc = alpha * acc + jnp.dot(p.astype(BF16), v, preferred_element_type=F32)
            return m_new, l, acc

        for i in range(nq):
            rows = slice(i * tq, (i + 1) * tq)
            qs = _head_masks(q_ref[rows, :] * 0.125)
            states = []
            for hh in range(2):
                state = (jnp.full((tq, 1), NEG, F32), jnp.zeros((tq, 1), F32), jnp.zeros((tq, LANES), F32))
                for j in range(i):
                    cols = slice(j * tq, (j + 1) * tq)
                    s = _dot_nt(qs[hh], k_ref[cols, :]) - cr_ref[j, hh:hh + 1, :]
                    state = update(state, s, v_ref[cols, :])
                states.append(state)
            r0 = 0
            for r1 in DIAG_ENDS:
                keys = slice(i * tq, i * tq + r1)
                outs, lses = [], []
                for hh in range(2):
                    s = _dot_nt(qs[hh][r0:r1], k_ref[keys, :]) - cr_ref[i, hh:hh + 1, 0:r1]
                    s = jnp.where(_causal_piece(r0, r1), s, NEG)
                    m, l, acc = update(tuple(x_[r0:r1] for x_ in states[hh]), s, v_ref[keys, :])
                    outs.append(acc / l)
                    lses.append(m + jnp.log(l))
                lane = lax.broadcasted_iota(jnp.int32, (r1 - r0, LANES), 1)
                piece = slice(i * tq + r0, i * tq + r1)
                o_ref[piece, :] = jnp.where(lane < HEAD_DIM, outs[0], outs[1]).astype(BF16)
                lse_ref[piece, :] = jnp.where(lane < HEAD_DIM, lses[0], lses[1])
                r0 = r1

    qkv_spec, cumr_spec = _attn_specs(t, nq)
    return pl.pallas_call(
        body, name=name, grid=(batch, n_pairs),
        out_shape=[jax.ShapeDtypeStruct((n, w), BF16), jax.ShapeDtypeStruct((n, w), F32)],
        in_specs=[qkv_spec(0), qkv_spec(n_pairs), qkv_spec(2 * n_pairs), cumr_spec],
        out_specs=[qkv_spec(0), qkv_spec(0)],
        compiler_params=_params(("parallel", "parallel")))(qkv, qkv, qkv, cumr)


def _attn_bwd(name, qkv, o, do, lse, cumr, batch):
    n, w3 = qkv.shape
    w = w3 // 3
    t = n // batch
    nq = t // ROW_TILE
    n_pairs = w // LANES
    tq = ROW_TILE

    def body(q_ref, k_ref, v_ref, o_ref, do_ref, lse_ref, cr_ref, dq_ref, dk_ref, dv_ref, dcr_ref, dcq_ref,
             dk_acc, dv_acc, dq_acc):
        pair = pl.program_id(1)
        dk_acc[...] = jnp.zeros(dk_acc.shape, F32)
        dv_acc[...] = jnp.zeros(dv_acc.shape, F32)
        dcr_ref[...] = jnp.zeros(dcr_ref.shape, F32)

        @pl.when(pair == 0)
        def _():
            dcq_ref[...] = jnp.zeros(dcq_ref.shape, F32)

        def block(hh, q, dout, lse, delta, keys, key_blk, key_hi, q_rows, mask):
            s = _dot_nt(q, k_ref[keys, :]) - cr_ref[key_blk, hh:hh + 1, 0:key_hi]
            p = jnp.exp(s - lse)
            if mask is not None:
                p = jnp.where(mask, p, 0.0)
            dp = _dot_nt(dout, v_ref[keys, :])
            ds = p * (dp - delta)
            pb, dsb = p.astype(BF16), ds.astype(BF16)
            dv_acc[keys, :] += _dot_tn(pb, dout)
            dk_acc[keys, :] += _dot_tn(dsb, q)
            dq_acc[q_rows, :] += jnp.dot(dsb, _head_masks(k_ref[keys, :])[hh], preferred_element_type=F32)
            dcr_ref[key_blk, hh:hh + 1, 0:key_hi] -= jnp.sum(ds, axis=0, keepdims=True)
            return jnp.sum(ds, axis=1, keepdims=True)

        for i in range(nq):
            rows = slice(i * tq, (i + 1) * tq)
            qs = _head_masks(q_ref[rows, :] * 0.125)
            dos = _head_masks(do_ref[rows, :])
            dq_acc[...] = jnp.zeros(dq_acc.shape, F32)
            lses, deltas, row_sums = [], [], []
            for hh in range(2):
                lse = lse_ref[rows, hh * HEAD_DIM:hh * HEAD_DIM + 1]
                delta = jnp.sum(dos[hh].astype(F32) * o_ref[rows, :].astype(F32), axis=1, keepdims=True)
                row_sum = jnp.zeros((tq, 1), F32)
                for j in range(i):
                    row_sum = row_sum + block(hh, qs[hh], dos[hh], lse, delta, slice(j * tq, (j + 1) * tq), j, tq,
                                              slice(0, tq), None)
                lses.append(lse)
                deltas.append(delta)
                row_sums.append(row_sum)
            r0 = 0
            for r1 in DIAG_ENDS:
                keys = slice(i * tq, i * tq + r1)
                sums = [row_sums[hh][r0:r1]
                        + block(hh, qs[hh][r0:r1], dos[hh][r0:r1], lses[hh][r0:r1], deltas[hh][r0:r1], keys, i, r1,
                                slice(r0, r1), _causal_piece(r0, r1)) for hh in range(2)]
                lane = lax.broadcasted_iota(jnp.int32, (r1 - r0, LANES), 1)
                piece = slice(i * tq + r0, i * tq + r1)
                dcq_ref[piece, :] = jnp.where(lane == 2 * pair, sums[0],
                                              jnp.where(lane == 2 * pair + 1, sums[1], dcq_ref[piece, :]))
                r0 = r1
            dq_ref[rows, :] = (dq_acc[...] * 0.125).astype(BF16)
        dk_ref[...] = dk_acc[...].astype(BF16)
        dv_ref[...] = dv_acc[...].astype(BF16)

    qkv_spec, cumr_spec = _attn_specs(t, nq)
    act = jax.ShapeDtypeStruct((n, w), BF16)
    return pl.pallas_call(
        body, name=name, grid=(batch, n_pairs),
        out_shape=[act, act, act, jax.ShapeDtypeStruct(cumr.shape, F32), jax.ShapeDtypeStruct((n, LANES), F32)],
        in_specs=[qkv_spec(0), qkv_spec(n_pairs), qkv_spec(2 * n_pairs), qkv_spec(0), qkv_spec(0), qkv_spec(0),
                  cumr_spec],
        out_specs=[qkv_spec(0), qkv_spec(0), qkv_spec(0), cumr_spec, pl.BlockSpec((t, LANES), lambda b, h: (b, 0))],
        scratch_shapes=[pltpu.VMEM((t, LANES), F32), pltpu.VMEM((t, LANES), F32), pltpu.VMEM((tq, LANES), F32)],
        compiler_params=_params(("parallel", "arbitrary")))(qkv, qkv, qkv, o, do, lse, cumr)


def _adamw(name, parts, w, m, v):
    r, c = w.shape
    tr = 128 if r % 128 == 0 else r
    c1 = 1.0 - ADAM_B1 ** ADAM_STEP
    c2 = 1.0 - ADAM_B2 ** ADAM_STEP

    def body(p_ref, w_ref, m_ref, v_ref, g_ref, d_ref, m2_ref, v2_ref):
        g = p_ref[0].astype(F32)
        for s in range(1, N_DEV):
            g = g + p_ref[s].astype(F32)
        m2 = ADAM_B1 * m_ref[...] + (1.0 - ADAM_B1) * g
        v2 = ADAM_B2 * v_ref[...] + (1.0 - ADAM_B2) * (g * g)
        g_ref[...] = g
        m2_ref[...] = m2
        v2_ref[...] = v2
        d_ref[...] = -ADAM_LR * ((m2 / c1) / (jnp.sqrt(v2 / c2) + ADAM_EPS) + ADAM_WD * w_ref[...])

    blk = pl.BlockSpec((tr, c), lambda i: (i, 0))
    shp = jax.ShapeDtypeStruct((r, c), F32)
    return pl.pallas_call(
        body, name=name, out_shape=[shp] * 4, grid=(r // tr,),
        in_specs=[pl.BlockSpec((N_DEV, tr, c), lambda i: (0, i, 0)), blk, blk, blk], out_specs=[blk] * 4,
        compiler_params=_params(("parallel",)))(parts, w, m, v)


def _cat_small(vals):
    parts = []
    for name in SMALL:
        v = vals[name].reshape(1, -1).astype(F32)
        parts.append(jnp.pad(v, ((0, 0), (0, SMALL_W[name] - v.shape[1]))))
    return jnp.concatenate(parts, axis=1)


def _split_small(row, shapes):
    out, off = {}, 0
    for name in SMALL:
        out[name] = row[0, off:off + SMALL_N[name]].reshape(shapes[name])
        off += SMALL_W[name]
    return out


def _w_in_pieces(lo, hi, n_fg):
    o_fg = 3 * SEG
    out = []
    c = lo
    while c < hi:
        if c < o_fg:
            src, base, end = c // SEG, (c // SEG) * SEG, (c // SEG + 1) * SEG
        elif c < o_fg + n_fg:
            src, base, end = 7, o_fg, o_fg + n_fg
        else:
            k = (c - o_fg - n_fg) // SEG
            src, base, end = 3 + k, o_fg + n_fg + k * SEG, o_fg + n_fg + (k + 1) * SEG
        stop = min(hi, end)
        out.append((src, c - base, stop - base, c))
        c = stop
    return out


def _cols_from_shards(g):
    return jnp.transpose(g, (1, 0, 2)).reshape(g.shape[1], N_DEV * g.shape[2])


def _shards_from_cols(a):
    r, c = a.shape
    return jnp.transpose(a.reshape(r, N_DEV, c // N_DEV), (1, 0, 2))


def kernel(x, meta_tokens, norm_mix_gain, w_in, b_forget, w_attn_out, b_glu, conv_dw_w, conv_dw_b, conv_ln_gain, conv_ln_bias, w_conv_out, b_conv_out, w_out, norm_mlp_gain, w_mlp_up, w_mlp_down, final_norm_gain, loss_target, m_meta_tokens, m_norm_mix_gain, m_w_in, m_b_forget, m_w_attn_out, m_b_glu, m_conv_dw_w, m_conv_dw_b, m_conv_ln_gain, m_conv_ln_bias, m_w_conv_out, m_b_conv_out, m_w_out, m_norm_mlp_gain, m_w_mlp_up, m_w_mlp_down, m_final_norm_gain, v_meta_tokens, v_norm_mix_gain, v_w_in, v_b_forget, v_w_attn_out, v_b_glu, v_conv_dw_w, v_conv_dw_b, v_conv_ln_gain, v_conv_ln_bias, v_w_conv_out, v_b_conv_out, v_w_out, v_norm_mlp_gain, v_w_mlp_up, v_w_mlp_down, v_final_norm_gain):
    weights = dict(meta_tokens=meta_tokens, norm_mix_gain=norm_mix_gain, w_in=w_in, b_forget=b_forget, w_attn_out=w_attn_out, b_glu=b_glu, conv_dw_w=conv_dw_w, conv_dw_b=conv_dw_b, conv_ln_gain=conv_ln_gain, conv_ln_bias=conv_ln_bias, w_conv_out=w_conv_out, b_conv_out=b_conv_out, w_out=w_out, norm_mlp_gain=norm_mlp_gain, w_mlp_up=w_mlp_up, w_mlp_down=w_mlp_down, final_norm_gain=final_norm_gain)
    mom_m = dict(meta_tokens=m_meta_tokens, norm_mix_gain=m_norm_mix_gain, w_in=m_w_in, b_forget=m_b_forget, w_attn_out=m_w_attn_out, b_glu=m_b_glu, conv_dw_w=m_conv_dw_w, conv_dw_b=m_conv_dw_b, conv_ln_gain=m_conv_ln_gain, conv_ln_bias=m_conv_ln_bias, w_conv_out=m_w_conv_out, b_conv_out=m_b_conv_out, w_out=m_w_out, norm_mlp_gain=m_norm_mlp_gain, w_mlp_up=m_w_mlp_up, w_mlp_down=m_w_mlp_down, final_norm_gain=m_final_norm_gain)
    mom_v = dict(meta_tokens=v_meta_tokens, norm_mix_gain=v_norm_mix_gain, w_in=v_w_in, b_forget=v_b_forget, w_attn_out=v_w_attn_out, b_glu=v_b_glu, conv_dw_w=v_conv_dw_w, conv_dw_b=v_conv_dw_b, conv_ln_gain=v_conv_ln_gain, conv_ln_bias=v_conv_ln_bias, w_conv_out=v_w_conv_out, b_conv_out=v_b_conv_out, w_out=v_w_out, norm_mlp_gain=v_norm_mlp_gain, w_mlp_up=v_w_mlp_up, w_mlp_down=v_w_mlp_down, final_norm_gain=v_final_norm_gain)
    names = list(weights)
    batch, seq, d = x.shape
    t = seq + N_META
    n = batch * t
    nq = t // ROW_TILE
    n_pairs = d // LANES
    assert t % ROW_TILE == 0 and d == SEG

    gather_a = _exchange_start("gather_in_start", [(w_in[0].astype(BF16), False), (meta_tokens, False),
                                                   (conv_dw_w[0], False)])
    w_in_g, meta_g, w_dw_g = _exchange_wait("gather_in_wait", gather_a, gather_a["token"])
    gather_b = _exchange_start("gather_rest_start", [
        (w_attn_out[0].astype(BF16), False), (w_conv_out[0].astype(BF16), False), (w_out[0].astype(BF16), False),
        (w_mlp_up[0].astype(BF16), False), (w_mlp_down[0].astype(BF16), False)])
    n_fg = b_forget.shape[1]
    shard_w = w_in.shape[2]
    seg_cols = [[] for _ in range(8)]
    for p in range(N_DEV):
        for src, c0, c1, orig in _w_in_pieces(p * shard_w, (p + 1) * shard_w, n_fg):
            seg_cols[src].append(w_in_g[p][:, orig - p * shard_w:orig - p * shard_w + c1 - c0])
    w_pad = jnp.concatenate([c_ for src in range(8) for c_ in seg_cols[src]]
                            + [jnp.zeros((d, FG_PAD - n_fg), BF16)], axis=1)
    d_ff = w_mlp_down.shape[1] * N_DEV
    ff_blk = d_ff // N_DEV
    meta_f = _cols_from_shards(meta_g)
    w_dw = _cols_from_shards(w_dw_g)

    row2 = lambda v: v.reshape(1, -1)
    g1, g2, g3 = row2(norm_mix_gain) + gather_b["token"][0:1, 0:1], row2(norm_mlp_gain), row2(final_norm_gain)
    b_fg = jnp.pad(b_forget, ((0, 0), (0, FG_PAD - n_fg)))
    h0 = jnp.concatenate([jnp.broadcast_to(meta_f[None], (batch, N_META, d)), x], axis=1).reshape(n, d)
    tgt = jnp.concatenate([jnp.zeros((batch, N_META, d), F32), loss_target], axis=1).reshape(n, d)

    hn1 = _rms_fwd("rms1", h0, g1)
    qkv = _mm_nn("proj_qkv", hn1, w_pad, _w_cols(d, SEG, 0), 3 * SEG, SEG, BF16)
    glu = _mm_nn("proj_glu", hn1, w_pad, _w_cols(d, SEG, 3), 2 * SEG, SEG, F32)
    gates = _mm_nn("proj_gates", hn1, w_pad, _w_cols(d, SEG, 5), 2 * SEG, SEG, F32)
    fg = _mm_nn("proj_fg", hn1, w_pad, _w_cols(d, FG_PAD, 7 * SEG // FG_PAD), FG_PAD, FG_PAD, F32)

    cum = _fox_prep_fwd("fox_cumsum", fg, b_fg, batch)
    cum_h = cum.reshape(batch, t, FG_PAD)[:, :, :2 * n_pairs].reshape(batch, t, n_pairs, 2)
    cumr = jnp.transpose(cum_h.reshape(batch, nq, ROW_TILE, n_pairs, 2), (0, 3, 1, 4, 2))
    cumr = jnp.pad(cumr, ((0, 0), (0, 0), (0, 0), (0, 6), (0, 0)))
    o, lse = _attn_fwd("attn_fwd", qkv, cumr, batch)
    rest = _exchange_wait("gather_rest_wait", gather_b, o)
    w_ao, w_co, w_o = [r_.reshape(d, d) for r_ in rest[:3]]
    w_up = rest[3]
    w_dn = rest[4].reshape(d_ff, d)
    a = _mm_nn("attn_out", o, w_ao, _w_cols(d, d, 0), d, d, F32)

    c1 = _glu_conv_fwd("glu_conv", glu, b_glu, w_dw, conv_dw_b, batch)
    c3 = _ln_silu_fwd("ln_silu", c1, conv_ln_gain, conv_ln_bias)
    c = _mm_nn("conv_out", c3, w_co, _w_cols(d, d, 0), d, d, F32)

    mrg = _merge_fwd("merge", gates, a, c, b_conv_out)
    mo = _mm_nn("mix_out", mrg, w_o, _w_cols(d, d, 0), d, d, F32)
    h1, hn2 = _rms_fwd("resid_rms2", h0, g2, res=mo)
    per = ff_blk // 512
    up, act = _mm_nn("mlp_up", hn2, w_up, pl.BlockSpec((None, d, 512), lambda i, j: (j // per, 0, j % per)),
                     d_ff, 512, BF16, relu2=True)
    dn = _mm_nn("mlp_down", act, w_dn, _w_cols(d_ff, d, 0), d, d, F32, tm=ROW_TILE)
    dh2, dh2b, loss_blk, dg3 = _final("final_loss", h1, dn, tgt, g3, batch)

    dup = _mm_nt("d_mlp_down", [(dh2b, _a_rows(d), w_dn, _w_rows(d, d))], n, d_ff, d, BF16, relu_bwd_of=up)
    dw_dn = _grad_w("gw_mlp_down", act, dh2b)
    dhn2 = _mm_nt("d_mlp_up", [(dup, _a_rows(ff_blk, g), w_up, pl.BlockSpec((None, 512, ff_blk), lambda i, j, g=g: (g, j, 0)))
                               for g in range(N_DEV)], n, d, 512, F32)
    dw_up = _mm_tn("gw_mlp_up", hn2, lambda a_: 0, d, dup, lambda b_: b_, ff_blk, (N_DEV, d, ff_blk),
                   pl.BlockSpec((None, d, ff_blk), lambda a_, b_, k: (b_, 0, 0)), (1, N_DEV))
    scatter_1 = _exchange_start("scatter_mlp_start", [(dw_dn.reshape(N_DEV, ff_blk, d), True), (dw_up, True)])
    dh1, dg2, dh1b = _rms_bwd("rms2_bwd", dhn2, h1, g2 + scatter_1["token"][0:1, 0:1], dh2, batch, with_bf16=True)

    dm = _mm_nt("d_mix_out", [(dh1b, _a_rows(d), w_o, _w_rows(d, d))], n, d, d, F32)
    dw_o = _grad_w("gw_mix_out", mrg, dh1b)
    da, dc, dga, dgc, dbco = _merge_bwd("merge_bwd", dm, gates, a, c, b_conv_out)

    do = _mm_nt("d_attn_out", [(da, _a_rows(d), w_ao, _w_rows(d, d))], n, d, d, BF16)
    dw_ao = _grad_w("gw_attn_out", o, da)
    dc3 = _mm_nt("d_conv_out", [(dc, _a_rows(d), w_co, _w_rows(d, d))], n, d, d, F32)
    dw_co = _grad_w("gw_conv_out", c3, dc)

    scatter_2 = _exchange_start("scatter_mix_start", [(dw_.reshape(N_DEV, d // N_DEV, d), True)
                                                      for dw_ in (dw_o, dw_ao, dw_co)])
    dc1, dg_ln, db_ln = _ln_silu_bwd("ln_silu_bwd", dc3, c1, conv_ln_gain + scatter_2["token"][0:1, 0:1],
                                     conv_ln_bias)
    dglu_a, dglu_g, dw_dw, db_dw, dbg_a, dbg_g = _glu_conv_bwd("glu_conv_bwd", dc1, glu, b_glu, w_dw, batch)

    dq, dk, dv, dcumr, dcum_q = _attn_bwd("attn_bwd", qkv, o, do, lse, cumr, batch)
    dcum_k = jnp.transpose(dcumr[:, :, :, :2, :], (0, 2, 4, 1, 3)).reshape(n, 2 * n_pairs)
    dcum_k = jnp.pad(dcum_k, ((0, 0), (0, FG_PAD - 2 * n_pairs)))
    dfg, db_fg = _fox_prep_bwd("fox_cumsum_bwd", dcum_k, dcum_q, fg, b_fg, batch)

    segs = [dq, dk, dv, dglu_a, dglu_g, dga, dgc]
    gw_src = [_grad_w("gw_in_%d" % i, hn1, s_) for i, s_ in enumerate(segs)] + [_grad_w("gw_in_fg", hn1, dfg)]
    dw_in = jnp.stack([jnp.concatenate([gw_src[src][:, c0:c1]
                                        for src, c0, c1, _ in _w_in_pieces(p * shard_w, (p + 1) * shard_w, n_fg)], axis=1)
                       for p in range(N_DEV)])
    scatter_3 = _exchange_start("scatter_in_start", [(dw_in, True)])
    pairs = [(s_, _a_rows(SEG, 0, ROW_TILE), w_pad, _w_rows(512, SEG, i)) for i, s_ in enumerate(segs)]
    pairs.append((dfg, _a_rows(FG_PAD, 0, ROW_TILE), w_pad, _w_rows(512, FG_PAD, 7 * SEG // FG_PAD)))
    dhn1 = _mm_nt("d_proj_in", pairs, n, d, 512, F32, tm=ROW_TILE, after=scatter_3["token"])
    dh0, dg1, dmeta = _rms_bwd("rms1_bwd", dhn1, h0, g1, dh1, batch, with_meta=True)
    grad_x = dh0.reshape(batch, t, d)[:, N_META:, :]

    small_g = dict(norm_mix_gain=dg1, b_forget=db_fg[:, :n_fg], b_glu=jnp.concatenate([dbg_a, dbg_g], axis=1),
                   conv_dw_b=db_dw, conv_ln_gain=dg_ln, conv_ln_bias=db_ln, b_conv_out=dbco, norm_mlp_gain=dg2,
                   final_norm_gain=dg3)
    scatter_4 = _exchange_start("scatter_small_start", [
        (_shards_from_cols(dmeta), True), (_shards_from_cols(dw_dw), True), (_cat_small(small_g), False),
        (loss_blk[0:1, :], False)])

    grads, deltas, new_m, new_v = {}, {}, {}, {}

    def update(k, parts):
        shp = weights[k].shape
        w2 = lambda arr: arr.reshape(parts.shape[1:])
        res_ = _adamw("adamw_" + k, parts, w2(weights[k]), w2(mom_m[k]), w2(mom_v[k]))
        grads[k], deltas[k], new_m[k], new_v[k] = [r.reshape(shp) for r in res_]

    for k, parts in zip(("w_mlp_down", "w_mlp_up"), _exchange_wait("scatter_mlp_wait", scatter_1, scatter_4["token"])):
        update(k, parts)
    for k, parts in zip(("w_out", "w_attn_out", "w_conv_out"),
                        _exchange_wait("scatter_mix_wait", scatter_2, deltas["w_mlp_up"])):
        update(k, parts)
    update("w_in", _exchange_wait("scatter_in_wait", scatter_3, deltas["w_conv_out"])[0])
    reduced = _exchange_wait("scatter_small_wait", scatter_4, deltas["w_in"])
    loss = jnp.sum(reduced.pop()[:, 0, 0])
    for k, parts in zip(("meta_tokens", "conv_dw_w"), reduced[:-1]):
        update(k, parts)
    res = _adamw("adamw_small", reduced[-1], _cat_small(weights), _cat_small(mom_m), _cat_small(mom_v))
    shapes = {k: weights[k].shape for k in SMALL}
    for dst, r in zip((grads, deltas, new_m, new_v), res):
        dst.update(_split_small(r, shapes))

    return (loss, grad_x, *[grads[k] for k in names], *[deltas[k] for k in names],
            *[new_m[k] for k in names], *[new_v[k] for k in names])
```

```python
import functools

import jax
import jax.numpy as jnp
from jax import lax
from jax.experimental import pallas as pl
from jax.experimental.pallas import tpu as pltpu

F32, BF16 = jnp.float32, jnp.bfloat16
N_DEV = 8
N_META = 16
HEAD_DIM = 64
LANES = 128
CONV_W = 31
RMS_EPS = 1e-6
LN_EPS = 1e-5
ROW_TILE = 688
MM_TM = 2 * ROW_TILE
SEG = 1024
FG_PAD = 128
VMEM_LIMIT = 56 * 1024 * 1024
ADAM_LR, ADAM_B1, ADAM_B2, ADAM_EPS, ADAM_WD, ADAM_STEP = 0.001, 0.9, 0.999, 1e-08, 0.01, 10
NEG = -1e30

SMALL = ("norm_mix_gain", "b_forget", "b_glu", "conv_dw_b", "conv_ln_gain", "conv_ln_bias", "b_conv_out",
         "norm_mlp_gain", "final_norm_gain")
SMALL_W = {"norm_mix_gain": 1024, "b_forget": 128, "b_glu": 2048, "conv_dw_b": 1024, "conv_ln_gain": 1024,
           "conv_ln_bias": 1024, "b_conv_out": 1024, "norm_mlp_gain": 1024, "final_norm_gain": 1024}
SMALL_N = {"norm_mix_gain": 1024, "b_forget": 16, "b_glu": 2048, "conv_dw_b": 1024, "conv_ln_gain": 1024,
           "conv_ln_bias": 1024, "b_conv_out": 1024, "norm_mlp_gain": 1024, "final_norm_gain": 1024}


def _params(sem=None):
    return pltpu.CompilerParams(dimension_semantics=sem, vmem_limit_bytes=VMEM_LIMIT)


def _sigmoid(x):
    return 1.0 / (1.0 + jnp.exp(-x))


def _dot_nt(a, b):
    return lax.dot_general(a, b, (((1,), (1,)), ((), ())), preferred_element_type=F32)


def _dot_tn(a, b):
    return lax.dot_general(a, b, (((0,), (0,)), ((), ())), preferred_element_type=F32)


HBM_SPEC = pl.BlockSpec(memory_space=pltpu.HBM)
SEM_SPEC = pl.BlockSpec(memory_space=pltpu.SEMAPHORE)
DATAFLOW = pltpu.SideEffectType.DATAFLOW_SIDE_EFFECTING


def _device_index():
    return 4 * lax.axis_index("x") + 2 * lax.axis_index("y") + lax.axis_index("c")


def _peers():
    x, y, c = lax.axis_index("x"), lax.axis_index("y"), lax.axis_index("c")
    out = []
    for k in range(1, N_DEV):
        px = 1 - x if k & 4 else x
        py = 1 - y if k & 2 else y
        pc = 1 - c if k & 1 else c
        out.append((k, (px, py, pc), 4 * px + 2 * py + pc))
    return out


def _peer_copy(per_dest, src_ref, land_ref, send_sems, recv_sems, a, k, dev, peer):
    src = src_ref.at[peer] if per_dest else src_ref
    return pltpu.make_async_remote_copy(
        src_ref=src, dst_ref=land_ref.at[_device_index()], send_sem=send_sems.at[a * (N_DEV - 1) + k - 1],
        recv_sem=recv_sems.at[a * (N_DEV - 1) + k - 1], device_id=dev, device_id_type=pl.DeviceIdType.MESH)


ALL_PEERS = tuple(range(1, N_DEV))
CHIP_PEERS = (1, 2, 4, 6)
FAR_PEERS = (2, 4, 6)


def _exchange_start(name, items, ks=ALL_PEERS):
    n = len(items)
    per_dest = [it[1] for it in items]

    def body(*refs):
        srcs, lands = refs[:n], refs[n:2 * n]
        send_sems, recv_sems, token = refs[2 * n], refs[2 * n + 1], refs[-1]
        for a in range(n):
            for k, dev, peer in _peers():
                if k in ks:
                    _peer_copy(per_dest[a], srcs[a], lands[a], send_sems, recv_sems, a, k, dev, peer).start()
        token[...] = jnp.zeros(token.shape, F32)

    srcs = [pltpu.with_memory_space_constraint(it[0], pltpu.HBM) for it in items]
    lands = []
    for arr, pd in items:
        shp = arr.shape if pd else (N_DEV,) + arr.shape
        lands.append(pltpu.with_memory_space_constraint(lax.empty(shp, arr.dtype), pltpu.HBM))
    sems = pltpu.SemaphoreType.DMA((n * (N_DEV - 1),))
    res = pl.pallas_call(
        body, name=name,
        out_shape=(sems, sems, *[pltpu.HBM(a_.shape, a_.dtype) for a_ in srcs + lands],
                   jax.ShapeDtypeStruct((8, 128), F32)),
        in_specs=[HBM_SPEC] * (2 * n),
        out_specs=(SEM_SPEC, SEM_SPEC, *[HBM_SPEC] * (2 * n), pl.BlockSpec(memory_space=pltpu.VMEM)),
        input_output_aliases={i: 2 + i for i in range(2 * n)},
        compiler_params=pltpu.CompilerParams(has_side_effects=DATAFLOW),
    )(*srcs, *lands)
    return dict(per_dest=per_dest, ks=ks, send=res[0], recv=res[1], srcs=list(res[2:2 + n]),
                lands=list(res[2 + n:2 + 2 * n]), token=res[-1])


def _fill_own(per_dest, srcs, lands):
    me = _device_index()
    out = []
    for pd, src, land in zip(per_dest, srcs, lands):
        own = lax.dynamic_index_in_dim(src, me, 0, keepdims=True) if pd else src[None]
        out.append(lax.dynamic_update_slice_in_dim(land, own, me, axis=0))
    return out


def _exchange_wait(name, started, after, fill_own=True):
    per_dest = started["per_dest"]
    n = len(per_dest)

    def body(*refs):
        srcs, lands = refs[:n], refs[n:2 * n]
        send_sems, recv_sems = refs[2 * n], refs[2 * n + 1]
        for a in range(n):
            for k, dev, peer in _peers():
                if k in started["ks"]:
                    cp = _peer_copy(per_dest[a], srcs[a], lands[a], send_sems, recv_sems, a, k, dev, peer)
                    cp.wait_send()
                    cp.wait_recv()

    bufs = started["srcs"] + started["lands"]
    res = pl.pallas_call(
        body, name=name, out_shape=tuple(pltpu.HBM(b_.shape, b_.dtype) for b_ in bufs),
        in_specs=[HBM_SPEC] * (2 * n) + [SEM_SPEC, SEM_SPEC, pl.BlockSpec(memory_space=pl.ANY)],
        out_specs=tuple([HBM_SPEC] * (2 * n)), input_output_aliases={i: i for i in range(2 * n)},
        compiler_params=pltpu.CompilerParams(has_side_effects=DATAFLOW),
    )(*bufs, started["send"], started["recv"], after)
    return _fill_own(per_dest, res[:n], res[n:]) if fill_own else list(res[n:])


def _pass_on_copy(land_ref, send_sems, recv_sems, a, idx, slot):
    sibling = (lax.axis_index("x"), lax.axis_index("y"), 1 - lax.axis_index("c"))
    return pltpu.make_async_remote_copy(
        src_ref=land_ref.at[slot], dst_ref=land_ref.at[slot], send_sem=send_sems.at[a * len(FAR_PEERS) + idx],
        recv_sem=recv_sems.at[a * len(FAR_PEERS) + idx], device_id=sibling, device_id_type=pl.DeviceIdType.MESH)


def _pass_on_start(name, lands):
    n = len(lands)

    def body(*refs):
        send_sems, recv_sems, token = refs[n], refs[n + 1], refs[-1]
        slots = {k: peer for k, _, peer in _peers()}
        for a in range(n):
            for idx, k in enumerate(FAR_PEERS):
                _pass_on_copy(refs[a], send_sems, recv_sems, a, idx, slots[k]).start()
        token[...] = jnp.zeros(token.shape, F32)

    lands = [pltpu.with_memory_space_constraint(l_, pltpu.HBM) for l_ in lands]
    sems = pltpu.SemaphoreType.DMA((n * len(FAR_PEERS),))
    res = pl.pallas_call(
        body, name=name,
        out_shape=(sems, sems, *[pltpu.HBM(l_.shape, l_.dtype) for l_ in lands], jax.ShapeDtypeStruct((8, 128), F32)),
        in_specs=[HBM_SPEC] * n, out_specs=(SEM_SPEC, SEM_SPEC, *[HBM_SPEC] * n, pl.BlockSpec(memory_space=pltpu.VMEM)),
        input_output_aliases={i: 2 + i for i in range(n)},
        compiler_params=pltpu.CompilerParams(has_side_effects=DATAFLOW),
    )(*lands)
    return dict(send=res[0], recv=res[1], lands=list(res[2:2 + n]), token=res[-1])


def _pass_on_wait(name, passed, after, owns):
    n = len(passed["lands"])

    def body(*refs):
        send_sems, recv_sems = refs[n], refs[n + 1]
        slots = {k: peer for k, _, peer in _peers()}
        for a in range(n):
            for idx, k in enumerate(FAR_PEERS):
                _pass_on_copy(refs[a], send_sems, recv_sems, a, idx, slots[k]).wait_send()
                _pass_on_copy(refs[a], send_sems, recv_sems, a, idx, slots[k ^ 1]).wait_recv()

    res = pl.pallas_call(
        body, name=name, out_shape=tuple(pltpu.HBM(l_.shape, l_.dtype) for l_ in passed["lands"]),
        in_specs=[HBM_SPEC] * n + [SEM_SPEC, SEM_SPEC, pl.BlockSpec(memory_space=pl.ANY)],
        out_specs=tuple([HBM_SPEC] * n), input_output_aliases={i: i for i in range(n)},
        compiler_params=pltpu.CompilerParams(has_side_effects=DATAFLOW),
    )(*passed["lands"], passed["send"], passed["recv"], after)
    return _fill_own([False] * n, owns, res)


def _mm_nn(name, x, w, w_spec, n_out, tn, out_dtype, relu2=False, tm=MM_TM):
    m, k = x.shape

    def body(x_ref, w_ref, *outs):
        acc = jnp.dot(x_ref[...], w_ref[...], preferred_element_type=F32)
        outs[0][...] = acc.astype(outs[0].dtype)
        if relu2:
            r = jnp.maximum(acc, 0.0)
            outs[1][...] = (r * r).astype(outs[1].dtype)

    o_spec = pl.BlockSpec((tm, tn), lambda i, j: (i, j))
    shapes = [jax.ShapeDtypeStruct((m, n_out), out_dtype)]
    if relu2:
        shapes.append(jax.ShapeDtypeStruct((m, n_out), BF16))
    res = pl.pallas_call(
        body, name=name, out_shape=shapes, grid=(m // tm, n_out // tn),
        in_specs=[pl.BlockSpec((tm, k), lambda i, j: (i, 0)), w_spec],
        out_specs=[o_spec] * len(shapes), compiler_params=_params(("parallel", "parallel")),
    )(x, w)
    return res if relu2 else res[0]


def _mm_nt(name, pairs, m, n_out, tn, out_dtype, relu_bwd_of=None, tm=MM_TM, after=None):
    np_ = len(pairs)

    def body(*refs):
        acc = None
        for p in range(np_):
            d = _dot_nt(refs[2 * p][...], refs[2 * p + 1][...])
            acc = d if acc is None else acc + d
        if relu_bwd_of is not None:
            acc = acc * (2.0 * jnp.maximum(refs[2 * np_][...].astype(F32), 0.0))
        refs[-1][...] = acc.astype(refs[-1].dtype)

    o_spec = pl.BlockSpec((tm, tn), lambda i, j: (i, j))
    operands, specs = [], []
    for a, a_spec, w, w_spec in pairs:
        operands += [a, w]
        specs += [a_spec, w_spec]
    if relu_bwd_of is not None:
        operands.append(relu_bwd_of)
        specs.append(o_spec)
    if after is not None:
        operands.append(after)
        specs.append(pl.BlockSpec((8, 128), lambda i, j: (0, 0)))
    return pl.pallas_call(
        body, name=name, out_shape=jax.ShapeDtypeStruct((m, n_out), out_dtype), grid=(m // tm, n_out // tn),
        in_specs=specs, out_specs=o_spec, compiler_params=_params(("parallel", "parallel")),
    )(*operands)


def _mm_tn(name, x, x_col, ta, dy, dy_col, tb, out_shape, out_spec, grid_ab):
    m = x.shape[0]
    tm = MM_TM
    nk = m // tm

    def body(x_ref, dy_ref, o_ref, acc_ref):
        k = pl.program_id(2)

        @pl.when(k == 0)
        def _():
            acc_ref[...] = jnp.zeros(acc_ref.shape, F32)

        acc_ref[...] += _dot_tn(x_ref[...], dy_ref[...])

        @pl.when(k == nk - 1)
        def _():
            o_ref[...] = acc_ref[...].astype(BF16)

    return pl.pallas_call(
        body, name=name, out_shape=jax.ShapeDtypeStruct(out_shape, BF16), grid=grid_ab + (nk,),
        in_specs=[pl.BlockSpec((tm, ta), lambda a, b, k: (k, x_col(a))),
                  pl.BlockSpec((tm, tb), lambda a, b, k: (k, dy_col(b)))],
        out_specs=out_spec, scratch_shapes=[pltpu.VMEM((ta, tb), F32)],
        compiler_params=_params(("parallel", "parallel", "arbitrary")),
    )(x, dy)


def _w_cols(k, tn, off_blocks):
    return pl.BlockSpec((k, tn), lambda i, j: (0, off_blocks + j))


def _a_rows(kw, col_block=0, tm=MM_TM):
    return pl.BlockSpec((tm, kw), lambda i, j: (i, col_block))


def _w_rows(tn, kw, col_block=0):
    return pl.BlockSpec((tn, kw), lambda i, j: (j, col_block))


def _grad_w(name, x, dy):
    nb = dy.shape[1]
    tb = min(nb, 1024)
    return _mm_tn(name, x, lambda a: a, 1024, dy, lambda b: b, tb, (x.shape[1], nb),
                  pl.BlockSpec((1024, tb), lambda a, b, k: (a, b)), (x.shape[1] // 1024, nb // tb))


def _row_spec(width):
    return pl.BlockSpec((ROW_TILE, width), lambda i: (i, 0))


def _vec_spec(width):
    return pl.BlockSpec((1, width), lambda i: (0, 0))


def _rms_fwd(name, h, g, res=None):
    n, d = h.shape

    def body(*refs):
        if res is None:
            h_ref, g_ref, hn_ref = refs
            hv = h_ref[...]
        else:
            h_ref, r_ref, g_ref, hs_ref, hn_ref = refs
            hv = h_ref[...] + r_ref[...]
            hs_ref[...] = hv
        r = lax.rsqrt(jnp.mean(hv * hv, axis=-1, keepdims=True) + RMS_EPS)
        hn_ref[...] = (hv * r * g_ref[...]).astype(BF16)

    ins = [h, g] if res is None else [h, res, g]
    in_specs = [_row_spec(d), _vec_spec(d)] if res is None else [_row_spec(d), _row_spec(d), _vec_spec(d)]
    hn_shape = jax.ShapeDtypeStruct((n, d), BF16)
    if res is None:
        out_shape, out_specs = hn_shape, _row_spec(d)
    else:
        out_shape, out_specs = [jax.ShapeDtypeStruct((n, d), F32), hn_shape], [_row_spec(d), _row_spec(d)]
    return pl.pallas_call(body, name=name, out_shape=out_shape, grid=(n // ROW_TILE,), in_specs=in_specs,
                          out_specs=out_specs, compiler_params=_params(("parallel",)))(*ins)


def _rms_bwd(name, dhn, h, g, dres, batch, with_bf16=False, with_meta=False):
    n, d = h.shape
    t = n // batch
    nt = t // ROW_TILE

    def body(dhn_ref, h_ref, g_ref, dres_ref, *outs):
        first = (pl.program_id(0) == 0) & (pl.program_id(1) == 0)
        hv = h_ref[...]
        r = lax.rsqrt(jnp.mean(hv * hv, axis=-1, keepdims=True) + RMS_EPS)
        nrm = hv * r
        dn = dhn_ref[...] * g_ref[...]
        dh = dres_ref[...] + r * (dn - nrm * jnp.mean(dn * nrm, axis=-1, keepdims=True))
        outs[0][...] = dh
        dg_ref = outs[1]

        @pl.when(first)
        def _():
            dg_ref[...] = jnp.zeros(dg_ref.shape, F32)

        dg_ref[...] += jnp.sum(dhn_ref[...] * nrm, axis=0, keepdims=True)
        nxt = 2
        if with_bf16:
            outs[nxt][...] = dh.astype(BF16)
            nxt += 1
        if with_meta:
            meta_ref = outs[nxt]

            @pl.when(first)
            def _():
                meta_ref[...] = jnp.zeros(meta_ref.shape, F32)

            @pl.when(pl.program_id(1) == 0)
            def _():
                meta_ref[...] += dh[0:N_META, :]

    row = pl.BlockSpec((ROW_TILE, d), lambda b, j: (b * nt + j, 0))
    vec = pl.BlockSpec((1, d), lambda b, j: (0, 0))
    shapes = [jax.ShapeDtypeStruct((n, d), F32), jax.ShapeDtypeStruct((1, d), F32)]
    specs = [row, vec]
    if with_bf16:
        shapes.append(jax.ShapeDtypeStruct((n, d), BF16))
        specs.append(row)
    if with_meta:
        shapes.append(jax.ShapeDtypeStruct((N_META, d), F32))
        specs.append(pl.BlockSpec((N_META, d), lambda b, j: (0, 0)))
    return pl.pallas_call(body, name=name, out_shape=shapes, grid=(batch, nt), in_specs=[row, row, vec, row],
                          out_specs=specs, compiler_params=_params(("arbitrary", "arbitrary")))(dhn, h, g, dres)


def _final(name, h1, dn, tgt, g, batch):
    n, d = h1.shape
    t = n // batch
    nt = t // ROW_TILE

    def body(h1_ref, dn_ref, tgt_ref, g_ref, dh_ref, dhb_ref, loss_ref, dg_ref):
        first = (pl.program_id(0) == 0) & (pl.program_id(1) == 0)
        hv = h1_ref[...] + dn_ref[...]
        r = lax.rsqrt(jnp.mean(hv * hv, axis=-1, keepdims=True) + RMS_EPS)
        nrm = hv * r
        gv = g_ref[...]
        pos = pl.program_id(1) * ROW_TILE + lax.broadcasted_iota(jnp.int32, (ROW_TILE, 1), 0)
        diff = jnp.where(pos >= N_META, nrm * gv - tgt_ref[...], 0.0)
        dy = diff * (1.0 / d)

        @pl.when(first)
        def _():
            loss_ref[...] = jnp.zeros(loss_ref.shape, F32)
            dg_ref[...] = jnp.zeros(dg_ref.shape, F32)

        loss_ref[...] += jnp.full(loss_ref.shape, 0.5 / d, F32) * jnp.sum(diff * diff)
        dg_ref[...] += jnp.sum(dy * nrm, axis=0, keepdims=True)
        dng = dy * gv
        dh = r * (dng - nrm * jnp.mean(dng * nrm, axis=-1, keepdims=True))
        dh_ref[...] = dh
        dhb_ref[...] = dh.astype(BF16)

    row = pl.BlockSpec((ROW_TILE, d), lambda b, j: (b * nt + j, 0))
    vec = pl.BlockSpec((1, d), lambda b, j: (0, 0))
    return pl.pallas_call(
        body, name=name, grid=(batch, nt), in_specs=[row, row, row, vec],
        out_shape=[jax.ShapeDtypeStruct((n, d), F32), jax.ShapeDtypeStruct((n, d), BF16),
                   jax.ShapeDtypeStruct((8, 128), F32), jax.ShapeDtypeStruct((1, d), F32)],
        out_specs=[row, row, pl.BlockSpec((8, 128), lambda b, j: (0, 0)), vec],
        compiler_params=_params(("arbitrary", "arbitrary")))(h1, dn, tgt, g)


def _ln_silu_fwd(name, c1, g, b):
    n, d = c1.shape

    def body(c_ref, g_ref, b_ref, o_ref):
        xv = c_ref[...]
        xc = xv - jnp.mean(xv, axis=-1, keepdims=True)
        rstd = lax.rsqrt(jnp.mean(xc * xc, axis=-1, keepdims=True) + LN_EPS)
        c2 = xc * rstd * g_ref[...] + b_ref[...]
        o_ref[...] = (c2 * _sigmoid(c2)).astype(BF16)

    return pl.pallas_call(body, name=name, out_shape=jax.ShapeDtypeStruct((n, d), BF16), grid=(n // ROW_TILE,),
                          in_specs=[_row_spec(d), _vec_spec(d), _vec_spec(d)], out_specs=_row_spec(d),
                          compiler_params=_params(("parallel",)))(c1, g, b)


def _ln_silu_bwd(name, dc3, c1, g, b):
    n, d = c1.shape

    def body(d_ref, c_ref, g_ref, b_ref, dc1_ref, dg_ref, db_ref):
        xv = c_ref[...]
        xc = xv - jnp.mean(xv, axis=-1, keepdims=True)
        rstd = lax.rsqrt(jnp.mean(xc * xc, axis=-1, keepdims=True) + LN_EPS)
        xh = xc * rstd
        c2 = xh * g_ref[...] + b_ref[...]
        s = _sigmoid(c2)
        dc2 = d_ref[...] * (s * (1.0 + c2 * (1.0 - s)))

        @pl.when(pl.program_id(0) == 0)
        def _():
            dg_ref[...] = jnp.zeros(dg_ref.shape, F32)
            db_ref[...] = jnp.zeros(db_ref.shape, F32)

        dg_ref[...] += jnp.sum(dc2 * xh, axis=0, keepdims=True)
        db_ref[...] += jnp.sum(dc2, axis=0, keepdims=True)
        dxh = dc2 * g_ref[...]
        dc1_ref[...] = rstd * (dxh - jnp.mean(dxh, axis=-1, keepdims=True)
                               - xh * jnp.mean(dxh * xh, axis=-1, keepdims=True))

    return pl.pallas_call(
        body, name=name, grid=(n // ROW_TILE,),
        out_shape=[jax.ShapeDtypeStruct((n, d), F32), jax.ShapeDtypeStruct((1, d), F32),
                   jax.ShapeDtypeStruct((1, d), F32)],
        in_specs=[_row_spec(d), _row_spec(d), _vec_spec(d), _vec_spec(d)],
        out_specs=[_row_spec(d), _vec_spec(d), _vec_spec(d)],
        compiler_params=_params(("arbitrary",)))(dc3, c1, g, b)


MERGE_TC = 512


def _merge_fwd(name, gates, a, c, b_co):
    n, d = a.shape
    nc = d // MERGE_TC

    def body(ga_ref, gc_ref, a_ref, c_ref, b_ref, m_ref):
        f32 = lambda r_: r_[...].astype(F32)
        m = _sigmoid(f32(ga_ref)) * f32(a_ref) + _sigmoid(f32(gc_ref)) * (f32(c_ref) + b_ref[...])
        m_ref[...] = m.astype(BF16)

    blk = lambda off: pl.BlockSpec((ROW_TILE, MERGE_TC), lambda i, j: (i, off + j))
    return pl.pallas_call(
        body, name=name, out_shape=jax.ShapeDtypeStruct((n, d), BF16), grid=(n // ROW_TILE, nc),
        in_specs=[blk(0), blk(nc), blk(0), blk(0), pl.BlockSpec((1, MERGE_TC), lambda i, j: (0, j))],
        out_specs=blk(0), compiler_params=_params(("parallel", "parallel")))(gates, gates, a, c, b_co)


def _merge_bwd(name, dm, gates, a, c, b_co):
    n, d = a.shape
    nc = d // MERGE_TC

    def body(dm_ref, ga_ref, gc_ref, a_ref, c_ref, b_ref, da_ref, dc_ref, dga_ref, dgc_ref, dbco_ref):
        f32 = lambda r_: r_[...].astype(F32)
        dmv = dm_ref[...]
        sa, sc = _sigmoid(f32(ga_ref)), _sigmoid(f32(gc_ref))
        dc = dmv * sc
        da_ref[...] = (dmv * sa).astype(BF16)
        dc_ref[...] = dc.astype(BF16)
        dga_ref[...] = (dmv * f32(a_ref) * sa * (1.0 - sa)).astype(BF16)
        dgc_ref[...] = (dmv * (f32(c_ref) + b_ref[...]) * sc * (1.0 - sc)).astype(BF16)

        @pl.when(pl.program_id(1) == 0)
        def _():
            dbco_ref[...] = jnp.zeros(dbco_ref.shape, F32)

        dbco_ref[...] += jnp.sum(dc, axis=0, keepdims=True)

    blk = lambda off: pl.BlockSpec((ROW_TILE, MERGE_TC), lambda j, i: (i, off + j))
    vec = pl.BlockSpec((1, MERGE_TC), lambda j, i: (0, j))
    act = jax.ShapeDtypeStruct((n, d), BF16)
    return pl.pallas_call(
        body, name=name, grid=(nc, n // ROW_TILE),
        out_shape=[act, act, act, act, jax.ShapeDtypeStruct((1, d), F32)],
        in_specs=[blk(0), blk(0), blk(nc), blk(0), blk(0), vec],
        out_specs=[blk(0), blk(0), blk(0), blk(0), vec],
        compiler_params=_params(("parallel", "arbitrary")))(dm, gates, gates, a, c, b_co)


CONV_TC = 128
CONV_HALO = 32


def _conv_chunk(t):
    return 48 if t % 48 == 0 else 32 if t % 32 == 0 else 16


def _fold8(x):
    out = x[0:8]
    for k in range(1, x.shape[0] // 8):
        out = out + x[8 * k:8 * k + 8]
    return out


def _glu_conv_fwd(name, glu, b_glu, w_dw, b_dw, batch):
    n, c2 = glu.shape
    c = c2 // 2
    t = n // batch
    nc = c // CONV_TC

    def body(a_ref, gt_ref, ba_ref, bg_ref, w_ref, bdw_ref, o_ref, pad_ref):
        u = (a_ref[...] + ba_ref[...]) * _sigmoid(gt_ref[...] + bg_ref[...])
        pad_ref[0:CONV_HALO, :] = jnp.zeros((CONV_HALO, CONV_TC), F32)
        pad_ref[CONV_HALO:CONV_HALO + t, :] = u
        ch = _conv_chunk(t)
        for r0 in range(0, t, ch):
            acc = jnp.zeros((ch, CONV_TC), F32) + bdw_ref[...]
            for j in range(CONV_W):
                off = r0 + CONV_HALO - (CONV_W - 1) + j
                acc = acc + w_ref[j:j + 1, :] * pad_ref[off:off + ch, :]
            o_ref[r0:r0 + ch, :] = acc

    seq = lambda off: pl.BlockSpec((t, CONV_TC), lambda b, j: (b, off + j))
    vec = lambda off: pl.BlockSpec((1, CONV_TC), lambda b, j: (0, off + j))
    return pl.pallas_call(
        body, name=name, out_shape=jax.ShapeDtypeStruct((n, c), F32), grid=(batch, nc),
        in_specs=[seq(0), seq(nc), vec(0), vec(nc), pl.BlockSpec((CONV_W, CONV_TC), lambda b, j: (0, j)), vec(0)],
        out_specs=seq(0), scratch_shapes=[pltpu.VMEM((t + CONV_HALO, CONV_TC), F32)],
        compiler_params=_params(("parallel", "parallel")))(glu, glu, b_glu, b_glu, w_dw, b_dw)


def _glu_conv_bwd(name, dc1, glu, b_glu, w_dw, batch):
    n, c2 = glu.shape
    c = c2 // 2
    t = n // batch
    nc = c // CONV_TC

    def body(d_ref, a_ref, gt_ref, ba_ref, bg_ref, w_ref, dga_ref, dgg_ref, dw_ref, dbdw_ref, dba_ref, dbg_ref,
             padu_ref, padd_ref):
        av = a_ref[...] + ba_ref[...]
        sg = _sigmoid(gt_ref[...] + bg_ref[...])
        dc = d_ref[...]
        padu_ref[0:CONV_HALO, :] = jnp.zeros((CONV_HALO, CONV_TC), F32)
        padu_ref[CONV_HALO:CONV_HALO + t, :] = av * sg
        padd_ref[0:t, :] = dc
        padd_ref[t:t + CONV_HALO, :] = jnp.zeros((CONV_HALO, CONV_TC), F32)

        @pl.when(pl.program_id(1) == 0)
        def _():
            dw_ref[...] = jnp.zeros(dw_ref.shape, F32)
            dbdw_ref[...] = jnp.zeros(dbdw_ref.shape, F32)
            dba_ref[...] = jnp.zeros(dba_ref.shape, F32)
            dbg_ref[...] = jnp.zeros(dbg_ref.shape, F32)

        ch = _conv_chunk(t)
        zero8 = jnp.zeros((8, CONV_TC), F32)
        dw_acc = [zero8] * CONV_W
        sum_dc, sum_a, sum_g = zero8, zero8, zero8
        for r0 in range(0, t, ch):
            dcc = d_ref[r0:r0 + ch, :]
            du = jnp.zeros((ch, CONV_TC), F32)
            for j in range(CONV_W):
                back = r0 + CONV_W - 1 - j
                du = du + w_ref[j:j + 1, :] * padd_ref[back:back + ch, :]
                off = r0 + CONV_HALO - (CONV_W - 1) + j
                dw_acc[j] = dw_acc[j] + _fold8(dcc * padu_ref[off:off + ch, :])
            sgc = _sigmoid(gt_ref[r0:r0 + ch, :] + bg_ref[...])
            dga = du * sgc
            dgg = du * padu_ref[CONV_HALO + r0:CONV_HALO + r0 + ch, :] * (1.0 - sgc)
            dga_ref[r0:r0 + ch, :] = dga.astype(BF16)
            dgg_ref[r0:r0 + ch, :] = dgg.astype(BF16)
            sum_dc, sum_a, sum_g = sum_dc + _fold8(dcc), sum_a + _fold8(dga), sum_g + _fold8(dgg)
        for j in range(CONV_W):
            dw_ref[j:j + 1, :] += jnp.sum(dw_acc[j], axis=0, keepdims=True)
        dbdw_ref[...] += jnp.sum(sum_dc, axis=0, keepdims=True)
        dba_ref[...] += jnp.sum(sum_a, axis=0, keepdims=True)
        dbg_ref[...] += jnp.sum(sum_g, axis=0, keepdims=True)

    seq = lambda off: pl.BlockSpec((t, CONV_TC), lambda j, b: (b, off + j))
    vec = lambda off: pl.BlockSpec((1, CONV_TC), lambda j, b: (0, off + j))
    wsp = pl.BlockSpec((CONV_W, CONV_TC), lambda j, b: (0, j))
    act = jax.ShapeDtypeStruct((n, c), BF16)
    v = jax.ShapeDtypeStruct((1, c), F32)
    return pl.pallas_call(
        body, name=name, grid=(nc, batch),
        out_shape=[act, act, jax.ShapeDtypeStruct((CONV_W, c), F32), v, v, v],
        in_specs=[seq(0), seq(0), seq(nc), vec(0), vec(nc), wsp],
        out_specs=[seq(0), seq(0), wsp, vec(0), vec(0), vec(0)],
        scratch_shapes=[pltpu.VMEM((t + CONV_HALO, CONV_TC), F32), pltpu.VMEM((t + CONV_HALO, CONV_TC), F32)],
        compiler_params=_params(("parallel", "arbitrary")))(dc1, glu, glu, b_glu, b_glu, w_dw)


def _split3(x):
    hi = x.astype(BF16)
    r = x - hi.astype(F32)
    mid = r.astype(BF16)
    lo = (r - mid.astype(F32)).astype(BF16)
    return hi, mid, lo


def _tri_matmul(tri, x):
    hi, mid, lo = _split3(x)
    dot = lambda v: jnp.dot(tri, v, preferred_element_type=F32)
    return dot(hi) + dot(mid) + dot(lo)


def _fox_prep_fwd(name, fg, b_fg, batch):
    n, w = fg.shape
    t = n // batch
    nq = t // ROW_TILE

    def body(fg_ref, b_ref, cum_ref):
        row = lax.broadcasted_iota(jnp.int32, (ROW_TILE, ROW_TILE), 0)
        col = lax.broadcasted_iota(jnp.int32, (ROW_TILE, ROW_TILE), 1)
        tri = (row >= col).astype(BF16)
        for k in range(nq):
            rows = slice(k * ROW_TILE, (k + 1) * ROW_TILE)
            z = fg_ref[rows, :] + b_ref[...]
            logf = jnp.minimum(z, 0.0) - jnp.log(1.0 + jnp.exp(-jnp.abs(z)))
            cum = _tri_matmul(tri, logf)
            if k > 0:
                cum = cum + cum_ref[k * ROW_TILE - 1:k * ROW_TILE, :]
            cum_ref[rows, :] = cum

    seq = pl.BlockSpec((t, w), lambda b: (b, 0))
    return pl.pallas_call(body, name=name, out_shape=jax.ShapeDtypeStruct((n, w), F32), grid=(batch,),
                          in_specs=[seq, pl.BlockSpec((1, w), lambda b: (0, 0))], out_specs=seq,
                          compiler_params=_params(("parallel",)))(fg, b_fg)


def _fox_prep_bwd(name, dcum_k, dcum_q, fg, b_fg, batch):
    n, w = fg.shape
    t = n // batch
    nq = t // ROW_TILE

    def body(dk_ref, dq_ref, fg_ref, b_ref, dfg_ref, db_ref, rev_ref):
        row = lax.broadcasted_iota(jnp.int32, (ROW_TILE, ROW_TILE), 0)
        col = lax.broadcasted_iota(jnp.int32, (ROW_TILE, ROW_TILE), 1)
        tri = (col >= row).astype(BF16)

        @pl.when(pl.program_id(0) == 0)
        def _():
            db_ref[...] = jnp.zeros(db_ref.shape, F32)

        for k in reversed(range(nq)):
            rows = slice(k * ROW_TILE, (k + 1) * ROW_TILE)
            dlog = _tri_matmul(tri, dk_ref[rows, :] + dq_ref[rows, :])
            if k < nq - 1:
                dlog = dlog + rev_ref[(k + 1) * ROW_TILE:(k + 1) * ROW_TILE + 1, :]
            rev_ref[rows, :] = dlog
            dfg = dlog * _sigmoid(-(fg_ref[rows, :] + b_ref[...]))
            dfg_ref[rows, :] = dfg.astype(BF16)
            db_ref[...] += jnp.sum(dfg, axis=0, keepdims=True)

    seq = pl.BlockSpec((t, w), lambda b: (b, 0))
    vec = pl.BlockSpec((1, w), lambda b: (0, 0))
    return pl.pallas_call(
        body, name=name, grid=(batch,),
        out_shape=[jax.ShapeDtypeStruct((n, w), BF16), jax.ShapeDtypeStruct((1, w), F32)],
        in_specs=[seq, seq, seq, vec], out_specs=[seq, vec], scratch_shapes=[pltpu.VMEM((t, w), F32)],
        compiler_params=_params(("arbitrary",)))(dcum_k, dcum_q, fg, b_fg)


def _head_masks(x):
    lane = lax.broadcasted_iota(jnp.int32, x.shape, 1)
    zero = jnp.zeros(x.shape, x.dtype)
    return jnp.where(lane < HEAD_DIM, x, zero), jnp.where(lane >= HEAD_DIM, x, zero)


def _attn_specs(t, nq):
    qkv = lambda off: pl.BlockSpec((t, LANES), lambda b, h: (b, off + h))
    cumr = pl.BlockSpec((None, None, nq, 8, ROW_TILE), lambda b, h: (b, h, 0, 0, 0))
    return qkv, cumr


def _attn_fwd(name, qkv, cumr, batch):
    n, w3 = qkv.shape
    w = w3 // 3
    t = n // batch
    nq = t // ROW_TILE
    n_pairs = w // LANES
    tq = ROW_TILE

    def body(q_ref, k_ref, v_ref, cr_ref, o_ref, lse_ref):
        row = lax.broadcasted_iota(jnp.int32, (tq, tq), 0)
        col = lax.broadcasted_iota(jnp.int32, (tq, tq), 1)
        causal = row >= col
        lane = lax.broadcasted_iota(jnp.int32, (tq, LANES), 1)
        for i in range(nq):
            rows = slice(i * tq, (i + 1) * tq)
            qs = _head_masks(q_ref[rows, :] * 0.125)
            outs, lses = [], []
            for hh in range(2):
                m = jnp.full((tq, 1), NEG, F32)
                l = jnp.zeros((tq, 1), F32)
                acc = jnp.zeros((tq, LANES), F32)
                for j in range(i + 1):
                    cols = slice(j * tq, (j + 1) * tq)
                    s = _dot_nt(qs[hh], k_ref[cols, :]) - cr_ref[j, hh:hh + 1, :]
                    if j == i:
                        s = jnp.where(causal, s, NEG)
                    m_new = jnp.maximum(m, jnp.max(s, axis=1, keepdims=True))
                    alpha = jnp.exp(m - m_new)
                    p = jnp.exp(s - m_new)
                    l = alpha * l + jnp.sum(p, axis=1, keepdims=True)
                    acc = alpha * acc + jnp.dot(p.astype(BF16), v_ref[cols, :], preferred_element_type=F32)
                    m = m_new
                outs.append(acc / l)
                lses.append(m + jnp.log(l))
            o_ref[rows, :] = jnp.where(lane < HEAD_DIM, outs[0], outs[1]).astype(BF16)
            lse_ref[rows, :] = jnp.where(lane < HEAD_DIM, lses[0], lses[1])

    qkv_spec, cumr_spec = _attn_specs(t, nq)
    return pl.pallas_call(
        body, name=name, grid=(batch, n_pairs),
        out_shape=[jax.ShapeDtypeStruct((n, w), BF16), jax.ShapeDtypeStruct((n, w), F32)],
        in_specs=[qkv_spec(0), qkv_spec(n_pairs), qkv_spec(2 * n_pairs), cumr_spec],
        out_specs=[qkv_spec(0), qkv_spec(0)],
        compiler_params=_params(("parallel", "parallel")))(qkv, qkv, qkv, cumr)


def _attn_bwd(name, qkv, o, do, lse, cumr, batch):
    n, w3 = qkv.shape
    w = w3 // 3
    t = n // batch
    nq = t // ROW_TILE
    n_pairs = w // LANES
    tq = ROW_TILE

    def body(q_ref, k_ref, v_ref, o_ref, do_ref, lse_ref, cr_ref, dq_ref, dk_ref, dv_ref, dcr_ref, dcq_ref,
             dk_acc, dv_acc):
        pair = pl.program_id(1)
        row = lax.broadcasted_iota(jnp.int32, (tq, tq), 0)
        col = lax.broadcasted_iota(jnp.int32, (tq, tq), 1)
        causal = row >= col
        lane = lax.broadcasted_iota(jnp.int32, (tq, LANES), 1)
        dk_acc[...] = jnp.zeros(dk_acc.shape, F32)
        dv_acc[...] = jnp.zeros(dv_acc.shape, F32)
        dcr_ref[...] = jnp.zeros(dcr_ref.shape, F32)

        @pl.when(pair == 0)
        def _():
            dcq_ref[...] = jnp.zeros(dcq_ref.shape, F32)

        for i in range(nq):
            rows = slice(i * tq, (i + 1) * tq)
            qs = _head_masks(q_ref[rows, :] * 0.125)
            dos = _head_masks(do_ref[rows, :])
            dq = jnp.zeros((tq, LANES), F32)
            dcq = []
            for hh in range(2):
                row_sum = jnp.zeros((tq, 1), F32)
                lse = lse_ref[rows, hh * HEAD_DIM:hh * HEAD_DIM + 1]
                delta = jnp.sum(dos[hh].astype(F32) * o_ref[rows, :].astype(F32), axis=1, keepdims=True)
                for j in range(i + 1):
                    cols = slice(j * tq, (j + 1) * tq)
                    s = _dot_nt(qs[hh], k_ref[cols, :]) - cr_ref[j, hh:hh + 1, :]
                    p = jnp.exp(s - lse)
                    if j == i:
                        p = jnp.where(causal, p, 0.0)
                    dp = _dot_nt(dos[hh], v_ref[cols, :])
                    ds = p * (dp - delta)
                    pb, dsb = p.astype(BF16), ds.astype(BF16)
                    km = _head_masks(k_ref[cols, :])[hh]
                    dv_acc[cols, :] += _dot_tn(pb, dos[hh])
                    dk_acc[cols, :] += _dot_tn(dsb, qs[hh])
                    dq = dq + jnp.dot(dsb, km, preferred_element_type=F32)
                    dcr_ref[j, hh:hh + 1, :] -= jnp.sum(ds, axis=0, keepdims=True)
                    row_sum = row_sum + jnp.sum(ds, axis=1, keepdims=True)
                dcq.append(row_sum)
            dq_ref[rows, :] = (dq * 0.125).astype(BF16)
            dcq_ref[rows, :] = jnp.where(lane == 2 * pair, dcq[0],
                                         jnp.where(lane == 2 * pair + 1, dcq[1], dcq_ref[rows, :]))
        dk_ref[...] = dk_acc[...].astype(BF16)
        dv_ref[...] = dv_acc[...].astype(BF16)

    qkv_spec, cumr_spec = _attn_specs(t, nq)
    act = jax.ShapeDtypeStruct((n, w), BF16)
    return pl.pallas_call(
        body, name=name, grid=(batch, n_pairs),
        out_shape=[act, act, act, jax.ShapeDtypeStruct(cumr.shape, F32), jax.ShapeDtypeStruct((n, LANES), F32)],
        in_specs=[qkv_spec(0), qkv_spec(n_pairs), qkv_spec(2 * n_pairs), qkv_spec(0), qkv_spec(0), qkv_spec(0),
                  cumr_spec],
        out_specs=[qkv_spec(0), qkv_spec(0), qkv_spec(0), cumr_spec, pl.BlockSpec((t, LANES), lambda b, h: (b, 0))],
        scratch_shapes=[pltpu.VMEM((t, LANES), F32), pltpu.VMEM((t, LANES), F32)],
        compiler_params=_params(("parallel", "arbitrary")))(qkv, qkv, qkv, o, do, lse, cumr)


def _adamw(name, parts, w, m, v):
    r, c = w.shape
    tr = 128 if r % 128 == 0 else r
    c1 = 1.0 - ADAM_B1 ** ADAM_STEP
    c2 = 1.0 - ADAM_B2 ** ADAM_STEP

    def body(p_ref, w_ref, m_ref, v_ref, g_ref, d_ref, m2_ref, v2_ref):
        g = p_ref[0].astype(F32)
        for s in range(1, N_DEV):
            g = g + p_ref[s].astype(F32)
        m2 = ADAM_B1 * m_ref[...] + (1.0 - ADAM_B1) * g
        v2 = ADAM_B2 * v_ref[...] + (1.0 - ADAM_B2) * (g * g)
        g_ref[...] = g
        m2_ref[...] = m2
        v2_ref[...] = v2
        d_ref[...] = -ADAM_LR * ((m2 / c1) / (jnp.sqrt(v2 / c2) + ADAM_EPS) + ADAM_WD * w_ref[...])

    blk = pl.BlockSpec((tr, c), lambda i: (i, 0))
    shp = jax.ShapeDtypeStruct((r, c), F32)
    return pl.pallas_call(
        body, name=name, out_shape=[shp] * 4, grid=(r // tr,),
        in_specs=[pl.BlockSpec((N_DEV, tr, c), lambda i: (0, i, 0)), blk, blk, blk], out_specs=[blk] * 4,
        compiler_params=_params(("parallel",)))(parts, w, m, v)


def _cat_small(vals):
    parts = []
    for name in SMALL:
        v = vals[name].reshape(1, -1).astype(F32)
        parts.append(jnp.pad(v, ((0, 0), (0, SMALL_W[name] - v.shape[1]))))
    return jnp.concatenate(parts, axis=1)


def _split_small(row, shapes):
    out, off = {}, 0
    for name in SMALL:
        out[name] = row[0, off:off + SMALL_N[name]].reshape(shapes[name])
        off += SMALL_W[name]
    return out


def _w_in_pieces(lo, hi, n_fg):
    o_fg = 3 * SEG
    out = []
    c = lo
    while c < hi:
        if c < o_fg:
            src, base, end = c // SEG, (c // SEG) * SEG, (c // SEG + 1) * SEG
        elif c < o_fg + n_fg:
            src, base, end = 7, o_fg, o_fg + n_fg
        else:
            k = (c - o_fg - n_fg) // SEG
            src, base, end = 3 + k, o_fg + n_fg + k * SEG, o_fg + n_fg + (k + 1) * SEG
        stop = min(hi, end)
        out.append((src, c - base, stop - base, c))
        c = stop
    return out


def _cols_from_shards(g):
    return jnp.transpose(g, (1, 0, 2)).reshape(g.shape[1], N_DEV * g.shape[2])


def _shards_from_cols(a):
    r, c = a.shape
    return jnp.transpose(a.reshape(r, N_DEV, c // N_DEV), (1, 0, 2))


def kernel(x, meta_tokens, norm_mix_gain, w_in, b_forget, w_attn_out, b_glu, conv_dw_w, conv_dw_b, conv_ln_gain, conv_ln_bias, w_conv_out, b_conv_out, w_out, norm_mlp_gain, w_mlp_up, w_mlp_down, final_norm_gain, loss_target, m_meta_tokens, m_norm_mix_gain, m_w_in, m_b_forget, m_w_attn_out, m_b_glu, m_conv_dw_w, m_conv_dw_b, m_conv_ln_gain, m_conv_ln_bias, m_w_conv_out, m_b_conv_out, m_w_out, m_norm_mlp_gain, m_w_mlp_up, m_w_mlp_down, m_final_norm_gain, v_meta_tokens, v_norm_mix_gain, v_w_in, v_b_forget, v_w_attn_out, v_b_glu, v_conv_dw_w, v_conv_dw_b, v_conv_ln_gain, v_conv_ln_bias, v_w_conv_out, v_b_conv_out, v_w_out, v_norm_mlp_gain, v_w_mlp_up, v_w_mlp_down, v_final_norm_gain):
    weights = dict(meta_tokens=meta_tokens, norm_mix_gain=norm_mix_gain, w_in=w_in, b_forget=b_forget, w_attn_out=w_attn_out, b_glu=b_glu, conv_dw_w=conv_dw_w, conv_dw_b=conv_dw_b, conv_ln_gain=conv_ln_gain, conv_ln_bias=conv_ln_bias, w_conv_out=w_conv_out, b_conv_out=b_conv_out, w_out=w_out, norm_mlp_gain=norm_mlp_gain, w_mlp_up=w_mlp_up, w_mlp_down=w_mlp_down, final_norm_gain=final_norm_gain)
    mom_m = dict(meta_tokens=m_meta_tokens, norm_mix_gain=m_norm_mix_gain, w_in=m_w_in, b_forget=m_b_forget, w_attn_out=m_w_attn_out, b_glu=m_b_glu, conv_dw_w=m_conv_dw_w, conv_dw_b=m_conv_dw_b, conv_ln_gain=m_conv_ln_gain, conv_ln_bias=m_conv_ln_bias, w_conv_out=m_w_conv_out, b_conv_out=m_b_conv_out, w_out=m_w_out, norm_mlp_gain=m_norm_mlp_gain, w_mlp_up=m_w_mlp_up, w_mlp_down=m_w_mlp_down, final_norm_gain=m_final_norm_gain)
    mom_v = dict(meta_tokens=v_meta_tokens, norm_mix_gain=v_norm_mix_gain, w_in=v_w_in, b_forget=v_b_forget, w_attn_out=v_w_attn_out, b_glu=v_b_glu, conv_dw_w=v_conv_dw_w, conv_dw_b=v_conv_dw_b, conv_ln_gain=v_conv_ln_gain, conv_ln_bias=v_conv_ln_bias, w_conv_out=v_w_conv_out, b_conv_out=v_b_conv_out, w_out=v_w_out, norm_mlp_gain=v_norm_mlp_gain, w_mlp_up=v_w_mlp_up, w_mlp_down=v_w_mlp_down, final_norm_gain=v_final_norm_gain)
    names = list(weights)
    batch, seq, d = x.shape
    t = seq + N_META
    n = batch * t
    nq = t // ROW_TILE
    n_pairs = d // LANES
    assert t % ROW_TILE == 0 and d == SEG

    first = [w_in[0].astype(BF16), meta_tokens, conv_dw_w[0]]
    gather_a = _exchange_start("gather_in_start", [(f_, False) for f_ in first], ks=CHIP_PEERS)
    passed = _pass_on_start("gather_in_pass_start",
                            _exchange_wait("gather_in_wait", gather_a, gather_a["token"], fill_own=False))
    w_in_g, meta_g, w_dw_g = _pass_on_wait("gather_in_pass_wait", passed, passed["token"], first)
    gather_b = _exchange_start("gather_rest_start", [
        (w_attn_out[0].astype(BF16), False), (w_conv_out[0].astype(BF16), False), (w_out[0].astype(BF16), False),
        (w_mlp_up[0].astype(BF16), False), (w_mlp_down[0].astype(BF16), False)])
    n_fg = b_forget.shape[1]
    shard_w = w_in.shape[2]
    seg_cols = [[] for _ in range(8)]
    for p in range(N_DEV):
        for src, c0, c1, orig in _w_in_pieces(p * shard_w, (p + 1) * shard_w, n_fg):
            seg_cols[src].append(w_in_g[p][:, orig - p * shard_w:orig - p * shard_w + c1 - c0])
    w_pad = jnp.concatenate([c_ for src in range(8) for c_ in seg_cols[src]]
                            + [jnp.zeros((d, FG_PAD - n_fg), BF16)], axis=1)
    d_ff = w_mlp_down.shape[1] * N_DEV
    ff_blk = d_ff // N_DEV
    meta_f = _cols_from_shards(meta_g)
    w_dw = _cols_from_shards(w_dw_g)

    row2 = lambda v: v.reshape(1, -1)
    g1, g2, g3 = row2(norm_mix_gain) + gather_b["token"][0:1, 0:1], row2(norm_mlp_gain), row2(final_norm_gain)
    b_fg = jnp.pad(b_forget, ((0, 0), (0, FG_PAD - n_fg)))
    h0 = jnp.concatenate([jnp.broadcast_to(meta_f[None], (batch, N_META, d)), x], axis=1).reshape(n, d)
    tgt = jnp.concatenate([jnp.zeros((batch, N_META, d), F32), loss_target], axis=1).reshape(n, d)

    hn1 = _rms_fwd("rms1", h0, g1)
    qkv = _mm_nn("proj_qkv", hn1, w_pad, _w_cols(d, SEG, 0), 3 * SEG, SEG, BF16)
    glu = _mm_nn("proj_glu", hn1, w_pad, _w_cols(d, SEG, 3), 2 * SEG, SEG, F32)
    gates = _mm_nn("proj_gates", hn1, w_pad, _w_cols(d, SEG, 5), 2 * SEG, SEG, BF16)
    fg = _mm_nn("proj_fg", hn1, w_pad, _w_cols(d, FG_PAD, 7 * SEG // FG_PAD), FG_PAD, FG_PAD, F32)

    cum = _fox_prep_fwd("fox_cumsum", fg, b_fg, batch)
    cum_h = cum.reshape(batch, t, FG_PAD)[:, :, :2 * n_pairs].reshape(batch, t, n_pairs, 2)
    cumr = jnp.transpose(cum_h.reshape(batch, nq, ROW_TILE, n_pairs, 2), (0, 3, 1, 4, 2))
    cumr = jnp.pad(cumr, ((0, 0), (0, 0), (0, 0), (0, 6), (0, 0)))
    o, lse = _attn_fwd("attn_fwd", qkv, cumr, batch)
    rest = _exchange_wait("gather_rest_wait", gather_b, o)
    w_ao, w_co, w_o = [r_.reshape(d, d) for r_ in rest[:3]]
    w_up = rest[3]
    w_dn = rest[4].reshape(d_ff, d)
    a = _mm_nn("attn_out", o, w_ao, _w_cols(d, d, 0), d, d, BF16)

    c1 = _glu_conv_fwd("glu_conv", glu, b_glu, w_dw, conv_dw_b, batch)
    c3 = _ln_silu_fwd("ln_silu", c1, conv_ln_gain, conv_ln_bias)
    c = _mm_nn("conv_out", c3, w_co, _w_cols(d, d, 0), d, d, BF16)

    mrg = _merge_fwd("merge", gates, a, c, b_conv_out)
    mo = _mm_nn("mix_out", mrg, w_o, _w_cols(d, d, 0), d, d, F32)
    h1, hn2 = _rms_fwd("resid_rms2", h0, g2, res=mo)
    per = ff_blk // 512
    up, act = _mm_nn("mlp_up", hn2, w_up, pl.BlockSpec((None, d, 512), lambda i, j: (j // per, 0, j % per)),
                     d_ff, 512, BF16, relu2=True)
    dn = _mm_nn("mlp_down", act, w_dn, _w_cols(d_ff, d, 0), d, d, F32, tm=ROW_TILE)
    dh2, dh2b, loss_blk, dg3 = _final("final_loss", h1, dn, tgt, g3, batch)

    dup = _mm_nt("d_mlp_down", [(dh2b, _a_rows(d), w_dn, _w_rows(d, d))], n, d_ff, d, BF16, relu_bwd_of=up)
    dw_dn = _grad_w("gw_mlp_down", act, dh2b)
    dhn2 = _mm_nt("d_mlp_up", [(dup, _a_rows(ff_blk, g), w_up, pl.BlockSpec((None, 512, ff_blk), lambda i, j, g=g: (g, j, 0)))
                               for g in range(N_DEV)], n, d, 512, F32)
    dw_up = _mm_tn("gw_mlp_up", hn2, lambda a_: 0, d, dup, lambda b_: b_, ff_blk, (N_DEV, d, ff_blk),
                   pl.BlockSpec((None, d, ff_blk), lambda a_, b_, k: (b_, 0, 0)), (1, N_DEV))
    scatter_1 = _exchange_start("scatter_mlp_start", [(dw_dn.reshape(N_DEV, ff_blk, d), True), (dw_up, True)])
    dh1, dg2, dh1b = _rms_bwd("rms2_bwd", dhn2, h1, g2 + scatter_1["token"][0:1, 0:1], dh2, batch, with_bf16=True)

    dm = _mm_nt("d_mix_out", [(dh1b, _a_rows(d), w_o, _w_rows(d, d))], n, d, d, F32)
    dw_o = _grad_w("gw_mix_out", mrg, dh1b)
    da, dc, dga, dgc, dbco = _merge_bwd("merge_bwd", dm, gates, a, c, b_conv_out)

    do = _mm_nt("d_attn_out", [(da, _a_rows(d), w_ao, _w_rows(d, d))], n, d, d, BF16)
    dw_ao = _grad_w("gw_attn_out", o, da)
    dc3 = _mm_nt("d_conv_out", [(dc, _a_rows(d), w_co, _w_rows(d, d))], n, d, d, F32)
    dw_co = _grad_w("gw_conv_out", c3, dc)

    scatter_2 = _exchange_start("scatter_mix_start", [(dw_.reshape(N_DEV, d // N_DEV, d), True)
                                                      for dw_ in (dw_o, dw_ao, dw_co)])
    dc1, dg_ln, db_ln = _ln_silu_bwd("ln_silu_bwd", dc3, c1, conv_ln_gain + scatter_2["token"][0:1, 0:1],
                                     conv_ln_bias)
    dglu_a, dglu_g, dw_dw, db_dw, dbg_a, dbg_g = _glu_conv_bwd("glu_conv_bwd", dc1, glu, b_glu, w_dw, batch)

    dq, dk, dv, dcumr, dcum_q = _attn_bwd("attn_bwd", qkv, o, do, lse, cumr, batch)
    dcum_k = jnp.transpose(dcumr[:, :, :, :2, :], (0, 2, 4, 1, 3)).reshape(n, 2 * n_pairs)
    dcum_k = jnp.pad(dcum_k, ((0, 0), (0, FG_PAD - 2 * n_pairs)))
    dfg, db_fg = _fox_prep_bwd("fox_cumsum_bwd", dcum_k, dcum_q, fg, b_fg, batch)

    segs = [dq, dk, dv, dglu_a, dglu_g, dga, dgc]
    gw_src = [_grad_w("gw_in_%d" % i, hn1, s_) for i, s_ in enumerate(segs)] + [_grad_w("gw_in_fg", hn1, dfg)]
    dw_in = jnp.stack([jnp.concatenate([gw_src[src][:, c0:c1]
                                        for src, c0, c1, _ in _w_in_pieces(p * shard_w, (p + 1) * shard_w, n_fg)], axis=1)
                       for p in range(N_DEV)])
    scatter_3 = _exchange_start("scatter_in_start", [(dw_in, True)])
    pairs = [(s_, _a_rows(SEG, 0, ROW_TILE), w_pad, _w_rows(512, SEG, i)) for i, s_ in enumerate(segs)]
    pairs.append((dfg, _a_rows(FG_PAD, 0, ROW_TILE), w_pad, _w_rows(512, FG_PAD, 7 * SEG // FG_PAD)))
    dhn1 = _mm_nt("d_proj_in", pairs, n, d, 512, F32, tm=ROW_TILE, after=scatter_3["token"])
    dh0, dg1, dmeta = _rms_bwd("rms1_bwd", dhn1, h0, g1, dh1, batch, with_meta=True)
    grad_x = dh0.reshape(batch, t, d)[:, N_META:, :]

    small_g = dict(norm_mix_gain=dg1, b_forget=db_fg[:, :n_fg], b_glu=jnp.concatenate([dbg_a, dbg_g], axis=1),
                   conv_dw_b=db_dw, conv_ln_gain=dg_ln, conv_ln_bias=db_ln, b_conv_out=dbco, norm_mlp_gain=dg2,
                   final_norm_gain=dg3)
    scatter_4 = _exchange_start("scatter_small_start", [
        (_shards_from_cols(dmeta), True), (_shards_from_cols(dw_dw), True), (_cat_small(small_g), False),
        (loss_blk[0:1, :], False)])

    grads, deltas, new_m, new_v = {}, {}, {}, {}

    def update(k, parts):
        shp = weights[k].shape
        w2 = lambda arr: arr.reshape(parts.shape[1:])
        res_ = _adamw("adamw_" + k, parts, w2(weights[k]), w2(mom_m[k]), w2(mom_v[k]))
        grads[k], deltas[k], new_m[k], new_v[k] = [r.reshape(shp) for r in res_]

    for k, parts in zip(("w_mlp_down", "w_mlp_up"), _exchange_wait("scatter_mlp_wait", scatter_1, scatter_4["token"])):
        update(k, parts)
    for k, parts in zip(("w_out", "w_attn_out", "w_conv_out"),
                        _exchange_wait("scatter_mix_wait", scatter_2, deltas["w_mlp_up"])):
        update(k, parts)
    update("w_in", _exchange_wait("scatter_in_wait", scatter_3, deltas["w_conv_out"])[0])
    reduced = _exchange_wait("scatter_small_wait", scatter_4, deltas["w_in"])
    loss = jnp.sum(reduced.pop()[:, 0, 0])
    for k, parts in zip(("meta_tokens", "conv_dw_w"), reduced[:-1]):
        update(k, parts)
    res = _adamw("adamw_small", reduced[-1], _cat_small(weights), _cat_small(mom_m), _cat_small(mom_v))
    shapes = {k: weights[k].shape for k in SMALL}
    for dst, r in zip((grads, deltas, new_m, new_v), res):
        dst.update(_split_small(r, shapes))

    return (loss, grad_x, *[grads[k] for k in names], *[deltas[k] for k in names],
            *[new_m[k] for k in names], *[new_v[k] for k in names])
```

```python
import functools

import jax
import jax.numpy as jnp
from jax import lax
from jax.experimental import pallas as pl
from jax.experimental.pallas import tpu as pltpu

F32, BF16 = jnp.float32, jnp.bfloat16
N_DEV = 8
N_META = 16
HEAD_DIM = 64
LANES = 128
CONV_W = 31
RMS_EPS = 1e-6
LN_EPS = 1e-5
ROW_TILE = 688
MM_TM = 2 * ROW_TILE
SEG = 1024
FG_PAD = 128
VMEM_LIMIT = 56 * 1024 * 1024
ADAM_LR, ADAM_B1, ADAM_B2, ADAM_EPS, ADAM_WD, ADAM_STEP = 0.001, 0.9, 0.999, 1e-08, 0.01, 10
NEG = -1e30

SMALL = ("norm_mix_gain", "b_forget", "b_glu", "conv_dw_b", "conv_ln_gain", "conv_ln_bias", "b_conv_out",
         "norm_mlp_gain", "final_norm_gain")
SMALL_W = {"norm_mix_gain": 1024, "b_forget": 128, "b_glu": 2048, "conv_dw_b": 1024, "conv_ln_gain": 1024,
           "conv_ln_bias": 1024, "b_conv_out": 1024, "norm_mlp_gain": 1024, "final_norm_gain": 1024}
SMALL_N = {"norm_mix_gain": 1024, "b_forget": 16, "b_glu": 2048, "conv_dw_b": 1024, "conv_ln_gain": 1024,
           "conv_ln_bias": 1024, "b_conv_out": 1024, "norm_mlp_gain": 1024, "final_norm_gain": 1024}


def _params(sem=None):
    return pltpu.CompilerParams(dimension_semantics=sem, vmem_limit_bytes=VMEM_LIMIT)


def _sigmoid(x):
    return 1.0 / (1.0 + jnp.exp(-x))


def _dot_nt(a, b):
    return lax.dot_general(a, b, (((1,), (1,)), ((), ())), preferred_element_type=F32)


def _dot_tn(a, b):
    return lax.dot_general(a, b, (((0,), (0,)), ((), ())), preferred_element_type=F32)


HBM_SPEC = pl.BlockSpec(memory_space=pltpu.HBM)
SEM_SPEC = pl.BlockSpec(memory_space=pltpu.SEMAPHORE)
DATAFLOW = pltpu.SideEffectType.DATAFLOW_SIDE_EFFECTING


def _device_index():
    return 4 * lax.axis_index("x") + 2 * lax.axis_index("y") + lax.axis_index("c")


def _peers():
    x, y, c = lax.axis_index("x"), lax.axis_index("y"), lax.axis_index("c")
    out = []
    for k in range(1, N_DEV):
        px = 1 - x if k & 4 else x
        py = 1 - y if k & 2 else y
        pc = 1 - c if k & 1 else c
        out.append((k, (px, py, pc), 4 * px + 2 * py + pc))
    return out


def _peer_copy(per_dest, src_ref, land_ref, send_sems, recv_sems, a, k, dev, peer):
    src = src_ref.at[peer] if per_dest else src_ref
    return pltpu.make_async_remote_copy(
        src_ref=src, dst_ref=land_ref.at[_device_index()], send_sem=send_sems.at[a * (N_DEV - 1) + k - 1],
        recv_sem=recv_sems.at[a * (N_DEV - 1) + k - 1], device_id=dev, device_id_type=pl.DeviceIdType.MESH)


ALL_PEERS = tuple(range(1, N_DEV))
CHIP_PEERS = (1, 2, 4, 6)
FAR_PEERS = (2, 4, 6)


def _exchange_start(name, items, ks=ALL_PEERS):
    n = len(items)
    per_dest = [it[1] for it in items]

    def body(*refs):
        srcs, lands = refs[:n], refs[n:2 * n]
        send_sems, recv_sems, token = refs[2 * n], refs[2 * n + 1], refs[-1]
        for a in range(n):
            for k, dev, peer in _peers():
                if k in ks:
                    _peer_copy(per_dest[a], srcs[a], lands[a], send_sems, recv_sems, a, k, dev, peer).start()
        token[...] = jnp.zeros(token.shape, F32)

    srcs = [pltpu.with_memory_space_constraint(it[0], pltpu.HBM) for it in items]
    lands = []
    for arr, pd in items:
        shp = arr.shape if pd else (N_DEV,) + arr.shape
        lands.append(pltpu.with_memory_space_constraint(lax.empty(shp, arr.dtype), pltpu.HBM))
    sems = pltpu.SemaphoreType.DMA((n * (N_DEV - 1),))
    res = pl.pallas_call(
        body, name=name,
        out_shape=(sems, sems, *[pltpu.HBM(a_.shape, a_.dtype) for a_ in srcs + lands],
                   jax.ShapeDtypeStruct((8, 128), F32)),
        in_specs=[HBM_SPEC] * (2 * n),
        out_specs=(SEM_SPEC, SEM_SPEC, *[HBM_SPEC] * (2 * n), pl.BlockSpec(memory_space=pltpu.VMEM)),
        input_output_aliases={i: 2 + i for i in range(2 * n)},
        compiler_params=pltpu.CompilerParams(has_side_effects=DATAFLOW),
    )(*srcs, *lands)
    return dict(per_dest=per_dest, ks=ks, send=res[0], recv=res[1], srcs=list(res[2:2 + n]),
                lands=list(res[2 + n:2 + 2 * n]), token=res[-1])


def _fill_own(per_dest, srcs, lands):
    me = _device_index()
    out = []
    for pd, src, land in zip(per_dest, srcs, lands):
        own = lax.dynamic_index_in_dim(src, me, 0, keepdims=True) if pd else src[None]
        out.append(lax.dynamic_update_slice_in_dim(land, own, me, axis=0))
    return out


def _exchange_wait(name, started, after, fill_own=True):
    per_dest = started["per_dest"]
    n = len(per_dest)

    def body(*refs):
        srcs, lands = refs[:n], refs[n:2 * n]
        send_sems, recv_sems = refs[2 * n], refs[2 * n + 1]
        for a in range(n):
            for k, dev, peer in _peers():
                if k in started["ks"]:
                    cp = _peer_copy(per_dest[a], srcs[a], lands[a], send_sems, recv_sems, a, k, dev, peer)
                    cp.wait_send()
                    cp.wait_recv()

    bufs = started["srcs"] + started["lands"]
    res = pl.pallas_call(
        body, name=name, out_shape=tuple(pltpu.HBM(b_.shape, b_.dtype) for b_ in bufs),
        in_specs=[HBM_SPEC] * (2 * n) + [SEM_SPEC, SEM_SPEC, pl.BlockSpec(memory_space=pl.ANY)],
        out_specs=tuple([HBM_SPEC] * (2 * n)), input_output_aliases={i: i for i in range(2 * n)},
        compiler_params=pltpu.CompilerParams(has_side_effects=DATAFLOW),
    )(*bufs, started["send"], started["recv"], after)
    return _fill_own(per_dest, res[:n], res[n:]) if fill_own else list(res[n:])


def _pass_on_copy(land_ref, send_sems, recv_sems, a, idx, slot):
    sibling = (lax.axis_index("x"), lax.axis_index("y"), 1 - lax.axis_index("c"))
    return pltpu.make_async_remote_copy(
        src_ref=land_ref.at[slot], dst_ref=land_ref.at[slot], send_sem=send_sems.at[a * len(FAR_PEERS) + idx],
        recv_sem=recv_sems.at[a * len(FAR_PEERS) + idx], device_id=sibling, device_id_type=pl.DeviceIdType.MESH)


def _pass_on_start(name, lands):
    n = len(lands)

    def body(*refs):
        send_sems, recv_sems, token = refs[n], refs[n + 1], refs[-1]
        slots = {k: peer for k, _, peer in _peers()}
        for a in range(n):
            for idx, k in enumerate(FAR_PEERS):
                _pass_on_copy(refs[a], send_sems, recv_sems, a, idx, slots[k]).start()
        token[...] = jnp.zeros(token.shape, F32)

    lands = [pltpu.with_memory_space_constraint(l_, pltpu.HBM) for l_ in lands]
    sems = pltpu.SemaphoreType.DMA((n * len(FAR_PEERS),))
    res = pl.pallas_call(
        body, name=name,
        out_shape=(sems, sems, *[pltpu.HBM(l_.shape, l_.dtype) for l_ in lands], jax.ShapeDtypeStruct((8, 128), F32)),
        in_specs=[HBM_SPEC] * n, out_specs=(SEM_SPEC, SEM_SPEC, *[HBM_SPEC] * n, pl.BlockSpec(memory_space=pltpu.VMEM)),
        input_output_aliases={i: 2 + i for i in range(n)},
        compiler_params=pltpu.CompilerParams(has_side_effects=DATAFLOW),
    )(*lands)
    return dict(send=res[0], recv=res[1], lands=list(res[2:2 + n]), token=res[-1])


def _pass_on_wait(name, passed, after, owns):
    n = len(passed["lands"])

    def body(*refs):
        send_sems, recv_sems = refs[n], refs[n + 1]
        slots = {k: peer for k, _, peer in _peers()}
        for a in range(n):
            for idx, k in enumerate(FAR_PEERS):
                _pass_on_copy(refs[a], send_sems, recv_sems, a, idx, slots[k]).wait_send()
                _pass_on_copy(refs[a], send_sems, recv_sems, a, idx, slots[k ^ 1]).wait_recv()

    res = pl.pallas_call(
        body, name=name, out_shape=tuple(pltpu.HBM(l_.shape, l_.dtype) for l_ in passed["lands"]),
        in_specs=[HBM_SPEC] * n + [SEM_SPEC, SEM_SPEC, pl.BlockSpec(memory_space=pl.ANY)],
        out_specs=tuple([HBM_SPEC] * n), input_output_aliases={i: i for i in range(n)},
        compiler_params=pltpu.CompilerParams(has_side_effects=DATAFLOW),
    )(*passed["lands"], passed["send"], passed["recv"], after)
    return _fill_own([False] * n, owns, res)


def _mm_nn(name, x, w, w_spec, n_out, tn, out_dtype, relu2=False, tm=MM_TM):
    m, k = x.shape

    def body(x_ref, w_ref, *outs):
        acc = jnp.dot(x_ref[...], w_ref[...], preferred_element_type=F32)
        outs[0][...] = acc.astype(outs[0].dtype)
        if relu2:
            r = jnp.maximum(acc, 0.0)
            outs[1][...] = (r * r).astype(outs[1].dtype)

    o_spec = pl.BlockSpec((tm, tn), lambda i, j: (i, j))
    shapes = [jax.ShapeDtypeStruct((m, n_out), out_dtype)]
    if relu2:
        shapes.append(jax.ShapeDtypeStruct((m, n_out), BF16))
    res = pl.pallas_call(
        body, name=name, out_shape=shapes, grid=(m // tm, n_out // tn),
        in_specs=[pl.BlockSpec((tm, k), lambda i, j: (i, 0)), w_spec],
        out_specs=[o_spec] * len(shapes), compiler_params=_params(("parallel", "parallel")),
    )(x, w)
    return res if relu2 else res[0]


def _mm_nt(name, pairs, m, n_out, tn, out_dtype, relu_bwd_of=None, tm=MM_TM, after=None):
    np_ = len(pairs)

    def body(*refs):
        acc = None
        for p in range(np_):
            d = _dot_nt(refs[2 * p][...], refs[2 * p + 1][...])
            acc = d if acc is None else acc + d
        if relu_bwd_of is not None:
            acc = acc * (2.0 * jnp.maximum(refs[2 * np_][...].astype(F32), 0.0))
        refs[-1][...] = acc.astype(refs[-1].dtype)

    o_spec = pl.BlockSpec((tm, tn), lambda i, j: (i, j))
    operands, specs = [], []
    for a, a_spec, w, w_spec in pairs:
        operands += [a, w]
        specs += [a_spec, w_spec]
    if relu_bwd_of is not None:
        operands.append(relu_bwd_of)
        specs.append(o_spec)
    if after is not None:
        operands.append(after)
        specs.append(pl.BlockSpec((8, 128), lambda i, j: (0, 0)))
    return pl.pallas_call(
        body, name=name, out_shape=jax.ShapeDtypeStruct((m, n_out), out_dtype), grid=(m // tm, n_out // tn),
        in_specs=specs, out_specs=o_spec, compiler_params=_params(("parallel", "parallel")),
    )(*operands)


def _mm_tn(name, x, x_col, ta, dy, dy_col, tb, out_shape, out_spec, grid_ab):
    m = x.shape[0]

    def body(x_ref, dy_ref, o_ref):
        o_ref[...] = _dot_tn(x_ref[...], dy_ref[...]).astype(BF16)

    return pl.pallas_call(
        body, name=name, out_shape=jax.ShapeDtypeStruct(out_shape, BF16), grid=grid_ab,
        in_specs=[pl.BlockSpec((m, ta), lambda a, b: (0, x_col(a))),
                  pl.BlockSpec((m, tb), lambda a, b: (0, dy_col(b)))],
        out_specs=out_spec, compiler_params=_params(("parallel", "parallel")),
    )(x, dy)


def _w_cols(k, tn, off_blocks):
    return pl.BlockSpec((k, tn), lambda i, j: (0, off_blocks + j))


def _a_rows(kw, col_block=0, tm=MM_TM):
    return pl.BlockSpec((tm, kw), lambda i, j: (i, col_block))


def _w_rows(tn, kw, col_block=0):
    return pl.BlockSpec((tn, kw), lambda i, j: (j, col_block))


def _grad_w(name, x, dy):
    nb = dy.shape[1]
    tb = min(nb, 512)
    return _mm_tn(name, x, lambda a: a, 1024, dy, lambda b: b, tb, (x.shape[1], nb),
                  pl.BlockSpec((1024, tb), lambda a, b: (a, b)), (x.shape[1] // 1024, nb // tb))


def _row_spec(width):
    return pl.BlockSpec((ROW_TILE, width), lambda i: (i, 0))


def _vec_spec(width):
    return pl.BlockSpec((1, width), lambda i: (0, 0))


def _rms_fwd(name, h, g, res=None):
    n, d = h.shape

    def body(*refs):
        if res is None:
            h_ref, g_ref, hn_ref = refs
            hv = h_ref[...]
        else:
            h_ref, r_ref, g_ref, hs_ref, hn_ref = refs
            hv = h_ref[...] + r_ref[...]
            hs_ref[...] = hv
        r = lax.rsqrt(jnp.mean(hv * hv, axis=-1, keepdims=True) + RMS_EPS)
        hn_ref[...] = (hv * r * g_ref[...]).astype(BF16)

    ins = [h, g] if res is None else [h, res, g]
    in_specs = [_row_spec(d), _vec_spec(d)] if res is None else [_row_spec(d), _row_spec(d), _vec_spec(d)]
    hn_shape = jax.ShapeDtypeStruct((n, d), BF16)
    if res is None:
        out_shape, out_specs = hn_shape, _row_spec(d)
    else:
        out_shape, out_specs = [jax.ShapeDtypeStruct((n, d), F32), hn_shape], [_row_spec(d), _row_spec(d)]
    return pl.pallas_call(body, name=name, out_shape=out_shape, grid=(n // ROW_TILE,), in_specs=in_specs,
                          out_specs=out_specs, compiler_params=_params(("parallel",)))(*ins)


def _rms_bwd(name, dhn, h, g, dres, batch, with_bf16=False, with_meta=False):
    n, d = h.shape
    t = n // batch
    nt = t // ROW_TILE

    def body(dhn_ref, h_ref, g_ref, dres_ref, *outs):
        first = (pl.program_id(0) == 0) & (pl.program_id(1) == 0)
        hv = h_ref[...]
        r = lax.rsqrt(jnp.mean(hv * hv, axis=-1, keepdims=True) + RMS_EPS)
        nrm = hv * r
        dn = dhn_ref[...] * g_ref[...]
        dh = dres_ref[...] + r * (dn - nrm * jnp.mean(dn * nrm, axis=-1, keepdims=True))
        outs[0][...] = dh
        dg_ref = outs[1]

        @pl.when(first)
        def _():
            dg_ref[...] = jnp.zeros(dg_ref.shape, F32)

        dg_ref[...] += jnp.sum(dhn_ref[...] * nrm, axis=0, keepdims=True)
        nxt = 2
        if with_bf16:
            outs[nxt][...] = dh.astype(BF16)
            nxt += 1
        if with_meta:
            meta_ref = outs[nxt]

            @pl.when(first)
            def _():
                meta_ref[...] = jnp.zeros(meta_ref.shape, F32)

            @pl.when(pl.program_id(1) == 0)
            def _():
                meta_ref[...] += dh[0:N_META, :]

    row = pl.BlockSpec((ROW_TILE, d), lambda b, j: (b * nt + j, 0))
    vec = pl.BlockSpec((1, d), lambda b, j: (0, 0))
    shapes = [jax.ShapeDtypeStruct((n, d), F32), jax.ShapeDtypeStruct((1, d), F32)]
    specs = [row, vec]
    if with_bf16:
        shapes.append(jax.ShapeDtypeStruct((n, d), BF16))
        specs.append(row)
    if with_meta:
        shapes.append(jax.ShapeDtypeStruct((N_META, d), F32))
        specs.append(pl.BlockSpec((N_META, d), lambda b, j: (0, 0)))
    return pl.pallas_call(body, name=name, out_shape=shapes, grid=(batch, nt), in_specs=[row, row, vec, row],
                          out_specs=specs, compiler_params=_params(("arbitrary", "arbitrary")))(dhn, h, g, dres)


def _final(name, h1, dn, tgt, g, batch):
    n, d = h1.shape
    t = n // batch
    nt = t // ROW_TILE

    def body(h1_ref, dn_ref, tgt_ref, g_ref, dh_ref, dhb_ref, loss_ref, dg_ref):
        first = (pl.program_id(0) == 0) & (pl.program_id(1) == 0)
        hv = h1_ref[...] + dn_ref[...]
        r = lax.rsqrt(jnp.mean(hv * hv, axis=-1, keepdims=True) + RMS_EPS)
        nrm = hv * r
        gv = g_ref[...]
        pos = pl.program_id(1) * ROW_TILE + lax.broadcasted_iota(jnp.int32, (ROW_TILE, 1), 0)
        diff = jnp.where(pos >= N_META, nrm * gv - tgt_ref[...], 0.0)
        dy = diff * (1.0 / d)

        @pl.when(first)
        def _():
            loss_ref[...] = jnp.zeros(loss_ref.shape, F32)
            dg_ref[...] = jnp.zeros(dg_ref.shape, F32)

        loss_ref[...] += jnp.full(loss_ref.shape, 0.5 / d, F32) * jnp.sum(diff * diff)
        dg_ref[...] += jnp.sum(dy * nrm, axis=0, keepdims=True)
        dng = dy * gv
        dh = r * (dng - nrm * jnp.mean(dng * nrm, axis=-1, keepdims=True))
        dh_ref[...] = dh
        dhb_ref[...] = dh.astype(BF16)

    row = pl.BlockSpec((ROW_TILE, d), lambda b, j: (b * nt + j, 0))
    vec = pl.BlockSpec((1, d), lambda b, j: (0, 0))
    return pl.pallas_call(
        body, name=name, grid=(batch, nt), in_specs=[row, row, row, vec],
        out_shape=[jax.ShapeDtypeStruct((n, d), F32), jax.ShapeDtypeStruct((n, d), BF16),
                   jax.ShapeDtypeStruct((8, 128), F32), jax.ShapeDtypeStruct((1, d), F32)],
        out_specs=[row, row, pl.BlockSpec((8, 128), lambda b, j: (0, 0)), vec],
        compiler_params=_params(("arbitrary", "arbitrary")))(h1, dn, tgt, g)


def _ln_silu_fwd(name, c1, g, b):
    n, d = c1.shape

    def body(c_ref, g_ref, b_ref, o_ref):
        xv = c_ref[...]
        xc = xv - jnp.mean(xv, axis=-1, keepdims=True)
        rstd = lax.rsqrt(jnp.mean(xc * xc, axis=-1, keepdims=True) + LN_EPS)
        c2 = xc * rstd * g_ref[...] + b_ref[...]
        o_ref[...] = (c2 * _sigmoid(c2)).astype(BF16)

    return pl.pallas_call(body, name=name, out_shape=jax.ShapeDtypeStruct((n, d), BF16), grid=(n // ROW_TILE,),
                          in_specs=[_row_spec(d), _vec_spec(d), _vec_spec(d)], out_specs=_row_spec(d),
                          compiler_params=_params(("parallel",)))(c1, g, b)


def _ln_silu_bwd(name, dc3, c1, g, b):
    n, d = c1.shape

    def body(d_ref, c_ref, g_ref, b_ref, dc1_ref, dg_ref, db_ref):
        xv = c_ref[...]
        xc = xv - jnp.mean(xv, axis=-1, keepdims=True)
        rstd = lax.rsqrt(jnp.mean(xc * xc, axis=-1, keepdims=True) + LN_EPS)
        xh = xc * rstd
        c2 = xh * g_ref[...] + b_ref[...]
        s = _sigmoid(c2)
        dc2 = d_ref[...] * (s * (1.0 + c2 * (1.0 - s)))

        @pl.when(pl.program_id(0) == 0)
        def _():
            dg_ref[...] = jnp.zeros(dg_ref.shape, F32)
            db_ref[...] = jnp.zeros(db_ref.shape, F32)

        dg_ref[...] += jnp.sum(dc2 * xh, axis=0, keepdims=True)
        db_ref[...] += jnp.sum(dc2, axis=0, keepdims=True)
        dxh = dc2 * g_ref[...]
        dc1_ref[...] = rstd * (dxh - jnp.mean(dxh, axis=-1, keepdims=True)
                               - xh * jnp.mean(dxh * xh, axis=-1, keepdims=True))

    return pl.pallas_call(
        body, name=name, grid=(n // ROW_TILE,),
        out_shape=[jax.ShapeDtypeStruct((n, d), F32), jax.ShapeDtypeStruct((1, d), F32),
                   jax.ShapeDtypeStruct((1, d), F32)],
        in_specs=[_row_spec(d), _row_spec(d), _vec_spec(d), _vec_spec(d)],
        out_specs=[_row_spec(d), _vec_spec(d), _vec_spec(d)],
        compiler_params=_params(("arbitrary",)))(dc3, c1, g, b)


MERGE_TC = 512


def _merge_fwd(name, gates, a, c, b_co):
    n, d = a.shape
    nc = d // MERGE_TC

    def body(ga_ref, gc_ref, a_ref, c_ref, b_ref, m_ref):
        f32 = lambda r_: r_[...].astype(F32)
        m = _sigmoid(f32(ga_ref)) * f32(a_ref) + _sigmoid(f32(gc_ref)) * (f32(c_ref) + b_ref[...])
        m_ref[...] = m.astype(BF16)

    blk = lambda off: pl.BlockSpec((ROW_TILE, MERGE_TC), lambda i, j: (i, off + j))
    return pl.pallas_call(
        body, name=name, out_shape=jax.ShapeDtypeStruct((n, d), BF16), grid=(n // ROW_TILE, nc),
        in_specs=[blk(0), blk(nc), blk(0), blk(0), pl.BlockSpec((1, MERGE_TC), lambda i, j: (0, j))],
        out_specs=blk(0), compiler_params=_params(("parallel", "parallel")))(gates, gates, a, c, b_co)


def _merge_bwd(name, dm, gates, a, c, b_co):
    n, d = a.shape
    nc = d // MERGE_TC

    def body(dm_ref, ga_ref, gc_ref, a_ref, c_ref, b_ref, da_ref, dc_ref, dga_ref, dgc_ref, dbco_ref):
        f32 = lambda r_: r_[...].astype(F32)
        dmv = dm_ref[...]
        sa, sc = _sigmoid(f32(ga_ref)), _sigmoid(f32(gc_ref))
        dc = dmv * sc
        da_ref[...] = (dmv * sa).astype(BF16)
        dc_ref[...] = dc.astype(BF16)
        dga_ref[...] = (dmv * f32(a_ref) * sa * (1.0 - sa)).astype(BF16)
        dgc_ref[...] = (dmv * (f32(c_ref) + b_ref[...]) * sc * (1.0 - sc)).astype(BF16)

        @pl.when(pl.program_id(1) == 0)
        def _():
            dbco_ref[...] = jnp.zeros(dbco_ref.shape, F32)

        dbco_ref[...] += jnp.sum(dc, axis=0, keepdims=True)

    blk = lambda off: pl.BlockSpec((ROW_TILE, MERGE_TC), lambda j, i: (i, off + j))
    vec = pl.BlockSpec((1, MERGE_TC), lambda j, i: (0, j))
    act = jax.ShapeDtypeStruct((n, d), BF16)
    return pl.pallas_call(
        body, name=name, grid=(nc, n // ROW_TILE),
        out_shape=[act, act, act, act, jax.ShapeDtypeStruct((1, d), F32)],
        in_specs=[blk(0), blk(0), blk(nc), blk(0), blk(0), vec],
        out_specs=[blk(0), blk(0), blk(0), blk(0), vec],
        compiler_params=_params(("parallel", "arbitrary")))(dm, gates, gates, a, c, b_co)


CONV_TC = 128
CONV_HALO = 32


def _conv_chunk(t):
    return 48 if t % 48 == 0 else 32 if t % 32 == 0 else 16


def _fold8(x):
    out = x[0:8]
    for k in range(1, x.shape[0] // 8):
        out = out + x[8 * k:8 * k + 8]
    return out


def _glu_conv_fwd(name, glu, b_glu, w_dw, b_dw, batch):
    n, c2 = glu.shape
    c = c2 // 2
    t = n // batch
    nc = c // CONV_TC

    def body(a_ref, gt_ref, ba_ref, bg_ref, w_ref, bdw_ref, o_ref, pad_ref):
        u = (a_ref[...].astype(F32) + ba_ref[...]) * _sigmoid(gt_ref[...].astype(F32) + bg_ref[...])
        pad_ref[0:CONV_HALO, :] = jnp.zeros((CONV_HALO, CONV_TC), F32)
        pad_ref[CONV_HALO:CONV_HALO + t, :] = u
        ch = _conv_chunk(t)
        for r0 in range(0, t, ch):
            acc = jnp.zeros((ch, CONV_TC), F32) + bdw_ref[...]
            for j in range(CONV_W):
                off = r0 + CONV_HALO - (CONV_W - 1) + j
                acc = acc + w_ref[j:j + 1, :] * pad_ref[off:off + ch, :]
            o_ref[r0:r0 + ch, :] = acc

    seq = lambda off: pl.BlockSpec((t, CONV_TC), lambda b, j: (b, off + j))
    vec = lambda off: pl.BlockSpec((1, CONV_TC), lambda b, j: (0, off + j))
    return pl.pallas_call(
        body, name=name, out_shape=jax.ShapeDtypeStruct((n, c), F32), grid=(batch, nc),
        in_specs=[seq(0), seq(nc), vec(0), vec(nc), pl.BlockSpec((CONV_W, CONV_TC), lambda b, j: (0, j)), vec(0)],
        out_specs=seq(0), scratch_shapes=[pltpu.VMEM((t + CONV_HALO, CONV_TC), F32)],
        compiler_params=_params(("parallel", "parallel")))(glu, glu, b_glu, b_glu, w_dw, b_dw)


def _glu_conv_bwd(name, dc1, glu, b_glu, w_dw, batch):
    n, c2 = glu.shape
    c = c2 // 2
    t = n // batch
    nc = c // CONV_TC

    def body(d_ref, a_ref, gt_ref, ba_ref, bg_ref, w_ref, dga_ref, dgg_ref, dw_ref, dbdw_ref, dba_ref, dbg_ref,
             padu_ref, padd_ref):
        av = a_ref[...].astype(F32) + ba_ref[...]
        sg = _sigmoid(gt_ref[...].astype(F32) + bg_ref[...])
        dc = d_ref[...]
        padu_ref[0:CONV_HALO, :] = jnp.zeros((CONV_HALO, CONV_TC), F32)
        padu_ref[CONV_HALO:CONV_HALO + t, :] = av * sg
        padd_ref[0:t, :] = dc
        padd_ref[t:t + CONV_HALO, :] = jnp.zeros((CONV_HALO, CONV_TC), F32)

        @pl.when(pl.program_id(1) == 0)
        def _():
            dw_ref[...] = jnp.zeros(dw_ref.shape, F32)
            dbdw_ref[...] = jnp.zeros(dbdw_ref.shape, F32)
            dba_ref[...] = jnp.zeros(dba_ref.shape, F32)
            dbg_ref[...] = jnp.zeros(dbg_ref.shape, F32)

        ch = _conv_chunk(t)
        zero8 = jnp.zeros((8, CONV_TC), F32)
        dw_acc = [zero8] * CONV_W
        sum_dc, sum_a, sum_g = zero8, zero8, zero8
        for r0 in range(0, t, ch):
            dcc = d_ref[r0:r0 + ch, :]
            du = jnp.zeros((ch, CONV_TC), F32)
            for j in range(CONV_W):
                back = r0 + CONV_W - 1 - j
                du = du + w_ref[j:j + 1, :] * padd_ref[back:back + ch, :]
                off = r0 + CONV_HALO - (CONV_W - 1) + j
                dw_acc[j] = dw_acc[j] + _fold8(dcc * padu_ref[off:off + ch, :])
            sgc = _sigmoid(gt_ref[r0:r0 + ch, :].astype(F32) + bg_ref[...])
            dga = du * sgc
            dgg = du * padu_ref[CONV_HALO + r0:CONV_HALO + r0 + ch, :] * (1.0 - sgc)
            dga_ref[r0:r0 + ch, :] = dga.astype(BF16)
            dgg_ref[r0:r0 + ch, :] = dgg.astype(BF16)
            sum_dc, sum_a, sum_g = sum_dc + _fold8(dcc), sum_a + _fold8(dga), sum_g + _fold8(dgg)
        for j in range(CONV_W):
            dw_ref[j:j + 1, :] += jnp.sum(dw_acc[j], axis=0, keepdims=True)
        dbdw_ref[...] += jnp.sum(sum_dc, axis=0, keepdims=True)
        dba_ref[...] += jnp.sum(sum_a, axis=0, keepdims=True)
        dbg_ref[...] += jnp.sum(sum_g, axis=0, keepdims=True)

    seq = lambda off: pl.BlockSpec((t, CONV_TC), lambda j, b: (b, off + j))
    vec = lambda off: pl.BlockSpec((1, CONV_TC), lambda j, b: (0, off + j))
    wsp = pl.BlockSpec((CONV_W, CONV_TC), lambda j, b: (0, j))
    act = jax.ShapeDtypeStruct((n, c), BF16)
    v = jax.ShapeDtypeStruct((1, c), F32)
    return pl.pallas_call(
        body, name=name, grid=(nc, batch),
        out_shape=[act, act, jax.ShapeDtypeStruct((CONV_W, c), F32), v, v, v],
        in_specs=[seq(0), seq(0), seq(nc), vec(0), vec(nc), wsp],
        out_specs=[seq(0), seq(0), wsp, vec(0), vec(0), vec(0)],
        scratch_shapes=[pltpu.VMEM((t + CONV_HALO, CONV_TC), F32), pltpu.VMEM((t + CONV_HALO, CONV_TC), F32)],
        compiler_params=_params(("parallel", "arbitrary")))(dc1, glu, glu, b_glu, b_glu, w_dw)


def _split3(x):
    hi = x.astype(BF16)
    r = x - hi.astype(F32)
    mid = r.astype(BF16)
    lo = (r - mid.astype(F32)).astype(BF16)
    return hi, mid, lo


def _tri_matmul(tri, x):
    hi, mid, lo = _split3(x)
    dot = lambda v: jnp.dot(tri, v, preferred_element_type=F32)
    return dot(hi) + dot(mid) + dot(lo)


def _fox_prep_fwd(name, fg, b_fg, batch):
    n, w = fg.shape
    t = n // batch
    nq = t // ROW_TILE

    def body(fg_ref, b_ref, cum_ref):
        row = lax.broadcasted_iota(jnp.int32, (ROW_TILE, ROW_TILE), 0)
        col = lax.broadcasted_iota(jnp.int32, (ROW_TILE, ROW_TILE), 1)
        tri = (row >= col).astype(BF16)
        for k in range(nq):
            rows = slice(k * ROW_TILE, (k + 1) * ROW_TILE)
            z = fg_ref[rows, :] + b_ref[...]
            logf = jnp.minimum(z, 0.0) - jnp.log(1.0 + jnp.exp(-jnp.abs(z)))
            cum = _tri_matmul(tri, logf)
            if k > 0:
                cum = cum + cum_ref[k * ROW_TILE - 1:k * ROW_TILE, :]
            cum_ref[rows, :] = cum

    seq = pl.BlockSpec((t, w), lambda b: (b, 0))
    return pl.pallas_call(body, name=name, out_shape=jax.ShapeDtypeStruct((n, w), F32), grid=(batch,),
                          in_specs=[seq, pl.BlockSpec((1, w), lambda b: (0, 0))], out_specs=seq,
                          compiler_params=_params(("parallel",)))(fg, b_fg)


def _fox_prep_bwd(name, dcum_k, dcum_q, fg, b_fg, batch):
    n, w = fg.shape
    t = n // batch
    nq = t // ROW_TILE

    def body(dk_ref, dq_ref, fg_ref, b_ref, dfg_ref, db_ref, rev_ref):
        row = lax.broadcasted_iota(jnp.int32, (ROW_TILE, ROW_TILE), 0)
        col = lax.broadcasted_iota(jnp.int32, (ROW_TILE, ROW_TILE), 1)
        tri = (col >= row).astype(BF16)

        @pl.when(pl.program_id(0) == 0)
        def _():
            db_ref[...] = jnp.zeros(db_ref.shape, F32)

        for k in reversed(range(nq)):
            rows = slice(k * ROW_TILE, (k + 1) * ROW_TILE)
            dlog = _tri_matmul(tri, dk_ref[rows, :] + dq_ref[rows, :])
            if k < nq - 1:
                dlog = dlog + rev_ref[(k + 1) * ROW_TILE:(k + 1) * ROW_TILE + 1, :]
            rev_ref[rows, :] = dlog
            dfg = dlog * _sigmoid(-(fg_ref[rows, :] + b_ref[...]))
            dfg_ref[rows, :] = dfg.astype(BF16)
            db_ref[...] += jnp.sum(dfg, axis=0, keepdims=True)

    seq = pl.BlockSpec((t, w), lambda b: (b, 0))
    vec = pl.BlockSpec((1, w), lambda b: (0, 0))
    return pl.pallas_call(
        body, name=name, grid=(batch,),
        out_shape=[jax.ShapeDtypeStruct((n, w), BF16), jax.ShapeDtypeStruct((1, w), F32)],
        in_specs=[seq, seq, seq, vec], out_specs=[seq, vec], scratch_shapes=[pltpu.VMEM((t, w), F32)],
        compiler_params=_params(("arbitrary",)))(dcum_k, dcum_q, fg, b_fg)


def _head_masks(x):
    lane = lax.broadcasted_iota(jnp.int32, x.shape, 1)
    zero = jnp.zeros(x.shape, x.dtype)
    return jnp.where(lane < HEAD_DIM, x, zero), jnp.where(lane >= HEAD_DIM, x, zero)


def _attn_specs(t, nq):
    qkv = lambda off: pl.BlockSpec((t, LANES), lambda b, h: (b, off + h))
    cumr = pl.BlockSpec((None, None, nq, 8, ROW_TILE), lambda b, h: (b, h, 0, 0, 0))
    return qkv, cumr


def _attn_fwd(name, qkv, cumr, batch):
    n, w3 = qkv.shape
    w = w3 // 3
    t = n // batch
    nq = t // ROW_TILE
    n_pairs = w // LANES
    tq = ROW_TILE

    def body(q_ref, k_ref, v_ref, cr_ref, o_ref, lse_ref):
        row = lax.broadcasted_iota(jnp.int32, (tq, tq), 0)
        col = lax.broadcasted_iota(jnp.int32, (tq, tq), 1)
        causal = row >= col
        lane = lax.broadcasted_iota(jnp.int32, (tq, LANES), 1)
        for i in range(nq):
            rows = slice(i * tq, (i + 1) * tq)
            qs = _head_masks(q_ref[rows, :] * 0.125)
            outs, lses = [], []
            for hh in range(2):
                m = jnp.full((tq, 1), NEG, F32)
                l = jnp.zeros((tq, 1), F32)
                acc = jnp.zeros((tq, LANES), F32)
                for j in range(i + 1):
                    cols = slice(j * tq, (j + 1) * tq)
                    s = _dot_nt(qs[hh], k_ref[cols, :]) - cr_ref[j, hh:hh + 1, :]
                    if j == i:
                        s = jnp.where(causal, s, NEG)
                    m_new = jnp.maximum(m, jnp.max(s, axis=1, keepdims=True))
                    alpha = jnp.exp(m - m_new)
                    p = jnp.exp(s - m_new)
                    l = alpha * l + jnp.sum(p, axis=1, keepdims=True)
                    acc = alpha * acc + jnp.dot(p.astype(BF16), v_ref[cols, :], preferred_element_type=F32)
                    m = m_new
                outs.append(acc / l)
                lses.append(m + jnp.log(l))
            o_ref[rows, :] = jnp.where(lane < HEAD_DIM, outs[0], outs[1]).astype(BF16)
            lse_ref[rows, :] = jnp.where(lane < HEAD_DIM, lses[0], lses[1])

    qkv_spec, cumr_spec = _attn_specs(t, nq)
    return pl.pallas_call(
        body, name=name, grid=(batch, n_pairs),
        out_shape=[jax.ShapeDtypeStruct((n, w), BF16), jax.ShapeDtypeStruct((n, w), F32)],
        in_specs=[qkv_spec(0), qkv_spec(n_pairs), qkv_spec(2 * n_pairs), cumr_spec],
        out_specs=[qkv_spec(0), qkv_spec(0)],
        compiler_params=_params(("parallel", "parallel")))(qkv, qkv, qkv, cumr)


def _attn_bwd(name, qkv, o, do, lse, cumr, batch):
    n, w3 = qkv.shape
    w = w3 // 3
    t = n // batch
    nq = t // ROW_TILE
    n_pairs = w // LANES
    tq = ROW_TILE

    def body(q_ref, k_ref, v_ref, o_ref, do_ref, lse_ref, cr_ref, dq_ref, dk_ref, dv_ref, dcr_ref, dcq_ref,
             dk_acc, dv_acc):
        pair = pl.program_id(1)
        row = lax.broadcasted_iota(jnp.int32, (tq, tq), 0)
        col = lax.broadcasted_iota(jnp.int32, (tq, tq), 1)
        causal = row >= col
        lane = lax.broadcasted_iota(jnp.int32, (tq, LANES), 1)
        dk_acc[...] = jnp.zeros(dk_acc.shape, F32)
        dv_acc[...] = jnp.zeros(dv_acc.shape, F32)
        dcr_ref[...] = jnp.zeros(dcr_ref.shape, F32)

        @pl.when(pair == 0)
        def _():
            dcq_ref[...] = jnp.zeros(dcq_ref.shape, F32)

        for i in range(nq):
            rows = slice(i * tq, (i + 1) * tq)
            qs = _head_masks(q_ref[rows, :] * 0.125)
            dos = _head_masks(do_ref[rows, :])
            dq = jnp.zeros((tq, LANES), F32)
            dcq = []
            for hh in range(2):
                row_sum = jnp.zeros((tq, 1), F32)
                lse = lse_ref[rows, hh * HEAD_DIM:hh * HEAD_DIM + 1]
                delta = jnp.sum(dos[hh].astype(F32) * o_ref[rows, :].astype(F32), axis=1, keepdims=True)
                for j in range(i + 1):
                    cols = slice(j * tq, (j + 1) * tq)
                    s = _dot_nt(qs[hh], k_ref[cols, :]) - cr_ref[j, hh:hh + 1, :]
                    p = jnp.exp(s - lse)
                    if j == i:
                        p = jnp.where(causal, p, 0.0)
                    dp = _dot_nt(dos[hh], v_ref[cols, :])
                    ds = p * (dp - delta)
                    pb, dsb = p.astype(BF16), ds.astype(BF16)
                    km = _head_masks(k_ref[cols, :])[hh]
                    dv_acc[cols, :] += _dot_tn(pb, dos[hh])
                    dk_acc[cols, :] += _dot_tn(dsb, qs[hh])
                    dq = dq + jnp.dot(dsb, km, preferred_element_type=F32)
                    dcr_ref[j, hh:hh + 1, :] -= jnp.sum(ds, axis=0, keepdims=True)
                    row_sum = row_sum + jnp.sum(ds, axis=1, keepdims=True)
                dcq.append(row_sum)
            dq_ref[rows, :] = (dq * 0.125).astype(BF16)
            dcq_ref[rows, :] = jnp.where(lane == 2 * pair, dcq[0],
                                         jnp.where(lane == 2 * pair + 1, dcq[1], dcq_ref[rows, :]))
        dk_ref[...] = dk_acc[...].astype(BF16)
        dv_ref[...] = dv_acc[...].astype(BF16)

    qkv_spec, cumr_spec = _attn_specs(t, nq)
    act = jax.ShapeDtypeStruct((n, w), BF16)
    return pl.pallas_call(
        body, name=name, grid=(batch, n_pairs),
        out_shape=[act, act, act, jax.ShapeDtypeStruct(cumr.shape, F32), jax.ShapeDtypeStruct((n, LANES), F32)],
        in_specs=[qkv_spec(0), qkv_spec(n_pairs), qkv_spec(2 * n_pairs), qkv_spec(0), qkv_spec(0), qkv_spec(0),
                  cumr_spec],
        out_specs=[qkv_spec(0), qkv_spec(0), qkv_spec(0), cumr_spec, pl.BlockSpec((t, LANES), lambda b, h: (b, 0))],
        scratch_shapes=[pltpu.VMEM((t, LANES), F32), pltpu.VMEM((t, LANES), F32)],
        compiler_params=_params(("parallel", "arbitrary")))(qkv, qkv, qkv, o, do, lse, cumr)


def _adamw(name, parts, w, m, v):
    r, c = w.shape
    tr = 128 if r % 128 == 0 else r
    c1 = 1.0 - ADAM_B1 ** ADAM_STEP
    c2 = 1.0 - ADAM_B2 ** ADAM_STEP

    def body(p_ref, w_ref, m_ref, v_ref, g_ref, d_ref, m2_ref, v2_ref):
        g = p_ref[0].astype(F32)
        for s in range(1, N_DEV):
            g = g + p_ref[s].astype(F32)
        m2 = ADAM_B1 * m_ref[...] + (1.0 - ADAM_B1) * g
        v2 = ADAM_B2 * v_ref[...] + (1.0 - ADAM_B2) * (g * g)
        g_ref[...] = g
        m2_ref[...] = m2
        v2_ref[...] = v2
        d_ref[...] = -ADAM_LR * ((m2 / c1) / (jnp.sqrt(v2 / c2) + ADAM_EPS) + ADAM_WD * w_ref[...])

    blk = pl.BlockSpec((tr, c), lambda i: (i, 0))
    shp = jax.ShapeDtypeStruct((r, c), F32)
    return pl.pallas_call(
        body, name=name, out_shape=[shp] * 4, grid=(r // tr,),
        in_specs=[pl.BlockSpec((N_DEV, tr, c), lambda i: (0, i, 0)), blk, blk, blk], out_specs=[blk] * 4,
        compiler_params=_params(("parallel",)))(parts, w, m, v)


def _cat_small(vals):
    parts = []
    for name in SMALL:
        v = vals[name].reshape(1, -1).astype(F32)
        parts.append(jnp.pad(v, ((0, 0), (0, SMALL_W[name] - v.shape[1]))))
    return jnp.concatenate(parts, axis=1)


def _split_small(row, shapes):
    out, off = {}, 0
    for name in SMALL:
        out[name] = row[0, off:off + SMALL_N[name]].reshape(shapes[name])
        off += SMALL_W[name]
    return out


def _w_in_pieces(lo, hi, n_fg):
    o_fg = 3 * SEG
    out = []
    c = lo
    while c < hi:
        if c < o_fg:
            src, base, end = c // SEG, (c // SEG) * SEG, (c // SEG + 1) * SEG
        elif c < o_fg + n_fg:
            src, base, end = 7, o_fg, o_fg + n_fg
        else:
            k = (c - o_fg - n_fg) // SEG
            src, base, end = 3 + k, o_fg + n_fg + k * SEG, o_fg + n_fg + (k + 1) * SEG
        stop = min(hi, end)
        out.append((src, c - base, stop - base, c))
        c = stop
    return out


def _cols_from_shards(g):
    return jnp.transpose(g, (1, 0, 2)).reshape(g.shape[1], N_DEV * g.shape[2])


def _shards_from_cols(a):
    r, c = a.shape
    return jnp.transpose(a.reshape(r, N_DEV, c // N_DEV), (1, 0, 2))


def kernel(x, meta_tokens, norm_mix_gain, w_in, b_forget, w_attn_out, b_glu, conv_dw_w, conv_dw_b, conv_ln_gain, conv_ln_bias, w_conv_out, b_conv_out, w_out, norm_mlp_gain, w_mlp_up, w_mlp_down, final_norm_gain, loss_target, m_meta_tokens, m_norm_mix_gain, m_w_in, m_b_forget, m_w_attn_out, m_b_glu, m_conv_dw_w, m_conv_dw_b, m_conv_ln_gain, m_conv_ln_bias, m_w_conv_out, m_b_conv_out, m_w_out, m_norm_mlp_gain, m_w_mlp_up, m_w_mlp_down, m_final_norm_gain, v_meta_tokens, v_norm_mix_gain, v_w_in, v_b_forget, v_w_attn_out, v_b_glu, v_conv_dw_w, v_conv_dw_b, v_conv_ln_gain, v_conv_ln_bias, v_w_conv_out, v_b_conv_out, v_w_out, v_norm_mlp_gain, v_w_mlp_up, v_w_mlp_down, v_final_norm_gain):
    weights = dict(meta_tokens=meta_tokens, norm_mix_gain=norm_mix_gain, w_in=w_in, b_forget=b_forget, w_attn_out=w_attn_out, b_glu=b_glu, conv_dw_w=conv_dw_w, conv_dw_b=conv_dw_b, conv_ln_gain=conv_ln_gain, conv_ln_bias=conv_ln_bias, w_conv_out=w_conv_out, b_conv_out=b_conv_out, w_out=w_out, norm_mlp_gain=norm_mlp_gain, w_mlp_up=w_mlp_up, w_mlp_down=w_mlp_down, final_norm_gain=final_norm_gain)
    mom_m = dict(meta_tokens=m_meta_tokens, norm_mix_gain=m_norm_mix_gain, w_in=m_w_in, b_forget=m_b_forget, w_attn_out=m_w_attn_out, b_glu=m_b_glu, conv_dw_w=m_conv_dw_w, conv_dw_b=m_conv_dw_b, conv_ln_gain=m_conv_ln_gain, conv_ln_bias=m_conv_ln_bias, w_conv_out=m_w_conv_out, b_conv_out=m_b_conv_out, w_out=m_w_out, norm_mlp_gain=m_norm_mlp_gain, w_mlp_up=m_w_mlp_up, w_mlp_down=m_w_mlp_down, final_norm_gain=m_final_norm_gain)
    mom_v = dict(meta_tokens=v_meta_tokens, norm_mix_gain=v_norm_mix_gain, w_in=v_w_in, b_forget=v_b_forget, w_attn_out=v_w_attn_out, b_glu=v_b_glu, conv_dw_w=v_conv_dw_w, conv_dw_b=v_conv_dw_b, conv_ln_gain=v_conv_ln_gain, conv_ln_bias=v_conv_ln_bias, w_conv_out=v_w_conv_out, b_conv_out=v_b_conv_out, w_out=v_w_out, norm_mlp_gain=v_norm_mlp_gain, w_mlp_up=v_w_mlp_up, w_mlp_down=v_w_mlp_down, final_norm_gain=v_final_norm_gain)
    names = list(weights)
    batch, seq, d = x.shape
    t = seq + N_META
    n = batch * t
    nq = t // ROW_TILE
    n_pairs = d // LANES
    assert t % ROW_TILE == 0 and d == SEG

    first = [w_in[0].astype(BF16), meta_tokens, conv_dw_w[0]]
    gather_a = _exchange_start("gather_in_start", [(f_, False) for f_ in first], ks=CHIP_PEERS)
    passed = _pass_on_start("gather_in_pass_start",
                            _exchange_wait("gather_in_wait", gather_a, gather_a["token"], fill_own=False))
    w_in_g, meta_g, w_dw_g = _pass_on_wait("gather_in_pass_wait", passed, passed["token"], first)
    gather_b = _exchange_start("gather_rest_start", [
        (w_attn_out[0].astype(BF16), False), (w_conv_out[0].astype(BF16), False), (w_out[0].astype(BF16), False),
        (w_mlp_up[0].astype(BF16), False), (w_mlp_down[0].astype(BF16), False)])
    n_fg = b_forget.shape[1]
    shard_w = w_in.shape[2]
    seg_cols = [[] for _ in range(8)]
    for p in range(N_DEV):
        for src, c0, c1, orig in _w_in_pieces(p * shard_w, (p + 1) * shard_w, n_fg):
            seg_cols[src].append(w_in_g[p][:, orig - p * shard_w:orig - p * shard_w + c1 - c0])
    w_pad = jnp.concatenate([c_ for src in range(8) for c_ in seg_cols[src]]
                            + [jnp.zeros((d, FG_PAD - n_fg), BF16)], axis=1)
    d_ff = w_mlp_down.shape[1] * N_DEV
    ff_blk = d_ff // N_DEV
    meta_f = _cols_from_shards(meta_g)
    w_dw = _cols_from_shards(w_dw_g)

    row2 = lambda v: v.reshape(1, -1)
    g1, g2, g3 = row2(norm_mix_gain) + gather_b["token"][0:1, 0:1], row2(norm_mlp_gain), row2(final_norm_gain)
    b_fg = jnp.pad(b_forget, ((0, 0), (0, FG_PAD - n_fg)))
    h0 = jnp.concatenate([jnp.broadcast_to(meta_f[None], (batch, N_META, d)), x], axis=1).reshape(n, d)
    tgt = jnp.concatenate([jnp.zeros((batch, N_META, d), F32), loss_target], axis=1).reshape(n, d)

    hn1 = _rms_fwd("rms1", h0, g1)
    qkv = _mm_nn("proj_qkv", hn1, w_pad, _w_cols(d, SEG, 0), 3 * SEG, SEG, BF16)
    glu = _mm_nn("proj_glu", hn1, w_pad, _w_cols(d, SEG, 3), 2 * SEG, SEG, BF16)
    gates = _mm_nn("proj_gates", hn1, w_pad, _w_cols(d, SEG, 5), 2 * SEG, SEG, BF16)
    fg = _mm_nn("proj_fg", hn1, w_pad, _w_cols(d, FG_PAD, 7 * SEG // FG_PAD), FG_PAD, FG_PAD, F32)

    cum = _fox_prep_fwd("fox_cumsum", fg, b_fg, batch)
    cum_h = cum.reshape(batch, t, FG_PAD)[:, :, :2 * n_pairs].reshape(batch, t, n_pairs, 2)
    cumr = jnp.transpose(cum_h.reshape(batch, nq, ROW_TILE, n_pairs, 2), (0, 3, 1, 4, 2))
    cumr = jnp.pad(cumr, ((0, 0), (0, 0), (0, 0), (0, 6), (0, 0)))
    o, lse = _attn_fwd("attn_fwd", qkv, cumr, batch)
    rest = _exchange_wait("gather_rest_wait", gather_b, o)
    w_ao, w_co, w_o = [r_.reshape(d, d) for r_ in rest[:3]]
    w_up = rest[3]
    w_dn = rest[4].reshape(d_ff, d)
    a = _mm_nn("attn_out", o, w_ao, _w_cols(d, d, 0), d, d, BF16)

    c1 = _glu_conv_fwd("glu_conv", glu, b_glu, w_dw, conv_dw_b, batch)
    c3 = _ln_silu_fwd("ln_silu", c1, conv_ln_gain, conv_ln_bias)
    c = _mm_nn("conv_out", c3, w_co, _w_cols(d, d, 0), d, d, BF16)

    mrg = _merge_fwd("merge", gates, a, c, b_conv_out)
    mo = _mm_nn("mix_out", mrg, w_o, _w_cols(d, d, 0), d, d, F32)
    h1, hn2 = _rms_fwd("resid_rms2", h0, g2, res=mo)
    per = ff_blk // 512
    up, act = _mm_nn("mlp_up", hn2, w_up, pl.BlockSpec((None, d, 512), lambda i, j: (j // per, 0, j % per)),
                     d_ff, 512, BF16, relu2=True)
    dn = _mm_nn("mlp_down", act, w_dn, _w_cols(d_ff, d, 0), d, d, F32, tm=ROW_TILE)
    dh2, dh2b, loss_blk, dg3 = _final("final_loss", h1, dn, tgt, g3, batch)

    dup = _mm_nt("d_mlp_down", [(dh2b, _a_rows(d), w_dn, _w_rows(d, d))], n, d_ff, d, BF16, relu_bwd_of=up)
    dw_dn = _grad_w("gw_mlp_down", act, dh2b)
    dhn2 = _mm_nt("d_mlp_up", [(dup, _a_rows(ff_blk, g), w_up, pl.BlockSpec((None, 512, ff_blk), lambda i, j, g=g: (g, j, 0)))
                               for g in range(N_DEV)], n, d, 512, F32)
    dw_up = _mm_tn("gw_mlp_up", hn2, lambda a_: 0, d, dup, lambda b_: b_, ff_blk, (N_DEV, d, ff_blk),
                   pl.BlockSpec((None, d, ff_blk), lambda a_, b_: (b_, 0, 0)), (1, N_DEV))
    scatter_1 = _exchange_start("scatter_mlp_start", [(dw_dn.reshape(N_DEV, ff_blk, d), True), (dw_up, True)])
    dh1, dg2, dh1b = _rms_bwd("rms2_bwd", dhn2, h1, g2 + scatter_1["token"][0:1, 0:1], dh2, batch, with_bf16=True)

    dm = _mm_nt("d_mix_out", [(dh1b, _a_rows(d), w_o, _w_rows(d, d))], n, d, d, F32)
    dw_o = _grad_w("gw_mix_out", mrg, dh1b)
    da, dc, dga, dgc, dbco = _merge_bwd("merge_bwd", dm, gates, a, c, b_conv_out)

    do = _mm_nt("d_attn_out", [(da, _a_rows(d), w_ao, _w_rows(d, d))], n, d, d, BF16)
    dw_ao = _grad_w("gw_attn_out", o, da)
    dc3 = _mm_nt("d_conv_out", [(dc, _a_rows(d), w_co, _w_rows(d, d))], n, d, d, F32)
    dw_co = _grad_w("gw_conv_out", c3, dc)

    scatter_2 = _exchange_start("scatter_mix_start", [(dw_.reshape(N_DEV, d // N_DEV, d), True)
                                                      for dw_ in (dw_o, dw_ao, dw_co)])
    dc1, dg_ln, db_ln = _ln_silu_bwd("ln_silu_bwd", dc3, c1, conv_ln_gain + scatter_2["token"][0:1, 0:1],
                                     conv_ln_bias)
    dglu_a, dglu_g, dw_dw, db_dw, dbg_a, dbg_g = _glu_conv_bwd("glu_conv_bwd", dc1, glu, b_glu, w_dw, batch)

    dq, dk, dv, dcumr, dcum_q = _attn_bwd("attn_bwd", qkv, o, do, lse, cumr, batch)
    dcum_k = jnp.transpose(dcumr[:, :, :, :2, :], (0, 2, 4, 1, 3)).reshape(n, 2 * n_pairs)
    dcum_k = jnp.pad(dcum_k, ((0, 0), (0, FG_PAD - 2 * n_pairs)))
    dfg, db_fg = _fox_prep_bwd("fox_cumsum_bwd", dcum_k, dcum_q, fg, b_fg, batch)

    segs = [dq, dk, dv, dglu_a, dglu_g, dga, dgc]
    gw_src = [_grad_w("gw_in_%d" % i, hn1, s_) for i, s_ in enumerate(segs)] + [_grad_w("gw_in_fg", hn1, dfg)]
    dw_in = jnp.stack([jnp.concatenate([gw_src[src][:, c0:c1]
                                        for src, c0, c1, _ in _w_in_pieces(p * shard_w, (p + 1) * shard_w, n_fg)], axis=1)
                       for p in range(N_DEV)])
    scatter_3 = _exchange_start("scatter_in_start", [(dw_in, True)])
    pairs = [(s_, _a_rows(SEG, 0, ROW_TILE), w_pad, _w_rows(512, SEG, i)) for i, s_ in enumerate(segs)]
    pairs.append((dfg, _a_rows(FG_PAD, 0, ROW_TILE), w_pad, _w_rows(512, FG_PAD, 7 * SEG // FG_PAD)))
    dhn1 = _mm_nt("d_proj_in", pairs, n, d, 512, F32, tm=ROW_TILE, after=scatter_3["token"])
    dh0, dg1, dmeta = _rms_bwd("rms1_bwd", dhn1, h0, g1, dh1, batch, with_meta=True)
    grad_x = dh0.reshape(batch, t, d)[:, N_META:, :]

    small_g = dict(norm_mix_gain=dg1, b_forget=db_fg[:, :n_fg], b_glu=jnp.concatenate([dbg_a, dbg_g], axis=1),
                   conv_dw_b=db_dw, conv_ln_gain=dg_ln, conv_ln_bias=db_ln, b_conv_out=dbco, norm_mlp_gain=dg2,
                   final_norm_gain=dg3)
    scatter_4 = _exchange_start("scatter_small_start", [
        (_shards_from_cols(dmeta), True), (_shards_from_cols(dw_dw), True), (_cat_small(small_g), False),
        (loss_blk[0:1, :], False)])

    grads, deltas, new_m, new_v = {}, {}, {}, {}

    def update(k, parts):
        shp = weights[k].shape
        w2 = lambda arr: arr.reshape(parts.shape[1:])
        res_ = _adamw("adamw_" + k, parts, w2(weights[k]), w2(mom_m[k]), w2(mom_v[k]))
        grads[k], deltas[k], new_m[k], new_v[k] = [r.reshape(shp) for r in res_]

    for k, parts in zip(("w_mlp_down", "w_mlp_up"), _exchange_wait("scatter_mlp_wait", scatter_1, scatter_4["token"])):
        update(k, parts)
    for k, parts in zip(("w_out", "w_attn_out", "w_conv_out"),
                        _exchange_wait("scatter_mix_wait", scatter_2, deltas["w_mlp_up"])):
        update(k, parts)
    update("w_in", _exchange_wait("scatter_in_wait", scatter_3, deltas["w_conv_out"])[0])
    reduced = _exchange_wait("scatter_small_wait", scatter_4, deltas["w_in"])
    loss = jnp.sum(reduced.pop()[:, 0, 0])
    for k, parts in zip(("meta_tokens", "conv_dw_w"), reduced[:-1]):
        update(k, parts)
    res = _adamw("adamw_small", reduced[-1], _cat_small(weights), _cat_small(mom_m), _cat_small(mom_v))
    shapes = {k: weights[k].shape for k in SMALL}
    for dst, r in zip((grads, deltas, new_m, new_v), res):
        dst.update(_split_small(r, shapes))

    return (loss, grad_x, *[grads[k] for k in names], *[deltas[k] for k in names],
            *[new_m[k] for k in names], *[new_v[k] for k in names])
```

```python
import functools

import jax
import jax.numpy as jnp
from jax import lax
from jax.experimental import pallas as pl
from jax.experimental.pallas import tpu as pltpu

F32, BF16 = jnp.float32, jnp.bfloat16
N_DEV = 8
N_META = 16
HEAD_DIM = 64
LANES = 128
CONV_W = 31
RMS_EPS = 1e-6
LN_EPS = 1e-5
ROW_TILE = 688
MM_TM = 2 * ROW_TILE
SEG = 1024
FG_PAD = 128
VMEM_LIMIT = 56 * 1024 * 1024
ADAM_LR, ADAM_B1, ADAM_B2, ADAM_EPS, ADAM_WD, ADAM_STEP = 0.001, 0.9, 0.999, 1e-08, 0.01, 10
NEG = -1e30

SMALL = ("norm_mix_gain", "b_forget", "b_glu", "conv_dw_b", "conv_ln_gain", "conv_ln_bias", "b_conv_out",
         "norm_mlp_gain", "final_norm_gain")
SMALL_W = {"norm_mix_gain": 1024, "b_forget": 128, "b_glu": 2048, "conv_dw_b": 1024, "conv_ln_gain": 1024,
           "conv_ln_bias": 1024, "b_conv_out": 1024, "norm_mlp_gain": 1024, "final_norm_gain": 1024}
SMALL_N = {"norm_mix_gain": 1024, "b_forget": 16, "b_glu": 2048, "conv_dw_b": 1024, "conv_ln_gain": 1024,
           "conv_ln_bias": 1024, "b_conv_out": 1024, "norm_mlp_gain": 1024, "final_norm_gain": 1024}


def _params(sem=None):
    return pltpu.CompilerParams(dimension_semantics=sem, vmem_limit_bytes=VMEM_LIMIT)


def _sigmoid(x):
    return 1.0 / (1.0 + jnp.exp(-x))


def _dot_nt(a, b):
    return lax.dot_general(a, b, (((1,), (1,)), ((), ())), preferred_element_type=F32)


def _dot_tn(a, b):
    return lax.dot_general(a, b, (((0,), (0,)), ((), ())), preferred_element_type=F32)


HBM_SPEC = pl.BlockSpec(memory_space=pltpu.HBM)
SEM_SPEC = pl.BlockSpec(memory_space=pltpu.SEMAPHORE)
DATAFLOW = pltpu.SideEffectType.DATAFLOW_SIDE_EFFECTING


def _device_index():
    return 4 * lax.axis_index("x") + 2 * lax.axis_index("y") + lax.axis_index("c")


def _peers():
    x, y, c = lax.axis_index("x"), lax.axis_index("y"), lax.axis_index("c")
    out = []
    for k in range(1, N_DEV):
        px = 1 - x if k & 4 else x
        py = 1 - y if k & 2 else y
        pc = 1 - c if k & 1 else c
        out.append((k, (px, py, pc), 4 * px + 2 * py + pc))
    return out


def _peer_copy(per_dest, src_ref, land_ref, send_sems, recv_sems, a, k, dev, peer):
    src = src_ref.at[peer] if per_dest else src_ref
    return pltpu.make_async_remote_copy(
        src_ref=src, dst_ref=land_ref.at[_device_index()], send_sem=send_sems.at[a * (N_DEV - 1) + k - 1],
        recv_sem=recv_sems.at[a * (N_DEV - 1) + k - 1], device_id=dev, device_id_type=pl.DeviceIdType.MESH)


ALL_PEERS = tuple(range(1, N_DEV))
CHIP_PEERS = (1, 2, 4, 6)
FAR_PEERS = (2, 4, 6)


def _exchange_start(name, items, ks=ALL_PEERS):
    n = len(items)
    per_dest = [it[1] for it in items]

    def body(*refs):
        srcs, lands = refs[:n], refs[n:2 * n]
        send_sems, recv_sems, token = refs[2 * n], refs[2 * n + 1], refs[-1]
        for a in range(n):
            for k, dev, peer in _peers():
                if k in ks:
                    _peer_copy(per_dest[a], srcs[a], lands[a], send_sems, recv_sems, a, k, dev, peer).start()
        token[...] = jnp.zeros(token.shape, F32)

    srcs = [pltpu.with_memory_space_constraint(it[0], pltpu.HBM) for it in items]
    lands = []
    for arr, pd in items:
        shp = arr.shape if pd else (N_DEV,) + arr.shape
        lands.append(pltpu.with_memory_space_constraint(lax.empty(shp, arr.dtype), pltpu.HBM))
    sems = pltpu.SemaphoreType.DMA((n * (N_DEV - 1),))
    res = pl.pallas_call(
        body, name=name,
        out_shape=(sems, sems, *[pltpu.HBM(a_.shape, a_.dtype) for a_ in srcs + lands],
                   jax.ShapeDtypeStruct((8, 128), F32)),
        in_specs=[HBM_SPEC] * (2 * n),
        out_specs=(SEM_SPEC, SEM_SPEC, *[HBM_SPEC] * (2 * n), pl.BlockSpec(memory_space=pltpu.VMEM)),
        input_output_aliases={i: 2 + i for i in range(2 * n)},
        compiler_params=pltpu.CompilerParams(has_side_effects=DATAFLOW),
    )(*srcs, *lands)
    return dict(per_dest=per_dest, ks=ks, send=res[0], recv=res[1], srcs=list(res[2:2 + n]),
                lands=list(res[2 + n:2 + 2 * n]), token=res[-1])


def _fill_own(per_dest, srcs, lands):
    me = _device_index()
    out = []
    for pd, src, land in zip(per_dest, srcs, lands):
        own = lax.dynamic_index_in_dim(src, me, 0, keepdims=True) if pd else src[None]
        out.append(lax.dynamic_update_slice_in_dim(land, own, me, axis=0))
    return out


def _exchange_wait(name, started, after, fill_own=True):
    per_dest = started["per_dest"]
    n = len(per_dest)

    def body(*refs):
        srcs, lands = refs[:n], refs[n:2 * n]
        send_sems, recv_sems = refs[2 * n], refs[2 * n + 1]
        for a in range(n):
            for k, dev, peer in _peers():
                if k in started["ks"]:
                    cp = _peer_copy(per_dest[a], srcs[a], lands[a], send_sems, recv_sems, a, k, dev, peer)
                    cp.wait_send()
                    cp.wait_recv()

    bufs = started["srcs"] + started["lands"]
    after = list(after) if isinstance(after, (list, tuple)) else [after]
    res = pl.pallas_call(
        body, name=name, out_shape=tuple(pltpu.HBM(b_.shape, b_.dtype) for b_ in bufs),
        in_specs=[HBM_SPEC] * (2 * n) + [SEM_SPEC, SEM_SPEC] + [pl.BlockSpec(memory_space=pl.ANY)] * len(after),
        out_specs=tuple([HBM_SPEC] * (2 * n)), input_output_aliases={i: i for i in range(2 * n)},
        compiler_params=pltpu.CompilerParams(has_side_effects=DATAFLOW),
    )(*bufs, started["send"], started["recv"], *after)
    return _fill_own(per_dest, res[:n], res[n:]) if fill_own else list(res[n:])


def _pass_on_copy(land_ref, send_sems, recv_sems, a, idx, slot):
    sibling = (lax.axis_index("x"), lax.axis_index("y"), 1 - lax.axis_index("c"))
    return pltpu.make_async_remote_copy(
        src_ref=land_ref.at[slot], dst_ref=land_ref.at[slot], send_sem=send_sems.at[a * len(FAR_PEERS) + idx],
        recv_sem=recv_sems.at[a * len(FAR_PEERS) + idx], device_id=sibling, device_id_type=pl.DeviceIdType.MESH)


def _pass_on_start(name, lands):
    n = len(lands)

    def body(*refs):
        send_sems, recv_sems, token = refs[n], refs[n + 1], refs[-1]
        slots = {k: peer for k, _, peer in _peers()}
        for a in range(n):
            for idx, k in enumerate(FAR_PEERS):
                _pass_on_copy(refs[a], send_sems, recv_sems, a, idx, slots[k]).start()
        token[...] = jnp.zeros(token.shape, F32)

    lands = [pltpu.with_memory_space_constraint(l_, pltpu.HBM) for l_ in lands]
    sems = pltpu.SemaphoreType.DMA((n * len(FAR_PEERS),))
    res = pl.pallas_call(
        body, name=name,
        out_shape=(sems, sems, *[pltpu.HBM(l_.shape, l_.dtype) for l_ in lands], jax.ShapeDtypeStruct((8, 128), F32)),
        in_specs=[HBM_SPEC] * n, out_specs=(SEM_SPEC, SEM_SPEC, *[HBM_SPEC] * n, pl.BlockSpec(memory_space=pltpu.VMEM)),
        input_output_aliases={i: 2 + i for i in range(n)},
        compiler_params=pltpu.CompilerParams(has_side_effects=DATAFLOW),
    )(*lands)
    return dict(send=res[0], recv=res[1], lands=list(res[2:2 + n]), token=res[-1])


def _pass_on_wait(name, passed, after, owns):
    n = len(passed["lands"])

    def body(*refs):
        send_sems, recv_sems = refs[n], refs[n + 1]
        slots = {k: peer for k, _, peer in _peers()}
        for a in range(n):
            for idx, k in enumerate(FAR_PEERS):
                _pass_on_copy(refs[a], send_sems, recv_sems, a, idx, slots[k]).wait_send()
                _pass_on_copy(refs[a], send_sems, recv_sems, a, idx, slots[k ^ 1]).wait_recv()

    res = pl.pallas_call(
        body, name=name, out_shape=tuple(pltpu.HBM(l_.shape, l_.dtype) for l_ in passed["lands"]),
        in_specs=[HBM_SPEC] * n + [SEM_SPEC, SEM_SPEC, pl.BlockSpec(memory_space=pl.ANY)],
        out_specs=tuple([HBM_SPEC] * n), input_output_aliases={i: i for i in range(n)},
        compiler_params=pltpu.CompilerParams(has_side_effects=DATAFLOW),
    )(*passed["lands"], passed["send"], passed["recv"], after)
    return _fill_own([False] * n, owns, res)


def _mm_nn(name, x, w, w_spec, n_out, tn, out_dtype, relu2=False, tm=MM_TM):
    m, k = x.shape

    def body(x_ref, w_ref, *outs):
        acc = jnp.dot(x_ref[...], w_ref[...], preferred_element_type=F32)
        outs[0][...] = acc.astype(outs[0].dtype)
        if relu2:
            r = jnp.maximum(acc, 0.0)
            outs[1][...] = (r * r).astype(outs[1].dtype)

    o_spec = pl.BlockSpec((tm, tn), lambda i, j: (i, j))
    shapes = [jax.ShapeDtypeStruct((m, n_out), out_dtype)]
    if relu2:
        shapes.append(jax.ShapeDtypeStruct((m, n_out), BF16))
    res = pl.pallas_call(
        body, name=name, out_shape=shapes, grid=(m // tm, n_out // tn),
        in_specs=[pl.BlockSpec((tm, k), lambda i, j: (i, 0)), w_spec],
        out_specs=[o_spec] * len(shapes), compiler_params=_params(("parallel", "parallel")),
    )(x, w)
    return res if relu2 else res[0]


def _mm_nt(name, pairs, m, n_out, tn, out_dtype, relu_bwd_of=None, tm=MM_TM, after=None):
    np_ = len(pairs)

    def body(*refs):
        acc = None
        for p in range(np_):
            if len(pairs[p]) == 5:
                d = jnp.dot(refs[2 * p][...], refs[2 * p + 1][...], preferred_element_type=F32)
            else:
                d = _dot_nt(refs[2 * p][...], refs[2 * p + 1][...])
            acc = d if acc is None else acc + d
        if relu_bwd_of is not None:
            acc = acc * (2.0 * jnp.maximum(refs[2 * np_][...].astype(F32), 0.0))
        refs[-1][...] = acc.astype(refs[-1].dtype)

    o_spec = pl.BlockSpec((tm, tn), lambda i, j: (i, j))
    operands, specs = [], []
    for pair in pairs:
        operands += [pair[0], pair[2]]
        specs += [pair[1], pair[3]]
    if relu_bwd_of is not None:
        operands.append(relu_bwd_of)
        specs.append(o_spec)
    if after is not None:
        operands.append(after)
        specs.append(pl.BlockSpec((8, 128), lambda i, j: (0, 0)))
    return pl.pallas_call(
        body, name=name, out_shape=jax.ShapeDtypeStruct((m, n_out), out_dtype), grid=(m // tm, n_out // tn),
        in_specs=specs, out_specs=o_spec, compiler_params=_params(("parallel", "parallel")),
    )(*operands)


def _mm_tn(name, x, x_col, ta, dy, dy_col, tb, out_shape, out_spec, grid_ab):
    m = x.shape[0]

    def body(x_ref, dy_ref, o_ref):
        o_ref[...] = _dot_tn(x_ref[...], dy_ref[...]).astype(BF16)

    return pl.pallas_call(
        body, name=name, out_shape=jax.ShapeDtypeStruct(out_shape, BF16), grid=grid_ab,
        in_specs=[pl.BlockSpec((m, ta), lambda a, b: (0, x_col(a))),
                  pl.BlockSpec((m, tb), lambda a, b: (0, dy_col(b)))],
        out_specs=out_spec, compiler_params=_params(("parallel", "parallel")),
    )(x, dy)


def _w_cols(k, tn, off_blocks):
    return pl.BlockSpec((k, tn), lambda i, j: (0, off_blocks + j))


def _a_rows(kw, col_block=0, tm=MM_TM):
    return pl.BlockSpec((tm, kw), lambda i, j: (i, col_block))


def _w_rows(tn, kw, col_block=0):
    return pl.BlockSpec((tn, kw), lambda i, j: (j, col_block))


def _wt_rows(tn, off):
    return pl.BlockSpec((pl.Element(tn), pl.Element(SEG)), lambda i, j: (pl.multiple_of(off + tn * j, 16), 0))


def _wt_block(k, off, tn):
    return pl.BlockSpec((pl.Element(k), pl.Element(tn)), lambda i, j: (off, pl.multiple_of(tn * j, 128)))


def _grad_w(name, x, dy):
    na, nb = x.shape[1], dy.shape[1]
    ta, tb = min(na, 1024), min(nb, 512)
    return _mm_tn(name, x, lambda a: a, ta, dy, lambda b: b, tb, (na, nb),
                  pl.BlockSpec((ta, tb), lambda a, b: (a, b)), (na // ta, nb // tb))


def _row_spec(width):
    return pl.BlockSpec((ROW_TILE, width), lambda i: (i, 0))


def _vec_spec(width):
    return pl.BlockSpec((1, width), lambda i: (0, 0))


def _rms_fwd(name, h, g, res=None):
    n, d = h.shape

    def body(*refs):
        if res is None:
            h_ref, g_ref, hn_ref = refs
            hv = h_ref[...]
        else:
            h_ref, r_ref, g_ref, hs_ref, hn_ref = refs
            hv = h_ref[...] + r_ref[...]
            hs_ref[...] = hv
        r = lax.rsqrt(jnp.mean(hv * hv, axis=-1, keepdims=True) + RMS_EPS)
        hn_ref[...] = (hv * r * g_ref[...]).astype(BF16)

    ins = [h, g] if res is None else [h, res, g]
    in_specs = [_row_spec(d), _vec_spec(d)] if res is None else [_row_spec(d), _row_spec(d), _vec_spec(d)]
    hn_shape = jax.ShapeDtypeStruct((n, d), BF16)
    if res is None:
        out_shape, out_specs = hn_shape, _row_spec(d)
    else:
        out_shape, out_specs = [jax.ShapeDtypeStruct((n, d), F32), hn_shape], [_row_spec(d), _row_spec(d)]
    return pl.pallas_call(body, name=name, out_shape=out_shape, grid=(n // ROW_TILE,), in_specs=in_specs,
                          out_specs=out_specs, compiler_params=_params(("parallel",)))(*ins)


def _rms_bwd(name, dhn, h, g, dres, batch, with_bf16=False, with_meta=False):
    n, d = h.shape
    t = n // batch
    nt = t // ROW_TILE

    def body(dhn_ref, h_ref, g_ref, dres_ref, *outs):
        first = (pl.program_id(0) == 0) & (pl.program_id(1) == 0)
        hv = h_ref[...]
        r = lax.rsqrt(jnp.mean(hv * hv, axis=-1, keepdims=True) + RMS_EPS)
        nrm = hv * r
        dn = dhn_ref[...] * g_ref[...]
        dh = dres_ref[...] + r * (dn - nrm * jnp.mean(dn * nrm, axis=-1, keepdims=True))
        outs[0][...] = dh
        dg_ref = outs[1]

        @pl.when(first)
        def _():
            dg_ref[...] = jnp.zeros(dg_ref.shape, F32)

        dg_ref[...] += jnp.sum(dhn_ref[...] * nrm, axis=0, keepdims=True)
        nxt = 2
        if with_bf16:
            outs[nxt][...] = dh.astype(BF16)
            nxt += 1
        if with_meta:
            meta_ref = outs[nxt]

            @pl.when(first)
            def _():
                meta_ref[...] = jnp.zeros(meta_ref.shape, F32)

            @pl.when(pl.program_id(1) == 0)
            def _():
                meta_ref[...] += dh[0:N_META, :]

    row = pl.BlockSpec((ROW_TILE, d), lambda b, j: (b * nt + j, 0))
    vec = pl.BlockSpec((1, d), lambda b, j: (0, 0))
    shapes = [jax.ShapeDtypeStruct((n, d), F32), jax.ShapeDtypeStruct((1, d), F32)]
    specs = [row, vec]
    if with_bf16:
        shapes.append(jax.ShapeDtypeStruct((n, d), BF16))
        specs.append(row)
    if with_meta:
        shapes.append(jax.ShapeDtypeStruct((N_META, d), F32))
        specs.append(pl.BlockSpec((N_META, d), lambda b, j: (0, 0)))
    return pl.pallas_call(body, name=name, out_shape=shapes, grid=(batch, nt), in_specs=[row, row, vec, row],
                          out_specs=specs, compiler_params=_params(("arbitrary", "arbitrary")))(dhn, h, g, dres)


def _final(name, h1, dn, tgt, g, batch):
    n, d = h1.shape
    t = n // batch
    nt = t // ROW_TILE

    def body(h1_ref, dn_ref, tgt_ref, g_ref, dh_ref, dhb_ref, loss_ref, dg_ref):
        first = (pl.program_id(0) == 0) & (pl.program_id(1) == 0)
        hv = h1_ref[...] + dn_ref[...]
        r = lax.rsqrt(jnp.mean(hv * hv, axis=-1, keepdims=True) + RMS_EPS)
        nrm = hv * r
        gv = g_ref[...]
        pos = pl.program_id(1) * ROW_TILE + lax.broadcasted_iota(jnp.int32, (ROW_TILE, 1), 0)
        diff = jnp.where(pos >= N_META, nrm * gv - tgt_ref[...], 0.0)
        dy = diff * (1.0 / d)

        @pl.when(first)
        def _():
            loss_ref[...] = jnp.zeros(loss_ref.shape, F32)
            dg_ref[...] = jnp.zeros(dg_ref.shape, F32)

        loss_ref[...] += jnp.full(loss_ref.shape, 0.5 / d, F32) * jnp.sum(diff * diff)
        dg_ref[...] += jnp.sum(dy * nrm, axis=0, keepdims=True)
        dng = dy * gv
        dh = r * (dng - nrm * jnp.mean(dng * nrm, axis=-1, keepdims=True))
        dh_ref[...] = dh
        dhb_ref[...] = dh.astype(BF16)

    row = pl.BlockSpec((ROW_TILE, d), lambda b, j: (b * nt + j, 0))
    vec = pl.BlockSpec((1, d), lambda b, j: (0, 0))
    return pl.pallas_call(
        body, name=name, grid=(batch, nt), in_specs=[row, row, row, vec],
        out_shape=[jax.ShapeDtypeStruct((n, d), F32), jax.ShapeDtypeStruct((n, d), BF16),
                   jax.ShapeDtypeStruct((8, 128), F32), jax.ShapeDtypeStruct((1, d), F32)],
        out_specs=[row, row, pl.BlockSpec((8, 128), lambda b, j: (0, 0)), vec],
        compiler_params=_params(("arbitrary", "arbitrary")))(h1, dn, tgt, g)


def _ln_silu_fwd(name, c1, g, b):
    n, d = c1.shape

    def body(c_ref, g_ref, b_ref, o_ref):
        xv = c_ref[...]
        xc = xv - jnp.mean(xv, axis=-1, keepdims=True)
        rstd = lax.rsqrt(jnp.mean(xc * xc, axis=-1, keepdims=True) + LN_EPS)
        c2 = xc * rstd * g_ref[...] + b_ref[...]
        o_ref[...] = (c2 * _sigmoid(c2)).astype(BF16)

    return pl.pallas_call(body, name=name, out_shape=jax.ShapeDtypeStruct((n, d), BF16), grid=(n // ROW_TILE,),
                          in_specs=[_row_spec(d), _vec_spec(d), _vec_spec(d)], out_specs=_row_spec(d),
                          compiler_params=_params(("parallel",)))(c1, g, b)


def _ln_silu_bwd(name, dc3, c1, g, b):
    n, d = c1.shape

    def body(d_ref, c_ref, g_ref, b_ref, dc1_ref, dg_ref, db_ref):
        xv = c_ref[...]
        xc = xv - jnp.mean(xv, axis=-1, keepdims=True)
        rstd = lax.rsqrt(jnp.mean(xc * xc, axis=-1, keepdims=True) + LN_EPS)
        xh = xc * rstd
        c2 = xh * g_ref[...] + b_ref[...]
        s = _sigmoid(c2)
        dc2 = d_ref[...] * (s * (1.0 + c2 * (1.0 - s)))

        @pl.when(pl.program_id(0) == 0)
        def _():
            dg_ref[...] = jnp.zeros(dg_ref.shape, F32)
            db_ref[...] = jnp.zeros(db_ref.shape, F32)

        dg_ref[...] += jnp.sum(dc2 * xh, axis=0, keepdims=True)
        db_ref[...] += jnp.sum(dc2, axis=0, keepdims=True)
        dxh = dc2 * g_ref[...]
        dc1_ref[...] = rstd * (dxh - jnp.mean(dxh, axis=-1, keepdims=True)
                               - xh * jnp.mean(dxh * xh, axis=-1, keepdims=True))

    return pl.pallas_call(
        body, name=name, grid=(n // ROW_TILE,),
        out_shape=[jax.ShapeDtypeStruct((n, d), F32), jax.ShapeDtypeStruct((1, d), F32),
                   jax.ShapeDtypeStruct((1, d), F32)],
        in_specs=[_row_spec(d), _row_spec(d), _vec_spec(d), _vec_spec(d)],
        out_specs=[_row_spec(d), _vec_spec(d), _vec_spec(d)],
        compiler_params=_params(("arbitrary",)))(dc3, c1, g, b)


MERGE_TC = 512


def _merge_fwd(name, gates, a, c, b_co):
    n, d = a.shape
    nc = d // MERGE_TC

    def body(ga_ref, gc_ref, a_ref, c_ref, b_ref, m_ref):
        f32 = lambda r_: r_[...].astype(F32)
        m = _sigmoid(f32(ga_ref)) * f32(a_ref) + _sigmoid(f32(gc_ref)) * (f32(c_ref) + b_ref[...])
        m_ref[...] = m.astype(BF16)

    blk = lambda off: pl.BlockSpec((ROW_TILE, MERGE_TC), lambda i, j: (i, off + j))
    return pl.pallas_call(
        body, name=name, out_shape=jax.ShapeDtypeStruct((n, d), BF16), grid=(n // ROW_TILE, nc),
        in_specs=[blk(0), blk(nc), blk(0), blk(0), pl.BlockSpec((1, MERGE_TC), lambda i, j: (0, j))],
        out_specs=blk(0), compiler_params=_params(("parallel", "parallel")))(gates, gates, a, c, b_co)


def _merge_bwd(name, dm, gates, a, c, b_co):
    n, d = a.shape
    nc = d // MERGE_TC

    def body(dm_ref, ga_ref, gc_ref, a_ref, c_ref, b_ref, da_ref, dc_ref, dga_ref, dgc_ref, dbco_ref):
        f32 = lambda r_: r_[...].astype(F32)
        dmv = dm_ref[...]
        sa, sc = _sigmoid(f32(ga_ref)), _sigmoid(f32(gc_ref))
        dc = dmv * sc
        da_ref[...] = (dmv * sa).astype(BF16)
        dc_ref[...] = dc.astype(BF16)
        dga_ref[...] = (dmv * f32(a_ref) * sa * (1.0 - sa)).astype(BF16)
        dgc_ref[...] = (dmv * (f32(c_ref) + b_ref[...]) * sc * (1.0 - sc)).astype(BF16)

        @pl.when(pl.program_id(1) == 0)
        def _():
            dbco_ref[...] = jnp.zeros(dbco_ref.shape, F32)

        dbco_ref[...] += jnp.sum(dc, axis=0, keepdims=True)

    blk = lambda off: pl.BlockSpec((ROW_TILE, MERGE_TC), lambda j, i: (i, off + j))
    vec = pl.BlockSpec((1, MERGE_TC), lambda j, i: (0, j))
    act = jax.ShapeDtypeStruct((n, d), BF16)
    return pl.pallas_call(
        body, name=name, grid=(nc, n // ROW_TILE),
        out_shape=[act, act, act, act, jax.ShapeDtypeStruct((1, d), F32)],
        in_specs=[blk(0), blk(0), blk(nc), blk(0), blk(0), vec],
        out_specs=[blk(0), blk(0), blk(0), blk(0), vec],
        compiler_params=_params(("parallel", "arbitrary")))(dm, gates, gates, a, c, b_co)


CONV_TC = 128
CONV_HALO = 32


def _conv_chunk(t):
    return 48 if t % 48 == 0 else 32 if t % 32 == 0 else 16


def _fold8(x):
    out = x[0:8]
    for k in range(1, x.shape[0] // 8):
        out = out + x[8 * k:8 * k + 8]
    return out


def _glu_conv_fwd(name, glu, b_glu, w_dw, b_dw, batch):
    n, c2 = glu.shape
    c = c2 // 2
    t = n // batch
    nc = c // CONV_TC

    def body(a_ref, gt_ref, ba_ref, bg_ref, w_ref, bdw_ref, o_ref, pad_ref):
        u = (a_ref[...].astype(F32) + ba_ref[...]) * _sigmoid(gt_ref[...].astype(F32) + bg_ref[...])
        pad_ref[0:CONV_HALO, :] = jnp.zeros((CONV_HALO, CONV_TC), F32)
        pad_ref[CONV_HALO:CONV_HALO + t, :] = u
        ch = _conv_chunk(t)
        for r0 in range(0, t, ch):
            acc = jnp.zeros((ch, CONV_TC), F32) + bdw_ref[...]
            for j in range(CONV_W):
                off = r0 + CONV_HALO - (CONV_W - 1) + j
                acc = acc + w_ref[j:j + 1, :] * pad_ref[off:off + ch, :]
            o_ref[r0:r0 + ch, :] = acc

    seq = lambda off: pl.BlockSpec((t, CONV_TC), lambda b, j: (b, off + j))
    vec = lambda off: pl.BlockSpec((1, CONV_TC), lambda b, j: (0, off + j))
    return pl.pallas_call(
        body, name=name, out_shape=jax.ShapeDtypeStruct((n, c), F32), grid=(batch, nc),
        in_specs=[seq(0), seq(nc), vec(0), vec(nc), pl.BlockSpec((CONV_W, CONV_TC), lambda b, j: (0, j)), vec(0)],
        out_specs=seq(0), scratch_shapes=[pltpu.VMEM((t + CONV_HALO, CONV_TC), F32)],
        compiler_params=_params(("parallel", "parallel")))(glu, glu, b_glu, b_glu, w_dw, b_dw)


def _glu_conv_bwd(name, dc1, glu, b_glu, w_dw, batch):
    n, c2 = glu.shape
    c = c2 // 2
    t = n // batch
    nc = c // CONV_TC

    def body(d_ref, a_ref, gt_ref, ba_ref, bg_ref, w_ref, dga_ref, dgg_ref, dw_ref, dbdw_ref, dba_ref, dbg_ref,
             padu_ref, padd_ref):
        av = a_ref[...].astype(F32) + ba_ref[...]
        sg = _sigmoid(gt_ref[...].astype(F32) + bg_ref[...])
        dc = d_ref[...]
        padu_ref[0:CONV_HALO, :] = jnp.zeros((CONV_HALO, CONV_TC), F32)
        padu_ref[CONV_HALO:CONV_HALO + t, :] = av * sg
        padd_ref[0:t, :] = dc
        padd_ref[t:t + CONV_HALO, :] = jnp.zeros((CONV_HALO, CONV_TC), F32)

        @pl.when(pl.program_id(1) == 0)
        def _():
            dw_ref[...] = jnp.zeros(dw_ref.shape, F32)
            dbdw_ref[...] = jnp.zeros(dbdw_ref.shape, F32)
            dba_ref[...] = jnp.zeros(dba_ref.shape, F32)
            dbg_ref[...] = jnp.zeros(dbg_ref.shape, F32)

        ch = _conv_chunk(t)
        zero8 = jnp.zeros((8, CONV_TC), F32)
        dw_acc = [zero8] * CONV_W
        sum_dc, sum_a, sum_g = zero8, zero8, zero8
        for r0 in range(0, t, ch):
            dcc = d_ref[r0:r0 + ch, :]
            du = jnp.zeros((ch, CONV_TC), F32)
            for j in range(CONV_W):
                back = r0 + CONV_W - 1 - j
                du = du + w_ref[j:j + 1, :] * padd_ref[back:back + ch, :]
                off = r0 + CONV_HALO - (CONV_W - 1) + j
                dw_acc[j] = dw_acc[j] + _fold8(dcc * padu_ref[off:off + ch, :])
            sgc = _sigmoid(gt_ref[r0:r0 + ch, :].astype(F32) + bg_ref[...])
            dga = du * sgc
            dgg = du * padu_ref[CONV_HALO + r0:CONV_HALO + r0 + ch, :] * (1.0 - sgc)
            dga_ref[r0:r0 + ch, :] = dga.astype(BF16)
            dgg_ref[r0:r0 + ch, :] = dgg.astype(BF16)
            sum_dc, sum_a, sum_g = sum_dc + _fold8(dcc), sum_a + _fold8(dga), sum_g + _fold8(dgg)
        for j in range(CONV_W):
            dw_ref[j:j + 1, :] += jnp.sum(dw_acc[j], axis=0, keepdims=True)
        dbdw_ref[...] += jnp.sum(sum_dc, axis=0, keepdims=True)
        dba_ref[...] += jnp.sum(sum_a, axis=0, keepdims=True)
        dbg_ref[...] += jnp.sum(sum_g, axis=0, keepdims=True)

    seq = lambda off: pl.BlockSpec((t, CONV_TC), lambda j, b: (b, off + j))
    vec = lambda off: pl.BlockSpec((1, CONV_TC), lambda j, b: (0, off + j))
    wsp = pl.BlockSpec((CONV_W, CONV_TC), lambda j, b: (0, j))
    act = jax.ShapeDtypeStruct((n, c), BF16)
    v = jax.ShapeDtypeStruct((1, c), F32)
    return pl.pallas_call(
        body, name=name, grid=(nc, batch),
        out_shape=[act, act, jax.ShapeDtypeStruct((CONV_W, c), F32), v, v, v],
        in_specs=[seq(0), seq(0), seq(nc), vec(0), vec(nc), wsp],
        out_specs=[seq(0), seq(0), wsp, vec(0), vec(0), vec(0)],
        scratch_shapes=[pltpu.VMEM((t + CONV_HALO, CONV_TC), F32), pltpu.VMEM((t + CONV_HALO, CONV_TC), F32)],
        compiler_params=_params(("parallel", "arbitrary")))(dc1, glu, glu, b_glu, b_glu, w_dw)


def _split3(x):
    hi = x.astype(BF16)
    r = x - hi.astype(F32)
    mid = r.astype(BF16)
    lo = (r - mid.astype(F32)).astype(BF16)
    return hi, mid, lo


def _tri_matmul(tri, x):
    hi, mid, lo = _split3(x)
    dot = lambda v: jnp.dot(tri, v, preferred_element_type=F32)
    return dot(hi) + dot(mid) + dot(lo)


def _fox_prep_fwd(name, fg, b_fg, batch):
    n, w = fg.shape
    t = n // batch
    nq = t // ROW_TILE

    def body(fg_ref, b_ref, cum_ref):
        row = lax.broadcasted_iota(jnp.int32, (ROW_TILE, ROW_TILE), 0)
        col = lax.broadcasted_iota(jnp.int32, (ROW_TILE, ROW_TILE), 1)
        tri = (row >= col).astype(BF16)
        for k in range(nq):
            rows = slice(k * ROW_TILE, (k + 1) * ROW_TILE)
            z = fg_ref[rows, :] + b_ref[...]
            logf = jnp.minimum(z, 0.0) - jnp.log(1.0 + jnp.exp(-jnp.abs(z)))
            cum = _tri_matmul(tri, logf)
            if k > 0:
                cum = cum + cum_ref[k * ROW_TILE - 1:k * ROW_TILE, :]
            cum_ref[rows, :] = cum

    seq = pl.BlockSpec((t, w), lambda b: (b, 0))
    return pl.pallas_call(body, name=name, out_shape=jax.ShapeDtypeStruct((n, w), F32), grid=(batch,),
                          in_specs=[seq, pl.BlockSpec((1, w), lambda b: (0, 0))], out_specs=seq,
                          compiler_params=_params(("parallel",)))(fg, b_fg)


def _fox_prep_bwd(name, dcum_k, dcum_q, fg, b_fg, batch):
    n, w = fg.shape
    t = n // batch
    nq = t // ROW_TILE

    def body(dk_ref, dq_ref, fg_ref, b_ref, dfg_ref, db_ref, rev_ref):
        row = lax.broadcasted_iota(jnp.int32, (ROW_TILE, ROW_TILE), 0)
        col = lax.broadcasted_iota(jnp.int32, (ROW_TILE, ROW_TILE), 1)
        tri = (col >= row).astype(BF16)

        @pl.when(pl.program_id(0) == 0)
        def _():
            db_ref[...] = jnp.zeros(db_ref.shape, F32)

        for k in reversed(range(nq)):
            rows = slice(k * ROW_TILE, (k + 1) * ROW_TILE)
            dlog = _tri_matmul(tri, dk_ref[rows, :] + dq_ref[rows, :])
            if k < nq - 1:
                dlog = dlog + rev_ref[(k + 1) * ROW_TILE:(k + 1) * ROW_TILE + 1, :]
            rev_ref[rows, :] = dlog
            dfg = dlog * _sigmoid(-(fg_ref[rows, :] + b_ref[...]))
            dfg_ref[rows, :] = dfg.astype(BF16)
            db_ref[...] += jnp.sum(dfg, axis=0, keepdims=True)

    seq = pl.BlockSpec((t, w), lambda b: (b, 0))
    vec = pl.BlockSpec((1, w), lambda b: (0, 0))
    return pl.pallas_call(
        body, name=name, grid=(batch,),
        out_shape=[jax.ShapeDtypeStruct((n, w), BF16), jax.ShapeDtypeStruct((1, w), F32)],
        in_specs=[seq, seq, seq, vec], out_specs=[seq, vec], scratch_shapes=[pltpu.VMEM((t, w), F32)],
        compiler_params=_params(("arbitrary",)))(dcum_k, dcum_q, fg, b_fg)


def _head_masks(x):
    lane = lax.broadcasted_iota(jnp.int32, x.shape, 1)
    zero = jnp.zeros(x.shape, x.dtype)
    return jnp.where(lane < HEAD_DIM, x, zero), jnp.where(lane >= HEAD_DIM, x, zero)


def _attn_specs(t, nq):
    qkv = lambda off: pl.BlockSpec((t, LANES), lambda b, h: (b, off + h))
    cumr = pl.BlockSpec((None, None, nq, 8, ROW_TILE), lambda b, h: (b, h, 0, 0, 0))
    return qkv, cumr


def _attn_fwd(name, qkv, cumr, batch):
    n, w3 = qkv.shape
    w = w3 // 3
    t = n // batch
    nq = t // ROW_TILE
    n_pairs = w // LANES
    tq = ROW_TILE

    def body(q_ref, k_ref, v_ref, cr_ref, o_ref, lse_ref):
        row = lax.broadcasted_iota(jnp.int32, (tq, tq), 0)
        col = lax.broadcasted_iota(jnp.int32, (tq, tq), 1)
        causal = row >= col
        lane = lax.broadcasted_iota(jnp.int32, (tq, LANES), 1)
        for i in range(nq):
            rows = slice(i * tq, (i + 1) * tq)
            qs = _head_masks(q_ref[rows, :] * 0.125)
            outs, lses = [], []
            for hh in range(2):
                m = jnp.full((tq, 1), NEG, F32)
                l = jnp.zeros((tq, 1), F32)
                acc = jnp.zeros((tq, LANES), F32)
                for j in range(i + 1):
                    cols = slice(j * tq, (j + 1) * tq)
                    s = _dot_nt(qs[hh], k_ref[cols, :]) - cr_ref[j, hh:hh + 1, :]
                    if j == i:
                        s = jnp.where(causal, s, NEG)
                    m_new = jnp.maximum(m, jnp.max(s, axis=1, keepdims=True))
                    alpha = jnp.exp(m - m_new)
                    p = jnp.exp(s - m_new)
                    l = alpha * l + jnp.sum(p, axis=1, keepdims=True)
                    acc = alpha * acc + jnp.dot(p.astype(BF16), v_ref[cols, :], preferred_element_type=F32)
                    m = m_new
                outs.append(acc / l)
                lses.append(m + jnp.log(l))
            o_ref[rows, :] = jnp.where(lane < HEAD_DIM, outs[0], outs[1]).astype(BF16)
            lse_ref[rows, :] = jnp.where(lane < HEAD_DIM, lses[0], lses[1])

    qkv_spec, cumr_spec = _attn_specs(t, nq)
    return pl.pallas_call(
        body, name=name, grid=(batch, n_pairs),
        out_shape=[jax.ShapeDtypeStruct((n, w), BF16), jax.ShapeDtypeStruct((n, w), F32)],
        in_specs=[qkv_spec(0), qkv_spec(n_pairs), qkv_spec(2 * n_pairs), cumr_spec],
        out_specs=[qkv_spec(0), qkv_spec(0)],
        compiler_params=_params(("parallel", "parallel")))(qkv, qkv, qkv, cumr)


def _attn_bwd(name, qkv, o, do, lse, cumr, batch):
    n, w3 = qkv.shape
    w = w3 // 3
    t = n // batch
    nq = t // ROW_TILE
    n_pairs = w // LANES
    tq = ROW_TILE

    def body(q_ref, k_ref, v_ref, o_ref, do_ref, lse_ref, cr_ref, dq_ref, dk_ref, dv_ref, dcr_ref, dcq_ref,
             dk_acc, dv_acc):
        pair = pl.program_id(1)
        row = lax.broadcasted_iota(jnp.int32, (tq, tq), 0)
        col = lax.broadcasted_iota(jnp.int32, (tq, tq), 1)
        causal = row >= col
        lane = lax.broadcasted_iota(jnp.int32, (tq, LANES), 1)
        dk_acc[...] = jnp.zeros(dk_acc.shape, F32)
        dv_acc[...] = jnp.zeros(dv_acc.shape, F32)
        dcr_ref[...] = jnp.zeros(dcr_ref.shape, F32)

        @pl.when(pair == 0)
        def _():
            dcq_ref[...] = jnp.zeros(dcq_ref.shape, F32)

        for i in range(nq):
            rows = slice(i * tq, (i + 1) * tq)
            qs = _head_masks(q_ref[rows, :] * 0.125)
            dos = _head_masks(do_ref[rows, :])
            dq = jnp.zeros((tq, LANES), F32)
            dcq = []
            for hh in range(2):
                row_sum = jnp.zeros((tq, 1), F32)
                lse = lse_ref[rows, hh * HEAD_DIM:hh * HEAD_DIM + 1]
                delta = jnp.sum(dos[hh].astype(F32) * o_ref[rows, :].astype(F32), axis=1, keepdims=True)
                for j in range(i + 1):
                    cols = slice(j * tq, (j + 1) * tq)
                    s = _dot_nt(qs[hh], k_ref[cols, :]) - cr_ref[j, hh:hh + 1, :]
                    p = jnp.exp(s - lse)
                    if j == i:
                        p = jnp.where(causal, p, 0.0)
                    dp = _dot_nt(dos[hh], v_ref[cols, :])
                    ds = p * (dp - delta)
                    pb, dsb = p.astype(BF16), ds.astype(BF16)
                    km = _head_masks(k_ref[cols, :])[hh]
                    dv_acc[cols, :] += _dot_tn(pb, dos[hh])
                    dk_acc[cols, :] += _dot_tn(dsb, qs[hh])
                    dq = dq + jnp.dot(dsb, km, preferred_element_type=F32)
                    dcr_ref[j, hh:hh + 1, :] -= jnp.sum(ds, axis=0, keepdims=True)
                    row_sum = row_sum + jnp.sum(ds, axis=1, keepdims=True)
                dcq.append(row_sum)
            dq_ref[rows, :] = (dq * 0.125).astype(BF16)
            dcq_ref[rows, :] = jnp.where(lane == 2 * pair, dcq[0],
                                         jnp.where(lane == 2 * pair + 1, dcq[1], dcq_ref[rows, :]))
        dk_ref[...] = dk_acc[...].astype(BF16)
        dv_ref[...] = dv_acc[...].astype(BF16)

    qkv_spec, cumr_spec = _attn_specs(t, nq)
    act = jax.ShapeDtypeStruct((n, w), BF16)
    return pl.pallas_call(
        body, name=name, grid=(batch, n_pairs),
        out_shape=[act, act, act, jax.ShapeDtypeStruct(cumr.shape, F32), jax.ShapeDtypeStruct((n, LANES), F32)],
        in_specs=[qkv_spec(0), qkv_spec(n_pairs), qkv_spec(2 * n_pairs), qkv_spec(0), qkv_spec(0), qkv_spec(0),
                  cumr_spec],
        out_specs=[qkv_spec(0), qkv_spec(0), qkv_spec(0), cumr_spec, pl.BlockSpec((t, LANES), lambda b, h: (b, 0))],
        scratch_shapes=[pltpu.VMEM((t, LANES), F32), pltpu.VMEM((t, LANES), F32)],
        compiler_params=_params(("parallel", "arbitrary")))(qkv, qkv, qkv, o, do, lse, cumr)


def _adamw(name, parts, w, m, v):
    r, c = w.shape
    tr = 128 if r % 128 == 0 else r
    tc = 256 if tr > 128 and c % 256 == 0 else c
    c1 = 1.0 - ADAM_B1 ** ADAM_STEP
    c2 = 1.0 - ADAM_B2 ** ADAM_STEP

    def body(p_ref, w_ref, m_ref, v_ref, g_ref, d_ref, m2_ref, v2_ref):
        g = p_ref[0].astype(F32)
        for s in range(1, N_DEV):
            g = g + p_ref[s].astype(F32)
        m2 = ADAM_B1 * m_ref[...] + (1.0 - ADAM_B1) * g
        v2 = ADAM_B2 * v_ref[...] + (1.0 - ADAM_B2) * (g * g)
        g_ref[...] = g
        m2_ref[...] = m2
        v2_ref[...] = v2
        d_ref[...] = -ADAM_LR * ((m2 / c1) / (jnp.sqrt(v2 / c2) + ADAM_EPS) + ADAM_WD * w_ref[...])

    blk = pl.BlockSpec((tr, tc), lambda i, j: (i, j))
    shp = jax.ShapeDtypeStruct((r, c), F32)
    return pl.pallas_call(
        body, name=name, out_shape=[shp] * 4, grid=(r // tr, c // tc),
        in_specs=[pl.BlockSpec((N_DEV, tr, tc), lambda i, j: (0, i, j)), blk, blk, blk], out_specs=[blk] * 4,
        compiler_params=_params(("parallel", "parallel")))(parts, w, m, v)


def _cat_small(vals):
    parts = []
    for name in SMALL:
        v = vals[name].reshape(1, -1).astype(F32)
        parts.append(jnp.pad(v, ((0, 0), (0, SMALL_W[name] - v.shape[1]))))
    return jnp.concatenate(parts, axis=1)


def _split_small(row, shapes):
    out, off = {}, 0
    for name in SMALL:
        out[name] = row[0, off:off + SMALL_N[name]].reshape(shapes[name])
        off += SMALL_W[name]
    return out


def _cols_from_shards(g):
    return jnp.transpose(g, (1, 0, 2)).reshape(g.shape[1], N_DEV * g.shape[2])


def _shards_from_cols(a):
    r, c = a.shape
    return jnp.transpose(a.reshape(r, N_DEV, c // N_DEV), (1, 0, 2))


def kernel(x, meta_tokens, norm_mix_gain, w_in, b_forget, w_attn_out, b_glu, conv_dw_w, conv_dw_b, conv_ln_gain, conv_ln_bias, w_conv_out, b_conv_out, w_out, norm_mlp_gain, w_mlp_up, w_mlp_down, final_norm_gain, loss_target, m_meta_tokens, m_norm_mix_gain, m_w_in, m_b_forget, m_w_attn_out, m_b_glu, m_conv_dw_w, m_conv_dw_b, m_conv_ln_gain, m_conv_ln_bias, m_w_conv_out, m_b_conv_out, m_w_out, m_norm_mlp_gain, m_w_mlp_up, m_w_mlp_down, m_final_norm_gain, v_meta_tokens, v_norm_mix_gain, v_w_in, v_b_forget, v_w_attn_out, v_b_glu, v_conv_dw_w, v_conv_dw_b, v_conv_ln_gain, v_conv_ln_bias, v_w_conv_out, v_b_conv_out, v_w_out, v_norm_mlp_gain, v_w_mlp_up, v_w_mlp_down, v_final_norm_gain):
    weights = dict(meta_tokens=meta_tokens, norm_mix_gain=norm_mix_gain, w_in=w_in, b_forget=b_forget, w_attn_out=w_attn_out, b_glu=b_glu, conv_dw_w=conv_dw_w, conv_dw_b=conv_dw_b, conv_ln_gain=conv_ln_gain, conv_ln_bias=conv_ln_bias, w_conv_out=w_conv_out, b_conv_out=b_conv_out, w_out=w_out, norm_mlp_gain=norm_mlp_gain, w_mlp_up=w_mlp_up, w_mlp_down=w_mlp_down, final_norm_gain=final_norm_gain)
    mom_m = dict(meta_tokens=m_meta_tokens, norm_mix_gain=m_norm_mix_gain, w_in=m_w_in, b_forget=m_b_forget, w_attn_out=m_w_attn_out, b_glu=m_b_glu, conv_dw_w=m_conv_dw_w, conv_dw_b=m_conv_dw_b, conv_ln_gain=m_conv_ln_gain, conv_ln_bias=m_conv_ln_bias, w_conv_out=m_w_conv_out, b_conv_out=m_b_conv_out, w_out=m_w_out, norm_mlp_gain=m_norm_mlp_gain, w_mlp_up=m_w_mlp_up, w_mlp_down=m_w_mlp_down, final_norm_gain=m_final_norm_gain)
    mom_v = dict(meta_tokens=v_meta_tokens, norm_mix_gain=v_norm_mix_gain, w_in=v_w_in, b_forget=v_b_forget, w_attn_out=v_w_attn_out, b_glu=v_b_glu, conv_dw_w=v_conv_dw_w, conv_dw_b=v_conv_dw_b, conv_ln_gain=v_conv_ln_gain, conv_ln_bias=v_conv_ln_bias, w_conv_out=v_w_conv_out, b_conv_out=v_b_conv_out, w_out=v_w_out, norm_mlp_gain=v_norm_mlp_gain, w_mlp_up=v_w_mlp_up, w_mlp_down=v_w_mlp_down, final_norm_gain=v_final_norm_gain)
    names = list(weights)
    batch, seq, d = x.shape
    t = seq + N_META
    n = batch * t
    nq = t // ROW_TILE
    n_pairs = d // LANES
    assert t % ROW_TILE == 0 and d == SEG

    to_rows = lambda w3: jnp.transpose(w3[0])
    w_in_t, m_in_t, v_in_t = to_rows(w_in), to_rows(m_w_in), to_rows(v_w_in)
    first = [w_in_t.astype(BF16), meta_tokens, conv_dw_w[0]]
    rest = [w_[0].astype(BF16) for w_ in (w_attn_out, w_conv_out, w_out, w_mlp_up, w_mlp_down)]
    tgt = jnp.concatenate([jnp.zeros((batch, N_META, d), F32), loss_target], axis=1).reshape(n, d)
    gather_a = _exchange_start("gather_in_start", [(f_, False) for f_ in first], ks=CHIP_PEERS)
    level_1 = _exchange_wait("gather_in_wait", gather_a, [gather_a["token"], tgt, w_in_t, m_in_t, v_in_t] + rest,
                             fill_own=False)
    passed = _pass_on_start("gather_in_pass_start", level_1)
    w_in_g, meta_g, w_dw_g = _pass_on_wait("gather_in_pass_wait", passed, passed["token"], first)
    gather_b = _exchange_start("gather_rest_start", [(r_, False) for r_ in rest])
    n_fg = b_forget.shape[1]
    shard_w = w_in.shape[2]
    wt = w_in_g.reshape(N_DEV * shard_w, d)
    o_fg = 3 * SEG
    seg_rows = [0, SEG, 2 * SEG] + [o_fg + n_fg + i * SEG for i in range(4)]
    d_ff = w_mlp_down.shape[1] * N_DEV
    ff_blk = d_ff // N_DEV
    meta_f = _cols_from_shards(meta_g)
    w_dw = _cols_from_shards(w_dw_g)

    row2 = lambda v: v.reshape(1, -1)
    g1, g2, g3 = row2(norm_mix_gain) + gather_b["token"][0:1, 0:1], row2(norm_mlp_gain), row2(final_norm_gain)
    b_fg = jnp.pad(b_forget, ((0, 0), (0, FG_PAD - n_fg)))
    h0 = jnp.concatenate([jnp.broadcast_to(meta_f[None], (batch, N_META, d)), x], axis=1).reshape(n, d)

    hn1 = _rms_fwd("rms1", h0, g1)
    proj = lambda name, off, width, tn, dt: _mm_nt(name, [(hn1, _a_rows(d), wt, _wt_rows(tn, off))], n, width, tn, dt)
    qkv = proj("proj_qkv", 0, 3 * SEG, SEG, BF16)
    glu = proj("proj_glu", seg_rows[3], 2 * SEG, SEG, BF16)
    gates = proj("proj_gates", seg_rows[5], 2 * SEG, SEG, BF16)
    fg = proj("proj_fg", o_fg, FG_PAD, FG_PAD, F32)

    cum = _fox_prep_fwd("fox_cumsum", fg, b_fg, batch)
    cum_h = cum.reshape(batch, t, FG_PAD)[:, :, :2 * n_pairs].reshape(batch, t, n_pairs, 2)
    cumr = jnp.transpose(cum_h.reshape(batch, nq, ROW_TILE, n_pairs, 2), (0, 3, 1, 4, 2))
    cumr = jnp.pad(cumr, ((0, 0), (0, 0), (0, 0), (0, 6), (0, 0)))
    o, lse = _attn_fwd("attn_fwd", qkv, cumr, batch)
    rest = _exchange_wait("gather_rest_wait", gather_b, o)
    w_ao, w_co, w_o = [r_.reshape(d, d) for r_ in rest[:3]]
    w_up = rest[3]
    w_dn = rest[4].reshape(d_ff, d)
    a = _mm_nn("attn_out", o, w_ao, _w_cols(d, d, 0), d, d, BF16)

    c1 = _glu_conv_fwd("glu_conv", glu, b_glu, w_dw, conv_dw_b, batch)
    c3 = _ln_silu_fwd("ln_silu", c1, conv_ln_gain, conv_ln_bias)
    c = _mm_nn("conv_out", c3, w_co, _w_cols(d, d, 0), d, d, BF16)

    mrg = _merge_fwd("merge", gates, a, c, b_conv_out)
    mo = _mm_nn("mix_out", mrg, w_o, _w_cols(d, d, 0), d, d, F32)
    h1, hn2 = _rms_fwd("resid_rms2", h0, g2, res=mo)
    per = ff_blk // 512
    up, act = _mm_nn("mlp_up", hn2, w_up, pl.BlockSpec((None, d, 512), lambda i, j: (j // per, 0, j % per)),
                     d_ff, 512, BF16, relu2=True)
    dn = _mm_nn("mlp_down", act, w_dn, _w_cols(d_ff, d, 0), d, d, F32, tm=ROW_TILE)
    dh2, dh2b, loss_blk, dg3 = _final("final_loss", h1, dn, tgt, g3, batch)

    dup = _mm_nt("d_mlp_down", [(dh2b, _a_rows(d), w_dn, _w_rows(d, d))], n, d_ff, d, BF16, relu_bwd_of=up)
    dw_dn = _grad_w("gw_mlp_down", act, dh2b)
    dhn2 = _mm_nt("d_mlp_up", [(dup, _a_rows(ff_blk, g), w_up, pl.BlockSpec((None, 512, ff_blk), lambda i, j, g=g: (g, j, 0)))
                               for g in range(N_DEV)], n, d, 512, F32)
    dw_up = _mm_tn("gw_mlp_up", hn2, lambda a_: 0, d, dup, lambda b_: b_, ff_blk, (N_DEV, d, ff_blk),
                   pl.BlockSpec((None, d, ff_blk), lambda a_, b_: (b_, 0, 0)), (1, N_DEV))
    scatter_1 = _exchange_start("scatter_mlp_start", [(dw_dn.reshape(N_DEV, ff_blk, d), True), (dw_up, True)])
    dh1, dg2, dh1b = _rms_bwd("rms2_bwd", dhn2, h1, g2 + scatter_1["token"][0:1, 0:1], dh2, batch, with_bf16=True)

    dm = _mm_nt("d_mix_out", [(dh1b, _a_rows(d), w_o, _w_rows(d, d))], n, d, d, F32)
    dw_o = _grad_w("gw_mix_out", mrg, dh1b)
    da, dc, dga, dgc, dbco = _merge_bwd("merge_bwd", dm, gates, a, c, b_conv_out)

    do = _mm_nt("d_attn_out", [(da, _a_rows(d), w_ao, _w_rows(d, d))], n, d, d, BF16)
    dw_ao = _grad_w("gw_attn_out", o, da)
    dc3 = _mm_nt("d_conv_out", [(dc, _a_rows(d), w_co, _w_rows(d, d))], n, d, d, F32)
    dw_co = _grad_w("gw_conv_out", c3, dc)

    scatter_2 = _exchange_start("scatter_mix_start", [(dw_.reshape(N_DEV, d // N_DEV, d), True)
                                                      for dw_ in (dw_o, dw_ao, dw_co)])
    dc1, dg_ln, db_ln = _ln_silu_bwd("ln_silu_bwd", dc3, c1, conv_ln_gain + scatter_2["token"][0:1, 0:1],
                                     conv_ln_bias)
    dglu_a, dglu_g, dw_dw, db_dw, dbg_a, dbg_g = _glu_conv_bwd("glu_conv_bwd", dc1, glu, b_glu, w_dw, batch)

    dq, dk, dv, dcumr, dcum_q = _attn_bwd("attn_bwd", qkv, o, do, lse, cumr, batch)
    dcum_k = jnp.transpose(dcumr[:, :, :, :2, :], (0, 2, 4, 1, 3)).reshape(n, 2 * n_pairs)
    dcum_k = jnp.pad(dcum_k, ((0, 0), (0, FG_PAD - 2 * n_pairs)))
    dfg, db_fg = _fox_prep_bwd("fox_cumsum_bwd", dcum_k, dcum_q, fg, b_fg, batch)

    segs = [dq, dk, dv, dglu_a, dglu_g, dga, dgc]
    gw_t = [_grad_w("gw_in_%d" % i, s_, hn1) for i, s_ in enumerate(segs)]
    gw_fg = _grad_w("gw_in_fg", dfg, hn1)[:n_fg]
    dw_in_t = jnp.concatenate(gw_t[:3] + [gw_fg] + gw_t[3:], axis=0).reshape(N_DEV, shard_w, d)
    scatter_3 = _exchange_start("scatter_in_start", [(dw_in_t, True)])
    pairs = [(s_, _a_rows(SEG, 0, ROW_TILE), wt, _wt_block(SEG, seg_rows[i], 512), "nn") for i, s_ in enumerate(segs)]
    pairs.append((dfg, _a_rows(FG_PAD, 0, ROW_TILE), wt, _wt_block(FG_PAD, o_fg, 512), "nn"))
    dhn1 = _mm_nt("d_proj_in", pairs, n, d, 512, F32, tm=ROW_TILE, after=scatter_3["token"])
    dh0, dg1, dmeta = _rms_bwd("rms1_bwd", dhn1, h0, g1, dh1, batch, with_meta=True)
    grad_x = dh0.reshape(batch, t, d)[:, N_META:, :]

    small_g = dict(norm_mix_gain=dg1, b_forget=db_fg[:, :n_fg], b_glu=jnp.concatenate([dbg_a, dbg_g], axis=1),
                   conv_dw_b=db_dw, conv_ln_gain=dg_ln, conv_ln_bias=db_ln, b_conv_out=dbco, norm_mlp_gain=dg2,
                   final_norm_gain=dg3)
    scatter_4 = _exchange_start("scatter_small_start", [
        (_shards_from_cols(dmeta), True), (_shards_from_cols(dw_dw), True), (_cat_small(small_g), False),
        (loss_blk[0:1, :], False)])

    grads, deltas, new_m, new_v = {}, {}, {}, {}

    def update(k, parts):
        shp = weights[k].shape
        if k == "w_in":
            res_ = _adamw("adamw_" + k, parts, w_in_t, m_in_t, v_in_t)
            res_ = [jnp.transpose(r) for r in res_]
        else:
            w2 = lambda arr: arr.reshape(parts.shape[1:])
            res_ = _adamw("adamw_" + k, parts, w2(weights[k]), w2(mom_m[k]), w2(mom_v[k]))
        grads[k], deltas[k], new_m[k], new_v[k] = [r.reshape(shp) for r in res_]

    for k, parts in zip(("w_mlp_down", "w_mlp_up"), _exchange_wait("scatter_mlp_wait", scatter_1, scatter_4["token"])):
        update(k, parts)
    for k, parts in zip(("w_out", "w_attn_out", "w_conv_out"),
                        _exchange_wait("scatter_mix_wait", scatter_2, deltas["w_mlp_up"])):
        update(k, parts)
    update("w_in", _exchange_wait("scatter_in_wait", scatter_3, deltas["w_conv_out"])[0])
    reduced = _exchange_wait("scatter_small_wait", scatter_4, deltas["w_in"])
    loss = jnp.sum(reduced.pop()[:, 0, 0])
    for k, parts in zip(("meta_tokens", "conv_dw_w"), reduced[:-1]):
        update(k, parts)
    res = _adamw("adamw_small", reduced[-1], _cat_small(weights), _cat_small(mom_m), _cat_small(mom_v))
    shapes = {k: weights[k].shape for k in SMALL}
    for dst, r in zip((grads, deltas, new_m, new_v), res):
        dst.update(_split_small(r, shapes))

    return (loss, grad_x, *[grads[k] for k in names], *[deltas[k] for k in names],
            *[new_m[k] for k in names], *[new_v[k] for k in names])
```

```python
import functools

import jax
import jax.numpy as jnp
from jax import lax
from jax.experimental import pallas as pl
from jax.experimental.pallas import tpu as pltpu

F32, BF16 = jnp.float32, jnp.bfloat16
N_DEV = 8
N_META = 16
HEAD_DIM = 64
LANES = 128
CONV_W = 31
RMS_EPS = 1e-6
LN_EPS = 1e-5
ROW_TILE = 688
MM_TM = 2 * ROW_TILE
SEG = 1024
FG_PAD = 128
VMEM_LIMIT = 56 * 1024 * 1024
ADAM_LR, ADAM_B1, ADAM_B2, ADAM_EPS, ADAM_WD, ADAM_STEP = 0.001, 0.9, 0.999, 1e-08, 0.01, 10
NEG = -1e30

SMALL = ("norm_mix_gain", "b_forget", "b_glu", "conv_dw_b", "conv_ln_gain", "conv_ln_bias", "b_conv_out",
         "norm_mlp_gain", "final_norm_gain")
SMALL_W = {"norm_mix_gain": 1024, "b_forget": 128, "b_glu": 2048, "conv_dw_b": 1024, "conv_ln_gain": 1024,
           "conv_ln_bias": 1024, "b_conv_out": 1024, "norm_mlp_gain": 1024, "final_norm_gain": 1024}
SMALL_N = {"norm_mix_gain": 1024, "b_forget": 16, "b_glu": 2048, "conv_dw_b": 1024, "conv_ln_gain": 1024,
           "conv_ln_bias": 1024, "b_conv_out": 1024, "norm_mlp_gain": 1024, "final_norm_gain": 1024}


def _params(sem=None):
    return pltpu.CompilerParams(dimension_semantics=sem, vmem_limit_bytes=VMEM_LIMIT)


def _sigmoid(x):
    return 1.0 / (1.0 + jnp.exp(-x))


def _dot_nt(a, b):
    return lax.dot_general(a, b, (((1,), (1,)), ((), ())), preferred_element_type=F32)


def _dot_tn(a, b):
    return lax.dot_general(a, b, (((0,), (0,)), ((), ())), preferred_element_type=F32)


HBM_SPEC = pl.BlockSpec(memory_space=pltpu.HBM)
SEM_SPEC = pl.BlockSpec(memory_space=pltpu.SEMAPHORE)
DATAFLOW = pltpu.SideEffectType.DATAFLOW_SIDE_EFFECTING


def _device_index():
    return 4 * lax.axis_index("x") + 2 * lax.axis_index("y") + lax.axis_index("c")


def _peers():
    x, y, c = lax.axis_index("x"), lax.axis_index("y"), lax.axis_index("c")
    out = []
    for k in range(1, N_DEV):
        px = 1 - x if k & 4 else x
        py = 1 - y if k & 2 else y
        pc = 1 - c if k & 1 else c
        out.append((k, (px, py, pc), 4 * px + 2 * py + pc))
    return out


def _peer_copy(per_dest, src_ref, land_ref, send_sems, recv_sems, a, k, dev, peer):
    src = src_ref.at[peer] if per_dest else src_ref
    return pltpu.make_async_remote_copy(
        src_ref=src, dst_ref=land_ref.at[_device_index()], send_sem=send_sems.at[a * (N_DEV - 1) + k - 1],
        recv_sem=recv_sems.at[a * (N_DEV - 1) + k - 1], device_id=dev, device_id_type=pl.DeviceIdType.MESH)


ALL_PEERS = tuple(range(1, N_DEV))
CHIP_PEERS = (1, 2, 4, 6)
FAR_PEERS = (2, 4, 6)


def _exchange_start(name, items, ks=ALL_PEERS):
    n = len(items)
    per_dest = [it[1] for it in items]

    def body(*refs):
        srcs, lands = refs[:n], refs[n:2 * n]
        send_sems, recv_sems, token = refs[2 * n], refs[2 * n + 1], refs[-1]
        for a in range(n):
            for k, dev, peer in _peers():
                if k in ks:
                    _peer_copy(per_dest[a], srcs[a], lands[a], send_sems, recv_sems, a, k, dev, peer).start()
        token[...] = jnp.zeros(token.shape, F32)

    srcs = [pltpu.with_memory_space_constraint(it[0], pltpu.HBM) for it in items]
    lands = []
    for arr, pd in items:
        shp = arr.shape if pd else (N_DEV,) + arr.shape
        lands.append(pltpu.with_memory_space_constraint(lax.empty(shp, arr.dtype), pltpu.HBM))
    sems = pltpu.SemaphoreType.DMA((n * (N_DEV - 1),))
    res = pl.pallas_call(
        body, name=name,
        out_shape=(sems, sems, *[pltpu.HBM(a_.shape, a_.dtype) for a_ in srcs + lands],
                   jax.ShapeDtypeStruct((8, 128), F32)),
        in_specs=[HBM_SPEC] * (2 * n),
        out_specs=(SEM_SPEC, SEM_SPEC, *[HBM_SPEC] * (2 * n), pl.BlockSpec(memory_space=pltpu.VMEM)),
        input_output_aliases={i: 2 + i for i in range(2 * n)},
        compiler_params=pltpu.CompilerParams(has_side_effects=DATAFLOW),
    )(*srcs, *lands)
    return dict(per_dest=per_dest, ks=ks, send=res[0], recv=res[1], srcs=list(res[2:2 + n]),
                lands=list(res[2 + n:2 + 2 * n]), token=res[-1])


def _fill_own(per_dest, srcs, lands):
    me = _device_index()
    out = []
    for pd, src, land in zip(per_dest, srcs, lands):
        own = lax.dynamic_index_in_dim(src, me, 0, keepdims=True) if pd else src[None]
        out.append(lax.dynamic_update_slice_in_dim(land, own, me, axis=0))
    return out


def _exchange_wait(name, started, after, fill_own=True):
    per_dest = started["per_dest"]
    n = len(per_dest)

    def body(*refs):
        srcs, lands = refs[:n], refs[n:2 * n]
        send_sems, recv_sems = refs[2 * n], refs[2 * n + 1]
        for a in range(n):
            for k, dev, peer in _peers():
                if k in started["ks"]:
                    cp = _peer_copy(per_dest[a], srcs[a], lands[a], send_sems, recv_sems, a, k, dev, peer)
                    cp.wait_send()
                    cp.wait_recv()

    bufs = started["srcs"] + started["lands"]
    after = list(after) if isinstance(after, (list, tuple)) else [after]
    res = pl.pallas_call(
        body, name=name, out_shape=tuple(pltpu.HBM(b_.shape, b_.dtype) for b_ in bufs),
        in_specs=[HBM_SPEC] * (2 * n) + [SEM_SPEC, SEM_SPEC] + [pl.BlockSpec(memory_space=pl.ANY)] * len(after),
        out_specs=tuple([HBM_SPEC] * (2 * n)), input_output_aliases={i: i for i in range(2 * n)},
        compiler_params=pltpu.CompilerParams(has_side_effects=DATAFLOW),
    )(*bufs, started["send"], started["recv"], *after)
    return _fill_own(per_dest, res[:n], res[n:]) if fill_own else list(res[n:])


def _pass_on_copy(land_ref, send_sems, recv_sems, a, idx, slot):
    sibling = (lax.axis_index("x"), lax.axis_index("y"), 1 - lax.axis_index("c"))
    return pltpu.make_async_remote_copy(
        src_ref=land_ref.at[slot], dst_ref=land_ref.at[slot], send_sem=send_sems.at[a * len(FAR_PEERS) + idx],
        recv_sem=recv_sems.at[a * len(FAR_PEERS) + idx], device_id=sibling, device_id_type=pl.DeviceIdType.MESH)


def _pass_on_start(name, lands):
    n = len(lands)

    def body(*refs):
        send_sems, recv_sems, token = refs[n], refs[n + 1], refs[-1]
        slots = {k: peer for k, _, peer in _peers()}
        for a in range(n):
            for idx, k in enumerate(FAR_PEERS):
                _pass_on_copy(refs[a], send_sems, recv_sems, a, idx, slots[k]).start()
        token[...] = jnp.zeros(token.shape, F32)

    lands = [pltpu.with_memory_space_constraint(l_, pltpu.HBM) for l_ in lands]
    sems = pltpu.SemaphoreType.DMA((n * len(FAR_PEERS),))
    res = pl.pallas_call(
        body, name=name,
        out_shape=(sems, sems, *[pltpu.HBM(l_.shape, l_.dtype) for l_ in lands], jax.ShapeDtypeStruct((8, 128), F32)),
        in_specs=[HBM_SPEC] * n, out_specs=(SEM_SPEC, SEM_SPEC, *[HBM_SPEC] * n, pl.BlockSpec(memory_space=pltpu.VMEM)),
        input_output_aliases={i: 2 + i for i in range(n)},
        compiler_params=pltpu.CompilerParams(has_side_effects=DATAFLOW),
    )(*lands)
    return dict(send=res[0], recv=res[1], lands=list(res[2:2 + n]), token=res[-1])


def _pass_on_wait(name, passed, after, owns):
    n = len(passed["lands"])

    def body(*refs):
        send_sems, recv_sems = refs[n], refs[n + 1]
        slots = {k: peer for k, _, peer in _peers()}
        for a in range(n):
            for idx, k in enumerate(FAR_PEERS):
                _pass_on_copy(refs[a], send_sems, recv_sems, a, idx, slots[k]).wait_send()
                _pass_on_copy(refs[a], send_sems, recv_sems, a, idx, slots[k ^ 1]).wait_recv()

    res = pl.pallas_call(
        body, name=name, out_shape=tuple(pltpu.HBM(l_.shape, l_.dtype) for l_ in passed["lands"]),
        in_specs=[HBM_SPEC] * n + [SEM_SPEC, SEM_SPEC, pl.BlockSpec(memory_space=pl.ANY)],
        out_specs=tuple([HBM_SPEC] * n), input_output_aliases={i: i for i in range(n)},
        compiler_params=pltpu.CompilerParams(has_side_effects=DATAFLOW),
    )(*passed["lands"], passed["send"], passed["recv"], after)
    return _fill_own([False] * n, owns, res)


def _mm_nn(name, x, w, w_spec, n_out, tn, out_dtype, relu2=False, tm=MM_TM):
    m, k = x.shape

    def body(x_ref, w_ref, *outs):
        acc = jnp.dot(x_ref[...], w_ref[...], preferred_element_type=F32)
        outs[0][...] = acc.astype(outs[0].dtype)
        if relu2:
            r = jnp.maximum(acc, 0.0)
            outs[1][...] = (r * r).astype(outs[1].dtype)

    o_spec = pl.BlockSpec((tm, tn), lambda i, j: (i, j))
    shapes = [jax.ShapeDtypeStruct((m, n_out), out_dtype)]
    if relu2:
        shapes.append(jax.ShapeDtypeStruct((m, n_out), BF16))
    res = pl.pallas_call(
        body, name=name, out_shape=shapes, grid=(m // tm, n_out // tn),
        in_specs=[pl.BlockSpec((tm, k), lambda i, j: (i, 0)), w_spec],
        out_specs=[o_spec] * len(shapes), compiler_params=_params(("parallel", "parallel")),
    )(x, w)
    return res if relu2 else res[0]


def _mm_nt(name, pairs, m, n_out, tn, out_dtype, relu_bwd_of=None, tm=MM_TM, after=None):
    np_ = len(pairs)

    def body(*refs):
        acc = None
        for p in range(np_):
            if len(pairs[p]) == 5:
                d = jnp.dot(refs[2 * p][...], refs[2 * p + 1][...], preferred_element_type=F32)
            else:
                d = _dot_nt(refs[2 * p][...], refs[2 * p + 1][...])
            acc = d if acc is None else acc + d
        if relu_bwd_of is not None:
            acc = acc * (2.0 * jnp.maximum(refs[2 * np_][...].astype(F32), 0.0))
        refs[-1][...] = acc.astype(refs[-1].dtype)

    o_spec = pl.BlockSpec((tm, tn), lambda i, j: (i, j))
    operands, specs = [], []
    for pair in pairs:
        operands += [pair[0], pair[2]]
        specs += [pair[1], pair[3]]
    if relu_bwd_of is not None:
        operands.append(relu_bwd_of)
        specs.append(o_spec)
    if after is not None:
        operands.append(after)
        specs.append(pl.BlockSpec((8, 128), lambda i, j: (0, 0)))
    return pl.pallas_call(
        body, name=name, out_shape=jax.ShapeDtypeStruct((m, n_out), out_dtype), grid=(m // tm, n_out // tn),
        in_specs=specs, out_specs=o_spec, compiler_params=_params(("parallel", "parallel")),
    )(*operands)


def _mm_tn(name, x, x_col, ta, dy, dy_col, tb, out_shape, out_spec, grid_ab):
    m = x.shape[0]

    def body(x_ref, dy_ref, o_ref):
        o_ref[...] = _dot_tn(x_ref[...], dy_ref[...]).astype(BF16)

    return pl.pallas_call(
        body, name=name, out_shape=jax.ShapeDtypeStruct(out_shape, BF16), grid=grid_ab,
        in_specs=[pl.BlockSpec((m, ta), lambda a, b: (0, x_col(a))),
                  pl.BlockSpec((m, tb), lambda a, b: (0, dy_col(b)))],
        out_specs=out_spec, compiler_params=_params(("parallel", "parallel")),
    )(x, dy)


def _w_cols(k, tn, off_blocks):
    return pl.BlockSpec((k, tn), lambda i, j: (0, off_blocks + j))


def _a_rows(kw, col_block=0, tm=MM_TM):
    return pl.BlockSpec((tm, kw), lambda i, j: (i, col_block))


def _w_rows(tn, kw, col_block=0):
    return pl.BlockSpec((tn, kw), lambda i, j: (j, col_block))


def _wt_rows(tn, off):
    return pl.BlockSpec((pl.Element(tn), pl.Element(SEG)), lambda i, j: (pl.multiple_of(off + tn * j, 16), 0))


def _wt_block(k, off, tn):
    return pl.BlockSpec((pl.Element(k), pl.Element(tn)), lambda i, j: (off, pl.multiple_of(tn * j, 128)))


def _grad_w(name, x, dy):
    na, nb = x.shape[1], dy.shape[1]
    ta, tb = min(na, 1024), min(nb, 512)
    return _mm_tn(name, x, lambda a: a, ta, dy, lambda b: b, tb, (na, nb),
                  pl.BlockSpec((ta, tb), lambda a, b: (a, b)), (na // ta, nb // tb))


def _row_spec(width):
    return pl.BlockSpec((ROW_TILE, width), lambda i: (i, 0))


def _vec_spec(width):
    return pl.BlockSpec((1, width), lambda i: (0, 0))


def _rms_fwd(name, h, g, res=None):
    n, d = h.shape

    def body(*refs):
        if res is None:
            h_ref, g_ref, hn_ref = refs
            hv = h_ref[...]
        else:
            h_ref, r_ref, g_ref, hs_ref, hn_ref = refs
            hv = h_ref[...] + r_ref[...]
            hs_ref[...] = hv
        r = lax.rsqrt(jnp.mean(hv * hv, axis=-1, keepdims=True) + RMS_EPS)
        hn_ref[...] = (hv * r * g_ref[...]).astype(BF16)

    ins = [h, g] if res is None else [h, res, g]
    in_specs = [_row_spec(d), _vec_spec(d)] if res is None else [_row_spec(d), _row_spec(d), _vec_spec(d)]
    hn_shape = jax.ShapeDtypeStruct((n, d), BF16)
    if res is None:
        out_shape, out_specs = hn_shape, _row_spec(d)
    else:
        out_shape, out_specs = [jax.ShapeDtypeStruct((n, d), F32), hn_shape], [_row_spec(d), _row_spec(d)]
    return pl.pallas_call(body, name=name, out_shape=out_shape, grid=(n // ROW_TILE,), in_specs=in_specs,
                          out_specs=out_specs, compiler_params=_params(("parallel",)))(*ins)


def _rms_bwd(name, dhn, h, g, dres, batch, with_bf16=False, with_meta=False):
    n, d = h.shape
    t = n // batch
    nt = t // ROW_TILE

    def body(dhn_ref, h_ref, g_ref, dres_ref, *outs):
        first = (pl.program_id(0) == 0) & (pl.program_id(1) == 0)
        hv = h_ref[...]
        r = lax.rsqrt(jnp.mean(hv * hv, axis=-1, keepdims=True) + RMS_EPS)
        nrm = hv * r
        dhn = dhn_ref[...].astype(F32)
        dn = dhn * g_ref[...]
        dh = dres_ref[...] + r * (dn - nrm * jnp.mean(dn * nrm, axis=-1, keepdims=True))
        outs[0][...] = dh
        dg_ref = outs[1]

        @pl.when(first)
        def _():
            dg_ref[...] = jnp.zeros(dg_ref.shape, F32)

        dg_ref[...] += jnp.sum(dhn * nrm, axis=0, keepdims=True)
        nxt = 2
        if with_bf16:
            outs[nxt][...] = dh.astype(BF16)
            nxt += 1
        if with_meta:
            meta_ref = outs[nxt]

            @pl.when(first)
            def _():
                meta_ref[...] = jnp.zeros(meta_ref.shape, F32)

            @pl.when(pl.program_id(1) == 0)
            def _():
                meta_ref[...] += dh[0:N_META, :]

    row = pl.BlockSpec((ROW_TILE, d), lambda b, j: (b * nt + j, 0))
    vec = pl.BlockSpec((1, d), lambda b, j: (0, 0))
    shapes = [jax.ShapeDtypeStruct((n, d), F32), jax.ShapeDtypeStruct((1, d), F32)]
    specs = [row, vec]
    if with_bf16:
        shapes.append(jax.ShapeDtypeStruct((n, d), BF16))
        specs.append(row)
    if with_meta:
        shapes.append(jax.ShapeDtypeStruct((N_META, d), F32))
        specs.append(pl.BlockSpec((N_META, d), lambda b, j: (0, 0)))
    return pl.pallas_call(body, name=name, out_shape=shapes, grid=(batch, nt), in_specs=[row, row, vec, row],
                          out_specs=specs, compiler_params=_params(("arbitrary", "arbitrary")))(dhn, h, g, dres)


def _final(name, h1, dn, tgt, g, batch):
    n, d = h1.shape
    t = n // batch
    nt = t // ROW_TILE

    def body(h1_ref, dn_ref, tgt_ref, g_ref, dh_ref, dhb_ref, loss_ref, dg_ref):
        first = (pl.program_id(0) == 0) & (pl.program_id(1) == 0)
        hv = h1_ref[...] + dn_ref[...]
        r = lax.rsqrt(jnp.mean(hv * hv, axis=-1, keepdims=True) + RMS_EPS)
        nrm = hv * r
        gv = g_ref[...]
        pos = pl.program_id(1) * ROW_TILE + lax.broadcasted_iota(jnp.int32, (ROW_TILE, 1), 0)
        diff = jnp.where(pos >= N_META, nrm * gv - tgt_ref[...], 0.0)
        dy = diff * (1.0 / d)

        @pl.when(first)
        def _():
            loss_ref[...] = jnp.zeros(loss_ref.shape, F32)
            dg_ref[...] = jnp.zeros(dg_ref.shape, F32)

        loss_ref[...] += jnp.full(loss_ref.shape, 0.5 / d, F32) * jnp.sum(diff * diff)
        dg_ref[...] += jnp.sum(dy * nrm, axis=0, keepdims=True)
        dng = dy * gv
        dh = r * (dng - nrm * jnp.mean(dng * nrm, axis=-1, keepdims=True))
        dh_ref[...] = dh
        dhb_ref[...] = dh.astype(BF16)

    row = pl.BlockSpec((ROW_TILE, d), lambda b, j: (b * nt + j, 0))
    vec = pl.BlockSpec((1, d), lambda b, j: (0, 0))
    return pl.pallas_call(
        body, name=name, grid=(batch, nt), in_specs=[row, row, row, vec],
        out_shape=[jax.ShapeDtypeStruct((n, d), F32), jax.ShapeDtypeStruct((n, d), BF16),
                   jax.ShapeDtypeStruct((8, 128), F32), jax.ShapeDtypeStruct((1, d), F32)],
        out_specs=[row, row, pl.BlockSpec((8, 128), lambda b, j: (0, 0)), vec],
        compiler_params=_params(("arbitrary", "arbitrary")))(h1, dn, tgt, g)


def _ln_silu_fwd(name, c1, g, b):
    n, d = c1.shape

    def body(c_ref, g_ref, b_ref, o_ref):
        xv = c_ref[...]
        xc = xv - jnp.mean(xv, axis=-1, keepdims=True)
        rstd = lax.rsqrt(jnp.mean(xc * xc, axis=-1, keepdims=True) + LN_EPS)
        c2 = xc * rstd * g_ref[...] + b_ref[...]
        o_ref[...] = (c2 * _sigmoid(c2)).astype(BF16)

    return pl.pallas_call(body, name=name, out_shape=jax.ShapeDtypeStruct((n, d), BF16), grid=(n // ROW_TILE,),
                          in_specs=[_row_spec(d), _vec_spec(d), _vec_spec(d)], out_specs=_row_spec(d),
                          compiler_params=_params(("parallel",)))(c1, g, b)


def _ln_silu_bwd(name, dc3, c1, g, b):
    n, d = c1.shape

    def body(d_ref, c_ref, g_ref, b_ref, dc1_ref, dg_ref, db_ref):
        xv = c_ref[...]
        xc = xv - jnp.mean(xv, axis=-1, keepdims=True)
        rstd = lax.rsqrt(jnp.mean(xc * xc, axis=-1, keepdims=True) + LN_EPS)
        xh = xc * rstd
        c2 = xh * g_ref[...] + b_ref[...]
        s = _sigmoid(c2)
        dc2 = d_ref[...].astype(F32) * (s * (1.0 + c2 * (1.0 - s)))

        @pl.when(pl.program_id(0) == 0)
        def _():
            dg_ref[...] = jnp.zeros(dg_ref.shape, F32)
            db_ref[...] = jnp.zeros(db_ref.shape, F32)

        dg_ref[...] += jnp.sum(dc2 * xh, axis=0, keepdims=True)
        db_ref[...] += jnp.sum(dc2, axis=0, keepdims=True)
        dxh = dc2 * g_ref[...]
        dc1_ref[...] = rstd * (dxh - jnp.mean(dxh, axis=-1, keepdims=True)
                               - xh * jnp.mean(dxh * xh, axis=-1, keepdims=True))

    return pl.pallas_call(
        body, name=name, grid=(n // ROW_TILE,),
        out_shape=[jax.ShapeDtypeStruct((n, d), F32), jax.ShapeDtypeStruct((1, d), F32),
                   jax.ShapeDtypeStruct((1, d), F32)],
        in_specs=[_row_spec(d), _row_spec(d), _vec_spec(d), _vec_spec(d)],
        out_specs=[_row_spec(d), _vec_spec(d), _vec_spec(d)],
        compiler_params=_params(("arbitrary",)))(dc3, c1, g, b)


MERGE_TC = 512


def _merge_fwd(name, gates, a, c, b_co):
    n, d = a.shape
    nc = d // MERGE_TC

    def body(ga_ref, gc_ref, a_ref, c_ref, b_ref, m_ref):
        f32 = lambda r_: r_[...].astype(F32)
        m = _sigmoid(f32(ga_ref)) * f32(a_ref) + _sigmoid(f32(gc_ref)) * (f32(c_ref) + b_ref[...])
        m_ref[...] = m.astype(BF16)

    blk = lambda off: pl.BlockSpec((ROW_TILE, MERGE_TC), lambda i, j: (i, off + j))
    return pl.pallas_call(
        body, name=name, out_shape=jax.ShapeDtypeStruct((n, d), BF16), grid=(n // ROW_TILE, nc),
        in_specs=[blk(0), blk(nc), blk(0), blk(0), pl.BlockSpec((1, MERGE_TC), lambda i, j: (0, j))],
        out_specs=blk(0), compiler_params=_params(("parallel", "parallel")))(gates, gates, a, c, b_co)


def _merge_bwd(name, dm, gates, a, c, b_co):
    n, d = a.shape
    nc = d // MERGE_TC

    def body(dm_ref, ga_ref, gc_ref, a_ref, c_ref, b_ref, da_ref, dc_ref, dga_ref, dgc_ref, dbco_ref):
        f32 = lambda r_: r_[...].astype(F32)
        dmv = f32(dm_ref)
        sa, sc = _sigmoid(f32(ga_ref)), _sigmoid(f32(gc_ref))
        dc = dmv * sc
        da_ref[...] = (dmv * sa).astype(BF16)
        dc_ref[...] = dc.astype(BF16)
        dga_ref[...] = (dmv * f32(a_ref) * sa * (1.0 - sa)).astype(BF16)
        dgc_ref[...] = (dmv * (f32(c_ref) + b_ref[...]) * sc * (1.0 - sc)).astype(BF16)

        @pl.when(pl.program_id(1) == 0)
        def _():
            dbco_ref[...] = jnp.zeros(dbco_ref.shape, F32)

        dbco_ref[...] += jnp.sum(dc, axis=0, keepdims=True)

    blk = lambda off: pl.BlockSpec((ROW_TILE, MERGE_TC), lambda j, i: (i, off + j))
    vec = pl.BlockSpec((1, MERGE_TC), lambda j, i: (0, j))
    act = jax.ShapeDtypeStruct((n, d), BF16)
    return pl.pallas_call(
        body, name=name, grid=(nc, n // ROW_TILE),
        out_shape=[act, act, act, act, jax.ShapeDtypeStruct((1, d), F32)],
        in_specs=[blk(0), blk(0), blk(nc), blk(0), blk(0), vec],
        out_specs=[blk(0), blk(0), blk(0), blk(0), vec],
        compiler_params=_params(("parallel", "arbitrary")))(dm, gates, gates, a, c, b_co)


CONV_TC = 128
CONV_HALO = 32


def _conv_chunk(t):
    return 48 if t % 48 == 0 else 32 if t % 32 == 0 else 16


def _fold8(x):
    out = x[0:8]
    for k in range(1, x.shape[0] // 8):
        out = out + x[8 * k:8 * k + 8]
    return out


def _glu_conv_fwd(name, glu, b_glu, w_dw, b_dw, batch):
    n, c2 = glu.shape
    c = c2 // 2
    t = n // batch
    nc = c // CONV_TC

    def body(a_ref, gt_ref, ba_ref, bg_ref, w_ref, bdw_ref, o_ref, pad_ref):
        u = (a_ref[...].astype(F32) + ba_ref[...]) * _sigmoid(gt_ref[...].astype(F32) + bg_ref[...])
        pad_ref[0:CONV_HALO, :] = jnp.zeros((CONV_HALO, CONV_TC), F32)
        pad_ref[CONV_HALO:CONV_HALO + t, :] = u
        ch = _conv_chunk(t)
        for r0 in range(0, t, ch):
            acc = jnp.zeros((ch, CONV_TC), F32) + bdw_ref[...]
            for j in range(CONV_W):
                off = r0 + CONV_HALO - (CONV_W - 1) + j
                acc = acc + w_ref[j:j + 1, :] * pad_ref[off:off + ch, :]
            o_ref[r0:r0 + ch, :] = acc

    seq = lambda off: pl.BlockSpec((t, CONV_TC), lambda b, j: (b, off + j))
    vec = lambda off: pl.BlockSpec((1, CONV_TC), lambda b, j: (0, off + j))
    return pl.pallas_call(
        body, name=name, out_shape=jax.ShapeDtypeStruct((n, c), F32), grid=(batch, nc),
        in_specs=[seq(0), seq(nc), vec(0), vec(nc), pl.BlockSpec((CONV_W, CONV_TC), lambda b, j: (0, j)), vec(0)],
        out_specs=seq(0), scratch_shapes=[pltpu.VMEM((t + CONV_HALO, CONV_TC), F32)],
        compiler_params=_params(("parallel", "parallel")))(glu, glu, b_glu, b_glu, w_dw, b_dw)


def _glu_conv_bwd(name, dc1, glu, b_glu, w_dw, batch):
    n, c2 = glu.shape
    c = c2 // 2
    t = n // batch
    nc = c // CONV_TC

    def body(d_ref, a_ref, gt_ref, ba_ref, bg_ref, w_ref, dga_ref, dgg_ref, dw_ref, dbdw_ref, dba_ref, dbg_ref,
             padu_ref, padd_ref):
        av = a_ref[...].astype(F32) + ba_ref[...]
        sg = _sigmoid(gt_ref[...].astype(F32) + bg_ref[...])
        dc = d_ref[...]
        padu_ref[0:CONV_HALO, :] = jnp.zeros((CONV_HALO, CONV_TC), F32)
        padu_ref[CONV_HALO:CONV_HALO + t, :] = av * sg
        padd_ref[0:t, :] = dc
        padd_ref[t:t + CONV_HALO, :] = jnp.zeros((CONV_HALO, CONV_TC), F32)

        @pl.when(pl.program_id(1) == 0)
        def _():
            dw_ref[...] = jnp.zeros(dw_ref.shape, F32)
            dbdw_ref[...] = jnp.zeros(dbdw_ref.shape, F32)
            dba_ref[...] = jnp.zeros(dba_ref.shape, F32)
            dbg_ref[...] = jnp.zeros(dbg_ref.shape, F32)

        ch = _conv_chunk(t)
        zero8 = jnp.zeros((8, CONV_TC), F32)
        dw_acc = [zero8] * CONV_W
        sum_dc, sum_a, sum_g = zero8, zero8, zero8
        for r0 in range(0, t, ch):
            dcc = d_ref[r0:r0 + ch, :]
            du = jnp.zeros((ch, CONV_TC), F32)
            for j in range(CONV_W):
                back = r0 + CONV_W - 1 - j
                du = du + w_ref[j:j + 1, :] * padd_ref[back:back + ch, :]
                off = r0 + CONV_HALO - (CONV_W - 1) + j
                dw_acc[j] = dw_acc[j] + _fold8(dcc * padu_ref[off:off + ch, :])
            sgc = _sigmoid(gt_ref[r0:r0 + ch, :].astype(F32) + bg_ref[...])
            dga = du * sgc
            dgg = du * padu_ref[CONV_HALO + r0:CONV_HALO + r0 + ch, :] * (1.0 - sgc)
            dga_ref[r0:r0 + ch, :] = dga.astype(BF16)
            dgg_ref[r0:r0 + ch, :] = dgg.astype(BF16)
            sum_dc, sum_a, sum_g = sum_dc + _fold8(dcc), sum_a + _fold8(dga), sum_g + _fold8(dgg)
        for j in range(CONV_W):
            dw_ref[j:j + 1, :] += jnp.sum(dw_acc[j], axis=0, keepdims=True)
        dbdw_ref[...] += jnp.sum(sum_dc, axis=0, keepdims=True)
        dba_ref[...] += jnp.sum(sum_a, axis=0, keepdims=True)
        dbg_ref[...] += jnp.sum(sum_g, axis=0, keepdims=True)

    seq = lambda off: pl.BlockSpec((t, CONV_TC), lambda j, b: (b, off + j))
    vec = lambda off: pl.BlockSpec((1, CONV_TC), lambda j, b: (0, off + j))
    wsp = pl.BlockSpec((CONV_W, CONV_TC), lambda j, b: (0, j))
    act = jax.ShapeDtypeStruct((n, c), BF16)
    v = jax.ShapeDtypeStruct((1, c), F32)
    return pl.pallas_call(
        body, name=name, grid=(nc, batch),
        out_shape=[act, act, jax.ShapeDtypeStruct((CONV_W, c), F32), v, v, v],
        in_specs=[seq(0), seq(0), seq(nc), vec(0), vec(nc), wsp],
        out_specs=[seq(0), seq(0), wsp, vec(0), vec(0), vec(0)],
        scratch_shapes=[pltpu.VMEM((t + CONV_HALO, CONV_TC), F32), pltpu.VMEM((t + CONV_HALO, CONV_TC), F32)],
        compiler_params=_params(("parallel", "arbitrary")))(dc1, glu, glu, b_glu, b_glu, w_dw)


def _split3(x):
    hi = x.astype(BF16)
    r = x - hi.astype(F32)
    mid = r.astype(BF16)
    lo = (r - mid.astype(F32)).astype(BF16)
    return hi, mid, lo


def _tri_matmul(tri, x):
    hi, mid, lo = _split3(x)
    dot = lambda v: jnp.dot(tri, v, preferred_element_type=F32)
    return dot(hi) + dot(mid) + dot(lo)


def _fox_prep_fwd(name, fg, b_fg, batch):
    n, w = fg.shape
    t = n // batch
    nq = t // ROW_TILE

    def body(fg_ref, b_ref, cum_ref):
        row = lax.broadcasted_iota(jnp.int32, (ROW_TILE, ROW_TILE), 0)
        col = lax.broadcasted_iota(jnp.int32, (ROW_TILE, ROW_TILE), 1)
        tri = (row >= col).astype(BF16)
        for k in range(nq):
            rows = slice(k * ROW_TILE, (k + 1) * ROW_TILE)
            z = fg_ref[rows, :] + b_ref[...]
            logf = jnp.minimum(z, 0.0) - jnp.log(1.0 + jnp.exp(-jnp.abs(z)))
            cum = _tri_matmul(tri, logf)
            if k > 0:
                cum = cum + cum_ref[k * ROW_TILE - 1:k * ROW_TILE, :]
            cum_ref[rows, :] = cum

    seq = pl.BlockSpec((t, w), lambda b: (b, 0))
    return pl.pallas_call(body, name=name, out_shape=jax.ShapeDtypeStruct((n, w), F32), grid=(batch,),
                          in_specs=[seq, pl.BlockSpec((1, w), lambda b: (0, 0))], out_specs=seq,
                          compiler_params=_params(("parallel",)))(fg, b_fg)


def _fox_prep_bwd(name, dcum_k, dcum_q, fg, b_fg, batch):
    n, w = fg.shape
    t = n // batch
    nq = t // ROW_TILE

    def body(dk_ref, dq_ref, fg_ref, b_ref, dfg_ref, db_ref, rev_ref):
        row = lax.broadcasted_iota(jnp.int32, (ROW_TILE, ROW_TILE), 0)
        col = lax.broadcasted_iota(jnp.int32, (ROW_TILE, ROW_TILE), 1)
        tri = (col >= row).astype(BF16)

        @pl.when(pl.program_id(0) == 0)
        def _():
            db_ref[...] = jnp.zeros(db_ref.shape, F32)

        for k in reversed(range(nq)):
            rows = slice(k * ROW_TILE, (k + 1) * ROW_TILE)
            dlog = _tri_matmul(tri, dk_ref[rows, :] + dq_ref[rows, :])
            if k < nq - 1:
                dlog = dlog + rev_ref[(k + 1) * ROW_TILE:(k + 1) * ROW_TILE + 1, :]
            rev_ref[rows, :] = dlog
            dfg = dlog * _sigmoid(-(fg_ref[rows, :] + b_ref[...]))
            dfg_ref[rows, :] = dfg.astype(BF16)
            db_ref[...] += jnp.sum(dfg, axis=0, keepdims=True)

    seq = pl.BlockSpec((t, w), lambda b: (b, 0))
    vec = pl.BlockSpec((1, w), lambda b: (0, 0))
    return pl.pallas_call(
        body, name=name, grid=(batch,),
        out_shape=[jax.ShapeDtypeStruct((n, w), BF16), jax.ShapeDtypeStruct((1, w), F32)],
        in_specs=[seq, seq, seq, vec], out_specs=[seq, vec], scratch_shapes=[pltpu.VMEM((t, w), F32)],
        compiler_params=_params(("arbitrary",)))(dcum_k, dcum_q, fg, b_fg)


def _head_masks(x):
    lane = lax.broadcasted_iota(jnp.int32, x.shape, 1)
    zero = jnp.zeros(x.shape, x.dtype)
    return jnp.where(lane < HEAD_DIM, x, zero), jnp.where(lane >= HEAD_DIM, x, zero)


ATTN_BLOCK = 512


def _attn_blocks(t):
    nb = max(t // ATTN_BLOCK, 1)
    blocks = [(i * ATTN_BLOCK, ATTN_BLOCK) for i in range(nb - 1)]
    return blocks + [((nb - 1) * ATTN_BLOCK, t - (nb - 1) * ATTN_BLOCK)]


def _attn_specs(t):
    blocks = _attn_blocks(t)
    width = max(sz for _, sz in blocks)
    qkv = lambda off: pl.BlockSpec((t, LANES), lambda b, h: (b, off + h))
    cumr = pl.BlockSpec((None, None, len(blocks), 8, width), lambda b, h: (b, h, 0, 0, 0))
    return qkv, cumr


def _key_sums_to_blocks(cum, batch, t, n_pairs):
    blocks = _attn_blocks(t)
    width = max(sz for _, sz in blocks)
    cum_h = cum.reshape(batch, t, -1)[:, :, :2 * n_pairs].reshape(batch, t, n_pairs, 2)
    rows = [jnp.pad(jnp.transpose(cum_h[:, s0:s0 + sz], (0, 2, 3, 1)), ((0, 0), (0, 0), (0, 6), (0, width - sz)))
            for s0, sz in blocks]
    return jnp.stack(rows, axis=2)


def _key_sums_from_blocks(dcumr, batch, t, n_pairs):
    cols = [jnp.transpose(dcumr[:, :, j, :2, :sz], (0, 3, 1, 2)) for j, (_, sz) in enumerate(_attn_blocks(t))]
    return jnp.concatenate(cols, axis=1).reshape(batch * t, 2 * n_pairs)


def _causal(size):
    row = lax.broadcasted_iota(jnp.int32, (size, size), 0)
    col = lax.broadcasted_iota(jnp.int32, (size, size), 1)
    return row >= col


def _attn_fwd(name, qkv, cumr, batch):
    n, w3 = qkv.shape
    w = w3 // 3
    t = n // batch
    n_pairs = w // LANES
    blocks = _attn_blocks(t)

    def body(q_ref, k_ref, v_ref, cr_ref, o_ref, lse_ref):
        for i, (q0, qn) in enumerate(blocks):
            rows = slice(q0, q0 + qn)
            qs = _head_masks(q_ref[rows, :] * 0.125)
            outs, lses = [], []
            for hh in range(2):
                m = jnp.full((qn, 1), NEG, F32)
                l = jnp.zeros((qn, 1), F32)
                acc = jnp.zeros((qn, LANES), F32)
                for j in range(i + 1):
                    k0, kn = blocks[j]
                    cols = slice(k0, k0 + kn)
                    s = _dot_nt(qs[hh], k_ref[cols, :]) - cr_ref[j, hh:hh + 1, 0:kn]
                    if j == i:
                        s = jnp.where(_causal(qn), s, NEG)
                    m_new = jnp.maximum(m, jnp.max(s, axis=1, keepdims=True))
                    alpha = jnp.exp(m - m_new)
                    p = jnp.exp(s - m_new)
                    l = alpha * l + jnp.sum(p, axis=1, keepdims=True)
                    acc = alpha * acc + jnp.dot(p.astype(BF16), v_ref[cols, :], preferred_element_type=F32)
                    m = m_new
                outs.append(acc / l)
                lses.append(m + jnp.log(l))
            lane = lax.broadcasted_iota(jnp.int32, (qn, LANES), 1)
            o_ref[rows, :] = jnp.where(lane < HEAD_DIM, outs[0], outs[1]).astype(BF16)
            lse_ref[rows, :] = jnp.where(lane < HEAD_DIM, lses[0], lses[1])

    qkv_spec, cumr_spec = _attn_specs(t)
    return pl.pallas_call(
        body, name=name, grid=(batch, n_pairs),
        out_shape=[jax.ShapeDtypeStruct((n, w), BF16), jax.ShapeDtypeStruct((n, w), F32)],
        in_specs=[qkv_spec(0), qkv_spec(n_pairs), qkv_spec(2 * n_pairs), cumr_spec],
        out_specs=[qkv_spec(0), qkv_spec(0)],
        compiler_params=_params(("parallel", "parallel")))(qkv, qkv, qkv, cumr)


def _attn_bwd(name, qkv, o, do, lse, cumr, batch):
    n, w3 = qkv.shape
    w = w3 // 3
    t = n // batch
    n_pairs = w // LANES
    blocks = _attn_blocks(t)

    def body(q_ref, k_ref, v_ref, o_ref, do_ref, lse_ref, cr_ref, dq_ref, dk_ref, dv_ref, dcr_ref, dcq_ref,
             dk_acc, dv_acc):
        pair = pl.program_id(1)
        dk_acc[...] = jnp.zeros(dk_acc.shape, F32)
        dv_acc[...] = jnp.zeros(dv_acc.shape, F32)
        dcr_ref[...] = jnp.zeros(dcr_ref.shape, F32)

        @pl.when(pair == 0)
        def _():
            dcq_ref[...] = jnp.zeros(dcq_ref.shape, F32)

        for i, (q0, qn) in enumerate(blocks):
            rows = slice(q0, q0 + qn)
            qs = _head_masks(q_ref[rows, :] * 0.125)
            dos = _head_masks(do_ref[rows, :])
            dq = jnp.zeros((qn, LANES), F32)
            dcq = []
            for hh in range(2):
                row_sum = jnp.zeros((qn, 1), F32)
                lse = lse_ref[rows, hh * HEAD_DIM:hh * HEAD_DIM + 1]
                delta = jnp.sum(dos[hh].astype(F32) * o_ref[rows, :].astype(F32), axis=1, keepdims=True)
                for j in range(i + 1):
                    k0, kn = blocks[j]
                    cols = slice(k0, k0 + kn)
                    s = _dot_nt(qs[hh], k_ref[cols, :]) - cr_ref[j, hh:hh + 1, 0:kn]
                    p = jnp.exp(s - lse)
                    if j == i:
                        p = jnp.where(_causal(qn), p, 0.0)
                    dp = _dot_nt(dos[hh], v_ref[cols, :])
                    ds = p * (dp - delta)
                    pb, dsb = p.astype(BF16), ds.astype(BF16)
                    km = _head_masks(k_ref[cols, :])[hh]
                    dv_acc[cols, :] += _dot_tn(pb, dos[hh])
                    dk_acc[cols, :] += _dot_tn(dsb, qs[hh])
                    dq = dq + jnp.dot(dsb, km, preferred_element_type=F32)
                    dcr_ref[j, hh:hh + 1, 0:kn] -= jnp.sum(ds, axis=0, keepdims=True)
                    row_sum = row_sum + jnp.sum(ds, axis=1, keepdims=True)
                dcq.append(row_sum)
            lane = lax.broadcasted_iota(jnp.int32, (qn, LANES), 1)
            dq_ref[rows, :] = (dq * 0.125).astype(BF16)
            dcq_ref[rows, :] = jnp.where(lane == 2 * pair, dcq[0],
                                         jnp.where(lane == 2 * pair + 1, dcq[1], dcq_ref[rows, :]))
        dk_ref[...] = dk_acc[...].astype(BF16)
        dv_ref[...] = dv_acc[...].astype(BF16)

    qkv_spec, cumr_spec = _attn_specs(t)
    act = jax.ShapeDtypeStruct((n, w), BF16)
    return pl.pallas_call(
        body, name=name, grid=(batch, n_pairs),
        out_shape=[act, act, act, jax.ShapeDtypeStruct(cumr.shape, F32), jax.ShapeDtypeStruct((n, LANES), F32)],
        in_specs=[qkv_spec(0), qkv_spec(n_pairs), qkv_spec(2 * n_pairs), qkv_spec(0), qkv_spec(0), qkv_spec(0),
                  cumr_spec],
        out_specs=[qkv_spec(0), qkv_spec(0), qkv_spec(0), cumr_spec, pl.BlockSpec((t, LANES), lambda b, h: (b, 0))],
        scratch_shapes=[pltpu.VMEM((t, LANES), F32), pltpu.VMEM((t, LANES), F32)],
        compiler_params=_params(("parallel", "arbitrary")))(qkv, qkv, qkv, o, do, lse, cumr)


def _adamw(name, parts, w, m, v):
    r, c = w.shape
    tr = 128 if r % 128 == 0 else r
    tc = 256 if tr > 128 and c % 256 == 0 else c
    c1 = 1.0 - ADAM_B1 ** ADAM_STEP
    c2 = 1.0 - ADAM_B2 ** ADAM_STEP

    def body(p_ref, w_ref, m_ref, v_ref, g_ref, d_ref, m2_ref, v2_ref):
        g = p_ref[0].astype(F32)
        for s in range(1, N_DEV):
            g = g + p_ref[s].astype(F32)
        m2 = ADAM_B1 * m_ref[...] + (1.0 - ADAM_B1) * g
        v2 = ADAM_B2 * v_ref[...] + (1.0 - ADAM_B2) * (g * g)
        g_ref[...] = g
        m2_ref[...] = m2
        v2_ref[...] = v2
        d_ref[...] = -ADAM_LR * ((m2 / c1) / (jnp.sqrt(v2 / c2) + ADAM_EPS) + ADAM_WD * w_ref[...])

    blk = pl.BlockSpec((tr, tc), lambda i, j: (i, j))
    shp = jax.ShapeDtypeStruct((r, c), F32)
    return pl.pallas_call(
        body, name=name, out_shape=[shp] * 4, grid=(r // tr, c // tc),
        in_specs=[pl.BlockSpec((N_DEV, tr, tc), lambda i, j: (0, i, j)), blk, blk, blk], out_specs=[blk] * 4,
        compiler_params=_params(("parallel", "parallel")))(parts, w, m, v)


def _cat_small(vals):
    parts = []
    for name in SMALL:
        v = vals[name].reshape(1, -1).astype(F32)
        parts.append(jnp.pad(v, ((0, 0), (0, SMALL_W[name] - v.shape[1]))))
    return jnp.concatenate(parts, axis=1)


def _split_small(row, shapes):
    out, off = {}, 0
    for name in SMALL:
        out[name] = row[0, off:off + SMALL_N[name]].reshape(shapes[name])
        off += SMALL_W[name]
    return out


def _cols_from_shards(g):
    return jnp.transpose(g, (1, 0, 2)).reshape(g.shape[1], N_DEV * g.shape[2])


def _shards_from_cols(a):
    r, c = a.shape
    return jnp.transpose(a.reshape(r, N_DEV, c // N_DEV), (1, 0, 2))


def kernel(x, meta_tokens, norm_mix_gain, w_in, b_forget, w_attn_out, b_glu, conv_dw_w, conv_dw_b, conv_ln_gain, conv_ln_bias, w_conv_out, b_conv_out, w_out, norm_mlp_gain, w_mlp_up, w_mlp_down, final_norm_gain, loss_target, m_meta_tokens, m_norm_mix_gain, m_w_in, m_b_forget, m_w_attn_out, m_b_glu, m_conv_dw_w, m_conv_dw_b, m_conv_ln_gain, m_conv_ln_bias, m_w_conv_out, m_b_conv_out, m_w_out, m_norm_mlp_gain, m_w_mlp_up, m_w_mlp_down, m_final_norm_gain, v_meta_tokens, v_norm_mix_gain, v_w_in, v_b_forget, v_w_attn_out, v_b_glu, v_conv_dw_w, v_conv_dw_b, v_conv_ln_gain, v_conv_ln_bias, v_w_conv_out, v_b_conv_out, v_w_out, v_norm_mlp_gain, v_w_mlp_up, v_w_mlp_down, v_final_norm_gain):
    weights = dict(meta_tokens=meta_tokens, norm_mix_gain=norm_mix_gain, w_in=w_in, b_forget=b_forget, w_attn_out=w_attn_out, b_glu=b_glu, conv_dw_w=conv_dw_w, conv_dw_b=conv_dw_b, conv_ln_gain=conv_ln_gain, conv_ln_bias=conv_ln_bias, w_conv_out=w_conv_out, b_conv_out=b_conv_out, w_out=w_out, norm_mlp_gain=norm_mlp_gain, w_mlp_up=w_mlp_up, w_mlp_down=w_mlp_down, final_norm_gain=final_norm_gain)
    mom_m = dict(meta_tokens=m_meta_tokens, norm_mix_gain=m_norm_mix_gain, w_in=m_w_in, b_forget=m_b_forget, w_attn_out=m_w_attn_out, b_glu=m_b_glu, conv_dw_w=m_conv_dw_w, conv_dw_b=m_conv_dw_b, conv_ln_gain=m_conv_ln_gain, conv_ln_bias=m_conv_ln_bias, w_conv_out=m_w_conv_out, b_conv_out=m_b_conv_out, w_out=m_w_out, norm_mlp_gain=m_norm_mlp_gain, w_mlp_up=m_w_mlp_up, w_mlp_down=m_w_mlp_down, final_norm_gain=m_final_norm_gain)
    mom_v = dict(meta_tokens=v_meta_tokens, norm_mix_gain=v_norm_mix_gain, w_in=v_w_in, b_forget=v_b_forget, w_attn_out=v_w_attn_out, b_glu=v_b_glu, conv_dw_w=v_conv_dw_w, conv_dw_b=v_conv_dw_b, conv_ln_gain=v_conv_ln_gain, conv_ln_bias=v_conv_ln_bias, w_conv_out=v_w_conv_out, b_conv_out=v_b_conv_out, w_out=v_w_out, norm_mlp_gain=v_norm_mlp_gain, w_mlp_up=v_w_mlp_up, w_mlp_down=v_w_mlp_down, final_norm_gain=v_final_norm_gain)
    names = list(weights)
    batch, seq, d = x.shape
    t = seq + N_META
    n = batch * t
    n_pairs = d // LANES
    assert t % ROW_TILE == 0 and d == SEG

    to_rows = lambda w3: jnp.transpose(w3[0])
    w_in_t, m_in_t, v_in_t = to_rows(w_in), to_rows(m_w_in), to_rows(v_w_in)
    first = [w_in_t.astype(BF16), meta_tokens, conv_dw_w[0]]
    rest = [w_[0].astype(BF16) for w_ in (w_attn_out, w_conv_out, w_out, w_mlp_up, w_mlp_down)]
    tgt = jnp.concatenate([jnp.zeros((batch, N_META, d), F32), loss_target], axis=1).reshape(n, d)
    gather_a = _exchange_start("gather_in_start", [(f_, False) for f_ in first], ks=CHIP_PEERS)
    level_1 = _exchange_wait("gather_in_wait", gather_a, [gather_a["token"], tgt, w_in_t, m_in_t, v_in_t] + rest,
                             fill_own=False)
    passed = _pass_on_start("gather_in_pass_start", level_1)
    w_in_g, meta_g, w_dw_g = _pass_on_wait("gather_in_pass_wait", passed, passed["token"], first)
    gather_b = _exchange_start("gather_rest_start", [(r_, False) for r_ in rest])
    n_fg = b_forget.shape[1]
    shard_w = w_in.shape[2]
    wt = w_in_g.reshape(N_DEV * shard_w, d)
    o_fg = 3 * SEG
    seg_rows = [0, SEG, 2 * SEG] + [o_fg + n_fg + i * SEG for i in range(4)]
    d_ff = w_mlp_down.shape[1] * N_DEV
    ff_blk = d_ff // N_DEV
    meta_f = _cols_from_shards(meta_g)
    w_dw = _cols_from_shards(w_dw_g)

    row2 = lambda v: v.reshape(1, -1)
    g1, g2, g3 = row2(norm_mix_gain) + gather_b["token"][0:1, 0:1], row2(norm_mlp_gain), row2(final_norm_gain)
    b_fg = jnp.pad(b_forget, ((0, 0), (0, FG_PAD - n_fg)))
    h0 = jnp.concatenate([jnp.broadcast_to(meta_f[None], (batch, N_META, d)), x], axis=1).reshape(n, d)

    hn1 = _rms_fwd("rms1", h0, g1)
    proj = lambda name, off, width, tn, dt: _mm_nt(name, [(hn1, _a_rows(d), wt, _wt_rows(tn, off))], n, width, tn, dt)
    qkv = proj("proj_qkv", 0, 3 * SEG, SEG, BF16)
    glu = proj("proj_glu", seg_rows[3], 2 * SEG, SEG, BF16)
    gates = proj("proj_gates", seg_rows[5], 2 * SEG, SEG, BF16)
    fg = proj("proj_fg", o_fg, FG_PAD, FG_PAD, F32)

    cum = _fox_prep_fwd("fox_cumsum", fg, b_fg, batch)
    cumr = _key_sums_to_blocks(cum, batch, t, n_pairs)
    o, lse = _attn_fwd("attn_fwd", qkv, cumr, batch)
    rest = _exchange_wait("gather_rest_wait", gather_b, o)
    w_ao, w_co, w_o = [r_.reshape(d, d) for r_ in rest[:3]]
    w_up = rest[3]
    w_dn = rest[4].reshape(d_ff, d)
    a = _mm_nn("attn_out", o, w_ao, _w_cols(d, d, 0), d, d, BF16)

    c1 = _glu_conv_fwd("glu_conv", glu, b_glu, w_dw, conv_dw_b, batch)
    c3 = _ln_silu_fwd("ln_silu", c1, conv_ln_gain, conv_ln_bias)
    c = _mm_nn("conv_out", c3, w_co, _w_cols(d, d, 0), d, d, BF16)

    mrg = _merge_fwd("merge", gates, a, c, b_conv_out)
    mo = _mm_nn("mix_out", mrg, w_o, _w_cols(d, d, 0), d, d, F32)
    h1, hn2 = _rms_fwd("resid_rms2", h0, g2, res=mo)
    per = ff_blk // 512
    up, act = _mm_nn("mlp_up", hn2, w_up, pl.BlockSpec((None, d, 512), lambda i, j: (j // per, 0, j % per)),
                     d_ff, 512, BF16, relu2=True)
    dn = _mm_nn("mlp_down", act, w_dn, _w_cols(d_ff, d, 0), d, d, F32, tm=ROW_TILE)
    dh2, dh2b, loss_blk, dg3 = _final("final_loss", h1, dn, tgt, g3, batch)

    dup = _mm_nt("d_mlp_down", [(dh2b, _a_rows(d), w_dn, _w_rows(d, d))], n, d_ff, d, BF16, relu_bwd_of=up)
    dw_dn = _grad_w("gw_mlp_down", act, dh2b)
    dhn2 = _mm_nt("d_mlp_up", [(dup, _a_rows(ff_blk, g), w_up, pl.BlockSpec((None, 512, ff_blk), lambda i, j, g=g: (g, j, 0)))
                               for g in range(N_DEV)], n, d, 512, BF16)
    dw_up = _mm_tn("gw_mlp_up", hn2, lambda a_: 0, d, dup, lambda b_: b_, ff_blk, (N_DEV, d, ff_blk),
                   pl.BlockSpec((None, d, ff_blk), lambda a_, b_: (b_, 0, 0)), (1, N_DEV))
    scatter_1 = _exchange_start("scatter_mlp_start", [(dw_dn.reshape(N_DEV, ff_blk, d), True), (dw_up, True)])
    dh1, dg2, dh1b = _rms_bwd("rms2_bwd", dhn2, h1, g2 + scatter_1["token"][0:1, 0:1], dh2, batch, with_bf16=True)

    dm = _mm_nt("d_mix_out", [(dh1b, _a_rows(d), w_o, _w_rows(d, d))], n, d, d, BF16)
    dw_o = _grad_w("gw_mix_out", mrg, dh1b)
    da, dc, dga, dgc, dbco = _merge_bwd("merge_bwd", dm, gates, a, c, b_conv_out)

    do = _mm_nt("d_attn_out", [(da, _a_rows(d), w_ao, _w_rows(d, d))], n, d, d, BF16)
    dw_ao = _grad_w("gw_attn_out", o, da)
    dc3 = _mm_nt("d_conv_out", [(dc, _a_rows(d), w_co, _w_rows(d, d))], n, d, d, BF16)
    dw_co = _grad_w("gw_conv_out", c3, dc)

    scatter_2 = _exchange_start("scatter_mix_start", [(dw_.reshape(N_DEV, d // N_DEV, d), True)
                                                      for dw_ in (dw_o, dw_ao, dw_co)])
    dc1, dg_ln, db_ln = _ln_silu_bwd("ln_silu_bwd", dc3, c1, conv_ln_gain + scatter_2["token"][0:1, 0:1],
                                     conv_ln_bias)
    dglu_a, dglu_g, dw_dw, db_dw, dbg_a, dbg_g = _glu_conv_bwd("glu_conv_bwd", dc1, glu, b_glu, w_dw, batch)

    dq, dk, dv, dcumr, dcum_q = _attn_bwd("attn_bwd", qkv, o, do, lse, cumr, batch)
    dcum_k = jnp.pad(_key_sums_from_blocks(dcumr, batch, t, n_pairs), ((0, 0), (0, FG_PAD - 2 * n_pairs)))
    dfg, db_fg = _fox_prep_bwd("fox_cumsum_bwd", dcum_k, dcum_q, fg, b_fg, batch)

    segs = [dq, dk, dv, dglu_a, dglu_g, dga, dgc]
    gw_t = [_grad_w("gw_in_%d" % i, s_, hn1) for i, s_ in enumerate(segs)]
    gw_fg = _grad_w("gw_in_fg", dfg, hn1)[:n_fg]
    dw_in_t = jnp.concatenate(gw_t[:3] + [gw_fg] + gw_t[3:], axis=0).reshape(N_DEV, shard_w, d)
    scatter_3 = _exchange_start("scatter_in_start", [(dw_in_t, True)])
    pairs = [(s_, _a_rows(SEG, 0, ROW_TILE), wt, _wt_block(SEG, seg_rows[i], 512), "nn") for i, s_ in enumerate(segs)]
    pairs.append((dfg, _a_rows(FG_PAD, 0, ROW_TILE), wt, _wt_block(FG_PAD, o_fg, 512), "nn"))
    dhn1 = _mm_nt("d_proj_in", pairs, n, d, 512, BF16, tm=ROW_TILE, after=scatter_3["token"])
    dh0, dg1, dmeta = _rms_bwd("rms1_bwd", dhn1, h0, g1, dh1, batch, with_meta=True)
    grad_x = dh0.reshape(batch, t, d)[:, N_META:, :]

    small_g = dict(norm_mix_gain=dg1, b_forget=db_fg[:, :n_fg], b_glu=jnp.concatenate([dbg_a, dbg_g], axis=1),
                   conv_dw_b=db_dw, conv_ln_gain=dg_ln, conv_ln_bias=db_ln, b_conv_out=dbco, norm_mlp_gain=dg2,
                   final_norm_gain=dg3)
    scatter_4 = _exchange_start("scatter_small_start", [
        (_shards_from_cols(dmeta), True), (_shards_from_cols(dw_dw), True), (_cat_small(small_g), False),
        (loss_blk[0:1, :], False)])

    grads, deltas, new_m, new_v = {}, {}, {}, {}

    def update(k, parts):
        shp = weights[k].shape
        if k == "w_in":
            res_ = _adamw("adamw_" + k, parts, w_in_t, m_in_t, v_in_t)
            res_ = [jnp.transpose(r) for r in res_]
        else:
            w2 = lambda arr: arr.reshape(parts.shape[1:])
            res_ = _adamw("adamw_" + k, parts, w2(weights[k]), w2(mom_m[k]), w2(mom_v[k]))
        grads[k], deltas[k], new_m[k], new_v[k] = [r.reshape(shp) for r in res_]

    for k, parts in zip(("w_mlp_down", "w_mlp_up"), _exchange_wait("scatter_mlp_wait", scatter_1, scatter_4["token"])):
        update(k, parts)
    for k, parts in zip(("w_out", "w_attn_out", "w_conv_out"),
                        _exchange_wait("scatter_mix_wait", scatter_2, deltas["w_mlp_up"])):
        update(k, parts)
    update("w_in", _exchange_wait("scatter_in_wait", scatter_3, deltas["w_conv_out"])[0])
    reduced = _exchange_wait("scatter_small_wait", scatter_4, deltas["w_in"])
    loss = jnp.sum(reduced.pop()[:, 0, 0])
    for k, parts in zip(("meta_tokens", "conv_dw_w"), reduced[:-1]):
        update(k, parts)
    res = _adamw("adamw_small", reduced[-1], _cat_small(weights), _cat_small(mom_m), _cat_small(mom_v))
    shapes = {k: weights[k].shape for k in SMALL}
    for dst, r in zip((grads, deltas, new_m, new_v), res):
        dst.update(_split_small(r, shapes))

    return (loss, grad_x, *[grads[k] for k in names], *[deltas[k] for k in names],
            *[new_m[k] for k in names], *[new_v[k] for k in names])
```

```python
import functools

import jax
import jax.numpy as jnp
from jax import lax
from jax.experimental import pallas as pl
from jax.experimental.pallas import tpu as pltpu

F32, BF16 = jnp.float32, jnp.bfloat16
N_DEV = 8
N_META = 16
HEAD_DIM = 64
LANES = 128
CONV_W = 31
RMS_EPS = 1e-6
LN_EPS = 1e-5
ROW_TILE = 688
MM_TM = 2 * ROW_TILE
SEG = 1024
FG_PAD = 128
VMEM_LIMIT = 56 * 1024 * 1024
ADAM_LR, ADAM_B1, ADAM_B2, ADAM_EPS, ADAM_WD, ADAM_STEP = 0.001, 0.9, 0.999, 1e-08, 0.01, 10
NEG = -1e30
LOG2E = 1.4426950408889634

SMALL = ("norm_mix_gain", "b_forget", "b_glu", "conv_dw_b", "conv_ln_gain", "conv_ln_bias", "b_conv_out",
         "norm_mlp_gain", "final_norm_gain")
SMALL_W = {"norm_mix_gain": 1024, "b_forget": 128, "b_glu": 2048, "conv_dw_b": 1024, "conv_ln_gain": 1024,
           "conv_ln_bias": 1024, "b_conv_out": 1024, "norm_mlp_gain": 1024, "final_norm_gain": 1024}
SMALL_N = {"norm_mix_gain": 1024, "b_forget": 16, "b_glu": 2048, "conv_dw_b": 1024, "conv_ln_gain": 1024,
           "conv_ln_bias": 1024, "b_conv_out": 1024, "norm_mlp_gain": 1024, "final_norm_gain": 1024}


def _params(sem=None):
    return pltpu.CompilerParams(dimension_semantics=sem, vmem_limit_bytes=VMEM_LIMIT)


def _sigmoid(x):
    return 1.0 / (1.0 + jnp.exp(-x))


def _dot_nt(a, b):
    return lax.dot_general(a, b, (((1,), (1,)), ((), ())), preferred_element_type=F32)


def _dot_tn(a, b):
    return lax.dot_general(a, b, (((0,), (0,)), ((), ())), preferred_element_type=F32)


HBM_SPEC = pl.BlockSpec(memory_space=pltpu.HBM)
SEM_SPEC = pl.BlockSpec(memory_space=pltpu.SEMAPHORE)
DATAFLOW = pltpu.SideEffectType.DATAFLOW_SIDE_EFFECTING


def _device_index():
    return 4 * lax.axis_index("x") + 2 * lax.axis_index("y") + lax.axis_index("c")


def _peers():
    x, y, c = lax.axis_index("x"), lax.axis_index("y"), lax.axis_index("c")
    out = []
    for k in range(1, N_DEV):
        px = 1 - x if k & 4 else x
        py = 1 - y if k & 2 else y
        pc = 1 - c if k & 1 else c
        out.append((k, (px, py, pc), 4 * px + 2 * py + pc))
    return out


def _peer_copy(per_dest, src_ref, land_ref, send_sems, recv_sems, a, k, dev, peer):
    src = src_ref.at[peer] if per_dest else src_ref
    return pltpu.make_async_remote_copy(
        src_ref=src, dst_ref=land_ref.at[_device_index()], send_sem=send_sems.at[a * (N_DEV - 1) + k - 1],
        recv_sem=recv_sems.at[a * (N_DEV - 1) + k - 1], device_id=dev, device_id_type=pl.DeviceIdType.MESH)


ALL_PEERS = tuple(range(1, N_DEV))
CHIP_PEERS = (1, 2, 4, 6)
FAR_PEERS = (2, 4, 6)


def _exchange_start(name, items, ks=ALL_PEERS):
    n = len(items)
    per_dest = [it[1] for it in items]

    def body(*refs):
        srcs, lands = refs[:n], refs[n:2 * n]
        send_sems, recv_sems, token = refs[2 * n], refs[2 * n + 1], refs[-1]
        for a in range(n):
            for k, dev, peer in _peers():
                if k in ks:
                    _peer_copy(per_dest[a], srcs[a], lands[a], send_sems, recv_sems, a, k, dev, peer).start()
        token[...] = jnp.zeros(token.shape, F32)

    srcs = [pltpu.with_memory_space_constraint(it[0], pltpu.HBM) for it in items]
    lands = []
    for arr, pd in items:
        shp = arr.shape if pd else (N_DEV,) + arr.shape
        lands.append(pltpu.with_memory_space_constraint(lax.empty(shp, arr.dtype), pltpu.HBM))
    sems = pltpu.SemaphoreType.DMA((n * (N_DEV - 1),))
    res = pl.pallas_call(
        body, name=name,
        out_shape=(sems, sems, *[pltpu.HBM(a_.shape, a_.dtype) for a_ in srcs + lands],
                   jax.ShapeDtypeStruct((8, 128), F32)),
        in_specs=[HBM_SPEC] * (2 * n),
        out_specs=(SEM_SPEC, SEM_SPEC, *[HBM_SPEC] * (2 * n), pl.BlockSpec(memory_space=pltpu.VMEM)),
        input_output_aliases={i: 2 + i for i in range(2 * n)},
        compiler_params=pltpu.CompilerParams(has_side_effects=DATAFLOW),
    )(*srcs, *lands)
    return dict(per_dest=per_dest, ks=ks, send=res[0], recv=res[1], srcs=list(res[2:2 + n]),
                lands=list(res[2 + n:2 + 2 * n]), token=res[-1])


def _fill_own(per_dest, srcs, lands):
    me = _device_index()
    out = []
    for pd, src, land in zip(per_dest, srcs, lands):
        own = lax.dynamic_index_in_dim(src, me, 0, keepdims=True) if pd else src[None]
        out.append(lax.dynamic_update_slice_in_dim(land, own, me, axis=0))
    return out


def _exchange_wait(name, started, after, fill_own=True):
    per_dest = started["per_dest"]
    n = len(per_dest)

    def body(*refs):
        srcs, lands = refs[:n], refs[n:2 * n]
        send_sems, recv_sems = refs[2 * n], refs[2 * n + 1]
        for a in range(n):
            for k, dev, peer in _peers():
                if k in started["ks"]:
                    cp = _peer_copy(per_dest[a], srcs[a], lands[a], send_sems, recv_sems, a, k, dev, peer)
                    cp.wait_send()
                    cp.wait_recv()

    bufs = started["srcs"] + started["lands"]
    after = list(after) if isinstance(after, (list, tuple)) else [after]
    res = pl.pallas_call(
        body, name=name, out_shape=tuple(pltpu.HBM(b_.shape, b_.dtype) for b_ in bufs),
        in_specs=[HBM_SPEC] * (2 * n) + [SEM_SPEC, SEM_SPEC] + [pl.BlockSpec(memory_space=pl.ANY)] * len(after),
        out_specs=tuple([HBM_SPEC] * (2 * n)), input_output_aliases={i: i for i in range(2 * n)},
        compiler_params=pltpu.CompilerParams(has_side_effects=DATAFLOW),
    )(*bufs, started["send"], started["recv"], *after)
    return _fill_own(per_dest, res[:n], res[n:]) if fill_own else list(res[n:])


def _pass_on_copy(land_ref, send_sems, recv_sems, a, idx, slot):
    sibling = (lax.axis_index("x"), lax.axis_index("y"), 1 - lax.axis_index("c"))
    return pltpu.make_async_remote_copy(
        src_ref=land_ref.at[slot], dst_ref=land_ref.at[slot], send_sem=send_sems.at[a * len(FAR_PEERS) + idx],
        recv_sem=recv_sems.at[a * len(FAR_PEERS) + idx], device_id=sibling, device_id_type=pl.DeviceIdType.MESH)


def _pass_on_start(name, lands):
    n = len(lands)

    def body(*refs):
        send_sems, recv_sems, token = refs[n], refs[n + 1], refs[-1]
        slots = {k: peer for k, _, peer in _peers()}
        for a in range(n):
            for idx, k in enumerate(FAR_PEERS):
                _pass_on_copy(refs[a], send_sems, recv_sems, a, idx, slots[k]).start()
        token[...] = jnp.zeros(token.shape, F32)

    lands = [pltpu.with_memory_space_constraint(l_, pltpu.HBM) for l_ in lands]
    sems = pltpu.SemaphoreType.DMA((n * len(FAR_PEERS),))
    res = pl.pallas_call(
        body, name=name,
        out_shape=(sems, sems, *[pltpu.HBM(l_.shape, l_.dtype) for l_ in lands], jax.ShapeDtypeStruct((8, 128), F32)),
        in_specs=[HBM_SPEC] * n, out_specs=(SEM_SPEC, SEM_SPEC, *[HBM_SPEC] * n, pl.BlockSpec(memory_space=pltpu.VMEM)),
        input_output_aliases={i: 2 + i for i in range(n)},
        compiler_params=pltpu.CompilerParams(has_side_effects=DATAFLOW),
    )(*lands)
    return dict(send=res[0], recv=res[1], lands=list(res[2:2 + n]), token=res[-1])


def _pass_on_wait(name, passed, after, owns):
    n = len(passed["lands"])

    def body(*refs):
        send_sems, recv_sems = refs[n], refs[n + 1]
        slots = {k: peer for k, _, peer in _peers()}
        for a in range(n):
            for idx, k in enumerate(FAR_PEERS):
                _pass_on_copy(refs[a], send_sems, recv_sems, a, idx, slots[k]).wait_send()
                _pass_on_copy(refs[a], send_sems, recv_sems, a, idx, slots[k ^ 1]).wait_recv()

    res = pl.pallas_call(
        body, name=name, out_shape=tuple(pltpu.HBM(l_.shape, l_.dtype) for l_ in passed["lands"]),
        in_specs=[HBM_SPEC] * n + [SEM_SPEC, SEM_SPEC, pl.BlockSpec(memory_space=pl.ANY)],
        out_specs=tuple([HBM_SPEC] * n), input_output_aliases={i: i for i in range(n)},
        compiler_params=pltpu.CompilerParams(has_side_effects=DATAFLOW),
    )(*passed["lands"], passed["send"], passed["recv"], after)
    return _fill_own([False] * n, owns, res)


def _mm_nn(name, x, w, w_spec, n_out, tn, out_dtype, relu2=False, tm=MM_TM):
    m, k = x.shape

    def body(x_ref, w_ref, *outs):
        acc = jnp.dot(x_ref[...], w_ref[...], preferred_element_type=F32)
        if relu2:
            acc = jnp.maximum(acc, 0.0)
            acc = acc * acc
        outs[0][...] = acc.astype(outs[0].dtype)

    o_spec = pl.BlockSpec((tm, tn), lambda i, j: (i, j))
    return pl.pallas_call(
        body, name=name, out_shape=jax.ShapeDtypeStruct((m, n_out), out_dtype), grid=(m // tm, n_out // tn),
        in_specs=[pl.BlockSpec((tm, k), lambda i, j: (i, 0)), w_spec],
        out_specs=o_spec, compiler_params=_params(("parallel", "parallel")),
    )(x, w)


def _mm_nt(name, pairs, m, n_out, tn, out_dtype, relu_bwd_of=None, tm=MM_TM, after=None):
    np_ = len(pairs)

    def body(*refs):
        acc = None
        for p in range(np_):
            if len(pairs[p]) == 5:
                d = jnp.dot(refs[2 * p][...], refs[2 * p + 1][...], preferred_element_type=F32)
            else:
                d = _dot_nt(refs[2 * p][...], refs[2 * p + 1][...])
            acc = d if acc is None else acc + d
        if relu_bwd_of is not None:
            acc = acc * (2.0 * jnp.sqrt(refs[2 * np_][...].astype(F32)))
        refs[-1][...] = acc.astype(refs[-1].dtype)

    o_spec = pl.BlockSpec((tm, tn), lambda i, j: (i, j))
    operands, specs = [], []
    for pair in pairs:
        operands += [pair[0], pair[2]]
        specs += [pair[1], pair[3]]
    if relu_bwd_of is not None:
        operands.append(relu_bwd_of)
        specs.append(o_spec)
    if after is not None:
        operands.append(after)
        specs.append(pl.BlockSpec((8, 128), lambda i, j: (0, 0)))
    return pl.pallas_call(
        body, name=name, out_shape=jax.ShapeDtypeStruct((m, n_out), out_dtype), grid=(m // tm, n_out // tn),
        in_specs=specs, out_specs=o_spec, compiler_params=_params(("parallel", "parallel")),
    )(*operands)


def _mm_tn(name, x, x_col, ta, dy, dy_col, tb, out_shape, out_spec, grid_ab):
    m = x.shape[0]

    def body(x_ref, dy_ref, o_ref):
        o_ref[...] = _dot_tn(x_ref[...], dy_ref[...]).astype(BF16)

    return pl.pallas_call(
        body, name=name, out_shape=jax.ShapeDtypeStruct(out_shape, BF16), grid=grid_ab,
        in_specs=[pl.BlockSpec((m, ta), lambda a, b: (0, x_col(a))),
                  pl.BlockSpec((m, tb), lambda a, b: (0, dy_col(b)))],
        out_specs=out_spec, compiler_params=_params(("parallel", "parallel")),
    )(x, dy)


def _w_cols(k, tn, off_blocks):
    return pl.BlockSpec((k, tn), lambda i, j: (0, off_blocks + j))


def _a_rows(kw, col_block=0, tm=MM_TM):
    return pl.BlockSpec((tm, kw), lambda i, j: (i, col_block))


def _w_rows(tn, kw, col_block=0):
    return pl.BlockSpec((tn, kw), lambda i, j: (j, col_block))


def _wt_rows(tn, off):
    return pl.BlockSpec((pl.Element(tn), pl.Element(SEG)), lambda i, j: (pl.multiple_of(off + tn * j, 16), 0))


def _wt_block(k, off, tn):
    return pl.BlockSpec((pl.Element(k), pl.Element(tn)), lambda i, j: (off, pl.multiple_of(tn * j, 128)))


def _grad_w(name, x, dy):
    na, nb = x.shape[1], dy.shape[1]
    ta, tb = min(na, 1024), min(nb, 512)
    return _mm_tn(name, x, lambda a: a, ta, dy, lambda b: b, tb, (na, nb),
                  pl.BlockSpec((ta, tb), lambda a, b: (a, b)), (na // ta, nb // tb))


def _row_spec(width):
    return pl.BlockSpec((ROW_TILE, width), lambda i: (i, 0))


def _vec_spec(width):
    return pl.BlockSpec((1, width), lambda i: (0, 0))


def _rms_fwd(name, h, g, res=None):
    n, d = h.shape

    def body(*refs):
        if res is None:
            h_ref, g_ref, hn_ref = refs
            hv = h_ref[...]
        else:
            h_ref, r_ref, g_ref, hs_ref, hn_ref = refs
            hv = h_ref[...] + r_ref[...]
            hs_ref[...] = hv
        r = lax.rsqrt(jnp.mean(hv * hv, axis=-1, keepdims=True) + RMS_EPS)
        hn_ref[...] = (hv * r * g_ref[...]).astype(BF16)

    ins = [h, g] if res is None else [h, res, g]
    in_specs = [_row_spec(d), _vec_spec(d)] if res is None else [_row_spec(d), _row_spec(d), _vec_spec(d)]
    hn_shape = jax.ShapeDtypeStruct((n, d), BF16)
    if res is None:
        out_shape, out_specs = hn_shape, _row_spec(d)
    else:
        out_shape, out_specs = [jax.ShapeDtypeStruct((n, d), F32), hn_shape], [_row_spec(d), _row_spec(d)]
    return pl.pallas_call(body, name=name, out_shape=out_shape, grid=(n // ROW_TILE,), in_specs=in_specs,
                          out_specs=out_specs, compiler_params=_params(("parallel",)))(*ins)


def _rms_bwd(name, dhn, h, g, dres, batch, with_bf16=False, with_meta=False):
    n, d = h.shape
    t = n // batch
    nt = t // ROW_TILE

    def body(dhn_ref, h_ref, g_ref, dres_ref, *outs):
        first = (pl.program_id(0) == 0) & (pl.program_id(1) == 0)
        hv = h_ref[...]
        r = lax.rsqrt(jnp.mean(hv * hv, axis=-1, keepdims=True) + RMS_EPS)
        nrm = hv * r
        dhn = dhn_ref[...].astype(F32)
        dn = dhn * g_ref[...]
        dh = dres_ref[...] + r * (dn - nrm * jnp.mean(dn * nrm, axis=-1, keepdims=True))
        outs[0][...] = dh
        dg_ref = outs[1]

        @pl.when(first)
        def _():
            dg_ref[...] = jnp.zeros(dg_ref.shape, F32)

        dg_ref[...] += jnp.sum(dhn * nrm, axis=0, keepdims=True)
        nxt = 2
        if with_bf16:
            outs[nxt][...] = dh.astype(BF16)
            nxt += 1
        if with_meta:
            meta_ref = outs[nxt]

            @pl.when(first)
            def _():
                meta_ref[...] = jnp.zeros(meta_ref.shape, F32)

            @pl.when(pl.program_id(1) == 0)
            def _():
                meta_ref[...] += dh[0:N_META, :]

    row = pl.BlockSpec((ROW_TILE, d), lambda b, j: (b * nt + j, 0))
    vec = pl.BlockSpec((1, d), lambda b, j: (0, 0))
    shapes = [jax.ShapeDtypeStruct((n, d), F32), jax.ShapeDtypeStruct((1, d), F32)]
    specs = [row, vec]
    if with_bf16:
        shapes.append(jax.ShapeDtypeStruct((n, d), BF16))
        specs.append(row)
    if with_meta:
        shapes.append(jax.ShapeDtypeStruct((N_META, d), F32))
        specs.append(pl.BlockSpec((N_META, d), lambda b, j: (0, 0)))
    return pl.pallas_call(body, name=name, out_shape=shapes, grid=(batch, nt), in_specs=[row, row, vec, row],
                          out_specs=specs, compiler_params=_params(("arbitrary", "arbitrary")))(dhn, h, g, dres)


def _final(name, h1, dn, tgt, g, batch):
    n, d = h1.shape
    t = n // batch
    nt = t // ROW_TILE

    def body(h1_ref, dn_ref, tgt_ref, g_ref, dh_ref, dhb_ref, loss_ref, dg_ref):
        first = (pl.program_id(0) == 0) & (pl.program_id(1) == 0)
        hv = h1_ref[...] + dn_ref[...]
        r = lax.rsqrt(jnp.mean(hv * hv, axis=-1, keepdims=True) + RMS_EPS)
        nrm = hv * r
        gv = g_ref[...]
        pos = pl.program_id(1) * ROW_TILE + lax.broadcasted_iota(jnp.int32, (ROW_TILE, 1), 0)
        diff = jnp.where(pos >= N_META, nrm * gv - tgt_ref[...], 0.0)
        dy = diff * (1.0 / d)

        @pl.when(first)
        def _():
            loss_ref[...] = jnp.zeros(loss_ref.shape, F32)
            dg_ref[...] = jnp.zeros(dg_ref.shape, F32)

        loss_ref[...] += jnp.full(loss_ref.shape, 0.5 / d, F32) * jnp.sum(diff * diff)
        dg_ref[...] += jnp.sum(dy * nrm, axis=0, keepdims=True)
        dng = dy * gv
        dh = r * (dng - nrm * jnp.mean(dng * nrm, axis=-1, keepdims=True))
        dh_ref[...] = dh
        dhb_ref[...] = dh.astype(BF16)

    row = pl.BlockSpec((ROW_TILE, d), lambda b, j: (b * nt + j, 0))
    vec = pl.BlockSpec((1, d), lambda b, j: (0, 0))
    return pl.pallas_call(
        body, name=name, grid=(batch, nt), in_specs=[row, row, row, vec],
        out_shape=[jax.ShapeDtypeStruct((n, d), F32), jax.ShapeDtypeStruct((n, d), BF16),
                   jax.ShapeDtypeStruct((8, 128), F32), jax.ShapeDtypeStruct((1, d), F32)],
        out_specs=[row, row, pl.BlockSpec((8, 128), lambda b, j: (0, 0)), vec],
        compiler_params=_params(("arbitrary", "arbitrary")))(h1, dn, tgt, g)


def _ln_silu_fwd(name, c1, g, b):
    n, d = c1.shape

    def body(c_ref, g_ref, b_ref, o_ref):
        xv = c_ref[...]
        xc = xv - jnp.mean(xv, axis=-1, keepdims=True)
        rstd = lax.rsqrt(jnp.mean(xc * xc, axis=-1, keepdims=True) + LN_EPS)
        c2 = xc * rstd * g_ref[...] + b_ref[...]
        o_ref[...] = (c2 * _sigmoid(c2)).astype(BF16)

    return pl.pallas_call(body, name=name, out_shape=jax.ShapeDtypeStruct((n, d), BF16), grid=(n // ROW_TILE,),
                          in_specs=[_row_spec(d), _vec_spec(d), _vec_spec(d)], out_specs=_row_spec(d),
                          compiler_params=_params(("parallel",)))(c1, g, b)


def _ln_silu_bwd(name, dc3, c1, g, b):
    n, d = c1.shape

    def body(d_ref, c_ref, g_ref, b_ref, dc1_ref, dg_ref, db_ref):
        xv = c_ref[...]
        xc = xv - jnp.mean(xv, axis=-1, keepdims=True)
        rstd = lax.rsqrt(jnp.mean(xc * xc, axis=-1, keepdims=True) + LN_EPS)
        xh = xc * rstd
        c2 = xh * g_ref[...] + b_ref[...]
        s = _sigmoid(c2)
        dc2 = d_ref[...].astype(F32) * (s * (1.0 + c2 * (1.0 - s)))

        @pl.when(pl.program_id(0) == 0)
        def _():
            dg_ref[...] = jnp.zeros(dg_ref.shape, F32)
            db_ref[...] = jnp.zeros(db_ref.shape, F32)

        dg_ref[...] += jnp.sum(dc2 * xh, axis=0, keepdims=True)
        db_ref[...] += jnp.sum(dc2, axis=0, keepdims=True)
        dxh = dc2 * g_ref[...]
        dc1_ref[...] = rstd * (dxh - jnp.mean(dxh, axis=-1, keepdims=True)
                               - xh * jnp.mean(dxh * xh, axis=-1, keepdims=True))

    return pl.pallas_call(
        body, name=name, grid=(n // ROW_TILE,),
        out_shape=[jax.ShapeDtypeStruct((n, d), F32), jax.ShapeDtypeStruct((1, d), F32),
                   jax.ShapeDtypeStruct((1, d), F32)],
        in_specs=[_row_spec(d), _row_spec(d), _vec_spec(d), _vec_spec(d)],
        out_specs=[_row_spec(d), _vec_spec(d), _vec_spec(d)],
        compiler_params=_params(("arbitrary",)))(dc3, c1, g, b)


MERGE_TC = 512


def _merge_fwd(name, gates, a, c, b_co):
    n, d = a.shape
    nc = d // MERGE_TC

    def body(ga_ref, gc_ref, a_ref, c_ref, b_ref, m_ref):
        f32 = lambda r_: r_[...].astype(F32)
        m = _sigmoid(f32(ga_ref)) * f32(a_ref) + _sigmoid(f32(gc_ref)) * (f32(c_ref) + b_ref[...])
        m_ref[...] = m.astype(BF16)

    blk = lambda off: pl.BlockSpec((ROW_TILE, MERGE_TC), lambda i, j: (i, off + j))
    return pl.pallas_call(
        body, name=name, out_shape=jax.ShapeDtypeStruct((n, d), BF16), grid=(n // ROW_TILE, nc),
        in_specs=[blk(0), blk(nc), blk(0), blk(0), pl.BlockSpec((1, MERGE_TC), lambda i, j: (0, j))],
        out_specs=blk(0), compiler_params=_params(("parallel", "parallel")))(gates, gates, a, c, b_co)


def _merge_bwd(name, dm, gates, a, c, b_co):
    n, d = a.shape
    nc = d // MERGE_TC

    def body(dm_ref, ga_ref, gc_ref, a_ref, c_ref, b_ref, da_ref, dc_ref, dga_ref, dgc_ref, dbco_ref):
        f32 = lambda r_: r_[...].astype(F32)
        dmv = f32(dm_ref)
        sa, sc = _sigmoid(f32(ga_ref)), _sigmoid(f32(gc_ref))
        dc = dmv * sc
        da_ref[...] = (dmv * sa).astype(BF16)
        dc_ref[...] = dc.astype(BF16)
        dga_ref[...] = (dmv * f32(a_ref) * sa * (1.0 - sa)).astype(BF16)
        dgc_ref[...] = (dmv * (f32(c_ref) + b_ref[...]) * sc * (1.0 - sc)).astype(BF16)

        @pl.when(pl.program_id(1) == 0)
        def _():
            dbco_ref[...] = jnp.zeros(dbco_ref.shape, F32)

        dbco_ref[...] += jnp.sum(dc, axis=0, keepdims=True)

    blk = lambda off: pl.BlockSpec((ROW_TILE, MERGE_TC), lambda j, i: (i, off + j))
    vec = pl.BlockSpec((1, MERGE_TC), lambda j, i: (0, j))
    act = jax.ShapeDtypeStruct((n, d), BF16)
    return pl.pallas_call(
        body, name=name, grid=(nc, n // ROW_TILE),
        out_shape=[act, act, act, act, jax.ShapeDtypeStruct((1, d), F32)],
        in_specs=[blk(0), blk(0), blk(nc), blk(0), blk(0), vec],
        out_specs=[blk(0), blk(0), blk(0), blk(0), vec],
        compiler_params=_params(("parallel", "arbitrary")))(dm, gates, gates, a, c, b_co)


CONV_TC = 128
CONV_HALO = 32


def _conv_chunk(t):
    return 48 if t % 48 == 0 else 32 if t % 32 == 0 else 16


def _fold8(x):
    out = x[0:8]
    for k in range(1, x.shape[0] // 8):
        out = out + x[8 * k:8 * k + 8]
    return out


def _glu_conv_fwd(name, glu, b_glu, w_dw, b_dw, batch):
    n, c2 = glu.shape
    c = c2 // 2
    t = n // batch
    nc = c // CONV_TC

    def body(a_ref, gt_ref, ba_ref, bg_ref, w_ref, bdw_ref, o_ref, pad_ref):
        u = (a_ref[...].astype(F32) + ba_ref[...]) * _sigmoid(gt_ref[...].astype(F32) + bg_ref[...])
        pad_ref[0:CONV_HALO, :] = jnp.zeros((CONV_HALO, CONV_TC), F32)
        pad_ref[CONV_HALO:CONV_HALO + t, :] = u
        ch = _conv_chunk(t)
        for r0 in range(0, t, ch):
            acc = jnp.zeros((ch, CONV_TC), F32) + bdw_ref[...]
            for j in range(CONV_W):
                off = r0 + CONV_HALO - (CONV_W - 1) + j
                acc = acc + w_ref[j:j + 1, :] * pad_ref[off:off + ch, :]
            o_ref[r0:r0 + ch, :] = acc

    seq = lambda off: pl.BlockSpec((t, CONV_TC), lambda b, j: (b, off + j))
    vec = lambda off: pl.BlockSpec((1, CONV_TC), lambda b, j: (0, off + j))
    return pl.pallas_call(
        body, name=name, out_shape=jax.ShapeDtypeStruct((n, c), F32), grid=(batch, nc),
        in_specs=[seq(0), seq(nc), vec(0), vec(nc), pl.BlockSpec((CONV_W, CONV_TC), lambda b, j: (0, j)), vec(0)],
        out_specs=seq(0), scratch_shapes=[pltpu.VMEM((t + CONV_HALO, CONV_TC), F32)],
        compiler_params=_params(("parallel", "parallel")))(glu, glu, b_glu, b_glu, w_dw, b_dw)


def _glu_conv_bwd(name, dc1, glu, b_glu, w_dw, batch):
    n, c2 = glu.shape
    c = c2 // 2
    t = n // batch
    nc = c // CONV_TC

    def body(d_ref, a_ref, gt_ref, ba_ref, bg_ref, w_ref, dga_ref, dgg_ref, dw_ref, dbdw_ref, dba_ref, dbg_ref,
             padu_ref, padd_ref):
        av = a_ref[...].astype(F32) + ba_ref[...]
        sg = _sigmoid(gt_ref[...].astype(F32) + bg_ref[...])
        dc = d_ref[...]
        padu_ref[0:CONV_HALO, :] = jnp.zeros((CONV_HALO, CONV_TC), F32)
        padu_ref[CONV_HALO:CONV_HALO + t, :] = av * sg
        padd_ref[0:t, :] = dc
        padd_ref[t:t + CONV_HALO, :] = jnp.zeros((CONV_HALO, CONV_TC), F32)

        @pl.when(pl.program_id(1) == 0)
        def _():
            dw_ref[...] = jnp.zeros(dw_ref.shape, F32)
            dbdw_ref[...] = jnp.zeros(dbdw_ref.shape, F32)
            dba_ref[...] = jnp.zeros(dba_ref.shape, F32)
            dbg_ref[...] = jnp.zeros(dbg_ref.shape, F32)

        ch = _conv_chunk(t)
        zero8 = jnp.zeros((8, CONV_TC), F32)
        dw_acc = [zero8] * CONV_W
        sum_dc, sum_a, sum_g = zero8, zero8, zero8
        for r0 in range(0, t, ch):
            dcc = d_ref[r0:r0 + ch, :]
            du = jnp.zeros((ch, CONV_TC), F32)
            for j in range(CONV_W):
                back = r0 + CONV_W - 1 - j
                du = du + w_ref[j:j + 1, :] * padd_ref[back:back + ch, :]
                off = r0 + CONV_HALO - (CONV_W - 1) + j
                dw_acc[j] = dw_acc[j] + _fold8(dcc * padu_ref[off:off + ch, :])
            sgc = _sigmoid(gt_ref[r0:r0 + ch, :].astype(F32) + bg_ref[...])
            dga = du * sgc
            dgg = du * padu_ref[CONV_HALO + r0:CONV_HALO + r0 + ch, :] * (1.0 - sgc)
            dga_ref[r0:r0 + ch, :] = dga.astype(BF16)
            dgg_ref[r0:r0 + ch, :] = dgg.astype(BF16)
            sum_dc, sum_a, sum_g = sum_dc + _fold8(dcc), sum_a + _fold8(dga), sum_g + _fold8(dgg)
        for j in range(CONV_W):
            dw_ref[j:j + 1, :] += jnp.sum(dw_acc[j], axis=0, keepdims=True)
        dbdw_ref[...] += jnp.sum(sum_dc, axis=0, keepdims=True)
        dba_ref[...] += jnp.sum(sum_a, axis=0, keepdims=True)
        dbg_ref[...] += jnp.sum(sum_g, axis=0, keepdims=True)

    seq = lambda off: pl.BlockSpec((t, CONV_TC), lambda j, b: (b, off + j))
    vec = lambda off: pl.BlockSpec((1, CONV_TC), lambda j, b: (0, off + j))
    wsp = pl.BlockSpec((CONV_W, CONV_TC), lambda j, b: (0, j))
    act = jax.ShapeDtypeStruct((n, c), BF16)
    v = jax.ShapeDtypeStruct((1, c), F32)
    return pl.pallas_call(
        body, name=name, grid=(nc, batch),
        out_shape=[act, act, jax.ShapeDtypeStruct((CONV_W, c), F32), v, v, v],
        in_specs=[seq(0), seq(0), seq(nc), vec(0), vec(nc), wsp],
        out_specs=[seq(0), seq(0), wsp, vec(0), vec(0), vec(0)],
        scratch_shapes=[pltpu.VMEM((t + CONV_HALO, CONV_TC), F32), pltpu.VMEM((t + CONV_HALO, CONV_TC), F32)],
        compiler_params=_params(("parallel", "arbitrary")))(dc1, glu, glu, b_glu, b_glu, w_dw)


def _split3(x):
    hi = x.astype(BF16)
    r = x - hi.astype(F32)
    mid = r.astype(BF16)
    lo = (r - mid.astype(F32)).astype(BF16)
    return hi, mid, lo


def _tri_matmul(tri, x):
    hi, mid, lo = _split3(x)
    dot = lambda v: jnp.dot(tri, v, preferred_element_type=F32)
    return dot(hi) + dot(mid) + dot(lo)


def _fox_prep_fwd(name, fg, b_fg, batch):
    n, w = fg.shape
    t = n // batch
    nq = t // ROW_TILE

    def body(fg_ref, b_ref, cum_ref):
        row = lax.broadcasted_iota(jnp.int32, (ROW_TILE, ROW_TILE), 0)
        col = lax.broadcasted_iota(jnp.int32, (ROW_TILE, ROW_TILE), 1)
        tri = (row >= col).astype(BF16)
        for k in range(nq):
            rows = slice(k * ROW_TILE, (k + 1) * ROW_TILE)
            z = fg_ref[rows, :] + b_ref[...]
            logf = jnp.minimum(z, 0.0) - jnp.log(1.0 + jnp.exp(-jnp.abs(z)))
            cum = _tri_matmul(tri, logf)
            if k > 0:
                cum = cum + cum_ref[k * ROW_TILE - 1:k * ROW_TILE, :]
            cum_ref[rows, :] = cum

    seq = pl.BlockSpec((t, w), lambda b: (b, 0))
    return pl.pallas_call(body, name=name, out_shape=jax.ShapeDtypeStruct((n, w), F32), grid=(batch,),
                          in_specs=[seq, pl.BlockSpec((1, w), lambda b: (0, 0))], out_specs=seq,
                          compiler_params=_params(("parallel",)))(fg, b_fg)


def _fox_prep_bwd(name, dcum_k, dcum_q, fg, b_fg, batch):
    n, w = fg.shape
    t = n // batch
    nq = t // ROW_TILE

    def body(dk_ref, dq_ref, fg_ref, b_ref, dfg_ref, db_ref, rev_ref):
        row = lax.broadcasted_iota(jnp.int32, (ROW_TILE, ROW_TILE), 0)
        col = lax.broadcasted_iota(jnp.int32, (ROW_TILE, ROW_TILE), 1)
        tri = (col >= row).astype(BF16)

        @pl.when(pl.program_id(0) == 0)
        def _():
            db_ref[...] = jnp.zeros(db_ref.shape, F32)

        for k in reversed(range(nq)):
            rows = slice(k * ROW_TILE, (k + 1) * ROW_TILE)
            dlog = _tri_matmul(tri, dk_ref[rows, :] + dq_ref[rows, :])
            if k < nq - 1:
                dlog = dlog + rev_ref[(k + 1) * ROW_TILE:(k + 1) * ROW_TILE + 1, :]
            rev_ref[rows, :] = dlog
            dfg = dlog * _sigmoid(-(fg_ref[rows, :] + b_ref[...]))
            dfg_ref[rows, :] = dfg.astype(BF16)
            db_ref[...] += jnp.sum(dfg, axis=0, keepdims=True)

    seq = pl.BlockSpec((t, w), lambda b: (b, 0))
    vec = pl.BlockSpec((1, w), lambda b: (0, 0))
    return pl.pallas_call(
        body, name=name, grid=(batch,),
        out_shape=[jax.ShapeDtypeStruct((n, w), BF16), jax.ShapeDtypeStruct((1, w), F32)],
        in_specs=[seq, seq, seq, vec], out_specs=[seq, vec], scratch_shapes=[pltpu.VMEM((t, w), F32)],
        compiler_params=_params(("arbitrary",)))(dcum_k, dcum_q, fg, b_fg)


def _head_masks(x):
    lane = lax.broadcasted_iota(jnp.int32, x.shape, 1)
    zero = jnp.zeros(x.shape, x.dtype)
    return jnp.where(lane < HEAD_DIM, x, zero), jnp.where(lane >= HEAD_DIM, x, zero)


ATTN_BLOCK = 512


def _attn_blocks(t):
    nb = max(t // ATTN_BLOCK, 1)
    blocks = [(i * ATTN_BLOCK, ATTN_BLOCK) for i in range(nb - 1)]
    return blocks + [((nb - 1) * ATTN_BLOCK, t - (nb - 1) * ATTN_BLOCK)]


def _attn_specs(t):
    blocks = _attn_blocks(t)
    width = max(sz for _, sz in blocks)
    qkv = lambda off: pl.BlockSpec((t, LANES), lambda b, h: (b, off + h))
    cumr = pl.BlockSpec((None, None, len(blocks), 8, width), lambda b, h: (b, h, 0, 0, 0))
    return qkv, cumr


def _key_sums_to_blocks(cum, batch, t, n_pairs):
    blocks = _attn_blocks(t)
    width = max(sz for _, sz in blocks)
    cum_h = cum.reshape(batch, t, -1)[:, :, :2 * n_pairs].reshape(batch, t, n_pairs, 2)
    rows = [jnp.pad(jnp.transpose(cum_h[:, s0:s0 + sz], (0, 2, 3, 1)), ((0, 0), (0, 0), (0, 6), (0, width - sz)))
            for s0, sz in blocks]
    return jnp.stack(rows, axis=2)


def _key_sums_from_blocks(dcumr, batch, t, n_pairs):
    cols = [jnp.transpose(dcumr[:, :, j, :2, :sz], (0, 3, 1, 2)) for j, (_, sz) in enumerate(_attn_blocks(t))]
    return jnp.concatenate(cols, axis=1).reshape(batch * t, 2 * n_pairs)


def _causal(size):
    row = lax.broadcasted_iota(jnp.int32, (size, size), 0)
    col = lax.broadcasted_iota(jnp.int32, (size, size), 1)
    return row >= col


def _attn_fwd(name, qkv, cumr, batch):
    n, w3 = qkv.shape
    w = w3 // 3
    t = n // batch
    n_pairs = w // LANES
    blocks = _attn_blocks(t)

    def body(q_ref, k_ref, v_ref, cr_ref, o_ref, lse_ref):
        for i, (q0, qn) in enumerate(blocks):
            rows = slice(q0, q0 + qn)
            qs = _head_masks(q_ref[rows, :] * (0.125 * LOG2E))
            outs, lses = [], []
            for hh in range(2):
                m = jnp.full((qn, 1), NEG, F32)
                l = jnp.zeros((qn, 1), F32)
                acc = jnp.zeros((qn, LANES), F32)
                for j in range(i + 1):
                    k0, kn = blocks[j]
                    cols = slice(k0, k0 + kn)
                    s = _dot_nt(qs[hh], k_ref[cols, :]) - cr_ref[j, hh:hh + 1, 0:kn] * LOG2E
                    if j == i:
                        s = jnp.where(_causal(qn), s, NEG)
                    m_new = jnp.maximum(m, jnp.max(s, axis=1, keepdims=True))
                    alpha = jnp.exp2(m - m_new)
                    p = jnp.exp2(s - m_new)
                    l = alpha * l + jnp.sum(p, axis=1, keepdims=True)
                    acc = alpha * acc + jnp.dot(p.astype(BF16), v_ref[cols, :], preferred_element_type=F32)
                    m = m_new
                outs.append(acc / l)
                lses.append(m + jnp.log2(l))
            lane = lax.broadcasted_iota(jnp.int32, (qn, LANES), 1)
            o_ref[rows, :] = jnp.where(lane < HEAD_DIM, outs[0], outs[1]).astype(BF16)
            lse_ref[rows, :] = jnp.where(lane < HEAD_DIM, lses[0], lses[1])

    qkv_spec, cumr_spec = _attn_specs(t)
    return pl.pallas_call(
        body, name=name, grid=(batch, n_pairs),
        out_shape=[jax.ShapeDtypeStruct((n, w), BF16), jax.ShapeDtypeStruct((n, w), F32)],
        in_specs=[qkv_spec(0), qkv_spec(n_pairs), qkv_spec(2 * n_pairs), cumr_spec],
        out_specs=[qkv_spec(0), qkv_spec(0)],
        compiler_params=_params(("parallel", "parallel")))(qkv, qkv, qkv, cumr)


def _attn_bwd(name, qkv, o, do, lse, cumr, batch):
    n, w3 = qkv.shape
    w = w3 // 3
    t = n // batch
    n_pairs = w // LANES
    blocks = _attn_blocks(t)

    def body(q_ref, k_ref, v_ref, o_ref, do_ref, lse_ref, cr_ref, dq_ref, dk_ref, dv_ref, dcr_ref, dcq_ref,
             dk_acc, dv_acc):
        pair = pl.program_id(1)
        dk_acc[...] = jnp.zeros(dk_acc.shape, F32)
        dv_acc[...] = jnp.zeros(dv_acc.shape, F32)
        dcr_ref[...] = jnp.zeros(dcr_ref.shape, F32)

        @pl.when(pair == 0)
        def _():
            dcq_ref[...] = jnp.zeros(dcq_ref.shape, F32)

        for i, (q0, qn) in enumerate(blocks):
            rows = slice(q0, q0 + qn)
            qs = _head_masks(q_ref[rows, :] * 0.125)
            q2 = _head_masks(q_ref[rows, :] * (0.125 * LOG2E))
            dos = _head_masks(do_ref[rows, :])
            dq = jnp.zeros((qn, LANES), F32)
            dcq = []
            for hh in range(2):
                row_sum = jnp.zeros((qn, 1), F32)
                lse = lse_ref[rows, hh * HEAD_DIM:hh * HEAD_DIM + 1]
                delta = jnp.sum(dos[hh].astype(F32) * o_ref[rows, :].astype(F32), axis=1, keepdims=True)
                for j in range(i + 1):
                    k0, kn = blocks[j]
                    cols = slice(k0, k0 + kn)
                    s = _dot_nt(q2[hh], k_ref[cols, :]) - cr_ref[j, hh:hh + 1, 0:kn] * LOG2E
                    p = jnp.exp2(s - lse)
                    if j == i:
                        p = jnp.where(_causal(qn), p, 0.0)
                    dp = _dot_nt(dos[hh], v_ref[cols, :])
                    ds = p * (dp - delta)
                    pb, dsb = p.astype(BF16), ds.astype(BF16)
                    km = _head_masks(k_ref[cols, :])[hh]
                    dv_acc[cols, :] += _dot_tn(pb, dos[hh])
                    dk_acc[cols, :] += _dot_tn(dsb, qs[hh])
                    dq = dq + jnp.dot(dsb, km, preferred_element_type=F32)
                    dcr_ref[j, hh:hh + 1, 0:kn] -= jnp.sum(ds, axis=0, keepdims=True)
                    row_sum = row_sum + jnp.sum(ds, axis=1, keepdims=True)
                dcq.append(row_sum)
            lane = lax.broadcasted_iota(jnp.int32, (qn, LANES), 1)
            dq_ref[rows, :] = (dq * 0.125).astype(BF16)
            dcq_ref[rows, :] = jnp.where(lane == 2 * pair, dcq[0],
                                         jnp.where(lane == 2 * pair + 1, dcq[1], dcq_ref[rows, :]))
        dk_ref[...] = dk_acc[...].astype(BF16)
        dv_ref[...] = dv_acc[...].astype(BF16)

    qkv_spec, cumr_spec = _attn_specs(t)
    act = jax.ShapeDtypeStruct((n, w), BF16)
    return pl.pallas_call(
        body, name=name, grid=(batch, n_pairs),
        out_shape=[act, act, act, jax.ShapeDtypeStruct(cumr.shape, F32), jax.ShapeDtypeStruct((n, LANES), F32)],
        in_specs=[qkv_spec(0), qkv_spec(n_pairs), qkv_spec(2 * n_pairs), qkv_spec(0), qkv_spec(0), qkv_spec(0),
                  cumr_spec],
        out_specs=[qkv_spec(0), qkv_spec(0), qkv_spec(0), cumr_spec, pl.BlockSpec((t, LANES), lambda b, h: (b, 0))],
        scratch_shapes=[pltpu.VMEM((t, LANES), F32), pltpu.VMEM((t, LANES), F32)],
        compiler_params=_params(("parallel", "arbitrary")))(qkv, qkv, qkv, o, do, lse, cumr)


def _adamw(name, parts, w, m, v):
    r, c = w.shape
    tr = 128 if r % 128 == 0 else r
    tc = 256 if tr > 128 and c % 256 == 0 else c
    c1 = 1.0 - ADAM_B1 ** ADAM_STEP
    c2 = 1.0 - ADAM_B2 ** ADAM_STEP

    def body(p_ref, w_ref, m_ref, v_ref, g_ref, d_ref, m2_ref, v2_ref):
        g = p_ref[0].astype(F32)
        for s in range(1, N_DEV):
            g = g + p_ref[s].astype(F32)
        m2 = ADAM_B1 * m_ref[...] + (1.0 - ADAM_B1) * g
        v2 = ADAM_B2 * v_ref[...] + (1.0 - ADAM_B2) * (g * g)
        g_ref[...] = g
        m2_ref[...] = m2
        v2_ref[...] = v2
        d_ref[...] = -ADAM_LR * ((m2 / c1) / (jnp.sqrt(v2 / c2) + ADAM_EPS) + ADAM_WD * w_ref[...])

    blk = pl.BlockSpec((tr, tc), lambda i, j: (i, j))
    shp = jax.ShapeDtypeStruct((r, c), F32)
    return pl.pallas_call(
        body, name=name, out_shape=[shp] * 4, grid=(r // tr, c // tc),
        in_specs=[pl.BlockSpec((N_DEV, tr, tc), lambda i, j: (0, i, j)), blk, blk, blk], out_specs=[blk] * 4,
        compiler_params=_params(("parallel", "parallel")))(parts, w, m, v)


def _cat_small(vals):
    parts = []
    for name in SMALL:
        v = vals[name].reshape(1, -1).astype(F32)
        parts.append(jnp.pad(v, ((0, 0), (0, SMALL_W[name] - v.shape[1]))))
    return jnp.concatenate(parts, axis=1)


def _split_small(row, shapes):
    out, off = {}, 0
    for name in SMALL:
        out[name] = row[0, off:off + SMALL_N[name]].reshape(shapes[name])
        off += SMALL_W[name]
    return out


def _cols_from_shards(g):
    return jnp.transpose(g, (1, 0, 2)).reshape(g.shape[1], N_DEV * g.shape[2])


def _shards_from_cols(a):
    r, c = a.shape
    return jnp.transpose(a.reshape(r, N_DEV, c // N_DEV), (1, 0, 2))


def kernel(x, meta_tokens, norm_mix_gain, w_in, b_forget, w_attn_out, b_glu, conv_dw_w, conv_dw_b, conv_ln_gain, conv_ln_bias, w_conv_out, b_conv_out, w_out, norm_mlp_gain, w_mlp_up, w_mlp_down, final_norm_gain, loss_target, m_meta_tokens, m_norm_mix_gain, m_w_in, m_b_forget, m_w_attn_out, m_b_glu, m_conv_dw_w, m_conv_dw_b, m_conv_ln_gain, m_conv_ln_bias, m_w_conv_out, m_b_conv_out, m_w_out, m_norm_mlp_gain, m_w_mlp_up, m_w_mlp_down, m_final_norm_gain, v_meta_tokens, v_norm_mix_gain, v_w_in, v_b_forget, v_w_attn_out, v_b_glu, v_conv_dw_w, v_conv_dw_b, v_conv_ln_gain, v_conv_ln_bias, v_w_conv_out, v_b_conv_out, v_w_out, v_norm_mlp_gain, v_w_mlp_up, v_w_mlp_down, v_final_norm_gain):
    weights = dict(meta_tokens=meta_tokens, norm_mix_gain=norm_mix_gain, w_in=w_in, b_forget=b_forget, w_attn_out=w_attn_out, b_glu=b_glu, conv_dw_w=conv_dw_w, conv_dw_b=conv_dw_b, conv_ln_gain=conv_ln_gain, conv_ln_bias=conv_ln_bias, w_conv_out=w_conv_out, b_conv_out=b_conv_out, w_out=w_out, norm_mlp_gain=norm_mlp_gain, w_mlp_up=w_mlp_up, w_mlp_down=w_mlp_down, final_norm_gain=final_norm_gain)
    mom_m = dict(meta_tokens=m_meta_tokens, norm_mix_gain=m_norm_mix_gain, w_in=m_w_in, b_forget=m_b_forget, w_attn_out=m_w_attn_out, b_glu=m_b_glu, conv_dw_w=m_conv_dw_w, conv_dw_b=m_conv_dw_b, conv_ln_gain=m_conv_ln_gain, conv_ln_bias=m_conv_ln_bias, w_conv_out=m_w_conv_out, b_conv_out=m_b_conv_out, w_out=m_w_out, norm_mlp_gain=m_norm_mlp_gain, w_mlp_up=m_w_mlp_up, w_mlp_down=m_w_mlp_down, final_norm_gain=m_final_norm_gain)
    mom_v = dict(meta_tokens=v_meta_tokens, norm_mix_gain=v_norm_mix_gain, w_in=v_w_in, b_forget=v_b_forget, w_attn_out=v_w_attn_out, b_glu=v_b_glu, conv_dw_w=v_conv_dw_w, conv_dw_b=v_conv_dw_b, conv_ln_gain=v_conv_ln_gain, conv_ln_bias=v_conv_ln_bias, w_conv_out=v_w_conv_out, b_conv_out=v_b_conv_out, w_out=v_w_out, norm_mlp_gain=v_norm_mlp_gain, w_mlp_up=v_w_mlp_up, w_mlp_down=v_w_mlp_down, final_norm_gain=v_final_norm_gain)
    names = list(weights)
    batch, seq, d = x.shape
    t = seq + N_META
    n = batch * t
    n_pairs = d // LANES
    assert t % ROW_TILE == 0 and d == SEG

    to_rows = lambda w3: jnp.transpose(w3[0])
    w_in_t, m_in_t, v_in_t = to_rows(w_in), to_rows(m_w_in), to_rows(v_w_in)
    first = [w_in_t.astype(BF16), meta_tokens, conv_dw_w[0]]
    rest = [w_[0].astype(BF16) for w_ in (w_attn_out, w_conv_out, w_out, w_mlp_up, w_mlp_down)]
    tgt = jnp.concatenate([jnp.zeros((batch, N_META, d), F32), loss_target], axis=1).reshape(n, d)
    h0_rows = jnp.pad(x, ((0, 0), (N_META, 0), (0, 0)))
    gather_a = _exchange_start("gather_in_start", [(f_, False) for f_ in first], ks=CHIP_PEERS)
    level_1 = _exchange_wait("gather_in_wait", gather_a, [gather_a["token"], tgt, h0_rows, w_in_t, m_in_t, v_in_t] + rest,
                             fill_own=False)
    passed = _pass_on_start("gather_in_pass_start", level_1)
    w_in_g, meta_g, w_dw_g = _pass_on_wait("gather_in_pass_wait", passed, passed["token"], first)
    gather_b = _exchange_start("gather_rest_start", [(r_, False) for r_ in rest])
    n_fg = b_forget.shape[1]
    shard_w = w_in.shape[2]
    wt = w_in_g.reshape(N_DEV * shard_w, d)
    o_fg = 3 * SEG
    seg_rows = [0, SEG, 2 * SEG] + [o_fg + n_fg + i * SEG for i in range(4)]
    d_ff = w_mlp_down.shape[1] * N_DEV
    ff_blk = d_ff // N_DEV
    meta_f = _cols_from_shards(meta_g)
    w_dw = _cols_from_shards(w_dw_g)

    row2 = lambda v: v.reshape(1, -1)
    g1, g2, g3 = row2(norm_mix_gain) + gather_b["token"][0:1, 0:1], row2(norm_mlp_gain), row2(final_norm_gain)
    b_fg = jnp.pad(b_forget, ((0, 0), (0, FG_PAD - n_fg)))
    h0 = lax.dynamic_update_slice(h0_rows, jnp.broadcast_to(meta_f[None], (batch, N_META, d)), (0, 0, 0)).reshape(n, d)

    hn1 = _rms_fwd("rms1", h0, g1)
    proj = lambda name, off, width, tn, dt: _mm_nt(name, [(hn1, _a_rows(d), wt, _wt_rows(tn, off))], n, width, tn, dt)
    qkv = proj("proj_qkv", 0, 3 * SEG, SEG, BF16)
    glu = proj("proj_glu", seg_rows[3], 2 * SEG, SEG, BF16)
    gates = proj("proj_gates", seg_rows[5], 2 * SEG, SEG, BF16)
    fg = proj("proj_fg", o_fg, FG_PAD, FG_PAD, F32)

    cum = _fox_prep_fwd("fox_cumsum", fg, b_fg, batch)
    cumr = _key_sums_to_blocks(cum, batch, t, n_pairs)
    o, lse = _attn_fwd("attn_fwd", qkv, cumr, batch)
    rest = _exchange_wait("gather_rest_wait", gather_b, o)
    w_ao, w_co, w_o = [r_.reshape(d, d) for r_ in rest[:3]]
    w_up = rest[3]
    w_dn = rest[4].reshape(d_ff, d)
    a = _mm_nn("attn_out", o, w_ao, _w_cols(d, d, 0), d, d, BF16)

    c1 = _glu_conv_fwd("glu_conv", glu, b_glu, w_dw, conv_dw_b, batch)
    c3 = _ln_silu_fwd("ln_silu", c1, conv_ln_gain, conv_ln_bias)
    c = _mm_nn("conv_out", c3, w_co, _w_cols(d, d, 0), d, d, BF16)

    mrg = _merge_fwd("merge", gates, a, c, b_conv_out)
    mo = _mm_nn("mix_out", mrg, w_o, _w_cols(d, d, 0), d, d, F32)
    h1, hn2 = _rms_fwd("resid_rms2", h0, g2, res=mo)
    per = ff_blk // 512
    act = _mm_nn("mlp_up", hn2, w_up, pl.BlockSpec((None, d, 512), lambda i, j: (j // per, 0, j % per)),
                 d_ff, 512, BF16, relu2=True)
    dn = _mm_nn("mlp_down", act, w_dn, _w_cols(d_ff, d, 0), d, d, F32, tm=ROW_TILE)
    dh2, dh2b, loss_blk, dg3 = _final("final_loss", h1, dn, tgt, g3, batch)

    dup = _mm_nt("d_mlp_down", [(dh2b, _a_rows(d), w_dn, _w_rows(d, d))], n, d_ff, d, BF16, relu_bwd_of=act)
    dw_dn = _grad_w("gw_mlp_down", act, dh2b)
    dhn2 = _mm_nt("d_mlp_up", [(dup, _a_rows(ff_blk, g), w_up, pl.BlockSpec((None, 512, ff_blk), lambda i, j, g=g: (g, j, 0)))
                               for g in range(N_DEV)], n, d, 512, BF16)
    dw_up = _mm_tn("gw_mlp_up", hn2, lambda a_: 0, d, dup, lambda b_: b_, ff_blk, (N_DEV, d, ff_blk),
                   pl.BlockSpec((None, d, ff_blk), lambda a_, b_: (b_, 0, 0)), (1, N_DEV))
    scatter_1 = _exchange_start("scatter_mlp_start", [(dw_dn.reshape(N_DEV, ff_blk, d), True), (dw_up, True)])
    dh1, dg2, dh1b = _rms_bwd("rms2_bwd", dhn2, h1, g2 + scatter_1["token"][0:1, 0:1], dh2, batch, with_bf16=True)

    dm = _mm_nt("d_mix_out", [(dh1b, _a_rows(d), w_o, _w_rows(d, d))], n, d, d, BF16)
    dw_o = _grad_w("gw_mix_out", mrg, dh1b)
    da, dc, dga, dgc, dbco = _merge_bwd("merge_bwd", dm, gates, a, c, b_conv_out)

    do = _mm_nt("d_attn_out", [(da, _a_rows(d), w_ao, _w_rows(d, d))], n, d, d, BF16)
    dw_ao = _grad_w("gw_attn_out", o, da)
    dc3 = _mm_nt("d_conv_out", [(dc, _a_rows(d), w_co, _w_rows(d, d))], n, d, d, BF16)
    dw_co = _grad_w("gw_conv_out", c3, dc)

    scatter_2 = _exchange_start("scatter_mix_start", [(dw_.reshape(N_DEV, d // N_DEV, d), True)
                                                      for dw_ in (dw_o, dw_ao, dw_co)])
    dc1, dg_ln, db_ln = _ln_silu_bwd("ln_silu_bwd", dc3, c1, conv_ln_gain + scatter_2["token"][0:1, 0:1],
                                     conv_ln_bias)
    dglu_a, dglu_g, dw_dw, db_dw, dbg_a, dbg_g = _glu_conv_bwd("glu_conv_bwd", dc1, glu, b_glu, w_dw, batch)

    dq, dk, dv, dcumr, dcum_q = _attn_bwd("attn_bwd", qkv, o, do, lse, cumr, batch)
    dcum_k = jnp.pad(_key_sums_from_blocks(dcumr, batch, t, n_pairs), ((0, 0), (0, FG_PAD - 2 * n_pairs)))
    dfg, db_fg = _fox_prep_bwd("fox_cumsum_bwd", dcum_k, dcum_q, fg, b_fg, batch)

    segs = [dq, dk, dv, dglu_a, dglu_g, dga, dgc]
    gw_t = [_grad_w("gw_in_%d" % i, s_, hn1) for i, s_ in enumerate(segs)]
    gw_fg = _grad_w("gw_in_fg", dfg, hn1)[:n_fg]
    dw_in_t = jnp.concatenate(gw_t[:3] + [gw_fg] + gw_t[3:], axis=0).reshape(N_DEV, shard_w, d)
    scatter_3 = _exchange_start("scatter_in_start", [(dw_in_t, True)])
    pairs = [(s_, _a_rows(SEG, 0, ROW_TILE), wt, _wt_block(SEG, seg_rows[i], 512), "nn") for i, s_ in enumerate(segs)]
    pairs.append((dfg, _a_rows(FG_PAD, 0, ROW_TILE), wt, _wt_block(FG_PAD, o_fg, 512), "nn"))
    dhn1 = _mm_nt("d_proj_in", pairs, n, d, 512, BF16, tm=ROW_TILE, after=scatter_3["token"])
    dh0, dg1, dmeta = _rms_bwd("rms1_bwd", dhn1, h0, g1, dh1, batch, with_meta=True)
    grad_x = dh0.reshape(batch, t, d)[:, N_META:, :]

    small_g = dict(norm_mix_gain=dg1, b_forget=db_fg[:, :n_fg], b_glu=jnp.concatenate([dbg_a, dbg_g], axis=1),
                   conv_dw_b=db_dw, conv_ln_gain=dg_ln, conv_ln_bias=db_ln, b_conv_out=dbco, norm_mlp_gain=dg2,
                   final_norm_gain=dg3)
    scatter_4 = _exchange_start("scatter_small_start", [
        (_shards_from_cols(dmeta), True), (_shards_from_cols(dw_dw), True), (_cat_small(small_g), False),
        (loss_blk[0:1, :], False)])

    grads, deltas, new_m, new_v = {}, {}, {}, {}

    def update(k, parts):
        shp = weights[k].shape
        if k == "w_in":
            res_ = _adamw("adamw_" + k, parts, w_in_t, m_in_t, v_in_t)
            res_ = [jnp.transpose(r) for r in res_]
        else:
            w2 = lambda arr: arr.reshape(parts.shape[1:])
            res_ = _adamw("adamw_" + k, parts, w2(weights[k]), w2(mom_m[k]), w2(mom_v[k]))
        grads[k], deltas[k], new_m[k], new_v[k] = [r.reshape(shp) for r in res_]

    for k, parts in zip(("w_mlp_down", "w_mlp_up"), _exchange_wait("scatter_mlp_wait", scatter_1, scatter_4["token"])):
        update(k, parts)
    for k, parts in zip(("w_out", "w_attn_out", "w_conv_out"),
                        _exchange_wait("scatter_mix_wait", scatter_2, deltas["w_mlp_up"])):
        update(k, parts)
    update("w_in", _exchange_wait("scatter_in_wait", scatter_3, deltas["w_conv_out"])[0])
    reduced = _exchange_wait("scatter_small_wait", scatter_4, deltas["w_in"])
    loss = jnp.sum(reduced.pop()[:, 0, 0])
    for k, parts in zip(("meta_tokens", "conv_dw_w"), reduced[:-1]):
        update(k, parts)
    res = _adamw("adamw_small", reduced[-1], _cat_small(weights), _cat_small(mom_m), _cat_small(mom_v))
    shapes = {k: weights[k].shape for k in SMALL}
    for dst, r in zip((grads, deltas, new_m, new_v), res):
        dst.update(_split_small(r, shapes))

    return (loss, grad_x, *[grads[k] for k in names], *[deltas[k] for k in names],
            *[new_m[k] for k in names], *[new_v[k] for k in names])
```

```python
import functools

import jax
import jax.numpy as jnp
from jax import lax
from jax.experimental import pallas as pl
from jax.experimental.pallas import tpu as pltpu

F32, BF16 = jnp.float32, jnp.bfloat16
N_DEV = 8
N_META = 16
HEAD_DIM = 64
LANES = 128
CONV_W = 31
RMS_EPS = 1e-6
LN_EPS = 1e-5
ROW_TILE = 688
MM_TM = 2 * ROW_TILE
SEG = 1024
FG_PAD = 128
VMEM_LIMIT = 56 * 1024 * 1024
ADAM_LR, ADAM_B1, ADAM_B2, ADAM_EPS, ADAM_WD, ADAM_STEP = 0.001, 0.9, 0.999, 1e-08, 0.01, 10
NEG = -1e30
LOG2E = 1.4426950408889634

SMALL = ("norm_mix_gain", "b_forget", "b_glu", "conv_dw_b", "conv_ln_gain", "conv_ln_bias", "b_conv_out",
         "norm_mlp_gain", "final_norm_gain")
SMALL_W = {"norm_mix_gain": 1024, "b_forget": 128, "b_glu": 2048, "conv_dw_b": 1024, "conv_ln_gain": 1024,
           "conv_ln_bias": 1024, "b_conv_out": 1024, "norm_mlp_gain": 1024, "final_norm_gain": 1024}
SMALL_N = {"norm_mix_gain": 1024, "b_forget": 16, "b_glu": 2048, "conv_dw_b": 1024, "conv_ln_gain": 1024,
           "conv_ln_bias": 1024, "b_conv_out": 1024, "norm_mlp_gain": 1024, "final_norm_gain": 1024}


def _params(sem=None):
    return pltpu.CompilerParams(dimension_semantics=sem, vmem_limit_bytes=VMEM_LIMIT)


def _sigmoid(x):
    return 1.0 / (1.0 + jnp.exp(-x))


def _dot_nt(a, b):
    return lax.dot_general(a, b, (((1,), (1,)), ((), ())), preferred_element_type=F32)


def _dot_tn(a, b):
    return lax.dot_general(a, b, (((0,), (0,)), ((), ())), preferred_element_type=F32)


HBM_SPEC = pl.BlockSpec(memory_space=pltpu.HBM)
SEM_SPEC = pl.BlockSpec(memory_space=pltpu.SEMAPHORE)
DATAFLOW = pltpu.SideEffectType.DATAFLOW_SIDE_EFFECTING


def _device_index():
    return 4 * lax.axis_index("x") + 2 * lax.axis_index("y") + lax.axis_index("c")


def _peers():
    x, y, c = lax.axis_index("x"), lax.axis_index("y"), lax.axis_index("c")
    out = []
    for k in range(1, N_DEV):
        px = 1 - x if k & 4 else x
        py = 1 - y if k & 2 else y
        pc = 1 - c if k & 1 else c
        out.append((k, (px, py, pc), 4 * px + 2 * py + pc))
    return out


def _peer_copy(per_dest, src_ref, land_ref, send_sems, recv_sems, a, k, dev, peer):
    src = src_ref.at[peer] if per_dest else src_ref
    return pltpu.make_async_remote_copy(
        src_ref=src, dst_ref=land_ref.at[_device_index()], send_sem=send_sems.at[a * (N_DEV - 1) + k - 1],
        recv_sem=recv_sems.at[a * (N_DEV - 1) + k - 1], device_id=dev, device_id_type=pl.DeviceIdType.MESH)


ALL_PEERS = tuple(range(1, N_DEV))
CHIP_PEERS = (1, 2, 4, 6)
FAR_PEERS = (2, 4, 6)


def _own_copy(per_dest, src_ref, land_ref, send_sems, n, a):
    me = _device_index()
    return pltpu.make_async_copy(src_ref.at[me] if per_dest else src_ref, land_ref.at[me],
                                 send_sems.at[n * (N_DEV - 1) + a])


def _exchange_start(name, items, ks=ALL_PEERS):
    n = len(items)
    per_dest = [it[1] for it in items]

    def body(*refs):
        srcs, lands = refs[:n], refs[n:2 * n]
        send_sems, recv_sems, token = refs[2 * n], refs[2 * n + 1], refs[-1]
        for a in range(n):
            _own_copy(per_dest[a], srcs[a], lands[a], send_sems, n, a).start()
            for k, dev, peer in _peers():
                if k in ks:
                    _peer_copy(per_dest[a], srcs[a], lands[a], send_sems, recv_sems, a, k, dev, peer).start()
        token[...] = jnp.zeros(token.shape, F32)

    srcs = [pltpu.with_memory_space_constraint(it[0], pltpu.HBM) for it in items]
    lands = []
    for arr, pd in items:
        shp = arr.shape if pd else (N_DEV,) + arr.shape
        lands.append(pltpu.with_memory_space_constraint(lax.empty(shp, arr.dtype), pltpu.HBM))
    sems = pltpu.SemaphoreType.DMA((n * N_DEV,))
    res = pl.pallas_call(
        body, name=name,
        out_shape=(sems, sems, *[pltpu.HBM(a_.shape, a_.dtype) for a_ in srcs + lands],
                   jax.ShapeDtypeStruct((8, 128), F32)),
        in_specs=[HBM_SPEC] * (2 * n),
        out_specs=(SEM_SPEC, SEM_SPEC, *[HBM_SPEC] * (2 * n), pl.BlockSpec(memory_space=pltpu.VMEM)),
        input_output_aliases={i: 2 + i for i in range(2 * n)},
        compiler_params=pltpu.CompilerParams(has_side_effects=DATAFLOW),
    )(*srcs, *lands)
    return dict(per_dest=per_dest, ks=ks, send=res[0], recv=res[1], srcs=list(res[2:2 + n]),
                lands=list(res[2 + n:2 + 2 * n]), token=res[-1])


def _exchange_wait(name, started, after):
    per_dest = started["per_dest"]
    n = len(per_dest)

    def body(*refs):
        srcs, lands = refs[:n], refs[n:2 * n]
        send_sems, recv_sems = refs[2 * n], refs[2 * n + 1]
        for a in range(n):
            _own_copy(per_dest[a], srcs[a], lands[a], send_sems, n, a).wait()
            for k, dev, peer in _peers():
                if k in started["ks"]:
                    cp = _peer_copy(per_dest[a], srcs[a], lands[a], send_sems, recv_sems, a, k, dev, peer)
                    cp.wait_send()
                    cp.wait_recv()

    bufs = started["srcs"] + started["lands"]
    after = list(after) if isinstance(after, (list, tuple)) else [after]
    res = pl.pallas_call(
        body, name=name, out_shape=tuple(pltpu.HBM(b_.shape, b_.dtype) for b_ in bufs),
        in_specs=[HBM_SPEC] * (2 * n) + [SEM_SPEC, SEM_SPEC] + [pl.BlockSpec(memory_space=pl.ANY)] * len(after),
        out_specs=tuple([HBM_SPEC] * (2 * n)), input_output_aliases={i: i for i in range(2 * n)},
        compiler_params=pltpu.CompilerParams(has_side_effects=DATAFLOW),
    )(*bufs, started["send"], started["recv"], *after)
    return list(res[n:])


def _pass_on_copy(land_ref, send_sems, recv_sems, a, idx, slot):
    sibling = (lax.axis_index("x"), lax.axis_index("y"), 1 - lax.axis_index("c"))
    return pltpu.make_async_remote_copy(
        src_ref=land_ref.at[slot], dst_ref=land_ref.at[slot], send_sem=send_sems.at[a * len(FAR_PEERS) + idx],
        recv_sem=recv_sems.at[a * len(FAR_PEERS) + idx], device_id=sibling, device_id_type=pl.DeviceIdType.MESH)


def _pass_on_start(name, lands):
    n = len(lands)

    def body(*refs):
        send_sems, recv_sems, token = refs[n], refs[n + 1], refs[-1]
        slots = {k: peer for k, _, peer in _peers()}
        for a in range(n):
            for idx, k in enumerate(FAR_PEERS):
                _pass_on_copy(refs[a], send_sems, recv_sems, a, idx, slots[k]).start()
        token[...] = jnp.zeros(token.shape, F32)

    lands = [pltpu.with_memory_space_constraint(l_, pltpu.HBM) for l_ in lands]
    sems = pltpu.SemaphoreType.DMA((n * len(FAR_PEERS),))
    res = pl.pallas_call(
        body, name=name,
        out_shape=(sems, sems, *[pltpu.HBM(l_.shape, l_.dtype) for l_ in lands], jax.ShapeDtypeStruct((8, 128), F32)),
        in_specs=[HBM_SPEC] * n, out_specs=(SEM_SPEC, SEM_SPEC, *[HBM_SPEC] * n, pl.BlockSpec(memory_space=pltpu.VMEM)),
        input_output_aliases={i: 2 + i for i in range(n)},
        compiler_params=pltpu.CompilerParams(has_side_effects=DATAFLOW),
    )(*lands)
    return dict(send=res[0], recv=res[1], lands=list(res[2:2 + n]), token=res[-1])


def _pass_on_wait(name, passed, after):
    n = len(passed["lands"])

    def body(*refs):
        send_sems, recv_sems = refs[n], refs[n + 1]
        slots = {k: peer for k, _, peer in _peers()}
        for a in range(n):
            for idx, k in enumerate(FAR_PEERS):
                _pass_on_copy(refs[a], send_sems, recv_sems, a, idx, slots[k]).wait_send()
                _pass_on_copy(refs[a], send_sems, recv_sems, a, idx, slots[k ^ 1]).wait_recv()

    res = pl.pallas_call(
        body, name=name, out_shape=tuple(pltpu.HBM(l_.shape, l_.dtype) for l_ in passed["lands"]),
        in_specs=[HBM_SPEC] * n + [SEM_SPEC, SEM_SPEC, pl.BlockSpec(memory_space=pl.ANY)],
        out_specs=tuple([HBM_SPEC] * n), input_output_aliases={i: i for i in range(n)},
        compiler_params=pltpu.CompilerParams(has_side_effects=DATAFLOW),
    )(*passed["lands"], passed["send"], passed["recv"], after)
    return list(res)


def _mm_nn(name, x, w, w_spec, n_out, tn, out_dtype, relu2=False, tm=MM_TM):
    m, k = x.shape

    def body(x_ref, w_ref, *outs):
        acc = jnp.dot(x_ref[...], w_ref[...], preferred_element_type=F32)
        if relu2:
            acc = jnp.maximum(acc, 0.0)
            acc = acc * acc
        outs[0][...] = acc.astype(outs[0].dtype)

    o_spec = pl.BlockSpec((tm, tn), lambda i, j: (i, j))
    return pl.pallas_call(
        body, name=name, out_shape=jax.ShapeDtypeStruct((m, n_out), out_dtype), grid=(m // tm, n_out // tn),
        in_specs=[pl.BlockSpec((tm, k), lambda i, j: (i, 0)), w_spec],
        out_specs=o_spec, compiler_params=_params(("parallel", "parallel")),
    )(x, w)


def _mm_nt(name, pairs, m, n_out, tn, out_dtype, relu_bwd_of=None, tm=MM_TM, after=None):
    np_ = len(pairs)

    def body(*refs):
        acc = None
        for p in range(np_):
            if len(pairs[p]) == 5:
                d = jnp.dot(refs[2 * p][...], refs[2 * p + 1][...], preferred_element_type=F32)
            else:
                d = _dot_nt(refs[2 * p][...], refs[2 * p + 1][...])
            acc = d if acc is None else acc + d
        if relu_bwd_of is not None:
            acc = acc * (2.0 * jnp.sqrt(refs[2 * np_][...].astype(F32)))
        refs[-1][...] = acc.astype(refs[-1].dtype)

    o_spec = pl.BlockSpec((tm, tn), lambda i, j: (i, j))
    operands, specs = [], []
    for pair in pairs:
        operands += [pair[0], pair[2]]
        specs += [pair[1], pair[3]]
    if relu_bwd_of is not None:
        operands.append(relu_bwd_of)
        specs.append(o_spec)
    if after is not None:
        operands.append(after)
        specs.append(pl.BlockSpec((8, 128), lambda i, j: (0, 0)))
    return pl.pallas_call(
        body, name=name, out_shape=jax.ShapeDtypeStruct((m, n_out), out_dtype), grid=(m // tm, n_out // tn),
        in_specs=specs, out_specs=o_spec, compiler_params=_params(("parallel", "parallel")),
    )(*operands)


def _mm_tn(name, x, x_col, ta, dy, dy_col, tb, out_shape, out_spec, grid_ab):
    m = x.shape[0]

    def body(x_ref, dy_ref, o_ref):
        o_ref[...] = _dot_tn(x_ref[...], dy_ref[...]).astype(BF16)

    return pl.pallas_call(
        body, name=name, out_shape=jax.ShapeDtypeStruct(out_shape, BF16), grid=grid_ab,
        in_specs=[pl.BlockSpec((m, ta), lambda a, b: (0, x_col(a))),
                  pl.BlockSpec((m, tb), lambda a, b: (0, dy_col(b)))],
        out_specs=out_spec, compiler_params=_params(("parallel", "parallel")),
    )(x, dy)


def _w_cols(k, tn, off_blocks):
    return pl.BlockSpec((k, tn), lambda i, j: (0, off_blocks + j))


def _a_rows(kw, col_block=0, tm=MM_TM):
    return pl.BlockSpec((tm, kw), lambda i, j: (i, col_block))


def _w_rows(tn, kw, col_block=0):
    return pl.BlockSpec((tn, kw), lambda i, j: (j, col_block))


def _wt_rows(tn, off):
    return pl.BlockSpec((pl.Element(tn), pl.Element(SEG)), lambda i, j: (pl.multiple_of(off + tn * j, 16), 0))


def _wt_block(k, off, tn):
    return pl.BlockSpec((pl.Element(k), pl.Element(tn)), lambda i, j: (off, pl.multiple_of(tn * j, 128)))


def _grad_w(name, x, dy):
    na, nb = x.shape[1], dy.shape[1]
    ta, tb = min(na, 1024), min(nb, 512)
    return _mm_tn(name, x, lambda a: a, ta, dy, lambda b: b, tb, (na, nb),
                  pl.BlockSpec((ta, tb), lambda a, b: (a, b)), (na // ta, nb // tb))


def _row_spec(width):
    return pl.BlockSpec((ROW_TILE, width), lambda i: (i, 0))


def _vec_spec(width):
    return pl.BlockSpec((1, width), lambda i: (0, 0))


def _rms_fwd(name, h, g, res=None):
    n, d = h.shape

    def body(*refs):
        if res is None:
            h_ref, g_ref, hn_ref = refs
            hv = h_ref[...]
        else:
            h_ref, r_ref, g_ref, hs_ref, hn_ref = refs
            hv = h_ref[...] + r_ref[...]
            hs_ref[...] = hv
        r = lax.rsqrt(jnp.mean(hv * hv, axis=-1, keepdims=True) + RMS_EPS)
        hn_ref[...] = (hv * r * g_ref[...]).astype(BF16)

    ins = [h, g] if res is None else [h, res, g]
    in_specs = [_row_spec(d), _vec_spec(d)] if res is None else [_row_spec(d), _row_spec(d), _vec_spec(d)]
    hn_shape = jax.ShapeDtypeStruct((n, d), BF16)
    if res is None:
        out_shape, out_specs = hn_shape, _row_spec(d)
    else:
        out_shape, out_specs = [jax.ShapeDtypeStruct((n, d), F32), hn_shape], [_row_spec(d), _row_spec(d)]
    return pl.pallas_call(body, name=name, out_shape=out_shape, grid=(n // ROW_TILE,), in_specs=in_specs,
                          out_specs=out_specs, compiler_params=_params(("parallel",)))(*ins)


def _rms_bwd(name, dhn, h, g, dres, batch, with_bf16=False, with_meta=False):
    n, d = h.shape
    t = n // batch
    nt = t // ROW_TILE

    def body(dhn_ref, h_ref, g_ref, dres_ref, *outs):
        first = (pl.program_id(0) == 0) & (pl.program_id(1) == 0)
        hv = h_ref[...]
        r = lax.rsqrt(jnp.mean(hv * hv, axis=-1, keepdims=True) + RMS_EPS)
        nrm = hv * r
        dhn = dhn_ref[...].astype(F32)
        dn = dhn * g_ref[...]
        dh = dres_ref[...] + r * (dn - nrm * jnp.mean(dn * nrm, axis=-1, keepdims=True))
        outs[0][...] = dh
        dg_ref = outs[1]

        @pl.when(first)
        def _():
            dg_ref[...] = jnp.zeros(dg_ref.shape, F32)

        dg_ref[...] += jnp.sum(dhn * nrm, axis=0, keepdims=True)
        nxt = 2
        if with_bf16:
            outs[nxt][...] = dh.astype(BF16)
            nxt += 1
        if with_meta:
            meta_ref = outs[nxt]

            @pl.when(first)
            def _():
                meta_ref[...] = jnp.zeros(meta_ref.shape, F32)

            @pl.when(pl.program_id(1) == 0)
            def _():
                meta_ref[...] += dh[0:N_META, :]

    row = pl.BlockSpec((ROW_TILE, d), lambda b, j: (b * nt + j, 0))
    vec = pl.BlockSpec((1, d), lambda b, j: (0, 0))
    shapes = [jax.ShapeDtypeStruct((n, d), F32), jax.ShapeDtypeStruct((1, d), F32)]
    specs = [row, vec]
    if with_bf16:
        shapes.append(jax.ShapeDtypeStruct((n, d), BF16))
        specs.append(row)
    if with_meta:
        shapes.append(jax.ShapeDtypeStruct((N_META, d), F32))
        specs.append(pl.BlockSpec((N_META, d), lambda b, j: (0, 0)))
    return pl.pallas_call(body, name=name, out_shape=shapes, grid=(batch, nt), in_specs=[row, row, vec, row],
                          out_specs=specs, compiler_params=_params(("arbitrary", "arbitrary")))(dhn, h, g, dres)


def _final(name, h1, dn, tgt, g, batch):
    n, d = h1.shape
    t = n // batch
    nt = t // ROW_TILE

    def body(h1_ref, dn_ref, tgt_ref, g_ref, dh_ref, dhb_ref, loss_ref, dg_ref):
        first = (pl.program_id(0) == 0) & (pl.program_id(1) == 0)
        hv = h1_ref[...] + dn_ref[...]
        r = lax.rsqrt(jnp.mean(hv * hv, axis=-1, keepdims=True) + RMS_EPS)
        nrm = hv * r
        gv = g_ref[...]
        pos = pl.program_id(1) * ROW_TILE + lax.broadcasted_iota(jnp.int32, (ROW_TILE, 1), 0)
        diff = jnp.where(pos >= N_META, nrm * gv - tgt_ref[...], 0.0)
        dy = diff * (1.0 / d)

        @pl.when(first)
        def _():
            loss_ref[...] = jnp.zeros(loss_ref.shape, F32)
            dg_ref[...] = jnp.zeros(dg_ref.shape, F32)

        loss_ref[...] += jnp.full(loss_ref.shape, 0.5 / d, F32) * jnp.sum(diff * diff)
        dg_ref[...] += jnp.sum(dy * nrm, axis=0, keepdims=True)
        dng = dy * gv
        dh = r * (dng - nrm * jnp.mean(dng * nrm, axis=-1, keepdims=True))
        dh_ref[...] = dh
        dhb_ref[...] = dh.astype(BF16)

    row = pl.BlockSpec((ROW_TILE, d), lambda b, j: (b * nt + j, 0))
    vec = pl.BlockSpec((1, d), lambda b, j: (0, 0))
    return pl.pallas_call(
        body, name=name, grid=(batch, nt), in_specs=[row, row, row, vec],
        out_shape=[jax.ShapeDtypeStruct((n, d), F32), jax.ShapeDtypeStruct((n, d), BF16),
                   jax.ShapeDtypeStruct((8, 128), F32), jax.ShapeDtypeStruct((1, d), F32)],
        out_specs=[row, row, pl.BlockSpec((8, 128), lambda b, j: (0, 0)), vec],
        compiler_params=_params(("arbitrary", "arbitrary")))(h1, dn, tgt, g)


def _ln_silu_fwd(name, c1, g, b):
    n, d = c1.shape

    def body(c_ref, g_ref, b_ref, o_ref):
        xv = c_ref[...]
        xc = xv - jnp.mean(xv, axis=-1, keepdims=True)
        rstd = lax.rsqrt(jnp.mean(xc * xc, axis=-1, keepdims=True) + LN_EPS)
        c2 = xc * rstd * g_ref[...] + b_ref[...]
        o_ref[...] = (c2 * _sigmoid(c2)).astype(BF16)

    return pl.pallas_call(body, name=name, out_shape=jax.ShapeDtypeStruct((n, d), BF16), grid=(n // ROW_TILE,),
                          in_specs=[_row_spec(d), _vec_spec(d), _vec_spec(d)], out_specs=_row_spec(d),
                          compiler_params=_params(("parallel",)))(c1, g, b)


def _ln_silu_bwd(name, dc3, c1, g, b):
    n, d = c1.shape

    def body(d_ref, c_ref, g_ref, b_ref, dc1_ref, dg_ref, db_ref):
        xv = c_ref[...]
        xc = xv - jnp.mean(xv, axis=-1, keepdims=True)
        rstd = lax.rsqrt(jnp.mean(xc * xc, axis=-1, keepdims=True) + LN_EPS)
        xh = xc * rstd
        c2 = xh * g_ref[...] + b_ref[...]
        s = _sigmoid(c2)
        dc2 = d_ref[...].astype(F32) * (s * (1.0 + c2 * (1.0 - s)))

        @pl.when(pl.program_id(0) == 0)
        def _():
            dg_ref[...] = jnp.zeros(dg_ref.shape, F32)
            db_ref[...] = jnp.zeros(db_ref.shape, F32)

        dg_ref[...] += jnp.sum(dc2 * xh, axis=0, keepdims=True)
        db_ref[...] += jnp.sum(dc2, axis=0, keepdims=True)
        dxh = dc2 * g_ref[...]
        dc1_ref[...] = rstd * (dxh - jnp.mean(dxh, axis=-1, keepdims=True)
                               - xh * jnp.mean(dxh * xh, axis=-1, keepdims=True))

    return pl.pallas_call(
        body, name=name, grid=(n // ROW_TILE,),
        out_shape=[jax.ShapeDtypeStruct((n, d), F32), jax.ShapeDtypeStruct((1, d), F32),
                   jax.ShapeDtypeStruct((1, d), F32)],
        in_specs=[_row_spec(d), _row_spec(d), _vec_spec(d), _vec_spec(d)],
        out_specs=[_row_spec(d), _vec_spec(d), _vec_spec(d)],
        compiler_params=_params(("arbitrary",)))(dc3, c1, g, b)


MERGE_TC = 512


def _merge_fwd(name, gates, a, c, b_co):
    n, d = a.shape
    nc = d // MERGE_TC

    def body(ga_ref, gc_ref, a_ref, c_ref, b_ref, m_ref):
        f32 = lambda r_: r_[...].astype(F32)
        m = _sigmoid(f32(ga_ref)) * f32(a_ref) + _sigmoid(f32(gc_ref)) * (f32(c_ref) + b_ref[...])
        m_ref[...] = m.astype(BF16)

    blk = lambda off: pl.BlockSpec((ROW_TILE, MERGE_TC), lambda i, j: (i, off + j))
    return pl.pallas_call(
        body, name=name, out_shape=jax.ShapeDtypeStruct((n, d), BF16), grid=(n // ROW_TILE, nc),
        in_specs=[blk(0), blk(nc), blk(0), blk(0), pl.BlockSpec((1, MERGE_TC), lambda i, j: (0, j))],
        out_specs=blk(0), compiler_params=_params(("parallel", "parallel")))(gates, gates, a, c, b_co)


def _merge_bwd(name, dm, gates, a, c, b_co):
    n, d = a.shape
    nc = d // MERGE_TC

    def body(dm_ref, ga_ref, gc_ref, a_ref, c_ref, b_ref, da_ref, dc_ref, dga_ref, dgc_ref, dbco_ref):
        f32 = lambda r_: r_[...].astype(F32)
        dmv = f32(dm_ref)
        sa, sc = _sigmoid(f32(ga_ref)), _sigmoid(f32(gc_ref))
        dc = dmv * sc
        da_ref[...] = (dmv * sa).astype(BF16)
        dc_ref[...] = dc.astype(BF16)
        dga_ref[...] = (dmv * f32(a_ref) * sa * (1.0 - sa)).astype(BF16)
        dgc_ref[...] = (dmv * (f32(c_ref) + b_ref[...]) * sc * (1.0 - sc)).astype(BF16)

        @pl.when(pl.program_id(1) == 0)
        def _():
            dbco_ref[...] = jnp.zeros(dbco_ref.shape, F32)

        dbco_ref[...] += jnp.sum(dc, axis=0, keepdims=True)

    blk = lambda off: pl.BlockSpec((ROW_TILE, MERGE_TC), lambda j, i: (i, off + j))
    vec = pl.BlockSpec((1, MERGE_TC), lambda j, i: (0, j))
    act = jax.ShapeDtypeStruct((n, d), BF16)
    return pl.pallas_call(
        body, name=name, grid=(nc, n // ROW_TILE),
        out_shape=[act, act, act, act, jax.ShapeDtypeStruct((1, d), F32)],
        in_specs=[blk(0), blk(0), blk(nc), blk(0), blk(0), vec],
        out_specs=[blk(0), blk(0), blk(0), blk(0), vec],
        compiler_params=_params(("parallel", "arbitrary")))(dm, gates, gates, a, c, b_co)


CONV_TC = 128
CONV_HALO = 32


def _conv_chunk(t):
    return 48 if t % 48 == 0 else 32 if t % 32 == 0 else 16


def _fold8(x):
    out = x[0:8]
    for k in range(1, x.shape[0] // 8):
        out = out + x[8 * k:8 * k + 8]
    return out


def _glu_conv_fwd(name, glu, b_glu, w_dw, b_dw, batch):
    n, c2 = glu.shape
    c = c2 // 2
    t = n // batch
    nc = c // CONV_TC

    def body(a_ref, gt_ref, ba_ref, bg_ref, w_ref, bdw_ref, o_ref, pad_ref):
        u = (a_ref[...].astype(F32) + ba_ref[...]) * _sigmoid(gt_ref[...].astype(F32) + bg_ref[...])
        pad_ref[0:CONV_HALO, :] = jnp.zeros((CONV_HALO, CONV_TC), F32)
        pad_ref[CONV_HALO:CONV_HALO + t, :] = u
        ch = _conv_chunk(t)
        for r0 in range(0, t, ch):
            acc = jnp.zeros((ch, CONV_TC), F32) + bdw_ref[...]
            for j in range(CONV_W):
                off = r0 + CONV_HALO - (CONV_W - 1) + j
                acc = acc + w_ref[j:j + 1, :] * pad_ref[off:off + ch, :]
            o_ref[r0:r0 + ch, :] = acc

    seq = lambda off: pl.BlockSpec((t, CONV_TC), lambda b, j: (b, off + j))
    vec = lambda off: pl.BlockSpec((1, CONV_TC), lambda b, j: (0, off + j))
    return pl.pallas_call(
        body, name=name, out_shape=jax.ShapeDtypeStruct((n, c), F32), grid=(batch, nc),
        in_specs=[seq(0), seq(nc), vec(0), vec(nc), pl.BlockSpec((CONV_W, CONV_TC), lambda b, j: (0, j)), vec(0)],
        out_specs=seq(0), scratch_shapes=[pltpu.VMEM((t + CONV_HALO, CONV_TC), F32)],
        compiler_params=_params(("parallel", "parallel")))(glu, glu, b_glu, b_glu, w_dw, b_dw)


def _glu_conv_bwd(name, dc1, glu, b_glu, w_dw, batch):
    n, c2 = glu.shape
    c = c2 // 2
    t = n // batch
    nc = c // CONV_TC

    def body(d_ref, a_ref, gt_ref, ba_ref, bg_ref, w_ref, dga_ref, dgg_ref, dw_ref, dbdw_ref, dba_ref, dbg_ref,
             padu_ref, padd_ref):
        av = a_ref[...].astype(F32) + ba_ref[...]
        sg = _sigmoid(gt_ref[...].astype(F32) + bg_ref[...])
        dc = d_ref[...]
        padu_ref[0:CONV_HALO, :] = jnp.zeros((CONV_HALO, CONV_TC), F32)
        padu_ref[CONV_HALO:CONV_HALO + t, :] = av * sg
        padd_ref[0:t, :] = dc
        padd_ref[t:t + CONV_HALO, :] = jnp.zeros((CONV_HALO, CONV_TC), F32)

        @pl.when(pl.program_id(1) == 0)
        def _():
            dw_ref[...] = jnp.zeros(dw_ref.shape, F32)
            dbdw_ref[...] = jnp.zeros(dbdw_ref.shape, F32)
            dba_ref[...] = jnp.zeros(dba_ref.shape, F32)
            dbg_ref[...] = jnp.zeros(dbg_ref.shape, F32)

        ch = _conv_chunk(t)
        zero8 = jnp.zeros((8, CONV_TC), F32)
        dw_acc = [zero8] * CONV_W
        sum_dc, sum_a, sum_g = zero8, zero8, zero8
        for r0 in range(0, t, ch):
            dcc = d_ref[r0:r0 + ch, :]
            du = jnp.zeros((ch, CONV_TC), F32)
            for j in range(CONV_W):
                back = r0 + CONV_W - 1 - j
                du = du + w_ref[j:j + 1, :] * padd_ref[back:back + ch, :]
                off = r0 + CONV_HALO - (CONV_W - 1) + j
                dw_acc[j] = dw_acc[j] + _fold8(dcc * padu_ref[off:off + ch, :])
            sgc = _sigmoid(gt_ref[r0:r0 + ch, :].astype(F32) + bg_ref[...])
            dga = du * sgc
            dgg = du * padu_ref[CONV_HALO + r0:CONV_HALO + r0 + ch, :] * (1.0 - sgc)
            dga_ref[r0:r0 + ch, :] = dga.astype(BF16)
            dgg_ref[r0:r0 + ch, :] = dgg.astype(BF16)
            sum_dc, sum_a, sum_g = sum_dc + _fold8(dcc), sum_a + _fold8(dga), sum_g + _fold8(dgg)
        for j in range(CONV_W):
            dw_ref[j:j + 1, :] += jnp.sum(dw_acc[j], axis=0, keepdims=True)
        dbdw_ref[...] += jnp.sum(sum_dc, axis=0, keepdims=True)
        dba_ref[...] += jnp.sum(sum_a, axis=0, keepdims=True)
        dbg_ref[...] += jnp.sum(sum_g, axis=0, keepdims=True)

    seq = lambda off: pl.BlockSpec((t, CONV_TC), lambda j, b: (b, off + j))
    vec = lambda off: pl.BlockSpec((1, CONV_TC), lambda j, b: (0, off + j))
    wsp = pl.BlockSpec((CONV_W, CONV_TC), lambda j, b: (0, j))
    act = jax.ShapeDtypeStruct((n, c), BF16)
    v = jax.ShapeDtypeStruct((1, c), F32)
    return pl.pallas_call(
        body, name=name, grid=(nc, batch),
        out_shape=[act, act, jax.ShapeDtypeStruct((CONV_W, c), F32), v, v, v],
        in_specs=[seq(0), seq(0), seq(nc), vec(0), vec(nc), wsp],
        out_specs=[seq(0), seq(0), wsp, vec(0), vec(0), vec(0)],
        scratch_shapes=[pltpu.VMEM((t + CONV_HALO, CONV_TC), F32), pltpu.VMEM((t + CONV_HALO, CONV_TC), F32)],
        compiler_params=_params(("parallel", "arbitrary")))(dc1, glu, glu, b_glu, b_glu, w_dw)


def _split3(x):
    hi = x.astype(BF16)
    r = x - hi.astype(F32)
    mid = r.astype(BF16)
    lo = (r - mid.astype(F32)).astype(BF16)
    return hi, mid, lo


def _tri_matmul(tri, x):
    hi, mid, lo = _split3(x)
    dot = lambda v: jnp.dot(tri, v, preferred_element_type=F32)
    return dot(hi) + dot(mid) + dot(lo)


def _fox_prep_fwd(name, fg, b_fg, batch):
    n, w = fg.shape
    t = n // batch
    nq = t // ROW_TILE

    def body(fg_ref, b_ref, cum_ref):
        row = lax.broadcasted_iota(jnp.int32, (ROW_TILE, ROW_TILE), 0)
        col = lax.broadcasted_iota(jnp.int32, (ROW_TILE, ROW_TILE), 1)
        tri = (row >= col).astype(BF16)
        for k in range(nq):
            rows = slice(k * ROW_TILE, (k + 1) * ROW_TILE)
            z = fg_ref[rows, :] + b_ref[...]
            logf = jnp.minimum(z, 0.0) - jnp.log(1.0 + jnp.exp(-jnp.abs(z)))
            cum = _tri_matmul(tri, logf)
            if k > 0:
                cum = cum + cum_ref[k * ROW_TILE - 1:k * ROW_TILE, :]
            cum_ref[rows, :] = cum

    seq = pl.BlockSpec((t, w), lambda b: (b, 0))
    return pl.pallas_call(body, name=name, out_shape=jax.ShapeDtypeStruct((n, w), F32), grid=(batch,),
                          in_specs=[seq, pl.BlockSpec((1, w), lambda b: (0, 0))], out_specs=seq,
                          compiler_params=_params(("parallel",)))(fg, b_fg)


def _fox_prep_bwd(name, dcum_k, dcum_q, fg, b_fg, batch):
    n, w = fg.shape
    t = n // batch
    nq = t // ROW_TILE

    def body(dk_ref, dq_ref, fg_ref, b_ref, dfg_ref, db_ref, rev_ref):
        row = lax.broadcasted_iota(jnp.int32, (ROW_TILE, ROW_TILE), 0)
        col = lax.broadcasted_iota(jnp.int32, (ROW_TILE, ROW_TILE), 1)
        tri = (col >= row).astype(BF16)

        @pl.when(pl.program_id(0) == 0)
        def _():
            db_ref[...] = jnp.zeros(db_ref.shape, F32)

        for k in reversed(range(nq)):
            rows = slice(k * ROW_TILE, (k + 1) * ROW_TILE)
            dlog = _tri_matmul(tri, dk_ref[rows, :] + dq_ref[rows, :])
            if k < nq - 1:
                dlog = dlog + rev_ref[(k + 1) * ROW_TILE:(k + 1) * ROW_TILE + 1, :]
            rev_ref[rows, :] = dlog
            dfg = dlog * _sigmoid(-(fg_ref[rows, :] + b_ref[...]))
            dfg_ref[rows, :] = dfg.astype(BF16)
            db_ref[...] += jnp.sum(dfg, axis=0, keepdims=True)

    seq = pl.BlockSpec((t, w), lambda b: (b, 0))
    vec = pl.BlockSpec((1, w), lambda b: (0, 0))
    return pl.pallas_call(
        body, name=name, grid=(batch,),
        out_shape=[jax.ShapeDtypeStruct((n, w), BF16), jax.ShapeDtypeStruct((1, w), F32)],
        in_specs=[seq, seq, seq, vec], out_specs=[seq, vec], scratch_shapes=[pltpu.VMEM((t, w), F32)],
        compiler_params=_params(("arbitrary",)))(dcum_k, dcum_q, fg, b_fg)


def _head_masks(x):
    lane = lax.broadcasted_iota(jnp.int32, x.shape, 1)
    zero = jnp.zeros(x.shape, x.dtype)
    return jnp.where(lane < HEAD_DIM, x, zero), jnp.where(lane >= HEAD_DIM, x, zero)


ATTN_BLOCK = 512


def _attn_blocks(t):
    nb = max(t // ATTN_BLOCK, 1)
    blocks = [(i * ATTN_BLOCK, ATTN_BLOCK) for i in range(nb - 1)]
    return blocks + [((nb - 1) * ATTN_BLOCK, t - (nb - 1) * ATTN_BLOCK)]


def _attn_specs(t):
    blocks = _attn_blocks(t)
    width = max(sz for _, sz in blocks)
    qkv = lambda off: pl.BlockSpec((t, LANES), lambda b, h: (b, off + h))
    cumr = pl.BlockSpec((None, None, len(blocks), 8, width), lambda b, h: (b, h, 0, 0, 0))
    return qkv, cumr


def _key_sums_to_blocks(cum, batch, t, n_pairs):
    blocks = _attn_blocks(t)
    width = max(sz for _, sz in blocks)
    cum_h = cum.reshape(batch, t, -1)[:, :, :2 * n_pairs].reshape(batch, t, n_pairs, 2)
    rows = [jnp.pad(jnp.transpose(cum_h[:, s0:s0 + sz], (0, 2, 3, 1)), ((0, 0), (0, 0), (0, 6), (0, width - sz)))
            for s0, sz in blocks]
    return jnp.stack(rows, axis=2)


def _key_sums_from_blocks(dcumr, batch, t, n_pairs):
    cols = [jnp.transpose(dcumr[:, :, j, :2, :sz], (0, 3, 1, 2)) for j, (_, sz) in enumerate(_attn_blocks(t))]
    return jnp.concatenate(cols, axis=1).reshape(batch * t, 2 * n_pairs)


def _causal(size):
    row = lax.broadcasted_iota(jnp.int32, (size, size), 0)
    col = lax.broadcasted_iota(jnp.int32, (size, size), 1)
    return row >= col


def _attn_fwd(name, qkv, cumr, batch):
    n, w3 = qkv.shape
    w = w3 // 3
    t = n // batch
    n_pairs = w // LANES
    blocks = _attn_blocks(t)

    def body(q_ref, k_ref, v_ref, cr_ref, o_ref, lse_ref):
        for i, (q0, qn) in enumerate(blocks):
            rows = slice(q0, q0 + qn)
            qs = _head_masks(q_ref[rows, :] * (0.125 * LOG2E))
            outs, lses = [], []
            for hh in range(2):
                m = jnp.full((qn, 1), NEG, F32)
                l = jnp.zeros((qn, 1), F32)
                acc = jnp.zeros((qn, LANES), F32)
                for j in range(i + 1):
                    k0, kn = blocks[j]
                    cols = slice(k0, k0 + kn)
                    s = _dot_nt(qs[hh], k_ref[cols, :]) - cr_ref[j, hh:hh + 1, 0:kn] * LOG2E
                    if j == i:
                        s = jnp.where(_causal(qn), s, NEG)
                    m_new = jnp.maximum(m, jnp.max(s, axis=1, keepdims=True))
                    alpha = jnp.exp2(m - m_new)
                    p = jnp.exp2(s - m_new)
                    l = alpha * l + jnp.sum(p, axis=1, keepdims=True)
                    acc = alpha * acc + jnp.dot(p.astype(BF16), v_ref[cols, :], preferred_element_type=F32)
                    m = m_new
                outs.append(acc / l)
                lses.append(m + jnp.log2(l))
            lane = lax.broadcasted_iota(jnp.int32, (qn, LANES), 1)
            o_ref[rows, :] = jnp.where(lane < HEAD_DIM, outs[0], outs[1]).astype(BF16)
            lse_ref[rows, :] = jnp.where(lane < HEAD_DIM, lses[0], lses[1])

    qkv_spec, cumr_spec = _attn_specs(t)
    return pl.pallas_call(
        body, name=name, grid=(batch, n_pairs),
        out_shape=[jax.ShapeDtypeStruct((n, w), BF16), jax.ShapeDtypeStruct((n, w), F32)],
        in_specs=[qkv_spec(0), qkv_spec(n_pairs), qkv_spec(2 * n_pairs), cumr_spec],
        out_specs=[qkv_spec(0), qkv_spec(0)],
        compiler_params=_params(("parallel", "parallel")))(qkv, qkv, qkv, cumr)


def _attn_bwd(name, qkv, o, do, lse, cumr, batch):
    n, w3 = qkv.shape
    w = w3 // 3
    t = n // batch
    n_pairs = w // LANES
    blocks = _attn_blocks(t)

    def body(q_ref, k_ref, v_ref, o_ref, do_ref, lse_ref, cr_ref, dq_ref, dk_ref, dv_ref, dcr_ref, dcq_ref,
             dk_acc, dv_acc):
        pair = pl.program_id(1)
        dk_acc[...] = jnp.zeros(dk_acc.shape, F32)
        dv_acc[...] = jnp.zeros(dv_acc.shape, F32)
        dcr_ref[...] = jnp.zeros(dcr_ref.shape, F32)

        @pl.when(pair == 0)
        def _():
            dcq_ref[...] = jnp.zeros(dcq_ref.shape, F32)

        for i, (q0, qn) in enumerate(blocks):
            rows = slice(q0, q0 + qn)
            qs = _head_masks(q_ref[rows, :] * 0.125)
            q2 = _head_masks(q_ref[rows, :] * (0.125 * LOG2E))
            dos = _head_masks(do_ref[rows, :])
            dq = jnp.zeros((qn, LANES), F32)
            dcq = []
            for hh in range(2):
                row_sum = jnp.zeros((qn, 1), F32)
                lse = lse_ref[rows, hh * HEAD_DIM:hh * HEAD_DIM + 1]
                delta = jnp.sum(dos[hh].astype(F32) * o_ref[rows, :].astype(F32), axis=1, keepdims=True)
                for j in range(i + 1):
                    k0, kn = blocks[j]
                    cols = slice(k0, k0 + kn)
                    s = _dot_nt(q2[hh], k_ref[cols, :]) - cr_ref[j, hh:hh + 1, 0:kn] * LOG2E
                    p = jnp.exp2(s - lse)
                    if j == i:
                        p = jnp.where(_causal(qn), p, 0.0)
                    dp = _dot_nt(dos[hh], v_ref[cols, :])
                    ds = p * (dp - delta)
                    pb, dsb = p.astype(BF16), ds.astype(BF16)
                    km = _head_masks(k_ref[cols, :])[hh]
                    dv_acc[cols, :] += _dot_tn(pb, dos[hh])
                    dk_acc[cols, :] += _dot_tn(dsb, qs[hh])
                    dq = dq + jnp.dot(dsb, km, preferred_element_type=F32)
                    dcr_ref[j, hh:hh + 1, 0:kn] -= jnp.sum(ds, axis=0, keepdims=True)
                    row_sum = row_sum + jnp.sum(ds, axis=1, keepdims=True)
                dcq.append(row_sum)
            lane = lax.broadcasted_iota(jnp.int32, (qn, LANES), 1)
            dq_ref[rows, :] = (dq * 0.125).astype(BF16)
            dcq_ref[rows, :] = jnp.where(lane == 2 * pair, dcq[0],
                                         jnp.where(lane == 2 * pair + 1, dcq[1], dcq_ref[rows, :]))
        dk_ref[...] = dk_acc[...].astype(BF16)
        dv_ref[...] = dv_acc[...].astype(BF16)

    qkv_spec, cumr_spec = _attn_specs(t)
    act = jax.ShapeDtypeStruct((n, w), BF16)
    return pl.pallas_call(
        body, name=name, grid=(batch, n_pairs),
        out_shape=[act, act, act, jax.ShapeDtypeStruct(cumr.shape, F32), jax.ShapeDtypeStruct((n, LANES), F32)],
        in_specs=[qkv_spec(0), qkv_spec(n_pairs), qkv_spec(2 * n_pairs), qkv_spec(0), qkv_spec(0), qkv_spec(0),
                  cumr_spec],
        out_specs=[qkv_spec(0), qkv_spec(0), qkv_spec(0), cumr_spec, pl.BlockSpec((t, LANES), lambda b, h: (b, 0))],
        scratch_shapes=[pltpu.VMEM((t, LANES), F32), pltpu.VMEM((t, LANES), F32)],
        compiler_params=_params(("parallel", "arbitrary")))(qkv, qkv, qkv, o, do, lse, cumr)


def _adamw(name, parts, w, m, v):
    r, c = w.shape
    tr = 128 if r % 128 == 0 else r
    tc = 256 if tr > 128 and c % 256 == 0 else c
    c1 = 1.0 - ADAM_B1 ** ADAM_STEP
    c2 = 1.0 - ADAM_B2 ** ADAM_STEP

    def body(p_ref, w_ref, m_ref, v_ref, g_ref, d_ref, m2_ref, v2_ref):
        g = p_ref[0].astype(F32)
        for s in range(1, N_DEV):
            g = g + p_ref[s].astype(F32)
        m2 = ADAM_B1 * m_ref[...] + (1.0 - ADAM_B1) * g
        v2 = ADAM_B2 * v_ref[...] + (1.0 - ADAM_B2) * (g * g)
        g_ref[...] = g
        m2_ref[...] = m2
        v2_ref[...] = v2
        d_ref[...] = -ADAM_LR * ((m2 / c1) / (jnp.sqrt(v2 / c2) + ADAM_EPS) + ADAM_WD * w_ref[...])

    blk = pl.BlockSpec((tr, tc), lambda i, j: (i, j))
    shp = jax.ShapeDtypeStruct((r, c), F32)
    return pl.pallas_call(
        body, name=name, out_shape=[shp] * 4, grid=(r // tr, c // tc),
        in_specs=[pl.BlockSpec((N_DEV, tr, tc), lambda i, j: (0, i, j)), blk, blk, blk], out_specs=[blk] * 4,
        compiler_params=_params(("parallel", "parallel")))(parts, w, m, v)


def _cat_small(vals):
    parts = []
    for name in SMALL:
        v = vals[name].reshape(1, -1).astype(F32)
        parts.append(jnp.pad(v, ((0, 0), (0, SMALL_W[name] - v.shape[1]))))
    return jnp.concatenate(parts, axis=1)


def _split_small(row, shapes):
    out, off = {}, 0
    for name in SMALL:
        out[name] = row[0, off:off + SMALL_N[name]].reshape(shapes[name])
        off += SMALL_W[name]
    return out


def _cols_from_shards(g):
    return jnp.transpose(g, (1, 0, 2)).reshape(g.shape[1], N_DEV * g.shape[2])


def _shards_from_cols(a):
    r, c = a.shape
    return jnp.transpose(a.reshape(r, N_DEV, c // N_DEV), (1, 0, 2))


def kernel(x, meta_tokens, norm_mix_gain, w_in, b_forget, w_attn_out, b_glu, conv_dw_w, conv_dw_b, conv_ln_gain, conv_ln_bias, w_conv_out, b_conv_out, w_out, norm_mlp_gain, w_mlp_up, w_mlp_down, final_norm_gain, loss_target, m_meta_tokens, m_norm_mix_gain, m_w_in, m_b_forget, m_w_attn_out, m_b_glu, m_conv_dw_w, m_conv_dw_b, m_conv_ln_gain, m_conv_ln_bias, m_w_conv_out, m_b_conv_out, m_w_out, m_norm_mlp_gain, m_w_mlp_up, m_w_mlp_down, m_final_norm_gain, v_meta_tokens, v_norm_mix_gain, v_w_in, v_b_forget, v_w_attn_out, v_b_glu, v_conv_dw_w, v_conv_dw_b, v_conv_ln_gain, v_conv_ln_bias, v_w_conv_out, v_b_conv_out, v_w_out, v_norm_mlp_gain, v_w_mlp_up, v_w_mlp_down, v_final_norm_gain):
    weights = dict(meta_tokens=meta_tokens, norm_mix_gain=norm_mix_gain, w_in=w_in, b_forget=b_forget, w_attn_out=w_attn_out, b_glu=b_glu, conv_dw_w=conv_dw_w, conv_dw_b=conv_dw_b, conv_ln_gain=conv_ln_gain, conv_ln_bias=conv_ln_bias, w_conv_out=w_conv_out, b_conv_out=b_conv_out, w_out=w_out, norm_mlp_gain=norm_mlp_gain, w_mlp_up=w_mlp_up, w_mlp_down=w_mlp_down, final_norm_gain=final_norm_gain)
    mom_m = dict(meta_tokens=m_meta_tokens, norm_mix_gain=m_norm_mix_gain, w_in=m_w_in, b_forget=m_b_forget, w_attn_out=m_w_attn_out, b_glu=m_b_glu, conv_dw_w=m_conv_dw_w, conv_dw_b=m_conv_dw_b, conv_ln_gain=m_conv_ln_gain, conv_ln_bias=m_conv_ln_bias, w_conv_out=m_w_conv_out, b_conv_out=m_b_conv_out, w_out=m_w_out, norm_mlp_gain=m_norm_mlp_gain, w_mlp_up=m_w_mlp_up, w_mlp_down=m_w_mlp_down, final_norm_gain=m_final_norm_gain)
    mom_v = dict(meta_tokens=v_meta_tokens, norm_mix_gain=v_norm_mix_gain, w_in=v_w_in, b_forget=v_b_forget, w_attn_out=v_w_attn_out, b_glu=v_b_glu, conv_dw_w=v_conv_dw_w, conv_dw_b=v_conv_dw_b, conv_ln_gain=v_conv_ln_gain, conv_ln_bias=v_conv_ln_bias, w_conv_out=v_w_conv_out, b_conv_out=v_b_conv_out, w_out=v_w_out, norm_mlp_gain=v_norm_mlp_gain, w_mlp_up=v_w_mlp_up, w_mlp_down=v_w_mlp_down, final_norm_gain=v_final_norm_gain)
    names = list(weights)
    batch, seq, d = x.shape
    t = seq + N_META
    n = batch * t
    n_pairs = d // LANES
    assert t % ROW_TILE == 0 and d == SEG

    to_rows = lambda w3: jnp.transpose(w3[0])
    w_in_t, m_in_t, v_in_t = to_rows(w_in), to_rows(m_w_in), to_rows(v_w_in)
    first = [w_in_t.astype(BF16), meta_tokens, conv_dw_w[0]]
    rest = [w_[0].astype(BF16) for w_ in (w_attn_out, w_conv_out, w_out, w_mlp_up, w_mlp_down)]
    tgt = jnp.concatenate([jnp.zeros((batch, N_META, d), F32), loss_target], axis=1).reshape(n, d)
    h0_rows = jnp.pad(x, ((0, 0), (N_META, 0), (0, 0)))
    gather_a = _exchange_start("gather_in_start", [(f_, False) for f_ in first], ks=CHIP_PEERS)
    level_1 = _exchange_wait("gather_in_wait", gather_a, [gather_a["token"], tgt, h0_rows, w_in_t, m_in_t, v_in_t] + rest)
    passed = _pass_on_start("gather_in_pass_start", level_1)
    w_in_g, meta_g, w_dw_g = _pass_on_wait("gather_in_pass_wait", passed, passed["token"])
    gather_b = _exchange_start("gather_rest_start", [(r_, False) for r_ in rest])
    n_fg = b_forget.shape[1]
    shard_w = w_in.shape[2]
    wt = w_in_g.reshape(N_DEV * shard_w, d)
    o_fg = 3 * SEG
    seg_rows = [0, SEG, 2 * SEG] + [o_fg + n_fg + i * SEG for i in range(4)]
    d_ff = w_mlp_down.shape[1] * N_DEV
    ff_blk = d_ff // N_DEV
    meta_f = _cols_from_shards(meta_g)
    w_dw = _cols_from_shards(w_dw_g)

    row2 = lambda v: v.reshape(1, -1)
    g1, g2, g3 = row2(norm_mix_gain) + gather_b["token"][0:1, 0:1], row2(norm_mlp_gain), row2(final_norm_gain)
    b_fg = jnp.pad(b_forget, ((0, 0), (0, FG_PAD - n_fg)))
    h0 = lax.dynamic_update_slice(h0_rows, jnp.broadcast_to(meta_f[None], (batch, N_META, d)), (0, 0, 0)).reshape(n, d)

    hn1 = _rms_fwd("rms1", h0, g1)
    proj = lambda name, off, width, tn, dt: _mm_nt(name, [(hn1, _a_rows(d), wt, _wt_rows(tn, off))], n, width, tn, dt)
    qkv = proj("proj_qkv", 0, 3 * SEG, SEG, BF16)
    glu = proj("proj_glu", seg_rows[3], 2 * SEG, SEG, BF16)
    gates = proj("proj_gates", seg_rows[5], 2 * SEG, SEG, BF16)
    fg = proj("proj_fg", o_fg, FG_PAD, FG_PAD, F32)

    cum = _fox_prep_fwd("fox_cumsum", fg, b_fg, batch)
    cumr = _key_sums_to_blocks(cum, batch, t, n_pairs)
    o, lse = _attn_fwd("attn_fwd", qkv, cumr, batch)
    rest = _exchange_wait("gather_rest_wait", gather_b, o)
    w_ao, w_co, w_o = [r_.reshape(d, d) for r_ in rest[:3]]
    w_up = rest[3]
    w_dn = rest[4].reshape(d_ff, d)
    a = _mm_nn("attn_out", o, w_ao, _w_cols(d, d, 0), d, d, BF16)

    c1 = _glu_conv_fwd("glu_conv", glu, b_glu, w_dw, conv_dw_b, batch)
    c3 = _ln_silu_fwd("ln_silu", c1, conv_ln_gain, conv_ln_bias)
    c = _mm_nn("conv_out", c3, w_co, _w_cols(d, d, 0), d, d, BF16)

    mrg = _merge_fwd("merge", gates, a, c, b_conv_out)
    mo = _mm_nn("mix_out", mrg, w_o, _w_cols(d, d, 0), d, d, F32)
    h1, hn2 = _rms_fwd("resid_rms2", h0, g2, res=mo)
    per = ff_blk // 512
    act = _mm_nn("mlp_up", hn2, w_up, pl.BlockSpec((None, d, 512), lambda i, j: (j // per, 0, j % per)),
                 d_ff, 512, BF16, relu2=True)
    dn = _mm_nn("mlp_down", act, w_dn, _w_cols(d_ff, d, 0), d, d, F32, tm=ROW_TILE)
    dh2, dh2b, loss_blk, dg3 = _final("final_loss", h1, dn, tgt, g3, batch)

    dup = _mm_nt("d_mlp_down", [(dh2b, _a_rows(d), w_dn, _w_rows(d, d))], n, d_ff, d, BF16, relu_bwd_of=act)
    dw_dn = _grad_w("gw_mlp_down", act, dh2b)
    dhn2 = _mm_nt("d_mlp_up", [(dup, _a_rows(ff_blk, g), w_up, pl.BlockSpec((None, 512, ff_blk), lambda i, j, g=g: (g, j, 0)))
                               for g in range(N_DEV)], n, d, 512, BF16)
    dw_up = _mm_tn("gw_mlp_up", hn2, lambda a_: 0, d, dup, lambda b_: b_, ff_blk, (N_DEV, d, ff_blk),
                   pl.BlockSpec((None, d, ff_blk), lambda a_, b_: (b_, 0, 0)), (1, N_DEV))
    scatter_1 = _exchange_start("scatter_mlp_start", [(dw_dn.reshape(N_DEV, ff_blk, d), True), (dw_up, True)])
    dh1, dg2, dh1b = _rms_bwd("rms2_bwd", dhn2, h1, g2 + scatter_1["token"][0:1, 0:1], dh2, batch, with_bf16=True)

    dm = _mm_nt("d_mix_out", [(dh1b, _a_rows(d), w_o, _w_rows(d, d))], n, d, d, BF16)
    dw_o = _grad_w("gw_mix_out", mrg, dh1b)
    da, dc, dga, dgc, dbco = _merge_bwd("merge_bwd", dm, gates, a, c, b_conv_out)

    do = _mm_nt("d_attn_out", [(da, _a_rows(d), w_ao, _w_rows(d, d))], n, d, d, BF16)
    dw_ao = _grad_w("gw_attn_out", o, da)
    dc3 = _mm_nt("d_conv_out", [(dc, _a_rows(d), w_co, _w_rows(d, d))], n, d, d, BF16)
    dw_co = _grad_w("gw_conv_out", c3, dc)

    scatter_2 = _exchange_start("scatter_mix_start", [(dw_.reshape(N_DEV, d // N_DEV, d), True)
                                                      for dw_ in (dw_o, dw_ao, dw_co)])
    dc1, dg_ln, db_ln = _ln_silu_bwd("ln_silu_bwd", dc3, c1, conv_ln_gain + scatter_2["token"][0:1, 0:1],
                                     conv_ln_bias)
    dglu_a, dglu_g, dw_dw, db_dw, dbg_a, dbg_g = _glu_conv_bwd("glu_conv_bwd", dc1, glu, b_glu, w_dw, batch)

    dq, dk, dv, dcumr, dcum_q = _attn_bwd("attn_bwd", qkv, o, do, lse, cumr, batch)
    dcum_k = jnp.pad(_key_sums_from_blocks(dcumr, batch, t, n_pairs), ((0, 0), (0, FG_PAD - 2 * n_pairs)))
    dfg, db_fg = _fox_prep_bwd("fox_cumsum_bwd", dcum_k, dcum_q, fg, b_fg, batch)

    segs = [dq, dk, dv, dglu_a, dglu_g, dga, dgc]
    gw_t = [_grad_w("gw_in_%d" % i, s_, hn1) for i, s_ in enumerate(segs)]
    gw_fg = _grad_w("gw_in_fg", dfg, hn1)[:n_fg]
    dw_in_t = jnp.concatenate(gw_t[:3] + [gw_fg] + gw_t[3:], axis=0).reshape(N_DEV, shard_w, d)
    scatter_3 = _exchange_start("scatter_in_start", [(dw_in_t, True)])
    pairs = [(s_, _a_rows(SEG, 0, ROW_TILE), wt, _wt_block(SEG, seg_rows[i], 512), "nn") for i, s_ in enumerate(segs)]
    pairs.append((dfg, _a_rows(FG_PAD, 0, ROW_TILE), wt, _wt_block(FG_PAD, o_fg, 512), "nn"))
    dhn1 = _mm_nt("d_proj_in", pairs, n, d, 512, BF16, tm=ROW_TILE, after=scatter_3["token"])
    dh0, dg1, dmeta = _rms_bwd("rms1_bwd", dhn1, h0, g1, dh1, batch, with_meta=True)
    grad_x = dh0.reshape(batch, t, d)[:, N_META:, :]

    small_g = dict(norm_mix_gain=dg1, b_forget=db_fg[:, :n_fg], b_glu=jnp.concatenate([dbg_a, dbg_g], axis=1),
                   conv_dw_b=db_dw, conv_ln_gain=dg_ln, conv_ln_bias=db_ln, b_conv_out=dbco, norm_mlp_gain=dg2,
                   final_norm_gain=dg3)
    scatter_4 = _exchange_start("scatter_small_start", [
        (_shards_from_cols(dmeta), True), (_shards_from_cols(dw_dw), True), (_cat_small(small_g), False),
        (loss_blk[0:1, :], False)])

    grads, deltas, new_m, new_v = {}, {}, {}, {}

    def update(k, parts):
        shp = weights[k].shape
        if k == "w_in":
            res_ = _adamw("adamw_" + k, parts, w_in_t, m_in_t, v_in_t)
            res_ = [jnp.transpose(r) for r in res_]
        else:
            w2 = lambda arr: arr.reshape(parts.shape[1:])
            res_ = _adamw("adamw_" + k, parts, w2(weights[k]), w2(mom_m[k]), w2(mom_v[k]))
        grads[k], deltas[k], new_m[k], new_v[k] = [r.reshape(shp) for r in res_]

    for k, parts in zip(("w_mlp_down", "w_mlp_up"), _exchange_wait("scatter_mlp_wait", scatter_1, scatter_4["token"])):
        update(k, parts)
    for k, parts in zip(("w_out", "w_attn_out", "w_conv_out"),
                        _exchange_wait("scatter_mix_wait", scatter_2, deltas["w_mlp_up"])):
        update(k, parts)
    update("w_in", _exchange_wait("scatter_in_wait", scatter_3, deltas["w_conv_out"])[0])
    reduced = _exchange_wait("scatter_small_wait", scatter_4, deltas["w_in"])
    loss = jnp.sum(reduced.pop()[:, 0, 0])
    for k, parts in zip(("meta_tokens", "conv_dw_w"), reduced[:-1]):
        update(k, parts)
    res = _adamw("adamw_small", reduced[-1], _cat_small(weights), _cat_small(mom_m), _cat_small(mom_v))
    shapes = {k: weights[k].shape for k in SMALL}
    for dst, r in zip((grads, deltas, new_m, new_v), res):
        dst.update(_split_small(r, shapes))

    return (loss, grad_x, *[grads[k] for k in names], *[deltas[k] for k in names],
            *[new_m[k] for k in names], *[new_v[k] for k in names])
```

```python
import functools

import jax
import jax.numpy as jnp
from jax import lax
from jax.experimental import pallas as pl
from jax.experimental.pallas import tpu as pltpu

F32, BF16 = jnp.float32, jnp.bfloat16
N_DEV = 8
N_META = 16
HEAD_DIM = 64
LANES = 128
CONV_W = 31
RMS_EPS = 1e-6
LN_EPS = 1e-5
ROW_TILE = 688
MM_TM = 2 * ROW_TILE
SEG = 1024
FG_PAD = 128
VMEM_LIMIT = 56 * 1024 * 1024
ADAM_LR, ADAM_B1, ADAM_B2, ADAM_EPS, ADAM_WD, ADAM_STEP = 0.001, 0.9, 0.999, 1e-08, 0.01, 10
NEG = -1e30
LOG2E = 1.4426950408889634

SMALL = ("norm_mix_gain", "b_forget", "b_glu", "conv_dw_b", "conv_ln_gain", "conv_ln_bias", "b_conv_out",
         "norm_mlp_gain", "final_norm_gain")
SMALL_W = {"norm_mix_gain": 1024, "b_forget": 128, "b_glu": 2048, "conv_dw_b": 1024, "conv_ln_gain": 1024,
           "conv_ln_bias": 1024, "b_conv_out": 1024, "norm_mlp_gain": 1024, "final_norm_gain": 1024}
SMALL_N = {"norm_mix_gain": 1024, "b_forget": 16, "b_glu": 2048, "conv_dw_b": 1024, "conv_ln_gain": 1024,
           "conv_ln_bias": 1024, "b_conv_out": 1024, "norm_mlp_gain": 1024, "final_norm_gain": 1024}


def _params(sem=None):
    return pltpu.CompilerParams(dimension_semantics=sem, vmem_limit_bytes=VMEM_LIMIT)


def _sigmoid(x):
    return 1.0 / (1.0 + jnp.exp(-x))


def _dot_nt(a, b):
    return lax.dot_general(a, b, (((1,), (1,)), ((), ())), preferred_element_type=F32)


def _dot_tn(a, b):
    return lax.dot_general(a, b, (((0,), (0,)), ((), ())), preferred_element_type=F32)


HBM_SPEC = pl.BlockSpec(memory_space=pltpu.HBM)
SEM_SPEC = pl.BlockSpec(memory_space=pltpu.SEMAPHORE)
DATAFLOW = pltpu.SideEffectType.DATAFLOW_SIDE_EFFECTING


def _device_index():
    return 4 * lax.axis_index("x") + 2 * lax.axis_index("y") + lax.axis_index("c")


def _peers():
    x, y, c = lax.axis_index("x"), lax.axis_index("y"), lax.axis_index("c")
    out = []
    for k in range(1, N_DEV):
        px = 1 - x if k & 4 else x
        py = 1 - y if k & 2 else y
        pc = 1 - c if k & 1 else c
        out.append((k, (px, py, pc), 4 * px + 2 * py + pc))
    return out


def _peer_copy(per_dest, src_ref, land_ref, send_sems, recv_sems, a, k, dev, peer):
    src = src_ref.at[peer] if per_dest else src_ref
    return pltpu.make_async_remote_copy(
        src_ref=src, dst_ref=land_ref.at[_device_index()], send_sem=send_sems.at[a * (N_DEV - 1) + k - 1],
        recv_sem=recv_sems.at[a * (N_DEV - 1) + k - 1], device_id=dev, device_id_type=pl.DeviceIdType.MESH)


ALL_PEERS = tuple(range(1, N_DEV))
CHIP_PEERS = (1, 2, 4, 6)
FAR_PEERS = (2, 4, 6)


def _own_copy(per_dest, src_ref, land_ref, send_sems, n, a):
    me = _device_index()
    return pltpu.make_async_copy(src_ref.at[me] if per_dest else src_ref, land_ref.at[me],
                                 send_sems.at[n * (N_DEV - 1) + a])


def _exchange_start(name, items, ks=ALL_PEERS):
    n = len(items)
    per_dest = [it[1] for it in items]

    def body(*refs):
        srcs, lands = refs[:n], refs[n:2 * n]
        send_sems, recv_sems, token = refs[2 * n], refs[2 * n + 1], refs[-1]
        for a in range(n):
            _own_copy(per_dest[a], srcs[a], lands[a], send_sems, n, a).start()
            for k, dev, peer in _peers():
                if k in ks:
                    _peer_copy(per_dest[a], srcs[a], lands[a], send_sems, recv_sems, a, k, dev, peer).start()
        token[...] = jnp.zeros(token.shape, F32)

    srcs = [pltpu.with_memory_space_constraint(it[0], pltpu.HBM) for it in items]
    lands = []
    for arr, pd in items:
        shp = arr.shape if pd else (N_DEV,) + arr.shape
        lands.append(pltpu.with_memory_space_constraint(lax.empty(shp, arr.dtype), pltpu.HBM))
    sems = pltpu.SemaphoreType.DMA((n * N_DEV,))
    res = pl.pallas_call(
        body, name=name,
        out_shape=(sems, sems, *[pltpu.HBM(a_.shape, a_.dtype) for a_ in srcs + lands],
                   jax.ShapeDtypeStruct((8, 128), F32)),
        in_specs=[HBM_SPEC] * (2 * n),
        out_specs=(SEM_SPEC, SEM_SPEC, *[HBM_SPEC] * (2 * n), pl.BlockSpec(memory_space=pltpu.VMEM)),
        input_output_aliases={i: 2 + i for i in range(2 * n)},
        compiler_params=pltpu.CompilerParams(has_side_effects=DATAFLOW),
    )(*srcs, *lands)
    return dict(per_dest=per_dest, ks=ks, send=res[0], recv=res[1], srcs=list(res[2:2 + n]),
                lands=list(res[2 + n:2 + 2 * n]), token=res[-1])


def _exchange_wait(name, started, after):
    per_dest = started["per_dest"]
    n = len(per_dest)

    def body(*refs):
        srcs, lands = refs[:n], refs[n:2 * n]
        send_sems, recv_sems = refs[2 * n], refs[2 * n + 1]
        for a in range(n):
            _own_copy(per_dest[a], srcs[a], lands[a], send_sems, n, a).wait()
            for k, dev, peer in _peers():
                if k in started["ks"]:
                    cp = _peer_copy(per_dest[a], srcs[a], lands[a], send_sems, recv_sems, a, k, dev, peer)
                    cp.wait_send()
                    cp.wait_recv()

    bufs = started["srcs"] + started["lands"]
    after = list(after) if isinstance(after, (list, tuple)) else [after]
    res = pl.pallas_call(
        body, name=name, out_shape=tuple(pltpu.HBM(b_.shape, b_.dtype) for b_ in bufs),
        in_specs=[HBM_SPEC] * (2 * n) + [SEM_SPEC, SEM_SPEC] + [pl.BlockSpec(memory_space=pl.ANY)] * len(after),
        out_specs=tuple([HBM_SPEC] * (2 * n)), input_output_aliases={i: i for i in range(2 * n)},
        compiler_params=pltpu.CompilerParams(has_side_effects=DATAFLOW),
    )(*bufs, started["send"], started["recv"], *after)
    return list(res[n:])


def _pass_on_copy(land_ref, send_sems, recv_sems, a, idx, slot):
    sibling = (lax.axis_index("x"), lax.axis_index("y"), 1 - lax.axis_index("c"))
    return pltpu.make_async_remote_copy(
        src_ref=land_ref.at[slot], dst_ref=land_ref.at[slot], send_sem=send_sems.at[a * len(FAR_PEERS) + idx],
        recv_sem=recv_sems.at[a * len(FAR_PEERS) + idx], device_id=sibling, device_id_type=pl.DeviceIdType.MESH)


def _pass_on_start(name, lands):
    n = len(lands)

    def body(*refs):
        send_sems, recv_sems, token = refs[n], refs[n + 1], refs[-1]
        slots = {k: peer for k, _, peer in _peers()}
        for a in range(n):
            for idx, k in enumerate(FAR_PEERS):
                _pass_on_copy(refs[a], send_sems, recv_sems, a, idx, slots[k]).start()
        token[...] = jnp.zeros(token.shape, F32)

    lands = [pltpu.with_memory_space_constraint(l_, pltpu.HBM) for l_ in lands]
    sems = pltpu.SemaphoreType.DMA((n * len(FAR_PEERS),))
    res = pl.pallas_call(
        body, name=name,
        out_shape=(sems, sems, *[pltpu.HBM(l_.shape, l_.dtype) for l_ in lands], jax.ShapeDtypeStruct((8, 128), F32)),
        in_specs=[HBM_SPEC] * n, out_specs=(SEM_SPEC, SEM_SPEC, *[HBM_SPEC] * n, pl.BlockSpec(memory_space=pltpu.VMEM)),
        input_output_aliases={i: 2 + i for i in range(n)},
        compiler_params=pltpu.CompilerParams(has_side_effects=DATAFLOW),
    )(*lands)
    return dict(send=res[0], recv=res[1], lands=list(res[2:2 + n]), token=res[-1])


def _pass_on_wait(name, passed, after):
    n = len(passed["lands"])

    def body(*refs):
        send_sems, recv_sems = refs[n], refs[n + 1]
        slots = {k: peer for k, _, peer in _peers()}
        for a in range(n):
            for idx, k in enumerate(FAR_PEERS):
                _pass_on_copy(refs[a], send_sems, recv_sems, a, idx, slots[k]).wait_send()
                _pass_on_copy(refs[a], send_sems, recv_sems, a, idx, slots[k ^ 1]).wait_recv()

    res = pl.pallas_call(
        body, name=name, out_shape=tuple(pltpu.HBM(l_.shape, l_.dtype) for l_ in passed["lands"]),
        in_specs=[HBM_SPEC] * n + [SEM_SPEC, SEM_SPEC, pl.BlockSpec(memory_space=pl.ANY)],
        out_specs=tuple([HBM_SPEC] * n), input_output_aliases={i: i for i in range(n)},
        compiler_params=pltpu.CompilerParams(has_side_effects=DATAFLOW),
    )(*passed["lands"], passed["send"], passed["recv"], after)
    return list(res)


def _mm_nn(name, x, w, w_spec, n_out, tn, out_dtype, relu2=False, tm=MM_TM):
    m, k = x.shape

    def body(x_ref, w_ref, *outs):
        acc = jnp.dot(x_ref[...], w_ref[...], preferred_element_type=F32)
        if relu2:
            acc = jnp.maximum(acc, 0.0)
            acc = acc * acc
        outs[0][...] = acc.astype(outs[0].dtype)

    o_spec = pl.BlockSpec((tm, tn), lambda i, j: (i, j))
    return pl.pallas_call(
        body, name=name, out_shape=jax.ShapeDtypeStruct((m, n_out), out_dtype), grid=(m // tm, n_out // tn),
        in_specs=[pl.BlockSpec((tm, k), lambda i, j: (i, 0)), w_spec],
        out_specs=o_spec, compiler_params=_params(("parallel", "parallel")),
    )(x, w)


def _mm_nt(name, pairs, m, n_out, tn, out_dtype, relu_bwd_of=None, tm=MM_TM, after=None):
    np_ = len(pairs)

    def body(*refs):
        acc = None
        for p in range(np_):
            if len(pairs[p]) == 5:
                d = jnp.dot(refs[2 * p][...], refs[2 * p + 1][...], preferred_element_type=F32)
            else:
                d = _dot_nt(refs[2 * p][...], refs[2 * p + 1][...])
            acc = d if acc is None else acc + d
        if relu_bwd_of is not None:
            acc = acc * (2.0 * jnp.sqrt(refs[2 * np_][...].astype(F32)))
        refs[-1][...] = acc.astype(refs[-1].dtype)

    o_spec = pl.BlockSpec((tm, tn), lambda i, j: (i, j))
    operands, specs = [], []
    for pair in pairs:
        operands += [pair[0], pair[2]]
        specs += [pair[1], pair[3]]
    if relu_bwd_of is not None:
        operands.append(relu_bwd_of)
        specs.append(o_spec)
    if after is not None:
        operands.append(after)
        specs.append(pl.BlockSpec((8, 128), lambda i, j: (0, 0)))
    return pl.pallas_call(
        body, name=name, out_shape=jax.ShapeDtypeStruct((m, n_out), out_dtype), grid=(m // tm, n_out // tn),
        in_specs=specs, out_specs=o_spec, compiler_params=_params(("parallel", "parallel")),
    )(*operands)


def _mm_tn(name, x, x_col, ta, dy, dy_col, tb, out_shape, out_spec, grid_ab):
    m = x.shape[0]

    def body(x_ref, dy_ref, o_ref):
        o_ref[...] = _dot_tn(x_ref[...], dy_ref[...]).astype(BF16)

    return pl.pallas_call(
        body, name=name, out_shape=jax.ShapeDtypeStruct(out_shape, BF16), grid=grid_ab,
        in_specs=[pl.BlockSpec((m, ta), lambda a, b: (0, x_col(a))),
                  pl.BlockSpec((m, tb), lambda a, b: (0, dy_col(b)))],
        out_specs=out_spec, compiler_params=_params(("parallel", "parallel")),
    )(x, dy)


def _w_cols(k, tn, off_blocks):
    return pl.BlockSpec((k, tn), lambda i, j: (0, off_blocks + j))


def _a_rows(kw, col_block=0, tm=MM_TM):
    return pl.BlockSpec((tm, kw), lambda i, j: (i, col_block))


def _w_rows(tn, kw, col_block=0):
    return pl.BlockSpec((tn, kw), lambda i, j: (j, col_block))


def _wt_rows(tn, off):
    return pl.BlockSpec((pl.Element(tn), pl.Element(SEG)), lambda i, j: (pl.multiple_of(off + tn * j, 16), 0))


def _wt_block(k, off, tn):
    return pl.BlockSpec((pl.Element(k), pl.Element(tn)), lambda i, j: (off, pl.multiple_of(tn * j, 128)))


def _grad_w(name, x, dy):
    na, nb = x.shape[1], dy.shape[1]
    ta, tb = min(na, 1024), min(nb, 512)
    return _mm_tn(name, x, lambda a: a, ta, dy, lambda b: b, tb, (na, nb),
                  pl.BlockSpec((ta, tb), lambda a, b: (a, b)), (na // ta, nb // tb))


def _row_spec(width):
    return pl.BlockSpec((ROW_TILE, width), lambda i: (i, 0))


def _vec_spec(width):
    return pl.BlockSpec((1, width), lambda i: (0, 0))


def _rms_fwd(name, h, g, res=None):
    n, d = h.shape
    tile = ROW_TILE if n % ROW_TILE == 0 else n
    row_spec = pl.BlockSpec((tile, d), lambda i: (i, 0))

    def body(*refs):
        if res is None:
            h_ref, g_ref, hn_ref = refs
            hv = h_ref[...]
        else:
            h_ref, r_ref, g_ref, hs_ref, hn_ref = refs
            hv = h_ref[...] + r_ref[...]
            hs_ref[...] = hv
        r = lax.rsqrt(jnp.mean(hv * hv, axis=-1, keepdims=True) + RMS_EPS)
        hn_ref[...] = (hv * r * g_ref[...]).astype(BF16)

    ins = [h, g] if res is None else [h, res, g]
    in_specs = [row_spec, _vec_spec(d)] if res is None else [row_spec, row_spec, _vec_spec(d)]
    hn_shape = jax.ShapeDtypeStruct((n, d), BF16)
    if res is None:
        out_shape, out_specs = hn_shape, row_spec
    else:
        out_shape, out_specs = [jax.ShapeDtypeStruct((n, d), F32), hn_shape], [row_spec, row_spec]
    return pl.pallas_call(body, name=name, out_shape=out_shape, grid=(n // tile,), in_specs=in_specs,
                          out_specs=out_specs, compiler_params=_params(("parallel",)))(*ins)


def _rms_bwd(name, dhn, h, g, dres, batch, with_bf16=False, with_meta=False):
    n, d = h.shape
    t = n // batch
    nt = t // ROW_TILE

    def body(dhn_ref, h_ref, g_ref, dres_ref, *outs):
        first = (pl.program_id(0) == 0) & (pl.program_id(1) == 0)
        hv = h_ref[...]
        r = lax.rsqrt(jnp.mean(hv * hv, axis=-1, keepdims=True) + RMS_EPS)
        nrm = hv * r
        dhn = dhn_ref[...].astype(F32)
        dn = dhn * g_ref[...]
        dh = dres_ref[...] + r * (dn - nrm * jnp.mean(dn * nrm, axis=-1, keepdims=True))
        outs[0][...] = dh
        dg_ref = outs[1]

        @pl.when(first)
        def _():
            dg_ref[...] = jnp.zeros(dg_ref.shape, F32)

        dg_ref[...] += jnp.sum(dhn * nrm, axis=0, keepdims=True)
        nxt = 2
        if with_bf16:
            outs[nxt][...] = dh.astype(BF16)
            nxt += 1
        if with_meta:
            meta_ref = outs[nxt]

            @pl.when(first)
            def _():
                meta_ref[...] = jnp.zeros(meta_ref.shape, F32)

            @pl.when(pl.program_id(1) == 0)
            def _():
                meta_ref[...] += dh[0:N_META, :]

    row = pl.BlockSpec((ROW_TILE, d), lambda b, j: (b * nt + j, 0))
    vec = pl.BlockSpec((1, d), lambda b, j: (0, 0))
    shapes = [jax.ShapeDtypeStruct((n, d), F32), jax.ShapeDtypeStruct((1, d), F32)]
    specs = [row, vec]
    if with_bf16:
        shapes.append(jax.ShapeDtypeStruct((n, d), BF16))
        specs.append(row)
    if with_meta:
        shapes.append(jax.ShapeDtypeStruct((N_META, d), F32))
        specs.append(pl.BlockSpec((N_META, d), lambda b, j: (0, 0)))
    return pl.pallas_call(body, name=name, out_shape=shapes, grid=(batch, nt), in_specs=[row, row, vec, row],
                          out_specs=specs, compiler_params=_params(("arbitrary", "arbitrary")))(dhn, h, g, dres)


def _final(name, h1, dn, tgt, g, batch):
    n, d = h1.shape
    t = n // batch
    nt = t // ROW_TILE

    def body(h1_ref, dn_ref, tgt_ref, g_ref, dh_ref, dhb_ref, loss_ref, dg_ref):
        first = (pl.program_id(0) == 0) & (pl.program_id(1) == 0)
        hv = h1_ref[...] + dn_ref[...]
        r = lax.rsqrt(jnp.mean(hv * hv, axis=-1, keepdims=True) + RMS_EPS)
        nrm = hv * r
        gv = g_ref[...]
        pos = pl.program_id(1) * ROW_TILE + lax.broadcasted_iota(jnp.int32, (ROW_TILE, 1), 0)
        diff = jnp.where(pos >= N_META, nrm * gv - tgt_ref[...], 0.0)
        dy = diff * (1.0 / d)

        @pl.when(first)
        def _():
            loss_ref[...] = jnp.zeros(loss_ref.shape, F32)
            dg_ref[...] = jnp.zeros(dg_ref.shape, F32)

        loss_ref[...] += jnp.full(loss_ref.shape, 0.5 / d, F32) * jnp.sum(diff * diff)
        dg_ref[...] += jnp.sum(dy * nrm, axis=0, keepdims=True)
        dng = dy * gv
        dh = r * (dng - nrm * jnp.mean(dng * nrm, axis=-1, keepdims=True))
        dh_ref[...] = dh
        dhb_ref[...] = dh.astype(BF16)

    row = pl.BlockSpec((ROW_TILE, d), lambda b, j: (b * nt + j, 0))
    vec = pl.BlockSpec((1, d), lambda b, j: (0, 0))
    return pl.pallas_call(
        body, name=name, grid=(batch, nt), in_specs=[row, row, row, vec],
        out_shape=[jax.ShapeDtypeStruct((n, d), F32), jax.ShapeDtypeStruct((n, d), BF16),
                   jax.ShapeDtypeStruct((8, 128), F32), jax.ShapeDtypeStruct((1, d), F32)],
        out_specs=[row, row, pl.BlockSpec((8, 128), lambda b, j: (0, 0)), vec],
        compiler_params=_params(("arbitrary", "arbitrary")))(h1, dn, tgt, g)


def _ln_silu_fwd(name, c1, g, b):
    n, d = c1.shape

    def body(c_ref, g_ref, b_ref, o_ref):
        xv = c_ref[...]
        xc = xv - jnp.mean(xv, axis=-1, keepdims=True)
        rstd = lax.rsqrt(jnp.mean(xc * xc, axis=-1, keepdims=True) + LN_EPS)
        c2 = xc * rstd * g_ref[...] + b_ref[...]
        o_ref[...] = (c2 * _sigmoid(c2)).astype(BF16)

    return pl.pallas_call(body, name=name, out_shape=jax.ShapeDtypeStruct((n, d), BF16), grid=(n // ROW_TILE,),
                          in_specs=[_row_spec(d), _vec_spec(d), _vec_spec(d)], out_specs=_row_spec(d),
                          compiler_params=_params(("parallel",)))(c1, g, b)


def _ln_silu_bwd(name, dc3, c1, g, b):
    n, d = c1.shape

    def body(d_ref, c_ref, g_ref, b_ref, dc1_ref, dg_ref, db_ref):
        xv = c_ref[...]
        xc = xv - jnp.mean(xv, axis=-1, keepdims=True)
        rstd = lax.rsqrt(jnp.mean(xc * xc, axis=-1, keepdims=True) + LN_EPS)
        xh = xc * rstd
        c2 = xh * g_ref[...] + b_ref[...]
        s = _sigmoid(c2)
        dc2 = d_ref[...].astype(F32) * (s * (1.0 + c2 * (1.0 - s)))

        @pl.when(pl.program_id(0) == 0)
        def _():
            dg_ref[...] = jnp.zeros(dg_ref.shape, F32)
            db_ref[...] = jnp.zeros(db_ref.shape, F32)

        dg_ref[...] += jnp.sum(dc2 * xh, axis=0, keepdims=True)
        db_ref[...] += jnp.sum(dc2, axis=0, keepdims=True)
        dxh = dc2 * g_ref[...]
        dc1_ref[...] = rstd * (dxh - jnp.mean(dxh, axis=-1, keepdims=True)
                               - xh * jnp.mean(dxh * xh, axis=-1, keepdims=True))

    return pl.pallas_call(
        body, name=name, grid=(n // ROW_TILE,),
        out_shape=[jax.ShapeDtypeStruct((n, d), F32), jax.ShapeDtypeStruct((1, d), F32),
                   jax.ShapeDtypeStruct((1, d), F32)],
        in_specs=[_row_spec(d), _row_spec(d), _vec_spec(d), _vec_spec(d)],
        out_specs=[_row_spec(d), _vec_spec(d), _vec_spec(d)],
        compiler_params=_params(("arbitrary",)))(dc3, c1, g, b)


MERGE_TC = 512


def _merge_fwd(name, gates, a, c, b_co):
    n, d = a.shape
    nc = d // MERGE_TC

    def body(ga_ref, gc_ref, a_ref, c_ref, b_ref, m_ref):
        f32 = lambda r_: r_[...].astype(F32)
        m = _sigmoid(f32(ga_ref)) * f32(a_ref) + _sigmoid(f32(gc_ref)) * (f32(c_ref) + b_ref[...])
        m_ref[...] = m.astype(BF16)

    blk = lambda off: pl.BlockSpec((ROW_TILE, MERGE_TC), lambda i, j: (i, off + j))
    return pl.pallas_call(
        body, name=name, out_shape=jax.ShapeDtypeStruct((n, d), BF16), grid=(n // ROW_TILE, nc),
        in_specs=[blk(0), blk(nc), blk(0), blk(0), pl.BlockSpec((1, MERGE_TC), lambda i, j: (0, j))],
        out_specs=blk(0), compiler_params=_params(("parallel", "parallel")))(gates, gates, a, c, b_co)


def _merge_bwd(name, dm, gates, a, c, b_co):
    n, d = a.shape
    nc = d // MERGE_TC

    def body(dm_ref, ga_ref, gc_ref, a_ref, c_ref, b_ref, da_ref, dc_ref, dga_ref, dgc_ref, dbco_ref):
        f32 = lambda r_: r_[...].astype(F32)
        dmv = f32(dm_ref)
        sa, sc = _sigmoid(f32(ga_ref)), _sigmoid(f32(gc_ref))
        dc = dmv * sc
        da_ref[...] = (dmv * sa).astype(BF16)
        dc_ref[...] = dc.astype(BF16)
        dga_ref[...] = (dmv * f32(a_ref) * sa * (1.0 - sa)).astype(BF16)
        dgc_ref[...] = (dmv * (f32(c_ref) + b_ref[...]) * sc * (1.0 - sc)).astype(BF16)

        @pl.when(pl.program_id(1) == 0)
        def _():
            dbco_ref[...] = jnp.zeros(dbco_ref.shape, F32)

        dbco_ref[...] += jnp.sum(dc, axis=0, keepdims=True)

    blk = lambda off: pl.BlockSpec((ROW_TILE, MERGE_TC), lambda j, i: (i, off + j))
    vec = pl.BlockSpec((1, MERGE_TC), lambda j, i: (0, j))
    act = jax.ShapeDtypeStruct((n, d), BF16)
    return pl.pallas_call(
        body, name=name, grid=(nc, n // ROW_TILE),
        out_shape=[act, act, act, act, jax.ShapeDtypeStruct((1, d), F32)],
        in_specs=[blk(0), blk(0), blk(nc), blk(0), blk(0), vec],
        out_specs=[blk(0), blk(0), blk(0), blk(0), vec],
        compiler_params=_params(("parallel", "arbitrary")))(dm, gates, gates, a, c, b_co)


CONV_TC = 128
CONV_HALO = 32


def _conv_chunk(t):
    return 48 if t % 48 == 0 else 32 if t % 32 == 0 else 16


def _fold8(x):
    out = x[0:8]
    for k in range(1, x.shape[0] // 8):
        out = out + x[8 * k:8 * k + 8]
    return out


def _glu_conv_fwd(name, glu, b_glu, w_dw, b_dw, batch):
    n, c2 = glu.shape
    c = c2 // 2
    t = n // batch
    nc = c // CONV_TC

    def body(a_ref, gt_ref, ba_ref, bg_ref, w_ref, bdw_ref, o_ref, pad_ref):
        u = (a_ref[...].astype(F32) + ba_ref[...]) * _sigmoid(gt_ref[...].astype(F32) + bg_ref[...])
        pad_ref[0:CONV_HALO, :] = jnp.zeros((CONV_HALO, CONV_TC), F32)
        pad_ref[CONV_HALO:CONV_HALO + t, :] = u
        ch = _conv_chunk(t)
        for r0 in range(0, t, ch):
            acc = jnp.zeros((ch, CONV_TC), F32) + bdw_ref[...]
            for j in range(CONV_W):
                off = r0 + CONV_HALO - (CONV_W - 1) + j
                acc = acc + w_ref[j:j + 1, :] * pad_ref[off:off + ch, :]
            o_ref[r0:r0 + ch, :] = acc

    seq = lambda off: pl.BlockSpec((t, CONV_TC), lambda b, j: (b, off + j))
    vec = lambda off: pl.BlockSpec((1, CONV_TC), lambda b, j: (0, off + j))
    return pl.pallas_call(
        body, name=name, out_shape=jax.ShapeDtypeStruct((n, c), F32), grid=(batch, nc),
        in_specs=[seq(0), seq(nc), vec(0), vec(nc), pl.BlockSpec((CONV_W, CONV_TC), lambda b, j: (0, j)), vec(0)],
        out_specs=seq(0), scratch_shapes=[pltpu.VMEM((t + CONV_HALO, CONV_TC), F32)],
        compiler_params=_params(("parallel", "parallel")))(glu, glu, b_glu, b_glu, w_dw, b_dw)


def _glu_conv_bwd(name, dc1, glu, b_glu, w_dw, batch):
    n, c2 = glu.shape
    c = c2 // 2
    t = n // batch
    nc = c // CONV_TC

    def body(d_ref, a_ref, gt_ref, ba_ref, bg_ref, w_ref, dga_ref, dgg_ref, dw_ref, dbdw_ref, dba_ref, dbg_ref,
             padu_ref, padd_ref):
        av = a_ref[...].astype(F32) + ba_ref[...]
        sg = _sigmoid(gt_ref[...].astype(F32) + bg_ref[...])
        dc = d_ref[...]
        padu_ref[0:CONV_HALO, :] = jnp.zeros((CONV_HALO, CONV_TC), F32)
        padu_ref[CONV_HALO:CONV_HALO + t, :] = av * sg
        padd_ref[0:t, :] = dc
        padd_ref[t:t + CONV_HALO, :] = jnp.zeros((CONV_HALO, CONV_TC), F32)

        @pl.when(pl.program_id(1) == 0)
        def _():
            dw_ref[...] = jnp.zeros(dw_ref.shape, F32)
            dbdw_ref[...] = jnp.zeros(dbdw_ref.shape, F32)
            dba_ref[...] = jnp.zeros(dba_ref.shape, F32)
            dbg_ref[...] = jnp.zeros(dbg_ref.shape, F32)

        ch = _conv_chunk(t)
        zero8 = jnp.zeros((8, CONV_TC), F32)
        dw_acc = [zero8] * CONV_W
        sum_dc, sum_a, sum_g = zero8, zero8, zero8
        for r0 in range(0, t, ch):
            dcc = d_ref[r0:r0 + ch, :]
            du = jnp.zeros((ch, CONV_TC), F32)
            for j in range(CONV_W):
                back = r0 + CONV_W - 1 - j
                du = du + w_ref[j:j + 1, :] * padd_ref[back:back + ch, :]
                off = r0 + CONV_HALO - (CONV_W - 1) + j
                dw_acc[j] = dw_acc[j] + _fold8(dcc * padu_ref[off:off + ch, :])
            sgc = _sigmoid(gt_ref[r0:r0 + ch, :].astype(F32) + bg_ref[...])
            dga = du * sgc
            dgg = du * padu_ref[CONV_HALO + r0:CONV_HALO + r0 + ch, :] * (1.0 - sgc)
            dga_ref[r0:r0 + ch, :] = dga.astype(BF16)
            dgg_ref[r0:r0 + ch, :] = dgg.astype(BF16)
            sum_dc, sum_a, sum_g = sum_dc + _fold8(dcc), sum_a + _fold8(dga), sum_g + _fold8(dgg)
        for j in range(CONV_W):
            dw_ref[j:j + 1, :] += jnp.sum(dw_acc[j], axis=0, keepdims=True)
        dbdw_ref[...] += jnp.sum(sum_dc, axis=0, keepdims=True)
        dba_ref[...] += jnp.sum(sum_a, axis=0, keepdims=True)
        dbg_ref[...] += jnp.sum(sum_g, axis=0, keepdims=True)

    seq = lambda off: pl.BlockSpec((t, CONV_TC), lambda j, b: (b, off + j))
    vec = lambda off: pl.BlockSpec((1, CONV_TC), lambda j, b: (0, off + j))
    wsp = pl.BlockSpec((CONV_W, CONV_TC), lambda j, b: (0, j))
    act = jax.ShapeDtypeStruct((n, c), BF16)
    v = jax.ShapeDtypeStruct((1, c), F32)
    return pl.pallas_call(
        body, name=name, grid=(nc, batch),
        out_shape=[act, act, jax.ShapeDtypeStruct((CONV_W, c), F32), v, v, v],
        in_specs=[seq(0), seq(0), seq(nc), vec(0), vec(nc), wsp],
        out_specs=[seq(0), seq(0), wsp, vec(0), vec(0), vec(0)],
        scratch_shapes=[pltpu.VMEM((t + CONV_HALO, CONV_TC), F32), pltpu.VMEM((t + CONV_HALO, CONV_TC), F32)],
        compiler_params=_params(("parallel", "arbitrary")))(dc1, glu, glu, b_glu, b_glu, w_dw)


def _split3(x):
    hi = x.astype(BF16)
    r = x - hi.astype(F32)
    mid = r.astype(BF16)
    lo = (r - mid.astype(F32)).astype(BF16)
    return hi, mid, lo


def _tri_matmul(tri, x):
    hi, mid, lo = _split3(x)
    dot = lambda v: jnp.dot(tri, v, preferred_element_type=F32)
    return dot(hi) + dot(mid) + dot(lo)


def _fox_prep_fwd(name, fg, b_fg, batch):
    n, w = fg.shape
    t = n // batch
    nq = t // ROW_TILE

    def body(fg_ref, b_ref, cum_ref):
        row = lax.broadcasted_iota(jnp.int32, (ROW_TILE, ROW_TILE), 0)
        col = lax.broadcasted_iota(jnp.int32, (ROW_TILE, ROW_TILE), 1)
        tri = (row >= col).astype(BF16)
        for k in range(nq):
            rows = slice(k * ROW_TILE, (k + 1) * ROW_TILE)
            z = fg_ref[rows, :] + b_ref[...]
            logf = jnp.minimum(z, 0.0) - jnp.log(1.0 + jnp.exp(-jnp.abs(z)))
            cum = _tri_matmul(tri, logf)
            if k > 0:
                cum = cum + cum_ref[k * ROW_TILE - 1:k * ROW_TILE, :]
            cum_ref[rows, :] = cum

    seq = pl.BlockSpec((t, w), lambda b: (b, 0))
    return pl.pallas_call(body, name=name, out_shape=jax.ShapeDtypeStruct((n, w), F32), grid=(batch,),
                          in_specs=[seq, pl.BlockSpec((1, w), lambda b: (0, 0))], out_specs=seq,
                          compiler_params=_params(("parallel",)))(fg, b_fg)


def _fox_prep_bwd(name, dcum_k, dcum_q, fg, b_fg, batch):
    n, w = fg.shape
    t = n // batch
    nq = t // ROW_TILE

    def body(dk_ref, dq_ref, fg_ref, b_ref, dfg_ref, db_ref, rev_ref):
        row = lax.broadcasted_iota(jnp.int32, (ROW_TILE, ROW_TILE), 0)
        col = lax.broadcasted_iota(jnp.int32, (ROW_TILE, ROW_TILE), 1)
        tri = (col >= row).astype(BF16)

        @pl.when(pl.program_id(0) == 0)
        def _():
            db_ref[...] = jnp.zeros(db_ref.shape, F32)

        for k in reversed(range(nq)):
            rows = slice(k * ROW_TILE, (k + 1) * ROW_TILE)
            dlog = _tri_matmul(tri, dk_ref[rows, :] + dq_ref[rows, :])
            if k < nq - 1:
                dlog = dlog + rev_ref[(k + 1) * ROW_TILE:(k + 1) * ROW_TILE + 1, :]
            rev_ref[rows, :] = dlog
            dfg = dlog * _sigmoid(-(fg_ref[rows, :] + b_ref[...]))
            dfg_ref[rows, :] = dfg.astype(BF16)
            db_ref[...] += jnp.sum(dfg, axis=0, keepdims=True)

    seq = pl.BlockSpec((t, w), lambda b: (b, 0))
    vec = pl.BlockSpec((1, w), lambda b: (0, 0))
    return pl.pallas_call(
        body, name=name, grid=(batch,),
        out_shape=[jax.ShapeDtypeStruct((n, w), BF16), jax.ShapeDtypeStruct((1, w), F32)],
        in_specs=[seq, seq, seq, vec], out_specs=[seq, vec], scratch_shapes=[pltpu.VMEM((t, w), F32)],
        compiler_params=_params(("arbitrary",)))(dcum_k, dcum_q, fg, b_fg)


def _head_masks(x):
    lane = lax.broadcasted_iota(jnp.int32, x.shape, 1)
    zero = jnp.zeros(x.shape, x.dtype)
    return jnp.where(lane < HEAD_DIM, x, zero), jnp.where(lane >= HEAD_DIM, x, zero)


ATTN_BLOCK = 512


def _attn_blocks(t):
    nb = max(t // ATTN_BLOCK, 1)
    blocks = [(i * ATTN_BLOCK, ATTN_BLOCK) for i in range(nb - 1)]
    return blocks + [((nb - 1) * ATTN_BLOCK, t - (nb - 1) * ATTN_BLOCK)]


def _attn_specs(t):
    blocks = _attn_blocks(t)
    width = max(sz for _, sz in blocks)
    qkv = lambda off: pl.BlockSpec((t, LANES), lambda b, h: (b, off + h))
    cumr = pl.BlockSpec((None, None, len(blocks), 8, width), lambda b, h: (b, h, 0, 0, 0))
    return qkv, cumr


def _key_sums_to_blocks(cum, batch, t, n_pairs):
    blocks = _attn_blocks(t)
    width = max(sz for _, sz in blocks)
    cum_h = cum.reshape(batch, t, -1)[:, :, :2 * n_pairs].reshape(batch, t, n_pairs, 2)
    rows = [jnp.pad(jnp.transpose(cum_h[:, s0:s0 + sz], (0, 2, 3, 1)), ((0, 0), (0, 0), (0, 6), (0, width - sz)))
            for s0, sz in blocks]
    return jnp.stack(rows, axis=2)


def _key_sums_from_blocks(dcumr, batch, t, n_pairs):
    cols = [jnp.transpose(dcumr[:, :, j, :2, :sz], (0, 3, 1, 2)) for j, (_, sz) in enumerate(_attn_blocks(t))]
    return jnp.concatenate(cols, axis=1).reshape(batch * t, 2 * n_pairs)


def _causal(size):
    row = lax.broadcasted_iota(jnp.int32, (size, size), 0)
    col = lax.broadcasted_iota(jnp.int32, (size, size), 1)
    return row >= col


def _attn_fwd(name, qkv, cumr, batch):
    n, w3 = qkv.shape
    w = w3 // 3
    t = n // batch
    n_pairs = w // LANES
    blocks = _attn_blocks(t)

    def body(q_ref, k_ref, v_ref, cr_ref, o_ref, lse_ref):
        pair = pl.program_id(1)

        @pl.when(pair == 0)
        def _():
            lse_ref[...] = jnp.zeros(lse_ref.shape, F32)

        for i, (q0, qn) in enumerate(blocks):
            rows = slice(q0, q0 + qn)
            qs = _head_masks(q_ref[rows, :] * (0.125 * LOG2E))
            outs, lses = [], []
            for hh in range(2):
                m = jnp.full((qn, 1), NEG, F32)
                l = jnp.zeros((qn, 1), F32)
                acc = jnp.zeros((qn, LANES), F32)
                for j in range(i + 1):
                    k0, kn = blocks[j]
                    cols = slice(k0, k0 + kn)
                    s = _dot_nt(qs[hh], k_ref[cols, :]) - cr_ref[j, hh:hh + 1, 0:kn] * LOG2E
                    if j == i:
                        s = jnp.where(_causal(qn), s, NEG)
                    m_new = jnp.maximum(m, jnp.max(s, axis=1, keepdims=True))
                    alpha = jnp.exp2(m - m_new)
                    p = jnp.exp2(s - m_new)
                    l = alpha * l + jnp.sum(p, axis=1, keepdims=True)
                    acc = alpha * acc + jnp.dot(p.astype(BF16), v_ref[cols, :], preferred_element_type=F32)
                    m = m_new
                outs.append(acc / l)
                lses.append(m + jnp.log2(l))
            lane = lax.broadcasted_iota(jnp.int32, (qn, LANES), 1)
            o_ref[rows, :] = jnp.where(lane < HEAD_DIM, outs[0], outs[1]).astype(BF16)
            lse_ref[rows, :] = jnp.where(lane == 2 * pair, lses[0],
                                         jnp.where(lane == 2 * pair + 1, lses[1], lse_ref[rows, :]))

    qkv_spec, cumr_spec = _attn_specs(t)
    return pl.pallas_call(
        body, name=name, grid=(batch, n_pairs),
        out_shape=[jax.ShapeDtypeStruct((n, w), BF16), jax.ShapeDtypeStruct((n, LANES), F32)],
        in_specs=[qkv_spec(0), qkv_spec(n_pairs), qkv_spec(2 * n_pairs), cumr_spec],
        out_specs=[qkv_spec(0), pl.BlockSpec((t, LANES), lambda b, h: (b, 0))],
        compiler_params=_params(("parallel", "arbitrary")))(qkv, qkv, qkv, cumr)


def _attn_bwd(name, qkv, o, do, lse, cumr, batch):
    n, w3 = qkv.shape
    w = w3 // 3
    t = n // batch
    n_pairs = w // LANES
    blocks = _attn_blocks(t)

    def body(q_ref, k_ref, v_ref, o_ref, do_ref, lse_ref, cr_ref, dq_ref, dk_ref, dv_ref, dcr_ref, dcq_ref,
             dk_acc, dv_acc):
        pair = pl.program_id(1)
        dk_acc[...] = jnp.zeros(dk_acc.shape, F32)
        dv_acc[...] = jnp.zeros(dv_acc.shape, F32)
        dcr_ref[...] = jnp.zeros(dcr_ref.shape, F32)

        @pl.when(pair == 0)
        def _():
            dcq_ref[...] = jnp.zeros(dcq_ref.shape, F32)

        for i, (q0, qn) in enumerate(blocks):
            rows = slice(q0, q0 + qn)
            qs = _head_masks(q_ref[rows, :] * 0.125)
            q2 = _head_masks(q_ref[rows, :] * (0.125 * LOG2E))
            dos = _head_masks(do_ref[rows, :])
            dq = jnp.zeros((qn, LANES), F32)
            lane = lax.broadcasted_iota(jnp.int32, (qn, LANES), 1)
            dcq = []
            for hh in range(2):
                row_sum = jnp.zeros((qn, 1), F32)
                lse = jnp.sum(jnp.where(lane == 2 * pair + hh, lse_ref[rows, :], 0.0), axis=1, keepdims=True)
                delta = jnp.sum(dos[hh].astype(F32) * o_ref[rows, :].astype(F32), axis=1, keepdims=True)
                for j in range(i + 1):
                    k0, kn = blocks[j]
                    cols = slice(k0, k0 + kn)
                    s = _dot_nt(q2[hh], k_ref[cols, :]) - cr_ref[j, hh:hh + 1, 0:kn] * LOG2E
                    p = jnp.exp2(s - lse)
                    if j == i:
                        p = jnp.where(_causal(qn), p, 0.0)
                    dp = _dot_nt(dos[hh], v_ref[cols, :])
                    ds = p * (dp - delta)
                    pb, dsb = p.astype(BF16), ds.astype(BF16)
                    km = _head_masks(k_ref[cols, :])[hh]
                    dv_acc[cols, :] += _dot_tn(pb, dos[hh])
                    dk_acc[cols, :] += _dot_tn(dsb, qs[hh])
                    dq = dq + jnp.dot(dsb, km, preferred_element_type=F32)
                    dcr_ref[j, hh:hh + 1, 0:kn] -= jnp.sum(ds, axis=0, keepdims=True)
                    row_sum = row_sum + jnp.sum(ds, axis=1, keepdims=True)
                dcq.append(row_sum)
            dq_ref[rows, :] = (dq * 0.125).astype(BF16)
            dcq_ref[rows, :] = jnp.where(lane == 2 * pair, dcq[0],
                                         jnp.where(lane == 2 * pair + 1, dcq[1], dcq_ref[rows, :]))
        dk_ref[...] = dk_acc[...].astype(BF16)
        dv_ref[...] = dv_acc[...].astype(BF16)

    qkv_spec, cumr_spec = _attn_specs(t)
    act = jax.ShapeDtypeStruct((n, w), BF16)
    return pl.pallas_call(
        body, name=name, grid=(batch, n_pairs),
        out_shape=[act, act, act, jax.ShapeDtypeStruct(cumr.shape, F32), jax.ShapeDtypeStruct((n, LANES), F32)],
        in_specs=[qkv_spec(0), qkv_spec(n_pairs), qkv_spec(2 * n_pairs), qkv_spec(0), qkv_spec(0),
                  pl.BlockSpec((t, LANES), lambda b, h: (b, 0)), cumr_spec],
        out_specs=[qkv_spec(0), qkv_spec(0), qkv_spec(0), cumr_spec, pl.BlockSpec((t, LANES), lambda b, h: (b, 0))],
        scratch_shapes=[pltpu.VMEM((t, LANES), F32), pltpu.VMEM((t, LANES), F32)],
        compiler_params=_params(("parallel", "arbitrary")))(qkv, qkv, qkv, o, do, lse, cumr)


def _adamw(name, parts, w, m, v):
    r, c = w.shape
    tr = 128 if r % 128 == 0 else r
    tc = 256 if tr > 128 and c % 256 == 0 else c
    c1 = 1.0 - ADAM_B1 ** ADAM_STEP
    c2 = 1.0 - ADAM_B2 ** ADAM_STEP

    def body(p_ref, w_ref, m_ref, v_ref, g_ref, d_ref, m2_ref, v2_ref):
        g = p_ref[0].astype(F32)
        for s in range(1, N_DEV):
            g = g + p_ref[s].astype(F32)
        m2 = ADAM_B1 * m_ref[...] + (1.0 - ADAM_B1) * g
        v2 = ADAM_B2 * v_ref[...] + (1.0 - ADAM_B2) * (g * g)
        g_ref[...] = g
        m2_ref[...] = m2
        v2_ref[...] = v2
        d_ref[...] = -ADAM_LR * ((m2 / c1) / (jnp.sqrt(v2 / c2) + ADAM_EPS) + ADAM_WD * w_ref[...])

    blk = pl.BlockSpec((tr, tc), lambda i, j: (i, j))
    shp = jax.ShapeDtypeStruct((r, c), F32)
    return pl.pallas_call(
        body, name=name, out_shape=[shp] * 4, grid=(r // tr, c // tc),
        in_specs=[pl.BlockSpec((N_DEV, tr, tc), lambda i, j: (0, i, j)), blk, blk, blk], out_specs=[blk] * 4,
        compiler_params=_params(("parallel", "parallel")))(parts, w, m, v)


def _cat_small(vals):
    parts = []
    for name in SMALL:
        v = vals[name].reshape(1, -1).astype(F32)
        parts.append(jnp.pad(v, ((0, 0), (0, SMALL_W[name] - v.shape[1]))))
    return jnp.concatenate(parts, axis=1)


def _split_small(row, shapes):
    out, off = {}, 0
    for name in SMALL:
        out[name] = row[0, off:off + SMALL_N[name]].reshape(shapes[name])
        off += SMALL_W[name]
    return out


def _cols_from_shards(g):
    return jnp.transpose(g, (1, 0, 2)).reshape(g.shape[1], N_DEV * g.shape[2])


def _shards_from_cols(a):
    r, c = a.shape
    return jnp.transpose(a.reshape(r, N_DEV, c // N_DEV), (1, 0, 2))


def kernel(x, meta_tokens, norm_mix_gain, w_in, b_forget, w_attn_out, b_glu, conv_dw_w, conv_dw_b, conv_ln_gain, conv_ln_bias, w_conv_out, b_conv_out, w_out, norm_mlp_gain, w_mlp_up, w_mlp_down, final_norm_gain, loss_target, m_meta_tokens, m_norm_mix_gain, m_w_in, m_b_forget, m_w_attn_out, m_b_glu, m_conv_dw_w, m_conv_dw_b, m_conv_ln_gain, m_conv_ln_bias, m_w_conv_out, m_b_conv_out, m_w_out, m_norm_mlp_gain, m_w_mlp_up, m_w_mlp_down, m_final_norm_gain, v_meta_tokens, v_norm_mix_gain, v_w_in, v_b_forget, v_w_attn_out, v_b_glu, v_conv_dw_w, v_conv_dw_b, v_conv_ln_gain, v_conv_ln_bias, v_w_conv_out, v_b_conv_out, v_w_out, v_norm_mlp_gain, v_w_mlp_up, v_w_mlp_down, v_final_norm_gain):
    weights = dict(meta_tokens=meta_tokens, norm_mix_gain=norm_mix_gain, w_in=w_in, b_forget=b_forget, w_attn_out=w_attn_out, b_glu=b_glu, conv_dw_w=conv_dw_w, conv_dw_b=conv_dw_b, conv_ln_gain=conv_ln_gain, conv_ln_bias=conv_ln_bias, w_conv_out=w_conv_out, b_conv_out=b_conv_out, w_out=w_out, norm_mlp_gain=norm_mlp_gain, w_mlp_up=w_mlp_up, w_mlp_down=w_mlp_down, final_norm_gain=final_norm_gain)
    mom_m = dict(meta_tokens=m_meta_tokens, norm_mix_gain=m_norm_mix_gain, w_in=m_w_in, b_forget=m_b_forget, w_attn_out=m_w_attn_out, b_glu=m_b_glu, conv_dw_w=m_conv_dw_w, conv_dw_b=m_conv_dw_b, conv_ln_gain=m_conv_ln_gain, conv_ln_bias=m_conv_ln_bias, w_conv_out=m_w_conv_out, b_conv_out=m_b_conv_out, w_out=m_w_out, norm_mlp_gain=m_norm_mlp_gain, w_mlp_up=m_w_mlp_up, w_mlp_down=m_w_mlp_down, final_norm_gain=m_final_norm_gain)
    mom_v = dict(meta_tokens=v_meta_tokens, norm_mix_gain=v_norm_mix_gain, w_in=v_w_in, b_forget=v_b_forget, w_attn_out=v_w_attn_out, b_glu=v_b_glu, conv_dw_w=v_conv_dw_w, conv_dw_b=v_conv_dw_b, conv_ln_gain=v_conv_ln_gain, conv_ln_bias=v_conv_ln_bias, w_conv_out=v_w_conv_out, b_conv_out=v_b_conv_out, w_out=v_w_out, norm_mlp_gain=v_norm_mlp_gain, w_mlp_up=v_w_mlp_up, w_mlp_down=v_w_mlp_down, final_norm_gain=v_final_norm_gain)
    names = list(weights)
    batch, seq, d = x.shape
    t = seq + N_META
    n = batch * t
    n_pairs = d // LANES
    assert t % ROW_TILE == 0 and d == SEG

    to_rows = lambda w3: jnp.transpose(w3[0])
    w_in_t, m_in_t, v_in_t = to_rows(w_in), to_rows(m_w_in), to_rows(v_w_in)
    first = [w_in_t.astype(BF16), meta_tokens, conv_dw_w[0]]
    rest = [w_[0].astype(BF16) for w_ in (w_attn_out, w_conv_out, w_out, w_mlp_up, w_mlp_down)]
    tgt = jnp.concatenate([jnp.zeros((batch, N_META, d), F32), loss_target], axis=1).reshape(n, d)
    h0_rows = jnp.pad(x, ((0, 0), (N_META, 0), (0, 0)))
    g1 = norm_mix_gain.reshape(1, -1)
    hn1_rows = _rms_fwd("rms1", h0_rows.reshape(n, d), g1)
    gather_a = _exchange_start("gather_in_start", [(f_, False) for f_ in first], ks=CHIP_PEERS)
    level_1 = _exchange_wait("gather_in_wait", gather_a, [gather_a["token"], tgt, hn1_rows, w_in_t, m_in_t, v_in_t] + rest)
    passed = _pass_on_start("gather_in_pass_start", level_1)
    w_in_g, meta_g, w_dw_g = _pass_on_wait("gather_in_pass_wait", passed, passed["token"])
    gather_b = _exchange_start("gather_rest_start", [(r_, False) for r_ in rest])
    n_fg = b_forget.shape[1]
    shard_w = w_in.shape[2]
    wt = w_in_g.reshape(N_DEV * shard_w, d)
    o_fg = 3 * SEG
    seg_rows = [0, SEG, 2 * SEG] + [o_fg + n_fg + i * SEG for i in range(4)]
    d_ff = w_mlp_down.shape[1] * N_DEV
    ff_blk = d_ff // N_DEV
    meta_f = _cols_from_shards(meta_g)
    w_dw = _cols_from_shards(w_dw_g)

    row2 = lambda v: v.reshape(1, -1)
    g2, g3 = row2(norm_mlp_gain), row2(final_norm_gain)
    b_fg = jnp.pad(b_forget, ((0, 0), (0, FG_PAD - n_fg)))
    h0 = lax.dynamic_update_slice(h0_rows, jnp.broadcast_to(meta_f[None], (batch, N_META, d)), (0, 0, 0)).reshape(n, d)

    meta_n = jnp.broadcast_to(_rms_fwd("rms1_meta", meta_f, g1)[None], (batch, N_META, d))
    hn1 = lax.dynamic_update_slice(hn1_rows.reshape(batch, t, d), meta_n, (0, 0, 0)).reshape(n, d)
    proj = lambda name, off, width, tn, dt: _mm_nt(name, [(hn1, _a_rows(d), wt, _wt_rows(tn, off))], n, width, tn, dt,
                                                   after=gather_b["token"])
    qkv = proj("proj_qkv", 0, 3 * SEG, SEG, BF16)
    glu = proj("proj_glu", seg_rows[3], 2 * SEG, SEG, BF16)
    gates = proj("proj_gates", seg_rows[5], 2 * SEG, SEG, BF16)
    fg = proj("proj_fg", o_fg, FG_PAD, FG_PAD, F32)

    cum = _fox_prep_fwd("fox_cumsum", fg, b_fg, batch)
    cumr = _key_sums_to_blocks(cum, batch, t, n_pairs)
    o, lse = _attn_fwd("attn_fwd", qkv, cumr, batch)
    rest = _exchange_wait("gather_rest_wait", gather_b, o)
    w_ao, w_co, w_o = [r_.reshape(d, d) for r_ in rest[:3]]
    w_up = rest[3]
    w_dn = rest[4].reshape(d_ff, d)
    a = _mm_nn("attn_out", o, w_ao, _w_cols(d, d, 0), d, d, BF16)

    c1 = _glu_conv_fwd("glu_conv", glu, b_glu, w_dw, conv_dw_b, batch)
    c3 = _ln_silu_fwd("ln_silu", c1, conv_ln_gain, conv_ln_bias)
    c = _mm_nn("conv_out", c3, w_co, _w_cols(d, d, 0), d, d, BF16)

    mrg = _merge_fwd("merge", gates, a, c, b_conv_out)
    mo = _mm_nn("mix_out", mrg, w_o, _w_cols(d, d, 0), d, d, F32)
    h1, hn2 = _rms_fwd("resid_rms2", h0, g2, res=mo)
    per = ff_blk // 512
    act = _mm_nn("mlp_up", hn2, w_up, pl.BlockSpec((None, d, 512), lambda i, j: (j // per, 0, j % per)),
                 d_ff, 512, BF16, relu2=True)
    dn = _mm_nn("mlp_down", act, w_dn, _w_cols(d_ff, d, 0), d, d, F32, tm=ROW_TILE)
    dh2, dh2b, loss_blk, dg3 = _final("final_loss", h1, dn, tgt, g3, batch)

    dup = _mm_nt("d_mlp_down", [(dh2b, _a_rows(d), w_dn, _w_rows(d, d))], n, d_ff, d, BF16, relu_bwd_of=act)
    dw_dn = _grad_w("gw_mlp_down", act, dh2b)
    dhn2 = _mm_nt("d_mlp_up", [(dup, _a_rows(ff_blk, g), w_up, pl.BlockSpec((None, 512, ff_blk), lambda i, j, g=g: (g, j, 0)))
                               for g in range(N_DEV)], n, d, 512, BF16)
    dw_up = _mm_tn("gw_mlp_up", hn2, lambda a_: 0, d, dup, lambda b_: b_, ff_blk, (N_DEV, d, ff_blk),
                   pl.BlockSpec((None, d, ff_blk), lambda a_, b_: (b_, 0, 0)), (1, N_DEV))
    scatter_1 = _exchange_start("scatter_mlp_start", [(dw_dn.reshape(N_DEV, ff_blk, d), True), (dw_up, True)])
    dh1, dg2, dh1b = _rms_bwd("rms2_bwd", dhn2, h1, g2 + scatter_1["token"][0:1, 0:1], dh2, batch, with_bf16=True)

    dm = _mm_nt("d_mix_out", [(dh1b, _a_rows(d), w_o, _w_rows(d, d))], n, d, d, BF16)
    dw_o = _grad_w("gw_mix_out", mrg, dh1b)
    da, dc, dga, dgc, dbco = _merge_bwd("merge_bwd", dm, gates, a, c, b_conv_out)

    do = _mm_nt("d_attn_out", [(da, _a_rows(d), w_ao, _w_rows(d, d))], n, d, d, BF16)
    dw_ao = _grad_w("gw_attn_out", o, da)
    dc3 = _mm_nt("d_conv_out", [(dc, _a_rows(d), w_co, _w_rows(d, d))], n, d, d, BF16)
    dw_co = _grad_w("gw_conv_out", c3, dc)

    scatter_2 = _exchange_start("scatter_mix_start", [(dw_.reshape(N_DEV, d // N_DEV, d), True)
                                                      for dw_ in (dw_o, dw_ao, dw_co)])
    dc1, dg_ln, db_ln = _ln_silu_bwd("ln_silu_bwd", dc3, c1, conv_ln_gain + scatter_2["token"][0:1, 0:1],
                                     conv_ln_bias)
    dglu_a, dglu_g, dw_dw, db_dw, dbg_a, dbg_g = _glu_conv_bwd("glu_conv_bwd", dc1, glu, b_glu, w_dw, batch)

    dq, dk, dv, dcumr, dcum_q = _attn_bwd("attn_bwd", qkv, o, do, lse, cumr, batch)
    dcum_k = jnp.pad(_key_sums_from_blocks(dcumr, batch, t, n_pairs), ((0, 0), (0, FG_PAD - 2 * n_pairs)))
    dfg, db_fg = _fox_prep_bwd("fox_cumsum_bwd", dcum_k, dcum_q, fg, b_fg, batch)

    segs = [dq, dk, dv, dglu_a, dglu_g, dga, dgc]
    gw_t = [_grad_w("gw_in_%d" % i, s_, hn1) for i, s_ in enumerate(segs)]
    gw_fg = _grad_w("gw_in_fg", dfg, hn1)[:n_fg]
    dw_in_t = jnp.concatenate(gw_t[:3] + [gw_fg] + gw_t[3:], axis=0).reshape(N_DEV, shard_w, d)
    scatter_3 = _exchange_start("scatter_in_start", [(dw_in_t, True)])
    pairs = [(s_, _a_rows(SEG, 0, ROW_TILE), wt, _wt_block(SEG, seg_rows[i], 512), "nn") for i, s_ in enumerate(segs)]
    pairs.append((dfg, _a_rows(FG_PAD, 0, ROW_TILE), wt, _wt_block(FG_PAD, o_fg, 512), "nn"))
    dhn1 = _mm_nt("d_proj_in", pairs, n, d, 512, BF16, tm=ROW_TILE, after=scatter_3["token"])
    dh0, dg1, dmeta = _rms_bwd("rms1_bwd", dhn1, h0, g1, dh1, batch, with_meta=True)
    grad_x = dh0.reshape(batch, t, d)[:, N_META:, :]

    small_g = dict(norm_mix_gain=dg1, b_forget=db_fg[:, :n_fg], b_glu=jnp.concatenate([dbg_a, dbg_g], axis=1),
                   conv_dw_b=db_dw, conv_ln_gain=dg_ln, conv_ln_bias=db_ln, b_conv_out=dbco, norm_mlp_gain=dg2,
                   final_norm_gain=dg3)
    scatter_4 = _exchange_start("scatter_small_start", [
        (_shards_from_cols(dmeta), True), (_shards_from_cols(dw_dw), True), (_cat_small(small_g), False),
        (loss_blk[0:1, :], False)])

    grads, deltas, new_m, new_v = {}, {}, {}, {}

    def update(k, parts):
        shp = weights[k].shape
        if k == "w_in":
            res_ = _adamw("adamw_" + k, parts, w_in_t, m_in_t, v_in_t)
            res_ = [jnp.transpose(r) for r in res_]
        else:
            w2 = lambda arr: arr.reshape(parts.shape[1:])
            res_ = _adamw("adamw_" + k, parts, w2(weights[k]), w2(mom_m[k]), w2(mom_v[k]))
        grads[k], deltas[k], new_m[k], new_v[k] = [r.reshape(shp) for r in res_]

    for k, parts in zip(("w_mlp_down", "w_mlp_up"), _exchange_wait("scatter_mlp_wait", scatter_1, scatter_4["token"])):
        update(k, parts)
    for k, parts in zip(("w_out", "w_attn_out", "w_conv_out"),
                        _exchange_wait("scatter_mix_wait", scatter_2, deltas["w_mlp_up"])):
        update(k, parts)
    update("w_in", _exchange_wait("scatter_in_wait", scatter_3, deltas["w_conv_out"])[0])
    reduced = _exchange_wait("scatter_small_wait", scatter_4, deltas["w_in"])
    loss = jnp.sum(reduced.pop()[:, 0, 0])
    for k, parts in zip(("meta_tokens", "conv_dw_w"), reduced[:-1]):
        update(k, parts)
    res = _adamw("adamw_small", reduced[-1], _cat_small(weights), _cat_small(mom_m), _cat_small(mom_v))
    shapes = {k: weights[k].shape for k in SMALL}
    for dst, r in zip((grads, deltas, new_m, new_v), res):
        dst.update(_split_small(r, shapes))

    return (loss, grad_x, *[grads[k] for k in names], *[deltas[k] for k in names],
            *[new_m[k] for k in names], *[new_v[k] for k in names])
```

```python
import functools

import jax
import jax.numpy as jnp
from jax import lax
from jax.experimental import pallas as pl
from jax.experimental.pallas import tpu as pltpu

F32, BF16 = jnp.float32, jnp.bfloat16
N_DEV = 8
N_META = 16
HEAD_DIM = 64
LANES = 128
CONV_W = 31
RMS_EPS = 1e-6
LN_EPS = 1e-5
ROW_TILE = 688
MM_TM = 2 * ROW_TILE
SEG = 1024
FG_PAD = 128
VMEM_LIMIT = 56 * 1024 * 1024
ADAM_LR, ADAM_B1, ADAM_B2, ADAM_EPS, ADAM_WD, ADAM_STEP = 0.001, 0.9, 0.999, 1e-08, 0.01, 10
NEG = -1e30
LOG2E = 1.4426950408889634

SMALL = ("norm_mix_gain", "b_forget", "b_glu", "conv_dw_b", "conv_ln_gain", "conv_ln_bias", "b_conv_out",
         "norm_mlp_gain", "final_norm_gain")
SMALL_W = {"norm_mix_gain": 1024, "b_forget": 128, "b_glu": 2048, "conv_dw_b": 1024, "conv_ln_gain": 1024,
           "conv_ln_bias": 1024, "b_conv_out": 1024, "norm_mlp_gain": 1024, "final_norm_gain": 1024}
SMALL_N = {"norm_mix_gain": 1024, "b_forget": 16, "b_glu": 2048, "conv_dw_b": 1024, "conv_ln_gain": 1024,
           "conv_ln_bias": 1024, "b_conv_out": 1024, "norm_mlp_gain": 1024, "final_norm_gain": 1024}


def _params(sem=None):
    return pltpu.CompilerParams(dimension_semantics=sem, vmem_limit_bytes=VMEM_LIMIT)


def _sigmoid(x):
    return 1.0 / (1.0 + jnp.exp(-x))


def _dot_nt(a, b):
    return lax.dot_general(a, b, (((1,), (1,)), ((), ())), preferred_element_type=F32)


def _dot_tn(a, b):
    return lax.dot_general(a, b, (((0,), (0,)), ((), ())), preferred_element_type=F32)


HBM_SPEC = pl.BlockSpec(memory_space=pltpu.HBM)
SEM_SPEC = pl.BlockSpec(memory_space=pltpu.SEMAPHORE)
DATAFLOW = pltpu.SideEffectType.DATAFLOW_SIDE_EFFECTING


def _device_index():
    return 4 * lax.axis_index("x") + 2 * lax.axis_index("y") + lax.axis_index("c")


def _peers():
    x, y, c = lax.axis_index("x"), lax.axis_index("y"), lax.axis_index("c")
    out = []
    for k in range(1, N_DEV):
        px = 1 - x if k & 4 else x
        py = 1 - y if k & 2 else y
        pc = 1 - c if k & 1 else c
        out.append((k, (px, py, pc), 4 * px + 2 * py + pc))
    return out


def _peer_copy(per_dest, src_ref, land_ref, send_sems, recv_sems, a, k, dev, peer):
    src = src_ref.at[peer] if per_dest else src_ref
    return pltpu.make_async_remote_copy(
        src_ref=src, dst_ref=land_ref.at[_device_index()], send_sem=send_sems.at[a * (N_DEV - 1) + k - 1],
        recv_sem=recv_sems.at[a * (N_DEV - 1) + k - 1], device_id=dev, device_id_type=pl.DeviceIdType.MESH)


ALL_PEERS = tuple(range(1, N_DEV))
CHIP_PEERS = (1, 2, 4, 6)
FAR_PEERS = (2, 4, 6)


def _own_copy(per_dest, src_ref, land_ref, send_sems, n, a):
    me = _device_index()
    return pltpu.make_async_copy(src_ref.at[me] if per_dest else src_ref, land_ref.at[me],
                                 send_sems.at[n * (N_DEV - 1) + a])


def _exchange_start(name, items, ks=ALL_PEERS):
    n = len(items)
    per_dest = [it[1] for it in items]

    def body(*refs):
        srcs, lands = refs[:n], refs[n:2 * n]
        send_sems, recv_sems, token = refs[2 * n], refs[2 * n + 1], refs[-1]
        for a in range(n):
            _own_copy(per_dest[a], srcs[a], lands[a], send_sems, n, a).start()
            for k, dev, peer in _peers():
                if k in ks:
                    _peer_copy(per_dest[a], srcs[a], lands[a], send_sems, recv_sems, a, k, dev, peer).start()
        token[...] = jnp.zeros(token.shape, F32)

    srcs = [pltpu.with_memory_space_constraint(it[0], pltpu.HBM) for it in items]
    lands = []
    for arr, pd in items:
        shp = arr.shape if pd else (N_DEV,) + arr.shape
        lands.append(pltpu.with_memory_space_constraint(lax.empty(shp, arr.dtype), pltpu.HBM))
    sems = pltpu.SemaphoreType.DMA((n * N_DEV,))
    res = pl.pallas_call(
        body, name=name,
        out_shape=(sems, sems, *[pltpu.HBM(a_.shape, a_.dtype) for a_ in srcs + lands],
                   jax.ShapeDtypeStruct((8, 128), F32)),
        in_specs=[HBM_SPEC] * (2 * n),
        out_specs=(SEM_SPEC, SEM_SPEC, *[HBM_SPEC] * (2 * n), pl.BlockSpec(memory_space=pltpu.VMEM)),
        input_output_aliases={i: 2 + i for i in range(2 * n)},
        compiler_params=pltpu.CompilerParams(has_side_effects=DATAFLOW),
    )(*srcs, *lands)
    return dict(per_dest=per_dest, ks=ks, send=res[0], recv=res[1], srcs=list(res[2:2 + n]),
                lands=list(res[2 + n:2 + 2 * n]), token=res[-1])


def _exchange_wait(name, started, after):
    per_dest = started["per_dest"]
    n = len(per_dest)

    def body(*refs):
        srcs, lands = refs[:n], refs[n:2 * n]
        send_sems, recv_sems = refs[2 * n], refs[2 * n + 1]
        for a in range(n):
            _own_copy(per_dest[a], srcs[a], lands[a], send_sems, n, a).wait()
            for k, dev, peer in _peers():
                if k in started["ks"]:
                    cp = _peer_copy(per_dest[a], srcs[a], lands[a], send_sems, recv_sems, a, k, dev, peer)
                    cp.wait_send()
                    cp.wait_recv()

    bufs = started["srcs"] + started["lands"]
    after = list(after) if isinstance(after, (list, tuple)) else [after]
    res = pl.pallas_call(
        body, name=name, out_shape=tuple(pltpu.HBM(b_.shape, b_.dtype) for b_ in bufs),
        in_specs=[HBM_SPEC] * (2 * n) + [SEM_SPEC, SEM_SPEC] + [pl.BlockSpec(memory_space=pl.ANY)] * len(after),
        out_specs=tuple([HBM_SPEC] * (2 * n)), input_output_aliases={i: i for i in range(2 * n)},
        compiler_params=pltpu.CompilerParams(has_side_effects=DATAFLOW),
    )(*bufs, started["send"], started["recv"], *after)
    return list(res[n:])


def _pass_on_copy(land_ref, send_sems, recv_sems, a, idx, slot):
    sibling = (lax.axis_index("x"), lax.axis_index("y"), 1 - lax.axis_index("c"))
    return pltpu.make_async_remote_copy(
        src_ref=land_ref.at[slot], dst_ref=land_ref.at[slot], send_sem=send_sems.at[a * len(FAR_PEERS) + idx],
        recv_sem=recv_sems.at[a * len(FAR_PEERS) + idx], device_id=sibling, device_id_type=pl.DeviceIdType.MESH)


def _pass_on_start(name, lands):
    n = len(lands)

    def body(*refs):
        send_sems, recv_sems, token = refs[n], refs[n + 1], refs[-1]
        slots = {k: peer for k, _, peer in _peers()}
        for a in range(n):
            for idx, k in enumerate(FAR_PEERS):
                _pass_on_copy(refs[a], send_sems, recv_sems, a, idx, slots[k]).start()
        token[...] = jnp.zeros(token.shape, F32)

    lands = [pltpu.with_memory_space_constraint(l_, pltpu.HBM) for l_ in lands]
    sems = pltpu.SemaphoreType.DMA((n * len(FAR_PEERS),))
    res = pl.pallas_call(
        body, name=name,
        out_shape=(sems, sems, *[pltpu.HBM(l_.shape, l_.dtype) for l_ in lands], jax.ShapeDtypeStruct((8, 128), F32)),
        in_specs=[HBM_SPEC] * n, out_specs=(SEM_SPEC, SEM_SPEC, *[HBM_SPEC] * n, pl.BlockSpec(memory_space=pltpu.VMEM)),
        input_output_aliases={i: 2 + i for i in range(n)},
        compiler_params=pltpu.CompilerParams(has_side_effects=DATAFLOW),
    )(*lands)
    return dict(send=res[0], recv=res[1], lands=list(res[2:2 + n]), token=res[-1])


def _pass_on_wait(name, passed, after):
    n = len(passed["lands"])

    def body(*refs):
        send_sems, recv_sems = refs[n], refs[n + 1]
        slots = {k: peer for k, _, peer in _peers()}
        for a in range(n):
            for idx, k in enumerate(FAR_PEERS):
                _pass_on_copy(refs[a], send_sems, recv_sems, a, idx, slots[k]).wait_send()
                _pass_on_copy(refs[a], send_sems, recv_sems, a, idx, slots[k ^ 1]).wait_recv()

    res = pl.pallas_call(
        body, name=name, out_shape=tuple(pltpu.HBM(l_.shape, l_.dtype) for l_ in passed["lands"]),
        in_specs=[HBM_SPEC] * n + [SEM_SPEC, SEM_SPEC, pl.BlockSpec(memory_space=pl.ANY)],
        out_specs=tuple([HBM_SPEC] * n), input_output_aliases={i: i for i in range(n)},
        compiler_params=pltpu.CompilerParams(has_side_effects=DATAFLOW),
    )(*passed["lands"], passed["send"], passed["recv"], after)
    return list(res)


def _mm_nn(name, x, w, w_spec, n_out, tn, out_dtype, relu2=False, tm=MM_TM):
    m, k = x.shape

    def body(x_ref, w_ref, *outs):
        acc = jnp.dot(x_ref[...], w_ref[...], preferred_element_type=F32)
        if relu2:
            acc = jnp.maximum(acc, 0.0)
            acc = acc * acc
        outs[0][...] = acc.astype(outs[0].dtype)

    o_spec = pl.BlockSpec((tm, tn), lambda i, j: (i, j))
    return pl.pallas_call(
        body, name=name, out_shape=jax.ShapeDtypeStruct((m, n_out), out_dtype), grid=(m // tm, n_out // tn),
        in_specs=[pl.BlockSpec((tm, k), lambda i, j: (i, 0)), w_spec],
        out_specs=o_spec, compiler_params=_params(("parallel", "parallel")),
    )(x, w)


def _mm_nt(name, pairs, m, n_out, tn, out_dtype, relu_bwd_of=None, tm=MM_TM, after=None):
    np_ = len(pairs)

    def body(*refs):
        acc = None
        for p in range(np_):
            if len(pairs[p]) == 5:
                d = jnp.dot(refs[2 * p][...], refs[2 * p + 1][...], preferred_element_type=F32)
            else:
                d = _dot_nt(refs[2 * p][...], refs[2 * p + 1][...])
            acc = d if acc is None else acc + d
        if relu_bwd_of is not None:
            acc = acc * (2.0 * jnp.sqrt(refs[2 * np_][...].astype(F32)))
        refs[-1][...] = acc.astype(refs[-1].dtype)

    o_spec = pl.BlockSpec((tm, tn), lambda i, j: (i, j))
    operands, specs = [], []
    for pair in pairs:
        operands += [pair[0], pair[2]]
        specs += [pair[1], pair[3]]
    if relu_bwd_of is not None:
        operands.append(relu_bwd_of)
        specs.append(o_spec)
    if after is not None:
        operands.append(after)
        specs.append(pl.BlockSpec((8, 128), lambda i, j: (0, 0)))
    return pl.pallas_call(
        body, name=name, out_shape=jax.ShapeDtypeStruct((m, n_out), out_dtype), grid=(m // tm, n_out // tn),
        in_specs=specs, out_specs=o_spec, compiler_params=_params(("parallel", "parallel")),
    )(*operands)


def _mm_tn(name, x, x_col, ta, dy, dy_col, tb, out_shape, out_spec, grid_ab):
    m = x.shape[0]

    def body(x_ref, dy_ref, o_ref):
        o_ref[...] = _dot_tn(x_ref[...], dy_ref[...]).astype(BF16)

    return pl.pallas_call(
        body, name=name, out_shape=jax.ShapeDtypeStruct(out_shape, BF16), grid=grid_ab,
        in_specs=[pl.BlockSpec((m, ta), lambda a, b: (0, x_col(a))),
                  pl.BlockSpec((m, tb), lambda a, b: (0, dy_col(b)))],
        out_specs=out_spec, compiler_params=_params(("parallel", "parallel")),
    )(x, dy)


def _w_cols(k, tn, off_blocks):
    return pl.BlockSpec((k, tn), lambda i, j: (0, off_blocks + j))


def _a_rows(kw, col_block=0, tm=MM_TM):
    return pl.BlockSpec((tm, kw), lambda i, j: (i, col_block))


def _w_rows(tn, kw, col_block=0):
    return pl.BlockSpec((tn, kw), lambda i, j: (j, col_block))


def _wt_rows(tn, off):
    return pl.BlockSpec((pl.Element(tn), pl.Element(SEG)), lambda i, j: (pl.multiple_of(off + tn * j, 16), 0))


def _wt_block(k, off, tn):
    return pl.BlockSpec((pl.Element(k), pl.Element(tn)), lambda i, j: (off, pl.multiple_of(tn * j, 128)))


def _grad_w(name, x, dy):
    na, nb = x.shape[1], dy.shape[1]
    ta, tb = min(na, 1024), min(nb, 512)
    return _mm_tn(name, x, lambda a: a, ta, dy, lambda b: b, tb, (na, nb),
                  pl.BlockSpec((ta, tb), lambda a, b: (a, b)), (na // ta, nb // tb))


def _row_spec(width):
    return pl.BlockSpec((ROW_TILE, width), lambda i: (i, 0))


def _vec_spec(width):
    return pl.BlockSpec((1, width), lambda i: (0, 0))


def _rms_fwd(name, h, g, res=None):
    n, d = h.shape
    tile = ROW_TILE if n % ROW_TILE == 0 else n
    row_spec = pl.BlockSpec((tile, d), lambda i: (i, 0))

    def body(*refs):
        if res is None:
            h_ref, g_ref, hn_ref = refs
            hv = h_ref[...]
        else:
            h_ref, r_ref, g_ref, hs_ref, hn_ref = refs
            hv = h_ref[...] + r_ref[...]
            hs_ref[...] = hv
        r = lax.rsqrt(jnp.mean(hv * hv, axis=-1, keepdims=True) + RMS_EPS)
        hn_ref[...] = (hv * r * g_ref[...]).astype(BF16)

    ins = [h, g] if res is None else [h, res, g]
    in_specs = [row_spec, _vec_spec(d)] if res is None else [row_spec, row_spec, _vec_spec(d)]
    hn_shape = jax.ShapeDtypeStruct((n, d), BF16)
    if res is None:
        out_shape, out_specs = hn_shape, row_spec
    else:
        out_shape, out_specs = [jax.ShapeDtypeStruct((n, d), F32), hn_shape], [row_spec, row_spec]
    return pl.pallas_call(body, name=name, out_shape=out_shape, grid=(n // tile,), in_specs=in_specs,
                          out_specs=out_specs, compiler_params=_params(("parallel",)))(*ins)


def _rms_bwd(name, dhn, h, g, dres, batch, with_bf16=False, with_meta=False):
    n, d = h.shape
    t = n // batch
    nt = t // ROW_TILE

    def body(dhn_ref, h_ref, g_ref, dres_ref, *outs):
        first = (pl.program_id(0) == 0) & (pl.program_id(1) == 0)
        hv = h_ref[...]
        r = lax.rsqrt(jnp.mean(hv * hv, axis=-1, keepdims=True) + RMS_EPS)
        nrm = hv * r
        dhn = dhn_ref[...].astype(F32)
        dn = dhn * g_ref[...]
        dh = dres_ref[...] + r * (dn - nrm * jnp.mean(dn * nrm, axis=-1, keepdims=True))
        outs[0][...] = dh
        dg_ref = outs[1]

        @pl.when(first)
        def _():
            dg_ref[...] = jnp.zeros(dg_ref.shape, F32)

        dg_ref[...] += jnp.sum(dhn * nrm, axis=0, keepdims=True)
        nxt = 2
        if with_bf16:
            outs[nxt][...] = dh.astype(BF16)
            nxt += 1
        if with_meta:
            meta_ref = outs[nxt]

            @pl.when(first)
            def _():
                meta_ref[...] = jnp.zeros(meta_ref.shape, F32)

            @pl.when(pl.program_id(1) == 0)
            def _():
                meta_ref[...] += dh[0:N_META, :]

    row = pl.BlockSpec((ROW_TILE, d), lambda b, j: (b * nt + j, 0))
    vec = pl.BlockSpec((1, d), lambda b, j: (0, 0))
    shapes = [jax.ShapeDtypeStruct((n, d), F32), jax.ShapeDtypeStruct((1, d), F32)]
    specs = [row, vec]
    if with_bf16:
        shapes.append(jax.ShapeDtypeStruct((n, d), BF16))
        specs.append(row)
    if with_meta:
        shapes.append(jax.ShapeDtypeStruct((N_META, d), F32))
        specs.append(pl.BlockSpec((N_META, d), lambda b, j: (0, 0)))
    return pl.pallas_call(body, name=name, out_shape=shapes, grid=(batch, nt), in_specs=[row, row, vec, row],
                          out_specs=specs, compiler_params=_params(("arbitrary", "arbitrary")))(dhn, h, g, dres)


def _final(name, h1, dn, tgt, g, batch):
    n, d = h1.shape
    t = n // batch
    nt = t // ROW_TILE

    def body(h1_ref, dn_ref, tgt_ref, g_ref, dh_ref, dhb_ref, loss_ref, dg_ref):
        first = (pl.program_id(0) == 0) & (pl.program_id(1) == 0)
        hv = h1_ref[...] + dn_ref[...]
        r = lax.rsqrt(jnp.mean(hv * hv, axis=-1, keepdims=True) + RMS_EPS)
        nrm = hv * r
        gv = g_ref[...]
        pos = pl.program_id(1) * ROW_TILE + lax.broadcasted_iota(jnp.int32, (ROW_TILE, 1), 0)
        diff = jnp.where(pos >= N_META, nrm * gv - tgt_ref[...], 0.0)
        dy = diff * (1.0 / d)

        @pl.when(first)
        def _():
            loss_ref[...] = jnp.zeros(loss_ref.shape, F32)
            dg_ref[...] = jnp.zeros(dg_ref.shape, F32)

        loss_ref[...] += jnp.full(loss_ref.shape, 0.5 / d, F32) * jnp.sum(diff * diff)
        dg_ref[...] += jnp.sum(dy * nrm, axis=0, keepdims=True)
        dng = dy * gv
        dh = r * (dng - nrm * jnp.mean(dng * nrm, axis=-1, keepdims=True))
        dh_ref[...] = dh
        dhb_ref[...] = dh.astype(BF16)

    row = pl.BlockSpec((ROW_TILE, d), lambda b, j: (b * nt + j, 0))
    vec = pl.BlockSpec((1, d), lambda b, j: (0, 0))
    return pl.pallas_call(
        body, name=name, grid=(batch, nt), in_specs=[row, row, row, vec],
        out_shape=[jax.ShapeDtypeStruct((n, d), F32), jax.ShapeDtypeStruct((n, d), BF16),
                   jax.ShapeDtypeStruct((8, 128), F32), jax.ShapeDtypeStruct((1, d), F32)],
        out_specs=[row, row, pl.BlockSpec((8, 128), lambda b, j: (0, 0)), vec],
        compiler_params=_params(("arbitrary", "arbitrary")))(h1, dn, tgt, g)


def _ln_silu_fwd(name, c1, g, b):
    n, d = c1.shape

    def body(c_ref, g_ref, b_ref, o_ref):
        xv = c_ref[...]
        xc = xv - jnp.mean(xv, axis=-1, keepdims=True)
        rstd = lax.rsqrt(jnp.mean(xc * xc, axis=-1, keepdims=True) + LN_EPS)
        c2 = xc * rstd * g_ref[...] + b_ref[...]
        o_ref[...] = (c2 * _sigmoid(c2)).astype(BF16)

    return pl.pallas_call(body, name=name, out_shape=jax.ShapeDtypeStruct((n, d), BF16), grid=(n // ROW_TILE,),
                          in_specs=[_row_spec(d), _vec_spec(d), _vec_spec(d)], out_specs=_row_spec(d),
                          compiler_params=_params(("parallel",)))(c1, g, b)


def _ln_silu_bwd(name, dc3, c1, g, b):
    n, d = c1.shape

    def body(d_ref, c_ref, g_ref, b_ref, dc1_ref, dg_ref, db_ref):
        xv = c_ref[...]
        xc = xv - jnp.mean(xv, axis=-1, keepdims=True)
        rstd = lax.rsqrt(jnp.mean(xc * xc, axis=-1, keepdims=True) + LN_EPS)
        xh = xc * rstd
        c2 = xh * g_ref[...] + b_ref[...]
        s = _sigmoid(c2)
        dc2 = d_ref[...].astype(F32) * (s * (1.0 + c2 * (1.0 - s)))

        @pl.when(pl.program_id(0) == 0)
        def _():
            dg_ref[...] = jnp.zeros(dg_ref.shape, F32)
            db_ref[...] = jnp.zeros(db_ref.shape, F32)

        dg_ref[...] += jnp.sum(dc2 * xh, axis=0, keepdims=True)
        db_ref[...] += jnp.sum(dc2, axis=0, keepdims=True)
        dxh = dc2 * g_ref[...]
        dc1_ref[...] = rstd * (dxh - jnp.mean(dxh, axis=-1, keepdims=True)
                               - xh * jnp.mean(dxh * xh, axis=-1, keepdims=True))

    return pl.pallas_call(
        body, name=name, grid=(n // ROW_TILE,),
        out_shape=[jax.ShapeDtypeStruct((n, d), F32), jax.ShapeDtypeStruct((1, d), F32),
                   jax.ShapeDtypeStruct((1, d), F32)],
        in_specs=[_row_spec(d), _row_spec(d), _vec_spec(d), _vec_spec(d)],
        out_specs=[_row_spec(d), _vec_spec(d), _vec_spec(d)],
        compiler_params=_params(("arbitrary",)))(dc3, c1, g, b)


MERGE_TC = 512


def _merge_fwd(name, gates, a, c, b_co):
    n, d = a.shape
    nc = d // MERGE_TC

    def body(ga_ref, gc_ref, a_ref, c_ref, b_ref, m_ref):
        f32 = lambda r_: r_[...].astype(F32)
        m = _sigmoid(f32(ga_ref)) * f32(a_ref) + _sigmoid(f32(gc_ref)) * (f32(c_ref) + b_ref[...])
        m_ref[...] = m.astype(BF16)

    blk = lambda off: pl.BlockSpec((ROW_TILE, MERGE_TC), lambda i, j: (i, off + j))
    return pl.pallas_call(
        body, name=name, out_shape=jax.ShapeDtypeStruct((n, d), BF16), grid=(n // ROW_TILE, nc),
        in_specs=[blk(0), blk(nc), blk(0), blk(0), pl.BlockSpec((1, MERGE_TC), lambda i, j: (0, j))],
        out_specs=blk(0), compiler_params=_params(("parallel", "parallel")))(gates, gates, a, c, b_co)


def _merge_bwd(name, dm, gates, a, c, b_co):
    n, d = a.shape
    nc = d // MERGE_TC

    def body(dm_ref, ga_ref, gc_ref, a_ref, c_ref, b_ref, da_ref, dc_ref, dga_ref, dgc_ref, dbco_ref):
        f32 = lambda r_: r_[...].astype(F32)
        dmv = f32(dm_ref)
        sa, sc = _sigmoid(f32(ga_ref)), _sigmoid(f32(gc_ref))
        dc = dmv * sc
        da_ref[...] = (dmv * sa).astype(BF16)
        dc_ref[...] = dc.astype(BF16)
        dga_ref[...] = (dmv * f32(a_ref) * sa * (1.0 - sa)).astype(BF16)
        dgc_ref[...] = (dmv * (f32(c_ref) + b_ref[...]) * sc * (1.0 - sc)).astype(BF16)

        @pl.when(pl.program_id(1) == 0)
        def _():
            dbco_ref[...] = jnp.zeros(dbco_ref.shape, F32)

        dbco_ref[...] += jnp.sum(dc, axis=0, keepdims=True)

    blk = lambda off: pl.BlockSpec((ROW_TILE, MERGE_TC), lambda j, i: (i, off + j))
    vec = pl.BlockSpec((1, MERGE_TC), lambda j, i: (0, j))
    act = jax.ShapeDtypeStruct((n, d), BF16)
    return pl.pallas_call(
        body, name=name, grid=(nc, n // ROW_TILE),
        out_shape=[act, act, act, act, jax.ShapeDtypeStruct((1, d), F32)],
        in_specs=[blk(0), blk(0), blk(nc), blk(0), blk(0), vec],
        out_specs=[blk(0), blk(0), blk(0), blk(0), vec],
        compiler_params=_params(("parallel", "arbitrary")))(dm, gates, gates, a, c, b_co)


CONV_TC = 128
CONV_HALO = 32


def _conv_chunk(t):
    return 48 if t % 48 == 0 else 32 if t % 32 == 0 else 16


def _fold8(x):
    out = x[0:8]
    for k in range(1, x.shape[0] // 8):
        out = out + x[8 * k:8 * k + 8]
    return out


def _glu_conv_fwd(name, glu, b_glu, w_dw, b_dw, batch):
    n, c2 = glu.shape
    c = c2 // 2
    t = n // batch
    nc = c // CONV_TC

    def body(a_ref, gt_ref, ba_ref, bg_ref, w_ref, bdw_ref, o_ref, pad_ref):
        u = (a_ref[...].astype(F32) + ba_ref[...]) * _sigmoid(gt_ref[...].astype(F32) + bg_ref[...])
        pad_ref[0:CONV_HALO, :] = jnp.zeros((CONV_HALO, CONV_TC), F32)
        pad_ref[CONV_HALO:CONV_HALO + t, :] = u
        ch = _conv_chunk(t)
        for r0 in range(0, t, ch):
            acc = jnp.zeros((ch, CONV_TC), F32) + bdw_ref[...]
            for j in range(CONV_W):
                off = r0 + CONV_HALO - (CONV_W - 1) + j
                acc = acc + w_ref[j:j + 1, :] * pad_ref[off:off + ch, :]
            o_ref[r0:r0 + ch, :] = acc

    seq = lambda off: pl.BlockSpec((t, CONV_TC), lambda b, j: (b, off + j))
    vec = lambda off: pl.BlockSpec((1, CONV_TC), lambda b, j: (0, off + j))
    return pl.pallas_call(
        body, name=name, out_shape=jax.ShapeDtypeStruct((n, c), F32), grid=(batch, nc),
        in_specs=[seq(0), seq(nc), vec(0), vec(nc), pl.BlockSpec((CONV_W, CONV_TC), lambda b, j: (0, j)), vec(0)],
        out_specs=seq(0), scratch_shapes=[pltpu.VMEM((t + CONV_HALO, CONV_TC), F32)],
        compiler_params=_params(("parallel", "parallel")))(glu, glu, b_glu, b_glu, w_dw, b_dw)


def _glu_conv_bwd(name, dc1, glu, b_glu, w_dw, batch):
    n, c2 = glu.shape
    c = c2 // 2
    t = n // batch
    nc = c // CONV_TC

    def body(d_ref, a_ref, gt_ref, ba_ref, bg_ref, w_ref, dga_ref, dgg_ref, dw_ref, dbdw_ref, dba_ref, dbg_ref,
             padu_ref, padd_ref):
        av = a_ref[...].astype(F32) + ba_ref[...]
        sg = _sigmoid(gt_ref[...].astype(F32) + bg_ref[...])
        dc = d_ref[...]
        padu_ref[0:CONV_HALO, :] = jnp.zeros((CONV_HALO, CONV_TC), F32)
        padu_ref[CONV_HALO:CONV_HALO + t, :] = av * sg
        padd_ref[0:t, :] = dc
        padd_ref[t:t + CONV_HALO, :] = jnp.zeros((CONV_HALO, CONV_TC), F32)

        @pl.when(pl.program_id(1) == 0)
        def _():
            dw_ref[...] = jnp.zeros(dw_ref.shape, F32)
            dbdw_ref[...] = jnp.zeros(dbdw_ref.shape, F32)
            dba_ref[...] = jnp.zeros(dba_ref.shape, F32)
            dbg_ref[...] = jnp.zeros(dbg_ref.shape, F32)

        ch = _conv_chunk(t)
        zero8 = jnp.zeros((8, CONV_TC), F32)
        dw_acc = [zero8] * CONV_W
        sum_dc, sum_a, sum_g = zero8, zero8, zero8
        for r0 in range(0, t, ch):
            dcc = d_ref[r0:r0 + ch, :]
            du = jnp.zeros((ch, CONV_TC), F32)
            for j in range(CONV_W):
                back = r0 + CONV_W - 1 - j
                du = du + w_ref[j:j + 1, :] * padd_ref[back:back + ch, :]
                off = r0 + CONV_HALO - (CONV_W - 1) + j
                dw_acc[j] = dw_acc[j] + _fold8(dcc * padu_ref[off:off + ch, :])
            sgc = _sigmoid(gt_ref[r0:r0 + ch, :].astype(F32) + bg_ref[...])
            dga = du * sgc
            dgg = du * padu_ref[CONV_HALO + r0:CONV_HALO + r0 + ch, :] * (1.0 - sgc)
            dga_ref[r0:r0 + ch, :] = dga.astype(BF16)
            dgg_ref[r0:r0 + ch, :] = dgg.astype(BF16)
            sum_dc, sum_a, sum_g = sum_dc + _fold8(dcc), sum_a + _fold8(dga), sum_g + _fold8(dgg)
        for j in range(CONV_W):
            dw_ref[j:j + 1, :] += jnp.sum(dw_acc[j], axis=0, keepdims=True)
        dbdw_ref[...] += jnp.sum(sum_dc, axis=0, keepdims=True)
        dba_ref[...] += jnp.sum(sum_a, axis=0, keepdims=True)
        dbg_ref[...] += jnp.sum(sum_g, axis=0, keepdims=True)

    seq = lambda off: pl.BlockSpec((t, CONV_TC), lambda j, b: (b, off + j))
    vec = lambda off: pl.BlockSpec((1, CONV_TC), lambda j, b: (0, off + j))
    wsp = pl.BlockSpec((CONV_W, CONV_TC), lambda j, b: (0, j))
    act = jax.ShapeDtypeStruct((n, c), BF16)
    v = jax.ShapeDtypeStruct((1, c), F32)
    return pl.pallas_call(
        body, name=name, grid=(nc, batch),
        out_shape=[act, act, jax.ShapeDtypeStruct((CONV_W, c), F32), v, v, v],
        in_specs=[seq(0), seq(0), seq(nc), vec(0), vec(nc), wsp],
        out_specs=[seq(0), seq(0), wsp, vec(0), vec(0), vec(0)],
        scratch_shapes=[pltpu.VMEM((t + CONV_HALO, CONV_TC), F32), pltpu.VMEM((t + CONV_HALO, CONV_TC), F32)],
        compiler_params=_params(("parallel", "arbitrary")))(dc1, glu, glu, b_glu, b_glu, w_dw)


def _split3(x):
    hi = x.astype(BF16)
    r = x - hi.astype(F32)
    mid = r.astype(BF16)
    lo = (r - mid.astype(F32)).astype(BF16)
    return hi, mid, lo


def _tri_matmul(tri, x):
    hi, mid, lo = _split3(x)
    dot = lambda v: jnp.dot(tri, v, preferred_element_type=F32)
    return dot(hi) + dot(mid) + dot(lo)


def _fox_prep_fwd(name, fg, b_fg, batch):
    n, w = fg.shape
    t = n // batch
    nq = t // ROW_TILE

    def body(fg_ref, b_ref, cum_ref):
        row = lax.broadcasted_iota(jnp.int32, (ROW_TILE, ROW_TILE), 0)
        col = lax.broadcasted_iota(jnp.int32, (ROW_TILE, ROW_TILE), 1)
        tri = (row >= col).astype(BF16)
        for k in range(nq):
            rows = slice(k * ROW_TILE, (k + 1) * ROW_TILE)
            z = fg_ref[rows, :] + b_ref[...]
            logf = jnp.minimum(z, 0.0) - jnp.log(1.0 + jnp.exp(-jnp.abs(z)))
            cum = _tri_matmul(tri, logf)
            if k > 0:
                cum = cum + cum_ref[k * ROW_TILE - 1:k * ROW_TILE, :]
            cum_ref[rows, :] = cum

    seq = pl.BlockSpec((t, w), lambda b: (b, 0))
    return pl.pallas_call(body, name=name, out_shape=jax.ShapeDtypeStruct((n, w), F32), grid=(batch,),
                          in_specs=[seq, pl.BlockSpec((1, w), lambda b: (0, 0))], out_specs=seq,
                          compiler_params=_params(("parallel",)))(fg, b_fg)


def _fox_prep_bwd(name, dcum_k, dcum_q, fg, b_fg, batch):
    n, w = fg.shape
    t = n // batch
    nq = t // ROW_TILE

    def body(dk_ref, dq_ref, fg_ref, b_ref, dfg_ref, db_ref, rev_ref):
        row = lax.broadcasted_iota(jnp.int32, (ROW_TILE, ROW_TILE), 0)
        col = lax.broadcasted_iota(jnp.int32, (ROW_TILE, ROW_TILE), 1)
        tri = (col >= row).astype(BF16)

        @pl.when(pl.program_id(0) == 0)
        def _():
            db_ref[...] = jnp.zeros(db_ref.shape, F32)

        for k in reversed(range(nq)):
            rows = slice(k * ROW_TILE, (k + 1) * ROW_TILE)
            dlog = _tri_matmul(tri, dk_ref[rows, :] + dq_ref[rows, :])
            if k < nq - 1:
                dlog = dlog + rev_ref[(k + 1) * ROW_TILE:(k + 1) * ROW_TILE + 1, :]
            rev_ref[rows, :] = dlog
            dfg = dlog * _sigmoid(-(fg_ref[rows, :] + b_ref[...]))
            dfg_ref[rows, :] = dfg.astype(BF16)
            db_ref[...] += jnp.sum(dfg, axis=0, keepdims=True)

    seq = pl.BlockSpec((t, w), lambda b: (b, 0))
    vec = pl.BlockSpec((1, w), lambda b: (0, 0))
    return pl.pallas_call(
        body, name=name, grid=(batch,),
        out_shape=[jax.ShapeDtypeStruct((n, w), BF16), jax.ShapeDtypeStruct((1, w), F32)],
        in_specs=[seq, seq, seq, vec], out_specs=[seq, vec], scratch_shapes=[pltpu.VMEM((t, w), F32)],
        compiler_params=_params(("arbitrary",)))(dcum_k, dcum_q, fg, b_fg)


def _head_masks(x):
    lane = lax.broadcasted_iota(jnp.int32, x.shape, 1)
    zero = jnp.zeros(x.shape, x.dtype)
    return jnp.where(lane < HEAD_DIM, x, zero), jnp.where(lane >= HEAD_DIM, x, zero)


ATTN_BLOCK = 512


def _attn_blocks(t):
    nb = max(t // ATTN_BLOCK, 1)
    blocks = [(i * ATTN_BLOCK, ATTN_BLOCK) for i in range(nb - 1)]
    return blocks + [((nb - 1) * ATTN_BLOCK, t - (nb - 1) * ATTN_BLOCK)]


def _attn_specs(t):
    blocks = _attn_blocks(t)
    width = max(sz for _, sz in blocks)
    qkv = lambda off: pl.BlockSpec((t, LANES), lambda b, h: (b, off + h))
    cumr = pl.BlockSpec((None, None, len(blocks), 8, width), lambda b, h: (b, h, 0, 0, 0))
    return qkv, cumr


def _key_sums_to_blocks(cum, batch, t, n_pairs):
    blocks = _attn_blocks(t)
    width = max(sz for _, sz in blocks)
    cum_h = cum.reshape(batch, t, -1)[:, :, :2 * n_pairs].reshape(batch, t, n_pairs, 2)
    rows = [jnp.pad(jnp.transpose(cum_h[:, s0:s0 + sz], (0, 2, 3, 1)), ((0, 0), (0, 0), (0, 6), (0, width - sz)))
            for s0, sz in blocks]
    return jnp.stack(rows, axis=2)


def _key_sums_from_blocks(dcumr, batch, t, n_pairs):
    cols = [jnp.transpose(dcumr[:, :, j, :2, :sz], (0, 3, 1, 2)) for j, (_, sz) in enumerate(_attn_blocks(t))]
    return jnp.concatenate(cols, axis=1).reshape(batch * t, 2 * n_pairs)


def _causal(size):
    row = lax.broadcasted_iota(jnp.int32, (size, size), 0)
    col = lax.broadcasted_iota(jnp.int32, (size, size), 1)
    return row >= col


def _attn_fwd(name, qkv, cumr, batch):
    n, w3 = qkv.shape
    w = w3 // 3
    t = n // batch
    n_pairs = w // LANES
    blocks = _attn_blocks(t)

    def body(q_ref, k_ref, v_ref, cr_ref, o_ref, lse_ref):
        pair = pl.program_id(1)

        @pl.when(pair == 0)
        def _():
            lse_ref[...] = jnp.zeros(lse_ref.shape, F32)

        for i, (q0, qn) in enumerate(blocks):
            rows = slice(q0, q0 + qn)
            qs = _head_masks(q_ref[rows, :] * (0.125 * LOG2E))
            outs, lses = [], []
            for hh in range(2):
                m = jnp.full((qn, 1), NEG, F32)
                l = jnp.zeros((qn, 1), F32)
                acc = jnp.zeros((qn, LANES), F32)
                for j in range(i + 1):
                    k0, kn = blocks[j]
                    cols = slice(k0, k0 + kn)
                    s = _dot_nt(qs[hh], k_ref[cols, :]) - cr_ref[j, hh:hh + 1, 0:kn] * LOG2E
                    if j == i:
                        s = jnp.where(_causal(qn), s, NEG)
                    m_new = jnp.maximum(m, jnp.max(s, axis=1, keepdims=True))
                    alpha = jnp.exp2(m - m_new)
                    p = jnp.exp2(s - m_new)
                    l = alpha * l + jnp.sum(p, axis=1, keepdims=True)
                    acc = alpha * acc + jnp.dot(p.astype(BF16), v_ref[cols, :], preferred_element_type=F32)
                    m = m_new
                outs.append(acc / l)
                lses.append(m + jnp.log2(l))
            lane = lax.broadcasted_iota(jnp.int32, (qn, LANES), 1)
            o_ref[rows, :] = jnp.where(lane < HEAD_DIM, outs[0], outs[1]).astype(BF16)
            lse_ref[rows, :] = jnp.where(lane == 2 * pair, lses[0],
                                         jnp.where(lane == 2 * pair + 1, lses[1], lse_ref[rows, :]))

    qkv_spec, cumr_spec = _attn_specs(t)
    return pl.pallas_call(
        body, name=name, grid=(batch, n_pairs),
        out_shape=[jax.ShapeDtypeStruct((n, w), BF16), jax.ShapeDtypeStruct((n, LANES), F32)],
        in_specs=[qkv_spec(0), qkv_spec(n_pairs), qkv_spec(2 * n_pairs), cumr_spec],
        out_specs=[qkv_spec(0), pl.BlockSpec((t, LANES), lambda b, h: (b, 0))],
        compiler_params=_params(("parallel", "arbitrary")))(qkv, qkv, qkv, cumr)


def _attn_bwd(name, qkv, o, do, lse, cumr, batch):
    n, w3 = qkv.shape
    w = w3 // 3
    t = n // batch
    n_pairs = w // LANES
    blocks = _attn_blocks(t)

    def body(q_ref, k_ref, v_ref, o_ref, do_ref, lse_ref, cr_ref, dq_ref, dk_ref, dv_ref, dcr_ref, dcq_ref,
             dk_acc, dv_acc):
        pair = pl.program_id(1)
        dk_acc[...] = jnp.zeros(dk_acc.shape, F32)
        dv_acc[...] = jnp.zeros(dv_acc.shape, F32)
        dcr_ref[...] = jnp.zeros(dcr_ref.shape, F32)

        @pl.when(pair == 0)
        def _():
            dcq_ref[...] = jnp.zeros(dcq_ref.shape, F32)

        for i, (q0, qn) in enumerate(blocks):
            rows = slice(q0, q0 + qn)
            qs = _head_masks(q_ref[rows, :] * 0.125)
            q2 = _head_masks(q_ref[rows, :] * (0.125 * LOG2E))
            dos = _head_masks(do_ref[rows, :])
            dq = jnp.zeros((qn, LANES), F32)
            lane = lax.broadcasted_iota(jnp.int32, (qn, LANES), 1)
            dcq = []
            for hh in range(2):
                row_sum = jnp.zeros((qn, 1), F32)
                lse = jnp.sum(jnp.where(lane == 2 * pair + hh, lse_ref[rows, :], 0.0), axis=1, keepdims=True)
                delta = jnp.sum(dos[hh].astype(F32) * o_ref[rows, :].astype(F32), axis=1, keepdims=True)
                for j in range(i + 1):
                    k0, kn = blocks[j]
                    cols = slice(k0, k0 + kn)
                    s = _dot_nt(q2[hh], k_ref[cols, :]) - cr_ref[j, hh:hh + 1, 0:kn] * LOG2E
                    p = jnp.exp2(s - lse)
                    if j == i:
                        p = jnp.where(_causal(qn), p, 0.0)
                    dp = _dot_nt(dos[hh], v_ref[cols, :])
                    ds = p * (dp - delta)
                    pb, dsb = p.astype(BF16), ds.astype(BF16)
                    km = _head_masks(k_ref[cols, :])[hh]
                    dv_acc[j, :, 0:kn] += _dot_tn(dos[hh], pb)
                    dk_acc[j, :, 0:kn] += _dot_tn(qs[hh], dsb)
                    dq = dq + jnp.dot(dsb, km, preferred_element_type=F32)
                    dcr_ref[j, hh:hh + 1, 0:kn] -= jnp.sum(ds, axis=0, keepdims=True)
                    row_sum = row_sum + jnp.sum(ds, axis=1, keepdims=True)
                dcq.append(row_sum)
            dq_ref[rows, :] = (dq * 0.125).astype(BF16)
            dcq_ref[rows, :] = jnp.where(lane == 2 * pair, dcq[0],
                                         jnp.where(lane == 2 * pair + 1, dcq[1], dcq_ref[rows, :]))
        for j, (k0, kn) in enumerate(blocks):
            dk_ref[k0:k0 + kn, :] = dk_acc[j].T[0:kn, :].astype(BF16)
            dv_ref[k0:k0 + kn, :] = dv_acc[j].T[0:kn, :].astype(BF16)

    qkv_spec, cumr_spec = _attn_specs(t)
    wide = -(-max(sz for _, sz in blocks) // LANES) * LANES
    act = jax.ShapeDtypeStruct((n, w), BF16)
    return pl.pallas_call(
        body, name=name, grid=(batch, n_pairs),
        out_shape=[act, act, act, jax.ShapeDtypeStruct(cumr.shape, F32), jax.ShapeDtypeStruct((n, LANES), F32)],
        in_specs=[qkv_spec(0), qkv_spec(n_pairs), qkv_spec(2 * n_pairs), qkv_spec(0), qkv_spec(0),
                  pl.BlockSpec((t, LANES), lambda b, h: (b, 0)), cumr_spec],
        out_specs=[qkv_spec(0), qkv_spec(0), qkv_spec(0), cumr_spec, pl.BlockSpec((t, LANES), lambda b, h: (b, 0))],
        scratch_shapes=[pltpu.VMEM((len(blocks), LANES, wide), F32), pltpu.VMEM((len(blocks), LANES, wide), F32)],
        compiler_params=_params(("parallel", "arbitrary")))(qkv, qkv, qkv, o, do, lse, cumr)


def _adamw(name, parts, w, m, v):
    r, c = w.shape
    tr = 128 if r % 128 == 0 else r
    tc = 256 if tr > 128 and c % 256 == 0 else c
    c1 = 1.0 - ADAM_B1 ** ADAM_STEP
    c2 = 1.0 - ADAM_B2 ** ADAM_STEP

    def body(p_ref, w_ref, m_ref, v_ref, g_ref, d_ref, m2_ref, v2_ref):
        g = p_ref[0].astype(F32)
        for s in range(1, N_DEV):
            g = g + p_ref[s].astype(F32)
        m2 = ADAM_B1 * m_ref[...] + (1.0 - ADAM_B1) * g
        v2 = ADAM_B2 * v_ref[...] + (1.0 - ADAM_B2) * (g * g)
        g_ref[...] = g
        m2_ref[...] = m2
        v2_ref[...] = v2
        d_ref[...] = -ADAM_LR * ((m2 / c1) / (jnp.sqrt(v2 / c2) + ADAM_EPS) + ADAM_WD * w_ref[...])

    blk = pl.BlockSpec((tr, tc), lambda i, j: (i, j))
    shp = jax.ShapeDtypeStruct((r, c), F32)
    return pl.pallas_call(
        body, name=name, out_shape=[shp] * 4, grid=(r // tr, c // tc),
        in_specs=[pl.BlockSpec((N_DEV, tr, tc), lambda i, j: (0, i, j)), blk, blk, blk], out_specs=[blk] * 4,
        compiler_params=_params(("parallel", "parallel")))(parts, w, m, v)


def _cat_small(vals):
    parts = []
    for name in SMALL:
        v = vals[name].reshape(1, -1).astype(F32)
        parts.append(jnp.pad(v, ((0, 0), (0, SMALL_W[name] - v.shape[1]))))
    return jnp.concatenate(parts, axis=1)


def _split_small(row, shapes):
    out, off = {}, 0
    for name in SMALL:
        out[name] = row[0, off:off + SMALL_N[name]].reshape(shapes[name])
        off += SMALL_W[name]
    return out


def _cols_from_shards(g):
    return jnp.transpose(g, (1, 0, 2)).reshape(g.shape[1], N_DEV * g.shape[2])


def _shards_from_cols(a):
    r, c = a.shape
    return jnp.transpose(a.reshape(r, N_DEV, c // N_DEV), (1, 0, 2))


def kernel(x, meta_tokens, norm_mix_gain, w_in, b_forget, w_attn_out, b_glu, conv_dw_w, conv_dw_b, conv_ln_gain, conv_ln_bias, w_conv_out, b_conv_out, w_out, norm_mlp_gain, w_mlp_up, w_mlp_down, final_norm_gain, loss_target, m_meta_tokens, m_norm_mix_gain, m_w_in, m_b_forget, m_w_attn_out, m_b_glu, m_conv_dw_w, m_conv_dw_b, m_conv_ln_gain, m_conv_ln_bias, m_w_conv_out, m_b_conv_out, m_w_out, m_norm_mlp_gain, m_w_mlp_up, m_w_mlp_down, m_final_norm_gain, v_meta_tokens, v_norm_mix_gain, v_w_in, v_b_forget, v_w_attn_out, v_b_glu, v_conv_dw_w, v_conv_dw_b, v_conv_ln_gain, v_conv_ln_bias, v_w_conv_out, v_b_conv_out, v_w_out, v_norm_mlp_gain, v_w_mlp_up, v_w_mlp_down, v_final_norm_gain):
    weights = dict(meta_tokens=meta_tokens, norm_mix_gain=norm_mix_gain, w_in=w_in, b_forget=b_forget, w_attn_out=w_attn_out, b_glu=b_glu, conv_dw_w=conv_dw_w, conv_dw_b=conv_dw_b, conv_ln_gain=conv_ln_gain, conv_ln_bias=conv_ln_bias, w_conv_out=w_conv_out, b_conv_out=b_conv_out, w_out=w_out, norm_mlp_gain=norm_mlp_gain, w_mlp_up=w_mlp_up, w_mlp_down=w_mlp_down, final_norm_gain=final_norm_gain)
    mom_m = dict(meta_tokens=m_meta_tokens, norm_mix_gain=m_norm_mix_gain, w_in=m_w_in, b_forget=m_b_forget, w_attn_out=m_w_attn_out, b_glu=m_b_glu, conv_dw_w=m_conv_dw_w, conv_dw_b=m_conv_dw_b, conv_ln_gain=m_conv_ln_gain, conv_ln_bias=m_conv_ln_bias, w_conv_out=m_w_conv_out, b_conv_out=m_b_conv_out, w_out=m_w_out, norm_mlp_gain=m_norm_mlp_gain, w_mlp_up=m_w_mlp_up, w_mlp_down=m_w_mlp_down, final_norm_gain=m_final_norm_gain)
    mom_v = dict(meta_tokens=v_meta_tokens, norm_mix_gain=v_norm_mix_gain, w_in=v_w_in, b_forget=v_b_forget, w_attn_out=v_w_attn_out, b_glu=v_b_glu, conv_dw_w=v_conv_dw_w, conv_dw_b=v_conv_dw_b, conv_ln_gain=v_conv_ln_gain, conv_ln_bias=v_conv_ln_bias, w_conv_out=v_w_conv_out, b_conv_out=v_b_conv_out, w_out=v_w_out, norm_mlp_gain=v_norm_mlp_gain, w_mlp_up=v_w_mlp_up, w_mlp_down=v_w_mlp_down, final_norm_gain=v_final_norm_gain)
    names = list(weights)
    batch, seq, d = x.shape
    t = seq + N_META
    n = batch * t
    n_pairs = d // LANES
    assert t % ROW_TILE == 0 and d == SEG

    to_rows = lambda w3: jnp.transpose(w3[0])
    w_in_t, m_in_t, v_in_t = to_rows(w_in), to_rows(m_w_in), to_rows(v_w_in)
    first = [w_in_t.astype(BF16), meta_tokens, conv_dw_w[0]]
    rest = [w_[0].astype(BF16) for w_ in (w_attn_out, w_conv_out, w_out, w_mlp_up, w_mlp_down)]
    tgt = jnp.concatenate([jnp.zeros((batch, N_META, d), F32), loss_target], axis=1).reshape(n, d)
    h0_rows = jnp.pad(x, ((0, 0), (N_META, 0), (0, 0)))
    g1 = norm_mix_gain.reshape(1, -1)
    hn1_rows = _rms_fwd("rms1", h0_rows.reshape(n, d), g1)
    gather_a = _exchange_start("gather_in_start", [(f_, False) for f_ in first], ks=CHIP_PEERS)
    level_1 = _exchange_wait("gather_in_wait", gather_a, [gather_a["token"], tgt, hn1_rows, w_in_t, m_in_t, v_in_t] + rest)
    passed = _pass_on_start("gather_in_pass_start", level_1)
    w_in_g, meta_g, w_dw_g = _pass_on_wait("gather_in_pass_wait", passed, passed["token"])
    gather_b = _exchange_start("gather_rest_start", [(r_, False) for r_ in rest])
    n_fg = b_forget.shape[1]
    shard_w = w_in.shape[2]
    wt = w_in_g.reshape(N_DEV * shard_w, d)
    o_fg = 3 * SEG
    seg_rows = [0, SEG, 2 * SEG] + [o_fg + n_fg + i * SEG for i in range(4)]
    d_ff = w_mlp_down.shape[1] * N_DEV
    ff_blk = d_ff // N_DEV
    meta_f = _cols_from_shards(meta_g)
    w_dw = _cols_from_shards(w_dw_g)

    row2 = lambda v: v.reshape(1, -1)
    g2, g3 = row2(norm_mlp_gain), row2(final_norm_gain)
    b_fg = jnp.pad(b_forget, ((0, 0), (0, FG_PAD - n_fg)))
    h0 = lax.dynamic_update_slice(h0_rows, jnp.broadcast_to(meta_f[None], (batch, N_META, d)), (0, 0, 0)).reshape(n, d)

    meta_n = jnp.broadcast_to(_rms_fwd("rms1_meta", meta_f, g1)[None], (batch, N_META, d))
    hn1 = lax.dynamic_update_slice(hn1_rows.reshape(batch, t, d), meta_n, (0, 0, 0)).reshape(n, d)
    proj = lambda name, off, width, tn, dt: _mm_nt(name, [(hn1, _a_rows(d), wt, _wt_rows(tn, off))], n, width, tn, dt,
                                                   after=gather_b["token"])
    qkv = proj("proj_qkv", 0, 3 * SEG, SEG, BF16)
    glu = proj("proj_glu", seg_rows[3], 2 * SEG, SEG, BF16)
    gates = proj("proj_gates", seg_rows[5], 2 * SEG, SEG, BF16)
    fg = proj("proj_fg", o_fg, FG_PAD, FG_PAD, F32)

    cum = _fox_prep_fwd("fox_cumsum", fg, b_fg, batch)
    cumr = _key_sums_to_blocks(cum, batch, t, n_pairs)
    o, lse = _attn_fwd("attn_fwd", qkv, cumr, batch)
    rest = _exchange_wait("gather_rest_wait", gather_b, o)
    w_ao, w_co, w_o = [r_.reshape(d, d) for r_ in rest[:3]]
    w_up = rest[3]
    w_dn = rest[4].reshape(d_ff, d)
    a = _mm_nn("attn_out", o, w_ao, _w_cols(d, d, 0), d, d, BF16)

    c1 = _glu_conv_fwd("glu_conv", glu, b_glu, w_dw, conv_dw_b, batch)
    c3 = _ln_silu_fwd("ln_silu", c1, conv_ln_gain, conv_ln_bias)
    c = _mm_nn("conv_out", c3, w_co, _w_cols(d, d, 0), d, d, BF16)

    mrg = _merge_fwd("merge", gates, a, c, b_conv_out)
    mo = _mm_nn("mix_out", mrg, w_o, _w_cols(d, d, 0), d, d, F32)
    h1, hn2 = _rms_fwd("resid_rms2", h0, g2, res=mo)
    per = ff_blk // 512
    act = _mm_nn("mlp_up", hn2, w_up, pl.BlockSpec((None, d, 512), lambda i, j: (j // per, 0, j % per)),
                 d_ff, 512, BF16, relu2=True)
    dn = _mm_nn("mlp_down", act, w_dn, _w_cols(d_ff, d, 0), d, d, F32, tm=ROW_TILE)
    dh2, dh2b, loss_blk, dg3 = _final("final_loss", h1, dn, tgt, g3, batch)

    dup = _mm_nt("d_mlp_down", [(dh2b, _a_rows(d), w_dn, _w_rows(d, d))], n, d_ff, d, BF16, relu_bwd_of=act)
    dw_dn = _grad_w("gw_mlp_down", act, dh2b)
    dhn2 = _mm_nt("d_mlp_up", [(dup, _a_rows(ff_blk, g), w_up, pl.BlockSpec((None, 512, ff_blk), lambda i, j, g=g: (g, j, 0)))
                               for g in range(N_DEV)], n, d, 512, BF16)
    dw_up = _mm_tn("gw_mlp_up", hn2, lambda a_: 0, d, dup, lambda b_: b_, ff_blk, (N_DEV, d, ff_blk),
                   pl.BlockSpec((None, d, ff_blk), lambda a_, b_: (b_, 0, 0)), (1, N_DEV))
    scatter_1 = _exchange_start("scatter_mlp_start", [(dw_dn.reshape(N_DEV, ff_blk, d), True), (dw_up, True)])
    dh1, dg2, dh1b = _rms_bwd("rms2_bwd", dhn2, h1, g2 + scatter_1["token"][0:1, 0:1], dh2, batch, with_bf16=True)

    dm = _mm_nt("d_mix_out", [(dh1b, _a_rows(d), w_o, _w_rows(d, d))], n, d, d, BF16)
    dw_o = _grad_w("gw_mix_out", mrg, dh1b)
    da, dc, dga, dgc, dbco = _merge_bwd("merge_bwd", dm, gates, a, c, b_conv_out)

    do = _mm_nt("d_attn_out", [(da, _a_rows(d), w_ao, _w_rows(d, d))], n, d, d, BF16)
    dw_ao = _grad_w("gw_attn_out", o, da)
    dc3 = _mm_nt("d_conv_out", [(dc, _a_rows(d), w_co, _w_rows(d, d))], n, d, d, BF16)
    dw_co = _grad_w("gw_conv_out", c3, dc)

    scatter_2 = _exchange_start("scatter_mix_start", [(dw_.reshape(N_DEV, d // N_DEV, d), True)
                                                      for dw_ in (dw_o, dw_ao, dw_co)])
    dc1, dg_ln, db_ln = _ln_silu_bwd("ln_silu_bwd", dc3, c1, conv_ln_gain + scatter_2["token"][0:1, 0:1],
                                     conv_ln_bias)
    dglu_a, dglu_g, dw_dw, db_dw, dbg_a, dbg_g = _glu_conv_bwd("glu_conv_bwd", dc1, glu, b_glu, w_dw, batch)

    dq, dk, dv, dcumr, dcum_q = _attn_bwd("attn_bwd", qkv, o, do, lse, cumr, batch)
    dcum_k = jnp.pad(_key_sums_from_blocks(dcumr, batch, t, n_pairs), ((0, 0), (0, FG_PAD - 2 * n_pairs)))
    dfg, db_fg = _fox_prep_bwd("fox_cumsum_bwd", dcum_k, dcum_q, fg, b_fg, batch)

    segs = [dq, dk, dv, dglu_a, dglu_g, dga, dgc]
    gw_t = [_grad_w("gw_in_%d" % i, s_, hn1) for i, s_ in enumerate(segs)]
    gw_fg = _grad_w("gw_in_fg", dfg, hn1)[:n_fg]
    dw_in_t = jnp.concatenate(gw_t[:3] + [gw_fg] + gw_t[3:], axis=0).reshape(N_DEV, shard_w, d)
    scatter_3 = _exchange_start("scatter_in_start", [(dw_in_t, True)])
    pairs = [(s_, _a_rows(SEG, 0, ROW_TILE), wt, _wt_block(SEG, seg_rows[i], 512), "nn") for i, s_ in enumerate(segs)]
    pairs.append((dfg, _a_rows(FG_PAD, 0, ROW_TILE), wt, _wt_block(FG_PAD, o_fg, 512), "nn"))
    dhn1 = _mm_nt("d_proj_in", pairs, n, d, 512, BF16, tm=ROW_TILE, after=scatter_3["token"])
    dh0, dg1, dmeta = _rms_bwd("rms1_bwd", dhn1, h0, g1, dh1, batch, with_meta=True)
    grad_x = dh0.reshape(batch, t, d)[:, N_META:, :]

    small_g = dict(norm_mix_gain=dg1, b_forget=db_fg[:, :n_fg], b_glu=jnp.concatenate([dbg_a, dbg_g], axis=1),
                   conv_dw_b=db_dw, conv_ln_gain=dg_ln, conv_ln_bias=db_ln, b_conv_out=dbco, norm_mlp_gain=dg2,
                   final_norm_gain=dg3)
    scatter_4 = _exchange_start("scatter_small_start", [
        (_shards_from_cols(dmeta), True), (_shards_from_cols(dw_dw), True), (_cat_small(small_g), False),
        (loss_blk[0:1, :], False)])

    grads, deltas, new_m, new_v = {}, {}, {}, {}

    def update(k, parts):
        shp = weights[k].shape
        if k == "w_in":
            res_ = _adamw("adamw_" + k, parts, w_in_t, m_in_t, v_in_t)
            res_ = [jnp.transpose(r) for r in res_]
        else:
            w2 = lambda arr: arr.reshape(parts.shape[1:])
            res_ = _adamw("adamw_" + k, parts, w2(weights[k]), w2(mom_m[k]), w2(mom_v[k]))
        grads[k], deltas[k], new_m[k], new_v[k] = [r.reshape(shp) for r in res_]

    for k, parts in zip(("w_mlp_down", "w_mlp_up"), _exchange_wait("scatter_mlp_wait", scatter_1, scatter_4["token"])):
        update(k, parts)
    for k, parts in zip(("w_out", "w_attn_out", "w_conv_out"),
                        _exchange_wait("scatter_mix_wait", scatter_2, deltas["w_mlp_up"])):
        update(k, parts)
    update("w_in", _exchange_wait("scatter_in_wait", scatter_3, deltas["w_conv_out"])[0])
    reduced = _exchange_wait("scatter_small_wait", scatter_4, deltas["w_in"])
    loss = jnp.sum(reduced.pop()[:, 0, 0])
    for k, parts in zip(("meta_tokens", "conv_dw_w"), reduced[:-1]):
        update(k, parts)
    res = _adamw("adamw_small", reduced[-1], _cat_small(weights), _cat_small(mom_m), _cat_small(mom_v))
    shapes = {k: weights[k].shape for k in SMALL}
    for dst, r in zip((grads, deltas, new_m, new_v), res):
        dst.update(_split_small(r, shapes))

    return (loss, grad_x, *[grads[k] for k in names], *[deltas[k] for k in names],
            *[new_m[k] for k in names], *[new_v[k] for k in names])
```

```python
import functools

import jax
import jax.numpy as jnp
from jax import lax
from jax.experimental import pallas as pl
from jax.experimental.pallas import tpu as pltpu

F32, BF16 = jnp.float32, jnp.bfloat16
N_DEV = 8
N_META = 16
HEAD_DIM = 64
LANES = 128
CONV_W = 31
RMS_EPS = 1e-6
LN_EPS = 1e-5
ROW_TILE = 688
MM_TM = 2 * ROW_TILE
SEG = 1024
FG_PAD = 128
VMEM_LIMIT = 56 * 1024 * 1024
ADAM_LR, ADAM_B1, ADAM_B2, ADAM_EPS, ADAM_WD, ADAM_STEP = 0.001, 0.9, 0.999, 1e-08, 0.01, 10
NEG = -1e30
LOG2E = 1.4426950408889634

SMALL = ("norm_mix_gain", "b_forget", "b_glu", "conv_dw_b", "conv_ln_gain", "conv_ln_bias", "b_conv_out",
         "norm_mlp_gain", "final_norm_gain")
SMALL_W = {"norm_mix_gain": 1024, "b_forget": 128, "b_glu": 2048, "conv_dw_b": 1024, "conv_ln_gain": 1024,
           "conv_ln_bias": 1024, "b_conv_out": 1024, "norm_mlp_gain": 1024, "final_norm_gain": 1024}
SMALL_N = {"norm_mix_gain": 1024, "b_forget": 16, "b_glu": 2048, "conv_dw_b": 1024, "conv_ln_gain": 1024,
           "conv_ln_bias": 1024, "b_conv_out": 1024, "norm_mlp_gain": 1024, "final_norm_gain": 1024}


def _params(sem=None):
    return pltpu.CompilerParams(dimension_semantics=sem, vmem_limit_bytes=VMEM_LIMIT)


def _sigmoid(x):
    return 1.0 / (1.0 + jnp.exp(-x))


def _dot_nt(a, b):
    return lax.dot_general(a, b, (((1,), (1,)), ((), ())), preferred_element_type=F32)


def _dot_tn(a, b):
    return lax.dot_general(a, b, (((0,), (0,)), ((), ())), preferred_element_type=F32)


HBM_SPEC = pl.BlockSpec(memory_space=pltpu.HBM)
SEM_SPEC = pl.BlockSpec(memory_space=pltpu.SEMAPHORE)
DATAFLOW = pltpu.SideEffectType.DATAFLOW_SIDE_EFFECTING


def _device_index():
    return 4 * lax.axis_index("x") + 2 * lax.axis_index("y") + lax.axis_index("c")


def _peers():
    x, y, c = lax.axis_index("x"), lax.axis_index("y"), lax.axis_index("c")
    out = []
    for k in range(1, N_DEV):
        px = 1 - x if k & 4 else x
        py = 1 - y if k & 2 else y
        pc = 1 - c if k & 1 else c
        out.append((k, (px, py, pc), 4 * px + 2 * py + pc))
    return out


def _peer_copy(per_dest, src_ref, land_ref, send_sems, recv_sems, a, k, dev, peer):
    src = src_ref.at[peer] if per_dest else src_ref
    return pltpu.make_async_remote_copy(
        src_ref=src, dst_ref=land_ref.at[_device_index()], send_sem=send_sems.at[a * (N_DEV - 1) + k - 1],
        recv_sem=recv_sems.at[a * (N_DEV - 1) + k - 1], device_id=dev, device_id_type=pl.DeviceIdType.MESH)


ALL_PEERS = tuple(range(1, N_DEV))
CHIP_PEERS = (1, 2, 4, 6)
FAR_PEERS = (2, 4, 6)


def _own_copy(per_dest, src_ref, land_ref, send_sems, n, a):
    me = _device_index()
    return pltpu.make_async_copy(src_ref.at[me] if per_dest else src_ref, land_ref.at[me],
                                 send_sems.at[n * (N_DEV - 1) + a])


def _exchange_start(name, items, ks=ALL_PEERS):
    n = len(items)
    per_dest = [it[1] for it in items]

    def body(*refs):
        srcs, lands = refs[:n], refs[n:2 * n]
        send_sems, recv_sems, token = refs[2 * n], refs[2 * n + 1], refs[-1]
        for a in range(n):
            _own_copy(per_dest[a], srcs[a], lands[a], send_sems, n, a).start()
            for k, dev, peer in _peers():
                if k in ks:
                    _peer_copy(per_dest[a], srcs[a], lands[a], send_sems, recv_sems, a, k, dev, peer).start()
        token[...] = jnp.zeros(token.shape, F32)

    srcs = [pltpu.with_memory_space_constraint(it[0], pltpu.HBM) for it in items]
    lands = []
    for arr, pd in items:
        shp = arr.shape if pd else (N_DEV,) + arr.shape
        lands.append(pltpu.with_memory_space_constraint(lax.empty(shp, arr.dtype), pltpu.HBM))
    sems = pltpu.SemaphoreType.DMA((n * N_DEV,))
    res = pl.pallas_call(
        body, name=name,
        out_shape=(sems, sems, *[pltpu.HBM(a_.shape, a_.dtype) for a_ in srcs + lands],
                   jax.ShapeDtypeStruct((8, 128), F32)),
        in_specs=[HBM_SPEC] * (2 * n),
        out_specs=(SEM_SPEC, SEM_SPEC, *[HBM_SPEC] * (2 * n), pl.BlockSpec(memory_space=pltpu.VMEM)),
        input_output_aliases={i: 2 + i for i in range(2 * n)},
        compiler_params=pltpu.CompilerParams(has_side_effects=DATAFLOW),
    )(*srcs, *lands)
    return dict(per_dest=per_dest, ks=ks, send=res[0], recv=res[1], srcs=list(res[2:2 + n]),
                lands=list(res[2 + n:2 + 2 * n]), token=res[-1])


def _exchange_wait(name, started, after):
    per_dest = started["per_dest"]
    n = len(per_dest)

    def body(*refs):
        srcs, lands = refs[:n], refs[n:2 * n]
        send_sems, recv_sems = refs[2 * n], refs[2 * n + 1]
        for a in range(n):
            _own_copy(per_dest[a], srcs[a], lands[a], send_sems, n, a).wait()
            for k, dev, peer in _peers():
                if k in started["ks"]:
                    cp = _peer_copy(per_dest[a], srcs[a], lands[a], send_sems, recv_sems, a, k, dev, peer)
                    cp.wait_send()
                    cp.wait_recv()

    bufs = started["srcs"] + started["lands"]
    after = list(after) if isinstance(after, (list, tuple)) else [after]
    res = pl.pallas_call(
        body, name=name, out_shape=tuple(pltpu.HBM(b_.shape, b_.dtype) for b_ in bufs),
        in_specs=[HBM_SPEC] * (2 * n) + [SEM_SPEC, SEM_SPEC] + [pl.BlockSpec(memory_space=pl.ANY)] * len(after),
        out_specs=tuple([HBM_SPEC] * (2 * n)), input_output_aliases={i: i for i in range(2 * n)},
        compiler_params=pltpu.CompilerParams(has_side_effects=DATAFLOW),
    )(*bufs, started["send"], started["recv"], *after)
    return list(res[n:])


def _pass_on_copy(land_ref, send_sems, recv_sems, a, idx, slot):
    sibling = (lax.axis_index("x"), lax.axis_index("y"), 1 - lax.axis_index("c"))
    return pltpu.make_async_remote_copy(
        src_ref=land_ref.at[slot], dst_ref=land_ref.at[slot], send_sem=send_sems.at[a * len(FAR_PEERS) + idx],
        recv_sem=recv_sems.at[a * len(FAR_PEERS) + idx], device_id=sibling, device_id_type=pl.DeviceIdType.MESH)


def _pass_on_start(name, lands):
    n = len(lands)

    def body(*refs):
        send_sems, recv_sems, token = refs[n], refs[n + 1], refs[-1]
        slots = {k: peer for k, _, peer in _peers()}
        for a in range(n):
            for idx, k in enumerate(FAR_PEERS):
                _pass_on_copy(refs[a], send_sems, recv_sems, a, idx, slots[k]).start()
        token[...] = jnp.zeros(token.shape, F32)

    lands = [pltpu.with_memory_space_constraint(l_, pltpu.HBM) for l_ in lands]
    sems = pltpu.SemaphoreType.DMA((n * len(FAR_PEERS),))
    res = pl.pallas_call(
        body, name=name,
        out_shape=(sems, sems, *[pltpu.HBM(l_.shape, l_.dtype) for l_ in lands], jax.ShapeDtypeStruct((8, 128), F32)),
        in_specs=[HBM_SPEC] * n, out_specs=(SEM_SPEC, SEM_SPEC, *[HBM_SPEC] * n, pl.BlockSpec(memory_space=pltpu.VMEM)),
        input_output_aliases={i: 2 + i for i in range(n)},
        compiler_params=pltpu.CompilerParams(has_side_effects=DATAFLOW),
    )(*lands)
    return dict(send=res[0], recv=res[1], lands=list(res[2:2 + n]), token=res[-1])


def _pass_on_wait(name, passed, after):
    n = len(passed["lands"])

    def body(*refs):
        send_sems, recv_sems = refs[n], refs[n + 1]
        slots = {k: peer for k, _, peer in _peers()}
        for a in range(n):
            for idx, k in enumerate(FAR_PEERS):
                _pass_on_copy(refs[a], send_sems, recv_sems, a, idx, slots[k]).wait_send()
                _pass_on_copy(refs[a], send_sems, recv_sems, a, idx, slots[k ^ 1]).wait_recv()

    res = pl.pallas_call(
        body, name=name, out_shape=tuple(pltpu.HBM(l_.shape, l_.dtype) for l_ in passed["lands"]),
        in_specs=[HBM_SPEC] * n + [SEM_SPEC, SEM_SPEC, pl.BlockSpec(memory_space=pl.ANY)],
        out_specs=tuple([HBM_SPEC] * n), input_output_aliases={i: i for i in range(n)},
        compiler_params=pltpu.CompilerParams(has_side_effects=DATAFLOW),
    )(*passed["lands"], passed["send"], passed["recv"], after)
    return list(res)


def _mm_nn(name, x, w, w_spec, n_out, tn, out_dtype, relu2=False, tm=MM_TM):
    m, k = x.shape

    def body(x_ref, w_ref, *outs):
        acc = jnp.dot(x_ref[...], w_ref[...], preferred_element_type=F32)
        if relu2:
            acc = jnp.maximum(acc, 0.0)
            acc = acc * acc
        outs[0][...] = acc.astype(outs[0].dtype)

    o_spec = pl.BlockSpec((tm, tn), lambda i, j: (i, j))
    return pl.pallas_call(
        body, name=name, out_shape=jax.ShapeDtypeStruct((m, n_out), out_dtype), grid=(m // tm, n_out // tn),
        in_specs=[pl.BlockSpec((tm, k), lambda i, j: (i, 0)), w_spec],
        out_specs=o_spec, compiler_params=_params(("parallel", "parallel")),
    )(x, w)


def _mm_nt(name, pairs, m, n_out, tn, out_dtype, relu_bwd_of=None, tm=MM_TM, after=None):
    np_ = len(pairs)

    def body(*refs):
        acc = None
        for p in range(np_):
            if len(pairs[p]) == 5:
                d = jnp.dot(refs[2 * p][...], refs[2 * p + 1][...], preferred_element_type=F32)
            else:
                d = _dot_nt(refs[2 * p][...], refs[2 * p + 1][...])
            acc = d if acc is None else acc + d
        if relu_bwd_of is not None:
            acc = acc * (2.0 * jnp.sqrt(refs[2 * np_][...].astype(F32)))
        refs[-1][...] = acc.astype(refs[-1].dtype)

    o_spec = pl.BlockSpec((tm, tn), lambda i, j: (i, j))
    operands, specs = [], []
    for pair in pairs:
        operands += [pair[0], pair[2]]
        specs += [pair[1], pair[3]]
    if relu_bwd_of is not None:
        operands.append(relu_bwd_of)
        specs.append(o_spec)
    if after is not None:
        operands.append(after)
        specs.append(pl.BlockSpec((8, 128), lambda i, j: (0, 0)))
    return pl.pallas_call(
        body, name=name, out_shape=jax.ShapeDtypeStruct((m, n_out), out_dtype), grid=(m // tm, n_out // tn),
        in_specs=specs, out_specs=o_spec, compiler_params=_params(("parallel", "parallel")),
    )(*operands)


def _mm_tn(name, x, x_col, ta, dy, dy_col, tb, out_shape, out_spec, grid_ab):
    m = x.shape[0]

    def body(x_ref, dy_ref, o_ref):
        o_ref[...] = _dot_tn(x_ref[...], dy_ref[...]).astype(BF16)

    return pl.pallas_call(
        body, name=name, out_shape=jax.ShapeDtypeStruct(out_shape, BF16), grid=grid_ab,
        in_specs=[pl.BlockSpec((m, ta), lambda a, b: (0, x_col(a))),
                  pl.BlockSpec((m, tb), lambda a, b: (0, dy_col(b)))],
        out_specs=out_spec, compiler_params=_params(("parallel", "parallel")),
    )(x, dy)


def _w_cols(k, tn, off_blocks):
    return pl.BlockSpec((k, tn), lambda i, j: (0, off_blocks + j))


def _a_rows(kw, col_block=0, tm=MM_TM):
    return pl.BlockSpec((tm, kw), lambda i, j: (i, col_block))


def _w_rows(tn, kw, col_block=0):
    return pl.BlockSpec((tn, kw), lambda i, j: (j, col_block))


def _wt_rows(tn, off):
    return pl.BlockSpec((pl.Element(tn), pl.Element(SEG)), lambda i, j: (pl.multiple_of(off + tn * j, 16), 0))


def _wt_block(k, off, tn):
    return pl.BlockSpec((pl.Element(k), pl.Element(tn)), lambda i, j: (off, pl.multiple_of(tn * j, 128)))


GW_TILE = 512


def _grad_w_rows(name, xs, dy):
    m, nb = dy.shape
    per = [x_.shape[1] // GW_TILE for x_ in xs]
    steps = sum(per)
    tiles = [(k, h) for k in range(len(xs)) for h in range(per[k])]

    def body(*refs):
        x_refs, dy_ref, o_ref, buf, sems = refs[:len(xs)], refs[len(xs)], refs[len(xs) + 1], refs[-2], refs[-1]
        s_ = pl.program_id(0)

        def fetch(step, slot):
            for idx, (k, h) in enumerate(tiles):
                @pl.when(step == idx)
                def _():
                    pltpu.make_async_copy(x_refs[k].at[:, pl.ds(h * GW_TILE, GW_TILE)], buf.at[slot], sems.at[slot]).start()

        @pl.when(s_ == 0)
        def _():
            fetch(s_, 0)

        @pl.when(s_ + 1 < steps)
        def _():
            fetch(s_ + 1, (s_ + 1) % 2)

        slot = s_ % 2
        pltpu.make_async_copy(x_refs[0].at[:, pl.ds(0, GW_TILE)], buf.at[slot], sems.at[slot]).wait()
        o_ref[...] = _dot_tn(buf[slot], dy_ref[...]).astype(BF16)

    return pl.pallas_call(
        body, name=name, out_shape=jax.ShapeDtypeStruct((steps * GW_TILE, nb), BF16), grid=(steps,),
        in_specs=[pl.BlockSpec(memory_space=pl.ANY)] * len(xs) + [pl.BlockSpec((m, nb), lambda i: (0, 0))],
        out_specs=pl.BlockSpec((GW_TILE, nb), lambda i: (i, 0)),
        scratch_shapes=[pltpu.VMEM((2, m, GW_TILE), BF16), pltpu.SemaphoreType.DMA((2,))],
        compiler_params=_params(("arbitrary",)))(*xs, dy)


def _grad_w(name, x, dy):
    na, nb = x.shape[1], dy.shape[1]
    ta, tb = min(na, 1024), min(nb, 512)
    return _mm_tn(name, x, lambda a: a, ta, dy, lambda b: b, tb, (na, nb),
                  pl.BlockSpec((ta, tb), lambda a, b: (a, b)), (na // ta, nb // tb))


def _row_spec(width):
    return pl.BlockSpec((ROW_TILE, width), lambda i: (i, 0))


def _vec_spec(width):
    return pl.BlockSpec((1, width), lambda i: (0, 0))


def _rms_fwd(name, h, g, res=None):
    n, d = h.shape
    tile = ROW_TILE if n % ROW_TILE == 0 else n
    row_spec = pl.BlockSpec((tile, d), lambda i: (i, 0))

    def body(*refs):
        if res is None:
            h_ref, g_ref, hn_ref = refs
            hv = h_ref[...]
        else:
            h_ref, r_ref, g_ref, hs_ref, hn_ref = refs
            hv = h_ref[...] + r_ref[...]
            hs_ref[...] = hv
        r = lax.rsqrt(jnp.mean(hv * hv, axis=-1, keepdims=True) + RMS_EPS)
        hn_ref[...] = (hv * r * g_ref[...]).astype(BF16)

    ins = [h, g] if res is None else [h, res, g]
    in_specs = [row_spec, _vec_spec(d)] if res is None else [row_spec, row_spec, _vec_spec(d)]
    hn_shape = jax.ShapeDtypeStruct((n, d), BF16)
    if res is None:
        out_shape, out_specs = hn_shape, row_spec
    else:
        out_shape, out_specs = [jax.ShapeDtypeStruct((n, d), F32), hn_shape], [row_spec, row_spec]
    return pl.pallas_call(body, name=name, out_shape=out_shape, grid=(n // tile,), in_specs=in_specs,
                          out_specs=out_specs, compiler_params=_params(("parallel",)))(*ins)


def _rms_bwd(name, dhn, h, g, dres, batch, with_bf16=False, with_meta=False):
    n, d = h.shape
    t = n // batch
    nt = t // ROW_TILE

    def body(dhn_ref, h_ref, g_ref, dres_ref, *outs):
        first = (pl.program_id(0) == 0) & (pl.program_id(1) == 0)
        hv = h_ref[...]
        r = lax.rsqrt(jnp.mean(hv * hv, axis=-1, keepdims=True) + RMS_EPS)
        nrm = hv * r
        dhn = dhn_ref[...].astype(F32)
        dn = dhn * g_ref[...]
        dh = dres_ref[...] + r * (dn - nrm * jnp.mean(dn * nrm, axis=-1, keepdims=True))
        outs[0][...] = dh
        dg_ref = outs[1]

        @pl.when(first)
        def _():
            dg_ref[...] = jnp.zeros(dg_ref.shape, F32)

        dg_ref[...] += jnp.sum(dhn * nrm, axis=0, keepdims=True)
        nxt = 2
        if with_bf16:
            outs[nxt][...] = dh.astype(BF16)
            nxt += 1
        if with_meta:
            meta_ref = outs[nxt]

            @pl.when(first)
            def _():
                meta_ref[...] = jnp.zeros(meta_ref.shape, F32)

            @pl.when(pl.program_id(1) == 0)
            def _():
                meta_ref[...] += dh[0:N_META, :]

    row = pl.BlockSpec((ROW_TILE, d), lambda b, j: (b * nt + j, 0))
    vec = pl.BlockSpec((1, d), lambda b, j: (0, 0))
    shapes = [jax.ShapeDtypeStruct((n, d), F32), jax.ShapeDtypeStruct((1, d), F32)]
    specs = [row, vec]
    if with_bf16:
        shapes.append(jax.ShapeDtypeStruct((n, d), BF16))
        specs.append(row)
    if with_meta:
        shapes.append(jax.ShapeDtypeStruct((N_META, d), F32))
        specs.append(pl.BlockSpec((N_META, d), lambda b, j: (0, 0)))
    return pl.pallas_call(body, name=name, out_shape=shapes, grid=(batch, nt), in_specs=[row, row, vec, row],
                          out_specs=specs, compiler_params=_params(("arbitrary", "arbitrary")))(dhn, h, g, dres)


def _final(name, h1, dn, tgt, g, batch):
    n, d = h1.shape
    t = n // batch
    nt = t // ROW_TILE

    def body(h1_ref, dn_ref, tgt_ref, g_ref, dh_ref, dhb_ref, loss_ref, dg_ref):
        first = (pl.program_id(0) == 0) & (pl.program_id(1) == 0)
        hv = h1_ref[...] + dn_ref[...]
        r = lax.rsqrt(jnp.mean(hv * hv, axis=-1, keepdims=True) + RMS_EPS)
        nrm = hv * r
        gv = g_ref[...]
        pos = pl.program_id(1) * ROW_TILE + lax.broadcasted_iota(jnp.int32, (ROW_TILE, 1), 0)
        diff = jnp.where(pos >= N_META, nrm * gv - tgt_ref[...], 0.0)
        dy = diff * (1.0 / d)

        @pl.when(first)
        def _():
            loss_ref[...] = jnp.zeros(loss_ref.shape, F32)
            dg_ref[...] = jnp.zeros(dg_ref.shape, F32)

        loss_ref[...] += jnp.full(loss_ref.shape, 0.5 / d, F32) * jnp.sum(diff * diff)
        dg_ref[...] += jnp.sum(dy * nrm, axis=0, keepdims=True)
        dng = dy * gv
        dh = r * (dng - nrm * jnp.mean(dng * nrm, axis=-1, keepdims=True))
        dh_ref[...] = dh
        dhb_ref[...] = dh.astype(BF16)

    row = pl.BlockSpec((ROW_TILE, d), lambda b, j: (b * nt + j, 0))
    vec = pl.BlockSpec((1, d), lambda b, j: (0, 0))
    return pl.pallas_call(
        body, name=name, grid=(batch, nt), in_specs=[row, row, row, vec],
        out_shape=[jax.ShapeDtypeStruct((n, d), F32), jax.ShapeDtypeStruct((n, d), BF16),
                   jax.ShapeDtypeStruct((8, 128), F32), jax.ShapeDtypeStruct((1, d), F32)],
        out_specs=[row, row, pl.BlockSpec((8, 128), lambda b, j: (0, 0)), vec],
        compiler_params=_params(("arbitrary", "arbitrary")))(h1, dn, tgt, g)


def _ln_silu_fwd(name, c1, g, b):
    n, d = c1.shape

    def body(c_ref, g_ref, b_ref, o_ref):
        xv = c_ref[...]
        xc = xv - jnp.mean(xv, axis=-1, keepdims=True)
        rstd = lax.rsqrt(jnp.mean(xc * xc, axis=-1, keepdims=True) + LN_EPS)
        c2 = xc * rstd * g_ref[...] + b_ref[...]
        o_ref[...] = (c2 * _sigmoid(c2)).astype(BF16)

    return pl.pallas_call(body, name=name, out_shape=jax.ShapeDtypeStruct((n, d), BF16), grid=(n // ROW_TILE,),
                          in_specs=[_row_spec(d), _vec_spec(d), _vec_spec(d)], out_specs=_row_spec(d),
                          compiler_params=_params(("parallel",)))(c1, g, b)


def _ln_silu_bwd(name, dc3, c1, g, b):
    n, d = c1.shape

    def body(d_ref, c_ref, g_ref, b_ref, dc1_ref, dg_ref, db_ref):
        xv = c_ref[...]
        xc = xv - jnp.mean(xv, axis=-1, keepdims=True)
        rstd = lax.rsqrt(jnp.mean(xc * xc, axis=-1, keepdims=True) + LN_EPS)
        xh = xc * rstd
        c2 = xh * g_ref[...] + b_ref[...]
        s = _sigmoid(c2)
        dc2 = d_ref[...].astype(F32) * (s * (1.0 + c2 * (1.0 - s)))

        @pl.when(pl.program_id(0) == 0)
        def _():
            dg_ref[...] = jnp.zeros(dg_ref.shape, F32)
            db_ref[...] = jnp.zeros(db_ref.shape, F32)

        dg_ref[...] += jnp.sum(dc2 * xh, axis=0, keepdims=True)
        db_ref[...] += jnp.sum(dc2, axis=0, keepdims=True)
        dxh = dc2 * g_ref[...]
        dc1_ref[...] = rstd * (dxh - jnp.mean(dxh, axis=-1, keepdims=True)
                               - xh * jnp.mean(dxh * xh, axis=-1, keepdims=True))

    return pl.pallas_call(
        body, name=name, grid=(n // ROW_TILE,),
        out_shape=[jax.ShapeDtypeStruct((n, d), F32), jax.ShapeDtypeStruct((1, d), F32),
                   jax.ShapeDtypeStruct((1, d), F32)],
        in_specs=[_row_spec(d), _row_spec(d), _vec_spec(d), _vec_spec(d)],
        out_specs=[_row_spec(d), _vec_spec(d), _vec_spec(d)],
        compiler_params=_params(("arbitrary",)))(dc3, c1, g, b)


MERGE_TC = 512


def _merge_fwd(name, gates, a, c, b_co):
    n, d = a.shape
    nc = d // MERGE_TC

    def body(ga_ref, gc_ref, a_ref, c_ref, b_ref, m_ref):
        f32 = lambda r_: r_[...].astype(F32)
        m = _sigmoid(f32(ga_ref)) * f32(a_ref) + _sigmoid(f32(gc_ref)) * (f32(c_ref) + b_ref[...])
        m_ref[...] = m.astype(BF16)

    blk = lambda off: pl.BlockSpec((ROW_TILE, MERGE_TC), lambda i, j: (i, off + j))
    return pl.pallas_call(
        body, name=name, out_shape=jax.ShapeDtypeStruct((n, d), BF16), grid=(n // ROW_TILE, nc),
        in_specs=[blk(0), blk(nc), blk(0), blk(0), pl.BlockSpec((1, MERGE_TC), lambda i, j: (0, j))],
        out_specs=blk(0), compiler_params=_params(("parallel", "parallel")))(gates, gates, a, c, b_co)


def _merge_bwd(name, dm, gates, a, c, b_co):
    n, d = a.shape
    nc = d // MERGE_TC

    def body(dm_ref, ga_ref, gc_ref, a_ref, c_ref, b_ref, da_ref, dc_ref, dga_ref, dgc_ref, dbco_ref):
        f32 = lambda r_: r_[...].astype(F32)
        dmv = f32(dm_ref)
        sa, sc = _sigmoid(f32(ga_ref)), _sigmoid(f32(gc_ref))
        dc = dmv * sc
        da_ref[...] = (dmv * sa).astype(BF16)
        dc_ref[...] = dc.astype(BF16)
        dga_ref[...] = (dmv * f32(a_ref) * sa * (1.0 - sa)).astype(BF16)
        dgc_ref[...] = (dmv * (f32(c_ref) + b_ref[...]) * sc * (1.0 - sc)).astype(BF16)

        @pl.when(pl.program_id(1) == 0)
        def _():
            dbco_ref[...] = jnp.zeros(dbco_ref.shape, F32)

        dbco_ref[...] += jnp.sum(dc, axis=0, keepdims=True)

    blk = lambda off: pl.BlockSpec((ROW_TILE, MERGE_TC), lambda j, i: (i, off + j))
    vec = pl.BlockSpec((1, MERGE_TC), lambda j, i: (0, j))
    act = jax.ShapeDtypeStruct((n, d), BF16)
    return pl.pallas_call(
        body, name=name, grid=(nc, n // ROW_TILE),
        out_shape=[act, act, act, act, jax.ShapeDtypeStruct((1, d), F32)],
        in_specs=[blk(0), blk(0), blk(nc), blk(0), blk(0), vec],
        out_specs=[blk(0), blk(0), blk(0), blk(0), vec],
        compiler_params=_params(("parallel", "arbitrary")))(dm, gates, gates, a, c, b_co)


CONV_TC = 128
CONV_HALO = 32


def _conv_chunk(t):
    return 48 if t % 48 == 0 else 32 if t % 32 == 0 else 16


def _fold8(x):
    out = x[0:8]
    for k in range(1, x.shape[0] // 8):
        out = out + x[8 * k:8 * k + 8]
    return out


def _glu_conv_fwd(name, glu, b_glu, w_dw, b_dw, batch):
    n, c2 = glu.shape
    c = c2 // 2
    t = n // batch
    nc = c // CONV_TC

    def body(a_ref, gt_ref, ba_ref, bg_ref, w_ref, bdw_ref, o_ref, pad_ref):
        u = (a_ref[...].astype(F32) + ba_ref[...]) * _sigmoid(gt_ref[...].astype(F32) + bg_ref[...])
        pad_ref[0:CONV_HALO, :] = jnp.zeros((CONV_HALO, CONV_TC), F32)
        pad_ref[CONV_HALO:CONV_HALO + t, :] = u
        ch = _conv_chunk(t)
        for r0 in range(0, t, ch):
            acc = jnp.zeros((ch, CONV_TC), F32) + bdw_ref[...]
            for j in range(CONV_W):
                off = r0 + CONV_HALO - (CONV_W - 1) + j
                acc = acc + w_ref[j:j + 1, :] * pad_ref[off:off + ch, :]
            o_ref[r0:r0 + ch, :] = acc

    seq = lambda off: pl.BlockSpec((t, CONV_TC), lambda b, j: (b, off + j))
    vec = lambda off: pl.BlockSpec((1, CONV_TC), lambda b, j: (0, off + j))
    return pl.pallas_call(
        body, name=name, out_shape=jax.ShapeDtypeStruct((n, c), F32), grid=(batch, nc),
        in_specs=[seq(0), seq(nc), vec(0), vec(nc), pl.BlockSpec((CONV_W, CONV_TC), lambda b, j: (0, j)), vec(0)],
        out_specs=seq(0), scratch_shapes=[pltpu.VMEM((t + CONV_HALO, CONV_TC), F32)],
        compiler_params=_params(("parallel", "parallel")))(glu, glu, b_glu, b_glu, w_dw, b_dw)


def _glu_conv_bwd(name, dc1, glu, b_glu, w_dw, batch):
    n, c2 = glu.shape
    c = c2 // 2
    t = n // batch
    nc = c // CONV_TC

    def body(d_ref, a_ref, gt_ref, ba_ref, bg_ref, w_ref, dga_ref, dgg_ref, dw_ref, dbdw_ref, dba_ref, dbg_ref,
             padu_ref, padd_ref):
        av = a_ref[...].astype(F32) + ba_ref[...]
        sg = _sigmoid(gt_ref[...].astype(F32) + bg_ref[...])
        dc = d_ref[...]
        padu_ref[0:CONV_HALO, :] = jnp.zeros((CONV_HALO, CONV_TC), F32)
        padu_ref[CONV_HALO:CONV_HALO + t, :] = av * sg
        padd_ref[0:t, :] = dc
        padd_ref[t:t + CONV_HALO, :] = jnp.zeros((CONV_HALO, CONV_TC), F32)

        @pl.when(pl.program_id(1) == 0)
        def _():
            dw_ref[...] = jnp.zeros(dw_ref.shape, F32)
            dbdw_ref[...] = jnp.zeros(dbdw_ref.shape, F32)
            dba_ref[...] = jnp.zeros(dba_ref.shape, F32)
            dbg_ref[...] = jnp.zeros(dbg_ref.shape, F32)

        ch = _conv_chunk(t)
        zero8 = jnp.zeros((8, CONV_TC), F32)
        dw_acc = [zero8] * CONV_W
        sum_dc, sum_a, sum_g = zero8, zero8, zero8
        for r0 in range(0, t, ch):
            dcc = d_ref[r0:r0 + ch, :]
            du = jnp.zeros((ch, CONV_TC), F32)
            for j in range(CONV_W):
                back = r0 + CONV_W - 1 - j
                du = du + w_ref[j:j + 1, :] * padd_ref[back:back + ch, :]
                off = r0 + CONV_HALO - (CONV_W - 1) + j
                dw_acc[j] = dw_acc[j] + _fold8(dcc * padu_ref[off:off + ch, :])
            sgc = _sigmoid(gt_ref[r0:r0 + ch, :].astype(F32) + bg_ref[...])
            dga = du * sgc
            dgg = du * padu_ref[CONV_HALO + r0:CONV_HALO + r0 + ch, :] * (1.0 - sgc)
            dga_ref[r0:r0 + ch, :] = dga.astype(BF16)
            dgg_ref[r0:r0 + ch, :] = dgg.astype(BF16)
            sum_dc, sum_a, sum_g = sum_dc + _fold8(dcc), sum_a + _fold8(dga), sum_g + _fold8(dgg)
        for j in range(CONV_W):
            dw_ref[j:j + 1, :] += jnp.sum(dw_acc[j], axis=0, keepdims=True)
        dbdw_ref[...] += jnp.sum(sum_dc, axis=0, keepdims=True)
        dba_ref[...] += jnp.sum(sum_a, axis=0, keepdims=True)
        dbg_ref[...] += jnp.sum(sum_g, axis=0, keepdims=True)

    seq = lambda off: pl.BlockSpec((t, CONV_TC), lambda j, b: (b, off + j))
    vec = lambda off: pl.BlockSpec((1, CONV_TC), lambda j, b: (0, off + j))
    wsp = pl.BlockSpec((CONV_W, CONV_TC), lambda j, b: (0, j))
    act = jax.ShapeDtypeStruct((n, c), BF16)
    v = jax.ShapeDtypeStruct((1, c), F32)
    return pl.pallas_call(
        body, name=name, grid=(nc, batch),
        out_shape=[act, act, jax.ShapeDtypeStruct((CONV_W, c), F32), v, v, v],
        in_specs=[seq(0), seq(0), seq(nc), vec(0), vec(nc), wsp],
        out_specs=[seq(0), seq(0), wsp, vec(0), vec(0), vec(0)],
        scratch_shapes=[pltpu.VMEM((t + CONV_HALO, CONV_TC), F32), pltpu.VMEM((t + CONV_HALO, CONV_TC), F32)],
        compiler_params=_params(("parallel", "arbitrary")))(dc1, glu, glu, b_glu, b_glu, w_dw)


def _split3(x):
    hi = x.astype(BF16)
    r = x - hi.astype(F32)
    mid = r.astype(BF16)
    lo = (r - mid.astype(F32)).astype(BF16)
    return hi, mid, lo


def _tri_matmul(tri, x):
    hi, mid, lo = _split3(x)
    dot = lambda v: jnp.dot(tri, v, preferred_element_type=F32)
    return dot(hi) + dot(mid) + dot(lo)


def _fox_prep_fwd(name, fg, b_fg, batch):
    n, w = fg.shape
    t = n // batch
    nq = t // ROW_TILE

    def body(fg_ref, b_ref, cum_ref):
        row = lax.broadcasted_iota(jnp.int32, (ROW_TILE, ROW_TILE), 0)
        col = lax.broadcasted_iota(jnp.int32, (ROW_TILE, ROW_TILE), 1)
        tri = (row >= col).astype(BF16)
        for k in range(nq):
            rows = slice(k * ROW_TILE, (k + 1) * ROW_TILE)
            z = fg_ref[rows, :] + b_ref[...]
            logf = jnp.minimum(z, 0.0) - jnp.log(1.0 + jnp.exp(-jnp.abs(z)))
            cum = _tri_matmul(tri, logf)
            if k > 0:
                cum = cum + cum_ref[k * ROW_TILE - 1:k * ROW_TILE, :]
            cum_ref[rows, :] = cum

    seq = pl.BlockSpec((t, w), lambda b: (b, 0))
    return pl.pallas_call(body, name=name, out_shape=jax.ShapeDtypeStruct((n, w), F32), grid=(batch,),
                          in_specs=[seq, pl.BlockSpec((1, w), lambda b: (0, 0))], out_specs=seq,
                          compiler_params=_params(("parallel",)))(fg, b_fg)


def _fox_prep_bwd(name, dcum_k, dcum_q, fg, b_fg, batch):
    n, w = fg.shape
    t = n // batch
    nq = t // ROW_TILE

    def body(dk_ref, dq_ref, fg_ref, b_ref, dfg_ref, db_ref, rev_ref):
        row = lax.broadcasted_iota(jnp.int32, (ROW_TILE, ROW_TILE), 0)
        col = lax.broadcasted_iota(jnp.int32, (ROW_TILE, ROW_TILE), 1)
        tri = (col >= row).astype(BF16)

        @pl.when(pl.program_id(0) == 0)
        def _():
            db_ref[...] = jnp.zeros(db_ref.shape, F32)

        for k in reversed(range(nq)):
            rows = slice(k * ROW_TILE, (k + 1) * ROW_TILE)
            dlog = _tri_matmul(tri, dk_ref[rows, :] + dq_ref[rows, :])
            if k < nq - 1:
                dlog = dlog + rev_ref[(k + 1) * ROW_TILE:(k + 1) * ROW_TILE + 1, :]
            rev_ref[rows, :] = dlog
            dfg = dlog * _sigmoid(-(fg_ref[rows, :] + b_ref[...]))
            dfg_ref[rows, :] = dfg.astype(BF16)
            db_ref[...] += jnp.sum(dfg, axis=0, keepdims=True)

    seq = pl.BlockSpec((t, w), lambda b: (b, 0))
    vec = pl.BlockSpec((1, w), lambda b: (0, 0))
    return pl.pallas_call(
        body, name=name, grid=(batch,),
        out_shape=[jax.ShapeDtypeStruct((n, w), BF16), jax.ShapeDtypeStruct((1, w), F32)],
        in_specs=[seq, seq, seq, vec], out_specs=[seq, vec], scratch_shapes=[pltpu.VMEM((t, w), F32)],
        compiler_params=_params(("arbitrary",)))(dcum_k, dcum_q, fg, b_fg)


def _head_masks(x):
    lane = lax.broadcasted_iota(jnp.int32, x.shape, 1)
    zero = jnp.zeros(x.shape, x.dtype)
    return jnp.where(lane < HEAD_DIM, x, zero), jnp.where(lane >= HEAD_DIM, x, zero)


ATTN_BLOCK = 512


def _attn_blocks(t):
    nb = max(t // ATTN_BLOCK, 1)
    blocks = [(i * ATTN_BLOCK, ATTN_BLOCK) for i in range(nb - 1)]
    return blocks + [((nb - 1) * ATTN_BLOCK, t - (nb - 1) * ATTN_BLOCK)]


def _attn_specs(t):
    blocks = _attn_blocks(t)
    width = max(sz for _, sz in blocks)
    qkv = lambda off: pl.BlockSpec((t, LANES), lambda b, h: (b, off + h))
    cumr = pl.BlockSpec((None, None, len(blocks), 8, width), lambda b, h: (b, h, 0, 0, 0))
    return qkv, cumr


def _key_sums_to_blocks(cum, batch, t, n_pairs):
    blocks = _attn_blocks(t)
    width = max(sz for _, sz in blocks)
    cum_h = cum.reshape(batch, t, -1)[:, :, :2 * n_pairs].reshape(batch, t, n_pairs, 2)
    rows = [jnp.pad(jnp.transpose(cum_h[:, s0:s0 + sz], (0, 2, 3, 1)), ((0, 0), (0, 0), (0, 6), (0, width - sz)))
            for s0, sz in blocks]
    return jnp.stack(rows, axis=2)


def _key_sums_from_blocks(dcumr, batch, t, n_pairs):
    cols = [jnp.transpose(dcumr[:, :, j, :2, :sz], (0, 3, 1, 2)) for j, (_, sz) in enumerate(_attn_blocks(t))]
    return jnp.concatenate(cols, axis=1).reshape(batch * t, 2 * n_pairs)


def _causal(size):
    row = lax.broadcasted_iota(jnp.int32, (size, size), 0)
    col = lax.broadcasted_iota(jnp.int32, (size, size), 1)
    return row >= col


def _attn_fwd(name, qkv, cumr, batch):
    n, w3 = qkv.shape
    w = w3 // 3
    t = n // batch
    n_pairs = w // LANES
    blocks = _attn_blocks(t)

    def body(q_ref, k_ref, v_ref, cr_ref, o_ref, lse_ref):
        pair = pl.program_id(1)

        @pl.when(pair == 0)
        def _():
            lse_ref[...] = jnp.zeros(lse_ref.shape, F32)

        for i, (q0, qn) in enumerate(blocks):
            rows = slice(q0, q0 + qn)
            qs = _head_masks(q_ref[rows, :] * (0.125 * LOG2E))
            outs, lses = [], []
            for hh in range(2):
                m = jnp.full((qn, 1), NEG, F32)
                l = jnp.zeros((qn, 1), F32)
                acc = jnp.zeros((qn, LANES), F32)
                for j in range(i + 1):
                    k0, kn = blocks[j]
                    cols = slice(k0, k0 + kn)
                    s = _dot_nt(qs[hh], k_ref[cols, :]) - cr_ref[j, hh:hh + 1, 0:kn] * LOG2E
                    if j == i:
                        s = jnp.where(_causal(qn), s, NEG)
                    m_new = jnp.maximum(m, jnp.max(s, axis=1, keepdims=True))
                    alpha = jnp.exp2(m - m_new)
                    p = jnp.exp2(s - m_new)
                    l = alpha * l + jnp.sum(p, axis=1, keepdims=True)
                    acc = alpha * acc + jnp.dot(p.astype(BF16), v_ref[cols, :], preferred_element_type=F32)
                    m = m_new
                outs.append(acc / l)
                lses.append(m + jnp.log2(l))
            lane = lax.broadcasted_iota(jnp.int32, (qn, LANES), 1)
            o_ref[rows, :] = jnp.where(lane < HEAD_DIM, outs[0], outs[1]).astype(BF16)
            lse_ref[rows, :] = jnp.where(lane == 2 * pair, lses[0],
                                         jnp.where(lane == 2 * pair + 1, lses[1], lse_ref[rows, :]))

    qkv_spec, cumr_spec = _attn_specs(t)
    return pl.pallas_call(
        body, name=name, grid=(batch, n_pairs),
        out_shape=[jax.ShapeDtypeStruct((n, w), BF16), jax.ShapeDtypeStruct((n, LANES), F32)],
        in_specs=[qkv_spec(0), qkv_spec(n_pairs), qkv_spec(2 * n_pairs), cumr_spec],
        out_specs=[qkv_spec(0), pl.BlockSpec((t, LANES), lambda b, h: (b, 0))],
        compiler_params=_params(("parallel", "arbitrary")))(qkv, qkv, qkv, cumr)


def _attn_bwd(name, qkv, o, do, lse, cumr, batch):
    n, w3 = qkv.shape
    w = w3 // 3
    t = n // batch
    n_pairs = w // LANES
    blocks = _attn_blocks(t)

    def body(q_ref, k_ref, v_ref, o_ref, do_ref, lse_ref, cr_ref, dq_ref, dk_ref, dv_ref, dcr_ref, dcq_ref,
             dk_acc, dv_acc):
        pair = pl.program_id(1)
        dk_acc[...] = jnp.zeros(dk_acc.shape, F32)
        dv_acc[...] = jnp.zeros(dv_acc.shape, F32)
        dcr_ref[...] = jnp.zeros(dcr_ref.shape, F32)

        @pl.when(pair == 0)
        def _():
            dcq_ref[...] = jnp.zeros(dcq_ref.shape, F32)

        for i, (q0, qn) in enumerate(blocks):
            rows = slice(q0, q0 + qn)
            qs = _head_masks(q_ref[rows, :] * 0.125)
            q2 = _head_masks(q_ref[rows, :] * (0.125 * LOG2E))
            dos = _head_masks(do_ref[rows, :])
            dq = jnp.zeros((qn, LANES), F32)
            lane = lax.broadcasted_iota(jnp.int32, (qn, LANES), 1)
            dcq = []
            for hh in range(2):
                row_sum = jnp.zeros((qn, 1), F32)
                lse = jnp.sum(jnp.where(lane == 2 * pair + hh, lse_ref[rows, :], 0.0), axis=1, keepdims=True)
                delta = jnp.sum(dos[hh].astype(F32) * o_ref[rows, :].astype(F32), axis=1, keepdims=True)
                for j in range(i + 1):
                    k0, kn = blocks[j]
                    cols = slice(k0, k0 + kn)
                    s = _dot_nt(q2[hh], k_ref[cols, :]) - cr_ref[j, hh:hh + 1, 0:kn] * LOG2E
                    p = jnp.exp2(s - lse)
                    if j == i:
                        p = jnp.where(_causal(qn), p, 0.0)
                    dp = _dot_nt(dos[hh], v_ref[cols, :])
                    ds = p * (dp - delta)
                    pb, dsb = p.astype(BF16), ds.astype(BF16)
                    km = _head_masks(k_ref[cols, :])[hh]
                    dv_acc[j, :, 0:kn] += _dot_tn(dos[hh], pb)
                    dk_acc[j, :, 0:kn] += _dot_tn(qs[hh], dsb)
                    dq = dq + jnp.dot(dsb, km, preferred_element_type=F32)
                    dcr_ref[j, hh:hh + 1, 0:kn] -= jnp.sum(ds, axis=0, keepdims=True)
                    row_sum = row_sum + jnp.sum(ds, axis=1, keepdims=True)
                dcq.append(row_sum)
            dq_ref[rows, :] = (dq * 0.125).astype(BF16)
            dcq_ref[rows, :] = jnp.where(lane == 2 * pair, dcq[0],
                                         jnp.where(lane == 2 * pair + 1, dcq[1], dcq_ref[rows, :]))
        for j, (k0, kn) in enumerate(blocks):
            dk_ref[k0:k0 + kn, :] = dk_acc[j].T[0:kn, :].astype(BF16)
            dv_ref[k0:k0 + kn, :] = dv_acc[j].T[0:kn, :].astype(BF16)

    qkv_spec, cumr_spec = _attn_specs(t)
    wide = -(-max(sz for _, sz in blocks) // LANES) * LANES
    act = jax.ShapeDtypeStruct((n, w), BF16)
    return pl.pallas_call(
        body, name=name, grid=(batch, n_pairs),
        out_shape=[act, act, act, jax.ShapeDtypeStruct(cumr.shape, F32), jax.ShapeDtypeStruct((n, LANES), F32)],
        in_specs=[qkv_spec(0), qkv_spec(n_pairs), qkv_spec(2 * n_pairs), qkv_spec(0), qkv_spec(0),
                  pl.BlockSpec((t, LANES), lambda b, h: (b, 0)), cumr_spec],
        out_specs=[qkv_spec(0), qkv_spec(0), qkv_spec(0), cumr_spec, pl.BlockSpec((t, LANES), lambda b, h: (b, 0))],
        scratch_shapes=[pltpu.VMEM((len(blocks), LANES, wide), F32), pltpu.VMEM((len(blocks), LANES, wide), F32)],
        compiler_params=_params(("parallel", "arbitrary")))(qkv, qkv, qkv, o, do, lse, cumr)


def _adamw(name, parts, w, m, v):
    r, c = w.shape
    tr = 128 if r % 128 == 0 else r
    tc = 256 if tr > 128 and c % 256 == 0 else c
    c1 = 1.0 - ADAM_B1 ** ADAM_STEP
    c2 = 1.0 - ADAM_B2 ** ADAM_STEP

    def body(p_ref, w_ref, m_ref, v_ref, g_ref, d_ref, m2_ref, v2_ref):
        g = p_ref[0].astype(F32)
        for s in range(1, N_DEV):
            g = g + p_ref[s].astype(F32)
        m2 = ADAM_B1 * m_ref[...] + (1.0 - ADAM_B1) * g
        v2 = ADAM_B2 * v_ref[...] + (1.0 - ADAM_B2) * (g * g)
        g_ref[...] = g
        m2_ref[...] = m2
        v2_ref[...] = v2
        d_ref[...] = -ADAM_LR * ((m2 / c1) / (jnp.sqrt(v2 / c2) + ADAM_EPS) + ADAM_WD * w_ref[...])

    blk = pl.BlockSpec((tr, tc), lambda i, j: (i, j))
    shp = jax.ShapeDtypeStruct((r, c), F32)
    return pl.pallas_call(
        body, name=name, out_shape=[shp] * 4, grid=(r // tr, c // tc),
        in_specs=[pl.BlockSpec((N_DEV, tr, tc), lambda i, j: (0, i, j)), blk, blk, blk], out_specs=[blk] * 4,
        compiler_params=_params(("parallel", "parallel")))(parts, w, m, v)


def _cat_small(vals):
    parts = []
    for name in SMALL:
        v = vals[name].reshape(1, -1).astype(F32)
        parts.append(jnp.pad(v, ((0, 0), (0, SMALL_W[name] - v.shape[1]))))
    return jnp.concatenate(parts, axis=1)


def _split_small(row, shapes):
    out, off = {}, 0
    for name in SMALL:
        out[name] = row[0, off:off + SMALL_N[name]].reshape(shapes[name])
        off += SMALL_W[name]
    return out


def _cols_from_shards(g):
    return jnp.transpose(g, (1, 0, 2)).reshape(g.shape[1], N_DEV * g.shape[2])


def _shards_from_cols(a):
    r, c = a.shape
    return jnp.transpose(a.reshape(r, N_DEV, c // N_DEV), (1, 0, 2))


def kernel(x, meta_tokens, norm_mix_gain, w_in, b_forget, w_attn_out, b_glu, conv_dw_w, conv_dw_b, conv_ln_gain, conv_ln_bias, w_conv_out, b_conv_out, w_out, norm_mlp_gain, w_mlp_up, w_mlp_down, final_norm_gain, loss_target, m_meta_tokens, m_norm_mix_gain, m_w_in, m_b_forget, m_w_attn_out, m_b_glu, m_conv_dw_w, m_conv_dw_b, m_conv_ln_gain, m_conv_ln_bias, m_w_conv_out, m_b_conv_out, m_w_out, m_norm_mlp_gain, m_w_mlp_up, m_w_mlp_down, m_final_norm_gain, v_meta_tokens, v_norm_mix_gain, v_w_in, v_b_forget, v_w_attn_out, v_b_glu, v_conv_dw_w, v_conv_dw_b, v_conv_ln_gain, v_conv_ln_bias, v_w_conv_out, v_b_conv_out, v_w_out, v_norm_mlp_gain, v_w_mlp_up, v_w_mlp_down, v_final_norm_gain):
    weights = dict(meta_tokens=meta_tokens, norm_mix_gain=norm_mix_gain, w_in=w_in, b_forget=b_forget, w_attn_out=w_attn_out, b_glu=b_glu, conv_dw_w=conv_dw_w, conv_dw_b=conv_dw_b, conv_ln_gain=conv_ln_gain, conv_ln_bias=conv_ln_bias, w_conv_out=w_conv_out, b_conv_out=b_conv_out, w_out=w_out, norm_mlp_gain=norm_mlp_gain, w_mlp_up=w_mlp_up, w_mlp_down=w_mlp_down, final_norm_gain=final_norm_gain)
    mom_m = dict(meta_tokens=m_meta_tokens, norm_mix_gain=m_norm_mix_gain, w_in=m_w_in, b_forget=m_b_forget, w_attn_out=m_w_attn_out, b_glu=m_b_glu, conv_dw_w=m_conv_dw_w, conv_dw_b=m_conv_dw_b, conv_ln_gain=m_conv_ln_gain, conv_ln_bias=m_conv_ln_bias, w_conv_out=m_w_conv_out, b_conv_out=m_b_conv_out, w_out=m_w_out, norm_mlp_gain=m_norm_mlp_gain, w_mlp_up=m_w_mlp_up, w_mlp_down=m_w_mlp_down, final_norm_gain=m_final_norm_gain)
    mom_v = dict(meta_tokens=v_meta_tokens, norm_mix_gain=v_norm_mix_gain, w_in=v_w_in, b_forget=v_b_forget, w_attn_out=v_w_attn_out, b_glu=v_b_glu, conv_dw_w=v_conv_dw_w, conv_dw_b=v_conv_dw_b, conv_ln_gain=v_conv_ln_gain, conv_ln_bias=v_conv_ln_bias, w_conv_out=v_w_conv_out, b_conv_out=v_b_conv_out, w_out=v_w_out, norm_mlp_gain=v_norm_mlp_gain, w_mlp_up=v_w_mlp_up, w_mlp_down=v_w_mlp_down, final_norm_gain=v_final_norm_gain)
    names = list(weights)
    batch, seq, d = x.shape
    t = seq + N_META
    n = batch * t
    n_pairs = d // LANES
    assert t % ROW_TILE == 0 and d == SEG

    to_rows = lambda w3: jnp.transpose(w3[0])
    w_in_t, m_in_t, v_in_t = to_rows(w_in), to_rows(m_w_in), to_rows(v_w_in)
    first = [w_in_t.astype(BF16), meta_tokens, conv_dw_w[0]]
    rest = [w_[0].astype(BF16) for w_ in (w_attn_out, w_conv_out, w_out, w_mlp_up, w_mlp_down)]
    tgt = jnp.concatenate([jnp.zeros((batch, N_META, d), F32), loss_target], axis=1).reshape(n, d)
    h0_rows = jnp.pad(x, ((0, 0), (N_META, 0), (0, 0)))
    g1 = norm_mix_gain.reshape(1, -1)
    hn1_rows = _rms_fwd("rms1", h0_rows.reshape(n, d), g1)
    gather_a = _exchange_start("gather_in_start", [(f_, False) for f_ in first], ks=CHIP_PEERS)
    level_1 = _exchange_wait("gather_in_wait", gather_a, [gather_a["token"], tgt, hn1_rows, w_in_t, m_in_t, v_in_t] + rest)
    passed = _pass_on_start("gather_in_pass_start", level_1)
    w_in_g, meta_g, w_dw_g = _pass_on_wait("gather_in_pass_wait", passed, passed["token"])
    gather_b = _exchange_start("gather_rest_start", [(r_, False) for r_ in rest])
    n_fg = b_forget.shape[1]
    shard_w = w_in.shape[2]
    wt = w_in_g.reshape(N_DEV * shard_w, d)
    o_fg = 3 * SEG
    seg_rows = [0, SEG, 2 * SEG] + [o_fg + n_fg + i * SEG for i in range(4)]
    d_ff = w_mlp_down.shape[1] * N_DEV
    ff_blk = d_ff // N_DEV
    meta_f = _cols_from_shards(meta_g)
    w_dw = _cols_from_shards(w_dw_g)

    row2 = lambda v: v.reshape(1, -1)
    g2, g3 = row2(norm_mlp_gain), row2(final_norm_gain)
    b_fg = jnp.pad(b_forget, ((0, 0), (0, FG_PAD - n_fg)))
    h0 = lax.dynamic_update_slice(h0_rows, jnp.broadcast_to(meta_f[None], (batch, N_META, d)), (0, 0, 0)).reshape(n, d)

    meta_n = jnp.broadcast_to(_rms_fwd("rms1_meta", meta_f, g1)[None], (batch, N_META, d))
    hn1 = lax.dynamic_update_slice(hn1_rows.reshape(batch, t, d), meta_n, (0, 0, 0)).reshape(n, d)
    proj = lambda name, off, width, tn, dt: _mm_nt(name, [(hn1, _a_rows(d), wt, _wt_rows(tn, off))], n, width, tn, dt,
                                                   after=gather_b["token"])
    qkv = proj("proj_qkv", 0, 3 * SEG, SEG, BF16)
    glu = proj("proj_glu", seg_rows[3], 2 * SEG, SEG, BF16)
    gates = proj("proj_gates", seg_rows[5], 2 * SEG, SEG, BF16)
    fg = proj("proj_fg", o_fg, FG_PAD, FG_PAD, F32)

    cum = _fox_prep_fwd("fox_cumsum", fg, b_fg, batch)
    cumr = _key_sums_to_blocks(cum, batch, t, n_pairs)
    o, lse = _attn_fwd("attn_fwd", qkv, cumr, batch)
    rest = _exchange_wait("gather_rest_wait", gather_b, o)
    w_ao, w_co, w_o = [r_.reshape(d, d) for r_ in rest[:3]]
    w_up = rest[3]
    w_dn = rest[4].reshape(d_ff, d)
    a = _mm_nn("attn_out", o, w_ao, _w_cols(d, d, 0), d, d, BF16)

    c1 = _glu_conv_fwd("glu_conv", glu, b_glu, w_dw, conv_dw_b, batch)
    c3 = _ln_silu_fwd("ln_silu", c1, conv_ln_gain, conv_ln_bias)
    c = _mm_nn("conv_out", c3, w_co, _w_cols(d, d, 0), d, d, BF16)

    mrg = _merge_fwd("merge", gates, a, c, b_conv_out)
    mo = _mm_nn("mix_out", mrg, w_o, _w_cols(d, d, 0), d, d, F32)
    h1, hn2 = _rms_fwd("resid_rms2", h0, g2, res=mo)
    per = ff_blk // 512
    act = _mm_nn("mlp_up", hn2, w_up, pl.BlockSpec((None, d, 512), lambda i, j: (j // per, 0, j % per)),
                 d_ff, 512, BF16, relu2=True)
    dn = _mm_nn("mlp_down", act, w_dn, _w_cols(d_ff, d, 0), d, d, F32, tm=ROW_TILE)
    dh2, dh2b, loss_blk, dg3 = _final("final_loss", h1, dn, tgt, g3, batch)

    dup = _mm_nt("d_mlp_down", [(dh2b, _a_rows(d), w_dn, _w_rows(d, d))], n, d_ff, d, BF16, relu_bwd_of=act)
    dw_dn = _grad_w("gw_mlp_down", act, dh2b)
    dhn2 = _mm_nt("d_mlp_up", [(dup, _a_rows(ff_blk, g), w_up, pl.BlockSpec((None, 512, ff_blk), lambda i, j, g=g: (g, j, 0)))
                               for g in range(N_DEV)], n, d, 512, BF16)
    dw_up = _mm_tn("gw_mlp_up", hn2, lambda a_: 0, d, dup, lambda b_: b_, ff_blk, (N_DEV, d, ff_blk),
                   pl.BlockSpec((None, d, ff_blk), lambda a_, b_: (b_, 0, 0)), (1, N_DEV))
    scatter_1 = _exchange_start("scatter_mlp_start", [(dw_dn.reshape(N_DEV, ff_blk, d), True), (dw_up, True)])
    dh1, dg2, dh1b = _rms_bwd("rms2_bwd", dhn2, h1, g2 + scatter_1["token"][0:1, 0:1], dh2, batch, with_bf16=True)

    dm = _mm_nt("d_mix_out", [(dh1b, _a_rows(d), w_o, _w_rows(d, d))], n, d, d, BF16)
    dw_o = _grad_w("gw_mix_out", mrg, dh1b)
    da, dc, dga, dgc, dbco = _merge_bwd("merge_bwd", dm, gates, a, c, b_conv_out)

    do = _mm_nt("d_attn_out", [(da, _a_rows(d), w_ao, _w_rows(d, d))], n, d, d, BF16)
    dw_ao = _grad_w("gw_attn_out", o, da)
    dc3 = _mm_nt("d_conv_out", [(dc, _a_rows(d), w_co, _w_rows(d, d))], n, d, d, BF16)
    dw_co = _grad_w("gw_conv_out", c3, dc)

    scatter_2 = _exchange_start("scatter_mix_start", [(dw_.reshape(N_DEV, d // N_DEV, d), True)
                                                      for dw_ in (dw_o, dw_ao, dw_co)])
    dc1, dg_ln, db_ln = _ln_silu_bwd("ln_silu_bwd", dc3, c1, conv_ln_gain + scatter_2["token"][0:1, 0:1],
                                     conv_ln_bias)
    dglu_a, dglu_g, dw_dw, db_dw, dbg_a, dbg_g = _glu_conv_bwd("glu_conv_bwd", dc1, glu, b_glu, w_dw, batch)

    dq, dk, dv, dcumr, dcum_q = _attn_bwd("attn_bwd", qkv, o, do, lse, cumr, batch)
    dcum_k = jnp.pad(_key_sums_from_blocks(dcumr, batch, t, n_pairs), ((0, 0), (0, FG_PAD - 2 * n_pairs)))
    dfg, db_fg = _fox_prep_bwd("fox_cumsum_bwd", dcum_k, dcum_q, fg, b_fg, batch)

    segs = [dq, dk, dv, dglu_a, dglu_g, dga, dgc]
    gw_t = _grad_w_rows("gw_in", segs, hn1)
    gw_fg = _grad_w("gw_in_fg", dfg, hn1)[:n_fg]
    dw_in_t = jnp.concatenate([gw_t[:o_fg], gw_fg, gw_t[o_fg:]], axis=0).reshape(N_DEV, shard_w, d)
    scatter_3 = _exchange_start("scatter_in_start", [(dw_in_t, True)])
    pairs = [(s_, _a_rows(SEG, 0, ROW_TILE), wt, _wt_block(SEG, seg_rows[i], 512), "nn") for i, s_ in enumerate(segs)]
    pairs.append((dfg, _a_rows(FG_PAD, 0, ROW_TILE), wt, _wt_block(FG_PAD, o_fg, 512), "nn"))
    dhn1 = _mm_nt("d_proj_in", pairs, n, d, 512, BF16, tm=ROW_TILE, after=scatter_3["token"])
    dh0, dg1, dmeta = _rms_bwd("rms1_bwd", dhn1, h0, g1, dh1, batch, with_meta=True)
    grad_x = dh0.reshape(batch, t, d)[:, N_META:, :]

    small_g = dict(norm_mix_gain=dg1, b_forget=db_fg[:, :n_fg], b_glu=jnp.concatenate([dbg_a, dbg_g], axis=1),
                   conv_dw_b=db_dw, conv_ln_gain=dg_ln, conv_ln_bias=db_ln, b_conv_out=dbco, norm_mlp_gain=dg2,
                   final_norm_gain=dg3)
    scatter_4 = _exchange_start("scatter_small_start", [
        (_shards_from_cols(dmeta), True), (_shards_from_cols(dw_dw), True), (_cat_small(small_g), False),
        (loss_blk[0:1, :], False)])

    grads, deltas, new_m, new_v = {}, {}, {}, {}

    def update(k, parts):
        shp = weights[k].shape
        if k == "w_in":
            res_ = _adamw("adamw_" + k, parts, w_in_t, m_in_t, v_in_t)
            res_ = [jnp.transpose(r) for r in res_]
        else:
            w2 = lambda arr: arr.reshape(parts.shape[1:])
            res_ = _adamw("adamw_" + k, parts, w2(weights[k]), w2(mom_m[k]), w2(mom_v[k]))
        grads[k], deltas[k], new_m[k], new_v[k] = [r.reshape(shp) for r in res_]

    for k, parts in zip(("w_mlp_down", "w_mlp_up"), _exchange_wait("scatter_mlp_wait", scatter_1, scatter_4["token"])):
        update(k, parts)
    for k, parts in zip(("w_out", "w_attn_out", "w_conv_out"),
                        _exchange_wait("scatter_mix_wait", scatter_2, deltas["w_mlp_up"])):
        update(k, parts)
    update("w_in", _exchange_wait("scatter_in_wait", scatter_3, deltas["w_conv_out"])[0])
    reduced = _exchange_wait("scatter_small_wait", scatter_4, deltas["w_in"])
    loss = jnp.sum(reduced.pop()[:, 0, 0])
    for k, parts in zip(("meta_tokens", "conv_dw_w"), reduced[:-1]):
        update(k, parts)
    res = _adamw("adamw_small", reduced[-1], _cat_small(weights), _cat_small(mom_m), _cat_small(mom_v))
    shapes = {k: weights[k].shape for k in SMALL}
    for dst, r in zip((grads, deltas, new_m, new_v), res):
        dst.update(_split_small(r, shapes))

    return (loss, grad_x, *[grads[k] for k in names], *[deltas[k] for k in names],
            *[new_m[k] for k in names], *[new_v[k] for k in names])
```

```python
import functools

import jax
import jax.numpy as jnp
from jax import lax
from jax.experimental import pallas as pl
from jax.experimental.pallas import tpu as pltpu

F32, BF16 = jnp.float32, jnp.bfloat16
N_DEV = 8
N_META = 16
HEAD_DIM = 64
LANES = 128
CONV_W = 31
RMS_EPS = 1e-6
LN_EPS = 1e-5
ROW_TILE = 688
MM_TM = 2 * ROW_TILE
SEG = 1024
FG_PAD = 128
VMEM_LIMIT = 56 * 1024 * 1024
ADAM_LR, ADAM_B1, ADAM_B2, ADAM_EPS, ADAM_WD, ADAM_STEP = 0.001, 0.9, 0.999, 1e-08, 0.01, 10
NEG = -1e30
LOG2E = 1.4426950408889634

SMALL = ("norm_mix_gain", "b_forget", "b_glu", "conv_dw_b", "conv_ln_gain", "conv_ln_bias", "b_conv_out",
         "norm_mlp_gain", "final_norm_gain")
SMALL_W = {"norm_mix_gain": 1024, "b_forget": 128, "b_glu": 2048, "conv_dw_b": 1024, "conv_ln_gain": 1024,
           "conv_ln_bias": 1024, "b_conv_out": 1024, "norm_mlp_gain": 1024, "final_norm_gain": 1024}
SMALL_N = {"norm_mix_gain": 1024, "b_forget": 16, "b_glu": 2048, "conv_dw_b": 1024, "conv_ln_gain": 1024,
           "conv_ln_bias": 1024, "b_conv_out": 1024, "norm_mlp_gain": 1024, "final_norm_gain": 1024}


def _params(sem=None):
    return pltpu.CompilerParams(dimension_semantics=sem, vmem_limit_bytes=VMEM_LIMIT)


def _sigmoid(x):
    return 1.0 / (1.0 + jnp.exp(-x))


def _dot_nt(a, b):
    return lax.dot_general(a, b, (((1,), (1,)), ((), ())), preferred_element_type=F32)


def _dot_tn(a, b):
    return lax.dot_general(a, b, (((0,), (0,)), ((), ())), preferred_element_type=F32)


HBM_SPEC = pl.BlockSpec(memory_space=pltpu.HBM)
SEM_SPEC = pl.BlockSpec(memory_space=pltpu.SEMAPHORE)
DATAFLOW = pltpu.SideEffectType.DATAFLOW_SIDE_EFFECTING


def _device_index():
    return 4 * lax.axis_index("x") + 2 * lax.axis_index("y") + lax.axis_index("c")


def _peers():
    x, y, c = lax.axis_index("x"), lax.axis_index("y"), lax.axis_index("c")
    out = []
    for k in range(1, N_DEV):
        px = 1 - x if k & 4 else x
        py = 1 - y if k & 2 else y
        pc = 1 - c if k & 1 else c
        out.append((k, (px, py, pc), 4 * px + 2 * py + pc))
    return out


def _peer_copy(per_dest, src_ref, land_ref, send_sems, recv_sems, a, k, dev, peer):
    src = src_ref.at[peer] if per_dest else src_ref
    return pltpu.make_async_remote_copy(
        src_ref=src, dst_ref=land_ref.at[_device_index()], send_sem=send_sems.at[a * (N_DEV - 1) + k - 1],
        recv_sem=recv_sems.at[a * (N_DEV - 1) + k - 1], device_id=dev, device_id_type=pl.DeviceIdType.MESH)


ALL_PEERS = tuple(range(1, N_DEV))
CHIP_PEERS = (1, 2, 4, 6)
FAR_PEERS = (2, 4, 6)


def _own_copy(per_dest, src_ref, land_ref, send_sems, n, a):
    me = _device_index()
    return pltpu.make_async_copy(src_ref.at[me] if per_dest else src_ref, land_ref.at[me],
                                 send_sems.at[n * (N_DEV - 1) + a])


def _exchange_start(name, items, ks=ALL_PEERS):
    n = len(items)
    per_dest = [it[1] for it in items]

    def body(*refs):
        srcs, lands = refs[:n], refs[n:2 * n]
        send_sems, recv_sems, token = refs[2 * n], refs[2 * n + 1], refs[-1]
        for a in range(n):
            _own_copy(per_dest[a], srcs[a], lands[a], send_sems, n, a).start()
            for k, dev, peer in _peers():
                if k in ks:
                    _peer_copy(per_dest[a], srcs[a], lands[a], send_sems, recv_sems, a, k, dev, peer).start()
        token[...] = jnp.zeros(token.shape, F32)

    srcs = [pltpu.with_memory_space_constraint(it[0], pltpu.HBM) for it in items]
    lands = []
    for arr, pd in items:
        shp = arr.shape if pd else (N_DEV,) + arr.shape
        lands.append(pltpu.with_memory_space_constraint(lax.empty(shp, arr.dtype), pltpu.HBM))
    sems = pltpu.SemaphoreType.DMA((n * N_DEV,))
    res = pl.pallas_call(
        body, name=name,
        out_shape=(sems, sems, *[pltpu.HBM(a_.shape, a_.dtype) for a_ in srcs + lands],
                   jax.ShapeDtypeStruct((8, 128), F32)),
        in_specs=[HBM_SPEC] * (2 * n),
        out_specs=(SEM_SPEC, SEM_SPEC, *[HBM_SPEC] * (2 * n), pl.BlockSpec(memory_space=pltpu.VMEM)),
        input_output_aliases={i: 2 + i for i in range(2 * n)},
        compiler_params=pltpu.CompilerParams(has_side_effects=DATAFLOW),
    )(*srcs, *lands)
    return dict(per_dest=per_dest, ks=ks, send=res[0], recv=res[1], srcs=list(res[2:2 + n]),
                lands=list(res[2 + n:2 + 2 * n]), token=res[-1])


def _exchange_wait(name, started, after):
    per_dest = started["per_dest"]
    n = len(per_dest)

    def body(*refs):
        srcs, lands = refs[:n], refs[n:2 * n]
        send_sems, recv_sems = refs[2 * n], refs[2 * n + 1]
        for a in range(n):
            _own_copy(per_dest[a], srcs[a], lands[a], send_sems, n, a).wait()
            for k, dev, peer in _peers():
                if k in started["ks"]:
                    cp = _peer_copy(per_dest[a], srcs[a], lands[a], send_sems, recv_sems, a, k, dev, peer)
                    cp.wait_send()
                    cp.wait_recv()

    bufs = started["srcs"] + started["lands"]
    after = list(after) if isinstance(after, (list, tuple)) else [after]
    res = pl.pallas_call(
        body, name=name, out_shape=tuple(pltpu.HBM(b_.shape, b_.dtype) for b_ in bufs),
        in_specs=[HBM_SPEC] * (2 * n) + [SEM_SPEC, SEM_SPEC] + [pl.BlockSpec(memory_space=pl.ANY)] * len(after),
        out_specs=tuple([HBM_SPEC] * (2 * n)), input_output_aliases={i: i for i in range(2 * n)},
        compiler_params=pltpu.CompilerParams(has_side_effects=DATAFLOW),
    )(*bufs, started["send"], started["recv"], *after)
    return list(res[n:])


def _pass_on_copy(land_ref, send_sems, recv_sems, a, idx, slot):
    sibling = (lax.axis_index("x"), lax.axis_index("y"), 1 - lax.axis_index("c"))
    return pltpu.make_async_remote_copy(
        src_ref=land_ref.at[slot], dst_ref=land_ref.at[slot], send_sem=send_sems.at[a * len(FAR_PEERS) + idx],
        recv_sem=recv_sems.at[a * len(FAR_PEERS) + idx], device_id=sibling, device_id_type=pl.DeviceIdType.MESH)


def _pass_on_start(name, lands):
    n = len(lands)

    def body(*refs):
        send_sems, recv_sems, token = refs[n], refs[n + 1], refs[-1]
        slots = {k: peer for k, _, peer in _peers()}
        for a in range(n):
            for idx, k in enumerate(FAR_PEERS):
                _pass_on_copy(refs[a], send_sems, recv_sems, a, idx, slots[k]).start()
        token[...] = jnp.zeros(token.shape, F32)

    lands = [pltpu.with_memory_space_constraint(l_, pltpu.HBM) for l_ in lands]
    sems = pltpu.SemaphoreType.DMA((n * len(FAR_PEERS),))
    res = pl.pallas_call(
        body, name=name,
        out_shape=(sems, sems, *[pltpu.HBM(l_.shape, l_.dtype) for l_ in lands], jax.ShapeDtypeStruct((8, 128), F32)),
        in_specs=[HBM_SPEC] * n, out_specs=(SEM_SPEC, SEM_SPEC, *[HBM_SPEC] * n, pl.BlockSpec(memory_space=pltpu.VMEM)),
        input_output_aliases={i: 2 + i for i in range(n)},
        compiler_params=pltpu.CompilerParams(has_side_effects=DATAFLOW),
    )(*lands)
    return dict(send=res[0], recv=res[1], lands=list(res[2:2 + n]), token=res[-1])


def _pass_on_wait(name, passed, after):
    n = len(passed["lands"])

    def body(*refs):
        send_sems, recv_sems = refs[n], refs[n + 1]
        slots = {k: peer for k, _, peer in _peers()}
        for a in range(n):
            for idx, k in enumerate(FAR_PEERS):
                _pass_on_copy(refs[a], send_sems, recv_sems, a, idx, slots[k]).wait_send()
                _pass_on_copy(refs[a], send_sems, recv_sems, a, idx, slots[k ^ 1]).wait_recv()

    res = pl.pallas_call(
        body, name=name, out_shape=tuple(pltpu.HBM(l_.shape, l_.dtype) for l_ in passed["lands"]),
        in_specs=[HBM_SPEC] * n + [SEM_SPEC, SEM_SPEC, pl.BlockSpec(memory_space=pl.ANY)],
        out_specs=tuple([HBM_SPEC] * n), input_output_aliases={i: i for i in range(n)},
        compiler_params=pltpu.CompilerParams(has_side_effects=DATAFLOW),
    )(*passed["lands"], passed["send"], passed["recv"], after)
    return list(res)


def _mm_nn(name, x, w, w_spec, n_out, tn, out_dtype, relu2=False, tm=MM_TM):
    m, k = x.shape

    def body(x_ref, w_ref, *outs):
        acc = jnp.dot(x_ref[...], w_ref[...], preferred_element_type=F32)
        if relu2:
            acc = jnp.maximum(acc, 0.0)
            acc = acc * acc
        outs[0][...] = acc.astype(outs[0].dtype)

    o_spec = pl.BlockSpec((tm, tn), lambda i, j: (i, j))
    return pl.pallas_call(
        body, name=name, out_shape=jax.ShapeDtypeStruct((m, n_out), out_dtype), grid=(m // tm, n_out // tn),
        in_specs=[pl.BlockSpec((tm, k), lambda i, j: (i, 0)), w_spec],
        out_specs=o_spec, compiler_params=_params(("parallel", "parallel")),
    )(x, w)


def _mm_nt(name, pairs, m, n_out, tn, out_dtype, relu_bwd_of=None, tm=MM_TM, after=None):
    np_ = len(pairs)

    def body(*refs):
        acc = None
        for p in range(np_):
            if len(pairs[p]) == 5:
                d = jnp.dot(refs[2 * p][...], refs[2 * p + 1][...], preferred_element_type=F32)
            else:
                d = _dot_nt(refs[2 * p][...], refs[2 * p + 1][...])
            acc = d if acc is None else acc + d
        if relu_bwd_of is not None:
            acc = acc * (2.0 * jnp.sqrt(refs[2 * np_][...].astype(F32)))
        refs[-1][...] = acc.astype(refs[-1].dtype)

    o_spec = pl.BlockSpec((tm, tn), lambda i, j: (i, j))
    operands, specs = [], []
    for pair in pairs:
        operands += [pair[0], pair[2]]
        specs += [pair[1], pair[3]]
    if relu_bwd_of is not None:
        operands.append(relu_bwd_of)
        specs.append(o_spec)
    if after is not None:
        operands.append(after)
        specs.append(pl.BlockSpec((8, 128), lambda i, j: (0, 0)))
    return pl.pallas_call(
        body, name=name, out_shape=jax.ShapeDtypeStruct((m, n_out), out_dtype), grid=(m // tm, n_out // tn),
        in_specs=specs, out_specs=o_spec, compiler_params=_params(("parallel", "parallel")),
    )(*operands)


def _mm_tn(name, x, x_col, ta, dy, dy_col, tb, out_shape, out_spec, grid_ab):
    m = x.shape[0]

    def body(x_ref, dy_ref, o_ref):
        o_ref[...] = _dot_tn(x_ref[...], dy_ref[...]).astype(BF16)

    return pl.pallas_call(
        body, name=name, out_shape=jax.ShapeDtypeStruct(out_shape, BF16), grid=grid_ab,
        in_specs=[pl.BlockSpec((m, ta), lambda a, b: (0, x_col(a))),
                  pl.BlockSpec((m, tb), lambda a, b: (0, dy_col(b)))],
        out_specs=out_spec, compiler_params=_params(("parallel", "parallel")),
    )(x, dy)


def _w_cols(k, tn, off_blocks):
    return pl.BlockSpec((k, tn), lambda i, j: (0, off_blocks + j))


def _a_rows(kw, col_block=0, tm=MM_TM):
    return pl.BlockSpec((tm, kw), lambda i, j: (i, col_block))


def _w_rows(tn, kw, col_block=0):
    return pl.BlockSpec((tn, kw), lambda i, j: (j, col_block))


def _wt_rows(tn, off):
    return pl.BlockSpec((pl.Element(tn), pl.Element(SEG)), lambda i, j: (pl.multiple_of(off + tn * j, 16), 0))


def _wt_block(k, off, tn):
    return pl.BlockSpec((pl.Element(k), pl.Element(tn)), lambda i, j: (off, pl.multiple_of(tn * j, 128)))


GW_TILE = 512


def _grad_w_rows(name, xs, dy):
    m, nb = dy.shape
    per = [x_.shape[1] // GW_TILE for x_ in xs]
    steps = sum(per)
    tiles = [(k, h) for k in range(len(xs)) for h in range(per[k])]

    def body(*refs):
        x_refs, dy_ref, o_ref, buf, sems = refs[:len(xs)], refs[len(xs)], refs[len(xs) + 1], refs[-2], refs[-1]
        s_ = pl.program_id(0)

        def fetch(step, slot):
            for idx, (k, h) in enumerate(tiles):
                @pl.when(step == idx)
                def _():
                    pltpu.make_async_copy(x_refs[k].at[:, pl.ds(h * GW_TILE, GW_TILE)], buf.at[slot], sems.at[slot]).start()

        @pl.when(s_ == 0)
        def _():
            fetch(s_, 0)

        @pl.when(s_ + 1 < steps)
        def _():
            fetch(s_ + 1, (s_ + 1) % 2)

        slot = s_ % 2
        pltpu.make_async_copy(x_refs[0].at[:, pl.ds(0, GW_TILE)], buf.at[slot], sems.at[slot]).wait()
        o_ref[...] = _dot_tn(buf[slot], dy_ref[...]).astype(BF16)

    return pl.pallas_call(
        body, name=name, out_shape=jax.ShapeDtypeStruct((steps * GW_TILE, nb), BF16), grid=(steps,),
        in_specs=[pl.BlockSpec(memory_space=pl.ANY)] * len(xs) + [pl.BlockSpec((m, nb), lambda i: (0, 0))],
        out_specs=pl.BlockSpec((GW_TILE, nb), lambda i: (i, 0)),
        scratch_shapes=[pltpu.VMEM((2, m, GW_TILE), BF16), pltpu.SemaphoreType.DMA((2,))],
        compiler_params=_params(("arbitrary",)))(*xs, dy)


def _grad_w(name, x, dy):
    na, nb = x.shape[1], dy.shape[1]
    ta, tb = min(na, 1024), min(nb, 512)
    return _mm_tn(name, x, lambda a: a, ta, dy, lambda b: b, tb, (na, nb),
                  pl.BlockSpec((ta, tb), lambda a, b: (a, b)), (na // ta, nb // tb))


def _row_spec(width):
    return pl.BlockSpec((ROW_TILE, width), lambda i: (i, 0))


def _vec_spec(width):
    return pl.BlockSpec((1, width), lambda i: (0, 0))


def _rms_fwd(name, h, g, res=None):
    n, d = h.shape
    tile = ROW_TILE if n % ROW_TILE == 0 else n
    row_spec = pl.BlockSpec((tile, d), lambda i: (i, 0))

    def body(*refs):
        if res is None:
            h_ref, g_ref, hn_ref = refs
            hv = h_ref[...]
        else:
            h_ref, r_ref, g_ref, hs_ref, hn_ref = refs
            hv = h_ref[...] + r_ref[...]
            hs_ref[...] = hv
        r = lax.rsqrt(jnp.mean(hv * hv, axis=-1, keepdims=True) + RMS_EPS)
        hn_ref[...] = (hv * r * g_ref[...]).astype(BF16)

    ins = [h, g] if res is None else [h, res, g]
    in_specs = [row_spec, _vec_spec(d)] if res is None else [row_spec, row_spec, _vec_spec(d)]
    hn_shape = jax.ShapeDtypeStruct((n, d), BF16)
    if res is None:
        out_shape, out_specs = hn_shape, row_spec
    else:
        out_shape, out_specs = [jax.ShapeDtypeStruct((n, d), F32), hn_shape], [row_spec, row_spec]
    return pl.pallas_call(body, name=name, out_shape=out_shape, grid=(n // tile,), in_specs=in_specs,
                          out_specs=out_specs, compiler_params=_params(("parallel",)))(*ins)


def _rms_bwd(name, dhn, h, g, dres, batch, dh_dtype=F32, with_meta=False):
    n, d = h.shape
    t = n // batch
    nt = t // ROW_TILE

    def body(dhn_ref, h_ref, g_ref, dres_ref, *outs):
        first = (pl.program_id(0) == 0) & (pl.program_id(1) == 0)
        hv = h_ref[...]
        r = lax.rsqrt(jnp.mean(hv * hv, axis=-1, keepdims=True) + RMS_EPS)
        nrm = hv * r
        dhn = dhn_ref[...].astype(F32)
        dn = dhn * g_ref[...]
        dh = dres_ref[...].astype(F32) + r * (dn - nrm * jnp.mean(dn * nrm, axis=-1, keepdims=True))
        outs[0][...] = dh.astype(dh_dtype)
        dg_ref = outs[1]

        @pl.when(first)
        def _():
            dg_ref[...] = jnp.zeros(dg_ref.shape, F32)

        dg_ref[...] += jnp.sum(dhn * nrm, axis=0, keepdims=True)
        if with_meta:
            meta_ref = outs[2]

            @pl.when(first)
            def _():
                meta_ref[...] = jnp.zeros(meta_ref.shape, F32)

            @pl.when(pl.program_id(1) == 0)
            def _():
                meta_ref[...] += dh[0:N_META, :]

    row = pl.BlockSpec((ROW_TILE, d), lambda b, j: (b * nt + j, 0))
    vec = pl.BlockSpec((1, d), lambda b, j: (0, 0))
    shapes = [jax.ShapeDtypeStruct((n, d), dh_dtype), jax.ShapeDtypeStruct((1, d), F32)]
    specs = [row, vec]
    if with_meta:
        shapes.append(jax.ShapeDtypeStruct((N_META, d), F32))
        specs.append(pl.BlockSpec((N_META, d), lambda b, j: (0, 0)))
    return pl.pallas_call(body, name=name, out_shape=shapes, grid=(batch, nt), in_specs=[row, row, vec, row],
                          out_specs=specs, compiler_params=_params(("arbitrary", "arbitrary")))(dhn, h, g, dres)


def _final(name, h1, dn, tgt, g, batch):
    n, d = h1.shape
    t = n // batch
    nt = t // ROW_TILE

    def body(h1_ref, dn_ref, tgt_ref, g_ref, dhb_ref, loss_ref, dg_ref):
        first = (pl.program_id(0) == 0) & (pl.program_id(1) == 0)
        hv = h1_ref[...] + dn_ref[...]
        r = lax.rsqrt(jnp.mean(hv * hv, axis=-1, keepdims=True) + RMS_EPS)
        nrm = hv * r
        gv = g_ref[...]
        pos = pl.program_id(1) * ROW_TILE + lax.broadcasted_iota(jnp.int32, (ROW_TILE, 1), 0)
        diff = jnp.where(pos >= N_META, nrm * gv - tgt_ref[...], 0.0)
        dy = diff * (1.0 / d)

        @pl.when(first)
        def _():
            loss_ref[...] = jnp.zeros(loss_ref.shape, F32)
            dg_ref[...] = jnp.zeros(dg_ref.shape, F32)

        loss_ref[...] += jnp.full(loss_ref.shape, 0.5 / d, F32) * jnp.sum(diff * diff)
        dg_ref[...] += jnp.sum(dy * nrm, axis=0, keepdims=True)
        dng = dy * gv
        dh = r * (dng - nrm * jnp.mean(dng * nrm, axis=-1, keepdims=True))
        dhb_ref[...] = dh.astype(BF16)

    row = pl.BlockSpec((ROW_TILE, d), lambda b, j: (b * nt + j, 0))
    vec = pl.BlockSpec((1, d), lambda b, j: (0, 0))
    return pl.pallas_call(
        body, name=name, grid=(batch, nt), in_specs=[row, row, row, vec],
        out_shape=[jax.ShapeDtypeStruct((n, d), BF16), jax.ShapeDtypeStruct((8, 128), F32),
                   jax.ShapeDtypeStruct((1, d), F32)],
        out_specs=[row, pl.BlockSpec((8, 128), lambda b, j: (0, 0)), vec],
        compiler_params=_params(("arbitrary", "arbitrary")))(h1, dn, tgt, g)


def _ln_silu_fwd(name, c1, g, b):
    n, d = c1.shape

    def body(c_ref, g_ref, b_ref, o_ref):
        xv = c_ref[...]
        xc = xv - jnp.mean(xv, axis=-1, keepdims=True)
        rstd = lax.rsqrt(jnp.mean(xc * xc, axis=-1, keepdims=True) + LN_EPS)
        c2 = xc * rstd * g_ref[...] + b_ref[...]
        o_ref[...] = (c2 * _sigmoid(c2)).astype(BF16)

    return pl.pallas_call(body, name=name, out_shape=jax.ShapeDtypeStruct((n, d), BF16), grid=(n // ROW_TILE,),
                          in_specs=[_row_spec(d), _vec_spec(d), _vec_spec(d)], out_specs=_row_spec(d),
                          compiler_params=_params(("parallel",)))(c1, g, b)


def _ln_silu_bwd(name, dc3, c1, g, b):
    n, d = c1.shape

    def body(d_ref, c_ref, g_ref, b_ref, dc1_ref, dg_ref, db_ref):
        xv = c_ref[...]
        xc = xv - jnp.mean(xv, axis=-1, keepdims=True)
        rstd = lax.rsqrt(jnp.mean(xc * xc, axis=-1, keepdims=True) + LN_EPS)
        xh = xc * rstd
        c2 = xh * g_ref[...] + b_ref[...]
        s = _sigmoid(c2)
        dc2 = d_ref[...].astype(F32) * (s * (1.0 + c2 * (1.0 - s)))

        @pl.when(pl.program_id(0) == 0)
        def _():
            dg_ref[...] = jnp.zeros(dg_ref.shape, F32)
            db_ref[...] = jnp.zeros(db_ref.shape, F32)

        dg_ref[...] += jnp.sum(dc2 * xh, axis=0, keepdims=True)
        db_ref[...] += jnp.sum(dc2, axis=0, keepdims=True)
        dxh = dc2 * g_ref[...]
        dc1_ref[...] = rstd * (dxh - jnp.mean(dxh, axis=-1, keepdims=True)
                               - xh * jnp.mean(dxh * xh, axis=-1, keepdims=True))

    return pl.pallas_call(
        body, name=name, grid=(n // ROW_TILE,),
        out_shape=[jax.ShapeDtypeStruct((n, d), F32), jax.ShapeDtypeStruct((1, d), F32),
                   jax.ShapeDtypeStruct((1, d), F32)],
        in_specs=[_row_spec(d), _row_spec(d), _vec_spec(d), _vec_spec(d)],
        out_specs=[_row_spec(d), _vec_spec(d), _vec_spec(d)],
        compiler_params=_params(("arbitrary",)))(dc3, c1, g, b)


MERGE_TC = 512


def _merge_fwd(name, gates, a, c, b_co):
    n, d = a.shape
    nc = d // MERGE_TC

    def body(ga_ref, gc_ref, a_ref, c_ref, b_ref, m_ref):
        f32 = lambda r_: r_[...].astype(F32)
        m = _sigmoid(f32(ga_ref)) * f32(a_ref) + _sigmoid(f32(gc_ref)) * (f32(c_ref) + b_ref[...])
        m_ref[...] = m.astype(BF16)

    blk = lambda off: pl.BlockSpec((ROW_TILE, MERGE_TC), lambda i, j: (i, off + j))
    return pl.pallas_call(
        body, name=name, out_shape=jax.ShapeDtypeStruct((n, d), BF16), grid=(n // ROW_TILE, nc),
        in_specs=[blk(0), blk(nc), blk(0), blk(0), pl.BlockSpec((1, MERGE_TC), lambda i, j: (0, j))],
        out_specs=blk(0), compiler_params=_params(("parallel", "parallel")))(gates, gates, a, c, b_co)


def _merge_bwd(name, dm, gates, a, c, b_co):
    n, d = a.shape
    nc = d // MERGE_TC

    def body(dm_ref, ga_ref, gc_ref, a_ref, c_ref, b_ref, da_ref, dc_ref, dga_ref, dgc_ref, dbco_ref):
        f32 = lambda r_: r_[...].astype(F32)
        dmv = f32(dm_ref)
        sa, sc = _sigmoid(f32(ga_ref)), _sigmoid(f32(gc_ref))
        dc = dmv * sc
        da_ref[...] = (dmv * sa).astype(BF16)
        dc_ref[...] = dc.astype(BF16)
        dga_ref[...] = (dmv * f32(a_ref) * sa * (1.0 - sa)).astype(BF16)
        dgc_ref[...] = (dmv * (f32(c_ref) + b_ref[...]) * sc * (1.0 - sc)).astype(BF16)

        @pl.when(pl.program_id(1) == 0)
        def _():
            dbco_ref[...] = jnp.zeros(dbco_ref.shape, F32)

        dbco_ref[...] += jnp.sum(dc, axis=0, keepdims=True)

    blk = lambda off: pl.BlockSpec((ROW_TILE, MERGE_TC), lambda j, i: (i, off + j))
    vec = pl.BlockSpec((1, MERGE_TC), lambda j, i: (0, j))
    act = jax.ShapeDtypeStruct((n, d), BF16)
    return pl.pallas_call(
        body, name=name, grid=(nc, n // ROW_TILE),
        out_shape=[act, act, act, act, jax.ShapeDtypeStruct((1, d), F32)],
        in_specs=[blk(0), blk(0), blk(nc), blk(0), blk(0), vec],
        out_specs=[blk(0), blk(0), blk(0), blk(0), vec],
        compiler_params=_params(("parallel", "arbitrary")))(dm, gates, gates, a, c, b_co)


CONV_TC = 128
CONV_HALO = 32


def _conv_chunk(t):
    return 48 if t % 48 == 0 else 32 if t % 32 == 0 else 16


def _fold8(x):
    out = x[0:8]
    for k in range(1, x.shape[0] // 8):
        out = out + x[8 * k:8 * k + 8]
    return out


def _glu_conv_fwd(name, glu, b_glu, w_dw, b_dw, batch):
    n, c2 = glu.shape
    c = c2 // 2
    t = n // batch
    nc = c // CONV_TC

    def body(a_ref, gt_ref, ba_ref, bg_ref, w_ref, bdw_ref, o_ref, pad_ref):
        u = (a_ref[...].astype(F32) + ba_ref[...]) * _sigmoid(gt_ref[...].astype(F32) + bg_ref[...])
        pad_ref[0:CONV_HALO, :] = jnp.zeros((CONV_HALO, CONV_TC), F32)
        pad_ref[CONV_HALO:CONV_HALO + t, :] = u
        ch = _conv_chunk(t)
        for r0 in range(0, t, ch):
            acc = jnp.zeros((ch, CONV_TC), F32) + bdw_ref[...]
            for j in range(CONV_W):
                off = r0 + CONV_HALO - (CONV_W - 1) + j
                acc = acc + w_ref[j:j + 1, :] * pad_ref[off:off + ch, :]
            o_ref[r0:r0 + ch, :] = acc

    seq = lambda off: pl.BlockSpec((t, CONV_TC), lambda b, j: (b, off + j))
    vec = lambda off: pl.BlockSpec((1, CONV_TC), lambda b, j: (0, off + j))
    return pl.pallas_call(
        body, name=name, out_shape=jax.ShapeDtypeStruct((n, c), F32), grid=(batch, nc),
        in_specs=[seq(0), seq(nc), vec(0), vec(nc), pl.BlockSpec((CONV_W, CONV_TC), lambda b, j: (0, j)), vec(0)],
        out_specs=seq(0), scratch_shapes=[pltpu.VMEM((t + CONV_HALO, CONV_TC), F32)],
        compiler_params=_params(("parallel", "parallel")))(glu, glu, b_glu, b_glu, w_dw, b_dw)


def _glu_conv_bwd(name, dc1, glu, b_glu, w_dw, batch):
    n, c2 = glu.shape
    c = c2 // 2
    t = n // batch
    nc = c // CONV_TC

    def body(d_ref, a_ref, gt_ref, ba_ref, bg_ref, w_ref, dga_ref, dgg_ref, dw_ref, dbdw_ref, dba_ref, dbg_ref,
             padu_ref, padd_ref):
        av = a_ref[...].astype(F32) + ba_ref[...]
        sg = _sigmoid(gt_ref[...].astype(F32) + bg_ref[...])
        dc = d_ref[...]
        padu_ref[0:CONV_HALO, :] = jnp.zeros((CONV_HALO, CONV_TC), F32)
        padu_ref[CONV_HALO:CONV_HALO + t, :] = av * sg
        padd_ref[0:t, :] = dc
        padd_ref[t:t + CONV_HALO, :] = jnp.zeros((CONV_HALO, CONV_TC), F32)

        @pl.when(pl.program_id(1) == 0)
        def _():
            dw_ref[...] = jnp.zeros(dw_ref.shape, F32)
            dbdw_ref[...] = jnp.zeros(dbdw_ref.shape, F32)
            dba_ref[...] = jnp.zeros(dba_ref.shape, F32)
            dbg_ref[...] = jnp.zeros(dbg_ref.shape, F32)

        ch = _conv_chunk(t)
        zero8 = jnp.zeros((8, CONV_TC), F32)
        dw_acc = [zero8] * CONV_W
        sum_dc, sum_a, sum_g = zero8, zero8, zero8
        for r0 in range(0, t, ch):
            dcc = d_ref[r0:r0 + ch, :]
            du = jnp.zeros((ch, CONV_TC), F32)
            for j in range(CONV_W):
                back = r0 + CONV_W - 1 - j
                du = du + w_ref[j:j + 1, :] * padd_ref[back:back + ch, :]
                off = r0 + CONV_HALO - (CONV_W - 1) + j
                dw_acc[j] = dw_acc[j] + _fold8(dcc * padu_ref[off:off + ch, :])
            sgc = _sigmoid(gt_ref[r0:r0 + ch, :].astype(F32) + bg_ref[...])
            dga = du * sgc
            dgg = du * padu_ref[CONV_HALO + r0:CONV_HALO + r0 + ch, :] * (1.0 - sgc)
            dga_ref[r0:r0 + ch, :] = dga.astype(BF16)
            dgg_ref[r0:r0 + ch, :] = dgg.astype(BF16)
            sum_dc, sum_a, sum_g = sum_dc + _fold8(dcc), sum_a + _fold8(dga), sum_g + _fold8(dgg)
        for j in range(CONV_W):
            dw_ref[j:j + 1, :] += jnp.sum(dw_acc[j], axis=0, keepdims=True)
        dbdw_ref[...] += jnp.sum(sum_dc, axis=0, keepdims=True)
        dba_ref[...] += jnp.sum(sum_a, axis=0, keepdims=True)
        dbg_ref[...] += jnp.sum(sum_g, axis=0, keepdims=True)

    seq = lambda off: pl.BlockSpec((t, CONV_TC), lambda j, b: (b, off + j))
    vec = lambda off: pl.BlockSpec((1, CONV_TC), lambda j, b: (0, off + j))
    wsp = pl.BlockSpec((CONV_W, CONV_TC), lambda j, b: (0, j))
    act = jax.ShapeDtypeStruct((n, c), BF16)
    v = jax.ShapeDtypeStruct((1, c), F32)
    return pl.pallas_call(
        body, name=name, grid=(nc, batch),
        out_shape=[act, act, jax.ShapeDtypeStruct((CONV_W, c), F32), v, v, v],
        in_specs=[seq(0), seq(0), seq(nc), vec(0), vec(nc), wsp],
        out_specs=[seq(0), seq(0), wsp, vec(0), vec(0), vec(0)],
        scratch_shapes=[pltpu.VMEM((t + CONV_HALO, CONV_TC), F32), pltpu.VMEM((t + CONV_HALO, CONV_TC), F32)],
        compiler_params=_params(("parallel", "arbitrary")))(dc1, glu, glu, b_glu, b_glu, w_dw)


def _split3(x):
    hi = x.astype(BF16)
    r = x - hi.astype(F32)
    mid = r.astype(BF16)
    lo = (r - mid.astype(F32)).astype(BF16)
    return hi, mid, lo


def _tri_matmul(tri, x):
    hi, mid, lo = _split3(x)
    dot = lambda v: jnp.dot(tri, v, preferred_element_type=F32)
    return dot(hi) + dot(mid) + dot(lo)


def _fox_prep_fwd(name, fg, b_fg, batch):
    n, w = fg.shape
    t = n // batch
    nq = t // ROW_TILE

    def body(fg_ref, b_ref, cum_ref):
        row = lax.broadcasted_iota(jnp.int32, (ROW_TILE, ROW_TILE), 0)
        col = lax.broadcasted_iota(jnp.int32, (ROW_TILE, ROW_TILE), 1)
        tri = (row >= col).astype(BF16)
        for k in range(nq):
            rows = slice(k * ROW_TILE, (k + 1) * ROW_TILE)
            z = fg_ref[rows, :] + b_ref[...]
            logf = jnp.minimum(z, 0.0) - jnp.log(1.0 + jnp.exp(-jnp.abs(z)))
            cum = _tri_matmul(tri, logf)
            if k > 0:
                cum = cum + cum_ref[k * ROW_TILE - 1:k * ROW_TILE, :]
            cum_ref[rows, :] = cum

    seq = pl.BlockSpec((t, w), lambda b: (b, 0))
    return pl.pallas_call(body, name=name, out_shape=jax.ShapeDtypeStruct((n, w), F32), grid=(batch,),
                          in_specs=[seq, pl.BlockSpec((1, w), lambda b: (0, 0))], out_specs=seq,
                          compiler_params=_params(("parallel",)))(fg, b_fg)


def _fox_prep_bwd(name, dcum_k, dcum_q, fg, b_fg, batch):
    n, w = fg.shape
    t = n // batch
    nq = t // ROW_TILE

    def body(dk_ref, dq_ref, fg_ref, b_ref, dfg_ref, db_ref, rev_ref):
        row = lax.broadcasted_iota(jnp.int32, (ROW_TILE, ROW_TILE), 0)
        col = lax.broadcasted_iota(jnp.int32, (ROW_TILE, ROW_TILE), 1)
        tri = (col >= row).astype(BF16)

        @pl.when(pl.program_id(0) == 0)
        def _():
            db_ref[...] = jnp.zeros(db_ref.shape, F32)

        for k in reversed(range(nq)):
            rows = slice(k * ROW_TILE, (k + 1) * ROW_TILE)
            dlog = _tri_matmul(tri, dk_ref[rows, :] + dq_ref[rows, :])
            if k < nq - 1:
                dlog = dlog + rev_ref[(k + 1) * ROW_TILE:(k + 1) * ROW_TILE + 1, :]
            rev_ref[rows, :] = dlog
            dfg = dlog * _sigmoid(-(fg_ref[rows, :] + b_ref[...]))
            dfg_ref[rows, :] = dfg.astype(BF16)
            db_ref[...] += jnp.sum(dfg, axis=0, keepdims=True)

    seq = pl.BlockSpec((t, w), lambda b: (b, 0))
    vec = pl.BlockSpec((1, w), lambda b: (0, 0))
    return pl.pallas_call(
        body, name=name, grid=(batch,),
        out_shape=[jax.ShapeDtypeStruct((n, w), BF16), jax.ShapeDtypeStruct((1, w), F32)],
        in_specs=[seq, seq, seq, vec], out_specs=[seq, vec], scratch_shapes=[pltpu.VMEM((t, w), F32)],
        compiler_params=_params(("arbitrary",)))(dcum_k, dcum_q, fg, b_fg)


def _head_masks(x):
    lane = lax.broadcasted_iota(jnp.int32, x.shape, 1)
    zero = jnp.zeros(x.shape, x.dtype)
    return jnp.where(lane < HEAD_DIM, x, zero), jnp.where(lane >= HEAD_DIM, x, zero)


ATTN_BLOCK = 512


def _attn_blocks(t):
    nb = max(t // ATTN_BLOCK, 1)
    blocks = [(i * ATTN_BLOCK, ATTN_BLOCK) for i in range(nb - 1)]
    return blocks + [((nb - 1) * ATTN_BLOCK, t - (nb - 1) * ATTN_BLOCK)]


def _attn_specs(t):
    blocks = _attn_blocks(t)
    width = max(sz for _, sz in blocks)
    qkv = lambda off: pl.BlockSpec((t, LANES), lambda b, h: (b, off + h))
    cumr = pl.BlockSpec((None, None, len(blocks), 8, width), lambda b, h: (b, h, 0, 0, 0))
    return qkv, cumr


def _key_sums_to_blocks(cum, batch, t, n_pairs):
    blocks = _attn_blocks(t)
    width = max(sz for _, sz in blocks)
    cum_h = cum.reshape(batch, t, -1)[:, :, :2 * n_pairs].reshape(batch, t, n_pairs, 2)
    rows = [jnp.pad(jnp.transpose(cum_h[:, s0:s0 + sz], (0, 2, 3, 1)), ((0, 0), (0, 0), (0, 6), (0, width - sz)))
            for s0, sz in blocks]
    return jnp.stack(rows, axis=2)


def _key_sums_from_blocks(dcumr, batch, t, n_pairs):
    cols = [jnp.transpose(dcumr[:, :, j, :2, :sz], (0, 3, 1, 2)) for j, (_, sz) in enumerate(_attn_blocks(t))]
    return jnp.concatenate(cols, axis=1).reshape(batch * t, 2 * n_pairs)


def _causal(size):
    row = lax.broadcasted_iota(jnp.int32, (size, size), 0)
    col = lax.broadcasted_iota(jnp.int32, (size, size), 1)
    return row >= col


def _attn_fwd(name, qkv, cumr, batch):
    n, w3 = qkv.shape
    w = w3 // 3
    t = n // batch
    n_pairs = w // LANES
    blocks = _attn_blocks(t)

    def body(q_ref, k_ref, v_ref, cr_ref, o_ref, lse_ref):
        pair = pl.program_id(1)

        @pl.when(pair == 0)
        def _():
            lse_ref[...] = jnp.zeros(lse_ref.shape, F32)

        for i, (q0, qn) in enumerate(blocks):
            rows = slice(q0, q0 + qn)
            qs = _head_masks(q_ref[rows, :] * (0.125 * LOG2E))
            outs, lses = [], []
            for hh in range(2):
                m = jnp.full((qn, 1), NEG, F32)
                l = jnp.zeros((qn, 1), F32)
                acc = jnp.zeros((qn, LANES), F32)
                for j in range(i + 1):
                    k0, kn = blocks[j]
                    cols = slice(k0, k0 + kn)
                    s = _dot_nt(qs[hh], k_ref[cols, :]) - cr_ref[j, hh:hh + 1, 0:kn] * LOG2E
                    if j == i:
                        s = jnp.where(_causal(qn), s, NEG)
                    m_new = jnp.maximum(m, jnp.max(s, axis=1, keepdims=True))
                    alpha = jnp.exp2(m - m_new)
                    p = jnp.exp2(s - m_new)
                    l = alpha * l + jnp.sum(p, axis=1, keepdims=True)
                    acc = alpha * acc + jnp.dot(p.astype(BF16), v_ref[cols, :], preferred_element_type=F32)
                    m = m_new
                outs.append(acc / l)
                lses.append(m + jnp.log2(l))
            lane = lax.broadcasted_iota(jnp.int32, (qn, LANES), 1)
            o_ref[rows, :] = jnp.where(lane < HEAD_DIM, outs[0], outs[1]).astype(BF16)
            lse_ref[rows, :] = jnp.where(lane == 2 * pair, lses[0],
                                         jnp.where(lane == 2 * pair + 1, lses[1], lse_ref[rows, :]))

    qkv_spec, cumr_spec = _attn_specs(t)
    return pl.pallas_call(
        body, name=name, grid=(batch, n_pairs),
        out_shape=[jax.ShapeDtypeStruct((n, w), BF16), jax.ShapeDtypeStruct((n, LANES), F32)],
        in_specs=[qkv_spec(0), qkv_spec(n_pairs), qkv_spec(2 * n_pairs), cumr_spec],
        out_specs=[qkv_spec(0), pl.BlockSpec((t, LANES), lambda b, h: (b, 0))],
        compiler_params=_params(("parallel", "arbitrary")))(qkv, qkv, qkv, cumr)


def _attn_bwd(name, qkv, o, do, lse, cumr, batch):
    n, w3 = qkv.shape
    w = w3 // 3
    t = n // batch
    n_pairs = w // LANES
    blocks = _attn_blocks(t)

    def body(q_ref, k_ref, v_ref, o_ref, do_ref, lse_ref, cr_ref, dq_ref, dk_ref, dv_ref, dcr_ref, dcq_ref,
             dk_acc, dv_acc):
        pair = pl.program_id(1)
        dk_acc[...] = jnp.zeros(dk_acc.shape, F32)
        dv_acc[...] = jnp.zeros(dv_acc.shape, F32)
        dcr_ref[...] = jnp.zeros(dcr_ref.shape, F32)

        @pl.when(pair == 0)
        def _():
            dcq_ref[...] = jnp.zeros(dcq_ref.shape, F32)

        for i, (q0, qn) in enumerate(blocks):
            rows = slice(q0, q0 + qn)
            qs = _head_masks(q_ref[rows, :] * 0.125)
            q2 = _head_masks(q_ref[rows, :] * (0.125 * LOG2E))
            dos = _head_masks(do_ref[rows, :])
            dq = jnp.zeros((qn, LANES), F32)
            lane = lax.broadcasted_iota(jnp.int32, (qn, LANES), 1)
            dcq = []
            for hh in range(2):
                row_sum = jnp.zeros((qn, 1), F32)
                lse = jnp.sum(jnp.where(lane == 2 * pair + hh, lse_ref[rows, :], 0.0), axis=1, keepdims=True)
                delta = jnp.sum(dos[hh].astype(F32) * o_ref[rows, :].astype(F32), axis=1, keepdims=True)
                for j in range(i + 1):
                    k0, kn = blocks[j]
                    cols = slice(k0, k0 + kn)
                    s = _dot_nt(q2[hh], k_ref[cols, :]) - cr_ref[j, hh:hh + 1, 0:kn] * LOG2E
                    p = jnp.exp2(s - lse)
                    if j == i:
                        p = jnp.where(_causal(qn), p, 0.0)
                    dp = _dot_nt(dos[hh], v_ref[cols, :])
                    ds = p * (dp - delta)
                    pb, dsb = p.astype(BF16), ds.astype(BF16)
                    km = _head_masks(k_ref[cols, :])[hh]
                    dv_acc[j, :, 0:kn] += _dot_tn(dos[hh], pb)
                    dk_acc[j, :, 0:kn] += _dot_tn(qs[hh], dsb)
                    dq = dq + jnp.dot(dsb, km, preferred_element_type=F32)
                    dcr_ref[j, hh:hh + 1, 0:kn] -= jnp.sum(ds, axis=0, keepdims=True)
                    row_sum = row_sum + jnp.sum(ds, axis=1, keepdims=True)
                dcq.append(row_sum)
            dq_ref[rows, :] = (dq * 0.125).astype(BF16)
            dcq_ref[rows, :] = jnp.where(lane == 2 * pair, dcq[0],
                                         jnp.where(lane == 2 * pair + 1, dcq[1], dcq_ref[rows, :]))
        for j, (k0, kn) in enumerate(blocks):
            dk_ref[k0:k0 + kn, :] = dk_acc[j].T[0:kn, :].astype(BF16)
            dv_ref[k0:k0 + kn, :] = dv_acc[j].T[0:kn, :].astype(BF16)

    qkv_spec, cumr_spec = _attn_specs(t)
    wide = -(-max(sz for _, sz in blocks) // LANES) * LANES
    act = jax.ShapeDtypeStruct((n, w), BF16)
    return pl.pallas_call(
        body, name=name, grid=(batch, n_pairs),
        out_shape=[act, act, act, jax.ShapeDtypeStruct(cumr.shape, F32), jax.ShapeDtypeStruct((n, LANES), F32)],
        in_specs=[qkv_spec(0), qkv_spec(n_pairs), qkv_spec(2 * n_pairs), qkv_spec(0), qkv_spec(0),
                  pl.BlockSpec((t, LANES), lambda b, h: (b, 0)), cumr_spec],
        out_specs=[qkv_spec(0), qkv_spec(0), qkv_spec(0), cumr_spec, pl.BlockSpec((t, LANES), lambda b, h: (b, 0))],
        scratch_shapes=[pltpu.VMEM((len(blocks), LANES, wide), F32), pltpu.VMEM((len(blocks), LANES, wide), F32)],
        compiler_params=_params(("parallel", "arbitrary")))(qkv, qkv, qkv, o, do, lse, cumr)


def _adamw(name, parts, w, m, v):
    r, c = w.shape
    tr = 128 if r % 128 == 0 else r
    tc = 256 if tr > 128 and c % 256 == 0 else c
    c1 = 1.0 - ADAM_B1 ** ADAM_STEP
    c2 = 1.0 - ADAM_B2 ** ADAM_STEP

    def body(p_ref, w_ref, m_ref, v_ref, g_ref, d_ref, m2_ref, v2_ref):
        g = p_ref[0].astype(F32)
        for s in range(1, N_DEV):
            g = g + p_ref[s].astype(F32)
        m2 = ADAM_B1 * m_ref[...] + (1.0 - ADAM_B1) * g
        v2 = ADAM_B2 * v_ref[...] + (1.0 - ADAM_B2) * (g * g)
        g_ref[...] = g
        m2_ref[...] = m2
        v2_ref[...] = v2
        d_ref[...] = -ADAM_LR * ((m2 / c1) / (jnp.sqrt(v2 / c2) + ADAM_EPS) + ADAM_WD * w_ref[...])

    blk = pl.BlockSpec((tr, tc), lambda i, j: (i, j))
    shp = jax.ShapeDtypeStruct((r, c), F32)
    return pl.pallas_call(
        body, name=name, out_shape=[shp] * 4, grid=(r // tr, c // tc),
        in_specs=[pl.BlockSpec((N_DEV, tr, tc), lambda i, j: (0, i, j)), blk, blk, blk], out_specs=[blk] * 4,
        compiler_params=_params(("parallel", "parallel")))(parts, w, m, v)


def _cat_small(vals):
    parts = []
    for name in SMALL:
        v = vals[name].reshape(1, -1).astype(F32)
        parts.append(jnp.pad(v, ((0, 0), (0, SMALL_W[name] - v.shape[1]))))
    return jnp.concatenate(parts, axis=1)


def _split_small(row, shapes):
    out, off = {}, 0
    for name in SMALL:
        out[name] = row[0, off:off + SMALL_N[name]].reshape(shapes[name])
        off += SMALL_W[name]
    return out


def _cols_from_shards(g):
    return jnp.transpose(g, (1, 0, 2)).reshape(g.shape[1], N_DEV * g.shape[2])


def _shards_from_cols(a):
    r, c = a.shape
    return jnp.transpose(a.reshape(r, N_DEV, c // N_DEV), (1, 0, 2))


def kernel(x, meta_tokens, norm_mix_gain, w_in, b_forget, w_attn_out, b_glu, conv_dw_w, conv_dw_b, conv_ln_gain, conv_ln_bias, w_conv_out, b_conv_out, w_out, norm_mlp_gain, w_mlp_up, w_mlp_down, final_norm_gain, loss_target, m_meta_tokens, m_norm_mix_gain, m_w_in, m_b_forget, m_w_attn_out, m_b_glu, m_conv_dw_w, m_conv_dw_b, m_conv_ln_gain, m_conv_ln_bias, m_w_conv_out, m_b_conv_out, m_w_out, m_norm_mlp_gain, m_w_mlp_up, m_w_mlp_down, m_final_norm_gain, v_meta_tokens, v_norm_mix_gain, v_w_in, v_b_forget, v_w_attn_out, v_b_glu, v_conv_dw_w, v_conv_dw_b, v_conv_ln_gain, v_conv_ln_bias, v_w_conv_out, v_b_conv_out, v_w_out, v_norm_mlp_gain, v_w_mlp_up, v_w_mlp_down, v_final_norm_gain):
    weights = dict(meta_tokens=meta_tokens, norm_mix_gain=norm_mix_gain, w_in=w_in, b_forget=b_forget, w_attn_out=w_attn_out, b_glu=b_glu, conv_dw_w=conv_dw_w, conv_dw_b=conv_dw_b, conv_ln_gain=conv_ln_gain, conv_ln_bias=conv_ln_bias, w_conv_out=w_conv_out, b_conv_out=b_conv_out, w_out=w_out, norm_mlp_gain=norm_mlp_gain, w_mlp_up=w_mlp_up, w_mlp_down=w_mlp_down, final_norm_gain=final_norm_gain)
    mom_m = dict(meta_tokens=m_meta_tokens, norm_mix_gain=m_norm_mix_gain, w_in=m_w_in, b_forget=m_b_forget, w_attn_out=m_w_attn_out, b_glu=m_b_glu, conv_dw_w=m_conv_dw_w, conv_dw_b=m_conv_dw_b, conv_ln_gain=m_conv_ln_gain, conv_ln_bias=m_conv_ln_bias, w_conv_out=m_w_conv_out, b_conv_out=m_b_conv_out, w_out=m_w_out, norm_mlp_gain=m_norm_mlp_gain, w_mlp_up=m_w_mlp_up, w_mlp_down=m_w_mlp_down, final_norm_gain=m_final_norm_gain)
    mom_v = dict(meta_tokens=v_meta_tokens, norm_mix_gain=v_norm_mix_gain, w_in=v_w_in, b_forget=v_b_forget, w_attn_out=v_w_attn_out, b_glu=v_b_glu, conv_dw_w=v_conv_dw_w, conv_dw_b=v_conv_dw_b, conv_ln_gain=v_conv_ln_gain, conv_ln_bias=v_conv_ln_bias, w_conv_out=v_w_conv_out, b_conv_out=v_b_conv_out, w_out=v_w_out, norm_mlp_gain=v_norm_mlp_gain, w_mlp_up=v_w_mlp_up, w_mlp_down=v_w_mlp_down, final_norm_gain=v_final_norm_gain)
    names = list(weights)
    batch, seq, d = x.shape
    t = seq + N_META
    n = batch * t
    n_pairs = d // LANES
    assert t % ROW_TILE == 0 and d == SEG

    to_rows = lambda w3: jnp.transpose(w3[0])
    w_in_t, m_in_t, v_in_t = to_rows(w_in), to_rows(m_w_in), to_rows(v_w_in)
    first = [w_in_t.astype(BF16), meta_tokens, conv_dw_w[0]]
    rest = [w_[0].astype(BF16) for w_ in (w_attn_out, w_conv_out, w_out, w_mlp_up, w_mlp_down)]
    tgt = jnp.concatenate([jnp.zeros((batch, N_META, d), F32), loss_target], axis=1).reshape(n, d)
    h0_rows = jnp.pad(x, ((0, 0), (N_META, 0), (0, 0)))
    g1 = norm_mix_gain.reshape(1, -1)
    hn1_rows = _rms_fwd("rms1", h0_rows.reshape(n, d), g1)
    gather_a = _exchange_start("gather_in_start", [(f_, False) for f_ in first], ks=CHIP_PEERS)
    level_1 = _exchange_wait("gather_in_wait", gather_a, [gather_a["token"], tgt, hn1_rows, w_in_t, m_in_t, v_in_t] + rest)
    passed = _pass_on_start("gather_in_pass_start", level_1)
    w_in_g, meta_g, w_dw_g = _pass_on_wait("gather_in_pass_wait", passed, passed["token"])
    gather_b = _exchange_start("gather_rest_start", [(r_, False) for r_ in rest])
    n_fg = b_forget.shape[1]
    shard_w = w_in.shape[2]
    wt = w_in_g.reshape(N_DEV * shard_w, d)
    o_fg = 3 * SEG
    seg_rows = [0, SEG, 2 * SEG] + [o_fg + n_fg + i * SEG for i in range(4)]
    d_ff = w_mlp_down.shape[1] * N_DEV
    ff_blk = d_ff // N_DEV
    meta_f = _cols_from_shards(meta_g)
    w_dw = _cols_from_shards(w_dw_g)

    row2 = lambda v: v.reshape(1, -1)
    g2, g3 = row2(norm_mlp_gain), row2(final_norm_gain)
    b_fg = jnp.pad(b_forget, ((0, 0), (0, FG_PAD - n_fg)))
    h0 = lax.dynamic_update_slice(h0_rows, jnp.broadcast_to(meta_f[None], (batch, N_META, d)), (0, 0, 0)).reshape(n, d)

    meta_n = jnp.broadcast_to(_rms_fwd("rms1_meta", meta_f, g1)[None], (batch, N_META, d))
    hn1 = lax.dynamic_update_slice(hn1_rows.reshape(batch, t, d), meta_n, (0, 0, 0)).reshape(n, d)
    proj = lambda name, off, width, tn, dt: _mm_nt(name, [(hn1, _a_rows(d), wt, _wt_rows(tn, off))], n, width, tn, dt,
                                                   after=gather_b["token"])
    qkv = proj("proj_qkv", 0, 3 * SEG, SEG, BF16)
    glu = proj("proj_glu", seg_rows[3], 2 * SEG, SEG, BF16)
    gates = proj("proj_gates", seg_rows[5], 2 * SEG, SEG, BF16)
    fg = proj("proj_fg", o_fg, FG_PAD, FG_PAD, F32)

    cum = _fox_prep_fwd("fox_cumsum", fg, b_fg, batch)
    cumr = _key_sums_to_blocks(cum, batch, t, n_pairs)
    o, lse = _attn_fwd("attn_fwd", qkv, cumr, batch)
    rest = _exchange_wait("gather_rest_wait", gather_b, o)
    w_ao, w_co, w_o = [r_.reshape(d, d) for r_ in rest[:3]]
    w_up = rest[3]
    w_dn = rest[4].reshape(d_ff, d)
    a = _mm_nn("attn_out", o, w_ao, _w_cols(d, d, 0), d, d, BF16)

    c1 = _glu_conv_fwd("glu_conv", glu, b_glu, w_dw, conv_dw_b, batch)
    c3 = _ln_silu_fwd("ln_silu", c1, conv_ln_gain, conv_ln_bias)
    c = _mm_nn("conv_out", c3, w_co, _w_cols(d, d, 0), d, d, BF16)

    mrg = _merge_fwd("merge", gates, a, c, b_conv_out)
    mo = _mm_nn("mix_out", mrg, w_o, _w_cols(d, d, 0), d, d, F32)
    h1, hn2 = _rms_fwd("resid_rms2", h0, g2, res=mo)
    per = ff_blk // 512
    act = _mm_nn("mlp_up", hn2, w_up, pl.BlockSpec((None, d, 512), lambda i, j: (j // per, 0, j % per)),
                 d_ff, 512, BF16, relu2=True)
    dn = _mm_nn("mlp_down", act, w_dn, _w_cols(d_ff, d, 0), d, d, F32, tm=ROW_TILE)
    dh2b, loss_blk, dg3 = _final("final_loss", h1, dn, tgt, g3, batch)

    dup = _mm_nt("d_mlp_down", [(dh2b, _a_rows(d), w_dn, _w_rows(d, d))], n, d_ff, d, BF16, relu_bwd_of=act)
    dw_dn = _grad_w_rows("gw_mlp_down", [act], dh2b)
    dhn2 = _mm_nt("d_mlp_up", [(dup, _a_rows(ff_blk, g), w_up, pl.BlockSpec((None, 512, ff_blk), lambda i, j, g=g: (g, j, 0)))
                               for g in range(N_DEV)], n, d, 512, BF16)
    dw_up = _mm_tn("gw_mlp_up", hn2, lambda a_: 0, d, dup, lambda b_: b_, ff_blk, (N_DEV, d, ff_blk),
                   pl.BlockSpec((None, d, ff_blk), lambda a_, b_: (b_, 0, 0)), (1, N_DEV))
    scatter_1 = _exchange_start("scatter_mlp_start", [(dw_dn.reshape(N_DEV, ff_blk, d), True), (dw_up, True)])
    dh1b, dg2 = _rms_bwd("rms2_bwd", dhn2, h1, g2 + scatter_1["token"][0:1, 0:1], dh2b, batch, dh_dtype=BF16)

    dm = _mm_nt("d_mix_out", [(dh1b, _a_rows(d), w_o, _w_rows(d, d))], n, d, d, BF16)
    dw_o = _grad_w("gw_mix_out", mrg, dh1b)
    da, dc, dga, dgc, dbco = _merge_bwd("merge_bwd", dm, gates, a, c, b_conv_out)

    do = _mm_nt("d_attn_out", [(da, _a_rows(d), w_ao, _w_rows(d, d))], n, d, d, BF16)
    dw_ao = _grad_w("gw_attn_out", o, da)
    dc3 = _mm_nt("d_conv_out", [(dc, _a_rows(d), w_co, _w_rows(d, d))], n, d, d, BF16)
    dw_co = _grad_w("gw_conv_out", c3, dc)

    scatter_2 = _exchange_start("scatter_mix_start", [(dw_.reshape(N_DEV, d // N_DEV, d), True)
                                                      for dw_ in (dw_o, dw_ao, dw_co)])
    dc1, dg_ln, db_ln = _ln_silu_bwd("ln_silu_bwd", dc3, c1, conv_ln_gain + scatter_2["token"][0:1, 0:1],
                                     conv_ln_bias)
    dglu_a, dglu_g, dw_dw, db_dw, dbg_a, dbg_g = _glu_conv_bwd("glu_conv_bwd", dc1, glu, b_glu, w_dw, batch)

    dq, dk, dv, dcumr, dcum_q = _attn_bwd("attn_bwd", qkv, o, do, lse, cumr, batch)
    dcum_k = jnp.pad(_key_sums_from_blocks(dcumr, batch, t, n_pairs), ((0, 0), (0, FG_PAD - 2 * n_pairs)))
    dfg, db_fg = _fox_prep_bwd("fox_cumsum_bwd", dcum_k, dcum_q, fg, b_fg, batch)

    segs = [dq, dk, dv, dglu_a, dglu_g, dga, dgc]
    gw_t = _grad_w_rows("gw_in", segs, hn1)
    gw_fg = _grad_w("gw_in_fg", dfg, hn1)[:n_fg]
    dw_in_t = jnp.concatenate([gw_t[:o_fg], gw_fg, gw_t[o_fg:]], axis=0).reshape(N_DEV, shard_w, d)
    scatter_3 = _exchange_start("scatter_in_start", [(dw_in_t, True)])
    pairs = [(s_, _a_rows(SEG, 0, ROW_TILE), wt, _wt_block(SEG, seg_rows[i], 512), "nn") for i, s_ in enumerate(segs)]
    pairs.append((dfg, _a_rows(FG_PAD, 0, ROW_TILE), wt, _wt_block(FG_PAD, o_fg, 512), "nn"))
    dhn1 = _mm_nt("d_proj_in", pairs, n, d, 512, BF16, tm=ROW_TILE, after=scatter_3["token"])
    dh0, dg1, dmeta = _rms_bwd("rms1_bwd", dhn1, h0, g1, dh1b, batch, with_meta=True)
    grad_x = dh0.reshape(batch, t, d)[:, N_META:, :]

    small_g = dict(norm_mix_gain=dg1, b_forget=db_fg[:, :n_fg], b_glu=jnp.concatenate([dbg_a, dbg_g], axis=1),
                   conv_dw_b=db_dw, conv_ln_gain=dg_ln, conv_ln_bias=db_ln, b_conv_out=dbco, norm_mlp_gain=dg2,
                   final_norm_gain=dg3)
    scatter_4 = _exchange_start("scatter_small_start", [
        (_shards_from_cols(dmeta), True), (_shards_from_cols(dw_dw), True), (_cat_small(small_g), False),
        (loss_blk[0:1, :], False)])

    grads, deltas, new_m, new_v = {}, {}, {}, {}

    def update(k, parts):
        shp = weights[k].shape
        if k == "w_in":
            res_ = _adamw("adamw_" + k, parts, w_in_t, m_in_t, v_in_t)
            res_ = [jnp.transpose(r) for r in res_]
        else:
            w2 = lambda arr: arr.reshape(parts.shape[1:])
            res_ = _adamw("adamw_" + k, parts, w2(weights[k]), w2(mom_m[k]), w2(mom_v[k]))
        grads[k], deltas[k], new_m[k], new_v[k] = [r.reshape(shp) for r in res_]

    for k, parts in zip(("w_mlp_down", "w_mlp_up"), _exchange_wait("scatter_mlp_wait", scatter_1, scatter_4["token"])):
        update(k, parts)
    for k, parts in zip(("w_out", "w_attn_out", "w_conv_out"),
                        _exchange_wait("scatter_mix_wait", scatter_2, deltas["w_mlp_up"])):
        update(k, parts)
    update("w_in", _exchange_wait("scatter_in_wait", scatter_3, deltas["w_conv_out"])[0])
    reduced = _exchange_wait("scatter_small_wait", scatter_4, deltas["w_in"])
    loss = jnp.sum(reduced.pop()[:, 0, 0])
    for k, parts in zip(("meta_tokens", "conv_dw_w"), reduced[:-1]):
        update(k, parts)
    res = _adamw("adamw_small", reduced[-1], _cat_small(weights), _cat_small(mom_m), _cat_small(mom_v))
    shapes = {k: weights[k].shape for k in SMALL}
    for dst, r in zip((grads, deltas, new_m, new_v), res):
        dst.update(_split_small(r, shapes))

    return (loss, grad_x, *[grads[k] for k in names], *[deltas[k] for k in names],
            *[new_m[k] for k in names], *[new_v[k] for k in names])
```

```python
import functools

import jax
import jax.numpy as jnp
from jax import lax
from jax.experimental import pallas as pl
from jax.experimental.pallas import tpu as pltpu

F32, BF16 = jnp.float32, jnp.bfloat16
N_DEV = 8
N_META = 16
HEAD_DIM = 64
LANES = 128
CONV_W = 31
RMS_EPS = 1e-6
LN_EPS = 1e-5
ROW_TILE = 688
MM_TM = 2 * ROW_TILE
SEG = 1024
FG_PAD = 128
VMEM_LIMIT = 56 * 1024 * 1024
ADAM_LR, ADAM_B1, ADAM_B2, ADAM_EPS, ADAM_WD, ADAM_STEP = 0.001, 0.9, 0.999, 1e-08, 0.01, 10
NEG = -1e30
LOG2E = 1.4426950408889634

SMALL = ("norm_mix_gain", "b_forget", "b_glu", "conv_dw_b", "conv_ln_gain", "conv_ln_bias", "b_conv_out",
         "norm_mlp_gain", "final_norm_gain")
SMALL_W = {"norm_mix_gain": 1024, "b_forget": 128, "b_glu": 2048, "conv_dw_b": 1024, "conv_ln_gain": 1024,
           "conv_ln_bias": 1024, "b_conv_out": 1024, "norm_mlp_gain": 1024, "final_norm_gain": 1024}
SMALL_N = {"norm_mix_gain": 1024, "b_forget": 16, "b_glu": 2048, "conv_dw_b": 1024, "conv_ln_gain": 1024,
           "conv_ln_bias": 1024, "b_conv_out": 1024, "norm_mlp_gain": 1024, "final_norm_gain": 1024}


def _params(sem=None):
    return pltpu.CompilerParams(dimension_semantics=sem, vmem_limit_bytes=VMEM_LIMIT)


def _sigmoid(x):
    return 1.0 / (1.0 + jnp.exp(-x))


def _dot_nt(a, b):
    return lax.dot_general(a, b, (((1,), (1,)), ((), ())), preferred_element_type=F32)


def _dot_tn(a, b):
    return lax.dot_general(a, b, (((0,), (0,)), ((), ())), preferred_element_type=F32)


HBM_SPEC = pl.BlockSpec(memory_space=pltpu.HBM)
SEM_SPEC = pl.BlockSpec(memory_space=pltpu.SEMAPHORE)
DATAFLOW = pltpu.SideEffectType.DATAFLOW_SIDE_EFFECTING


def _device_index():
    return 4 * lax.axis_index("x") + 2 * lax.axis_index("y") + lax.axis_index("c")


def _peers():
    x, y, c = lax.axis_index("x"), lax.axis_index("y"), lax.axis_index("c")
    out = []
    for k in range(1, N_DEV):
        px = 1 - x if k & 4 else x
        py = 1 - y if k & 2 else y
        pc = 1 - c if k & 1 else c
        out.append((k, (px, py, pc), 4 * px + 2 * py + pc))
    return out


def _peer_copy(per_dest, src_ref, land_ref, send_sems, recv_sems, a, k, dev, peer):
    src = src_ref.at[peer] if per_dest else src_ref
    return pltpu.make_async_remote_copy(
        src_ref=src, dst_ref=land_ref.at[_device_index()], send_sem=send_sems.at[a * (N_DEV - 1) + k - 1],
        recv_sem=recv_sems.at[a * (N_DEV - 1) + k - 1], device_id=dev, device_id_type=pl.DeviceIdType.MESH)


ALL_PEERS = tuple(range(1, N_DEV))
CHIP_PEERS = (1, 2, 4, 6)
FAR_PEERS = (2, 4, 6)


def _own_copy(per_dest, src_ref, land_ref, send_sems, n, a):
    me = _device_index()
    return pltpu.make_async_copy(src_ref.at[me] if per_dest else src_ref, land_ref.at[me],
                                 send_sems.at[n * (N_DEV - 1) + a])


def _exchange_start(name, items, ks=ALL_PEERS):
    n = len(items)
    per_dest = [it[1] for it in items]

    def body(*refs):
        srcs, lands = refs[:n], refs[n:2 * n]
        send_sems, recv_sems, token = refs[2 * n], refs[2 * n + 1], refs[-1]
        for a in range(n):
            _own_copy(per_dest[a], srcs[a], lands[a], send_sems, n, a).start()
            for k, dev, peer in _peers():
                if k in ks:
                    _peer_copy(per_dest[a], srcs[a], lands[a], send_sems, recv_sems, a, k, dev, peer).start()
        token[...] = jnp.zeros(token.shape, F32)

    srcs = [pltpu.with_memory_space_constraint(it[0], pltpu.HBM) for it in items]
    lands = []
    for arr, pd in items:
        shp = arr.shape if pd else (N_DEV,) + arr.shape
        lands.append(pltpu.with_memory_space_constraint(lax.empty(shp, arr.dtype), pltpu.HBM))
    sems = pltpu.SemaphoreType.DMA((n * N_DEV,))
    res = pl.pallas_call(
        body, name=name,
        out_shape=(sems, sems, *[pltpu.HBM(a_.shape, a_.dtype) for a_ in srcs + lands],
                   jax.ShapeDtypeStruct((8, 128), F32)),
        in_specs=[HBM_SPEC] * (2 * n),
        out_specs=(SEM_SPEC, SEM_SPEC, *[HBM_SPEC] * (2 * n), pl.BlockSpec(memory_space=pltpu.VMEM)),
        input_output_aliases={i: 2 + i for i in range(2 * n)},
        compiler_params=pltpu.CompilerParams(has_side_effects=DATAFLOW),
    )(*srcs, *lands)
    return dict(per_dest=per_dest, ks=ks, send=res[0], recv=res[1], srcs=list(res[2:2 + n]),
                lands=list(res[2 + n:2 + 2 * n]), token=res[-1])


def _exchange_wait(name, started, after):
    per_dest = started["per_dest"]
    n = len(per_dest)

    def body(*refs):
        srcs, lands = refs[:n], refs[n:2 * n]
        send_sems, recv_sems = refs[2 * n], refs[2 * n + 1]
        for a in range(n):
            _own_copy(per_dest[a], srcs[a], lands[a], send_sems, n, a).wait()
            for k, dev, peer in _peers():
                if k in started["ks"]:
                    cp = _peer_copy(per_dest[a], srcs[a], lands[a], send_sems, recv_sems, a, k, dev, peer)
                    cp.wait_send()
                    cp.wait_recv()

    bufs = started["srcs"] + started["lands"]
    after = list(after) if isinstance(after, (list, tuple)) else [after]
    res = pl.pallas_call(
        body, name=name, out_shape=tuple(pltpu.HBM(b_.shape, b_.dtype) for b_ in bufs),
        in_specs=[HBM_SPEC] * (2 * n) + [SEM_SPEC, SEM_SPEC] + [pl.BlockSpec(memory_space=pl.ANY)] * len(after),
        out_specs=tuple([HBM_SPEC] * (2 * n)), input_output_aliases={i: i for i in range(2 * n)},
        compiler_params=pltpu.CompilerParams(has_side_effects=DATAFLOW),
    )(*bufs, started["send"], started["recv"], *after)
    return list(res[n:])


def _pass_on_copy(land_ref, send_sems, recv_sems, a, idx, slot):
    sibling = (lax.axis_index("x"), lax.axis_index("y"), 1 - lax.axis_index("c"))
    return pltpu.make_async_remote_copy(
        src_ref=land_ref.at[slot], dst_ref=land_ref.at[slot], send_sem=send_sems.at[a * len(FAR_PEERS) + idx],
        recv_sem=recv_sems.at[a * len(FAR_PEERS) + idx], device_id=sibling, device_id_type=pl.DeviceIdType.MESH)


def _pass_on_start(name, lands):
    n = len(lands)

    def body(*refs):
        send_sems, recv_sems, token = refs[n], refs[n + 1], refs[-1]
        slots = {k: peer for k, _, peer in _peers()}
        for a in range(n):
            for idx, k in enumerate(FAR_PEERS):
                _pass_on_copy(refs[a], send_sems, recv_sems, a, idx, slots[k]).start()
        token[...] = jnp.zeros(token.shape, F32)

    lands = [pltpu.with_memory_space_constraint(l_, pltpu.HBM) for l_ in lands]
    sems = pltpu.SemaphoreType.DMA((n * len(FAR_PEERS),))
    res = pl.pallas_call(
        body, name=name,
        out_shape=(sems, sems, *[pltpu.HBM(l_.shape, l_.dtype) for l_ in lands], jax.ShapeDtypeStruct((8, 128), F32)),
        in_specs=[HBM_SPEC] * n, out_specs=(SEM_SPEC, SEM_SPEC, *[HBM_SPEC] * n, pl.BlockSpec(memory_space=pltpu.VMEM)),
        input_output_aliases={i: 2 + i for i in range(n)},
        compiler_params=pltpu.CompilerParams(has_side_effects=DATAFLOW),
    )(*lands)
    return dict(send=res[0], recv=res[1], lands=list(res[2:2 + n]), token=res[-1])


def _pass_on_wait(name, passed, after):
    n = len(passed["lands"])

    def body(*refs):
        send_sems, recv_sems = refs[n], refs[n + 1]
        slots = {k: peer for k, _, peer in _peers()}
        for a in range(n):
            for idx, k in enumerate(FAR_PEERS):
                _pass_on_copy(refs[a], send_sems, recv_sems, a, idx, slots[k]).wait_send()
                _pass_on_copy(refs[a], send_sems, recv_sems, a, idx, slots[k ^ 1]).wait_recv()

    res = pl.pallas_call(
        body, name=name, out_shape=tuple(pltpu.HBM(l_.shape, l_.dtype) for l_ in passed["lands"]),
        in_specs=[HBM_SPEC] * n + [SEM_SPEC, SEM_SPEC, pl.BlockSpec(memory_space=pl.ANY)],
        out_specs=tuple([HBM_SPEC] * n), input_output_aliases={i: i for i in range(n)},
        compiler_params=pltpu.CompilerParams(has_side_effects=DATAFLOW),
    )(*passed["lands"], passed["send"], passed["recv"], after)
    return list(res)


def _mm_nn(name, x, w, w_spec, n_out, tn, out_dtype, relu=False, square_x=False, tm=MM_TM):
    m, k = x.shape

    def body(x_ref, w_ref, *outs):
        xv = x_ref[...]
        if square_x:
            xv = xv * xv
        acc = jnp.dot(xv, w_ref[...], preferred_element_type=F32)
        if relu:
            acc = jnp.maximum(acc, 0.0)
        outs[0][...] = acc.astype(outs[0].dtype)

    o_spec = pl.BlockSpec((tm, tn), lambda i, j: (i, j))
    return pl.pallas_call(
        body, name=name, out_shape=jax.ShapeDtypeStruct((m, n_out), out_dtype), grid=(m // tm, n_out // tn),
        in_specs=[pl.BlockSpec((tm, k), lambda i, j: (i, 0)), w_spec],
        out_specs=o_spec, compiler_params=_params(("parallel", "parallel")),
    )(x, w)


def _mm_nt(name, pairs, m, n_out, tn, out_dtype, relu_bwd_of=None, tm=MM_TM, after=None):
    np_ = len(pairs)

    def body(*refs):
        acc = None
        for p in range(np_):
            if len(pairs[p]) == 5:
                d = jnp.dot(refs[2 * p][...], refs[2 * p + 1][...], preferred_element_type=F32)
            else:
                d = _dot_nt(refs[2 * p][...], refs[2 * p + 1][...])
            acc = d if acc is None else acc + d
        if relu_bwd_of is not None:
            acc = acc * (2.0 * refs[2 * np_][...].astype(F32))
        refs[-1][...] = acc.astype(refs[-1].dtype)

    o_spec = pl.BlockSpec((tm, tn), lambda i, j: (i, j))
    operands, specs = [], []
    for pair in pairs:
        operands += [pair[0], pair[2]]
        specs += [pair[1], pair[3]]
    if relu_bwd_of is not None:
        operands.append(relu_bwd_of)
        specs.append(o_spec)
    if after is not None:
        operands.append(after)
        specs.append(pl.BlockSpec((8, 128), lambda i, j: (0, 0)))
    return pl.pallas_call(
        body, name=name, out_shape=jax.ShapeDtypeStruct((m, n_out), out_dtype), grid=(m // tm, n_out // tn),
        in_specs=specs, out_specs=o_spec, compiler_params=_params(("parallel", "parallel")),
    )(*operands)


def _mm_tn(name, x, x_col, ta, dy, dy_col, tb, out_shape, out_spec, grid_ab):
    m = x.shape[0]

    def body(x_ref, dy_ref, o_ref):
        o_ref[...] = _dot_tn(x_ref[...], dy_ref[...]).astype(BF16)

    return pl.pallas_call(
        body, name=name, out_shape=jax.ShapeDtypeStruct(out_shape, BF16), grid=grid_ab,
        in_specs=[pl.BlockSpec((m, ta), lambda a, b: (0, x_col(a))),
                  pl.BlockSpec((m, tb), lambda a, b: (0, dy_col(b)))],
        out_specs=out_spec, compiler_params=_params(("parallel", "parallel")),
    )(x, dy)


def _w_cols(k, tn, off_blocks):
    return pl.BlockSpec((k, tn), lambda i, j: (0, off_blocks + j))


def _a_rows(kw, col_block=0, tm=MM_TM):
    return pl.BlockSpec((tm, kw), lambda i, j: (i, col_block))


def _w_rows(tn, kw, col_block=0):
    return pl.BlockSpec((tn, kw), lambda i, j: (j, col_block))


def _wt_rows(tn, off):
    return pl.BlockSpec((pl.Element(tn), pl.Element(SEG)), lambda i, j: (pl.multiple_of(off + tn * j, 16), 0))


def _wt_block(k, off, tn):
    return pl.BlockSpec((pl.Element(k), pl.Element(tn)), lambda i, j: (off, pl.multiple_of(tn * j, 128)))


GW_TILE = 512


def _grad_w_rows(name, xs, dy, square_x=False):
    m, nb = dy.shape
    per = [x_.shape[1] // GW_TILE for x_ in xs]
    steps = sum(per)
    tiles = [(k, h) for k in range(len(xs)) for h in range(per[k])]

    def body(*refs):
        x_refs, dy_ref, o_ref, buf, sems = refs[:len(xs)], refs[len(xs)], refs[len(xs) + 1], refs[-2], refs[-1]
        s_ = pl.program_id(0)

        def fetch(step, slot):
            for idx, (k, h) in enumerate(tiles):
                @pl.when(step == idx)
                def _():
                    pltpu.make_async_copy(x_refs[k].at[:, pl.ds(h * GW_TILE, GW_TILE)], buf.at[slot], sems.at[slot]).start()

        @pl.when(s_ == 0)
        def _():
            fetch(s_, 0)

        @pl.when(s_ + 1 < steps)
        def _():
            fetch(s_ + 1, (s_ + 1) % 2)

        slot = s_ % 2
        pltpu.make_async_copy(x_refs[0].at[:, pl.ds(0, GW_TILE)], buf.at[slot], sems.at[slot]).wait()
        xv = buf[slot]
        if square_x:
            xv = xv * xv
        o_ref[...] = _dot_tn(xv, dy_ref[...]).astype(BF16)

    return pl.pallas_call(
        body, name=name, out_shape=jax.ShapeDtypeStruct((steps * GW_TILE, nb), BF16), grid=(steps,),
        in_specs=[pl.BlockSpec(memory_space=pl.ANY)] * len(xs) + [pl.BlockSpec((m, nb), lambda i: (0, 0))],
        out_specs=pl.BlockSpec((GW_TILE, nb), lambda i: (i, 0)),
        scratch_shapes=[pltpu.VMEM((2, m, GW_TILE), BF16), pltpu.SemaphoreType.DMA((2,))],
        compiler_params=_params(("arbitrary",)))(*xs, dy)


def _grad_w(name, x, dy):
    na, nb = x.shape[1], dy.shape[1]
    ta, tb = min(na, 1024), min(nb, 512)
    return _mm_tn(name, x, lambda a: a, ta, dy, lambda b: b, tb, (na, nb),
                  pl.BlockSpec((ta, tb), lambda a, b: (a, b)), (na // ta, nb // tb))


def _row_spec(width):
    return pl.BlockSpec((ROW_TILE, width), lambda i: (i, 0))


def _vec_spec(width):
    return pl.BlockSpec((1, width), lambda i: (0, 0))


def _rms_fwd(name, h, g, res=None):
    n, d = h.shape
    tile = ROW_TILE if n % ROW_TILE == 0 else n
    row_spec = pl.BlockSpec((tile, d), lambda i: (i, 0))

    def body(*refs):
        if res is None:
            h_ref, g_ref, hn_ref = refs
            hv = h_ref[...]
        else:
            h_ref, r_ref, g_ref, hs_ref, hn_ref = refs
            hv = h_ref[...] + r_ref[...]
            hs_ref[...] = hv
        r = lax.rsqrt(jnp.mean(hv * hv, axis=-1, keepdims=True) + RMS_EPS)
        hn_ref[...] = (hv * r * g_ref[...]).astype(BF16)

    ins = [h, g] if res is None else [h, res, g]
    in_specs = [row_spec, _vec_spec(d)] if res is None else [row_spec, row_spec, _vec_spec(d)]
    hn_shape = jax.ShapeDtypeStruct((n, d), BF16)
    if res is None:
        out_shape, out_specs = hn_shape, row_spec
    else:
        out_shape, out_specs = [jax.ShapeDtypeStruct((n, d), F32), hn_shape], [row_spec, row_spec]
    return pl.pallas_call(body, name=name, out_shape=out_shape, grid=(n // tile,), in_specs=in_specs,
                          out_specs=out_specs, compiler_params=_params(("parallel",)))(*ins)


def _rms_bwd(name, dhn, h, g, dres, batch, dh_dtype=F32, with_meta=False):
    n, d = h.shape
    t = n // batch
    nt = t // ROW_TILE

    def body(dhn_ref, h_ref, g_ref, dres_ref, *outs):
        first = (pl.program_id(0) == 0) & (pl.program_id(1) == 0)
        hv = h_ref[...]
        r = lax.rsqrt(jnp.mean(hv * hv, axis=-1, keepdims=True) + RMS_EPS)
        nrm = hv * r
        dhn = dhn_ref[...].astype(F32)
        dn = dhn * g_ref[...]
        dh = dres_ref[...].astype(F32) + r * (dn - nrm * jnp.mean(dn * nrm, axis=-1, keepdims=True))
        outs[0][...] = dh.astype(dh_dtype)
        dg_ref = outs[1]

        @pl.when(first)
        def _():
            dg_ref[...] = jnp.zeros(dg_ref.shape, F32)

        dg_ref[...] += jnp.sum(dhn * nrm, axis=0, keepdims=True)
        if with_meta:
            meta_ref = outs[2]

            @pl.when(first)
            def _():
                meta_ref[...] = jnp.zeros(meta_ref.shape, F32)

            @pl.when(pl.program_id(1) == 0)
            def _():
                meta_ref[...] += dh[0:N_META, :]

    row = pl.BlockSpec((ROW_TILE, d), lambda b, j: (b * nt + j, 0))
    vec = pl.BlockSpec((1, d), lambda b, j: (0, 0))
    shapes = [jax.ShapeDtypeStruct((n, d), dh_dtype), jax.ShapeDtypeStruct((1, d), F32)]
    specs = [row, vec]
    if with_meta:
        shapes.append(jax.ShapeDtypeStruct((N_META, d), F32))
        specs.append(pl.BlockSpec((N_META, d), lambda b, j: (0, 0)))
    return pl.pallas_call(body, name=name, out_shape=shapes, grid=(batch, nt), in_specs=[row, row, vec, row],
                          out_specs=specs, compiler_params=_params(("arbitrary", "arbitrary")))(dhn, h, g, dres)


def _final(name, h1, dn, tgt, g, batch):
    n, d = h1.shape
    t = n // batch
    nt = t // ROW_TILE

    def body(h1_ref, dn_ref, tgt_ref, g_ref, dhb_ref, loss_ref, dg_ref):
        first = (pl.program_id(0) == 0) & (pl.program_id(1) == 0)
        hv = h1_ref[...] + dn_ref[...]
        r = lax.rsqrt(jnp.mean(hv * hv, axis=-1, keepdims=True) + RMS_EPS)
        nrm = hv * r
        gv = g_ref[...]
        pos = pl.program_id(1) * ROW_TILE + lax.broadcasted_iota(jnp.int32, (ROW_TILE, 1), 0)
        diff = jnp.where(pos >= N_META, nrm * gv - tgt_ref[...], 0.0)
        dy = diff * (1.0 / d)

        @pl.when(first)
        def _():
            loss_ref[...] = jnp.zeros(loss_ref.shape, F32)
            dg_ref[...] = jnp.zeros(dg_ref.shape, F32)

        loss_ref[...] += jnp.full(loss_ref.shape, 0.5 / d, F32) * jnp.sum(diff * diff)
        dg_ref[...] += jnp.sum(dy * nrm, axis=0, keepdims=True)
        dng = dy * gv
        dh = r * (dng - nrm * jnp.mean(dng * nrm, axis=-1, keepdims=True))
        dhb_ref[...] = dh.astype(BF16)

    row = pl.BlockSpec((ROW_TILE, d), lambda b, j: (b * nt + j, 0))
    vec = pl.BlockSpec((1, d), lambda b, j: (0, 0))
    return pl.pallas_call(
        body, name=name, grid=(batch, nt), in_specs=[row, row, row, vec],
        out_shape=[jax.ShapeDtypeStruct((n, d), BF16), jax.ShapeDtypeStruct((8, 128), F32),
                   jax.ShapeDtypeStruct((1, d), F32)],
        out_specs=[row, pl.BlockSpec((8, 128), lambda b, j: (0, 0)), vec],
        compiler_params=_params(("arbitrary", "arbitrary")))(h1, dn, tgt, g)


def _ln_silu_fwd(name, c1, g, b):
    n, d = c1.shape

    def body(c_ref, g_ref, b_ref, o_ref):
        xv = c_ref[...]
        xc = xv - jnp.mean(xv, axis=-1, keepdims=True)
        rstd = lax.rsqrt(jnp.mean(xc * xc, axis=-1, keepdims=True) + LN_EPS)
        c2 = xc * rstd * g_ref[...] + b_ref[...]
        o_ref[...] = (c2 * _sigmoid(c2)).astype(BF16)

    return pl.pallas_call(body, name=name, out_shape=jax.ShapeDtypeStruct((n, d), BF16), grid=(n // ROW_TILE,),
                          in_specs=[_row_spec(d), _vec_spec(d), _vec_spec(d)], out_specs=_row_spec(d),
                          compiler_params=_params(("parallel",)))(c1, g, b)


def _ln_silu_bwd(name, dc3, c1, g, b):
    n, d = c1.shape

    def body(d_ref, c_ref, g_ref, b_ref, dc1_ref, dg_ref, db_ref):
        xv = c_ref[...]
        xc = xv - jnp.mean(xv, axis=-1, keepdims=True)
        rstd = lax.rsqrt(jnp.mean(xc * xc, axis=-1, keepdims=True) + LN_EPS)
        xh = xc * rstd
        c2 = xh * g_ref[...] + b_ref[...]
        s = _sigmoid(c2)
        dc2 = d_ref[...].astype(F32) * (s * (1.0 + c2 * (1.0 - s)))

        @pl.when(pl.program_id(0) == 0)
        def _():
            dg_ref[...] = jnp.zeros(dg_ref.shape, F32)
            db_ref[...] = jnp.zeros(db_ref.shape, F32)

        dg_ref[...] += jnp.sum(dc2 * xh, axis=0, keepdims=True)
        db_ref[...] += jnp.sum(dc2, axis=0, keepdims=True)
        dxh = dc2 * g_ref[...]
        dc1_ref[...] = rstd * (dxh - jnp.mean(dxh, axis=-1, keepdims=True)
                               - xh * jnp.mean(dxh * xh, axis=-1, keepdims=True))

    return pl.pallas_call(
        body, name=name, grid=(n // ROW_TILE,),
        out_shape=[jax.ShapeDtypeStruct((n, d), F32), jax.ShapeDtypeStruct((1, d), F32),
                   jax.ShapeDtypeStruct((1, d), F32)],
        in_specs=[_row_spec(d), _row_spec(d), _vec_spec(d), _vec_spec(d)],
        out_specs=[_row_spec(d), _vec_spec(d), _vec_spec(d)],
        compiler_params=_params(("arbitrary",)))(dc3, c1, g, b)


MERGE_TC = 512


def _merge_fwd(name, gates, a, c, b_co):
    n, d = a.shape
    nc = d // MERGE_TC

    def body(ga_ref, gc_ref, a_ref, c_ref, b_ref, m_ref):
        f32 = lambda r_: r_[...].astype(F32)
        m = _sigmoid(f32(ga_ref)) * f32(a_ref) + _sigmoid(f32(gc_ref)) * (f32(c_ref) + b_ref[...])
        m_ref[...] = m.astype(BF16)

    blk = lambda off: pl.BlockSpec((ROW_TILE, MERGE_TC), lambda i, j: (i, off + j))
    return pl.pallas_call(
        body, name=name, out_shape=jax.ShapeDtypeStruct((n, d), BF16), grid=(n // ROW_TILE, nc),
        in_specs=[blk(0), blk(nc), blk(0), blk(0), pl.BlockSpec((1, MERGE_TC), lambda i, j: (0, j))],
        out_specs=blk(0), compiler_params=_params(("parallel", "parallel")))(gates, gates, a, c, b_co)


def _merge_bwd(name, dm, gates, a, c, b_co):
    n, d = a.shape
    nc = d // MERGE_TC

    def body(dm_ref, ga_ref, gc_ref, a_ref, c_ref, b_ref, da_ref, dc_ref, dga_ref, dgc_ref, dbco_ref):
        f32 = lambda r_: r_[...].astype(F32)
        dmv = f32(dm_ref)
        sa, sc = _sigmoid(f32(ga_ref)), _sigmoid(f32(gc_ref))
        dc = dmv * sc
        da_ref[...] = (dmv * sa).astype(BF16)
        dc_ref[...] = dc.astype(BF16)
        dga_ref[...] = (dmv * f32(a_ref) * sa * (1.0 - sa)).astype(BF16)
        dgc_ref[...] = (dmv * (f32(c_ref) + b_ref[...]) * sc * (1.0 - sc)).astype(BF16)

        @pl.when(pl.program_id(1) == 0)
        def _():
            dbco_ref[...] = jnp.zeros(dbco_ref.shape, F32)

        dbco_ref[...] += jnp.sum(dc, axis=0, keepdims=True)

    blk = lambda off: pl.BlockSpec((ROW_TILE, MERGE_TC), lambda j, i: (i, off + j))
    vec = pl.BlockSpec((1, MERGE_TC), lambda j, i: (0, j))
    act = jax.ShapeDtypeStruct((n, d), BF16)
    return pl.pallas_call(
        body, name=name, grid=(nc, n // ROW_TILE),
        out_shape=[act, act, act, act, jax.ShapeDtypeStruct((1, d), F32)],
        in_specs=[blk(0), blk(0), blk(nc), blk(0), blk(0), vec],
        out_specs=[blk(0), blk(0), blk(0), blk(0), vec],
        compiler_params=_params(("parallel", "arbitrary")))(dm, gates, gates, a, c, b_co)


CONV_TC = 128
CONV_HALO = 32


def _conv_chunk(t):
    return 48 if t % 48 == 0 else 32 if t % 32 == 0 else 16


def _fold8(x):
    out = x[0:8]
    for k in range(1, x.shape[0] // 8):
        out = out + x[8 * k:8 * k + 8]
    return out


def _glu_conv_fwd(name, glu, b_glu, w_dw, b_dw, batch):
    n, c2 = glu.shape
    c = c2 // 2
    t = n // batch
    nc = c // CONV_TC

    def body(a_ref, gt_ref, ba_ref, bg_ref, w_ref, bdw_ref, o_ref, pad_ref):
        u = (a_ref[...].astype(F32) + ba_ref[...]) * _sigmoid(gt_ref[...].astype(F32) + bg_ref[...])
        pad_ref[0:CONV_HALO, :] = jnp.zeros((CONV_HALO, CONV_TC), F32)
        pad_ref[CONV_HALO:CONV_HALO + t, :] = u
        ch = _conv_chunk(t)
        for r0 in range(0, t, ch):
            acc = jnp.zeros((ch, CONV_TC), F32) + bdw_ref[...]
            for j in range(CONV_W):
                off = r0 + CONV_HALO - (CONV_W - 1) + j
                acc = acc + w_ref[j:j + 1, :] * pad_ref[off:off + ch, :]
            o_ref[r0:r0 + ch, :] = acc

    seq = lambda off: pl.BlockSpec((t, CONV_TC), lambda b, j: (b, off + j))
    vec = lambda off: pl.BlockSpec((1, CONV_TC), lambda b, j: (0, off + j))
    return pl.pallas_call(
        body, name=name, out_shape=jax.ShapeDtypeStruct((n, c), F32), grid=(batch, nc),
        in_specs=[seq(0), seq(nc), vec(0), vec(nc), pl.BlockSpec((CONV_W, CONV_TC), lambda b, j: (0, j)), vec(0)],
        out_specs=seq(0), scratch_shapes=[pltpu.VMEM((t + CONV_HALO, CONV_TC), F32)],
        compiler_params=_params(("parallel", "parallel")))(glu, glu, b_glu, b_glu, w_dw, b_dw)


def _glu_conv_bwd(name, dc1, glu, b_glu, w_dw, batch):
    n, c2 = glu.shape
    c = c2 // 2
    t = n // batch
    nc = c // CONV_TC

    def body(d_ref, a_ref, gt_ref, ba_ref, bg_ref, w_ref, dga_ref, dgg_ref, dw_ref, dbdw_ref, dba_ref, dbg_ref,
             padu_ref, padd_ref):
        av = a_ref[...].astype(F32) + ba_ref[...]
        sg = _sigmoid(gt_ref[...].astype(F32) + bg_ref[...])
        dc = d_ref[...]
        padu_ref[0:CONV_HALO, :] = jnp.zeros((CONV_HALO, CONV_TC), F32)
        padu_ref[CONV_HALO:CONV_HALO + t, :] = av * sg
        padd_ref[0:t, :] = dc
        padd_ref[t:t + CONV_HALO, :] = jnp.zeros((CONV_HALO, CONV_TC), F32)

        @pl.when(pl.program_id(1) == 0)
        def _():
            dw_ref[...] = jnp.zeros(dw_ref.shape, F32)
            dbdw_ref[...] = jnp.zeros(dbdw_ref.shape, F32)
            dba_ref[...] = jnp.zeros(dba_ref.shape, F32)
            dbg_ref[...] = jnp.zeros(dbg_ref.shape, F32)

        ch = _conv_chunk(t)
        zero8 = jnp.zeros((8, CONV_TC), F32)
        dw_acc = [zero8] * CONV_W
        sum_dc, sum_a, sum_g = zero8, zero8, zero8
        for r0 in range(0, t, ch):
            dcc = d_ref[r0:r0 + ch, :]
            du = jnp.zeros((ch, CONV_TC), F32)
            for j in range(CONV_W):
                back = r0 + CONV_W - 1 - j
                du = du + w_ref[j:j + 1, :] * padd_ref[back:back + ch, :]
                off = r0 + CONV_HALO - (CONV_W - 1) + j
                dw_acc[j] = dw_acc[j] + _fold8(dcc * padu_ref[off:off + ch, :])
            sgc = _sigmoid(gt_ref[r0:r0 + ch, :].astype(F32) + bg_ref[...])
            dga = du * sgc
            dgg = du * padu_ref[CONV_HALO + r0:CONV_HALO + r0 + ch, :] * (1.0 - sgc)
            dga_ref[r0:r0 + ch, :] = dga.astype(BF16)
            dgg_ref[r0:r0 + ch, :] = dgg.astype(BF16)
            sum_dc, sum_a, sum_g = sum_dc + _fold8(dcc), sum_a + _fold8(dga), sum_g + _fold8(dgg)
        for j in range(CONV_W):
            dw_ref[j:j + 1, :] += jnp.sum(dw_acc[j], axis=0, keepdims=True)
        dbdw_ref[...] += jnp.sum(sum_dc, axis=0, keepdims=True)
        dba_ref[...] += jnp.sum(sum_a, axis=0, keepdims=True)
        dbg_ref[...] += jnp.sum(sum_g, axis=0, keepdims=True)

    seq = lambda off: pl.BlockSpec((t, CONV_TC), lambda j, b: (b, off + j))
    vec = lambda off: pl.BlockSpec((1, CONV_TC), lambda j, b: (0, off + j))
    wsp = pl.BlockSpec((CONV_W, CONV_TC), lambda j, b: (0, j))
    act = jax.ShapeDtypeStruct((n, c), BF16)
    v = jax.ShapeDtypeStruct((1, c), F32)
    return pl.pallas_call(
        body, name=name, grid=(nc, batch),
        out_shape=[act, act, jax.ShapeDtypeStruct((CONV_W, c), F32), v, v, v],
        in_specs=[seq(0), seq(0), seq(nc), vec(0), vec(nc), wsp],
        out_specs=[seq(0), seq(0), wsp, vec(0), vec(0), vec(0)],
        scratch_shapes=[pltpu.VMEM((t + CONV_HALO, CONV_TC), F32), pltpu.VMEM((t + CONV_HALO, CONV_TC), F32)],
        compiler_params=_params(("parallel", "arbitrary")))(dc1, glu, glu, b_glu, b_glu, w_dw)


def _split3(x):
    hi = x.astype(BF16)
    r = x - hi.astype(F32)
    mid = r.astype(BF16)
    lo = (r - mid.astype(F32)).astype(BF16)
    return hi, mid, lo


def _tri_matmul(tri, x):
    hi, mid, lo = _split3(x)
    dot = lambda v: jnp.dot(tri, v, preferred_element_type=F32)
    return dot(hi) + dot(mid) + dot(lo)


def _fox_prep_fwd(name, fg, b_fg, batch):
    n, w = fg.shape
    t = n // batch
    nq = t // ROW_TILE

    def body(fg_ref, b_ref, cum_ref):
        row = lax.broadcasted_iota(jnp.int32, (ROW_TILE, ROW_TILE), 0)
        col = lax.broadcasted_iota(jnp.int32, (ROW_TILE, ROW_TILE), 1)
        tri = (row >= col).astype(BF16)
        for k in range(nq):
            rows = slice(k * ROW_TILE, (k + 1) * ROW_TILE)
            z = fg_ref[rows, :] + b_ref[...]
            logf = jnp.minimum(z, 0.0) - jnp.log(1.0 + jnp.exp(-jnp.abs(z)))
            cum = _tri_matmul(tri, logf)
            if k > 0:
                cum = cum + cum_ref[k * ROW_TILE - 1:k * ROW_TILE, :]
            cum_ref[rows, :] = cum

    seq = pl.BlockSpec((t, w), lambda b: (b, 0))
    return pl.pallas_call(body, name=name, out_shape=jax.ShapeDtypeStruct((n, w), F32), grid=(batch,),
                          in_specs=[seq, pl.BlockSpec((1, w), lambda b: (0, 0))], out_specs=seq,
                          compiler_params=_params(("parallel",)))(fg, b_fg)


def _fox_prep_bwd(name, dcum_k, dcum_q, fg, b_fg, batch):
    n, w = fg.shape
    t = n // batch
    nq = t // ROW_TILE

    def body(dk_ref, dq_ref, fg_ref, b_ref, dfg_ref, db_ref, rev_ref):
        row = lax.broadcasted_iota(jnp.int32, (ROW_TILE, ROW_TILE), 0)
        col = lax.broadcasted_iota(jnp.int32, (ROW_TILE, ROW_TILE), 1)
        tri = (col >= row).astype(BF16)

        @pl.when(pl.program_id(0) == 0)
        def _():
            db_ref[...] = jnp.zeros(db_ref.shape, F32)

        for k in reversed(range(nq)):
            rows = slice(k * ROW_TILE, (k + 1) * ROW_TILE)
            dlog = _tri_matmul(tri, dk_ref[rows, :] + dq_ref[rows, :])
            if k < nq - 1:
                dlog = dlog + rev_ref[(k + 1) * ROW_TILE:(k + 1) * ROW_TILE + 1, :]
            rev_ref[rows, :] = dlog
            dfg = dlog * _sigmoid(-(fg_ref[rows, :] + b_ref[...]))
            dfg_ref[rows, :] = dfg.astype(BF16)
            db_ref[...] += jnp.sum(dfg, axis=0, keepdims=True)

    seq = pl.BlockSpec((t, w), lambda b: (b, 0))
    vec = pl.BlockSpec((1, w), lambda b: (0, 0))
    return pl.pallas_call(
        body, name=name, grid=(batch,),
        out_shape=[jax.ShapeDtypeStruct((n, w), BF16), jax.ShapeDtypeStruct((1, w), F32)],
        in_specs=[seq, seq, seq, vec], out_specs=[seq, vec], scratch_shapes=[pltpu.VMEM((t, w), F32)],
        compiler_params=_params(("arbitrary",)))(dcum_k, dcum_q, fg, b_fg)


def _head_masks(x):
    lane = lax.broadcasted_iota(jnp.int32, x.shape, 1)
    zero = jnp.zeros(x.shape, x.dtype)
    return jnp.where(lane < HEAD_DIM, x, zero), jnp.where(lane >= HEAD_DIM, x, zero)


ATTN_BLOCK = 512


def _attn_blocks(t):
    nb = max(t // ATTN_BLOCK, 1)
    blocks = [(i * ATTN_BLOCK, ATTN_BLOCK) for i in range(nb - 1)]
    return blocks + [((nb - 1) * ATTN_BLOCK, t - (nb - 1) * ATTN_BLOCK)]


def _attn_specs(t):
    blocks = _attn_blocks(t)
    width = max(sz for _, sz in blocks)
    qkv = lambda off: pl.BlockSpec((t, LANES), lambda b, h: (b, off + h))
    cumr = pl.BlockSpec((None, None, len(blocks), 8, width), lambda b, h: (b, h, 0, 0, 0))
    return qkv, cumr


def _key_sums_to_blocks(cum, batch, t, n_pairs):
    blocks = _attn_blocks(t)
    width = max(sz for _, sz in blocks)
    cum_h = cum.reshape(batch, t, -1)[:, :, :2 * n_pairs].reshape(batch, t, n_pairs, 2)
    rows = [jnp.pad(jnp.transpose(cum_h[:, s0:s0 + sz], (0, 2, 3, 1)), ((0, 0), (0, 0), (0, 6), (0, width - sz)))
            for s0, sz in blocks]
    return jnp.stack(rows, axis=2)


def _key_sums_from_blocks(dcumr, batch, t, n_pairs):
    cols = [jnp.transpose(dcumr[:, :, j, :2, :sz], (0, 3, 1, 2)) for j, (_, sz) in enumerate(_attn_blocks(t))]
    return jnp.concatenate(cols, axis=1).reshape(batch * t, 2 * n_pairs)


def _causal(size):
    row = lax.broadcasted_iota(jnp.int32, (size, size), 0)
    col = lax.broadcasted_iota(jnp.int32, (size, size), 1)
    return row >= col


def _attn_fwd(name, qkv, cumr, batch):
    n, w3 = qkv.shape
    w = w3 // 3
    t = n // batch
    n_pairs = w // LANES
    blocks = _attn_blocks(t)

    def body(q_ref, k_ref, v_ref, cr_ref, o_ref, lse_ref):
        pair = pl.program_id(1)

        @pl.when(pair == 0)
        def _():
            lse_ref[...] = jnp.zeros(lse_ref.shape, F32)

        for i, (q0, qn) in enumerate(blocks):
            rows = slice(q0, q0 + qn)
            qs = _head_masks(q_ref[rows, :] * (0.125 * LOG2E))
            outs, lses = [], []
            for hh in range(2):
                m = jnp.full((qn, 1), NEG, F32)
                l = jnp.zeros((qn, 1), F32)
                acc = jnp.zeros((qn, LANES), F32)
                for j in range(i + 1):
                    k0, kn = blocks[j]
                    cols = slice(k0, k0 + kn)
                    s = _dot_nt(qs[hh], k_ref[cols, :]) - cr_ref[j, hh:hh + 1, 0:kn] * LOG2E
                    if j == i:
                        s = jnp.where(_causal(qn), s, NEG)
                    m_new = jnp.maximum(m, jnp.max(s, axis=1, keepdims=True))
                    alpha = jnp.exp2(m - m_new)
                    p = jnp.exp2(s - m_new)
                    l = alpha * l + jnp.sum(p, axis=1, keepdims=True)
                    acc = alpha * acc + jnp.dot(p.astype(BF16), v_ref[cols, :], preferred_element_type=F32)
                    m = m_new
                outs.append(acc / l)
                lses.append(m + jnp.log2(l))
            lane = lax.broadcasted_iota(jnp.int32, (qn, LANES), 1)
            o_ref[rows, :] = jnp.where(lane < HEAD_DIM, outs[0], outs[1]).astype(BF16)
            lse_ref[rows, :] = jnp.where(lane == 2 * pair, lses[0],
                                         jnp.where(lane == 2 * pair + 1, lses[1], lse_ref[rows, :]))

    qkv_spec, cumr_spec = _attn_specs(t)
    return pl.pallas_call(
        body, name=name, grid=(batch, n_pairs),
        out_shape=[jax.ShapeDtypeStruct((n, w), BF16), jax.ShapeDtypeStruct((n, LANES), F32)],
        in_specs=[qkv_spec(0), qkv_spec(n_pairs), qkv_spec(2 * n_pairs), cumr_spec],
        out_specs=[qkv_spec(0), pl.BlockSpec((t, LANES), lambda b, h: (b, 0))],
        compiler_params=_params(("parallel", "arbitrary")))(qkv, qkv, qkv, cumr)


def _attn_bwd(name, qkv, o, do, lse, cumr, batch):
    n, w3 = qkv.shape
    w = w3 // 3
    t = n // batch
    n_pairs = w // LANES
    blocks = _attn_blocks(t)

    def body(q_ref, k_ref, v_ref, o_ref, do_ref, lse_ref, cr_ref, dq_ref, dk_ref, dv_ref, dcr_ref, dcq_ref,
             dk_acc, dv_acc):
        pair = pl.program_id(1)
        dk_acc[...] = jnp.zeros(dk_acc.shape, F32)
        dv_acc[...] = jnp.zeros(dv_acc.shape, F32)
        dcr_ref[...] = jnp.zeros(dcr_ref.shape, F32)

        @pl.when(pair == 0)
        def _():
            dcq_ref[...] = jnp.zeros(dcq_ref.shape, F32)

        for i, (q0, qn) in enumerate(blocks):
            rows = slice(q0, q0 + qn)
            qs = _head_masks(q_ref[rows, :] * 0.125)
            q2 = _head_masks(q_ref[rows, :] * (0.125 * LOG2E))
            dos = _head_masks(do_ref[rows, :])
            dq = jnp.zeros((qn, LANES), F32)
            lane = lax.broadcasted_iota(jnp.int32, (qn, LANES), 1)
            dcq = []
            for hh in range(2):
                row_sum = jnp.zeros((qn, 1), F32)
                lse = jnp.sum(jnp.where(lane == 2 * pair + hh, lse_ref[rows, :], 0.0), axis=1, keepdims=True)
                delta = jnp.sum(dos[hh].astype(F32) * o_ref[rows, :].astype(F32), axis=1, keepdims=True)
                for j in range(i + 1):
                    k0, kn = blocks[j]
                    cols = slice(k0, k0 + kn)
                    s = _dot_nt(q2[hh], k_ref[cols, :]) - cr_ref[j, hh:hh + 1, 0:kn] * LOG2E
                    p = jnp.exp2(s - lse)
                    if j == i:
                        p = jnp.where(_causal(qn), p, 0.0)
                    dp = _dot_nt(dos[hh], v_ref[cols, :])
                    ds = p * (dp - delta)
                    pb, dsb = p.astype(BF16), ds.astype(BF16)
                    km = _head_masks(k_ref[cols, :])[hh]
                    dv_acc[j, :, 0:kn] += _dot_tn(dos[hh], pb)
                    dk_acc[j, :, 0:kn] += _dot_tn(qs[hh], dsb)
                    dq = dq + jnp.dot(dsb, km, preferred_element_type=F32)
                    dcr_ref[j, hh:hh + 1, 0:kn] -= jnp.sum(ds, axis=0, keepdims=True)
                    row_sum = row_sum + jnp.sum(ds, axis=1, keepdims=True)
                dcq.append(row_sum)
            dq_ref[rows, :] = (dq * 0.125).astype(BF16)
            dcq_ref[rows, :] = jnp.where(lane == 2 * pair, dcq[0],
                                         jnp.where(lane == 2 * pair + 1, dcq[1], dcq_ref[rows, :]))
        for j, (k0, kn) in enumerate(blocks):
            dk_ref[k0:k0 + kn, :] = dk_acc[j].T[0:kn, :].astype(BF16)
            dv_ref[k0:k0 + kn, :] = dv_acc[j].T[0:kn, :].astype(BF16)

    qkv_spec, cumr_spec = _attn_specs(t)
    wide = -(-max(sz for _, sz in blocks) // LANES) * LANES
    act = jax.ShapeDtypeStruct((n, w), BF16)
    return pl.pallas_call(
        body, name=name, grid=(batch, n_pairs),
        out_shape=[act, act, act, jax.ShapeDtypeStruct(cumr.shape, F32), jax.ShapeDtypeStruct((n, LANES), F32)],
        in_specs=[qkv_spec(0), qkv_spec(n_pairs), qkv_spec(2 * n_pairs), qkv_spec(0), qkv_spec(0),
                  pl.BlockSpec((t, LANES), lambda b, h: (b, 0)), cumr_spec],
        out_specs=[qkv_spec(0), qkv_spec(0), qkv_spec(0), cumr_spec, pl.BlockSpec((t, LANES), lambda b, h: (b, 0))],
        scratch_shapes=[pltpu.VMEM((len(blocks), LANES, wide), F32), pltpu.VMEM((len(blocks), LANES, wide), F32)],
        compiler_params=_params(("parallel", "arbitrary")))(qkv, qkv, qkv, o, do, lse, cumr)


def _adamw(name, parts, w, m, v):
    r, c = w.shape
    tr = 128 if r % 128 == 0 else r
    tc = 256 if tr > 128 and c % 256 == 0 else c
    c1 = 1.0 - ADAM_B1 ** ADAM_STEP
    c2 = 1.0 - ADAM_B2 ** ADAM_STEP

    def body(p_ref, w_ref, m_ref, v_ref, g_ref, d_ref, m2_ref, v2_ref):
        g = p_ref[0].astype(F32)
        for s in range(1, N_DEV):
            g = g + p_ref[s].astype(F32)
        m2 = ADAM_B1 * m_ref[...] + (1.0 - ADAM_B1) * g
        v2 = ADAM_B2 * v_ref[...] + (1.0 - ADAM_B2) * (g * g)
        g_ref[...] = g
        m2_ref[...] = m2
        v2_ref[...] = v2
        d_ref[...] = -ADAM_LR * ((m2 / c1) / (jnp.sqrt(v2 / c2) + ADAM_EPS) + ADAM_WD * w_ref[...])

    blk = pl.BlockSpec((tr, tc), lambda i, j: (i, j))
    shp = jax.ShapeDtypeStruct((r, c), F32)
    return pl.pallas_call(
        body, name=name, out_shape=[shp] * 4, grid=(r // tr, c // tc),
        in_specs=[pl.BlockSpec((N_DEV, tr, tc), lambda i, j: (0, i, j)), blk, blk, blk], out_specs=[blk] * 4,
        compiler_params=_params(("parallel", "parallel")))(parts, w, m, v)


def _cat_small(vals):
    parts = []
    for name in SMALL:
        v = vals[name].reshape(1, -1).astype(F32)
        parts.append(jnp.pad(v, ((0, 0), (0, SMALL_W[name] - v.shape[1]))))
    return jnp.concatenate(parts, axis=1)


def _split_small(row, shapes):
    out, off = {}, 0
    for name in SMALL:
        out[name] = row[0, off:off + SMALL_N[name]].reshape(shapes[name])
        off += SMALL_W[name]
    return out


def _cols_from_shards(g):
    return jnp.transpose(g, (1, 0, 2)).reshape(g.shape[1], N_DEV * g.shape[2])


def _shards_from_cols(a):
    r, c = a.shape
    return jnp.transpose(a.reshape(r, N_DEV, c // N_DEV), (1, 0, 2))


def kernel(x, meta_tokens, norm_mix_gain, w_in, b_forget, w_attn_out, b_glu, conv_dw_w, conv_dw_b, conv_ln_gain, conv_ln_bias, w_conv_out, b_conv_out, w_out, norm_mlp_gain, w_mlp_up, w_mlp_down, final_norm_gain, loss_target, m_meta_tokens, m_norm_mix_gain, m_w_in, m_b_forget, m_w_attn_out, m_b_glu, m_conv_dw_w, m_conv_dw_b, m_conv_ln_gain, m_conv_ln_bias, m_w_conv_out, m_b_conv_out, m_w_out, m_norm_mlp_gain, m_w_mlp_up, m_w_mlp_down, m_final_norm_gain, v_meta_tokens, v_norm_mix_gain, v_w_in, v_b_forget, v_w_attn_out, v_b_glu, v_conv_dw_w, v_conv_dw_b, v_conv_ln_gain, v_conv_ln_bias, v_w_conv_out, v_b_conv_out, v_w_out, v_norm_mlp_gain, v_w_mlp_up, v_w_mlp_down, v_final_norm_gain):
    weights = dict(meta_tokens=meta_tokens, norm_mix_gain=norm_mix_gain, w_in=w_in, b_forget=b_forget, w_attn_out=w_attn_out, b_glu=b_glu, conv_dw_w=conv_dw_w, conv_dw_b=conv_dw_b, conv_ln_gain=conv_ln_gain, conv_ln_bias=conv_ln_bias, w_conv_out=w_conv_out, b_conv_out=b_conv_out, w_out=w_out, norm_mlp_gain=norm_mlp_gain, w_mlp_up=w_mlp_up, w_mlp_down=w_mlp_down, final_norm_gain=final_norm_gain)
    mom_m = dict(meta_tokens=m_meta_tokens, norm_mix_gain=m_norm_mix_gain, w_in=m_w_in, b_forget=m_b_forget, w_attn_out=m_w_attn_out, b_glu=m_b_glu, conv_dw_w=m_conv_dw_w, conv_dw_b=m_conv_dw_b, conv_ln_gain=m_conv_ln_gain, conv_ln_bias=m_conv_ln_bias, w_conv_out=m_w_conv_out, b_conv_out=m_b_conv_out, w_out=m_w_out, norm_mlp_gain=m_norm_mlp_gain, w_mlp_up=m_w_mlp_up, w_mlp_down=m_w_mlp_down, final_norm_gain=m_final_norm_gain)
    mom_v = dict(meta_tokens=v_meta_tokens, norm_mix_gain=v_norm_mix_gain, w_in=v_w_in, b_forget=v_b_forget, w_attn_out=v_w_attn_out, b_glu=v_b_glu, conv_dw_w=v_conv_dw_w, conv_dw_b=v_conv_dw_b, conv_ln_gain=v_conv_ln_gain, conv_ln_bias=v_conv_ln_bias, w_conv_out=v_w_conv_out, b_conv_out=v_b_conv_out, w_out=v_w_out, norm_mlp_gain=v_norm_mlp_gain, w_mlp_up=v_w_mlp_up, w_mlp_down=v_w_mlp_down, final_norm_gain=v_final_norm_gain)
    names = list(weights)
    batch, seq, d = x.shape
    t = seq + N_META
    n = batch * t
    n_pairs = d // LANES
    assert t % ROW_TILE == 0 and d == SEG

    to_rows = lambda w3: jnp.transpose(w3[0])
    w_in_t, m_in_t, v_in_t = to_rows(w_in), to_rows(m_w_in), to_rows(v_w_in)
    first = [w_in_t.astype(BF16), meta_tokens, conv_dw_w[0]]
    rest = [w_[0].astype(BF16) for w_ in (w_attn_out, w_conv_out, w_out, w_mlp_up, w_mlp_down)]
    tgt = jnp.concatenate([jnp.zeros((batch, N_META, d), F32), loss_target], axis=1).reshape(n, d)
    h0_rows = jnp.pad(x, ((0, 0), (N_META, 0), (0, 0)))
    g1 = norm_mix_gain.reshape(1, -1)
    gather_a = _exchange_start("gather_in_start", [(f_, False) for f_ in first], ks=CHIP_PEERS)
    level_1 = _exchange_wait("gather_in_wait", gather_a, [gather_a["token"], tgt, h0_rows, w_in_t, m_in_t, v_in_t] + rest)
    passed = _pass_on_start("gather_in_pass_start", level_1)
    w_in_g, meta_g, w_dw_g = _pass_on_wait("gather_in_pass_wait", passed, passed["token"])
    gather_b = _exchange_start("gather_rest_start", [(r_, False) for r_ in rest])
    n_fg = b_forget.shape[1]
    shard_w = w_in.shape[2]
    wt = w_in_g.reshape(N_DEV * shard_w, d)
    o_fg = 3 * SEG
    seg_rows = [0, SEG, 2 * SEG] + [o_fg + n_fg + i * SEG for i in range(4)]
    d_ff = w_mlp_down.shape[1] * N_DEV
    ff_blk = d_ff // N_DEV
    meta_f = _cols_from_shards(meta_g)
    w_dw = _cols_from_shards(w_dw_g)

    row2 = lambda v: v.reshape(1, -1)
    g2, g3 = row2(norm_mlp_gain), row2(final_norm_gain)
    b_fg = jnp.pad(b_forget, ((0, 0), (0, FG_PAD - n_fg)))
    h0 = lax.dynamic_update_slice(h0_rows, jnp.broadcast_to(meta_f[None], (batch, N_META, d)), (0, 0, 0)).reshape(n, d)

    hn1 = _rms_fwd("rms1", h0, g1)
    proj = lambda name, off, width, tn, dt: _mm_nt(name, [(hn1, _a_rows(d), wt, _wt_rows(tn, off))], n, width, tn, dt,
                                                   after=gather_b["token"])
    qkv = proj("proj_qkv", 0, 3 * SEG, SEG, BF16)
    glu = proj("proj_glu", seg_rows[3], 2 * SEG, SEG, BF16)
    gates = proj("proj_gates", seg_rows[5], 2 * SEG, SEG, BF16)
    fg = proj("proj_fg", o_fg, FG_PAD, FG_PAD, F32)

    cum = _fox_prep_fwd("fox_cumsum", fg, b_fg, batch)
    cumr = _key_sums_to_blocks(cum, batch, t, n_pairs)
    o, lse = _attn_fwd("attn_fwd", qkv, cumr, batch)
    rest = _exchange_wait("gather_rest_wait", gather_b, o)
    w_ao, w_co, w_o = [r_.reshape(d, d) for r_ in rest[:3]]
    w_up = rest[3]
    w_dn = rest[4].reshape(d_ff, d)
    a = _mm_nn("attn_out", o, w_ao, _w_cols(d, d, 0), d, d, BF16)

    c1 = _glu_conv_fwd("glu_conv", glu, b_glu, w_dw, conv_dw_b, batch)
    c3 = _ln_silu_fwd("ln_silu", c1, conv_ln_gain, conv_ln_bias)
    c = _mm_nn("conv_out", c3, w_co, _w_cols(d, d, 0), d, d, BF16)

    mrg = _merge_fwd("merge", gates, a, c, b_conv_out)
    mo = _mm_nn("mix_out", mrg, w_o, _w_cols(d, d, 0), d, d, F32)
    h1, hn2 = _rms_fwd("resid_rms2", h0, g2, res=mo)
    per = ff_blk // 512
    r_up = _mm_nn("mlp_up", hn2, w_up, pl.BlockSpec((None, d, 512), lambda i, j: (j // per, 0, j % per)),
                  d_ff, 512, BF16, relu=True)
    dn = _mm_nn("mlp_down", r_up, w_dn, _w_cols(d_ff, d, 0), d, d, F32, square_x=True, tm=ROW_TILE)
    dh2b, loss_blk, dg3 = _final("final_loss", h1, dn, tgt, g3, batch)

    dup = _mm_nt("d_mlp_down", [(dh2b, _a_rows(d), w_dn, _w_rows(d, d))], n, d_ff, d, BF16, relu_bwd_of=r_up)
    dw_dn = _grad_w_rows("gw_mlp_down", [r_up], dh2b, square_x=True)
    dhn2 = _mm_nt("d_mlp_up", [(dup, _a_rows(ff_blk, g), w_up, pl.BlockSpec((None, 512, ff_blk), lambda i, j, g=g: (g, j, 0)))
                               for g in range(N_DEV)], n, d, 512, BF16)
    dw_up = _mm_tn("gw_mlp_up", hn2, lambda a_: 0, d, dup, lambda b_: b_, ff_blk, (N_DEV, d, ff_blk),
                   pl.BlockSpec((None, d, ff_blk), lambda a_, b_: (b_, 0, 0)), (1, N_DEV))
    scatter_1 = _exchange_start("scatter_mlp_start", [(dw_dn.reshape(N_DEV, ff_blk, d), True), (dw_up, True)])
    dh1b, dg2 = _rms_bwd("rms2_bwd", dhn2, h1, g2 + scatter_1["token"][0:1, 0:1], dh2b, batch, dh_dtype=BF16)

    dm = _mm_nt("d_mix_out", [(dh1b, _a_rows(d), w_o, _w_rows(d, d))], n, d, d, BF16)
    dw_o = _grad_w("gw_mix_out", mrg, dh1b)
    da, dc, dga, dgc, dbco = _merge_bwd("merge_bwd", dm, gates, a, c, b_conv_out)

    do = _mm_nt("d_attn_out", [(da, _a_rows(d), w_ao, _w_rows(d, d))], n, d, d, BF16)
    dw_ao = _grad_w("gw_attn_out", o, da)
    dc3 = _mm_nt("d_conv_out", [(dc, _a_rows(d), w_co, _w_rows(d, d))], n, d, d, BF16)
    dw_co = _grad_w("gw_conv_out", c3, dc)

    scatter_2 = _exchange_start("scatter_mix_start", [(dw_.reshape(N_DEV, d // N_DEV, d), True)
                                                      for dw_ in (dw_o, dw_ao, dw_co)])
    dc1, dg_ln, db_ln = _ln_silu_bwd("ln_silu_bwd", dc3, c1, conv_ln_gain + scatter_2["token"][0:1, 0:1],
                                     conv_ln_bias)
    dglu_a, dglu_g, dw_dw, db_dw, dbg_a, dbg_g = _glu_conv_bwd("glu_conv_bwd", dc1, glu, b_glu, w_dw, batch)

    dq, dk, dv, dcumr, dcum_q = _attn_bwd("attn_bwd", qkv, o, do, lse, cumr, batch)
    dcum_k = jnp.pad(_key_sums_from_blocks(dcumr, batch, t, n_pairs), ((0, 0), (0, FG_PAD - 2 * n_pairs)))
    dfg, db_fg = _fox_prep_bwd("fox_cumsum_bwd", dcum_k, dcum_q, fg, b_fg, batch)

    segs = [dq, dk, dv, dglu_a, dglu_g, dga, dgc]
    gw_t = _grad_w_rows("gw_in", segs, hn1)
    gw_fg = _grad_w("gw_in_fg", dfg, hn1)[:n_fg]
    dw_in_t = jnp.concatenate([gw_t[:o_fg], gw_fg, gw_t[o_fg:]], axis=0).reshape(N_DEV, shard_w, d)
    scatter_3 = _exchange_start("scatter_in_start", [(dw_in_t, True)])
    pairs = [(s_, _a_rows(SEG, 0, ROW_TILE), wt, _wt_block(SEG, seg_rows[i], 512), "nn") for i, s_ in enumerate(segs)]
    pairs.append((dfg, _a_rows(FG_PAD, 0, ROW_TILE), wt, _wt_block(FG_PAD, o_fg, 512), "nn"))
    dhn1 = _mm_nt("d_proj_in", pairs, n, d, 512, BF16, tm=ROW_TILE, after=scatter_3["token"])
    dh0, dg1, dmeta = _rms_bwd("rms1_bwd", dhn1, h0, g1, dh1b, batch, with_meta=True)
    grad_x = dh0.reshape(batch, t, d)[:, N_META:, :]

    small_g = dict(norm_mix_gain=dg1, b_forget=db_fg[:, :n_fg], b_glu=jnp.concatenate([dbg_a, dbg_g], axis=1),
                   conv_dw_b=db_dw, conv_ln_gain=dg_ln, conv_ln_bias=db_ln, b_conv_out=dbco, norm_mlp_gain=dg2,
                   final_norm_gain=dg3)
    scatter_4 = _exchange_start("scatter_small_start", [
        (_shards_from_cols(dmeta), True), (_shards_from_cols(dw_dw), True), (_cat_small(small_g), False),
        (loss_blk[0:1, :], False)])

    grads, deltas, new_m, new_v = {}, {}, {}, {}

    def update(k, parts):
        shp = weights[k].shape
        if k == "w_in":
            res_ = _adamw("adamw_" + k, parts, w_in_t, m_in_t, v_in_t)
            res_ = [jnp.transpose(r) for r in res_]
        else:
            w2 = lambda arr: arr.reshape(parts.shape[1:])
            res_ = _adamw("adamw_" + k, parts, w2(weights[k]), w2(mom_m[k]), w2(mom_v[k]))
        grads[k], deltas[k], new_m[k], new_v[k] = [r.reshape(shp) for r in res_]

    for k, parts in zip(("w_mlp_down", "w_mlp_up"), _exchange_wait("scatter_mlp_wait", scatter_1, scatter_4["token"])):
        update(k, parts)
    for k, parts in zip(("w_out", "w_attn_out", "w_conv_out"),
                        _exchange_wait("scatter_mix_wait", scatter_2, deltas["w_mlp_up"])):
        update(k, parts)
    update("w_in", _exchange_wait("scatter_in_wait", scatter_3, deltas["w_conv_out"])[0])
    reduced = _exchange_wait("scatter_small_wait", scatter_4, deltas["w_in"])
    loss = jnp.sum(reduced.pop()[:, 0, 0])
    for k, parts in zip(("meta_tokens", "conv_dw_w"), reduced[:-1]):
        update(k, parts)
    res = _adamw("adamw_small", reduced[-1], _cat_small(weights), _cat_small(mom_m), _cat_small(mom_v))
    shapes = {k: weights[k].shape for k in SMALL}
    for dst, r in zip((grads, deltas, new_m, new_v), res):
        dst.update(_split_small(r, shapes))

    return (loss, grad_x, *[grads[k] for k in names], *[deltas[k] for k in names],
            *[new_m[k] for k in names], *[new_v[k] for k in names])
```

```python
import functools

import jax
import jax.numpy as jnp
from jax import lax
from jax.experimental import pallas as pl
from jax.experimental.pallas import tpu as pltpu

F32, BF16 = jnp.float32, jnp.bfloat16
N_DEV = 8
N_META = 16
HEAD_DIM = 64
LANES = 128
CONV_W = 31
RMS_EPS = 1e-6
LN_EPS = 1e-5
ROW_TILE = 688
MM_TM = 2 * ROW_TILE
SEG = 1024
FG_PAD = 128
VMEM_LIMIT = 56 * 1024 * 1024
ADAM_LR, ADAM_B1, ADAM_B2, ADAM_EPS, ADAM_WD, ADAM_STEP = 0.001, 0.9, 0.999, 1e-08, 0.01, 10
NEG = -1e30
LOG2E = 1.4426950408889634

SMALL = ("norm_mix_gain", "b_forget", "b_glu", "conv_dw_b", "conv_ln_gain", "conv_ln_bias", "b_conv_out",
         "norm_mlp_gain", "final_norm_gain")
SMALL_W = {"norm_mix_gain": 1024, "b_forget": 128, "b_glu": 2048, "conv_dw_b": 1024, "conv_ln_gain": 1024,
           "conv_ln_bias": 1024, "b_conv_out": 1024, "norm_mlp_gain": 1024, "final_norm_gain": 1024}
SMALL_N = {"norm_mix_gain": 1024, "b_forget": 16, "b_glu": 2048, "conv_dw_b": 1024, "conv_ln_gain": 1024,
           "conv_ln_bias": 1024, "b_conv_out": 1024, "norm_mlp_gain": 1024, "final_norm_gain": 1024}


def _params(sem=None):
    return pltpu.CompilerParams(dimension_semantics=sem, vmem_limit_bytes=VMEM_LIMIT)


def _sigmoid(x):
    return 1.0 / (1.0 + jnp.exp(-x))


def _dot_nt(a, b):
    return lax.dot_general(a, b, (((1,), (1,)), ((), ())), preferred_element_type=F32)


def _dot_tn(a, b):
    return lax.dot_general(a, b, (((0,), (0,)), ((), ())), preferred_element_type=F32)


HBM_SPEC = pl.BlockSpec(memory_space=pltpu.HBM)
SEM_SPEC = pl.BlockSpec(memory_space=pltpu.SEMAPHORE)
DATAFLOW = pltpu.SideEffectType.DATAFLOW_SIDE_EFFECTING


def _device_index():
    return 4 * lax.axis_index("x") + 2 * lax.axis_index("y") + lax.axis_index("c")


def _peers():
    x, y, c = lax.axis_index("x"), lax.axis_index("y"), lax.axis_index("c")
    out = []
    for k in range(1, N_DEV):
        px = 1 - x if k & 4 else x
        py = 1 - y if k & 2 else y
        pc = 1 - c if k & 1 else c
        out.append((k, (px, py, pc), 4 * px + 2 * py + pc))
    return out


def _peer_copy(per_dest, src_ref, land_ref, send_sems, recv_sems, a, k, dev, peer):
    src = src_ref.at[peer] if per_dest else src_ref
    return pltpu.make_async_remote_copy(
        src_ref=src, dst_ref=land_ref.at[_device_index()], send_sem=send_sems.at[a * (N_DEV - 1) + k - 1],
        recv_sem=recv_sems.at[a * (N_DEV - 1) + k - 1], device_id=dev, device_id_type=pl.DeviceIdType.MESH)


ALL_PEERS = tuple(range(1, N_DEV))
CHIP_PEERS = (1, 2, 4, 6)
FAR_PEERS = (2, 4, 6)


def _own_copy(per_dest, src_ref, land_ref, send_sems, n, a):
    me = _device_index()
    return pltpu.make_async_copy(src_ref.at[me] if per_dest else src_ref, land_ref.at[me],
                                 send_sems.at[n * (N_DEV - 1) + a])


def _exchange_start(name, items, ks=ALL_PEERS):
    n = len(items)
    per_dest = [it[1] for it in items]

    def body(*refs):
        srcs, lands = refs[:n], refs[n:2 * n]
        send_sems, recv_sems, token = refs[2 * n], refs[2 * n + 1], refs[-1]
        for a in range(n):
            _own_copy(per_dest[a], srcs[a], lands[a], send_sems, n, a).start()
            for k, dev, peer in _peers():
                if k in ks:
                    _peer_copy(per_dest[a], srcs[a], lands[a], send_sems, recv_sems, a, k, dev, peer).start()
        token[...] = jnp.zeros(token.shape, F32)

    srcs = [pltpu.with_memory_space_constraint(it[0], pltpu.HBM) for it in items]
    lands = []
    for arr, pd in items:
        shp = arr.shape if pd else (N_DEV,) + arr.shape
        lands.append(pltpu.with_memory_space_constraint(lax.empty(shp, arr.dtype), pltpu.HBM))
    sems = pltpu.SemaphoreType.DMA((n * N_DEV,))
    res = pl.pallas_call(
        body, name=name,
        out_shape=(sems, sems, *[pltpu.HBM(a_.shape, a_.dtype) for a_ in srcs + lands],
                   jax.ShapeDtypeStruct((8, 128), F32)),
        in_specs=[HBM_SPEC] * (2 * n),
        out_specs=(SEM_SPEC, SEM_SPEC, *[HBM_SPEC] * (2 * n), pl.BlockSpec(memory_space=pltpu.VMEM)),
        input_output_aliases={i: 2 + i for i in range(2 * n)},
        compiler_params=pltpu.CompilerParams(has_side_effects=DATAFLOW),
    )(*srcs, *lands)
    return dict(per_dest=per_dest, ks=ks, send=res[0], recv=res[1], srcs=list(res[2:2 + n]),
                lands=list(res[2 + n:2 + 2 * n]), token=res[-1])


def _exchange_wait(name, started, after):
    per_dest = started["per_dest"]
    n = len(per_dest)

    def body(*refs):
        srcs, lands = refs[:n], refs[n:2 * n]
        send_sems, recv_sems = refs[2 * n], refs[2 * n + 1]
        for a in range(n):
            _own_copy(per_dest[a], srcs[a], lands[a], send_sems, n, a).wait()
            for k, dev, peer in _peers():
                if k in started["ks"]:
                    cp = _peer_copy(per_dest[a], srcs[a], lands[a], send_sems, recv_sems, a, k, dev, peer)
                    cp.wait_send()
                    cp.wait_recv()

    bufs = started["srcs"] + started["lands"]
    after = list(after) if isinstance(after, (list, tuple)) else [after]
    res = pl.pallas_call(
        body, name=name, out_shape=tuple(pltpu.HBM(b_.shape, b_.dtype) for b_ in bufs),
        in_specs=[HBM_SPEC] * (2 * n) + [SEM_SPEC, SEM_SPEC] + [pl.BlockSpec(memory_space=pl.ANY)] * len(after),
        out_specs=tuple([HBM_SPEC] * (2 * n)), input_output_aliases={i: i for i in range(2 * n)},
        compiler_params=pltpu.CompilerParams(has_side_effects=DATAFLOW),
    )(*bufs, started["send"], started["recv"], *after)
    return list(res[n:])


def _pass_on_copy(land_ref, send_sems, recv_sems, a, idx, slot):
    sibling = (lax.axis_index("x"), lax.axis_index("y"), 1 - lax.axis_index("c"))
    return pltpu.make_async_remote_copy(
        src_ref=land_ref.at[slot], dst_ref=land_ref.at[slot], send_sem=send_sems.at[a * len(FAR_PEERS) + idx],
        recv_sem=recv_sems.at[a * len(FAR_PEERS) + idx], device_id=sibling, device_id_type=pl.DeviceIdType.MESH)


def _pass_on_start(name, lands):
    n = len(lands)

    def body(*refs):
        send_sems, recv_sems, token = refs[n], refs[n + 1], refs[-1]
        slots = {k: peer for k, _, peer in _peers()}
        for a in range(n):
            for idx, k in enumerate(FAR_PEERS):
                _pass_on_copy(refs[a], send_sems, recv_sems, a, idx, slots[k]).start()
        token[...] = jnp.zeros(token.shape, F32)

    lands = [pltpu.with_memory_space_constraint(l_, pltpu.HBM) for l_ in lands]
    sems = pltpu.SemaphoreType.DMA((n * len(FAR_PEERS),))
    res = pl.pallas_call(
        body, name=name,
        out_shape=(sems, sems, *[pltpu.HBM(l_.shape, l_.dtype) for l_ in lands], jax.ShapeDtypeStruct((8, 128), F32)),
        in_specs=[HBM_SPEC] * n, out_specs=(SEM_SPEC, SEM_SPEC, *[HBM_SPEC] * n, pl.BlockSpec(memory_space=pltpu.VMEM)),
        input_output_aliases={i: 2 + i for i in range(n)},
        compiler_params=pltpu.CompilerParams(has_side_effects=DATAFLOW),
    )(*lands)
    return dict(send=res[0], recv=res[1], lands=list(res[2:2 + n]), token=res[-1])


def _pass_on_wait(name, passed, after):
    n = len(passed["lands"])

    def body(*refs):
        send_sems, recv_sems = refs[n], refs[n + 1]
        slots = {k: peer for k, _, peer in _peers()}
        for a in range(n):
            for idx, k in enumerate(FAR_PEERS):
                _pass_on_copy(refs[a], send_sems, recv_sems, a, idx, slots[k]).wait_send()
                _pass_on_copy(refs[a], send_sems, recv_sems, a, idx, slots[k ^ 1]).wait_recv()

    res = pl.pallas_call(
        body, name=name, out_shape=tuple(pltpu.HBM(l_.shape, l_.dtype) for l_ in passed["lands"]),
        in_specs=[HBM_SPEC] * n + [SEM_SPEC, SEM_SPEC, pl.BlockSpec(memory_space=pl.ANY)],
        out_specs=tuple([HBM_SPEC] * n), input_output_aliases={i: i for i in range(n)},
        compiler_params=pltpu.CompilerParams(has_side_effects=DATAFLOW),
    )(*passed["lands"], passed["send"], passed["recv"], after)
    return list(res)


def _mm_nn(name, x, w, w_spec, n_out, tn, out_dtype, relu=False, square_x=False, tm=MM_TM):
    m, k = x.shape

    def body(x_ref, w_ref, *outs):
        xv = x_ref[...]
        if square_x:
            xv = xv * xv
        acc = jnp.dot(xv, w_ref[...], preferred_element_type=F32)
        if relu:
            acc = jnp.maximum(acc, 0.0)
        outs[0][...] = acc.astype(outs[0].dtype)

    o_spec = pl.BlockSpec((tm, tn), lambda i, j: (i, j))
    return pl.pallas_call(
        body, name=name, out_shape=jax.ShapeDtypeStruct((m, n_out), out_dtype), grid=(m // tm, n_out // tn),
        in_specs=[pl.BlockSpec((tm, k), lambda i, j: (i, 0)), w_spec],
        out_specs=o_spec, compiler_params=_params(("parallel", "parallel")),
    )(x, w)


def _mm_nt(name, pairs, m, n_out, tn, out_dtype, relu_bwd_of=None, tm=MM_TM, after=None):
    np_ = len(pairs)

    def body(*refs):
        acc = None
        for p in range(np_):
            if len(pairs[p]) == 5:
                d = jnp.dot(refs[2 * p][...], refs[2 * p + 1][...], preferred_element_type=F32)
            else:
                d = _dot_nt(refs[2 * p][...], refs[2 * p + 1][...])
            acc = d if acc is None else acc + d
        if relu_bwd_of is not None:
            acc = acc * (2.0 * refs[2 * np_][...].astype(F32))
        refs[-1][...] = acc.astype(refs[-1].dtype)

    o_spec = pl.BlockSpec((tm, tn), lambda i, j: (i, j))
    operands, specs = [], []
    for pair in pairs:
        operands += [pair[0], pair[2]]
        specs += [pair[1], pair[3]]
    if relu_bwd_of is not None:
        operands.append(relu_bwd_of)
        specs.append(o_spec)
    if after is not None:
        operands.append(after)
        specs.append(pl.BlockSpec((8, 128), lambda i, j: (0, 0)))
    return pl.pallas_call(
        body, name=name, out_shape=jax.ShapeDtypeStruct((m, n_out), out_dtype), grid=(m // tm, n_out // tn),
        in_specs=specs, out_specs=o_spec, compiler_params=_params(("parallel", "parallel")),
    )(*operands)


def _mm_tn(name, x, x_col, ta, dy, dy_col, tb, out_shape, out_spec, grid_ab):
    m = x.shape[0]

    def body(x_ref, dy_ref, o_ref):
        o_ref[...] = _dot_tn(x_ref[...], dy_ref[...]).astype(BF16)

    return pl.pallas_call(
        body, name=name, out_shape=jax.ShapeDtypeStruct(out_shape, BF16), grid=grid_ab,
        in_specs=[pl.BlockSpec((m, ta), lambda a, b: (0, x_col(a))),
                  pl.BlockSpec((m, tb), lambda a, b: (0, dy_col(b)))],
        out_specs=out_spec, compiler_params=_params(("parallel", "parallel")),
    )(x, dy)


def _w_cols(k, tn, off_blocks):
    return pl.BlockSpec((k, tn), lambda i, j: (0, off_blocks + j))


def _a_rows(kw, col_block=0, tm=MM_TM):
    return pl.BlockSpec((tm, kw), lambda i, j: (i, col_block))


def _w_rows(tn, kw, col_block=0):
    return pl.BlockSpec((tn, kw), lambda i, j: (j, col_block))


def _wt_rows(tn, off):
    return pl.BlockSpec((pl.Element(tn), pl.Element(SEG)), lambda i, j: (pl.multiple_of(off + tn * j, 16), 0))


def _wt_block(k, off, tn):
    return pl.BlockSpec((pl.Element(k), pl.Element(tn)), lambda i, j: (off, pl.multiple_of(tn * j, 128)))


GW_TILE = 512


def _grad_w_rows(name, xs, dy, square_x=False, gap=None):
    m, nb = dy.shape
    per = [x_.shape[1] // GW_TILE for x_ in xs]
    steps = sum(per)
    tiles = [(k, h) for k in range(len(xs)) for h in range(per[k])]
    gap_tile, gap_size = (gap[0] // GW_TILE, gap[1]) if gap else (steps, 0)

    def out_row(i):
        return pl.multiple_of(i * GW_TILE + jnp.where(i >= gap_tile, gap_size, 0), 16), 0

    def body(*refs):
        x_refs, dy_ref, o_ref, buf, sems = refs[:len(xs)], refs[len(xs)], refs[len(xs) + 1], refs[-2], refs[-1]
        s_ = pl.program_id(0)

        def fetch(step, slot):
            for idx, (k, h) in enumerate(tiles):
                @pl.when(step == idx)
                def _():
                    pltpu.make_async_copy(x_refs[k].at[:, pl.ds(h * GW_TILE, GW_TILE)], buf.at[slot], sems.at[slot]).start()

        @pl.when(s_ == 0)
        def _():
            fetch(s_, 0)

        @pl.when(s_ + 1 < steps)
        def _():
            fetch(s_ + 1, (s_ + 1) % 2)

        slot = s_ % 2
        pltpu.make_async_copy(x_refs[0].at[:, pl.ds(0, GW_TILE)], buf.at[slot], sems.at[slot]).wait()
        xv = buf[slot]
        if square_x:
            xv = xv * xv
        o_ref[...] = _dot_tn(xv, dy_ref[...]).astype(BF16)

    return pl.pallas_call(
        body, name=name, out_shape=jax.ShapeDtypeStruct((steps * GW_TILE + gap_size, nb), BF16), grid=(steps,),
        in_specs=[pl.BlockSpec(memory_space=pl.ANY)] * len(xs) + [pl.BlockSpec((m, nb), lambda i: (0, 0))],
        out_specs=pl.BlockSpec((pl.Element(GW_TILE), pl.Element(nb)), out_row),
        scratch_shapes=[pltpu.VMEM((2, m, GW_TILE), BF16), pltpu.SemaphoreType.DMA((2,))],
        compiler_params=_params(("arbitrary",)))(*xs, dy)


def _grad_w(name, x, dy):
    na, nb = x.shape[1], dy.shape[1]
    ta, tb = min(na, 1024), min(nb, 512)
    return _mm_tn(name, x, lambda a: a, ta, dy, lambda b: b, tb, (na, nb),
                  pl.BlockSpec((ta, tb), lambda a, b: (a, b)), (na // ta, nb // tb))


def _row_spec(width):
    return pl.BlockSpec((ROW_TILE, width), lambda i: (i, 0))


def _vec_spec(width):
    return pl.BlockSpec((1, width), lambda i: (0, 0))


def _rms_fwd(name, h, g, res=None):
    n, d = h.shape
    tile = ROW_TILE if n % ROW_TILE == 0 else n
    row_spec = pl.BlockSpec((tile, d), lambda i: (i, 0))

    def body(*refs):
        if res is None:
            h_ref, g_ref, hn_ref = refs
            hv = h_ref[...]
        else:
            h_ref, r_ref, g_ref, hs_ref, hn_ref = refs
            hv = h_ref[...] + r_ref[...]
            hs_ref[...] = hv
        r = lax.rsqrt(jnp.mean(hv * hv, axis=-1, keepdims=True) + RMS_EPS)
        hn_ref[...] = (hv * r * g_ref[...]).astype(BF16)

    ins = [h, g] if res is None else [h, res, g]
    in_specs = [row_spec, _vec_spec(d)] if res is None else [row_spec, row_spec, _vec_spec(d)]
    hn_shape = jax.ShapeDtypeStruct((n, d), BF16)
    if res is None:
        out_shape, out_specs = hn_shape, row_spec
    else:
        out_shape, out_specs = [jax.ShapeDtypeStruct((n, d), F32), hn_shape], [row_spec, row_spec]
    return pl.pallas_call(body, name=name, out_shape=out_shape, grid=(n // tile,), in_specs=in_specs,
                          out_specs=out_specs, compiler_params=_params(("parallel",)))(*ins)


def _rms_bwd(name, dhn, h, g, dres, batch, dh_dtype=F32, with_meta=False):
    n, d = h.shape
    t = n // batch
    nt = t // ROW_TILE

    def body(dhn_ref, h_ref, g_ref, dres_ref, *outs):
        first = (pl.program_id(0) == 0) & (pl.program_id(1) == 0)
        hv = h_ref[...]
        r = lax.rsqrt(jnp.mean(hv * hv, axis=-1, keepdims=True) + RMS_EPS)
        nrm = hv * r
        dhn = dhn_ref[...].astype(F32)
        dn = dhn * g_ref[...]
        dh = dres_ref[...].astype(F32) + r * (dn - nrm * jnp.mean(dn * nrm, axis=-1, keepdims=True))
        outs[0][...] = dh.astype(dh_dtype)
        dg_ref = outs[1]

        @pl.when(first)
        def _():
            dg_ref[...] = jnp.zeros(dg_ref.shape, F32)

        dg_ref[...] += jnp.sum(dhn * nrm, axis=0, keepdims=True)
        if with_meta:
            meta_ref = outs[2]

            @pl.when(first)
            def _():
                meta_ref[...] = jnp.zeros(meta_ref.shape, F32)

            @pl.when(pl.program_id(1) == 0)
            def _():
                meta_ref[...] += dh[0:N_META, :]

    row = pl.BlockSpec((ROW_TILE, d), lambda b, j: (b * nt + j, 0))
    vec = pl.BlockSpec((1, d), lambda b, j: (0, 0))
    shapes = [jax.ShapeDtypeStruct((n, d), dh_dtype), jax.ShapeDtypeStruct((1, d), F32)]
    specs = [row, vec]
    if with_meta:
        shapes.append(jax.ShapeDtypeStruct((N_META, d), F32))
        specs.append(pl.BlockSpec((N_META, d), lambda b, j: (0, 0)))
    return pl.pallas_call(body, name=name, out_shape=shapes, grid=(batch, nt), in_specs=[row, row, vec, row],
                          out_specs=specs, compiler_params=_params(("arbitrary", "arbitrary")))(dhn, h, g, dres)


def _final(name, h1, dn, tgt, g, batch):
    n, d = h1.shape
    t = n // batch
    nt = t // ROW_TILE

    def body(h1_ref, dn_ref, tgt_ref, g_ref, dhb_ref, loss_ref, dg_ref):
        first = (pl.program_id(0) == 0) & (pl.program_id(1) == 0)
        hv = h1_ref[...] + dn_ref[...]
        r = lax.rsqrt(jnp.mean(hv * hv, axis=-1, keepdims=True) + RMS_EPS)
        nrm = hv * r
        gv = g_ref[...]
        pos = pl.program_id(1) * ROW_TILE + lax.broadcasted_iota(jnp.int32, (ROW_TILE, 1), 0)
        diff = jnp.where(pos >= N_META, nrm * gv - tgt_ref[...], 0.0)
        dy = diff * (1.0 / d)

        @pl.when(first)
        def _():
            loss_ref[...] = jnp.zeros(loss_ref.shape, F32)
            dg_ref[...] = jnp.zeros(dg_ref.shape, F32)

        loss_ref[...] += jnp.full(loss_ref.shape, 0.5 / d, F32) * jnp.sum(diff * diff)
        dg_ref[...] += jnp.sum(dy * nrm, axis=0, keepdims=True)
        dng = dy * gv
        dh = r * (dng - nrm * jnp.mean(dng * nrm, axis=-1, keepdims=True))
        dhb_ref[...] = dh.astype(BF16)

    row = pl.BlockSpec((ROW_TILE, d), lambda b, j: (b * nt + j, 0))
    vec = pl.BlockSpec((1, d), lambda b, j: (0, 0))
    return pl.pallas_call(
        body, name=name, grid=(batch, nt), in_specs=[row, row, row, vec],
        out_shape=[jax.ShapeDtypeStruct((n, d), BF16), jax.ShapeDtypeStruct((8, 128), F32),
                   jax.ShapeDtypeStruct((1, d), F32)],
        out_specs=[row, pl.BlockSpec((8, 128), lambda b, j: (0, 0)), vec],
        compiler_params=_params(("arbitrary", "arbitrary")))(h1, dn, tgt, g)


def _ln_silu_fwd(name, c1, g, b):
    n, d = c1.shape

    def body(c_ref, g_ref, b_ref, o_ref):
        xv = c_ref[...]
        xc = xv - jnp.mean(xv, axis=-1, keepdims=True)
        rstd = lax.rsqrt(jnp.mean(xc * xc, axis=-1, keepdims=True) + LN_EPS)
        c2 = xc * rstd * g_ref[...] + b_ref[...]
        o_ref[...] = (c2 * _sigmoid(c2)).astype(BF16)

    return pl.pallas_call(body, name=name, out_shape=jax.ShapeDtypeStruct((n, d), BF16), grid=(n // ROW_TILE,),
                          in_specs=[_row_spec(d), _vec_spec(d), _vec_spec(d)], out_specs=_row_spec(d),
                          compiler_params=_params(("parallel",)))(c1, g, b)


def _ln_silu_bwd(name, dc3, c1, g, b):
    n, d = c1.shape

    def body(d_ref, c_ref, g_ref, b_ref, dc1_ref, dg_ref, db_ref):
        xv = c_ref[...]
        xc = xv - jnp.mean(xv, axis=-1, keepdims=True)
        rstd = lax.rsqrt(jnp.mean(xc * xc, axis=-1, keepdims=True) + LN_EPS)
        xh = xc * rstd
        c2 = xh * g_ref[...] + b_ref[...]
        s = _sigmoid(c2)
        dc2 = d_ref[...].astype(F32) * (s * (1.0 + c2 * (1.0 - s)))

        @pl.when(pl.program_id(0) == 0)
        def _():
            dg_ref[...] = jnp.zeros(dg_ref.shape, F32)
            db_ref[...] = jnp.zeros(db_ref.shape, F32)

        dg_ref[...] += jnp.sum(dc2 * xh, axis=0, keepdims=True)
        db_ref[...] += jnp.sum(dc2, axis=0, keepdims=True)
        dxh = dc2 * g_ref[...]
        dc1_ref[...] = rstd * (dxh - jnp.mean(dxh, axis=-1, keepdims=True)
                               - xh * jnp.mean(dxh * xh, axis=-1, keepdims=True))

    return pl.pallas_call(
        body, name=name, grid=(n // ROW_TILE,),
        out_shape=[jax.ShapeDtypeStruct((n, d), F32), jax.ShapeDtypeStruct((1, d), F32),
                   jax.ShapeDtypeStruct((1, d), F32)],
        in_specs=[_row_spec(d), _row_spec(d), _vec_spec(d), _vec_spec(d)],
        out_specs=[_row_spec(d), _vec_spec(d), _vec_spec(d)],
        compiler_params=_params(("arbitrary",)))(dc3, c1, g, b)


MERGE_TC = 512


def _merge_fwd(name, gates, a, c, b_co):
    n, d = a.shape
    nc = d // MERGE_TC

    def body(ga_ref, gc_ref, a_ref, c_ref, b_ref, m_ref):
        f32 = lambda r_: r_[...].astype(F32)
        m = _sigmoid(f32(ga_ref)) * f32(a_ref) + _sigmoid(f32(gc_ref)) * (f32(c_ref) + b_ref[...])
        m_ref[...] = m.astype(BF16)

    blk = lambda off: pl.BlockSpec((ROW_TILE, MERGE_TC), lambda i, j: (i, off + j))
    return pl.pallas_call(
        body, name=name, out_shape=jax.ShapeDtypeStruct((n, d), BF16), grid=(n // ROW_TILE, nc),
        in_specs=[blk(0), blk(nc), blk(0), blk(0), pl.BlockSpec((1, MERGE_TC), lambda i, j: (0, j))],
        out_specs=blk(0), compiler_params=_params(("parallel", "parallel")))(gates, gates, a, c, b_co)


def _merge_bwd(name, dm, gates, a, c, b_co):
    n, d = a.shape
    nc = d // MERGE_TC

    def body(dm_ref, ga_ref, gc_ref, a_ref, c_ref, b_ref, da_ref, dc_ref, dga_ref, dgc_ref, dbco_ref):
        f32 = lambda r_: r_[...].astype(F32)
        dmv = f32(dm_ref)
        sa, sc = _sigmoid(f32(ga_ref)), _sigmoid(f32(gc_ref))
        dc = dmv * sc
        da_ref[...] = (dmv * sa).astype(BF16)
        dc_ref[...] = dc.astype(BF16)
        dga_ref[...] = (dmv * f32(a_ref) * sa * (1.0 - sa)).astype(BF16)
        dgc_ref[...] = (dmv * (f32(c_ref) + b_ref[...]) * sc * (1.0 - sc)).astype(BF16)

        @pl.when(pl.program_id(1) == 0)
        def _():
            dbco_ref[...] = jnp.zeros(dbco_ref.shape, F32)

        dbco_ref[...] += jnp.sum(dc, axis=0, keepdims=True)

    blk = lambda off: pl.BlockSpec((ROW_TILE, MERGE_TC), lambda j, i: (i, off + j))
    vec = pl.BlockSpec((1, MERGE_TC), lambda j, i: (0, j))
    act = jax.ShapeDtypeStruct((n, d), BF16)
    return pl.pallas_call(
        body, name=name, grid=(nc, n // ROW_TILE),
        out_shape=[act, act, act, act, jax.ShapeDtypeStruct((1, d), F32)],
        in_specs=[blk(0), blk(0), blk(nc), blk(0), blk(0), vec],
        out_specs=[blk(0), blk(0), blk(0), blk(0), vec],
        compiler_params=_params(("parallel", "arbitrary")))(dm, gates, gates, a, c, b_co)


CONV_TC = 128
CONV_HALO = 32


def _conv_chunk(t):
    return 48 if t % 48 == 0 else 32 if t % 32 == 0 else 16


def _fold8(x):
    out = x[0:8]
    for k in range(1, x.shape[0] // 8):
        out = out + x[8 * k:8 * k + 8]
    return out


def _glu_conv_fwd(name, glu, b_glu, w_dw, b_dw, batch):
    n, c2 = glu.shape
    c = c2 // 2
    t = n // batch
    nc = c // CONV_TC

    def body(a_ref, gt_ref, ba_ref, bg_ref, w_ref, bdw_ref, o_ref, pad_ref):
        u = (a_ref[...].astype(F32) + ba_ref[...]) * _sigmoid(gt_ref[...].astype(F32) + bg_ref[...])
        pad_ref[0:CONV_HALO, :] = jnp.zeros((CONV_HALO, CONV_TC), F32)
        pad_ref[CONV_HALO:CONV_HALO + t, :] = u
        ch = _conv_chunk(t)
        for r0 in range(0, t, ch):
            acc = jnp.zeros((ch, CONV_TC), F32) + bdw_ref[...]
            for j in range(CONV_W):
                off = r0 + CONV_HALO - (CONV_W - 1) + j
                acc = acc + w_ref[j:j + 1, :] * pad_ref[off:off + ch, :]
            o_ref[r0:r0 + ch, :] = acc

    seq = lambda off: pl.BlockSpec((t, CONV_TC), lambda b, j: (b, off + j))
    vec = lambda off: pl.BlockSpec((1, CONV_TC), lambda b, j: (0, off + j))
    return pl.pallas_call(
        body, name=name, out_shape=jax.ShapeDtypeStruct((n, c), F32), grid=(batch, nc),
        in_specs=[seq(0), seq(nc), vec(0), vec(nc), pl.BlockSpec((CONV_W, CONV_TC), lambda b, j: (0, j)), vec(0)],
        out_specs=seq(0), scratch_shapes=[pltpu.VMEM((t + CONV_HALO, CONV_TC), F32)],
        compiler_params=_params(("parallel", "parallel")))(glu, glu, b_glu, b_glu, w_dw, b_dw)


def _glu_conv_bwd(name, dc1, glu, b_glu, w_dw, batch):
    n, c2 = glu.shape
    c = c2 // 2
    t = n // batch
    nc = c // CONV_TC

    def body(d_ref, a_ref, gt_ref, ba_ref, bg_ref, w_ref, dga_ref, dgg_ref, dw_ref, dbdw_ref, dba_ref, dbg_ref,
             padu_ref, padd_ref):
        av = a_ref[...].astype(F32) + ba_ref[...]
        sg = _sigmoid(gt_ref[...].astype(F32) + bg_ref[...])
        dc = d_ref[...]
        padu_ref[0:CONV_HALO, :] = jnp.zeros((CONV_HALO, CONV_TC), F32)
        padu_ref[CONV_HALO:CONV_HALO + t, :] = av * sg
        padd_ref[0:t, :] = dc
        padd_ref[t:t + CONV_HALO, :] = jnp.zeros((CONV_HALO, CONV_TC), F32)

        @pl.when(pl.program_id(1) == 0)
        def _():
            dw_ref[...] = jnp.zeros(dw_ref.shape, F32)
            dbdw_ref[...] = jnp.zeros(dbdw_ref.shape, F32)
            dba_ref[...] = jnp.zeros(dba_ref.shape, F32)
            dbg_ref[...] = jnp.zeros(dbg_ref.shape, F32)

        ch = _conv_chunk(t)
        zero8 = jnp.zeros((8, CONV_TC), F32)
        dw_acc = [zero8] * CONV_W
        sum_dc, sum_a, sum_g = zero8, zero8, zero8
        for r0 in range(0, t, ch):
            dcc = d_ref[r0:r0 + ch, :]
            du = jnp.zeros((ch, CONV_TC), F32)
            for j in range(CONV_W):
                back = r0 + CONV_W - 1 - j
                du = du + w_ref[j:j + 1, :] * padd_ref[back:back + ch, :]
                off = r0 + CONV_HALO - (CONV_W - 1) + j
                dw_acc[j] = dw_acc[j] + _fold8(dcc * padu_ref[off:off + ch, :])
            sgc = _sigmoid(gt_ref[r0:r0 + ch, :].astype(F32) + bg_ref[...])
            dga = du * sgc
            dgg = du * padu_ref[CONV_HALO + r0:CONV_HALO + r0 + ch, :] * (1.0 - sgc)
            dga_ref[r0:r0 + ch, :] = dga.astype(BF16)
            dgg_ref[r0:r0 + ch, :] = dgg.astype(BF16)
            sum_dc, sum_a, sum_g = sum_dc + _fold8(dcc), sum_a + _fold8(dga), sum_g + _fold8(dgg)
        for j in range(CONV_W):
            dw_ref[j:j + 1, :] += jnp.sum(dw_acc[j], axis=0, keepdims=True)
        dbdw_ref[...] += jnp.sum(sum_dc, axis=0, keepdims=True)
        dba_ref[...] += jnp.sum(sum_a, axis=0, keepdims=True)
        dbg_ref[...] += jnp.sum(sum_g, axis=0, keepdims=True)

    seq = lambda off: pl.BlockSpec((t, CONV_TC), lambda j, b: (b, off + j))
    vec = lambda off: pl.BlockSpec((1, CONV_TC), lambda j, b: (0, off + j))
    wsp = pl.BlockSpec((CONV_W, CONV_TC), lambda j, b: (0, j))
    act = jax.ShapeDtypeStruct((n, c), BF16)
    v = jax.ShapeDtypeStruct((1, c), F32)
    return pl.pallas_call(
        body, name=name, grid=(nc, batch),
        out_shape=[act, act, jax.ShapeDtypeStruct((CONV_W, c), F32), v, v, v],
        in_specs=[seq(0), seq(0), seq(nc), vec(0), vec(nc), wsp],
        out_specs=[seq(0), seq(0), wsp, vec(0), vec(0), vec(0)],
        scratch_shapes=[pltpu.VMEM((t + CONV_HALO, CONV_TC), F32), pltpu.VMEM((t + CONV_HALO, CONV_TC), F32)],
        compiler_params=_params(("parallel", "arbitrary")))(dc1, glu, glu, b_glu, b_glu, w_dw)


def _split3(x):
    hi = x.astype(BF16)
    r = x - hi.astype(F32)
    mid = r.astype(BF16)
    lo = (r - mid.astype(F32)).astype(BF16)
    return hi, mid, lo


def _tri_matmul(tri, x):
    hi, mid, lo = _split3(x)
    dot = lambda v: jnp.dot(tri, v, preferred_element_type=F32)
    return dot(hi) + dot(mid) + dot(lo)


def _fox_prep_fwd(name, fg, b_fg, batch):
    n, w = fg.shape
    t = n // batch
    nq = t // ROW_TILE

    def body(fg_ref, b_ref, cum_ref):
        row = lax.broadcasted_iota(jnp.int32, (ROW_TILE, ROW_TILE), 0)
        col = lax.broadcasted_iota(jnp.int32, (ROW_TILE, ROW_TILE), 1)
        tri = (row >= col).astype(BF16)
        for k in range(nq):
            rows = slice(k * ROW_TILE, (k + 1) * ROW_TILE)
            z = fg_ref[rows, :] + b_ref[...]
            logf = jnp.minimum(z, 0.0) - jnp.log(1.0 + jnp.exp(-jnp.abs(z)))
            cum = _tri_matmul(tri, logf)
            if k > 0:
                cum = cum + cum_ref[k * ROW_TILE - 1:k * ROW_TILE, :]
            cum_ref[rows, :] = cum

    seq = pl.BlockSpec((t, w), lambda b: (b, 0))
    return pl.pallas_call(body, name=name, out_shape=jax.ShapeDtypeStruct((n, w), F32), grid=(batch,),
                          in_specs=[seq, pl.BlockSpec((1, w), lambda b: (0, 0))], out_specs=seq,
                          compiler_params=_params(("parallel",)))(fg, b_fg)


def _fox_prep_bwd(name, dcum_k, dcum_q, fg, b_fg, batch):
    n, w = fg.shape
    t = n // batch
    nq = t // ROW_TILE

    def body(dk_ref, dq_ref, fg_ref, b_ref, dfg_ref, db_ref, rev_ref):
        row = lax.broadcasted_iota(jnp.int32, (ROW_TILE, ROW_TILE), 0)
        col = lax.broadcasted_iota(jnp.int32, (ROW_TILE, ROW_TILE), 1)
        tri = (col >= row).astype(BF16)

        @pl.when(pl.program_id(0) == 0)
        def _():
            db_ref[...] = jnp.zeros(db_ref.shape, F32)

        for k in reversed(range(nq)):
            rows = slice(k * ROW_TILE, (k + 1) * ROW_TILE)
            dlog = _tri_matmul(tri, dk_ref[rows, :] + dq_ref[rows, :])
            if k < nq - 1:
                dlog = dlog + rev_ref[(k + 1) * ROW_TILE:(k + 1) * ROW_TILE + 1, :]
            rev_ref[rows, :] = dlog
            dfg = dlog * _sigmoid(-(fg_ref[rows, :] + b_ref[...]))
            dfg_ref[rows, :] = dfg.astype(BF16)
            db_ref[...] += jnp.sum(dfg, axis=0, keepdims=True)

    seq = pl.BlockSpec((t, w), lambda b: (b, 0))
    vec = pl.BlockSpec((1, w), lambda b: (0, 0))
    return pl.pallas_call(
        body, name=name, grid=(batch,),
        out_shape=[jax.ShapeDtypeStruct((n, w), BF16), jax.ShapeDtypeStruct((1, w), F32)],
        in_specs=[seq, seq, seq, vec], out_specs=[seq, vec], scratch_shapes=[pltpu.VMEM((t, w), F32)],
        compiler_params=_params(("arbitrary",)))(dcum_k, dcum_q, fg, b_fg)


def _head_masks(x):
    lane = lax.broadcasted_iota(jnp.int32, x.shape, 1)
    zero = jnp.zeros(x.shape, x.dtype)
    return jnp.where(lane < HEAD_DIM, x, zero), jnp.where(lane >= HEAD_DIM, x, zero)


ATTN_BLOCK = 512


def _attn_blocks(t):
    nb = max(t // ATTN_BLOCK, 1)
    blocks = [(i * ATTN_BLOCK, ATTN_BLOCK) for i in range(nb - 1)]
    return blocks + [((nb - 1) * ATTN_BLOCK, t - (nb - 1) * ATTN_BLOCK)]


def _attn_specs(t):
    blocks = _attn_blocks(t)
    width = max(sz for _, sz in blocks)
    qkv = lambda off: pl.BlockSpec((t, LANES), lambda b, h: (b, off + h))
    cumr = pl.BlockSpec((None, None, len(blocks), 8, width), lambda b, h: (b, h, 0, 0, 0))
    return qkv, cumr


def _key_sums_to_blocks(cum, batch, t, n_pairs):
    blocks = _attn_blocks(t)
    width = max(sz for _, sz in blocks)
    cum_h = cum.reshape(batch, t, -1)[:, :, :2 * n_pairs].reshape(batch, t, n_pairs, 2)
    rows = [jnp.pad(jnp.transpose(cum_h[:, s0:s0 + sz], (0, 2, 3, 1)), ((0, 0), (0, 0), (0, 6), (0, width - sz)))
            for s0, sz in blocks]
    return jnp.stack(rows, axis=2)


def _key_sums_from_blocks(dcumr, batch, t, n_pairs):
    cols = [jnp.transpose(dcumr[:, :, j, :2, :sz], (0, 3, 1, 2)) for j, (_, sz) in enumerate(_attn_blocks(t))]
    return jnp.concatenate(cols, axis=1).reshape(batch * t, 2 * n_pairs)


def _causal(size):
    row = lax.broadcasted_iota(jnp.int32, (size, size), 0)
    col = lax.broadcasted_iota(jnp.int32, (size, size), 1)
    return row >= col


def _attn_fwd(name, qkv, cumr, batch):
    n, w3 = qkv.shape
    w = w3 // 3
    t = n // batch
    n_pairs = w // LANES
    blocks = _attn_blocks(t)

    def body(q_ref, k_ref, v_ref, cr_ref, o_ref, lse_ref):
        pair = pl.program_id(1)

        @pl.when(pair == 0)
        def _():
            lse_ref[...] = jnp.zeros(lse_ref.shape, F32)

        for i, (q0, qn) in enumerate(blocks):
            rows = slice(q0, q0 + qn)
            qs = _head_masks(q_ref[rows, :] * (0.125 * LOG2E))
            outs, lses = [], []
            for hh in range(2):
                m = jnp.full((qn, 1), NEG, F32)
                l = jnp.zeros((qn, 1), F32)
                acc = jnp.zeros((qn, LANES), F32)
                for j in range(i + 1):
                    k0, kn = blocks[j]
                    cols = slice(k0, k0 + kn)
                    s = _dot_nt(qs[hh], k_ref[cols, :]) - cr_ref[j, hh:hh + 1, 0:kn] * LOG2E
                    if j == i:
                        s = jnp.where(_causal(qn), s, NEG)
                    m_new = jnp.maximum(m, jnp.max(s, axis=1, keepdims=True))
                    alpha = jnp.exp2(m - m_new)
                    p = jnp.exp2(s - m_new)
                    l = alpha * l + jnp.sum(p, axis=1, keepdims=True)
                    acc = alpha * acc + jnp.dot(p.astype(BF16), v_ref[cols, :], preferred_element_type=F32)
                    m = m_new
                outs.append(acc / l)
                lses.append(m + jnp.log2(l))
            lane = lax.broadcasted_iota(jnp.int32, (qn, LANES), 1)
            o_ref[rows, :] = jnp.where(lane < HEAD_DIM, outs[0], outs[1]).astype(BF16)
            lse_ref[rows, :] = jnp.where(lane == 2 * pair, lses[0],
                                         jnp.where(lane == 2 * pair + 1, lses[1], lse_ref[rows, :]))

    qkv_spec, cumr_spec = _attn_specs(t)
    return pl.pallas_call(
        body, name=name, grid=(batch, n_pairs),
        out_shape=[jax.ShapeDtypeStruct((n, w), BF16), jax.ShapeDtypeStruct((n, LANES), F32)],
        in_specs=[qkv_spec(0), qkv_spec(n_pairs), qkv_spec(2 * n_pairs), cumr_spec],
        out_specs=[qkv_spec(0), pl.BlockSpec((t, LANES), lambda b, h: (b, 0))],
        compiler_params=_params(("parallel", "arbitrary")))(qkv, qkv, qkv, cumr)


def _attn_bwd(name, qkv, o, do, lse, cumr, batch):
    n, w3 = qkv.shape
    w = w3 // 3
    t = n // batch
    n_pairs = w // LANES
    blocks = _attn_blocks(t)

    def body(q_ref, k_ref, v_ref, o_ref, do_ref, lse_ref, cr_ref, dq_ref, dk_ref, dv_ref, dcr_ref, dcq_ref,
             dk_acc, dv_acc):
        pair = pl.program_id(1)
        dk_acc[...] = jnp.zeros(dk_acc.shape, F32)
        dv_acc[...] = jnp.zeros(dv_acc.shape, F32)
        dcr_ref[...] = jnp.zeros(dcr_ref.shape, F32)

        @pl.when(pair == 0)
        def _():
            dcq_ref[...] = jnp.zeros(dcq_ref.shape, F32)

        for i, (q0, qn) in enumerate(blocks):
            rows = slice(q0, q0 + qn)
            qs = _head_masks(q_ref[rows, :] * 0.125)
            q2 = _head_masks(q_ref[rows, :] * (0.125 * LOG2E))
            dos = _head_masks(do_ref[rows, :])
            dq = jnp.zeros((qn, LANES), F32)
            lane = lax.broadcasted_iota(jnp.int32, (qn, LANES), 1)
            dcq = []
            for hh in range(2):
                row_sum = jnp.zeros((qn, 1), F32)
                lse = jnp.sum(jnp.where(lane == 2 * pair + hh, lse_ref[rows, :], 0.0), axis=1, keepdims=True)
                delta = jnp.sum(dos[hh].astype(F32) * o_ref[rows, :].astype(F32), axis=1, keepdims=True)
                for j in range(i + 1):
                    k0, kn = blocks[j]
                    cols = slice(k0, k0 + kn)
                    s = _dot_nt(q2[hh], k_ref[cols, :]) - cr_ref[j, hh:hh + 1, 0:kn] * LOG2E
                    p = jnp.exp2(s - lse)
                    if j == i:
                        p = jnp.where(_causal(qn), p, 0.0)
                    dp = _dot_nt(dos[hh], v_ref[cols, :])
                    ds = p * (dp - delta)
                    pb, dsb = p.astype(BF16), ds.astype(BF16)
                    km = _head_masks(k_ref[cols, :])[hh]
                    dv_acc[j, :, 0:kn] += _dot_tn(dos[hh], pb)
                    dk_acc[j, :, 0:kn] += _dot_tn(qs[hh], dsb)
                    dq = dq + jnp.dot(dsb, km, preferred_element_type=F32)
                    dcr_ref[j, hh:hh + 1, 0:kn] -= jnp.sum(ds, axis=0, keepdims=True)
                    row_sum = row_sum + jnp.sum(ds, axis=1, keepdims=True)
                dcq.append(row_sum)
            dq_ref[rows, :] = (dq * 0.125).astype(BF16)
            dcq_ref[rows, :] = jnp.where(lane == 2 * pair, dcq[0],
                                         jnp.where(lane == 2 * pair + 1, dcq[1], dcq_ref[rows, :]))
        for j, (k0, kn) in enumerate(blocks):
            dk_ref[k0:k0 + kn, :] = dk_acc[j].T[0:kn, :].astype(BF16)
            dv_ref[k0:k0 + kn, :] = dv_acc[j].T[0:kn, :].astype(BF16)

    qkv_spec, cumr_spec = _attn_specs(t)
    wide = -(-max(sz for _, sz in blocks) // LANES) * LANES
    act = jax.ShapeDtypeStruct((n, w), BF16)
    return pl.pallas_call(
        body, name=name, grid=(batch, n_pairs),
        out_shape=[act, act, act, jax.ShapeDtypeStruct(cumr.shape, F32), jax.ShapeDtypeStruct((n, LANES), F32)],
        in_specs=[qkv_spec(0), qkv_spec(n_pairs), qkv_spec(2 * n_pairs), qkv_spec(0), qkv_spec(0),
                  pl.BlockSpec((t, LANES), lambda b, h: (b, 0)), cumr_spec],
        out_specs=[qkv_spec(0), qkv_spec(0), qkv_spec(0), cumr_spec, pl.BlockSpec((t, LANES), lambda b, h: (b, 0))],
        scratch_shapes=[pltpu.VMEM((len(blocks), LANES, wide), F32), pltpu.VMEM((len(blocks), LANES, wide), F32)],
        compiler_params=_params(("parallel", "arbitrary")))(qkv, qkv, qkv, o, do, lse, cumr)


def _adamw(name, parts, w, m, v):
    r, c = w.shape
    tr = 128 if r % 128 == 0 else r
    tc = 256 if tr > 128 and c % 256 == 0 else c
    c1 = 1.0 - ADAM_B1 ** ADAM_STEP
    c2 = 1.0 - ADAM_B2 ** ADAM_STEP

    def body(p_ref, w_ref, m_ref, v_ref, g_ref, d_ref, m2_ref, v2_ref):
        g = p_ref[0].astype(F32)
        for s in range(1, N_DEV):
            g = g + p_ref[s].astype(F32)
        m2 = ADAM_B1 * m_ref[...] + (1.0 - ADAM_B1) * g
        v2 = ADAM_B2 * v_ref[...] + (1.0 - ADAM_B2) * (g * g)
        g_ref[...] = g
        m2_ref[...] = m2
        v2_ref[...] = v2
        d_ref[...] = -ADAM_LR * ((m2 / c1) / (jnp.sqrt(v2 / c2) + ADAM_EPS) + ADAM_WD * w_ref[...])

    blk = pl.BlockSpec((tr, tc), lambda i, j: (i, j))
    shp = jax.ShapeDtypeStruct((r, c), F32)
    return pl.pallas_call(
        body, name=name, out_shape=[shp] * 4, grid=(r // tr, c // tc),
        in_specs=[pl.BlockSpec((N_DEV, tr, tc), lambda i, j: (0, i, j)), blk, blk, blk], out_specs=[blk] * 4,
        compiler_params=_params(("parallel", "parallel")))(parts, w, m, v)


def _cat_small(vals):
    parts = []
    for name in SMALL:
        v = vals[name].reshape(1, -1).astype(F32)
        parts.append(jnp.pad(v, ((0, 0), (0, SMALL_W[name] - v.shape[1]))))
    return jnp.concatenate(parts, axis=1)


def _split_small(row, shapes):
    out, off = {}, 0
    for name in SMALL:
        out[name] = row[0, off:off + SMALL_N[name]].reshape(shapes[name])
        off += SMALL_W[name]
    return out


def _cols_from_shards(g):
    return jnp.transpose(g, (1, 0, 2)).reshape(g.shape[1], N_DEV * g.shape[2])


def _shards_from_cols(a):
    r, c = a.shape
    return jnp.transpose(a.reshape(r, N_DEV, c // N_DEV), (1, 0, 2))


def kernel(x, meta_tokens, norm_mix_gain, w_in, b_forget, w_attn_out, b_glu, conv_dw_w, conv_dw_b, conv_ln_gain, conv_ln_bias, w_conv_out, b_conv_out, w_out, norm_mlp_gain, w_mlp_up, w_mlp_down, final_norm_gain, loss_target, m_meta_tokens, m_norm_mix_gain, m_w_in, m_b_forget, m_w_attn_out, m_b_glu, m_conv_dw_w, m_conv_dw_b, m_conv_ln_gain, m_conv_ln_bias, m_w_conv_out, m_b_conv_out, m_w_out, m_norm_mlp_gain, m_w_mlp_up, m_w_mlp_down, m_final_norm_gain, v_meta_tokens, v_norm_mix_gain, v_w_in, v_b_forget, v_w_attn_out, v_b_glu, v_conv_dw_w, v_conv_dw_b, v_conv_ln_gain, v_conv_ln_bias, v_w_conv_out, v_b_conv_out, v_w_out, v_norm_mlp_gain, v_w_mlp_up, v_w_mlp_down, v_final_norm_gain):
    weights = dict(meta_tokens=meta_tokens, norm_mix_gain=norm_mix_gain, w_in=w_in, b_forget=b_forget, w_attn_out=w_attn_out, b_glu=b_glu, conv_dw_w=conv_dw_w, conv_dw_b=conv_dw_b, conv_ln_gain=conv_ln_gain, conv_ln_bias=conv_ln_bias, w_conv_out=w_conv_out, b_conv_out=b_conv_out, w_out=w_out, norm_mlp_gain=norm_mlp_gain, w_mlp_up=w_mlp_up, w_mlp_down=w_mlp_down, final_norm_gain=final_norm_gain)
    mom_m = dict(meta_tokens=m_meta_tokens, norm_mix_gain=m_norm_mix_gain, w_in=m_w_in, b_forget=m_b_forget, w_attn_out=m_w_attn_out, b_glu=m_b_glu, conv_dw_w=m_conv_dw_w, conv_dw_b=m_conv_dw_b, conv_ln_gain=m_conv_ln_gain, conv_ln_bias=m_conv_ln_bias, w_conv_out=m_w_conv_out, b_conv_out=m_b_conv_out, w_out=m_w_out, norm_mlp_gain=m_norm_mlp_gain, w_mlp_up=m_w_mlp_up, w_mlp_down=m_w_mlp_down, final_norm_gain=m_final_norm_gain)
    mom_v = dict(meta_tokens=v_meta_tokens, norm_mix_gain=v_norm_mix_gain, w_in=v_w_in, b_forget=v_b_forget, w_attn_out=v_w_attn_out, b_glu=v_b_glu, conv_dw_w=v_conv_dw_w, conv_dw_b=v_conv_dw_b, conv_ln_gain=v_conv_ln_gain, conv_ln_bias=v_conv_ln_bias, w_conv_out=v_w_conv_out, b_conv_out=v_b_conv_out, w_out=v_w_out, norm_mlp_gain=v_norm_mlp_gain, w_mlp_up=v_w_mlp_up, w_mlp_down=v_w_mlp_down, final_norm_gain=v_final_norm_gain)
    names = list(weights)
    batch, seq, d = x.shape
    t = seq + N_META
    n = batch * t
    n_pairs = d // LANES
    assert t % ROW_TILE == 0 and d == SEG

    to_rows = lambda w3: jnp.transpose(w3[0])
    w_in_t, m_in_t, v_in_t = to_rows(w_in), to_rows(m_w_in), to_rows(v_w_in)
    first = [w_in_t.astype(BF16), meta_tokens, conv_dw_w[0]]
    rest = [w_[0].astype(BF16) for w_ in (w_attn_out, w_conv_out, w_out, w_mlp_up, w_mlp_down)]
    tgt = jnp.concatenate([jnp.zeros((batch, N_META, d), F32), loss_target], axis=1).reshape(n, d)
    h0_rows = jnp.pad(x, ((0, 0), (N_META, 0), (0, 0)))
    g1 = norm_mix_gain.reshape(1, -1)
    gather_a = _exchange_start("gather_in_start", [(f_, False) for f_ in first], ks=CHIP_PEERS)
    level_1 = _exchange_wait("gather_in_wait", gather_a, [gather_a["token"], tgt, h0_rows, w_in_t, m_in_t, v_in_t] + rest)
    passed = _pass_on_start("gather_in_pass_start", level_1)
    w_in_g, meta_g, w_dw_g = _pass_on_wait("gather_in_pass_wait", passed, passed["token"])
    gather_b = _exchange_start("gather_rest_start", [(r_, False) for r_ in rest])
    n_fg = b_forget.shape[1]
    shard_w = w_in.shape[2]
    wt = w_in_g.reshape(N_DEV * shard_w, d)
    o_fg = 3 * SEG
    seg_rows = [0, SEG, 2 * SEG] + [o_fg + n_fg + i * SEG for i in range(4)]
    d_ff = w_mlp_down.shape[1] * N_DEV
    ff_blk = d_ff // N_DEV
    meta_f = _cols_from_shards(meta_g)
    w_dw = _cols_from_shards(w_dw_g)

    row2 = lambda v: v.reshape(1, -1)
    g2, g3 = row2(norm_mlp_gain), row2(final_norm_gain)
    b_fg = jnp.pad(b_forget, ((0, 0), (0, FG_PAD - n_fg)))
    h0 = lax.dynamic_update_slice(h0_rows, jnp.broadcast_to(meta_f[None], (batch, N_META, d)), (0, 0, 0)).reshape(n, d)

    hn1 = _rms_fwd("rms1", h0, g1)
    proj = lambda name, off, width, tn, dt: _mm_nt(name, [(hn1, _a_rows(d), wt, _wt_rows(tn, off))], n, width, tn, dt,
                                                   after=gather_b["token"])
    qkv = proj("proj_qkv", 0, 3 * SEG, SEG, BF16)
    glu = proj("proj_glu", seg_rows[3], 2 * SEG, SEG, BF16)
    gates = proj("proj_gates", seg_rows[5], 2 * SEG, SEG, BF16)
    fg = proj("proj_fg", o_fg, FG_PAD, FG_PAD, F32)

    cum = _fox_prep_fwd("fox_cumsum", fg, b_fg, batch)
    cumr = _key_sums_to_blocks(cum, batch, t, n_pairs)
    o, lse = _attn_fwd("attn_fwd", qkv, cumr, batch)
    rest = _exchange_wait("gather_rest_wait", gather_b, o)
    w_ao, w_co, w_o = [r_.reshape(d, d) for r_ in rest[:3]]
    w_up = rest[3]
    w_dn = rest[4].reshape(d_ff, d)
    a = _mm_nn("attn_out", o, w_ao, _w_cols(d, d, 0), d, d, BF16)

    c1 = _glu_conv_fwd("glu_conv", glu, b_glu, w_dw, conv_dw_b, batch)
    c3 = _ln_silu_fwd("ln_silu", c1, conv_ln_gain, conv_ln_bias)
    c = _mm_nn("conv_out", c3, w_co, _w_cols(d, d, 0), d, d, BF16)

    mrg = _merge_fwd("merge", gates, a, c, b_conv_out)
    mo = _mm_nn("mix_out", mrg, w_o, _w_cols(d, d, 0), d, d, F32)
    h1, hn2 = _rms_fwd("resid_rms2", h0, g2, res=mo)
    per = ff_blk // 512
    r_up = _mm_nn("mlp_up", hn2, w_up, pl.BlockSpec((None, d, 512), lambda i, j: (j // per, 0, j % per)),
                  d_ff, 512, BF16, relu=True)
    dn = _mm_nn("mlp_down", r_up, w_dn, _w_cols(d_ff, d, 0), d, d, F32, square_x=True, tm=ROW_TILE)
    dh2b, loss_blk, dg3 = _final("final_loss", h1, dn, tgt, g3, batch)

    dup = _mm_nt("d_mlp_down", [(dh2b, _a_rows(d), w_dn, _w_rows(d, d))], n, d_ff, d, BF16, relu_bwd_of=r_up)
    dw_dn = _grad_w_rows("gw_mlp_down", [r_up], dh2b, square_x=True)
    dhn2 = _mm_nt("d_mlp_up", [(dup, _a_rows(ff_blk, g), w_up, pl.BlockSpec((None, 512, ff_blk), lambda i, j, g=g: (g, j, 0)))
                               for g in range(N_DEV)], n, d, 512, BF16)
    dw_up = _mm_tn("gw_mlp_up", hn2, lambda a_: 0, d, dup, lambda b_: b_, ff_blk, (N_DEV, d, ff_blk),
                   pl.BlockSpec((None, d, ff_blk), lambda a_, b_: (b_, 0, 0)), (1, N_DEV))
    scatter_1 = _exchange_start("scatter_mlp_start", [(dw_dn.reshape(N_DEV, ff_blk, d), True), (dw_up, True)])
    dh1b, dg2 = _rms_bwd("rms2_bwd", dhn2, h1, g2 + scatter_1["token"][0:1, 0:1], dh2b, batch, dh_dtype=BF16)

    dm = _mm_nt("d_mix_out", [(dh1b, _a_rows(d), w_o, _w_rows(d, d))], n, d, d, BF16)
    dw_o = _grad_w("gw_mix_out", mrg, dh1b)
    da, dc, dga, dgc, dbco = _merge_bwd("merge_bwd", dm, gates, a, c, b_conv_out)

    do = _mm_nt("d_attn_out", [(da, _a_rows(d), w_ao, _w_rows(d, d))], n, d, d, BF16)
    dw_ao = _grad_w("gw_attn_out", o, da)
    dc3 = _mm_nt("d_conv_out", [(dc, _a_rows(d), w_co, _w_rows(d, d))], n, d, d, BF16)
    dw_co = _grad_w("gw_conv_out", c3, dc)

    scatter_2 = _exchange_start("scatter_mix_start", [(dw_.reshape(N_DEV, d // N_DEV, d), True)
                                                      for dw_ in (dw_o, dw_ao, dw_co)])
    dc1, dg_ln, db_ln = _ln_silu_bwd("ln_silu_bwd", dc3, c1, conv_ln_gain + scatter_2["token"][0:1, 0:1],
                                     conv_ln_bias)
    dglu_a, dglu_g, dw_dw, db_dw, dbg_a, dbg_g = _glu_conv_bwd("glu_conv_bwd", dc1, glu, b_glu, w_dw, batch)

    dq, dk, dv, dcumr, dcum_q = _attn_bwd("attn_bwd", qkv, o, do, lse, cumr, batch)
    dcum_k = jnp.pad(_key_sums_from_blocks(dcumr, batch, t, n_pairs), ((0, 0), (0, FG_PAD - 2 * n_pairs)))
    dfg, db_fg = _fox_prep_bwd("fox_cumsum_bwd", dcum_k, dcum_q, fg, b_fg, batch)

    segs = [dq, dk, dv, dglu_a, dglu_g, dga, dgc]
    gw_t = _grad_w_rows("gw_in", segs, hn1, gap=(o_fg, n_fg))
    gw_fg = _grad_w("gw_in_fg", dfg, hn1)[:n_fg]
    dw_in_t = lax.dynamic_update_slice(gw_t, gw_fg, (o_fg, 0)).reshape(N_DEV, shard_w, d)
    scatter_3 = _exchange_start("scatter_in_start", [(dw_in_t, True)])
    pairs = [(s_, _a_rows(SEG, 0, ROW_TILE), wt, _wt_block(SEG, seg_rows[i], 512), "nn") for i, s_ in enumerate(segs)]
    pairs.append((dfg, _a_rows(FG_PAD, 0, ROW_TILE), wt, _wt_block(FG_PAD, o_fg, 512), "nn"))
    dhn1 = _mm_nt("d_proj_in", pairs, n, d, 512, BF16, tm=ROW_TILE, after=scatter_3["token"])
    dh0, dg1, dmeta = _rms_bwd("rms1_bwd", dhn1, h0, g1, dh1b, batch, with_meta=True)
    grad_x = dh0.reshape(batch, t, d)[:, N_META:, :]

    small_g = dict(norm_mix_gain=dg1, b_forget=db_fg[:, :n_fg], b_glu=jnp.concatenate([dbg_a, dbg_g], axis=1),
                   conv_dw_b=db_dw, conv_ln_gain=dg_ln, conv_ln_bias=db_ln, b_conv_out=dbco, norm_mlp_gain=dg2,
                   final_norm_gain=dg3)
    scatter_4 = _exchange_start("scatter_small_start", [
        (_shards_from_cols(dmeta), True), (_shards_from_cols(dw_dw), True), (_cat_small(small_g), False),
        (loss_blk[0:1, :], False)])

    grads, deltas, new_m, new_v = {}, {}, {}, {}

    def update(k, parts):
        shp = weights[k].shape
        if k == "w_in":
            res_ = _adamw("adamw_" + k, parts, w_in_t, m_in_t, v_in_t)
            res_ = [jnp.transpose(r) for r in res_]
        else:
            w2 = lambda arr: arr.reshape(parts.shape[1:])
            res_ = _adamw("adamw_" + k, parts, w2(weights[k]), w2(mom_m[k]), w2(mom_v[k]))
        grads[k], deltas[k], new_m[k], new_v[k] = [r.reshape(shp) for r in res_]

    for k, parts in zip(("w_mlp_down", "w_mlp_up"), _exchange_wait("scatter_mlp_wait", scatter_1, scatter_4["token"])):
        update(k, parts)
    for k, parts in zip(("w_out", "w_attn_out", "w_conv_out"),
                        _exchange_wait("scatter_mix_wait", scatter_2, deltas["w_mlp_up"])):
        update(k, parts)
    update("w_in", _exchange_wait("scatter_in_wait", scatter_3, deltas["w_conv_out"])[0])
    reduced = _exchange_wait("scatter_small_wait", scatter_4, deltas["w_in"])
    loss = jnp.sum(reduced.pop()[:, 0, 0])
    for k, parts in zip(("meta_tokens", "conv_dw_w"), reduced[:-1]):
        update(k, parts)
    res = _adamw("adamw_small", reduced[-1], _cat_small(weights), _cat_small(mom_m), _cat_small(mom_v))
    shapes = {k: weights[k].shape for k in SMALL}
    for dst, r in zip((grads, deltas, new_m, new_v), res):
        dst.update(_split_small(r, shapes))

    return (loss, grad_x, *[grads[k] for k in names], *[deltas[k] for k in names],
            *[new_m[k] for k in names], *[new_v[k] for k in names])
```

```python
import functools

import jax
import jax.numpy as jnp
from jax import lax
from jax.experimental import pallas as pl
from jax.experimental.pallas import tpu as pltpu

F32, BF16 = jnp.float32, jnp.bfloat16
N_DEV = 8
N_META = 16
HEAD_DIM = 64
LANES = 128
CONV_W = 31
RMS_EPS = 1e-6
LN_EPS = 1e-5
ROW_TILE = 688
MM_TM = 2 * ROW_TILE
SEG = 1024
FG_PAD = 128
VMEM_LIMIT = 56 * 1024 * 1024
ADAM_LR, ADAM_B1, ADAM_B2, ADAM_EPS, ADAM_WD, ADAM_STEP = 0.001, 0.9, 0.999, 1e-08, 0.01, 10
NEG = -1e30
LOG2E = 1.4426950408889634

SMALL = ("norm_mix_gain", "b_forget", "b_glu", "conv_dw_b", "conv_ln_gain", "conv_ln_bias", "b_conv_out",
         "norm_mlp_gain", "final_norm_gain")
SMALL_W = {"norm_mix_gain": 1024, "b_forget": 128, "b_glu": 2048, "conv_dw_b": 1024, "conv_ln_gain": 1024,
           "conv_ln_bias": 1024, "b_conv_out": 1024, "norm_mlp_gain": 1024, "final_norm_gain": 1024}
SMALL_N = {"norm_mix_gain": 1024, "b_forget": 16, "b_glu": 2048, "conv_dw_b": 1024, "conv_ln_gain": 1024,
           "conv_ln_bias": 1024, "b_conv_out": 1024, "norm_mlp_gain": 1024, "final_norm_gain": 1024}


def _params(sem=None):
    return pltpu.CompilerParams(dimension_semantics=sem, vmem_limit_bytes=VMEM_LIMIT)


def _sigmoid(x):
    return 1.0 / (1.0 + jnp.exp(-x))


def _dot_nt(a, b):
    return lax.dot_general(a, b, (((1,), (1,)), ((), ())), preferred_element_type=F32)


def _dot_tn(a, b):
    return lax.dot_general(a, b, (((0,), (0,)), ((), ())), preferred_element_type=F32)


HBM_SPEC = pl.BlockSpec(memory_space=pltpu.HBM)
SEM_SPEC = pl.BlockSpec(memory_space=pltpu.SEMAPHORE)
DATAFLOW = pltpu.SideEffectType.DATAFLOW_SIDE_EFFECTING


def _device_index():
    return 4 * lax.axis_index("x") + 2 * lax.axis_index("y") + lax.axis_index("c")


def _peers():
    x, y, c = lax.axis_index("x"), lax.axis_index("y"), lax.axis_index("c")
    out = []
    for k in range(1, N_DEV):
        px = 1 - x if k & 4 else x
        py = 1 - y if k & 2 else y
        pc = 1 - c if k & 1 else c
        out.append((k, (px, py, pc), 4 * px + 2 * py + pc))
    return out


def _peer_copy(per_dest, src_ref, land_ref, send_sems, recv_sems, a, k, dev, peer):
    src = src_ref.at[peer] if per_dest else src_ref
    return pltpu.make_async_remote_copy(
        src_ref=src, dst_ref=land_ref.at[_device_index()], send_sem=send_sems.at[a * (N_DEV - 1) + k - 1],
        recv_sem=recv_sems.at[a * (N_DEV - 1) + k - 1], device_id=dev, device_id_type=pl.DeviceIdType.MESH)


ALL_PEERS = tuple(range(1, N_DEV))
CHIP_PEERS = (1, 2, 4, 6)
FAR_PEERS = (2, 4, 6)


def _own_copy(per_dest, src_ref, land_ref, send_sems, n, a):
    me = _device_index()
    return pltpu.make_async_copy(src_ref.at[me] if per_dest else src_ref, land_ref.at[me],
                                 send_sems.at[n * (N_DEV - 1) + a])


def _exchange_start(name, items, ks=ALL_PEERS):
    n = len(items)
    per_dest = [it[1] for it in items]

    def body(*refs):
        srcs, lands = refs[:n], refs[n:2 * n]
        send_sems, recv_sems, token = refs[2 * n], refs[2 * n + 1], refs[-1]
        for a in range(n):
            _own_copy(per_dest[a], srcs[a], lands[a], send_sems, n, a).start()
            for k, dev, peer in _peers():
                if k in ks:
                    _peer_copy(per_dest[a], srcs[a], lands[a], send_sems, recv_sems, a, k, dev, peer).start()
        token[...] = jnp.zeros(token.shape, F32)

    srcs = [pltpu.with_memory_space_constraint(it[0], pltpu.HBM) for it in items]
    lands = []
    for arr, pd in items:
        shp = arr.shape if pd else (N_DEV,) + arr.shape
        lands.append(pltpu.with_memory_space_constraint(lax.empty(shp, arr.dtype), pltpu.HBM))
    sems = pltpu.SemaphoreType.DMA((n * N_DEV,))
    res = pl.pallas_call(
        body, name=name,
        out_shape=(sems, sems, *[pltpu.HBM(a_.shape, a_.dtype) for a_ in srcs + lands],
                   jax.ShapeDtypeStruct((8, 128), F32)),
        in_specs=[HBM_SPEC] * (2 * n),
        out_specs=(SEM_SPEC, SEM_SPEC, *[HBM_SPEC] * (2 * n), pl.BlockSpec(memory_space=pltpu.VMEM)),
        input_output_aliases={i: 2 + i for i in range(2 * n)},
        compiler_params=pltpu.CompilerParams(has_side_effects=DATAFLOW),
    )(*srcs, *lands)
    return dict(per_dest=per_dest, ks=ks, send=res[0], recv=res[1], srcs=list(res[2:2 + n]),
                lands=list(res[2 + n:2 + 2 * n]), token=res[-1])


def _exchange_wait(name, started, after):
    per_dest = started["per_dest"]
    n = len(per_dest)

    def body(*refs):
        srcs, lands = refs[:n], refs[n:2 * n]
        send_sems, recv_sems = refs[2 * n], refs[2 * n + 1]
        for a in range(n):
            _own_copy(per_dest[a], srcs[a], lands[a], send_sems, n, a).wait()
            for k, dev, peer in _peers():
                if k in started["ks"]:
                    cp = _peer_copy(per_dest[a], srcs[a], lands[a], send_sems, recv_sems, a, k, dev, peer)
                    cp.wait_send()
                    cp.wait_recv()

    bufs = started["srcs"] + started["lands"]
    after = list(after) if isinstance(after, (list, tuple)) else [after]
    res = pl.pallas_call(
        body, name=name, out_shape=tuple(pltpu.HBM(b_.shape, b_.dtype) for b_ in bufs),
        in_specs=[HBM_SPEC] * (2 * n) + [SEM_SPEC, SEM_SPEC] + [pl.BlockSpec(memory_space=pl.ANY)] * len(after),
        out_specs=tuple([HBM_SPEC] * (2 * n)), input_output_aliases={i: i for i in range(2 * n)},
        compiler_params=pltpu.CompilerParams(has_side_effects=DATAFLOW),
    )(*bufs, started["send"], started["recv"], *after)
    return list(res[n:])


def _pass_on_copy(land_ref, send_sems, recv_sems, a, idx, slot):
    sibling = (lax.axis_index("x"), lax.axis_index("y"), 1 - lax.axis_index("c"))
    return pltpu.make_async_remote_copy(
        src_ref=land_ref.at[slot], dst_ref=land_ref.at[slot], send_sem=send_sems.at[a * len(FAR_PEERS) + idx],
        recv_sem=recv_sems.at[a * len(FAR_PEERS) + idx], device_id=sibling, device_id_type=pl.DeviceIdType.MESH)


def _pass_on_start(name, lands):
    n = len(lands)

    def body(*refs):
        send_sems, recv_sems, token = refs[n], refs[n + 1], refs[-1]
        slots = {k: peer for k, _, peer in _peers()}
        for a in range(n):
            for idx, k in enumerate(FAR_PEERS):
                _pass_on_copy(refs[a], send_sems, recv_sems, a, idx, slots[k]).start()
        token[...] = jnp.zeros(token.shape, F32)

    lands = [pltpu.with_memory_space_constraint(l_, pltpu.HBM) for l_ in lands]
    sems = pltpu.SemaphoreType.DMA((n * len(FAR_PEERS),))
    res = pl.pallas_call(
        body, name=name,
        out_shape=(sems, sems, *[pltpu.HBM(l_.shape, l_.dtype) for l_ in lands], jax.ShapeDtypeStruct((8, 128), F32)),
        in_specs=[HBM_SPEC] * n, out_specs=(SEM_SPEC, SEM_SPEC, *[HBM_SPEC] * n, pl.BlockSpec(memory_space=pltpu.VMEM)),
        input_output_aliases={i: 2 + i for i in range(n)},
        compiler_params=pltpu.CompilerParams(has_side_effects=DATAFLOW),
    )(*lands)
    return dict(send=res[0], recv=res[1], lands=list(res[2:2 + n]), token=res[-1])


def _pass_on_wait(name, passed, after):
    n = len(passed["lands"])

    def body(*refs):
        send_sems, recv_sems = refs[n], refs[n + 1]
        slots = {k: peer for k, _, peer in _peers()}
        for a in range(n):
            for idx, k in enumerate(FAR_PEERS):
                _pass_on_copy(refs[a], send_sems, recv_sems, a, idx, slots[k]).wait_send()
                _pass_on_copy(refs[a], send_sems, recv_sems, a, idx, slots[k ^ 1]).wait_recv()

    res = pl.pallas_call(
        body, name=name, out_shape=tuple(pltpu.HBM(l_.shape, l_.dtype) for l_ in passed["lands"]),
        in_specs=[HBM_SPEC] * n + [SEM_SPEC, SEM_SPEC, pl.BlockSpec(memory_space=pl.ANY)],
        out_specs=tuple([HBM_SPEC] * n), input_output_aliases={i: i for i in range(n)},
        compiler_params=pltpu.CompilerParams(has_side_effects=DATAFLOW),
    )(*passed["lands"], passed["send"], passed["recv"], after)
    return list(res)


def _mm_nn(name, x, w, w_spec, n_out, tn, out_dtype, relu=False, square_x=False, tm=MM_TM):
    m, k = x.shape

    def body(x_ref, w_ref, *outs):
        xv = x_ref[...]
        if square_x:
            xv = xv * xv
        acc = jnp.dot(xv, w_ref[...], preferred_element_type=F32)
        if relu:
            acc = jnp.maximum(acc, 0.0)
        outs[0][...] = acc.astype(outs[0].dtype)

    o_spec = pl.BlockSpec((tm, tn), lambda i, j: (i, j))
    return pl.pallas_call(
        body, name=name, out_shape=jax.ShapeDtypeStruct((m, n_out), out_dtype), grid=(m // tm, n_out // tn),
        in_specs=[pl.BlockSpec((tm, k), lambda i, j: (i, 0)), w_spec],
        out_specs=o_spec, compiler_params=_params(("parallel", "parallel")),
    )(x, w)


def _mm_nt(name, pairs, m, n_out, tn, out_dtype, relu_bwd_of=None, tm=MM_TM, after=None):
    np_ = len(pairs)

    def body(*refs):
        acc = None
        for p in range(np_):
            if len(pairs[p]) == 5:
                d = jnp.dot(refs[2 * p][...], refs[2 * p + 1][...], preferred_element_type=F32)
            else:
                d = _dot_nt(refs[2 * p][...], refs[2 * p + 1][...])
            acc = d if acc is None else acc + d
        if relu_bwd_of is not None:
            acc = acc * (2.0 * refs[2 * np_][...].astype(F32))
        refs[-1][...] = acc.astype(refs[-1].dtype)

    o_spec = pl.BlockSpec((tm, tn), lambda i, j: (i, j))
    operands, specs = [], []
    for pair in pairs:
        operands += [pair[0], pair[2]]
        specs += [pair[1], pair[3]]
    if relu_bwd_of is not None:
        operands.append(relu_bwd_of)
        specs.append(o_spec)
    if after is not None:
        operands.append(after)
        specs.append(pl.BlockSpec((8, 128), lambda i, j: (0, 0)))
    return pl.pallas_call(
        body, name=name, out_shape=jax.ShapeDtypeStruct((m, n_out), out_dtype), grid=(m // tm, n_out // tn),
        in_specs=specs, out_specs=o_spec, compiler_params=_params(("parallel", "parallel")),
    )(*operands)


def _mm_tn(name, x, x_col, ta, dy, dy_col, tb, out_shape, out_spec, grid_ab):
    m = x.shape[0]

    def body(x_ref, dy_ref, o_ref):
        o_ref[...] = _dot_tn(x_ref[...], dy_ref[...]).astype(BF16)

    return pl.pallas_call(
        body, name=name, out_shape=jax.ShapeDtypeStruct(out_shape, BF16), grid=grid_ab,
        in_specs=[pl.BlockSpec((m, ta), lambda a, b: (0, x_col(a))),
                  pl.BlockSpec((m, tb), lambda a, b: (0, dy_col(b)))],
        out_specs=out_spec, compiler_params=_params(("parallel", "parallel")),
    )(x, dy)


def _w_cols(k, tn, off_blocks):
    return pl.BlockSpec((k, tn), lambda i, j: (0, off_blocks + j))


def _a_rows(kw, col_block=0, tm=MM_TM):
    return pl.BlockSpec((tm, kw), lambda i, j: (i, col_block))


def _w_rows(tn, kw, col_block=0):
    return pl.BlockSpec((tn, kw), lambda i, j: (j, col_block))


def _wt_rows(tn, off):
    return pl.BlockSpec((pl.Element(tn), pl.Element(SEG)), lambda i, j: (pl.multiple_of(off + tn * j, 16), 0))


def _wt_block(k, off, tn):
    return pl.BlockSpec((pl.Element(k), pl.Element(tn)), lambda i, j: (off, pl.multiple_of(tn * j, 128)))


GW_TILE = 512


def _grad_w_rows(name, xs, dy, square_x=False, gap=None):
    m, nb = dy.shape
    per = [x_.shape[1] // GW_TILE for x_ in xs]
    steps = sum(per)
    tiles = [(k, h) for k in range(len(xs)) for h in range(per[k])]
    gap_tile, gap_size = (gap[0] // GW_TILE, gap[1]) if gap else (steps, 0)

    def out_row(i):
        return pl.multiple_of(i * GW_TILE + jnp.where(i >= gap_tile, gap_size, 0), 16), 0

    def body(*refs):
        x_refs, dy_ref, o_ref, buf, sems = refs[:len(xs)], refs[len(xs)], refs[len(xs) + 1], refs[-2], refs[-1]
        s_ = pl.program_id(0)

        def fetch(step, slot):
            for idx, (k, h) in enumerate(tiles):
                @pl.when(step == idx)
                def _():
                    pltpu.make_async_copy(x_refs[k].at[:, pl.ds(h * GW_TILE, GW_TILE)], buf.at[slot], sems.at[slot]).start()

        @pl.when(s_ == 0)
        def _():
            fetch(s_, 0)

        @pl.when(s_ + 1 < steps)
        def _():
            fetch(s_ + 1, (s_ + 1) % 2)

        slot = s_ % 2
        pltpu.make_async_copy(x_refs[0].at[:, pl.ds(0, GW_TILE)], buf.at[slot], sems.at[slot]).wait()
        xv = buf[slot]
        if square_x:
            xv = xv * xv
        o_ref[...] = _dot_tn(xv, dy_ref[...]).astype(BF16)

    return pl.pallas_call(
        body, name=name, out_shape=jax.ShapeDtypeStruct((steps * GW_TILE + gap_size, nb), BF16), grid=(steps,),
        in_specs=[pl.BlockSpec(memory_space=pl.ANY)] * len(xs) + [pl.BlockSpec((m, nb), lambda i: (0, 0))],
        out_specs=pl.BlockSpec((pl.Element(GW_TILE), pl.Element(nb)), out_row),
        scratch_shapes=[pltpu.VMEM((2, m, GW_TILE), BF16), pltpu.SemaphoreType.DMA((2,))],
        compiler_params=_params(("arbitrary",)))(*xs, dy)


def _grad_w(name, x, dy):
    na, nb = x.shape[1], dy.shape[1]
    ta, tb = min(na, 1024), min(nb, 512)
    return _mm_tn(name, x, lambda a: a, ta, dy, lambda b: b, tb, (na, nb),
                  pl.BlockSpec((ta, tb), lambda a, b: (a, b)), (na // ta, nb // tb))


def _row_spec(width):
    return pl.BlockSpec((ROW_TILE, width), lambda i: (i, 0))


def _vec_spec(width):
    return pl.BlockSpec((1, width), lambda i: (0, 0))


def _rms_fwd(name, h, g, res=None):
    n, d = h.shape
    tile = ROW_TILE if n % ROW_TILE == 0 else n
    row_spec = pl.BlockSpec((tile, d), lambda i: (i, 0))

    def body(*refs):
        if res is None:
            h_ref, g_ref, hn_ref = refs
            hv = h_ref[...]
        else:
            h_ref, r_ref, g_ref, hs_ref, hn_ref = refs
            hv = h_ref[...] + r_ref[...].astype(F32)
            hs_ref[...] = hv
        r = lax.rsqrt(jnp.mean(hv * hv, axis=-1, keepdims=True) + RMS_EPS)
        hn_ref[...] = (hv * r * g_ref[...]).astype(BF16)

    ins = [h, g] if res is None else [h, res, g]
    in_specs = [row_spec, _vec_spec(d)] if res is None else [row_spec, row_spec, _vec_spec(d)]
    hn_shape = jax.ShapeDtypeStruct((n, d), BF16)
    if res is None:
        out_shape, out_specs = hn_shape, row_spec
    else:
        out_shape, out_specs = [jax.ShapeDtypeStruct((n, d), F32), hn_shape], [row_spec, row_spec]
    return pl.pallas_call(body, name=name, out_shape=out_shape, grid=(n // tile,), in_specs=in_specs,
                          out_specs=out_specs, compiler_params=_params(("parallel",)))(*ins)


def _rms_bwd(name, dhn, h, g, dres, batch, dh_dtype=F32, with_meta=False):
    n, d = h.shape
    t = n // batch
    nt = t // ROW_TILE

    def body(dhn_ref, h_ref, g_ref, dres_ref, *outs):
        first = (pl.program_id(0) == 0) & (pl.program_id(1) == 0)
        hv = h_ref[...]
        r = lax.rsqrt(jnp.mean(hv * hv, axis=-1, keepdims=True) + RMS_EPS)
        nrm = hv * r
        dhn = dhn_ref[...].astype(F32)
        dn = dhn * g_ref[...]
        dh = dres_ref[...].astype(F32) + r * (dn - nrm * jnp.mean(dn * nrm, axis=-1, keepdims=True))
        outs[0][...] = dh.astype(dh_dtype)
        dg_ref = outs[1]

        @pl.when(first)
        def _():
            dg_ref[...] = jnp.zeros(dg_ref.shape, F32)

        dg_ref[...] += jnp.sum(dhn * nrm, axis=0, keepdims=True)
        if with_meta:
            meta_ref = outs[2]

            @pl.when(first)
            def _():
                meta_ref[...] = jnp.zeros(meta_ref.shape, F32)

            @pl.when(pl.program_id(1) == 0)
            def _():
                meta_ref[...] += dh[0:N_META, :]

    row = pl.BlockSpec((ROW_TILE, d), lambda b, j: (b * nt + j, 0))
    vec = pl.BlockSpec((1, d), lambda b, j: (0, 0))
    shapes = [jax.ShapeDtypeStruct((n, d), dh_dtype), jax.ShapeDtypeStruct((1, d), F32)]
    specs = [row, vec]
    if with_meta:
        shapes.append(jax.ShapeDtypeStruct((N_META, d), F32))
        specs.append(pl.BlockSpec((N_META, d), lambda b, j: (0, 0)))
    return pl.pallas_call(body, name=name, out_shape=shapes, grid=(batch, nt), in_specs=[row, row, vec, row],
                          out_specs=specs, compiler_params=_params(("arbitrary", "arbitrary")))(dhn, h, g, dres)


def _final(name, h1, dn, tgt, g, batch):
    n, d = h1.shape
    t = n // batch
    nt = t // ROW_TILE

    def body(h1_ref, dn_ref, tgt_ref, g_ref, dhb_ref, loss_ref, dg_ref):
        first = (pl.program_id(0) == 0) & (pl.program_id(1) == 0)
        hv = h1_ref[...] + dn_ref[...].astype(F32)
        r = lax.rsqrt(jnp.mean(hv * hv, axis=-1, keepdims=True) + RMS_EPS)
        nrm = hv * r
        gv = g_ref[...]
        pos = pl.program_id(1) * ROW_TILE + lax.broadcasted_iota(jnp.int32, (ROW_TILE, 1), 0)
        diff = jnp.where(pos >= N_META, nrm * gv - tgt_ref[...], 0.0)
        dy = diff * (1.0 / d)

        @pl.when(first)
        def _():
            loss_ref[...] = jnp.zeros(loss_ref.shape, F32)
            dg_ref[...] = jnp.zeros(dg_ref.shape, F32)

        loss_ref[...] += jnp.full(loss_ref.shape, 0.5 / d, F32) * jnp.sum(diff * diff)
        dg_ref[...] += jnp.sum(dy * nrm, axis=0, keepdims=True)
        dng = dy * gv
        dh = r * (dng - nrm * jnp.mean(dng * nrm, axis=-1, keepdims=True))
        dhb_ref[...] = dh.astype(BF16)

    row = pl.BlockSpec((ROW_TILE, d), lambda b, j: (b * nt + j, 0))
    vec = pl.BlockSpec((1, d), lambda b, j: (0, 0))
    return pl.pallas_call(
        body, name=name, grid=(batch, nt), in_specs=[row, row, row, vec],
        out_shape=[jax.ShapeDtypeStruct((n, d), BF16), jax.ShapeDtypeStruct((8, 128), F32),
                   jax.ShapeDtypeStruct((1, d), F32)],
        out_specs=[row, pl.BlockSpec((8, 128), lambda b, j: (0, 0)), vec],
        compiler_params=_params(("arbitrary", "arbitrary")))(h1, dn, tgt, g)


def _ln_silu_fwd(name, c1, g, b):
    n, d = c1.shape

    def body(c_ref, g_ref, b_ref, o_ref):
        xv = c_ref[...]
        xc = xv - jnp.mean(xv, axis=-1, keepdims=True)
        rstd = lax.rsqrt(jnp.mean(xc * xc, axis=-1, keepdims=True) + LN_EPS)
        c2 = xc * rstd * g_ref[...] + b_ref[...]
        o_ref[...] = (c2 * _sigmoid(c2)).astype(BF16)

    return pl.pallas_call(body, name=name, out_shape=jax.ShapeDtypeStruct((n, d), BF16), grid=(n // ROW_TILE,),
                          in_specs=[_row_spec(d), _vec_spec(d), _vec_spec(d)], out_specs=_row_spec(d),
                          compiler_params=_params(("parallel",)))(c1, g, b)


def _ln_silu_bwd(name, dc3, c1, g, b):
    n, d = c1.shape

    def body(d_ref, c_ref, g_ref, b_ref, dc1_ref, dg_ref, db_ref):
        xv = c_ref[...]
        xc = xv - jnp.mean(xv, axis=-1, keepdims=True)
        rstd = lax.rsqrt(jnp.mean(xc * xc, axis=-1, keepdims=True) + LN_EPS)
        xh = xc * rstd
        c2 = xh * g_ref[...] + b_ref[...]
        s = _sigmoid(c2)
        dc2 = d_ref[...].astype(F32) * (s * (1.0 + c2 * (1.0 - s)))

        @pl.when(pl.program_id(0) == 0)
        def _():
            dg_ref[...] = jnp.zeros(dg_ref.shape, F32)
            db_ref[...] = jnp.zeros(db_ref.shape, F32)

        dg_ref[...] += jnp.sum(dc2 * xh, axis=0, keepdims=True)
        db_ref[...] += jnp.sum(dc2, axis=0, keepdims=True)
        dxh = dc2 * g_ref[...]
        dc1_ref[...] = rstd * (dxh - jnp.mean(dxh, axis=-1, keepdims=True)
                               - xh * jnp.mean(dxh * xh, axis=-1, keepdims=True))

    return pl.pallas_call(
        body, name=name, grid=(n // ROW_TILE,),
        out_shape=[jax.ShapeDtypeStruct((n, d), F32), jax.ShapeDtypeStruct((1, d), F32),
                   jax.ShapeDtypeStruct((1, d), F32)],
        in_specs=[_row_spec(d), _row_spec(d), _vec_spec(d), _vec_spec(d)],
        out_specs=[_row_spec(d), _vec_spec(d), _vec_spec(d)],
        compiler_params=_params(("arbitrary",)))(dc3, c1, g, b)


MERGE_TC = 512


def _merge_fwd(name, gates, a, c, b_co):
    n, d = a.shape
    nc = d // MERGE_TC

    def body(ga_ref, gc_ref, a_ref, c_ref, b_ref, m_ref):
        f32 = lambda r_: r_[...].astype(F32)
        m = _sigmoid(f32(ga_ref)) * f32(a_ref) + _sigmoid(f32(gc_ref)) * (f32(c_ref) + b_ref[...])
        m_ref[...] = m.astype(BF16)

    blk = lambda off: pl.BlockSpec((ROW_TILE, MERGE_TC), lambda i, j: (i, off + j))
    return pl.pallas_call(
        body, name=name, out_shape=jax.ShapeDtypeStruct((n, d), BF16), grid=(n // ROW_TILE, nc),
        in_specs=[blk(0), blk(nc), blk(0), blk(0), pl.BlockSpec((1, MERGE_TC), lambda i, j: (0, j))],
        out_specs=blk(0), compiler_params=_params(("parallel", "parallel")))(gates, gates, a, c, b_co)


def _merge_bwd(name, dm, gates, a, c, b_co):
    n, d = a.shape
    nc = d // MERGE_TC

    def body(dm_ref, ga_ref, gc_ref, a_ref, c_ref, b_ref, da_ref, dc_ref, dga_ref, dgc_ref, dbco_ref):
        f32 = lambda r_: r_[...].astype(F32)
        dmv = f32(dm_ref)
        sa, sc = _sigmoid(f32(ga_ref)), _sigmoid(f32(gc_ref))
        dc = dmv * sc
        da_ref[...] = (dmv * sa).astype(BF16)
        dc_ref[...] = dc.astype(BF16)
        dga_ref[...] = (dmv * f32(a_ref) * sa * (1.0 - sa)).astype(BF16)
        dgc_ref[...] = (dmv * (f32(c_ref) + b_ref[...]) * sc * (1.0 - sc)).astype(BF16)

        @pl.when(pl.program_id(1) == 0)
        def _():
            dbco_ref[...] = jnp.zeros(dbco_ref.shape, F32)

        dbco_ref[...] += jnp.sum(dc, axis=0, keepdims=True)

    blk = lambda off: pl.BlockSpec((ROW_TILE, MERGE_TC), lambda j, i: (i, off + j))
    vec = pl.BlockSpec((1, MERGE_TC), lambda j, i: (0, j))
    act = jax.ShapeDtypeStruct((n, d), BF16)
    return pl.pallas_call(
        body, name=name, grid=(nc, n // ROW_TILE),
        out_shape=[act, act, act, act, jax.ShapeDtypeStruct((1, d), F32)],
        in_specs=[blk(0), blk(0), blk(nc), blk(0), blk(0), vec],
        out_specs=[blk(0), blk(0), blk(0), blk(0), vec],
        compiler_params=_params(("parallel", "arbitrary")))(dm, gates, gates, a, c, b_co)


CONV_TC = 128
CONV_HALO = 32


def _conv_chunk(t):
    return 48 if t % 48 == 0 else 32 if t % 32 == 0 else 16


def _fold8(x):
    out = x[0:8]
    for k in range(1, x.shape[0] // 8):
        out = out + x[8 * k:8 * k + 8]
    return out


def _glu_conv_fwd(name, glu, b_glu, w_dw, b_dw, batch):
    n, c2 = glu.shape
    c = c2 // 2
    t = n // batch
    nc = c // CONV_TC

    def body(a_ref, gt_ref, ba_ref, bg_ref, w_ref, bdw_ref, o_ref, pad_ref):
        u = (a_ref[...].astype(F32) + ba_ref[...]) * _sigmoid(gt_ref[...].astype(F32) + bg_ref[...])
        pad_ref[0:CONV_HALO, :] = jnp.zeros((CONV_HALO, CONV_TC), F32)
        pad_ref[CONV_HALO:CONV_HALO + t, :] = u
        ch = _conv_chunk(t)
        for r0 in range(0, t, ch):
            acc = jnp.zeros((ch, CONV_TC), F32) + bdw_ref[...]
            for j in range(CONV_W):
                off = r0 + CONV_HALO - (CONV_W - 1) + j
                acc = acc + w_ref[j:j + 1, :] * pad_ref[off:off + ch, :]
            o_ref[r0:r0 + ch, :] = acc

    seq = lambda off: pl.BlockSpec((t, CONV_TC), lambda b, j: (b, off + j))
    vec = lambda off: pl.BlockSpec((1, CONV_TC), lambda b, j: (0, off + j))
    return pl.pallas_call(
        body, name=name, out_shape=jax.ShapeDtypeStruct((n, c), F32), grid=(batch, nc),
        in_specs=[seq(0), seq(nc), vec(0), vec(nc), pl.BlockSpec((CONV_W, CONV_TC), lambda b, j: (0, j)), vec(0)],
        out_specs=seq(0), scratch_shapes=[pltpu.VMEM((t + CONV_HALO, CONV_TC), F32)],
        compiler_params=_params(("parallel", "parallel")))(glu, glu, b_glu, b_glu, w_dw, b_dw)


def _glu_conv_bwd(name, dc1, glu, b_glu, w_dw, batch):
    n, c2 = glu.shape
    c = c2 // 2
    t = n // batch
    nc = c // CONV_TC

    def body(d_ref, a_ref, gt_ref, ba_ref, bg_ref, w_ref, dga_ref, dgg_ref, dw_ref, dbdw_ref, dba_ref, dbg_ref,
             padu_ref, padd_ref):
        av = a_ref[...].astype(F32) + ba_ref[...]
        sg = _sigmoid(gt_ref[...].astype(F32) + bg_ref[...])
        dc = d_ref[...]
        padu_ref[0:CONV_HALO, :] = jnp.zeros((CONV_HALO, CONV_TC), F32)
        padu_ref[CONV_HALO:CONV_HALO + t, :] = av * sg
        padd_ref[0:t, :] = dc
        padd_ref[t:t + CONV_HALO, :] = jnp.zeros((CONV_HALO, CONV_TC), F32)

        @pl.when(pl.program_id(1) == 0)
        def _():
            dw_ref[...] = jnp.zeros(dw_ref.shape, F32)
            dbdw_ref[...] = jnp.zeros(dbdw_ref.shape, F32)
            dba_ref[...] = jnp.zeros(dba_ref.shape, F32)
            dbg_ref[...] = jnp.zeros(dbg_ref.shape, F32)

        ch = _conv_chunk(t)
        zero8 = jnp.zeros((8, CONV_TC), F32)
        dw_acc = [zero8] * CONV_W
        sum_dc, sum_a, sum_g = zero8, zero8, zero8
        for r0 in range(0, t, ch):
            dcc = d_ref[r0:r0 + ch, :]
            du = jnp.zeros((ch, CONV_TC), F32)
            for j in range(CONV_W):
                back = r0 + CONV_W - 1 - j
                du = du + w_ref[j:j + 1, :] * padd_ref[back:back + ch, :]
                off = r0 + CONV_HALO - (CONV_W - 1) + j
                dw_acc[j] = dw_acc[j] + _fold8(dcc * padu_ref[off:off + ch, :])
            sgc = _sigmoid(gt_ref[r0:r0 + ch, :].astype(F32) + bg_ref[...])
            dga = du * sgc
            dgg = du * padu_ref[CONV_HALO + r0:CONV_HALO + r0 + ch, :] * (1.0 - sgc)
            dga_ref[r0:r0 + ch, :] = dga.astype(BF16)
            dgg_ref[r0:r0 + ch, :] = dgg.astype(BF16)
            sum_dc, sum_a, sum_g = sum_dc + _fold8(dcc), sum_a + _fold8(dga), sum_g + _fold8(dgg)
        for j in range(CONV_W):
            dw_ref[j:j + 1, :] += jnp.sum(dw_acc[j], axis=0, keepdims=True)
        dbdw_ref[...] += jnp.sum(sum_dc, axis=0, keepdims=True)
        dba_ref[...] += jnp.sum(sum_a, axis=0, keepdims=True)
        dbg_ref[...] += jnp.sum(sum_g, axis=0, keepdims=True)

    seq = lambda off: pl.BlockSpec((t, CONV_TC), lambda j, b: (b, off + j))
    vec = lambda off: pl.BlockSpec((1, CONV_TC), lambda j, b: (0, off + j))
    wsp = pl.BlockSpec((CONV_W, CONV_TC), lambda j, b: (0, j))
    act = jax.ShapeDtypeStruct((n, c), BF16)
    v = jax.ShapeDtypeStruct((1, c), F32)
    return pl.pallas_call(
        body, name=name, grid=(nc, batch),
        out_shape=[act, act, jax.ShapeDtypeStruct((CONV_W, c), F32), v, v, v],
        in_specs=[seq(0), seq(0), seq(nc), vec(0), vec(nc), wsp],
        out_specs=[seq(0), seq(0), wsp, vec(0), vec(0), vec(0)],
        scratch_shapes=[pltpu.VMEM((t + CONV_HALO, CONV_TC), F32), pltpu.VMEM((t + CONV_HALO, CONV_TC), F32)],
        compiler_params=_params(("parallel", "arbitrary")))(dc1, glu, glu, b_glu, b_glu, w_dw)


def _split3(x):
    hi = x.astype(BF16)
    r = x - hi.astype(F32)
    mid = r.astype(BF16)
    lo = (r - mid.astype(F32)).astype(BF16)
    return hi, mid, lo


def _tri_matmul(tri, x):
    hi, mid, lo = _split3(x)
    dot = lambda v: jnp.dot(tri, v, preferred_element_type=F32)
    return dot(hi) + dot(mid) + dot(lo)


def _fox_prep_fwd(name, fg, b_fg, batch):
    n, w = fg.shape
    t = n // batch
    nq = t // ROW_TILE

    def body(fg_ref, b_ref, cum_ref):
        row = lax.broadcasted_iota(jnp.int32, (ROW_TILE, ROW_TILE), 0)
        col = lax.broadcasted_iota(jnp.int32, (ROW_TILE, ROW_TILE), 1)
        tri = (row >= col).astype(BF16)
        for k in range(nq):
            rows = slice(k * ROW_TILE, (k + 1) * ROW_TILE)
            z = fg_ref[rows, :] + b_ref[...]
            logf = jnp.minimum(z, 0.0) - jnp.log(1.0 + jnp.exp(-jnp.abs(z)))
            cum = _tri_matmul(tri, logf)
            if k > 0:
                cum = cum + cum_ref[k * ROW_TILE - 1:k * ROW_TILE, :]
            cum_ref[rows, :] = cum

    seq = pl.BlockSpec((t, w), lambda b: (b, 0))
    return pl.pallas_call(body, name=name, out_shape=jax.ShapeDtypeStruct((n, w), F32), grid=(batch,),
                          in_specs=[seq, pl.BlockSpec((1, w), lambda b: (0, 0))], out_specs=seq,
                          compiler_params=_params(("parallel",)))(fg, b_fg)


def _fox_prep_bwd(name, dcum_k, dcum_q, fg, b_fg, batch):
    n, w = fg.shape
    t = n // batch
    nq = t // ROW_TILE

    def body(dk_ref, dq_ref, fg_ref, b_ref, dfg_ref, db_ref, rev_ref):
        row = lax.broadcasted_iota(jnp.int32, (ROW_TILE, ROW_TILE), 0)
        col = lax.broadcasted_iota(jnp.int32, (ROW_TILE, ROW_TILE), 1)
        tri = (col >= row).astype(BF16)

        @pl.when(pl.program_id(0) == 0)
        def _():
            db_ref[...] = jnp.zeros(db_ref.shape, F32)

        for k in reversed(range(nq)):
            rows = slice(k * ROW_TILE, (k + 1) * ROW_TILE)
            dlog = _tri_matmul(tri, dk_ref[rows, :] + dq_ref[rows, :])
            if k < nq - 1:
                dlog = dlog + rev_ref[(k + 1) * ROW_TILE:(k + 1) * ROW_TILE + 1, :]
            rev_ref[rows, :] = dlog
            dfg = dlog * _sigmoid(-(fg_ref[rows, :] + b_ref[...]))
            dfg_ref[rows, :] = dfg.astype(BF16)
            db_ref[...] += jnp.sum(dfg, axis=0, keepdims=True)

    seq = pl.BlockSpec((t, w), lambda b: (b, 0))
    vec = pl.BlockSpec((1, w), lambda b: (0, 0))
    return pl.pallas_call(
        body, name=name, grid=(batch,),
        out_shape=[jax.ShapeDtypeStruct((n, w), BF16), jax.ShapeDtypeStruct((1, w), F32)],
        in_specs=[seq, seq, seq, vec], out_specs=[seq, vec], scratch_shapes=[pltpu.VMEM((t, w), F32)],
        compiler_params=_params(("arbitrary",)))(dcum_k, dcum_q, fg, b_fg)


def _head_masks(x):
    lane = lax.broadcasted_iota(jnp.int32, x.shape, 1)
    zero = jnp.zeros(x.shape, x.dtype)
    return jnp.where(lane < HEAD_DIM, x, zero), jnp.where(lane >= HEAD_DIM, x, zero)


ATTN_BLOCK = 512


def _attn_blocks(t):
    nb = max(t // ATTN_BLOCK, 1)
    blocks = [(i * ATTN_BLOCK, ATTN_BLOCK) for i in range(nb - 1)]
    return blocks + [((nb - 1) * ATTN_BLOCK, t - (nb - 1) * ATTN_BLOCK)]


def _attn_specs(t):
    blocks = _attn_blocks(t)
    width = max(sz for _, sz in blocks)
    qkv = lambda off: pl.BlockSpec((t, LANES), lambda b, h: (b, off + h))
    cumr = pl.BlockSpec((None, None, len(blocks), 8, width), lambda b, h: (b, h, 0, 0, 0))
    return qkv, cumr


def _key_sums_to_blocks(cum, batch, t, n_pairs):
    blocks = _attn_blocks(t)
    width = max(sz for _, sz in blocks)
    cum_h = cum.reshape(batch, t, -1)[:, :, :2 * n_pairs].reshape(batch, t, n_pairs, 2)
    rows = [jnp.pad(jnp.transpose(cum_h[:, s0:s0 + sz], (0, 2, 3, 1)), ((0, 0), (0, 0), (0, 6), (0, width - sz)))
            for s0, sz in blocks]
    return jnp.stack(rows, axis=2)


def _key_sums_from_blocks(dcumr, batch, t, n_pairs):
    cols = [jnp.transpose(dcumr[:, :, j, :2, :sz], (0, 3, 1, 2)) for j, (_, sz) in enumerate(_attn_blocks(t))]
    return jnp.concatenate(cols, axis=1).reshape(batch * t, 2 * n_pairs)


def _causal(size):
    row = lax.broadcasted_iota(jnp.int32, (size, size), 0)
    col = lax.broadcasted_iota(jnp.int32, (size, size), 1)
    return row >= col


def _attn_fwd(name, qkv, cumr, batch):
    n, w3 = qkv.shape
    w = w3 // 3
    t = n // batch
    n_pairs = w // LANES
    blocks = _attn_blocks(t)

    def body(q_ref, k_ref, v_ref, cr_ref, o_ref, lse_ref):
        pair = pl.program_id(1)

        @pl.when(pair == 0)
        def _():
            lse_ref[...] = jnp.zeros(lse_ref.shape, F32)

        for i, (q0, qn) in enumerate(blocks):
            rows = slice(q0, q0 + qn)
            qs = _head_masks(q_ref[rows, :] * (0.125 * LOG2E))
            outs, lses = [], []
            for hh in range(2):
                m = jnp.full((qn, 1), NEG, F32)
                l = jnp.zeros((qn, 1), F32)
                acc = jnp.zeros((qn, LANES), F32)
                for j in range(i + 1):
                    k0, kn = blocks[j]
                    cols = slice(k0, k0 + kn)
                    s = _dot_nt(qs[hh], k_ref[cols, :]) - cr_ref[j, hh:hh + 1, 0:kn] * LOG2E
                    if j == i:
                        s = jnp.where(_causal(qn), s, NEG)
                    m_new = jnp.maximum(m, jnp.max(s, axis=1, keepdims=True))
                    alpha = jnp.exp2(m - m_new)
                    p = jnp.exp2(s - m_new)
                    l = alpha * l + jnp.sum(p, axis=1, keepdims=True)
                    acc = alpha * acc + jnp.dot(p.astype(BF16), v_ref[cols, :], preferred_element_type=F32)
                    m = m_new
                outs.append(acc / l)
                lses.append(m + jnp.log2(l))
            lane = lax.broadcasted_iota(jnp.int32, (qn, LANES), 1)
            o_ref[rows, :] = jnp.where(lane < HEAD_DIM, outs[0], outs[1]).astype(BF16)
            lse_ref[rows, :] = jnp.where(lane == 2 * pair, lses[0],
                                         jnp.where(lane == 2 * pair + 1, lses[1], lse_ref[rows, :]))

    qkv_spec, cumr_spec = _attn_specs(t)
    return pl.pallas_call(
        body, name=name, grid=(batch, n_pairs),
        out_shape=[jax.ShapeDtypeStruct((n, w), BF16), jax.ShapeDtypeStruct((n, LANES), F32)],
        in_specs=[qkv_spec(0), qkv_spec(n_pairs), qkv_spec(2 * n_pairs), cumr_spec],
        out_specs=[qkv_spec(0), pl.BlockSpec((t, LANES), lambda b, h: (b, 0))],
        compiler_params=_params(("parallel", "arbitrary")))(qkv, qkv, qkv, cumr)


def _attn_bwd(name, qkv, o, do, lse, cumr, batch):
    n, w3 = qkv.shape
    w = w3 // 3
    t = n // batch
    n_pairs = w // LANES
    blocks = _attn_blocks(t)

    def body(q_ref, k_ref, v_ref, o_ref, do_ref, lse_ref, cr_ref, dq_ref, dk_ref, dv_ref, dcr_ref, dcq_ref,
             dk_acc, dv_acc):
        pair = pl.program_id(1)
        dk_acc[...] = jnp.zeros(dk_acc.shape, F32)
        dv_acc[...] = jnp.zeros(dv_acc.shape, F32)
        dcr_ref[...] = jnp.zeros(dcr_ref.shape, F32)

        @pl.when(pair == 0)
        def _():
            dcq_ref[...] = jnp.zeros(dcq_ref.shape, F32)

        for i, (q0, qn) in enumerate(blocks):
            rows = slice(q0, q0 + qn)
            qs = _head_masks(q_ref[rows, :] * 0.125)
            q2 = _head_masks(q_ref[rows, :] * (0.125 * LOG2E))
            dos = _head_masks(do_ref[rows, :])
            dq = jnp.zeros((qn, LANES), F32)
            lane = lax.broadcasted_iota(jnp.int32, (qn, LANES), 1)
            dcq = []
            for hh in range(2):
                row_sum = jnp.zeros((qn, 1), F32)
                lse = jnp.sum(jnp.where(lane == 2 * pair + hh, lse_ref[rows, :], 0.0), axis=1, keepdims=True)
                delta = jnp.sum(dos[hh].astype(F32) * o_ref[rows, :].astype(F32), axis=1, keepdims=True)
                for j in range(i + 1):
                    k0, kn = blocks[j]
                    cols = slice(k0, k0 + kn)
                    s = _dot_nt(q2[hh], k_ref[cols, :]) - cr_ref[j, hh:hh + 1, 0:kn] * LOG2E
                    p = jnp.exp2(s - lse)
                    if j == i:
                        p = jnp.where(_causal(qn), p, 0.0)
                    dp = _dot_nt(dos[hh], v_ref[cols, :])
                    ds = p * (dp - delta)
                    pb, dsb = p.astype(BF16), ds.astype(BF16)
                    km = _head_masks(k_ref[cols, :])[hh]
                    dv_acc[j, :, 0:kn] += _dot_tn(dos[hh], pb)
                    dk_acc[j, :, 0:kn] += _dot_tn(qs[hh], dsb)
                    dq = dq + jnp.dot(dsb, km, preferred_element_type=F32)
                    dcr_ref[j, hh:hh + 1, 0:kn] -= jnp.sum(ds, axis=0, keepdims=True)
                    row_sum = row_sum + jnp.sum(ds, axis=1, keepdims=True)
                dcq.append(row_sum)
            dq_ref[rows, :] = (dq * 0.125).astype(BF16)
            dcq_ref[rows, :] = jnp.where(lane == 2 * pair, dcq[0],
                                         jnp.where(lane == 2 * pair + 1, dcq[1], dcq_ref[rows, :]))
        for j, (k0, kn) in enumerate(blocks):
            dk_ref[k0:k0 + kn, :] = dk_acc[j].T[0:kn, :].astype(BF16)
            dv_ref[k0:k0 + kn, :] = dv_acc[j].T[0:kn, :].astype(BF16)

    qkv_spec, cumr_spec = _attn_specs(t)
    wide = -(-max(sz for _, sz in blocks) // LANES) * LANES
    act = jax.ShapeDtypeStruct((n, w), BF16)
    return pl.pallas_call(
        body, name=name, grid=(batch, n_pairs),
        out_shape=[act, act, act, jax.ShapeDtypeStruct(cumr.shape, F32), jax.ShapeDtypeStruct((n, LANES), F32)],
        in_specs=[qkv_spec(0), qkv_spec(n_pairs), qkv_spec(2 * n_pairs), qkv_spec(0), qkv_spec(0),
                  pl.BlockSpec((t, LANES), lambda b, h: (b, 0)), cumr_spec],
        out_specs=[qkv_spec(0), qkv_spec(0), qkv_spec(0), cumr_spec, pl.BlockSpec((t, LANES), lambda b, h: (b, 0))],
        scratch_shapes=[pltpu.VMEM((len(blocks), LANES, wide), F32), pltpu.VMEM((len(blocks), LANES, wide), F32)],
        compiler_params=_params(("parallel", "arbitrary")))(qkv, qkv, qkv, o, do, lse, cumr)


def _adamw(name, parts, w, m, v):
    r, c = w.shape
    tr = 128 if r % 128 == 0 else r
    tc = 256 if tr > 128 and c % 256 == 0 else c
    c1 = 1.0 - ADAM_B1 ** ADAM_STEP
    c2 = 1.0 - ADAM_B2 ** ADAM_STEP

    def body(p_ref, w_ref, m_ref, v_ref, g_ref, d_ref, m2_ref, v2_ref):
        g = p_ref[0].astype(F32)
        for s in range(1, N_DEV):
            g = g + p_ref[s].astype(F32)
        m2 = ADAM_B1 * m_ref[...] + (1.0 - ADAM_B1) * g
        v2 = ADAM_B2 * v_ref[...] + (1.0 - ADAM_B2) * (g * g)
        g_ref[...] = g
        m2_ref[...] = m2
        v2_ref[...] = v2
        d_ref[...] = -ADAM_LR * ((m2 / c1) / (jnp.sqrt(v2 / c2) + ADAM_EPS) + ADAM_WD * w_ref[...])

    blk = pl.BlockSpec((tr, tc), lambda i, j: (i, j))
    shp = jax.ShapeDtypeStruct((r, c), F32)
    return pl.pallas_call(
        body, name=name, out_shape=[shp] * 4, grid=(r // tr, c // tc),
        in_specs=[pl.BlockSpec((N_DEV, tr, tc), lambda i, j: (0, i, j)), blk, blk, blk], out_specs=[blk] * 4,
        compiler_params=_params(("parallel", "parallel")))(parts, w, m, v)


def _cat_small(vals):
    parts = []
    for name in SMALL:
        v = vals[name].reshape(1, -1).astype(F32)
        parts.append(jnp.pad(v, ((0, 0), (0, SMALL_W[name] - v.shape[1]))))
    return jnp.concatenate(parts, axis=1)


def _split_small(row, shapes):
    out, off = {}, 0
    for name in SMALL:
        out[name] = row[0, off:off + SMALL_N[name]].reshape(shapes[name])
        off += SMALL_W[name]
    return out


def _cols_from_shards(g):
    return jnp.transpose(g, (1, 0, 2)).reshape(g.shape[1], N_DEV * g.shape[2])


def _shards_from_cols(a):
    r, c = a.shape
    return jnp.transpose(a.reshape(r, N_DEV, c // N_DEV), (1, 0, 2))


def kernel(x, meta_tokens, norm_mix_gain, w_in, b_forget, w_attn_out, b_glu, conv_dw_w, conv_dw_b, conv_ln_gain, conv_ln_bias, w_conv_out, b_conv_out, w_out, norm_mlp_gain, w_mlp_up, w_mlp_down, final_norm_gain, loss_target, m_meta_tokens, m_norm_mix_gain, m_w_in, m_b_forget, m_w_attn_out, m_b_glu, m_conv_dw_w, m_conv_dw_b, m_conv_ln_gain, m_conv_ln_bias, m_w_conv_out, m_b_conv_out, m_w_out, m_norm_mlp_gain, m_w_mlp_up, m_w_mlp_down, m_final_norm_gain, v_meta_tokens, v_norm_mix_gain, v_w_in, v_b_forget, v_w_attn_out, v_b_glu, v_conv_dw_w, v_conv_dw_b, v_conv_ln_gain, v_conv_ln_bias, v_w_conv_out, v_b_conv_out, v_w_out, v_norm_mlp_gain, v_w_mlp_up, v_w_mlp_down, v_final_norm_gain):
    weights = dict(meta_tokens=meta_tokens, norm_mix_gain=norm_mix_gain, w_in=w_in, b_forget=b_forget, w_attn_out=w_attn_out, b_glu=b_glu, conv_dw_w=conv_dw_w, conv_dw_b=conv_dw_b, conv_ln_gain=conv_ln_gain, conv_ln_bias=conv_ln_bias, w_conv_out=w_conv_out, b_conv_out=b_conv_out, w_out=w_out, norm_mlp_gain=norm_mlp_gain, w_mlp_up=w_mlp_up, w_mlp_down=w_mlp_down, final_norm_gain=final_norm_gain)
    mom_m = dict(meta_tokens=m_meta_tokens, norm_mix_gain=m_norm_mix_gain, w_in=m_w_in, b_forget=m_b_forget, w_attn_out=m_w_attn_out, b_glu=m_b_glu, conv_dw_w=m_conv_dw_w, conv_dw_b=m_conv_dw_b, conv_ln_gain=m_conv_ln_gain, conv_ln_bias=m_conv_ln_bias, w_conv_out=m_w_conv_out, b_conv_out=m_b_conv_out, w_out=m_w_out, norm_mlp_gain=m_norm_mlp_gain, w_mlp_up=m_w_mlp_up, w_mlp_down=m_w_mlp_down, final_norm_gain=m_final_norm_gain)
    mom_v = dict(meta_tokens=v_meta_tokens, norm_mix_gain=v_norm_mix_gain, w_in=v_w_in, b_forget=v_b_forget, w_attn_out=v_w_attn_out, b_glu=v_b_glu, conv_dw_w=v_conv_dw_w, conv_dw_b=v_conv_dw_b, conv_ln_gain=v_conv_ln_gain, conv_ln_bias=v_conv_ln_bias, w_conv_out=v_w_conv_out, b_conv_out=v_b_conv_out, w_out=v_w_out, norm_mlp_gain=v_norm_mlp_gain, w_mlp_up=v_w_mlp_up, w_mlp_down=v_w_mlp_down, final_norm_gain=v_final_norm_gain)
    names = list(weights)
    batch, seq, d = x.shape
    t = seq + N_META
    n = batch * t
    n_pairs = d // LANES
    assert t % ROW_TILE == 0 and d == SEG

    to_rows = lambda w3: jnp.transpose(w3[0])
    w_in_t, m_in_t, v_in_t = to_rows(w_in), to_rows(m_w_in), to_rows(v_w_in)
    first = [w_in_t.astype(BF16), meta_tokens, conv_dw_w[0]]
    rest = [w_[0].astype(BF16) for w_ in (w_attn_out, w_conv_out, w_out, w_mlp_up, w_mlp_down)]
    tgt = jnp.concatenate([jnp.zeros((batch, N_META, d), F32), loss_target], axis=1).reshape(n, d)
    h0_rows = jnp.pad(x, ((0, 0), (N_META, 0), (0, 0)))
    g1 = norm_mix_gain.reshape(1, -1)
    gather_a = _exchange_start("gather_in_start", [(f_, False) for f_ in first], ks=CHIP_PEERS)
    level_1 = _exchange_wait("gather_in_wait", gather_a, [gather_a["token"], tgt, h0_rows, w_in_t, m_in_t, v_in_t] + rest)
    passed = _pass_on_start("gather_in_pass_start", level_1)
    w_in_g, meta_g, w_dw_g = _pass_on_wait("gather_in_pass_wait", passed, passed["token"])
    gather_b = _exchange_start("gather_rest_start", [(r_, False) for r_ in rest])
    n_fg = b_forget.shape[1]
    shard_w = w_in.shape[2]
    wt = w_in_g.reshape(N_DEV * shard_w, d)
    o_fg = 3 * SEG
    seg_rows = [0, SEG, 2 * SEG] + [o_fg + n_fg + i * SEG for i in range(4)]
    d_ff = w_mlp_down.shape[1] * N_DEV
    ff_blk = d_ff // N_DEV
    meta_f = _cols_from_shards(meta_g)
    w_dw = _cols_from_shards(w_dw_g)

    row2 = lambda v: v.reshape(1, -1)
    g2, g3 = row2(norm_mlp_gain), row2(final_norm_gain)
    b_fg = jnp.pad(b_forget, ((0, 0), (0, FG_PAD - n_fg)))
    h0 = lax.dynamic_update_slice(h0_rows, jnp.broadcast_to(meta_f[None], (batch, N_META, d)), (0, 0, 0)).reshape(n, d)

    hn1 = _rms_fwd("rms1", h0, g1)
    proj = lambda name, off, width, tn, dt: _mm_nt(name, [(hn1, _a_rows(d), wt, _wt_rows(tn, off))], n, width, tn, dt,
                                                   after=gather_b["token"])
    qkv = proj("proj_qkv", 0, 3 * SEG, SEG, BF16)
    glu = proj("proj_glu", seg_rows[3], 2 * SEG, SEG, BF16)
    gates = proj("proj_gates", seg_rows[5], 2 * SEG, SEG, BF16)
    fg = proj("proj_fg", o_fg, FG_PAD, FG_PAD, F32)

    cum = _fox_prep_fwd("fox_cumsum", fg, b_fg, batch)
    cumr = _key_sums_to_blocks(cum, batch, t, n_pairs)
    o, lse = _attn_fwd("attn_fwd", qkv, cumr, batch)
    rest = _exchange_wait("gather_rest_wait", gather_b, o)
    w_ao, w_co, w_o = [r_.reshape(d, d) for r_ in rest[:3]]
    w_up = rest[3]
    w_dn = rest[4].reshape(d_ff, d)
    a = _mm_nn("attn_out", o, w_ao, _w_cols(d, d, 0), d, d, BF16)

    c1 = _glu_conv_fwd("glu_conv", glu, b_glu, w_dw, conv_dw_b, batch)
    c3 = _ln_silu_fwd("ln_silu", c1, conv_ln_gain, conv_ln_bias)
    c = _mm_nn("conv_out", c3, w_co, _w_cols(d, d, 0), d, d, BF16)

    mrg = _merge_fwd("merge", gates, a, c, b_conv_out)
    mo = _mm_nn("mix_out", mrg, w_o, _w_cols(d, d, 0), d, d, BF16)
    h1, hn2 = _rms_fwd("resid_rms2", h0, g2, res=mo)
    per = ff_blk // 512
    r_up = _mm_nn("mlp_up", hn2, w_up, pl.BlockSpec((None, d, 512), lambda i, j: (j // per, 0, j % per)),
                  d_ff, 512, BF16, relu=True)
    dn = _mm_nn("mlp_down", r_up, w_dn, _w_cols(d_ff, d, 0), d, d, BF16, square_x=True, tm=ROW_TILE)
    dh2b, loss_blk, dg3 = _final("final_loss", h1, dn, tgt, g3, batch)

    dup = _mm_nt("d_mlp_down", [(dh2b, _a_rows(d), w_dn, _w_rows(d, d))], n, d_ff, d, BF16, relu_bwd_of=r_up)
    dw_dn = _grad_w_rows("gw_mlp_down", [r_up], dh2b, square_x=True)
    dhn2 = _mm_nt("d_mlp_up", [(dup, _a_rows(ff_blk, g), w_up, pl.BlockSpec((None, 512, ff_blk), lambda i, j, g=g: (g, j, 0)))
                               for g in range(N_DEV)], n, d, 512, BF16)
    dw_up = _mm_tn("gw_mlp_up", hn2, lambda a_: 0, d, dup, lambda b_: b_, ff_blk, (N_DEV, d, ff_blk),
                   pl.BlockSpec((None, d, ff_blk), lambda a_, b_: (b_, 0, 0)), (1, N_DEV))
    scatter_1 = _exchange_start("scatter_mlp_start", [(dw_dn.reshape(N_DEV, ff_blk, d), True), (dw_up, True)])
    dh1b, dg2 = _rms_bwd("rms2_bwd", dhn2, h1, g2 + scatter_1["token"][0:1, 0:1], dh2b, batch, dh_dtype=BF16)

    dm = _mm_nt("d_mix_out", [(dh1b, _a_rows(d), w_o, _w_rows(d, d))], n, d, d, BF16)
    dw_o = _grad_w("gw_mix_out", mrg, dh1b)
    da, dc, dga, dgc, dbco = _merge_bwd("merge_bwd", dm, gates, a, c, b_conv_out)

    do = _mm_nt("d_attn_out", [(da, _a_rows(d), w_ao, _w_rows(d, d))], n, d, d, BF16)
    dw_ao = _grad_w("gw_attn_out", o, da)
    dc3 = _mm_nt("d_conv_out", [(dc, _a_rows(d), w_co, _w_rows(d, d))], n, d, d, BF16)
    dw_co = _grad_w("gw_conv_out", c3, dc)

    scatter_2 = _exchange_start("scatter_mix_start", [(dw_.reshape(N_DEV, d // N_DEV, d), True)
                                                      for dw_ in (dw_o, dw_ao, dw_co)])
    dc1, dg_ln, db_ln = _ln_silu_bwd("ln_silu_bwd", dc3, c1, conv_ln_gain + scatter_2["token"][0:1, 0:1],
                                     conv_ln_bias)
    dglu_a, dglu_g, dw_dw, db_dw, dbg_a, dbg_g = _glu_conv_bwd("glu_conv_bwd", dc1, glu, b_glu, w_dw, batch)

    dq, dk, dv, dcumr, dcum_q = _attn_bwd("attn_bwd", qkv, o, do, lse, cumr, batch)
    dcum_k = jnp.pad(_key_sums_from_blocks(dcumr, batch, t, n_pairs), ((0, 0), (0, FG_PAD - 2 * n_pairs)))
    dfg, db_fg = _fox_prep_bwd("fox_cumsum_bwd", dcum_k, dcum_q, fg, b_fg, batch)

    segs = [dq, dk, dv, dglu_a, dglu_g, dga, dgc]
    gw_t = _grad_w_rows("gw_in", segs, hn1, gap=(o_fg, n_fg))
    gw_fg = _grad_w("gw_in_fg", dfg, hn1)[:n_fg]
    dw_in_t = lax.dynamic_update_slice(gw_t, gw_fg, (o_fg, 0)).reshape(N_DEV, shard_w, d)
    scatter_3 = _exchange_start("scatter_in_start", [(dw_in_t, True)])
    pairs = [(s_, _a_rows(SEG, 0, ROW_TILE), wt, _wt_block(SEG, seg_rows[i], 512), "nn") for i, s_ in enumerate(segs)]
    pairs.append((dfg, _a_rows(FG_PAD, 0, ROW_TILE), wt, _wt_block(FG_PAD, o_fg, 512), "nn"))
    dhn1 = _mm_nt("d_proj_in", pairs, n, d, 512, BF16, tm=ROW_TILE, after=scatter_3["token"])
    dh0, dg1, dmeta = _rms_bwd("rms1_bwd", dhn1, h0, g1, dh1b, batch, with_meta=True)
    grad_x = dh0.reshape(batch, t, d)[:, N_META:, :]

    small_g = dict(norm_mix_gain=dg1, b_forget=db_fg[:, :n_fg], b_glu=jnp.concatenate([dbg_a, dbg_g], axis=1),
                   conv_dw_b=db_dw, conv_ln_gain=dg_ln, conv_ln_bias=db_ln, b_conv_out=dbco, norm_mlp_gain=dg2,
                   final_norm_gain=dg3)
    scatter_4 = _exchange_start("scatter_small_start", [
        (_shards_from_cols(dmeta), True), (_shards_from_cols(dw_dw), True), (_cat_small(small_g), False),
        (loss_blk[0:1, :], False)])

    grads, deltas, new_m, new_v = {}, {}, {}, {}

    def update(k, parts):
        shp = weights[k].shape
        if k == "w_in":
            res_ = _adamw("adamw_" + k, parts, w_in_t, m_in_t, v_in_t)
            res_ = [jnp.transpose(r) for r in res_]
        else:
            w2 = lambda arr: arr.reshape(parts.shape[1:])
            res_ = _adamw("adamw_" + k, parts, w2(weights[k]), w2(mom_m[k]), w2(mom_v[k]))
        grads[k], deltas[k], new_m[k], new_v[k] = [r.reshape(shp) for r in res_]

    for k, parts in zip(("w_mlp_down", "w_mlp_up"), _exchange_wait("scatter_mlp_wait", scatter_1, scatter_4["token"])):
        update(k, parts)
    for k, parts in zip(("w_out", "w_attn_out", "w_conv_out"),
                        _exchange_wait("scatter_mix_wait", scatter_2, deltas["w_mlp_up"])):
        update(k, parts)
    update("w_in", _exchange_wait("scatter_in_wait", scatter_3, deltas["w_conv_out"])[0])
    reduced = _exchange_wait("scatter_small_wait", scatter_4, deltas["w_in"])
    loss = jnp.sum(reduced.pop()[:, 0, 0])
    for k, parts in zip(("meta_tokens", "conv_dw_w"), reduced[:-1]):
        update(k, parts)
    res = _adamw("adamw_small", reduced[-1], _cat_small(weights), _cat_small(mom_m), _cat_small(mom_v))
    shapes = {k: weights[k].shape for k in SMALL}
    for dst, r in zip((grads, deltas, new_m, new_v), res):
        dst.update(_split_small(r, shapes))

    return (loss, grad_x, *[grads[k] for k in names], *[deltas[k] for k in names],
            *[new_m[k] for k in names], *[new_v[k] for k in names])
```

```python
import functools

import jax
import jax.numpy as jnp
from jax import lax
from jax.experimental import pallas as pl
from jax.experimental.pallas import tpu as pltpu

F32, BF16 = jnp.float32, jnp.bfloat16
N_DEV = 8
N_META = 16
HEAD_DIM = 64
LANES = 128
CONV_W = 31
RMS_EPS = 1e-6
LN_EPS = 1e-5
ROW_TILE = 688
MM_TM = 2 * ROW_TILE
SEG = 1024
FG_PAD = 128
VMEM_LIMIT = 56 * 1024 * 1024
ADAM_LR, ADAM_B1, ADAM_B2, ADAM_EPS, ADAM_WD, ADAM_STEP = 0.001, 0.9, 0.999, 1e-08, 0.01, 10
NEG = -1e30
LOG2E = 1.4426950408889634

SMALL = ("norm_mix_gain", "b_forget", "b_glu", "conv_dw_b", "conv_ln_gain", "conv_ln_bias", "b_conv_out",
         "norm_mlp_gain", "final_norm_gain")
SMALL_W = {"norm_mix_gain": 1024, "b_forget": 128, "b_glu": 2048, "conv_dw_b": 1024, "conv_ln_gain": 1024,
           "conv_ln_bias": 1024, "b_conv_out": 1024, "norm_mlp_gain": 1024, "final_norm_gain": 1024}
SMALL_N = {"norm_mix_gain": 1024, "b_forget": 16, "b_glu": 2048, "conv_dw_b": 1024, "conv_ln_gain": 1024,
           "conv_ln_bias": 1024, "b_conv_out": 1024, "norm_mlp_gain": 1024, "final_norm_gain": 1024}


def _params(sem=None):
    return pltpu.CompilerParams(dimension_semantics=sem, vmem_limit_bytes=VMEM_LIMIT)


def _sigmoid(x):
    return 1.0 / (1.0 + jnp.exp(-x))


def _dot_nt(a, b):
    return lax.dot_general(a, b, (((1,), (1,)), ((), ())), preferred_element_type=F32)


def _dot_tn(a, b):
    return lax.dot_general(a, b, (((0,), (0,)), ((), ())), preferred_element_type=F32)


HBM_SPEC = pl.BlockSpec(memory_space=pltpu.HBM)
SEM_SPEC = pl.BlockSpec(memory_space=pltpu.SEMAPHORE)
DATAFLOW = pltpu.SideEffectType.DATAFLOW_SIDE_EFFECTING


def _device_index():
    return 4 * lax.axis_index("x") + 2 * lax.axis_index("y") + lax.axis_index("c")


def _peers():
    x, y, c = lax.axis_index("x"), lax.axis_index("y"), lax.axis_index("c")
    out = []
    for k in range(1, N_DEV):
        px = 1 - x if k & 4 else x
        py = 1 - y if k & 2 else y
        pc = 1 - c if k & 1 else c
        out.append((k, (px, py, pc), 4 * px + 2 * py + pc))
    return out


def _peer_copy(per_dest, src_ref, land_ref, send_sems, recv_sems, a, k, dev, peer):
    src = src_ref.at[peer] if per_dest else src_ref
    return pltpu.make_async_remote_copy(
        src_ref=src, dst_ref=land_ref.at[_device_index()], send_sem=send_sems.at[a * (N_DEV - 1) + k - 1],
        recv_sem=recv_sems.at[a * (N_DEV - 1) + k - 1], device_id=dev, device_id_type=pl.DeviceIdType.MESH)


ALL_PEERS = tuple(range(1, N_DEV))
CHIP_PEERS = (1, 2, 4, 6)
FAR_PEERS = (2, 4, 6)


def _own_copy(per_dest, src_ref, land_ref, send_sems, n, a):
    me = _device_index()
    return pltpu.make_async_copy(src_ref.at[me] if per_dest else src_ref, land_ref.at[me],
                                 send_sems.at[n * (N_DEV - 1) + a])


def _exchange_start(name, items, ks=ALL_PEERS):
    n = len(items)
    per_dest = [it[1] for it in items]

    def body(*refs):
        srcs, lands = refs[:n], refs[n:2 * n]
        send_sems, recv_sems, token = refs[2 * n], refs[2 * n + 1], refs[-1]
        for a in range(n):
            _own_copy(per_dest[a], srcs[a], lands[a], send_sems, n, a).start()
            for k, dev, peer in _peers():
                if k in ks:
                    _peer_copy(per_dest[a], srcs[a], lands[a], send_sems, recv_sems, a, k, dev, peer).start()
        token[...] = jnp.zeros(token.shape, F32)

    srcs = [pltpu.with_memory_space_constraint(it[0], pltpu.HBM) for it in items]
    lands = []
    for arr, pd in items:
        shp = arr.shape if pd else (N_DEV,) + arr.shape
        lands.append(pltpu.with_memory_space_constraint(lax.empty(shp, arr.dtype), pltpu.HBM))
    sems = pltpu.SemaphoreType.DMA((n * N_DEV,))
    res = pl.pallas_call(
        body, name=name,
        out_shape=(sems, sems, *[pltpu.HBM(a_.shape, a_.dtype) for a_ in srcs + lands],
                   jax.ShapeDtypeStruct((8, 128), F32)),
        in_specs=[HBM_SPEC] * (2 * n),
        out_specs=(SEM_SPEC, SEM_SPEC, *[HBM_SPEC] * (2 * n), pl.BlockSpec(memory_space=pltpu.VMEM)),
        input_output_aliases={i: 2 + i for i in range(2 * n)},
        compiler_params=pltpu.CompilerParams(has_side_effects=DATAFLOW),
    )(*srcs, *lands)
    return dict(per_dest=per_dest, ks=ks, send=res[0], recv=res[1], srcs=list(res[2:2 + n]),
                lands=list(res[2 + n:2 + 2 * n]), token=res[-1])


def _exchange_wait(name, started, after):
    per_dest = started["per_dest"]
    n = len(per_dest)

    def body(*refs):
        srcs, lands = refs[:n], refs[n:2 * n]
        send_sems, recv_sems = refs[2 * n], refs[2 * n + 1]
        for a in range(n):
            _own_copy(per_dest[a], srcs[a], lands[a], send_sems, n, a).wait()
            for k, dev, peer in _peers():
                if k in started["ks"]:
                    cp = _peer_copy(per_dest[a], srcs[a], lands[a], send_sems, recv_sems, a, k, dev, peer)
                    cp.wait_send()
                    cp.wait_recv()

    bufs = started["srcs"] + started["lands"]
    after = list(after) if isinstance(after, (list, tuple)) else [after]
    res = pl.pallas_call(
        body, name=name, out_shape=tuple(pltpu.HBM(b_.shape, b_.dtype) for b_ in bufs),
        in_specs=[HBM_SPEC] * (2 * n) + [SEM_SPEC, SEM_SPEC] + [pl.BlockSpec(memory_space=pl.ANY)] * len(after),
        out_specs=tuple([HBM_SPEC] * (2 * n)), input_output_aliases={i: i for i in range(2 * n)},
        compiler_params=pltpu.CompilerParams(has_side_effects=DATAFLOW),
    )(*bufs, started["send"], started["recv"], *after)
    return list(res[n:])


def _pass_on_copy(land_ref, send_sems, recv_sems, a, idx, slot):
    sibling = (lax.axis_index("x"), lax.axis_index("y"), 1 - lax.axis_index("c"))
    return pltpu.make_async_remote_copy(
        src_ref=land_ref.at[slot], dst_ref=land_ref.at[slot], send_sem=send_sems.at[a * len(FAR_PEERS) + idx],
        recv_sem=recv_sems.at[a * len(FAR_PEERS) + idx], device_id=sibling, device_id_type=pl.DeviceIdType.MESH)


def _pass_on_start(name, lands):
    n = len(lands)

    def body(*refs):
        send_sems, recv_sems, token = refs[n], refs[n + 1], refs[-1]
        slots = {k: peer for k, _, peer in _peers()}
        for a in range(n):
            for idx, k in enumerate(FAR_PEERS):
                _pass_on_copy(refs[a], send_sems, recv_sems, a, idx, slots[k]).start()
        token[...] = jnp.zeros(token.shape, F32)

    lands = [pltpu.with_memory_space_constraint(l_, pltpu.HBM) for l_ in lands]
    sems = pltpu.SemaphoreType.DMA((n * len(FAR_PEERS),))
    res = pl.pallas_call(
        body, name=name,
        out_shape=(sems, sems, *[pltpu.HBM(l_.shape, l_.dtype) for l_ in lands], jax.ShapeDtypeStruct((8, 128), F32)),
        in_specs=[HBM_SPEC] * n, out_specs=(SEM_SPEC, SEM_SPEC, *[HBM_SPEC] * n, pl.BlockSpec(memory_space=pltpu.VMEM)),
        input_output_aliases={i: 2 + i for i in range(n)},
        compiler_params=pltpu.CompilerParams(has_side_effects=DATAFLOW),
    )(*lands)
    return dict(send=res[0], recv=res[1], lands=list(res[2:2 + n]), token=res[-1])


def _pass_on_wait(name, passed, after):
    n = len(passed["lands"])

    def body(*refs):
        send_sems, recv_sems = refs[n], refs[n + 1]
        slots = {k: peer for k, _, peer in _peers()}
        for a in range(n):
            for idx, k in enumerate(FAR_PEERS):
                _pass_on_copy(refs[a], send_sems, recv_sems, a, idx, slots[k]).wait_send()
                _pass_on_copy(refs[a], send_sems, recv_sems, a, idx, slots[k ^ 1]).wait_recv()

    res = pl.pallas_call(
        body, name=name, out_shape=tuple(pltpu.HBM(l_.shape, l_.dtype) for l_ in passed["lands"]),
        in_specs=[HBM_SPEC] * n + [SEM_SPEC, SEM_SPEC, pl.BlockSpec(memory_space=pl.ANY)],
        out_specs=tuple([HBM_SPEC] * n), input_output_aliases={i: i for i in range(n)},
        compiler_params=pltpu.CompilerParams(has_side_effects=DATAFLOW),
    )(*passed["lands"], passed["send"], passed["recv"], after)
    return list(res)


def _mm_nn(name, x, w, w_spec, n_out, tn, out_dtype, relu=False, square_x=False, tm=MM_TM):
    m, k = x.shape

    def body(x_ref, w_ref, *outs):
        xv = x_ref[...]
        if square_x:
            xv = xv * xv
        acc = jnp.dot(xv, w_ref[...], preferred_element_type=F32)
        if relu:
            acc = jnp.maximum(acc, 0.0)
        outs[0][...] = acc.astype(outs[0].dtype)

    o_spec = pl.BlockSpec((tm, tn), lambda i, j: (i, j))
    return pl.pallas_call(
        body, name=name, out_shape=jax.ShapeDtypeStruct((m, n_out), out_dtype), grid=(m // tm, n_out // tn),
        in_specs=[pl.BlockSpec((tm, k), lambda i, j: (i, 0)), w_spec],
        out_specs=o_spec, compiler_params=_params(("parallel", "parallel")),
    )(x, w)


def _mm_nt(name, pairs, m, n_out, tn, out_dtype, relu_bwd_of=None, tm=MM_TM, after=None):
    np_ = len(pairs)

    def body(*refs):
        acc = None
        for p in range(np_):
            if len(pairs[p]) == 5:
                d = jnp.dot(refs[2 * p][...], refs[2 * p + 1][...], preferred_element_type=F32)
            else:
                d = _dot_nt(refs[2 * p][...], refs[2 * p + 1][...])
            acc = d if acc is None else acc + d
        if relu_bwd_of is not None:
            acc = acc * (2.0 * refs[2 * np_][...].astype(F32))
        refs[-1][...] = acc.astype(refs[-1].dtype)

    o_spec = pl.BlockSpec((tm, tn), lambda i, j: (i, j))
    operands, specs = [], []
    for pair in pairs:
        operands += [pair[0], pair[2]]
        specs += [pair[1], pair[3]]
    if relu_bwd_of is not None:
        operands.append(relu_bwd_of)
        specs.append(o_spec)
    if after is not None:
        operands.append(after)
        specs.append(pl.BlockSpec((8, 128), lambda i, j: (0, 0)))
    return pl.pallas_call(
        body, name=name, out_shape=jax.ShapeDtypeStruct((m, n_out), out_dtype), grid=(m // tm, n_out // tn),
        in_specs=specs, out_specs=o_spec, compiler_params=_params(("parallel", "parallel")),
    )(*operands)


def _mm_tn(name, x, x_col, ta, dy, dy_col, tb, out_shape, out_spec, grid_ab):
    m = x.shape[0]

    def body(x_ref, dy_ref, o_ref):
        o_ref[...] = _dot_tn(x_ref[...], dy_ref[...]).astype(BF16)

    return pl.pallas_call(
        body, name=name, out_shape=jax.ShapeDtypeStruct(out_shape, BF16), grid=grid_ab,
        in_specs=[pl.BlockSpec((m, ta), lambda a, b: (0, x_col(a))),
                  pl.BlockSpec((m, tb), lambda a, b: (0, dy_col(b)))],
        out_specs=out_spec, compiler_params=_params(("parallel", "parallel")),
    )(x, dy)


def _w_cols(k, tn, off_blocks):
    return pl.BlockSpec((k, tn), lambda i, j: (0, off_blocks + j))


def _a_rows(kw, col_block=0, tm=MM_TM):
    return pl.BlockSpec((tm, kw), lambda i, j: (i, col_block))


def _w_rows(tn, kw, col_block=0):
    return pl.BlockSpec((tn, kw), lambda i, j: (j, col_block))


def _wt_rows(tn, off):
    return pl.BlockSpec((pl.Element(tn), pl.Element(SEG)), lambda i, j: (pl.multiple_of(off + tn * j, 16), 0))


def _wt_block(k, off, tn):
    return pl.BlockSpec((pl.Element(k), pl.Element(tn)), lambda i, j: (off, pl.multiple_of(tn * j, 128)))


GW_TILE = 512


def _grad_w_rows(name, xs, dy, square_x=False, gap=None):
    m, nb = dy.shape
    per = [x_.shape[1] // GW_TILE for x_ in xs]
    steps = sum(per)
    tiles = [(k, h) for k in range(len(xs)) for h in range(per[k])]
    gap_tile, gap_size = (gap[0] // GW_TILE, gap[1]) if gap else (steps, 0)

    def out_row(i):
        return pl.multiple_of(i * GW_TILE + jnp.where(i >= gap_tile, gap_size, 0), 16), 0

    def body(*refs):
        x_refs, dy_ref, o_ref, buf, sems = refs[:len(xs)], refs[len(xs)], refs[len(xs) + 1], refs[-2], refs[-1]
        s_ = pl.program_id(0)

        def fetch(step, slot):
            for idx, (k, h) in enumerate(tiles):
                @pl.when(step == idx)
                def _():
                    pltpu.make_async_copy(x_refs[k].at[:, pl.ds(h * GW_TILE, GW_TILE)], buf.at[slot], sems.at[slot]).start()

        @pl.when(s_ == 0)
        def _():
            fetch(s_, 0)

        @pl.when(s_ + 1 < steps)
        def _():
            fetch(s_ + 1, (s_ + 1) % 2)

        slot = s_ % 2
        pltpu.make_async_copy(x_refs[0].at[:, pl.ds(0, GW_TILE)], buf.at[slot], sems.at[slot]).wait()
        xv = buf[slot]
        if square_x:
            xv = xv * xv
        o_ref[...] = _dot_tn(xv, dy_ref[...]).astype(BF16)

    return pl.pallas_call(
        body, name=name, out_shape=jax.ShapeDtypeStruct((steps * GW_TILE + gap_size, nb), BF16), grid=(steps,),
        in_specs=[pl.BlockSpec(memory_space=pl.ANY)] * len(xs) + [pl.BlockSpec((m, nb), lambda i: (0, 0))],
        out_specs=pl.BlockSpec((pl.Element(GW_TILE), pl.Element(nb)), out_row),
        scratch_shapes=[pltpu.VMEM((2, m, GW_TILE), BF16), pltpu.SemaphoreType.DMA((2,))],
        compiler_params=_params(("arbitrary",)))(*xs, dy)


def _grad_w(name, x, dy):
    na, nb = x.shape[1], dy.shape[1]
    ta, tb = min(na, 1024), min(nb, 512)
    return _mm_tn(name, x, lambda a: a, ta, dy, lambda b: b, tb, (na, nb),
                  pl.BlockSpec((ta, tb), lambda a, b: (a, b)), (na // ta, nb // tb))


def _row_spec(width):
    return pl.BlockSpec((ROW_TILE, width), lambda i: (i, 0))


def _vec_spec(width):
    return pl.BlockSpec((1, width), lambda i: (0, 0))


def _rms_fwd(name, h, g, res=None):
    n, d = h.shape
    tile = ROW_TILE if n % ROW_TILE == 0 else n
    row_spec = pl.BlockSpec((tile, d), lambda i: (i, 0))

    def body(*refs):
        if res is None:
            h_ref, g_ref, hn_ref = refs
            hv = h_ref[...]
        else:
            h_ref, r_ref, g_ref, hs_ref, hn_ref = refs
            hv = h_ref[...] + r_ref[...].astype(F32)
            hs_ref[...] = hv
        r = lax.rsqrt(jnp.mean(hv * hv, axis=-1, keepdims=True) + RMS_EPS)
        hn_ref[...] = (hv * r * g_ref[...]).astype(BF16)

    ins = [h, g] if res is None else [h, res, g]
    in_specs = [row_spec, _vec_spec(d)] if res is None else [row_spec, row_spec, _vec_spec(d)]
    hn_shape = jax.ShapeDtypeStruct((n, d), BF16)
    if res is None:
        out_shape, out_specs = hn_shape, row_spec
    else:
        out_shape, out_specs = [jax.ShapeDtypeStruct((n, d), F32), hn_shape], [row_spec, row_spec]
    return pl.pallas_call(body, name=name, out_shape=out_shape, grid=(n // tile,), in_specs=in_specs,
                          out_specs=out_specs, compiler_params=_params(("parallel",)))(*ins)


def _rms_bwd(name, dhn, h, g, dres, batch, dh_dtype=F32, with_meta=False):
    n, d = h.shape
    t = n // batch
    nt = t // ROW_TILE

    def body(dhn_ref, h_ref, g_ref, dres_ref, *outs):
        first = (pl.program_id(0) == 0) & (pl.program_id(1) == 0)
        hv = h_ref[...]
        r = lax.rsqrt(jnp.mean(hv * hv, axis=-1, keepdims=True) + RMS_EPS)
        nrm = hv * r
        dhn = dhn_ref[...].astype(F32)
        dn = dhn * g_ref[...]
        dh = dres_ref[...].astype(F32) + r * (dn - nrm * jnp.mean(dn * nrm, axis=-1, keepdims=True))
        outs[0][...] = dh.astype(dh_dtype)
        dg_ref = outs[1]

        @pl.when(first)
        def _():
            dg_ref[...] = jnp.zeros(dg_ref.shape, F32)

        dg_ref[...] += jnp.sum(dhn * nrm, axis=0, keepdims=True)
        if with_meta:
            meta_ref = outs[2]

            @pl.when(first)
            def _():
                meta_ref[...] = jnp.zeros(meta_ref.shape, F32)

            @pl.when(pl.program_id(1) == 0)
            def _():
                meta_ref[...] += dh[0:N_META, :]

    row = pl.BlockSpec((ROW_TILE, d), lambda b, j: (b * nt + j, 0))
    vec = pl.BlockSpec((1, d), lambda b, j: (0, 0))
    shapes = [jax.ShapeDtypeStruct((n, d), dh_dtype), jax.ShapeDtypeStruct((1, d), F32)]
    specs = [row, vec]
    if with_meta:
        shapes.append(jax.ShapeDtypeStruct((N_META, d), F32))
        specs.append(pl.BlockSpec((N_META, d), lambda b, j: (0, 0)))
    return pl.pallas_call(body, name=name, out_shape=shapes, grid=(batch, nt), in_specs=[row, row, vec, row],
                          out_specs=specs, compiler_params=_params(("arbitrary", "arbitrary")))(dhn, h, g, dres)


def _final(name, h1, dn, tgt, g, batch):
    n, d = h1.shape
    t = n // batch
    nt = t // ROW_TILE

    def body(h1_ref, dn_ref, tgt_ref, g_ref, dhb_ref, loss_ref, dg_ref):
        first = (pl.program_id(0) == 0) & (pl.program_id(1) == 0)
        hv = h1_ref[...] + dn_ref[...].astype(F32)
        r = lax.rsqrt(jnp.mean(hv * hv, axis=-1, keepdims=True) + RMS_EPS)
        nrm = hv * r
        gv = g_ref[...]
        pos = pl.program_id(1) * ROW_TILE + lax.broadcasted_iota(jnp.int32, (ROW_TILE, 1), 0)
        diff = jnp.where(pos >= N_META, nrm * gv - tgt_ref[...], 0.0)
        dy = diff * (1.0 / d)

        @pl.when(first)
        def _():
            loss_ref[...] = jnp.zeros(loss_ref.shape, F32)
            dg_ref[...] = jnp.zeros(dg_ref.shape, F32)

        loss_ref[...] += jnp.full(loss_ref.shape, 0.5 / d, F32) * jnp.sum(diff * diff)
        dg_ref[...] += jnp.sum(dy * nrm, axis=0, keepdims=True)
        dng = dy * gv
        dh = r * (dng - nrm * jnp.mean(dng * nrm, axis=-1, keepdims=True))
        dhb_ref[...] = dh.astype(BF16)

    row = pl.BlockSpec((ROW_TILE, d), lambda b, j: (b * nt + j, 0))
    vec = pl.BlockSpec((1, d), lambda b, j: (0, 0))
    return pl.pallas_call(
        body, name=name, grid=(batch, nt), in_specs=[row, row, row, vec],
        out_shape=[jax.ShapeDtypeStruct((n, d), BF16), jax.ShapeDtypeStruct((8, 128), F32),
                   jax.ShapeDtypeStruct((1, d), F32)],
        out_specs=[row, pl.BlockSpec((8, 128), lambda b, j: (0, 0)), vec],
        compiler_params=_params(("arbitrary", "arbitrary")))(h1, dn, tgt, g)


def _ln_silu_fwd(name, c1, g, b):
    n, d = c1.shape

    def body(c_ref, g_ref, b_ref, o_ref):
        xv = c_ref[...].astype(F32)
        xc = xv - jnp.mean(xv, axis=-1, keepdims=True)
        rstd = lax.rsqrt(jnp.mean(xc * xc, axis=-1, keepdims=True) + LN_EPS)
        c2 = xc * rstd * g_ref[...] + b_ref[...]
        o_ref[...] = (c2 * _sigmoid(c2)).astype(BF16)

    return pl.pallas_call(body, name=name, out_shape=jax.ShapeDtypeStruct((n, d), BF16), grid=(n // ROW_TILE,),
                          in_specs=[_row_spec(d), _vec_spec(d), _vec_spec(d)], out_specs=_row_spec(d),
                          compiler_params=_params(("parallel",)))(c1, g, b)


def _ln_silu_bwd(name, dc3, c1, g, b):
    n, d = c1.shape

    def body(d_ref, c_ref, g_ref, b_ref, dc1_ref, dg_ref, db_ref):
        xv = c_ref[...].astype(F32)
        xc = xv - jnp.mean(xv, axis=-1, keepdims=True)
        rstd = lax.rsqrt(jnp.mean(xc * xc, axis=-1, keepdims=True) + LN_EPS)
        xh = xc * rstd
        c2 = xh * g_ref[...] + b_ref[...]
        s = _sigmoid(c2)
        dc2 = d_ref[...].astype(F32) * (s * (1.0 + c2 * (1.0 - s)))

        @pl.when(pl.program_id(0) == 0)
        def _():
            dg_ref[...] = jnp.zeros(dg_ref.shape, F32)
            db_ref[...] = jnp.zeros(db_ref.shape, F32)

        dg_ref[...] += jnp.sum(dc2 * xh, axis=0, keepdims=True)
        db_ref[...] += jnp.sum(dc2, axis=0, keepdims=True)
        dxh = dc2 * g_ref[...]
        dc1_ref[...] = (rstd * (dxh - jnp.mean(dxh, axis=-1, keepdims=True)
                                - xh * jnp.mean(dxh * xh, axis=-1, keepdims=True))).astype(BF16)

    return pl.pallas_call(
        body, name=name, grid=(n // ROW_TILE,),
        out_shape=[jax.ShapeDtypeStruct((n, d), BF16), jax.ShapeDtypeStruct((1, d), F32),
                   jax.ShapeDtypeStruct((1, d), F32)],
        in_specs=[_row_spec(d), _row_spec(d), _vec_spec(d), _vec_spec(d)],
        out_specs=[_row_spec(d), _vec_spec(d), _vec_spec(d)],
        compiler_params=_params(("arbitrary",)))(dc3, c1, g, b)


MERGE_TC = 512


def _merge_fwd(name, gates, a, c, b_co):
    n, d = a.shape
    nc = d // MERGE_TC

    def body(ga_ref, gc_ref, a_ref, c_ref, b_ref, m_ref):
        f32 = lambda r_: r_[...].astype(F32)
        m = _sigmoid(f32(ga_ref)) * f32(a_ref) + _sigmoid(f32(gc_ref)) * (f32(c_ref) + b_ref[...])
        m_ref[...] = m.astype(BF16)

    blk = lambda off: pl.BlockSpec((ROW_TILE, MERGE_TC), lambda i, j: (i, off + j))
    return pl.pallas_call(
        body, name=name, out_shape=jax.ShapeDtypeStruct((n, d), BF16), grid=(n // ROW_TILE, nc),
        in_specs=[blk(0), blk(nc), blk(0), blk(0), pl.BlockSpec((1, MERGE_TC), lambda i, j: (0, j))],
        out_specs=blk(0), compiler_params=_params(("parallel", "parallel")))(gates, gates, a, c, b_co)


def _merge_bwd(name, dm, gates, a, c, b_co):
    n, d = a.shape
    nc = d // MERGE_TC

    def body(dm_ref, ga_ref, gc_ref, a_ref, c_ref, b_ref, da_ref, dc_ref, dga_ref, dgc_ref, dbco_ref):
        f32 = lambda r_: r_[...].astype(F32)
        dmv = f32(dm_ref)
        sa, sc = _sigmoid(f32(ga_ref)), _sigmoid(f32(gc_ref))
        dc = dmv * sc
        da_ref[...] = (dmv * sa).astype(BF16)
        dc_ref[...] = dc.astype(BF16)
        dga_ref[...] = (dmv * f32(a_ref) * sa * (1.0 - sa)).astype(BF16)
        dgc_ref[...] = (dmv * (f32(c_ref) + b_ref[...]) * sc * (1.0 - sc)).astype(BF16)

        @pl.when(pl.program_id(1) == 0)
        def _():
            dbco_ref[...] = jnp.zeros(dbco_ref.shape, F32)

        dbco_ref[...] += jnp.sum(dc, axis=0, keepdims=True)

    blk = lambda off: pl.BlockSpec((ROW_TILE, MERGE_TC), lambda j, i: (i, off + j))
    vec = pl.BlockSpec((1, MERGE_TC), lambda j, i: (0, j))
    act = jax.ShapeDtypeStruct((n, d), BF16)
    return pl.pallas_call(
        body, name=name, grid=(nc, n // ROW_TILE),
        out_shape=[act, act, act, act, jax.ShapeDtypeStruct((1, d), F32)],
        in_specs=[blk(0), blk(0), blk(nc), blk(0), blk(0), vec],
        out_specs=[blk(0), blk(0), blk(0), blk(0), vec],
        compiler_params=_params(("parallel", "arbitrary")))(dm, gates, gates, a, c, b_co)


CONV_TC = 128
CONV_HALO = 32


def _conv_chunk(t):
    return 48 if t % 48 == 0 else 32 if t % 32 == 0 else 16


def _fold8(x):
    out = x[0:8]
    for k in range(1, x.shape[0] // 8):
        out = out + x[8 * k:8 * k + 8]
    return out


def _glu_conv_fwd(name, glu, b_glu, w_dw, b_dw, batch):
    n, c2 = glu.shape
    c = c2 // 2
    t = n // batch
    nc = c // CONV_TC

    def body(a_ref, gt_ref, ba_ref, bg_ref, w_ref, bdw_ref, o_ref, pad_ref):
        u = (a_ref[...].astype(F32) + ba_ref[...]) * _sigmoid(gt_ref[...].astype(F32) + bg_ref[...])
        pad_ref[0:CONV_HALO, :] = jnp.zeros((CONV_HALO, CONV_TC), F32)
        pad_ref[CONV_HALO:CONV_HALO + t, :] = u
        ch = _conv_chunk(t)
        for r0 in range(0, t, ch):
            acc = jnp.zeros((ch, CONV_TC), F32) + bdw_ref[...]
            for j in range(CONV_W):
                off = r0 + CONV_HALO - (CONV_W - 1) + j
                acc = acc + w_ref[j:j + 1, :] * pad_ref[off:off + ch, :]
            o_ref[r0:r0 + ch, :] = acc.astype(BF16)

    seq = lambda off: pl.BlockSpec((t, CONV_TC), lambda b, j: (b, off + j))
    vec = lambda off: pl.BlockSpec((1, CONV_TC), lambda b, j: (0, off + j))
    return pl.pallas_call(
        body, name=name, out_shape=jax.ShapeDtypeStruct((n, c), BF16), grid=(batch, nc),
        in_specs=[seq(0), seq(nc), vec(0), vec(nc), pl.BlockSpec((CONV_W, CONV_TC), lambda b, j: (0, j)), vec(0)],
        out_specs=seq(0), scratch_shapes=[pltpu.VMEM((t + CONV_HALO, CONV_TC), F32)],
        compiler_params=_params(("parallel", "parallel")))(glu, glu, b_glu, b_glu, w_dw, b_dw)


def _glu_conv_bwd(name, dc1, glu, b_glu, w_dw, batch):
    n, c2 = glu.shape
    c = c2 // 2
    t = n // batch
    nc = c // CONV_TC

    def body(d_ref, a_ref, gt_ref, ba_ref, bg_ref, w_ref, dga_ref, dgg_ref, dw_ref, dbdw_ref, dba_ref, dbg_ref,
             padu_ref, padd_ref):
        av = a_ref[...].astype(F32) + ba_ref[...]
        sg = _sigmoid(gt_ref[...].astype(F32) + bg_ref[...])
        dc = d_ref[...].astype(F32)
        padu_ref[0:CONV_HALO, :] = jnp.zeros((CONV_HALO, CONV_TC), F32)
        padu_ref[CONV_HALO:CONV_HALO + t, :] = av * sg
        padd_ref[0:t, :] = dc
        padd_ref[t:t + CONV_HALO, :] = jnp.zeros((CONV_HALO, CONV_TC), F32)

        @pl.when(pl.program_id(1) == 0)
        def _():
            dw_ref[...] = jnp.zeros(dw_ref.shape, F32)
            dbdw_ref[...] = jnp.zeros(dbdw_ref.shape, F32)
            dba_ref[...] = jnp.zeros(dba_ref.shape, F32)
            dbg_ref[...] = jnp.zeros(dbg_ref.shape, F32)

        ch = _conv_chunk(t)
        zero8 = jnp.zeros((8, CONV_TC), F32)
        dw_acc = [zero8] * CONV_W
        sum_dc, sum_a, sum_g = zero8, zero8, zero8
        for r0 in range(0, t, ch):
            dcc = d_ref[r0:r0 + ch, :].astype(F32)
            du = jnp.zeros((ch, CONV_TC), F32)
            for j in range(CONV_W):
                back = r0 + CONV_W - 1 - j
                du = du + w_ref[j:j + 1, :] * padd_ref[back:back + ch, :]
                off = r0 + CONV_HALO - (CONV_W - 1) + j
                dw_acc[j] = dw_acc[j] + _fold8(dcc * padu_ref[off:off + ch, :])
            sgc = _sigmoid(gt_ref[r0:r0 + ch, :].astype(F32) + bg_ref[...])
            dga = du * sgc
            dgg = du * padu_ref[CONV_HALO + r0:CONV_HALO + r0 + ch, :] * (1.0 - sgc)
            dga_ref[r0:r0 + ch, :] = dga.astype(BF16)
            dgg_ref[r0:r0 + ch, :] = dgg.astype(BF16)
            sum_dc, sum_a, sum_g = sum_dc + _fold8(dcc), sum_a + _fold8(dga), sum_g + _fold8(dgg)
        for j in range(CONV_W):
            dw_ref[j:j + 1, :] += jnp.sum(dw_acc[j], axis=0, keepdims=True)
        dbdw_ref[...] += jnp.sum(sum_dc, axis=0, keepdims=True)
        dba_ref[...] += jnp.sum(sum_a, axis=0, keepdims=True)
        dbg_ref[...] += jnp.sum(sum_g, axis=0, keepdims=True)

    seq = lambda off: pl.BlockSpec((t, CONV_TC), lambda j, b: (b, off + j))
    vec = lambda off: pl.BlockSpec((1, CONV_TC), lambda j, b: (0, off + j))
    wsp = pl.BlockSpec((CONV_W, CONV_TC), lambda j, b: (0, j))
    act = jax.ShapeDtypeStruct((n, c), BF16)
    v = jax.ShapeDtypeStruct((1, c), F32)
    return pl.pallas_call(
        body, name=name, grid=(nc, batch),
        out_shape=[act, act, jax.ShapeDtypeStruct((CONV_W, c), F32), v, v, v],
        in_specs=[seq(0), seq(0), seq(nc), vec(0), vec(nc), wsp],
        out_specs=[seq(0), seq(0), wsp, vec(0), vec(0), vec(0)],
        scratch_shapes=[pltpu.VMEM((t + CONV_HALO, CONV_TC), F32), pltpu.VMEM((t + CONV_HALO, CONV_TC), F32)],
        compiler_params=_params(("parallel", "arbitrary")))(dc1, glu, glu, b_glu, b_glu, w_dw)


def _split3(x):
    hi = x.astype(BF16)
    r = x - hi.astype(F32)
    mid = r.astype(BF16)
    lo = (r - mid.astype(F32)).astype(BF16)
    return hi, mid, lo


def _tri_matmul(tri, x):
    hi, mid, lo = _split3(x)
    dot = lambda v: jnp.dot(tri, v, preferred_element_type=F32)
    return dot(hi) + dot(mid) + dot(lo)


def _fox_prep_fwd(name, fg, b_fg, batch):
    n, w = fg.shape
    t = n // batch
    nq = t // ROW_TILE

    def body(fg_ref, b_ref, cum_ref):
        row = lax.broadcasted_iota(jnp.int32, (ROW_TILE, ROW_TILE), 0)
        col = lax.broadcasted_iota(jnp.int32, (ROW_TILE, ROW_TILE), 1)
        tri = (row >= col).astype(BF16)
        for k in range(nq):
            rows = slice(k * ROW_TILE, (k + 1) * ROW_TILE)
            z = fg_ref[rows, :] + b_ref[...]
            logf = jnp.minimum(z, 0.0) - jnp.log(1.0 + jnp.exp(-jnp.abs(z)))
            cum = _tri_matmul(tri, logf)
            if k > 0:
                cum = cum + cum_ref[k * ROW_TILE - 1:k * ROW_TILE, :]
            cum_ref[rows, :] = cum

    seq = pl.BlockSpec((t, w), lambda b: (b, 0))
    return pl.pallas_call(body, name=name, out_shape=jax.ShapeDtypeStruct((n, w), F32), grid=(batch,),
                          in_specs=[seq, pl.BlockSpec((1, w), lambda b: (0, 0))], out_specs=seq,
                          compiler_params=_params(("parallel",)))(fg, b_fg)


def _fox_prep_bwd(name, dcum_k, dcum_q, fg, b_fg, batch):
    n, w = fg.shape
    t = n // batch
    nq = t // ROW_TILE

    def body(dk_ref, dq_ref, fg_ref, b_ref, dfg_ref, db_ref, rev_ref):
        row = lax.broadcasted_iota(jnp.int32, (ROW_TILE, ROW_TILE), 0)
        col = lax.broadcasted_iota(jnp.int32, (ROW_TILE, ROW_TILE), 1)
        tri = (col >= row).astype(BF16)

        @pl.when(pl.program_id(0) == 0)
        def _():
            db_ref[...] = jnp.zeros(db_ref.shape, F32)

        for k in reversed(range(nq)):
            rows = slice(k * ROW_TILE, (k + 1) * ROW_TILE)
            dlog = _tri_matmul(tri, dk_ref[rows, :] + dq_ref[rows, :])
            if k < nq - 1:
                dlog = dlog + rev_ref[(k + 1) * ROW_TILE:(k + 1) * ROW_TILE + 1, :]
            rev_ref[rows, :] = dlog
            dfg = dlog * _sigmoid(-(fg_ref[rows, :] + b_ref[...]))
            dfg_ref[rows, :] = dfg.astype(BF16)
            db_ref[...] += jnp.sum(dfg, axis=0, keepdims=True)

    seq = pl.BlockSpec((t, w), lambda b: (b, 0))
    vec = pl.BlockSpec((1, w), lambda b: (0, 0))
    return pl.pallas_call(
        body, name=name, grid=(batch,),
        out_shape=[jax.ShapeDtypeStruct((n, w), BF16), jax.ShapeDtypeStruct((1, w), F32)],
        in_specs=[seq, seq, seq, vec], out_specs=[seq, vec], scratch_shapes=[pltpu.VMEM((t, w), F32)],
        compiler_params=_params(("arbitrary",)))(dcum_k, dcum_q, fg, b_fg)


def _head_masks(x):
    lane = lax.broadcasted_iota(jnp.int32, x.shape, 1)
    zero = jnp.zeros(x.shape, x.dtype)
    return jnp.where(lane < HEAD_DIM, x, zero), jnp.where(lane >= HEAD_DIM, x, zero)


ATTN_BLOCK = 512


def _attn_blocks(t):
    nb = max(t // ATTN_BLOCK, 1)
    blocks = [(i * ATTN_BLOCK, ATTN_BLOCK) for i in range(nb - 1)]
    return blocks + [((nb - 1) * ATTN_BLOCK, t - (nb - 1) * ATTN_BLOCK)]


def _attn_specs(t):
    blocks = _attn_blocks(t)
    width = max(sz for _, sz in blocks)
    qkv = lambda off: pl.BlockSpec((t, LANES), lambda b, h: (b, off + h))
    cumr = pl.BlockSpec((None, None, len(blocks), 8, width), lambda b, h: (b, h, 0, 0, 0))
    return qkv, cumr


def _key_sums_to_blocks(cum, batch, t, n_pairs):
    blocks = _attn_blocks(t)
    width = max(sz for _, sz in blocks)
    cum_h = cum.reshape(batch, t, -1)[:, :, :2 * n_pairs].reshape(batch, t, n_pairs, 2)
    rows = [jnp.pad(jnp.transpose(cum_h[:, s0:s0 + sz], (0, 2, 3, 1)), ((0, 0), (0, 0), (0, 6), (0, width - sz)))
            for s0, sz in blocks]
    return jnp.stack(rows, axis=2)


def _key_sums_from_blocks(dcumr, batch, t, n_pairs):
    cols = [jnp.transpose(dcumr[:, :, j, :2, :sz], (0, 3, 1, 2)) for j, (_, sz) in enumerate(_attn_blocks(t))]
    return jnp.concatenate(cols, axis=1).reshape(batch * t, 2 * n_pairs)


def _causal(size):
    row = lax.broadcasted_iota(jnp.int32, (size, size), 0)
    col = lax.broadcasted_iota(jnp.int32, (size, size), 1)
    return row >= col


def _attn_fwd(name, qkv, cumr, batch):
    n, w3 = qkv.shape
    w = w3 // 3
    t = n // batch
    n_pairs = w // LANES
    blocks = _attn_blocks(t)

    def body(q_ref, k_ref, v_ref, cr_ref, o_ref, lse_ref):
        pair = pl.program_id(1)

        @pl.when(pair == 0)
        def _():
            lse_ref[...] = jnp.zeros(lse_ref.shape, F32)

        for i, (q0, qn) in enumerate(blocks):
            rows = slice(q0, q0 + qn)
            qs = _head_masks(q_ref[rows, :] * (0.125 * LOG2E))
            outs, lses = [], []
            for hh in range(2):
                m = jnp.full((qn, 1), NEG, F32)
                l = jnp.zeros((qn, 1), F32)
                acc = jnp.zeros((qn, LANES), F32)
                for j in range(i + 1):
                    k0, kn = blocks[j]
                    cols = slice(k0, k0 + kn)
                    s = _dot_nt(qs[hh], k_ref[cols, :]) - cr_ref[j, hh:hh + 1, 0:kn] * LOG2E
                    if j == i:
                        s = jnp.where(_causal(qn), s, NEG)
                    m_new = jnp.maximum(m, jnp.max(s, axis=1, keepdims=True))
                    alpha = jnp.exp2(m - m_new)
                    p = jnp.exp2(s - m_new)
                    l = alpha * l + jnp.sum(p, axis=1, keepdims=True)
                    acc = alpha * acc + jnp.dot(p.astype(BF16), v_ref[cols, :], preferred_element_type=F32)
                    m = m_new
                outs.append(acc / l)
                lses.append(m + jnp.log2(l))
            lane = lax.broadcasted_iota(jnp.int32, (qn, LANES), 1)
            o_ref[rows, :] = jnp.where(lane < HEAD_DIM, outs[0], outs[1]).astype(BF16)
            lse_ref[rows, :] = jnp.where(lane == 2 * pair, lses[0],
                                         jnp.where(lane == 2 * pair + 1, lses[1], lse_ref[rows, :]))

    qkv_spec, cumr_spec = _attn_specs(t)
    return pl.pallas_call(
        body, name=name, grid=(batch, n_pairs),
        out_shape=[jax.ShapeDtypeStruct((n, w), BF16), jax.ShapeDtypeStruct((n, LANES), F32)],
        in_specs=[qkv_spec(0), qkv_spec(n_pairs), qkv_spec(2 * n_pairs), cumr_spec],
        out_specs=[qkv_spec(0), pl.BlockSpec((t, LANES), lambda b, h: (b, 0))],
        compiler_params=_params(("parallel", "arbitrary")))(qkv, qkv, qkv, cumr)


def _attn_bwd(name, qkv, o, do, lse, cumr, batch):
    n, w3 = qkv.shape
    w = w3 // 3
    t = n // batch
    n_pairs = w // LANES
    blocks = _attn_blocks(t)

    def body(q_ref, k_ref, v_ref, o_ref, do_ref, lse_ref, cr_ref, dq_ref, dk_ref, dv_ref, dcr_ref, dcq_ref,
             dk_acc, dv_acc):
        pair = pl.program_id(1)
        dk_acc[...] = jnp.zeros(dk_acc.shape, F32)
        dv_acc[...] = jnp.zeros(dv_acc.shape, F32)
        dcr_ref[...] = jnp.zeros(dcr_ref.shape, F32)

        @pl.when(pair == 0)
        def _():
            dcq_ref[...] = jnp.zeros(dcq_ref.shape, F32)

        for i, (q0, qn) in enumerate(blocks):
            rows = slice(q0, q0 + qn)
            qs = _head_masks(q_ref[rows, :] * 0.125)
            q2 = _head_masks(q_ref[rows, :] * (0.125 * LOG2E))
            dos = _head_masks(do_ref[rows, :])
            dq = jnp.zeros((qn, LANES), F32)
            lane = lax.broadcasted_iota(jnp.int32, (qn, LANES), 1)
            dcq = []
            for hh in range(2):
                row_sum = jnp.zeros((qn, 1), F32)
                lse = jnp.sum(jnp.where(lane == 2 * pair + hh, lse_ref[rows, :], 0.0), axis=1, keepdims=True)
                delta = jnp.sum(dos[hh].astype(F32) * o_ref[rows, :].astype(F32), axis=1, keepdims=True)
                for j in range(i + 1):
                    k0, kn = blocks[j]
                    cols = slice(k0, k0 + kn)
                    s = _dot_nt(q2[hh], k_ref[cols, :]) - cr_ref[j, hh:hh + 1, 0:kn] * LOG2E
                    p = jnp.exp2(s - lse)
                    if j == i:
                        p = jnp.where(_causal(qn), p, 0.0)
                    dp = _dot_nt(dos[hh], v_ref[cols, :])
                    ds = p * (dp - delta)
                    pb, dsb = p.astype(BF16), ds.astype(BF16)
                    km = _head_masks(k_ref[cols, :])[hh]
                    dv_acc[j, :, 0:kn] += _dot_tn(dos[hh], pb)
                    dk_acc[j, :, 0:kn] += _dot_tn(qs[hh], dsb)
                    dq = dq + jnp.dot(dsb, km, preferred_element_type=F32)
                    dcr_ref[j, hh:hh + 1, 0:kn] -= jnp.sum(ds, axis=0, keepdims=True)
                    row_sum = row_sum + jnp.sum(ds, axis=1, keepdims=True)
                dcq.append(row_sum)
            dq_ref[rows, :] = (dq * 0.125).astype(BF16)
            dcq_ref[rows, :] = jnp.where(lane == 2 * pair, dcq[0],
                                         jnp.where(lane == 2 * pair + 1, dcq[1], dcq_ref[rows, :]))
        for j, (k0, kn) in enumerate(blocks):
            dk_ref[k0:k0 + kn, :] = dk_acc[j].T[0:kn, :].astype(BF16)
            dv_ref[k0:k0 + kn, :] = dv_acc[j].T[0:kn, :].astype(BF16)

    qkv_spec, cumr_spec = _attn_specs(t)
    wide = -(-max(sz for _, sz in blocks) // LANES) * LANES
    act = jax.ShapeDtypeStruct((n, w), BF16)
    return pl.pallas_call(
        body, name=name, grid=(batch, n_pairs),
        out_shape=[act, act, act, jax.ShapeDtypeStruct(cumr.shape, F32), jax.ShapeDtypeStruct((n, LANES), F32)],
        in_specs=[qkv_spec(0), qkv_spec(n_pairs), qkv_spec(2 * n_pairs), qkv_spec(0), qkv_spec(0),
                  pl.BlockSpec((t, LANES), lambda b, h: (b, 0)), cumr_spec],
        out_specs=[qkv_spec(0), qkv_spec(0), qkv_spec(0), cumr_spec, pl.BlockSpec((t, LANES), lambda b, h: (b, 0))],
        scratch_shapes=[pltpu.VMEM((len(blocks), LANES, wide), F32), pltpu.VMEM((len(blocks), LANES, wide), F32)],
        compiler_params=_params(("parallel", "arbitrary")))(qkv, qkv, qkv, o, do, lse, cumr)


def _adamw(name, parts, w, m, v):
    r, c = w.shape
    tr = 128 if r % 128 == 0 else r
    tc = 256 if tr > 128 and c % 256 == 0 else c
    c1 = 1.0 - ADAM_B1 ** ADAM_STEP
    c2 = 1.0 - ADAM_B2 ** ADAM_STEP

    def body(p_ref, w_ref, m_ref, v_ref, g_ref, d_ref, m2_ref, v2_ref):
        g = p_ref[0].astype(F32)
        for s in range(1, N_DEV):
            g = g + p_ref[s].astype(F32)
        m2 = ADAM_B1 * m_ref[...] + (1.0 - ADAM_B1) * g
        v2 = ADAM_B2 * v_ref[...] + (1.0 - ADAM_B2) * (g * g)
        g_ref[...] = g
        m2_ref[...] = m2
        v2_ref[...] = v2
        d_ref[...] = -ADAM_LR * ((m2 / c1) / (jnp.sqrt(v2 / c2) + ADAM_EPS) + ADAM_WD * w_ref[...])

    blk = pl.BlockSpec((tr, tc), lambda i, j: (i, j))
    shp = jax.ShapeDtypeStruct((r, c), F32)
    return pl.pallas_call(
        body, name=name, out_shape=[shp] * 4, grid=(r // tr, c // tc),
        in_specs=[pl.BlockSpec((N_DEV, tr, tc), lambda i, j: (0, i, j)), blk, blk, blk], out_specs=[blk] * 4,
        compiler_params=_params(("parallel", "parallel")))(parts, w, m, v)


def _cat_small(vals):
    parts = []
    for name in SMALL:
        v = vals[name].reshape(1, -1).astype(F32)
        parts.append(jnp.pad(v, ((0, 0), (0, SMALL_W[name] - v.shape[1]))))
    return jnp.concatenate(parts, axis=1)


def _split_small(row, shapes):
    out, off = {}, 0
    for name in SMALL:
        out[name] = row[0, off:off + SMALL_N[name]].reshape(shapes[name])
        off += SMALL_W[name]
    return out


def _cols_from_shards(g):
    return jnp.transpose(g, (1, 0, 2)).reshape(g.shape[1], N_DEV * g.shape[2])


def _shards_from_cols(a):
    r, c = a.shape
    return jnp.transpose(a.reshape(r, N_DEV, c // N_DEV), (1, 0, 2))


def kernel(x, meta_tokens, norm_mix_gain, w_in, b_forget, w_attn_out, b_glu, conv_dw_w, conv_dw_b, conv_ln_gain, conv_ln_bias, w_conv_out, b_conv_out, w_out, norm_mlp_gain, w_mlp_up, w_mlp_down, final_norm_gain, loss_target, m_meta_tokens, m_norm_mix_gain, m_w_in, m_b_forget, m_w_attn_out, m_b_glu, m_conv_dw_w, m_conv_dw_b, m_conv_ln_gain, m_conv_ln_bias, m_w_conv_out, m_b_conv_out, m_w_out, m_norm_mlp_gain, m_w_mlp_up, m_w_mlp_down, m_final_norm_gain, v_meta_tokens, v_norm_mix_gain, v_w_in, v_b_forget, v_w_attn_out, v_b_glu, v_conv_dw_w, v_conv_dw_b, v_conv_ln_gain, v_conv_ln_bias, v_w_conv_out, v_b_conv_out, v_w_out, v_norm_mlp_gain, v_w_mlp_up, v_w_mlp_down, v_final_norm_gain):
    weights = dict(meta_tokens=meta_tokens, norm_mix_gain=norm_mix_gain, w_in=w_in, b_forget=b_forget, w_attn_out=w_attn_out, b_glu=b_glu, conv_dw_w=conv_dw_w, conv_dw_b=conv_dw_b, conv_ln_gain=conv_ln_gain, conv_ln_bias=conv_ln_bias, w_conv_out=w_conv_out, b_conv_out=b_conv_out, w_out=w_out, norm_mlp_gain=norm_mlp_gain, w_mlp_up=w_mlp_up, w_mlp_down=w_mlp_down, final_norm_gain=final_norm_gain)
    mom_m = dict(meta_tokens=m_meta_tokens, norm_mix_gain=m_norm_mix_gain, w_in=m_w_in, b_forget=m_b_forget, w_attn_out=m_w_attn_out, b_glu=m_b_glu, conv_dw_w=m_conv_dw_w, conv_dw_b=m_conv_dw_b, conv_ln_gain=m_conv_ln_gain, conv_ln_bias=m_conv_ln_bias, w_conv_out=m_w_conv_out, b_conv_out=m_b_conv_out, w_out=m_w_out, norm_mlp_gain=m_norm_mlp_gain, w_mlp_up=m_w_mlp_up, w_mlp_down=m_w_mlp_down, final_norm_gain=m_final_norm_gain)
    mom_v = dict(meta_tokens=v_meta_tokens, norm_mix_gain=v_norm_mix_gain, w_in=v_w_in, b_forget=v_b_forget, w_attn_out=v_w_attn_out, b_glu=v_b_glu, conv_dw_w=v_conv_dw_w, conv_dw_b=v_conv_dw_b, conv_ln_gain=v_conv_ln_gain, conv_ln_bias=v_conv_ln_bias, w_conv_out=v_w_conv_out, b_conv_out=v_b_conv_out, w_out=v_w_out, norm_mlp_gain=v_norm_mlp_gain, w_mlp_up=v_w_mlp_up, w_mlp_down=v_w_mlp_down, final_norm_gain=v_final_norm_gain)
    names = list(weights)
    batch, seq, d = x.shape
    t = seq + N_META
    n = batch * t
    n_pairs = d // LANES
    assert t % ROW_TILE == 0 and d == SEG

    to_rows = lambda w3: jnp.transpose(w3[0])
    w_in_t, m_in_t, v_in_t = to_rows(w_in), to_rows(m_w_in), to_rows(v_w_in)
    first = [w_in_t.astype(BF16), meta_tokens, conv_dw_w[0]]
    rest = [w_[0].astype(BF16) for w_ in (w_attn_out, w_conv_out, w_out, w_mlp_up, w_mlp_down)]
    tgt = jnp.concatenate([jnp.zeros((batch, N_META, d), F32), loss_target], axis=1).reshape(n, d)
    h0_rows = jnp.pad(x, ((0, 0), (N_META, 0), (0, 0)))
    g1 = norm_mix_gain.reshape(1, -1)
    gather_a = _exchange_start("gather_in_start", [(f_, False) for f_ in first], ks=CHIP_PEERS)
    level_1 = _exchange_wait("gather_in_wait", gather_a, [gather_a["token"], tgt, h0_rows, w_in_t, m_in_t, v_in_t] + rest)
    passed = _pass_on_start("gather_in_pass_start", level_1)
    w_in_g, meta_g, w_dw_g = _pass_on_wait("gather_in_pass_wait", passed, passed["token"])
    gather_b = _exchange_start("gather_rest_start", [(r_, False) for r_ in rest])
    n_fg = b_forget.shape[1]
    shard_w = w_in.shape[2]
    wt = w_in_g.reshape(N_DEV * shard_w, d)
    o_fg = 3 * SEG
    seg_rows = [0, SEG, 2 * SEG] + [o_fg + n_fg + i * SEG for i in range(4)]
    d_ff = w_mlp_down.shape[1] * N_DEV
    ff_blk = d_ff // N_DEV
    meta_f = _cols_from_shards(meta_g)
    w_dw = _cols_from_shards(w_dw_g)

    row2 = lambda v: v.reshape(1, -1)
    g2, g3 = row2(norm_mlp_gain), row2(final_norm_gain)
    b_fg = jnp.pad(b_forget, ((0, 0), (0, FG_PAD - n_fg)))
    h0 = lax.dynamic_update_slice(h0_rows, jnp.broadcast_to(meta_f[None], (batch, N_META, d)), (0, 0, 0)).reshape(n, d)

    hn1 = _rms_fwd("rms1", h0, g1)
    proj = lambda name, off, width, tn, dt: _mm_nt(name, [(hn1, _a_rows(d), wt, _wt_rows(tn, off))], n, width, tn, dt,
                                                   after=gather_b["token"])
    qkv = proj("proj_qkv", 0, 3 * SEG, SEG, BF16)
    glu = proj("proj_glu", seg_rows[3], 2 * SEG, SEG, BF16)
    gates = proj("proj_gates", seg_rows[5], 2 * SEG, SEG, BF16)
    fg = proj("proj_fg", o_fg, FG_PAD, FG_PAD, F32)

    cum = _fox_prep_fwd("fox_cumsum", fg, b_fg, batch)
    cumr = _key_sums_to_blocks(cum, batch, t, n_pairs)
    o, lse = _attn_fwd("attn_fwd", qkv, cumr, batch)
    rest = _exchange_wait("gather_rest_wait", gather_b, o)
    w_ao, w_co, w_o = [r_.reshape(d, d) for r_ in rest[:3]]
    w_up = rest[3]
    w_dn = rest[4].reshape(d_ff, d)
    a = _mm_nn("attn_out", o, w_ao, _w_cols(d, d, 0), d, d, BF16)

    c1 = _glu_conv_fwd("glu_conv", glu, b_glu, w_dw, conv_dw_b, batch)
    c3 = _ln_silu_fwd("ln_silu", c1, conv_ln_gain, conv_ln_bias)
    c = _mm_nn("conv_out", c3, w_co, _w_cols(d, d, 0), d, d, BF16)

    mrg = _merge_fwd("merge", gates, a, c, b_conv_out)
    mo = _mm_nn("mix_out", mrg, w_o, _w_cols(d, d, 0), d, d, BF16)
    h1, hn2 = _rms_fwd("resid_rms2", h0, g2, res=mo)
    per = ff_blk // 512
    r_up = _mm_nn("mlp_up", hn2, w_up, pl.BlockSpec((None, d, 512), lambda i, j: (j // per, 0, j % per)),
                  d_ff, 512, BF16, relu=True)
    dn = _mm_nn("mlp_down", r_up, w_dn, _w_cols(d_ff, d, 0), d, d, BF16, square_x=True, tm=ROW_TILE)
    dh2b, loss_blk, dg3 = _final("final_loss", h1, dn, tgt, g3, batch)

    dup = _mm_nt("d_mlp_down", [(dh2b, _a_rows(d), w_dn, _w_rows(d, d))], n, d_ff, d, BF16, relu_bwd_of=r_up)
    dw_dn = _grad_w_rows("gw_mlp_down", [r_up], dh2b, square_x=True)
    dhn2 = _mm_nt("d_mlp_up", [(dup, _a_rows(ff_blk, g), w_up, pl.BlockSpec((None, 512, ff_blk), lambda i, j, g=g: (g, j, 0)))
                               for g in range(N_DEV)], n, d, 512, BF16)
    dw_up = _mm_tn("gw_mlp_up", hn2, lambda a_: 0, d, dup, lambda b_: b_, ff_blk, (N_DEV, d, ff_blk),
                   pl.BlockSpec((None, d, ff_blk), lambda a_, b_: (b_, 0, 0)), (1, N_DEV))
    scatter_1 = _exchange_start("scatter_mlp_start", [(dw_dn.reshape(N_DEV, ff_blk, d), True), (dw_up, True)])
    dh1b, dg2 = _rms_bwd("rms2_bwd", dhn2, h1, g2 + scatter_1["token"][0:1, 0:1], dh2b, batch, dh_dtype=BF16)

    dm = _mm_nt("d_mix_out", [(dh1b, _a_rows(d), w_o, _w_rows(d, d))], n, d, d, BF16)
    dw_o = _grad_w("gw_mix_out", mrg, dh1b)
    da, dc, dga, dgc, dbco = _merge_bwd("merge_bwd", dm, gates, a, c, b_conv_out)

    do = _mm_nt("d_attn_out", [(da, _a_rows(d), w_ao, _w_rows(d, d))], n, d, d, BF16)
    dw_ao = _grad_w("gw_attn_out", o, da)
    dc3 = _mm_nt("d_conv_out", [(dc, _a_rows(d), w_co, _w_rows(d, d))], n, d, d, BF16)
    dw_co = _grad_w("gw_conv_out", c3, dc)

    scatter_2 = _exchange_start("scatter_mix_start", [(dw_.reshape(N_DEV, d // N_DEV, d), True)
                                                      for dw_ in (dw_o, dw_ao, dw_co)])
    dc1, dg_ln, db_ln = _ln_silu_bwd("ln_silu_bwd", dc3, c1, conv_ln_gain + scatter_2["token"][0:1, 0:1],
                                     conv_ln_bias)
    dglu_a, dglu_g, dw_dw, db_dw, dbg_a, dbg_g = _glu_conv_bwd("glu_conv_bwd", dc1, glu, b_glu, w_dw, batch)

    dq, dk, dv, dcumr, dcum_q = _attn_bwd("attn_bwd", qkv, o, do, lse, cumr, batch)
    dcum_k = jnp.pad(_key_sums_from_blocks(dcumr, batch, t, n_pairs), ((0, 0), (0, FG_PAD - 2 * n_pairs)))
    dfg, db_fg = _fox_prep_bwd("fox_cumsum_bwd", dcum_k, dcum_q, fg, b_fg, batch)

    segs = [dq, dk, dv, dglu_a, dglu_g, dga, dgc]
    gw_t = _grad_w_rows("gw_in", segs, hn1, gap=(o_fg, n_fg))
    gw_fg = _grad_w("gw_in_fg", dfg, hn1)[:n_fg]
    dw_in_t = lax.dynamic_update_slice(gw_t, gw_fg, (o_fg, 0)).reshape(N_DEV, shard_w, d)
    scatter_3 = _exchange_start("scatter_in_start", [(dw_in_t, True)])
    pairs = [(s_, _a_rows(SEG, 0, ROW_TILE), wt, _wt_block(SEG, seg_rows[i], 512), "nn") for i, s_ in enumerate(segs)]
    pairs.append((dfg, _a_rows(FG_PAD, 0, ROW_TILE), wt, _wt_block(FG_PAD, o_fg, 512), "nn"))
    dhn1 = _mm_nt("d_proj_in", pairs, n, d, 512, BF16, tm=ROW_TILE, after=scatter_3["token"])
    dh0, dg1, dmeta = _rms_bwd("rms1_bwd", dhn1, h0, g1, dh1b, batch, with_meta=True)
    grad_x = dh0.reshape(batch, t, d)[:, N_META:, :]

    small_g = dict(norm_mix_gain=dg1, b_forget=db_fg[:, :n_fg], b_glu=jnp.concatenate([dbg_a, dbg_g], axis=1),
                   conv_dw_b=db_dw, conv_ln_gain=dg_ln, conv_ln_bias=db_ln, b_conv_out=dbco, norm_mlp_gain=dg2,
                   final_norm_gain=dg3)
    scatter_4 = _exchange_start("scatter_small_start", [
        (_shards_from_cols(dmeta), True), (_shards_from_cols(dw_dw), True), (_cat_small(small_g), False),
        (loss_blk[0:1, :], False)])

    grads, deltas, new_m, new_v = {}, {}, {}, {}

    def update(k, parts):
        shp = weights[k].shape
        if k == "w_in":
            res_ = _adamw("adamw_" + k, parts, w_in_t, m_in_t, v_in_t)
            res_ = [jnp.transpose(r) for r in res_]
        else:
            w2 = lambda arr: arr.reshape(parts.shape[1:])
            res_ = _adamw("adamw_" + k, parts, w2(weights[k]), w2(mom_m[k]), w2(mom_v[k]))
        grads[k], deltas[k], new_m[k], new_v[k] = [r.reshape(shp) for r in res_]

    for k, parts in zip(("w_mlp_down", "w_mlp_up"), _exchange_wait("scatter_mlp_wait", scatter_1, scatter_4["token"])):
        update(k, parts)
    for k, parts in zip(("w_out", "w_attn_out", "w_conv_out"),
                        _exchange_wait("scatter_mix_wait", scatter_2, deltas["w_mlp_up"])):
        update(k, parts)
    update("w_in", _exchange_wait("scatter_in_wait", scatter_3, deltas["w_conv_out"])[0])
    reduced = _exchange_wait("scatter_small_wait", scatter_4, deltas["w_in"])
    loss = jnp.sum(reduced.pop()[:, 0, 0])
    for k, parts in zip(("meta_tokens", "conv_dw_w"), reduced[:-1]):
        update(k, parts)
    res = _adamw("adamw_small", reduced[-1], _cat_small(weights), _cat_small(mom_m), _cat_small(mom_v))
    shapes = {k: weights[k].shape for k in SMALL}
    for dst, r in zip((grads, deltas, new_m, new_v), res):
        dst.update(_split_small(r, shapes))

    return (loss, grad_x, *[grads[k] for k in names], *[deltas[k] for k in names],
            *[new_m[k] for k in names], *[new_v[k] for k in names])
```

```python
import functools

import jax
import jax.numpy as jnp
from jax import lax
from jax.experimental import pallas as pl
from jax.experimental.pallas import tpu as pltpu

F32, BF16 = jnp.float32, jnp.bfloat16
N_DEV = 8
N_META = 16
HEAD_DIM = 64
LANES = 128
CONV_W = 31
RMS_EPS = 1e-6
LN_EPS = 1e-5
ROW_TILE = 688
MM_TM = 2 * ROW_TILE
SEG = 1024
FG_PAD = 128
VMEM_LIMIT = 56 * 1024 * 1024
ADAM_LR, ADAM_B1, ADAM_B2, ADAM_EPS, ADAM_WD, ADAM_STEP = 0.001, 0.9, 0.999, 1e-08, 0.01, 10
NEG = -1e30
LOG2E = 1.4426950408889634

SMALL = ("norm_mix_gain", "b_forget", "b_glu", "conv_dw_b", "conv_ln_gain", "conv_ln_bias", "b_conv_out",
         "norm_mlp_gain", "final_norm_gain")
SMALL_W = {"norm_mix_gain": 1024, "b_forget": 128, "b_glu": 2048, "conv_dw_b": 1024, "conv_ln_gain": 1024,
           "conv_ln_bias": 1024, "b_conv_out": 1024, "norm_mlp_gain": 1024, "final_norm_gain": 1024}
SMALL_N = {"norm_mix_gain": 1024, "b_forget": 16, "b_glu": 2048, "conv_dw_b": 1024, "conv_ln_gain": 1024,
           "conv_ln_bias": 1024, "b_conv_out": 1024, "norm_mlp_gain": 1024, "final_norm_gain": 1024}


def _params(sem=None):
    return pltpu.CompilerParams(dimension_semantics=sem, vmem_limit_bytes=VMEM_LIMIT)


def _sigmoid(x):
    return 1.0 / (1.0 + jnp.exp(-x))


def _dot_nt(a, b):
    return lax.dot_general(a, b, (((1,), (1,)), ((), ())), preferred_element_type=F32)


def _dot_tn(a, b):
    return lax.dot_general(a, b, (((0,), (0,)), ((), ())), preferred_element_type=F32)


HBM_SPEC = pl.BlockSpec(memory_space=pltpu.HBM)
SEM_SPEC = pl.BlockSpec(memory_space=pltpu.SEMAPHORE)
DATAFLOW = pltpu.SideEffectType.DATAFLOW_SIDE_EFFECTING


def _device_index():
    return 4 * lax.axis_index("x") + 2 * lax.axis_index("y") + lax.axis_index("c")


def _peers():
    x, y, c = lax.axis_index("x"), lax.axis_index("y"), lax.axis_index("c")
    out = []
    for k in range(1, N_DEV):
        px = 1 - x if k & 4 else x
        py = 1 - y if k & 2 else y
        pc = 1 - c if k & 1 else c
        out.append((k, (px, py, pc), 4 * px + 2 * py + pc))
    return out


def _peer_copy(per_dest, src_ref, land_ref, send_sems, recv_sems, a, k, dev, peer):
    src = src_ref.at[peer] if per_dest else src_ref
    return pltpu.make_async_remote_copy(
        src_ref=src, dst_ref=land_ref.at[_device_index()], send_sem=send_sems.at[a * (N_DEV - 1) + k - 1],
        recv_sem=recv_sems.at[a * (N_DEV - 1) + k - 1], device_id=dev, device_id_type=pl.DeviceIdType.MESH)


ALL_PEERS = tuple(range(1, N_DEV))
CHIP_PEERS = (1, 2, 4, 6)
FAR_PEERS = (2, 4, 6)


def _own_copy(per_dest, src_ref, land_ref, send_sems, n, a):
    me = _device_index()
    return pltpu.make_async_copy(src_ref.at[me] if per_dest else src_ref, land_ref.at[me],
                                 send_sems.at[n * (N_DEV - 1) + a])


def _exchange_start(name, items, ks=ALL_PEERS):
    n = len(items)
    per_dest = [it[1] for it in items]

    def body(*refs):
        srcs, lands = refs[:n], refs[n:2 * n]
        send_sems, recv_sems, token = refs[2 * n], refs[2 * n + 1], refs[-1]
        for a in range(n):
            _own_copy(per_dest[a], srcs[a], lands[a], send_sems, n, a).start()
            for k, dev, peer in _peers():
                if k in ks:
                    _peer_copy(per_dest[a], srcs[a], lands[a], send_sems, recv_sems, a, k, dev, peer).start()
        token[...] = jnp.zeros(token.shape, F32)

    srcs = [pltpu.with_memory_space_constraint(it[0], pltpu.HBM) for it in items]
    lands = []
    for arr, pd in items:
        shp = arr.shape if pd else (N_DEV,) + arr.shape
        lands.append(pltpu.with_memory_space_constraint(lax.empty(shp, arr.dtype), pltpu.HBM))
    sems = pltpu.SemaphoreType.DMA((n * N_DEV,))
    res = pl.pallas_call(
        body, name=name,
        out_shape=(sems, sems, *[pltpu.HBM(a_.shape, a_.dtype) for a_ in srcs + lands],
                   jax.ShapeDtypeStruct((8, 128), F32)),
        in_specs=[HBM_SPEC] * (2 * n),
        out_specs=(SEM_SPEC, SEM_SPEC, *[HBM_SPEC] * (2 * n), pl.BlockSpec(memory_space=pltpu.VMEM)),
        input_output_aliases={i: 2 + i for i in range(2 * n)},
        compiler_params=pltpu.CompilerParams(has_side_effects=DATAFLOW),
    )(*srcs, *lands)
    return dict(per_dest=per_dest, ks=ks, send=res[0], recv=res[1], srcs=list(res[2:2 + n]),
                lands=list(res[2 + n:2 + 2 * n]), token=res[-1])


def _exchange_wait(name, started, after):
    per_dest = started["per_dest"]
    n = len(per_dest)

    def body(*refs):
        srcs, lands = refs[:n], refs[n:2 * n]
        send_sems, recv_sems = refs[2 * n], refs[2 * n + 1]
        for a in range(n):
            _own_copy(per_dest[a], srcs[a], lands[a], send_sems, n, a).wait()
            for k, dev, peer in _peers():
                if k in started["ks"]:
                    cp = _peer_copy(per_dest[a], srcs[a], lands[a], send_sems, recv_sems, a, k, dev, peer)
                    cp.wait_send()
                    cp.wait_recv()

    bufs = started["srcs"] + started["lands"]
    after = list(after) if isinstance(after, (list, tuple)) else [after]
    res = pl.pallas_call(
        body, name=name, out_shape=tuple(pltpu.HBM(b_.shape, b_.dtype) for b_ in bufs),
        in_specs=[HBM_SPEC] * (2 * n) + [SEM_SPEC, SEM_SPEC] + [pl.BlockSpec(memory_space=pl.ANY)] * len(after),
        out_specs=tuple([HBM_SPEC] * (2 * n)), input_output_aliases={i: i for i in range(2 * n)},
        compiler_params=pltpu.CompilerParams(has_side_effects=DATAFLOW),
    )(*bufs, started["send"], started["recv"], *after)
    return list(res[n:])


def _pass_on_copy(land_ref, send_sems, recv_sems, a, idx, slot):
    sibling = (lax.axis_index("x"), lax.axis_index("y"), 1 - lax.axis_index("c"))
    return pltpu.make_async_remote_copy(
        src_ref=land_ref.at[slot], dst_ref=land_ref.at[slot], send_sem=send_sems.at[a * len(FAR_PEERS) + idx],
        recv_sem=recv_sems.at[a * len(FAR_PEERS) + idx], device_id=sibling, device_id_type=pl.DeviceIdType.MESH)


def _pass_on_start(name, lands):
    n = len(lands)

    def body(*refs):
        send_sems, recv_sems, token = refs[n], refs[n + 1], refs[-1]
        slots = {k: peer for k, _, peer in _peers()}
        for a in range(n):
            for idx, k in enumerate(FAR_PEERS):
                _pass_on_copy(refs[a], send_sems, recv_sems, a, idx, slots[k]).start()
        token[...] = jnp.zeros(token.shape, F32)

    lands = [pltpu.with_memory_space_constraint(l_, pltpu.HBM) for l_ in lands]
    sems = pltpu.SemaphoreType.DMA((n * len(FAR_PEERS),))
    res = pl.pallas_call(
        body, name=name,
        out_shape=(sems, sems, *[pltpu.HBM(l_.shape, l_.dtype) for l_ in lands], jax.ShapeDtypeStruct((8, 128), F32)),
        in_specs=[HBM_SPEC] * n, out_specs=(SEM_SPEC, SEM_SPEC, *[HBM_SPEC] * n, pl.BlockSpec(memory_space=pltpu.VMEM)),
        input_output_aliases={i: 2 + i for i in range(n)},
        compiler_params=pltpu.CompilerParams(has_side_effects=DATAFLOW),
    )(*lands)
    return dict(send=res[0], recv=res[1], lands=list(res[2:2 + n]), token=res[-1])


def _pass_on_wait(name, passed, after):
    n = len(passed["lands"])

    def body(*refs):
        send_sems, recv_sems = refs[n], refs[n + 1]
        slots = {k: peer for k, _, peer in _peers()}
        for a in range(n):
            for idx, k in enumerate(FAR_PEERS):
                _pass_on_copy(refs[a], send_sems, recv_sems, a, idx, slots[k]).wait_send()
                _pass_on_copy(refs[a], send_sems, recv_sems, a, idx, slots[k ^ 1]).wait_recv()

    res = pl.pallas_call(
        body, name=name, out_shape=tuple(pltpu.HBM(l_.shape, l_.dtype) for l_ in passed["lands"]),
        in_specs=[HBM_SPEC] * n + [SEM_SPEC, SEM_SPEC, pl.BlockSpec(memory_space=pl.ANY)],
        out_specs=tuple([HBM_SPEC] * n), input_output_aliases={i: i for i in range(n)},
        compiler_params=pltpu.CompilerParams(has_side_effects=DATAFLOW),
    )(*passed["lands"], passed["send"], passed["recv"], after)
    return list(res)


def _mm_nn(name, x, w, w_spec, n_out, tn, out_dtype, relu=False, square_x=False, tm=MM_TM):
    m, k = x.shape

    def body(x_ref, w_ref, *outs):
        xv = x_ref[...]
        if square_x:
            xv = xv * xv
        acc = jnp.dot(xv, w_ref[...], preferred_element_type=F32)
        if relu:
            acc = jnp.maximum(acc, 0.0)
        outs[0][...] = acc.astype(outs[0].dtype)

    o_spec = pl.BlockSpec((tm, tn), lambda i, j: (i, j))
    return pl.pallas_call(
        body, name=name, out_shape=jax.ShapeDtypeStruct((m, n_out), out_dtype), grid=(m // tm, n_out // tn),
        in_specs=[pl.BlockSpec((tm, k), lambda i, j: (i, 0)), w_spec],
        out_specs=o_spec, compiler_params=_params(("parallel", "parallel")),
    )(x, w)


def _mm_nt(name, pairs, m, n_out, tn, out_dtype, relu_bwd_of=None, tm=MM_TM, after=None):
    np_ = len(pairs)

    def body(*refs):
        acc = None
        for p in range(np_):
            if len(pairs[p]) == 5:
                d = jnp.dot(refs[2 * p][...], refs[2 * p + 1][...], preferred_element_type=F32)
            else:
                d = _dot_nt(refs[2 * p][...], refs[2 * p + 1][...])
            acc = d if acc is None else acc + d
        if relu_bwd_of is not None:
            acc = acc * (2.0 * refs[2 * np_][...].astype(F32))
        refs[-1][...] = acc.astype(refs[-1].dtype)

    o_spec = pl.BlockSpec((tm, tn), lambda i, j: (i, j))
    operands, specs = [], []
    for pair in pairs:
        operands += [pair[0], pair[2]]
        specs += [pair[1], pair[3]]
    if relu_bwd_of is not None:
        operands.append(relu_bwd_of)
        specs.append(o_spec)
    if after is not None:
        operands.append(after)
        specs.append(pl.BlockSpec((8, 128), lambda i, j: (0, 0)))
    return pl.pallas_call(
        body, name=name, out_shape=jax.ShapeDtypeStruct((m, n_out), out_dtype), grid=(m // tm, n_out // tn),
        in_specs=specs, out_specs=o_spec, compiler_params=_params(("parallel", "parallel")),
    )(*operands)


def _mm_tn(name, x, x_col, ta, dy, dy_col, tb, out_shape, out_spec, grid_ab):
    m = x.shape[0]

    def body(x_ref, dy_ref, o_ref):
        o_ref[...] = _dot_tn(x_ref[...], dy_ref[...]).astype(BF16)

    return pl.pallas_call(
        body, name=name, out_shape=jax.ShapeDtypeStruct(out_shape, BF16), grid=grid_ab,
        in_specs=[pl.BlockSpec((m, ta), lambda a, b: (0, x_col(a))),
                  pl.BlockSpec((m, tb), lambda a, b: (0, dy_col(b)))],
        out_specs=out_spec, compiler_params=_params(("parallel", "parallel")),
    )(x, dy)


def _w_cols(k, tn, off_blocks):
    return pl.BlockSpec((k, tn), lambda i, j: (0, off_blocks + j))


def _a_rows(kw, col_block=0, tm=MM_TM):
    return pl.BlockSpec((tm, kw), lambda i, j: (i, col_block))


def _w_rows(tn, kw, col_block=0):
    return pl.BlockSpec((tn, kw), lambda i, j: (j, col_block))


def _wt_rows(tn, off):
    return pl.BlockSpec((pl.Element(tn), pl.Element(SEG)), lambda i, j: (pl.multiple_of(off + tn * j, 16), 0))


def _wt_block(k, off, tn):
    return pl.BlockSpec((pl.Element(k), pl.Element(tn)), lambda i, j: (off, pl.multiple_of(tn * j, 128)))


GW_TILE = 512


def _grad_w_rows(name, xs, dy, square_x=False, gap=None):
    m, nb = dy.shape
    per = [x_.shape[1] // GW_TILE for x_ in xs]
    steps = sum(per)
    tiles = [(k, h) for k in range(len(xs)) for h in range(per[k])]
    gap_tile, gap_size = (gap[0] // GW_TILE, gap[1]) if gap else (steps, 0)

    def out_row(i):
        return pl.multiple_of(i * GW_TILE + jnp.where(i >= gap_tile, gap_size, 0), 16), 0

    def body(*refs):
        x_refs, dy_ref, o_ref, buf, sems = refs[:len(xs)], refs[len(xs)], refs[len(xs) + 1], refs[-2], refs[-1]
        s_ = pl.program_id(0)

        def fetch(step, slot):
            for idx, (k, h) in enumerate(tiles):
                @pl.when(step == idx)
                def _():
                    pltpu.make_async_copy(x_refs[k].at[:, pl.ds(h * GW_TILE, GW_TILE)], buf.at[slot], sems.at[slot]).start()

        @pl.when(s_ == 0)
        def _():
            fetch(s_, 0)

        @pl.when(s_ + 1 < steps)
        def _():
            fetch(s_ + 1, (s_ + 1) % 2)

        slot = s_ % 2
        pltpu.make_async_copy(x_refs[0].at[:, pl.ds(0, GW_TILE)], buf.at[slot], sems.at[slot]).wait()
        xv = buf[slot]
        if square_x:
            xv = xv * xv
        o_ref[...] = _dot_tn(xv, dy_ref[...]).astype(BF16)

    return pl.pallas_call(
        body, name=name, out_shape=jax.ShapeDtypeStruct((steps * GW_TILE + gap_size, nb), BF16), grid=(steps,),
        in_specs=[pl.BlockSpec(memory_space=pl.ANY)] * len(xs) + [pl.BlockSpec((m, nb), lambda i: (0, 0))],
        out_specs=pl.BlockSpec((pl.Element(GW_TILE), pl.Element(nb)), out_row),
        scratch_shapes=[pltpu.VMEM((2, m, GW_TILE), BF16), pltpu.SemaphoreType.DMA((2,))],
        compiler_params=_params(("arbitrary",)))(*xs, dy)


def _grad_w(name, x, dy):
    na, nb = x.shape[1], dy.shape[1]
    ta, tb = min(na, 1024), min(nb, 512)
    return _mm_tn(name, x, lambda a: a, ta, dy, lambda b: b, tb, (na, nb),
                  pl.BlockSpec((ta, tb), lambda a, b: (a, b)), (na // ta, nb // tb))


def _row_spec(width):
    return pl.BlockSpec((ROW_TILE, width), lambda i: (i, 0))


def _vec_spec(width):
    return pl.BlockSpec((1, width), lambda i: (0, 0))


def _rms_fwd(name, h, g, res=None):
    n, d = h.shape
    tile = ROW_TILE if n % ROW_TILE == 0 else n
    row_spec = pl.BlockSpec((tile, d), lambda i: (i, 0))

    def body(*refs):
        if res is None:
            h_ref, g_ref, hn_ref = refs
            hv = h_ref[...]
        else:
            h_ref, r_ref, g_ref, hs_ref, hn_ref = refs
            hv = h_ref[...] + r_ref[...].astype(F32)
            hs_ref[...] = hv
        r = lax.rsqrt(jnp.mean(hv * hv, axis=-1, keepdims=True) + RMS_EPS)
        hn_ref[...] = (hv * r * g_ref[...]).astype(BF16)

    ins = [h, g] if res is None else [h, res, g]
    in_specs = [row_spec, _vec_spec(d)] if res is None else [row_spec, row_spec, _vec_spec(d)]
    hn_shape = jax.ShapeDtypeStruct((n, d), BF16)
    if res is None:
        out_shape, out_specs = hn_shape, row_spec
    else:
        out_shape, out_specs = [jax.ShapeDtypeStruct((n, d), F32), hn_shape], [row_spec, row_spec]
    return pl.pallas_call(body, name=name, out_shape=out_shape, grid=(n // tile,), in_specs=in_specs,
                          out_specs=out_specs, compiler_params=_params(("parallel",)))(*ins)


def _rms_bwd(name, dhn, h, g, dres, batch, dh_dtype=F32, with_meta=False):
    n, d = h.shape
    t = n // batch
    nt = t // ROW_TILE

    def body(dhn_ref, h_ref, g_ref, dres_ref, *outs):
        first = (pl.program_id(0) == 0) & (pl.program_id(1) == 0)
        hv = h_ref[...]
        r = lax.rsqrt(jnp.mean(hv * hv, axis=-1, keepdims=True) + RMS_EPS)
        nrm = hv * r
        dhn = dhn_ref[...].astype(F32)
        dn = dhn * g_ref[...]
        dh = dres_ref[...].astype(F32) + r * (dn - nrm * jnp.mean(dn * nrm, axis=-1, keepdims=True))
        outs[0][...] = dh.astype(dh_dtype)
        dg_ref = outs[1]

        @pl.when(first)
        def _():
            dg_ref[...] = jnp.zeros(dg_ref.shape, F32)

        dg_ref[...] += jnp.sum(dhn * nrm, axis=0, keepdims=True)
        if with_meta:
            meta_ref = outs[2]

            @pl.when(first)
            def _():
                meta_ref[...] = jnp.zeros(meta_ref.shape, F32)

            @pl.when(pl.program_id(1) == 0)
            def _():
                meta_ref[...] += dh[0:N_META, :]

    row = pl.BlockSpec((ROW_TILE, d), lambda b, j: (b * nt + j, 0))
    vec = pl.BlockSpec((1, d), lambda b, j: (0, 0))
    shapes = [jax.ShapeDtypeStruct((n, d), dh_dtype), jax.ShapeDtypeStruct((1, d), F32)]
    specs = [row, vec]
    if with_meta:
        shapes.append(jax.ShapeDtypeStruct((N_META, d), F32))
        specs.append(pl.BlockSpec((N_META, d), lambda b, j: (0, 0)))
    return pl.pallas_call(body, name=name, out_shape=shapes, grid=(batch, nt), in_specs=[row, row, vec, row],
                          out_specs=specs, compiler_params=_params(("arbitrary", "arbitrary")))(dhn, h, g, dres)


def _final(name, h1, dn, tgt, g, batch):
    n, d = h1.shape
    t = n // batch
    nt = t // ROW_TILE

    def body(h1_ref, dn_ref, tgt_ref, g_ref, dhb_ref, loss_ref, dg_ref):
        first = (pl.program_id(0) == 0) & (pl.program_id(1) == 0)
        hv = h1_ref[...] + dn_ref[...].astype(F32)
        r = lax.rsqrt(jnp.mean(hv * hv, axis=-1, keepdims=True) + RMS_EPS)
        nrm = hv * r
        gv = g_ref[...]
        pos = pl.program_id(1) * ROW_TILE + lax.broadcasted_iota(jnp.int32, (ROW_TILE, 1), 0)
        diff = jnp.where(pos >= N_META, nrm * gv - tgt_ref[...], 0.0)
        dy = diff * (1.0 / d)

        @pl.when(first)
        def _():
            loss_ref[...] = jnp.zeros(loss_ref.shape, F32)
            dg_ref[...] = jnp.zeros(dg_ref.shape, F32)

        loss_ref[...] += jnp.full(loss_ref.shape, 0.5 / d, F32) * jnp.sum(diff * diff)
        dg_ref[...] += jnp.sum(dy * nrm, axis=0, keepdims=True)
        dng = dy * gv
        dh = r * (dng - nrm * jnp.mean(dng * nrm, axis=-1, keepdims=True))
        dhb_ref[...] = dh.astype(BF16)

    row = pl.BlockSpec((ROW_TILE, d), lambda b, j: (b * nt + j, 0))
    vec = pl.BlockSpec((1, d), lambda b, j: (0, 0))
    return pl.pallas_call(
        body, name=name, grid=(batch, nt), in_specs=[row, row, row, vec],
        out_shape=[jax.ShapeDtypeStruct((n, d), BF16), jax.ShapeDtypeStruct((8, 128), F32),
                   jax.ShapeDtypeStruct((1, d), F32)],
        out_specs=[row, pl.BlockSpec((8, 128), lambda b, j: (0, 0)), vec],
        compiler_params=_params(("arbitrary", "arbitrary")))(h1, dn, tgt, g)


def _ln_silu_fwd(name, c1, g, b):
    n, d = c1.shape

    def body(c_ref, g_ref, b_ref, o_ref):
        xv = c_ref[...]
        xc = xv - jnp.mean(xv, axis=-1, keepdims=True)
        rstd = lax.rsqrt(jnp.mean(xc * xc, axis=-1, keepdims=True) + LN_EPS)
        c2 = xc * rstd * g_ref[...] + b_ref[...]
        o_ref[...] = (c2 * _sigmoid(c2)).astype(BF16)

    return pl.pallas_call(body, name=name, out_shape=jax.ShapeDtypeStruct((n, d), BF16), grid=(n // ROW_TILE,),
                          in_specs=[_row_spec(d), _vec_spec(d), _vec_spec(d)], out_specs=_row_spec(d),
                          compiler_params=_params(("parallel",)))(c1, g, b)


def _ln_silu_bwd(name, dc3, c1, g, b):
    n, d = c1.shape

    def body(d_ref, c_ref, g_ref, b_ref, dc1_ref, dg_ref, db_ref):
        xv = c_ref[...]
        xc = xv - jnp.mean(xv, axis=-1, keepdims=True)
        rstd = lax.rsqrt(jnp.mean(xc * xc, axis=-1, keepdims=True) + LN_EPS)
        xh = xc * rstd
        c2 = xh * g_ref[...] + b_ref[...]
        s = _sigmoid(c2)
        dc2 = d_ref[...].astype(F32) * (s * (1.0 + c2 * (1.0 - s)))

        @pl.when(pl.program_id(0) == 0)
        def _():
            dg_ref[...] = jnp.zeros(dg_ref.shape, F32)
            db_ref[...] = jnp.zeros(db_ref.shape, F32)

        dg_ref[...] += jnp.sum(dc2 * xh, axis=0, keepdims=True)
        db_ref[...] += jnp.sum(dc2, axis=0, keepdims=True)
        dxh = dc2 * g_ref[...]
        dc1_ref[...] = rstd * (dxh - jnp.mean(dxh, axis=-1, keepdims=True)
                               - xh * jnp.mean(dxh * xh, axis=-1, keepdims=True))

    return pl.pallas_call(
        body, name=name, grid=(n // ROW_TILE,),
        out_shape=[jax.ShapeDtypeStruct((n, d), F32), jax.ShapeDtypeStruct((1, d), F32),
                   jax.ShapeDtypeStruct((1, d), F32)],
        in_specs=[_row_spec(d), _row_spec(d), _vec_spec(d), _vec_spec(d)],
        out_specs=[_row_spec(d), _vec_spec(d), _vec_spec(d)],
        compiler_params=_params(("arbitrary",)))(dc3, c1, g, b)


MERGE_TC = 512


def _merge_fwd(name, gates, a, c, b_co):
    n, d = a.shape
    nc = d // MERGE_TC

    def body(ga_ref, gc_ref, a_ref, c_ref, b_ref, m_ref):
        f32 = lambda r_: r_[...].astype(F32)
        m = _sigmoid(f32(ga_ref)) * f32(a_ref) + _sigmoid(f32(gc_ref)) * (f32(c_ref) + b_ref[...])
        m_ref[...] = m.astype(BF16)

    blk = lambda off: pl.BlockSpec((ROW_TILE, MERGE_TC), lambda i, j: (i, off + j))
    return pl.pallas_call(
        body, name=name, out_shape=jax.ShapeDtypeStruct((n, d), BF16), grid=(n // ROW_TILE, nc),
        in_specs=[blk(0), blk(nc), blk(0), blk(0), pl.BlockSpec((1, MERGE_TC), lambda i, j: (0, j))],
        out_specs=blk(0), compiler_params=_params(("parallel", "parallel")))(gates, gates, a, c, b_co)


def _merge_bwd(name, dm, gates, a, c, b_co):
    n, d = a.shape
    nc = d // MERGE_TC

    def body(dm_ref, ga_ref, gc_ref, a_ref, c_ref, b_ref, da_ref, dc_ref, dga_ref, dgc_ref, dbco_ref):
        f32 = lambda r_: r_[...].astype(F32)
        dmv = f32(dm_ref)
        sa, sc = _sigmoid(f32(ga_ref)), _sigmoid(f32(gc_ref))
        dc = dmv * sc
        da_ref[...] = (dmv * sa).astype(BF16)
        dc_ref[...] = dc.astype(BF16)
        dga_ref[...] = (dmv * f32(a_ref) * sa * (1.0 - sa)).astype(BF16)
        dgc_ref[...] = (dmv * (f32(c_ref) + b_ref[...]) * sc * (1.0 - sc)).astype(BF16)

        @pl.when(pl.program_id(1) == 0)
        def _():
            dbco_ref[...] = jnp.zeros(dbco_ref.shape, F32)

        dbco_ref[...] += jnp.sum(dc, axis=0, keepdims=True)

    blk = lambda off: pl.BlockSpec((ROW_TILE, MERGE_TC), lambda j, i: (i, off + j))
    vec = pl.BlockSpec((1, MERGE_TC), lambda j, i: (0, j))
    act = jax.ShapeDtypeStruct((n, d), BF16)
    return pl.pallas_call(
        body, name=name, grid=(nc, n // ROW_TILE),
        out_shape=[act, act, act, act, jax.ShapeDtypeStruct((1, d), F32)],
        in_specs=[blk(0), blk(0), blk(nc), blk(0), blk(0), vec],
        out_specs=[blk(0), blk(0), blk(0), blk(0), vec],
        compiler_params=_params(("parallel", "arbitrary")))(dm, gates, gates, a, c, b_co)


CONV_TC = 128
CONV_HALO = 32


def _conv_chunk(t):
    return 48 if t % 48 == 0 else 32 if t % 32 == 0 else 16


def _fold8(x):
    out = x[0:8]
    for k in range(1, x.shape[0] // 8):
        out = out + x[8 * k:8 * k + 8]
    return out


def _glu_conv_fwd(name, glu, b_glu, w_dw, b_dw, batch):
    n, c2 = glu.shape
    c = c2 // 2
    t = n // batch
    nc = c // CONV_TC

    def body(a_ref, gt_ref, ba_ref, bg_ref, w_ref, bdw_ref, o_ref, pad_ref):
        u = (a_ref[...].astype(F32) + ba_ref[...]) * _sigmoid(gt_ref[...].astype(F32) + bg_ref[...])
        pad_ref[0:CONV_HALO, :] = jnp.zeros((CONV_HALO, CONV_TC), F32)
        pad_ref[CONV_HALO:CONV_HALO + t, :] = u
        ch = _conv_chunk(t)
        for r0 in range(0, t, ch):
            acc = jnp.zeros((ch, CONV_TC), F32) + bdw_ref[...]
            for j in range(CONV_W):
                off = r0 + CONV_HALO - (CONV_W - 1) + j
                acc = acc + w_ref[j:j + 1, :] * pad_ref[off:off + ch, :]
            o_ref[r0:r0 + ch, :] = acc

    seq = lambda off: pl.BlockSpec((t, CONV_TC), lambda b, j: (b, off + j))
    vec = lambda off: pl.BlockSpec((1, CONV_TC), lambda b, j: (0, off + j))
    return pl.pallas_call(
        body, name=name, out_shape=jax.ShapeDtypeStruct((n, c), F32), grid=(batch, nc),
        in_specs=[seq(0), seq(nc), vec(0), vec(nc), pl.BlockSpec((CONV_W, CONV_TC), lambda b, j: (0, j)), vec(0)],
        out_specs=seq(0), scratch_shapes=[pltpu.VMEM((t + CONV_HALO, CONV_TC), F32)],
        compiler_params=_params(("parallel", "parallel")))(glu, glu, b_glu, b_glu, w_dw, b_dw)


def _glu_conv_bwd(name, dc1, glu, b_glu, w_dw, batch):
    n, c2 = glu.shape
    c = c2 // 2
    t = n // batch
    nc = c // CONV_TC

    def body(d_ref, a_ref, gt_ref, ba_ref, bg_ref, w_ref, dga_ref, dgg_ref, dw_ref, dbdw_ref, dba_ref, dbg_ref,
             padu_ref, padd_ref):
        av = a_ref[...].astype(F32) + ba_ref[...]
        sg = _sigmoid(gt_ref[...].astype(F32) + bg_ref[...])
        dc = d_ref[...]
        padu_ref[0:CONV_HALO, :] = jnp.zeros((CONV_HALO, CONV_TC), F32)
        padu_ref[CONV_HALO:CONV_HALO + t, :] = av * sg
        padd_ref[0:t, :] = dc
        padd_ref[t:t + CONV_HALO, :] = jnp.zeros((CONV_HALO, CONV_TC), F32)

        @pl.when(pl.program_id(1) == 0)
        def _():
            dw_ref[...] = jnp.zeros(dw_ref.shape, F32)
            dbdw_ref[...] = jnp.zeros(dbdw_ref.shape, F32)
            dba_ref[...] = jnp.zeros(dba_ref.shape, F32)
            dbg_ref[...] = jnp.zeros(dbg_ref.shape, F32)

        ch = _conv_chunk(t)
        zero8 = jnp.zeros((8, CONV_TC), F32)
        dw_acc = [zero8] * CONV_W
        sum_dc, sum_a, sum_g = zero8, zero8, zero8
        for r0 in range(0, t, ch):
            dcc = d_ref[r0:r0 + ch, :]
            du = jnp.zeros((ch, CONV_TC), F32)
            for j in range(CONV_W):
                back = r0 + CONV_W - 1 - j
                du = du + w_ref[j:j + 1, :] * padd_ref[back:back + ch, :]
                off = r0 + CONV_HALO - (CONV_W - 1) + j
                dw_acc[j] = dw_acc[j] + _fold8(dcc * padu_ref[off:off + ch, :])
            sgc = _sigmoid(gt_ref[r0:r0 + ch, :].astype(F32) + bg_ref[...])
            dga = du * sgc
            dgg = du * padu_ref[CONV_HALO + r0:CONV_HALO + r0 + ch, :] * (1.0 - sgc)
            dga_ref[r0:r0 + ch, :] = dga.astype(BF16)
            dgg_ref[r0:r0 + ch, :] = dgg.astype(BF16)
            sum_dc, sum_a, sum_g = sum_dc + _fold8(dcc), sum_a + _fold8(dga), sum_g + _fold8(dgg)
        for j in range(CONV_W):
            dw_ref[j:j + 1, :] += jnp.sum(dw_acc[j], axis=0, keepdims=True)
        dbdw_ref[...] += jnp.sum(sum_dc, axis=0, keepdims=True)
        dba_ref[...] += jnp.sum(sum_a, axis=0, keepdims=True)
        dbg_ref[...] += jnp.sum(sum_g, axis=0, keepdims=True)

    seq = lambda off: pl.BlockSpec((t, CONV_TC), lambda j, b: (b, off + j))
    vec = lambda off: pl.BlockSpec((1, CONV_TC), lambda j, b: (0, off + j))
    wsp = pl.BlockSpec((CONV_W, CONV_TC), lambda j, b: (0, j))
    act = jax.ShapeDtypeStruct((n, c), BF16)
    v = jax.ShapeDtypeStruct((1, c), F32)
    return pl.pallas_call(
        body, name=name, grid=(nc, batch),
        out_shape=[act, act, jax.ShapeDtypeStruct((CONV_W, c), F32), v, v, v],
        in_specs=[seq(0), seq(0), seq(nc), vec(0), vec(nc), wsp],
        out_specs=[seq(0), seq(0), wsp, vec(0), vec(0), vec(0)],
        scratch_shapes=[pltpu.VMEM((t + CONV_HALO, CONV_TC), F32), pltpu.VMEM((t + CONV_HALO, CONV_TC), F32)],
        compiler_params=_params(("parallel", "arbitrary")))(dc1, glu, glu, b_glu, b_glu, w_dw)


def _split3(x):
    hi = x.astype(BF16)
    r = x - hi.astype(F32)
    mid = r.astype(BF16)
    lo = (r - mid.astype(F32)).astype(BF16)
    return hi, mid, lo


def _tri_matmul(tri, x):
    hi, mid, lo = _split3(x)
    dot = lambda v: jnp.dot(tri, v, preferred_element_type=F32)
    return dot(hi) + dot(mid) + dot(lo)


def _fox_prep_fwd(name, fg, b_fg, batch):
    n, w = fg.shape
    t = n // batch
    nq = t // ROW_TILE

    def body(fg_ref, b_ref, cum_ref):
        row = lax.broadcasted_iota(jnp.int32, (ROW_TILE, ROW_TILE), 0)
        col = lax.broadcasted_iota(jnp.int32, (ROW_TILE, ROW_TILE), 1)
        tri = (row >= col).astype(BF16)
        for k in range(nq):
            rows = slice(k * ROW_TILE, (k + 1) * ROW_TILE)
            z = fg_ref[rows, :] + b_ref[...]
            logf = jnp.minimum(z, 0.0) - jnp.log(1.0 + jnp.exp(-jnp.abs(z)))
            cum = _tri_matmul(tri, logf)
            if k > 0:
                cum = cum + cum_ref[k * ROW_TILE - 1:k * ROW_TILE, :]
            cum_ref[rows, :] = cum

    seq = pl.BlockSpec((t, w), lambda b: (b, 0))
    return pl.pallas_call(body, name=name, out_shape=jax.ShapeDtypeStruct((n, w), F32), grid=(batch,),
                          in_specs=[seq, pl.BlockSpec((1, w), lambda b: (0, 0))], out_specs=seq,
                          compiler_params=_params(("parallel",)))(fg, b_fg)


def _fox_prep_bwd(name, dcum_k, dcum_q, fg, b_fg, batch):
    n, w = fg.shape
    t = n // batch
    nq = t // ROW_TILE

    def body(dk_ref, dq_ref, fg_ref, b_ref, dfg_ref, db_ref, rev_ref):
        row = lax.broadcasted_iota(jnp.int32, (ROW_TILE, ROW_TILE), 0)
        col = lax.broadcasted_iota(jnp.int32, (ROW_TILE, ROW_TILE), 1)
        tri = (col >= row).astype(BF16)

        @pl.when(pl.program_id(0) == 0)
        def _():
            db_ref[...] = jnp.zeros(db_ref.shape, F32)

        for k in reversed(range(nq)):
            rows = slice(k * ROW_TILE, (k + 1) * ROW_TILE)
            dlog = _tri_matmul(tri, dk_ref[rows, :] + dq_ref[rows, :])
            if k < nq - 1:
                dlog = dlog + rev_ref[(k + 1) * ROW_TILE:(k + 1) * ROW_TILE + 1, :]
            rev_ref[rows, :] = dlog
            dfg = dlog * _sigmoid(-(fg_ref[rows, :] + b_ref[...]))
            dfg_ref[rows, :] = dfg.astype(BF16)
            db_ref[...] += jnp.sum(dfg, axis=0, keepdims=True)

    seq = pl.BlockSpec((t, w), lambda b: (b, 0))
    vec = pl.BlockSpec((1, w), lambda b: (0, 0))
    return pl.pallas_call(
        body, name=name, grid=(batch,),
        out_shape=[jax.ShapeDtypeStruct((n, w), BF16), jax.ShapeDtypeStruct((1, w), F32)],
        in_specs=[seq, seq, seq, vec], out_specs=[seq, vec], scratch_shapes=[pltpu.VMEM((t, w), F32)],
        compiler_params=_params(("arbitrary",)))(dcum_k, dcum_q, fg, b_fg)


def _head_masks(x):
    lane = lax.broadcasted_iota(jnp.int32, x.shape, 1)
    zero = jnp.zeros(x.shape, x.dtype)
    return jnp.where(lane < HEAD_DIM, x, zero), jnp.where(lane >= HEAD_DIM, x, zero)


ATTN_BLOCK = 688


def _attn_blocks(t):
    nb = max(t // ATTN_BLOCK, 1)
    blocks = [(i * ATTN_BLOCK, ATTN_BLOCK) for i in range(nb - 1)]
    return blocks + [((nb - 1) * ATTN_BLOCK, t - (nb - 1) * ATTN_BLOCK)]


def _attn_specs(t):
    blocks = _attn_blocks(t)
    width = max(sz for _, sz in blocks)
    qkv = lambda off: pl.BlockSpec((t, LANES), lambda b, h: (b, off + h))
    cumr = pl.BlockSpec((None, None, len(blocks), 8, width), lambda b, h: (b, h, 0, 0, 0))
    return qkv, cumr


def _key_sums_to_blocks(cum, batch, t, n_pairs):
    blocks = _attn_blocks(t)
    width = max(sz for _, sz in blocks)
    cum_h = cum.reshape(batch, t, -1)[:, :, :2 * n_pairs].reshape(batch, t, n_pairs, 2)
    rows = [jnp.pad(jnp.transpose(cum_h[:, s0:s0 + sz], (0, 2, 3, 1)), ((0, 0), (0, 0), (0, 6), (0, width - sz)))
            for s0, sz in blocks]
    return jnp.stack(rows, axis=2)


def _key_sums_from_blocks(dcumr, batch, t, n_pairs):
    cols = [jnp.transpose(dcumr[:, :, j, :2, :sz], (0, 3, 1, 2)) for j, (_, sz) in enumerate(_attn_blocks(t))]
    return jnp.concatenate(cols, axis=1).reshape(batch * t, 2 * n_pairs)


def _causal(size):
    row = lax.broadcasted_iota(jnp.int32, (size, size), 0)
    col = lax.broadcasted_iota(jnp.int32, (size, size), 1)
    return row >= col


def _attn_fwd(name, qkv, cumr, batch):
    n, w3 = qkv.shape
    w = w3 // 3
    t = n // batch
    n_pairs = w // LANES
    blocks = _attn_blocks(t)

    def body(q_ref, k_ref, v_ref, cr_ref, o_ref, lse_ref):
        pair = pl.program_id(1)

        @pl.when(pair == 0)
        def _():
            lse_ref[...] = jnp.zeros(lse_ref.shape, F32)

        for i, (q0, qn) in enumerate(blocks):
            rows = slice(q0, q0 + qn)
            qs = _head_masks(q_ref[rows, :] * (0.125 * LOG2E))
            outs, lses = [], []
            for hh in range(2):
                m = jnp.full((qn, 1), NEG, F32)
                l = jnp.zeros((qn, 1), F32)
                acc = jnp.zeros((qn, LANES), F32)
                for j in range(i + 1):
                    k0, kn = blocks[j]
                    cols = slice(k0, k0 + kn)
                    s = _dot_nt(qs[hh], k_ref[cols, :]) - cr_ref[j, hh:hh + 1, 0:kn] * LOG2E
                    if j == i:
                        s = jnp.where(_causal(qn), s, NEG)
                    m_new = jnp.maximum(m, jnp.max(s, axis=1, keepdims=True))
                    alpha = jnp.exp2(m - m_new)
                    p = jnp.exp2(s - m_new)
                    l = alpha * l + jnp.sum(p, axis=1, keepdims=True)
                    acc = alpha * acc + jnp.dot(p.astype(BF16), v_ref[cols, :], preferred_element_type=F32)
                    m = m_new
                outs.append(acc / l)
                lses.append(m + jnp.log2(l))
            lane = lax.broadcasted_iota(jnp.int32, (qn, LANES), 1)
            o_ref[rows, :] = jnp.where(lane < HEAD_DIM, outs[0], outs[1]).astype(BF16)
            lse_ref[rows, :] = jnp.where(lane == 2 * pair, lses[0],
                                         jnp.where(lane == 2 * pair + 1, lses[1], lse_ref[rows, :]))

    qkv_spec, cumr_spec = _attn_specs(t)
    return pl.pallas_call(
        body, name=name, grid=(batch, n_pairs),
        out_shape=[jax.ShapeDtypeStruct((n, w), BF16), jax.ShapeDtypeStruct((n, LANES), F32)],
        in_specs=[qkv_spec(0), qkv_spec(n_pairs), qkv_spec(2 * n_pairs), cumr_spec],
        out_specs=[qkv_spec(0), pl.BlockSpec((t, LANES), lambda b, h: (b, 0))],
        compiler_params=_params(("parallel", "arbitrary")))(qkv, qkv, qkv, cumr)


def _attn_bwd(name, qkv, o, do, lse, cumr, batch):
    n, w3 = qkv.shape
    w = w3 // 3
    t = n // batch
    n_pairs = w // LANES
    blocks = _attn_blocks(t)

    def body(q_ref, k_ref, v_ref, o_ref, do_ref, lse_ref, cr_ref, dq_ref, dk_ref, dv_ref, dcr_ref, dcq_ref,
             dk_acc, dv_acc):
        pair = pl.program_id(1)
        dk_acc[...] = jnp.zeros(dk_acc.shape, F32)
        dv_acc[...] = jnp.zeros(dv_acc.shape, F32)
        dcr_ref[...] = jnp.zeros(dcr_ref.shape, F32)

        @pl.when(pair == 0)
        def _():
            dcq_ref[...] = jnp.zeros(dcq_ref.shape, F32)

        for i, (q0, qn) in enumerate(blocks):
            rows = slice(q0, q0 + qn)
            qs = _head_masks(q_ref[rows, :] * 0.125)
            q2 = _head_masks(q_ref[rows, :] * (0.125 * LOG2E))
            dos = _head_masks(do_ref[rows, :])
            dq = jnp.zeros((qn, LANES), F32)
            lane = lax.broadcasted_iota(jnp.int32, (qn, LANES), 1)
            dcq = []
            for hh in range(2):
                row_sum = jnp.zeros((qn, 1), F32)
                lse = jnp.sum(jnp.where(lane == 2 * pair + hh, lse_ref[rows, :], 0.0), axis=1, keepdims=True)
                delta = jnp.sum(dos[hh].astype(F32) * o_ref[rows, :].astype(F32), axis=1, keepdims=True)
                for j in range(i + 1):
                    k0, kn = blocks[j]
                    cols = slice(k0, k0 + kn)
                    s = _dot_nt(q2[hh], k_ref[cols, :]) - cr_ref[j, hh:hh + 1, 0:kn] * LOG2E
                    p = jnp.exp2(s - lse)
                    if j == i:
                        p = jnp.where(_causal(qn), p, 0.0)
                    dp = _dot_nt(dos[hh], v_ref[cols, :])
                    ds = p * (dp - delta)
                    pb, dsb = p.astype(BF16), ds.astype(BF16)
                    km = _head_masks(k_ref[cols, :])[hh]
                    dv_acc[j, :, 0:kn] += _dot_tn(dos[hh], pb)
                    dk_acc[j, :, 0:kn] += _dot_tn(qs[hh], dsb)
                    dq = dq + jnp.dot(dsb, km, preferred_element_type=F32)
                    dcr_ref[j, hh:hh + 1, 0:kn] -= jnp.sum(ds, axis=0, keepdims=True)
                    row_sum = row_sum + jnp.sum(ds, axis=1, keepdims=True)
                dcq.append(row_sum)
            dq_ref[rows, :] = (dq * 0.125).astype(BF16)
            dcq_ref[rows, :] = jnp.where(lane == 2 * pair, dcq[0],
                                         jnp.where(lane == 2 * pair + 1, dcq[1], dcq_ref[rows, :]))
        for j, (k0, kn) in enumerate(blocks):
            dk_ref[k0:k0 + kn, :] = dk_acc[j].T[0:kn, :].astype(BF16)
            dv_ref[k0:k0 + kn, :] = dv_acc[j].T[0:kn, :].astype(BF16)

    qkv_spec, cumr_spec = _attn_specs(t)
    wide = -(-max(sz for _, sz in blocks) // LANES) * LANES
    act = jax.ShapeDtypeStruct((n, w), BF16)
    return pl.pallas_call(
        body, name=name, grid=(batch, n_pairs),
        out_shape=[act, act, act, jax.ShapeDtypeStruct(cumr.shape, F32), jax.ShapeDtypeStruct((n, LANES), F32)],
        in_specs=[qkv_spec(0), qkv_spec(n_pairs), qkv_spec(2 * n_pairs), qkv_spec(0), qkv_spec(0),
                  pl.BlockSpec((t, LANES), lambda b, h: (b, 0)), cumr_spec],
        out_specs=[qkv_spec(0), qkv_spec(0), qkv_spec(0), cumr_spec, pl.BlockSpec((t, LANES), lambda b, h: (b, 0))],
        scratch_shapes=[pltpu.VMEM((len(blocks), LANES, wide), F32), pltpu.VMEM((len(blocks), LANES, wide), F32)],
        compiler_params=_params(("parallel", "arbitrary")))(qkv, qkv, qkv, o, do, lse, cumr)


def _adamw(name, parts, w, m, v):
    r, c = w.shape
    tr = 128 if r % 128 == 0 else r
    tc = 256 if tr > 128 and c % 256 == 0 else c
    c1 = 1.0 - ADAM_B1 ** ADAM_STEP
    c2 = 1.0 - ADAM_B2 ** ADAM_STEP

    def body(p_ref, w_ref, m_ref, v_ref, g_ref, d_ref, m2_ref, v2_ref):
        g = p_ref[0].astype(F32)
        for s in range(1, N_DEV):
            g = g + p_ref[s].astype(F32)
        m2 = ADAM_B1 * m_ref[...] + (1.0 - ADAM_B1) * g
        v2 = ADAM_B2 * v_ref[...] + (1.0 - ADAM_B2) * (g * g)
        g_ref[...] = g
        m2_ref[...] = m2
        v2_ref[...] = v2
        d_ref[...] = -ADAM_LR * ((m2 / c1) / (jnp.sqrt(v2 / c2) + ADAM_EPS) + ADAM_WD * w_ref[...])

    blk = pl.BlockSpec((tr, tc), lambda i, j: (i, j))
    shp = jax.ShapeDtypeStruct((r, c), F32)
    return pl.pallas_call(
        body, name=name, out_shape=[shp] * 4, grid=(r // tr, c // tc),
        in_specs=[pl.BlockSpec((N_DEV, tr, tc), lambda i, j: (0, i, j)), blk, blk, blk], out_specs=[blk] * 4,
        compiler_params=_params(("parallel", "parallel")))(parts, w, m, v)


def _cat_small(vals):
    parts = []
    for name in SMALL:
        v = vals[name].reshape(1, -1).astype(F32)
        parts.append(jnp.pad(v, ((0, 0), (0, SMALL_W[name] - v.shape[1]))))
    return jnp.concatenate(parts, axis=1)


def _split_small(row, shapes):
    out, off = {}, 0
    for name in SMALL:
        out[name] = row[0, off:off + SMALL_N[name]].reshape(shapes[name])
        off += SMALL_W[name]
    return out


def _cols_from_shards(g):
    return jnp.transpose(g, (1, 0, 2)).reshape(g.shape[1], N_DEV * g.shape[2])


def _shards_from_cols(a):
    r, c = a.shape
    return jnp.transpose(a.reshape(r, N_DEV, c // N_DEV), (1, 0, 2))


def kernel(x, meta_tokens, norm_mix_gain, w_in, b_forget, w_attn_out, b_glu, conv_dw_w, conv_dw_b, conv_ln_gain, conv_ln_bias, w_conv_out, b_conv_out, w_out, norm_mlp_gain, w_mlp_up, w_mlp_down, final_norm_gain, loss_target, m_meta_tokens, m_norm_mix_gain, m_w_in, m_b_forget, m_w_attn_out, m_b_glu, m_conv_dw_w, m_conv_dw_b, m_conv_ln_gain, m_conv_ln_bias, m_w_conv_out, m_b_conv_out, m_w_out, m_norm_mlp_gain, m_w_mlp_up, m_w_mlp_down, m_final_norm_gain, v_meta_tokens, v_norm_mix_gain, v_w_in, v_b_forget, v_w_attn_out, v_b_glu, v_conv_dw_w, v_conv_dw_b, v_conv_ln_gain, v_conv_ln_bias, v_w_conv_out, v_b_conv_out, v_w_out, v_norm_mlp_gain, v_w_mlp_up, v_w_mlp_down, v_final_norm_gain):
    weights = dict(meta_tokens=meta_tokens, norm_mix_gain=norm_mix_gain, w_in=w_in, b_forget=b_forget, w_attn_out=w_attn_out, b_glu=b_glu, conv_dw_w=conv_dw_w, conv_dw_b=conv_dw_b, conv_ln_gain=conv_ln_gain, conv_ln_bias=conv_ln_bias, w_conv_out=w_conv_out, b_conv_out=b_conv_out, w_out=w_out, norm_mlp_gain=norm_mlp_gain, w_mlp_up=w_mlp_up, w_mlp_down=w_mlp_down, final_norm_gain=final_norm_gain)
    mom_m = dict(meta_tokens=m_meta_tokens, norm_mix_gain=m_norm_mix_gain, w_in=m_w_in, b_forget=m_b_forget, w_attn_out=m_w_attn_out, b_glu=m_b_glu, conv_dw_w=m_conv_dw_w, conv_dw_b=m_conv_dw_b, conv_ln_gain=m_conv_ln_gain, conv_ln_bias=m_conv_ln_bias, w_conv_out=m_w_conv_out, b_conv_out=m_b_conv_out, w_out=m_w_out, norm_mlp_gain=m_norm_mlp_gain, w_mlp_up=m_w_mlp_up, w_mlp_down=m_w_mlp_down, final_norm_gain=m_final_norm_gain)
    mom_v = dict(meta_tokens=v_meta_tokens, norm_mix_gain=v_norm_mix_gain, w_in=v_w_in, b_forget=v_b_forget, w_attn_out=v_w_attn_out, b_glu=v_b_glu, conv_dw_w=v_conv_dw_w, conv_dw_b=v_conv_dw_b, conv_ln_gain=v_conv_ln_gain, conv_ln_bias=v_conv_ln_bias, w_conv_out=v_w_conv_out, b_conv_out=v_b_conv_out, w_out=v_w_out, norm_mlp_gain=v_norm_mlp_gain, w_mlp_up=v_w_mlp_up, w_mlp_down=v_w_mlp_down, final_norm_gain=v_final_norm_gain)
    names = list(weights)
    batch, seq, d = x.shape
    t = seq + N_META
    n = batch * t
    n_pairs = d // LANES
    assert t % ROW_TILE == 0 and d == SEG

    to_rows = lambda w3: jnp.transpose(w3[0])
    w_in_t, m_in_t, v_in_t = to_rows(w_in), to_rows(m_w_in), to_rows(v_w_in)
    first = [w_in_t.astype(BF16), meta_tokens, conv_dw_w[0]]
    rest = [w_[0].astype(BF16) for w_ in (w_attn_out, w_conv_out, w_out, w_mlp_up, w_mlp_down)]
    tgt = jnp.concatenate([jnp.zeros((batch, N_META, d), F32), loss_target], axis=1).reshape(n, d)
    h0_rows = jnp.pad(x, ((0, 0), (N_META, 0), (0, 0)))
    g1 = norm_mix_gain.reshape(1, -1)
    gather_a = _exchange_start("gather_in_start", [(f_, False) for f_ in first], ks=CHIP_PEERS)
    level_1 = _exchange_wait("gather_in_wait", gather_a, [gather_a["token"], tgt, h0_rows, w_in_t, m_in_t, v_in_t] + rest)
    passed = _pass_on_start("gather_in_pass_start", level_1)
    w_in_g, meta_g, w_dw_g = _pass_on_wait("gather_in_pass_wait", passed, passed["token"])
    gather_b = _exchange_start("gather_rest_start", [(r_, False) for r_ in rest])
    n_fg = b_forget.shape[1]
    shard_w = w_in.shape[2]
    wt = w_in_g.reshape(N_DEV * shard_w, d)
    o_fg = 3 * SEG
    seg_rows = [0, SEG, 2 * SEG] + [o_fg + n_fg + i * SEG for i in range(4)]
    d_ff = w_mlp_down.shape[1] * N_DEV
    ff_blk = d_ff // N_DEV
    meta_f = _cols_from_shards(meta_g)
    w_dw = _cols_from_shards(w_dw_g)

    row2 = lambda v: v.reshape(1, -1)
    g2, g3 = row2(norm_mlp_gain), row2(final_norm_gain)
    b_fg = jnp.pad(b_forget, ((0, 0), (0, FG_PAD - n_fg)))
    h0 = lax.dynamic_update_slice(h0_rows, jnp.broadcast_to(meta_f[None], (batch, N_META, d)), (0, 0, 0)).reshape(n, d)

    hn1 = _rms_fwd("rms1", h0, g1)
    proj = lambda name, off, width, tn, dt: _mm_nt(name, [(hn1, _a_rows(d), wt, _wt_rows(tn, off))], n, width, tn, dt,
                                                   after=gather_b["token"])
    qkv = proj("proj_qkv", 0, 3 * SEG, SEG, BF16)
    glu = proj("proj_glu", seg_rows[3], 2 * SEG, SEG, BF16)
    gates = proj("proj_gates", seg_rows[5], 2 * SEG, SEG, BF16)
    fg = proj("proj_fg", o_fg, FG_PAD, FG_PAD, F32)

    cum = _fox_prep_fwd("fox_cumsum", fg, b_fg, batch)
    cumr = _key_sums_to_blocks(cum, batch, t, n_pairs)
    o, lse = _attn_fwd("attn_fwd", qkv, cumr, batch)
    rest = _exchange_wait("gather_rest_wait", gather_b, o)
    w_ao, w_co, w_o = [r_.reshape(d, d) for r_ in rest[:3]]
    w_up = rest[3]
    w_dn = rest[4].reshape(d_ff, d)
    a = _mm_nn("attn_out", o, w_ao, _w_cols(d, d, 0), d, d, BF16)

    c1 = _glu_conv_fwd("glu_conv", glu, b_glu, w_dw, conv_dw_b, batch)
    c3 = _ln_silu_fwd("ln_silu", c1, conv_ln_gain, conv_ln_bias)
    c = _mm_nn("conv_out", c3, w_co, _w_cols(d, d, 0), d, d, BF16)

    mrg = _merge_fwd("merge", gates, a, c, b_conv_out)
    mo = _mm_nn("mix_out", mrg, w_o, _w_cols(d, d, 0), d, d, BF16)
    h1, hn2 = _rms_fwd("resid_rms2", h0, g2, res=mo)
    per = ff_blk // 512
    r_up = _mm_nn("mlp_up", hn2, w_up, pl.BlockSpec((None, d, 512), lambda i, j: (j // per, 0, j % per)),
                  d_ff, 512, BF16, relu=True)
    dn = _mm_nn("mlp_down", r_up, w_dn, _w_cols(d_ff, d, 0), d, d, BF16, square_x=True, tm=ROW_TILE)
    dh2b, loss_blk, dg3 = _final("final_loss", h1, dn, tgt, g3, batch)

    dup = _mm_nt("d_mlp_down", [(dh2b, _a_rows(d), w_dn, _w_rows(d, d))], n, d_ff, d, BF16, relu_bwd_of=r_up)
    dw_dn = _grad_w_rows("gw_mlp_down", [r_up], dh2b, square_x=True)
    dhn2 = _mm_nt("d_mlp_up", [(dup, _a_rows(ff_blk, g), w_up, pl.BlockSpec((None, 512, ff_blk), lambda i, j, g=g: (g, j, 0)))
                               for g in range(N_DEV)], n, d, 512, BF16)
    dw_up = _mm_tn("gw_mlp_up", hn2, lambda a_: 0, d, dup, lambda b_: b_, ff_blk, (N_DEV, d, ff_blk),
                   pl.BlockSpec((None, d, ff_blk), lambda a_, b_: (b_, 0, 0)), (1, N_DEV))
    scatter_1 = _exchange_start("scatter_mlp_start", [(dw_dn.reshape(N_DEV, ff_blk, d), True), (dw_up, True)])
    dh1b, dg2 = _rms_bwd("rms2_bwd", dhn2, h1, g2 + scatter_1["token"][0:1, 0:1], dh2b, batch, dh_dtype=BF16)

    dm = _mm_nt("d_mix_out", [(dh1b, _a_rows(d), w_o, _w_rows(d, d))], n, d, d, BF16)
    dw_o = _grad_w("gw_mix_out", mrg, dh1b)
    da, dc, dga, dgc, dbco = _merge_bwd("merge_bwd", dm, gates, a, c, b_conv_out)

    do = _mm_nt("d_attn_out", [(da, _a_rows(d), w_ao, _w_rows(d, d))], n, d, d, BF16)
    dw_ao = _grad_w("gw_attn_out", o, da)
    dc3 = _mm_nt("d_conv_out", [(dc, _a_rows(d), w_co, _w_rows(d, d))], n, d, d, BF16)
    dw_co = _grad_w("gw_conv_out", c3, dc)

    scatter_2 = _exchange_start("scatter_mix_start", [(dw_.reshape(N_DEV, d // N_DEV, d), True)
                                                      for dw_ in (dw_o, dw_ao, dw_co)])
    dc1, dg_ln, db_ln = _ln_silu_bwd("ln_silu_bwd", dc3, c1, conv_ln_gain + scatter_2["token"][0:1, 0:1],
                                     conv_ln_bias)
    dglu_a, dglu_g, dw_dw, db_dw, dbg_a, dbg_g = _glu_conv_bwd("glu_conv_bwd", dc1, glu, b_glu, w_dw, batch)

    dq, dk, dv, dcumr, dcum_q = _attn_bwd("attn_bwd", qkv, o, do, lse, cumr, batch)
    dcum_k = jnp.pad(_key_sums_from_blocks(dcumr, batch, t, n_pairs), ((0, 0), (0, FG_PAD - 2 * n_pairs)))
    dfg, db_fg = _fox_prep_bwd("fox_cumsum_bwd", dcum_k, dcum_q, fg, b_fg, batch)

    segs = [dq, dk, dv, dglu_a, dglu_g, dga, dgc]
    gw_t = _grad_w_rows("gw_in", segs, hn1, gap=(o_fg, n_fg))
    gw_fg = _grad_w("gw_in_fg", dfg, hn1)[:n_fg]
    dw_in_t = lax.dynamic_update_slice(gw_t, gw_fg, (o_fg, 0)).reshape(N_DEV, shard_w, d)
    scatter_3 = _exchange_start("scatter_in_start", [(dw_in_t, True)])
    pairs = [(s_, _a_rows(SEG, 0, ROW_TILE), wt, _wt_block(SEG, seg_rows[i], 512), "nn") for i, s_ in enumerate(segs)]
    pairs.append((dfg, _a_rows(FG_PAD, 0, ROW_TILE), wt, _wt_block(FG_PAD, o_fg, 512), "nn"))
    dhn1 = _mm_nt("d_proj_in", pairs, n, d, 512, BF16, tm=ROW_TILE, after=scatter_3["token"])
    dh0, dg1, dmeta = _rms_bwd("rms1_bwd", dhn1, h0, g1, dh1b, batch, with_meta=True)
    grad_x = dh0.reshape(batch, t, d)[:, N_META:, :]

    small_g = dict(norm_mix_gain=dg1, b_forget=db_fg[:, :n_fg], b_glu=jnp.concatenate([dbg_a, dbg_g], axis=1),
                   conv_dw_b=db_dw, conv_ln_gain=dg_ln, conv_ln_bias=db_ln, b_conv_out=dbco, norm_mlp_gain=dg2,
                   final_norm_gain=dg3)
    scatter_4 = _exchange_start("scatter_small_start", [
        (_shards_from_cols(dmeta), True), (_shards_from_cols(dw_dw), True), (_cat_small(small_g), False),
        (loss_blk[0:1, :], False)])

    grads, deltas, new_m, new_v = {}, {}, {}, {}

    def update(k, parts):
        shp = weights[k].shape
        if k == "w_in":
            res_ = _adamw("adamw_" + k, parts, w_in_t, m_in_t, v_in_t)
            res_ = [jnp.transpose(r) for r in res_]
        else:
            w2 = lambda arr: arr.reshape(parts.shape[1:])
            res_ = _adamw("adamw_" + k, parts, w2(weights[k]), w2(mom_m[k]), w2(mom_v[k]))
        grads[k], deltas[k], new_m[k], new_v[k] = [r.reshape(shp) for r in res_]

    for k, parts in zip(("w_mlp_down", "w_mlp_up"), _exchange_wait("scatter_mlp_wait", scatter_1, scatter_4["token"])):
        update(k, parts)
    for k, parts in zip(("w_out", "w_attn_out", "w_conv_out"),
                        _exchange_wait("scatter_mix_wait", scatter_2, deltas["w_mlp_up"])):
        update(k, parts)
    update("w_in", _exchange_wait("scatter_in_wait", scatter_3, deltas["w_conv_out"])[0])
    reduced = _exchange_wait("scatter_small_wait", scatter_4, deltas["w_in"])
    loss = jnp.sum(reduced.pop()[:, 0, 0])
    for k, parts in zip(("meta_tokens", "conv_dw_w"), reduced[:-1]):
        update(k, parts)
    res = _adamw("adamw_small", reduced[-1], _cat_small(weights), _cat_small(mom_m), _cat_small(mom_v))
    shapes = {k: weights[k].shape for k in SMALL}
    for dst, r in zip((grads, deltas, new_m, new_v), res):
        dst.update(_split_small(r, shapes))

    return (loss, grad_x, *[grads[k] for k in names], *[deltas[k] for k in names],
            *[new_m[k] for k in names], *[new_v[k] for k in names])
```
